```python
import math
import jax, jax.numpy as jnp
from jax import lax
import numpy as np

D_MODEL = 1024
BATCH = 2
SEQ = 8192
DEPTH = 2

N_EVEN = (DEPTH + 1) // 2
N_ODD = DEPTH // 2
GROUP_WIDTH = D_MODEL // 2

RET_HEADS = 4
RET_HEAD_DIM = GROUP_WIDTH // RET_HEADS
RET_W = RET_HEADS * RET_HEAD_DIM
RET_THETA = 10000.0
DIFF_HEADS = 4
DIFF_HEAD_DIM = GROUP_WIDTH // (2 * DIFF_HEADS)
DIFF_V_DIM = 2 * DIFF_HEAD_DIM
DIFF_QK_W = 2 * DIFF_HEADS * DIFF_HEAD_DIM
DIFF_V_W = DIFF_HEADS * DIFF_V_DIM
ROPE_THETA = 500000.0
DIFF_ROT_DIM = DIFF_HEAD_DIM // 4
MLA_HEADS = 4
MLA_Q_RANK = 256
MLA_KV_RANK = 128
MLA_NOPE = 64
MLA_ROPE = 32
MLA_V = GROUP_WIDTH // MLA_HEADS
GLA_HEADS = 4
GLA_K_DIM = GROUP_WIDTH // 2 // GLA_HEADS
GLA_V_DIM = GROUP_WIDTH // GLA_HEADS
GLA_K_W = GLA_HEADS * GLA_K_DIM
GLA_V_W = GLA_HEADS * GLA_V_DIM
GLA_GATE_RANK = 16
GLA_TAU = 16.0
CHUNK = 64
Q_BLOCK = 128
N_GROUPS = 4
EXPERTS_PER_GROUP = 8
N_EXPERTS = N_GROUPS * EXPERTS_PER_GROUP
TOP_K_IN_GROUP = 2
D_EXPERT = 512
MOE_BLOCK = 128
ALPHA = (2.0 * DEPTH) ** 0.25
BETA = (8.0 * DEPTH) ** -0.25
LN_EPS = 1e-5

EVEN_SPLITS = (RET_W, RET_W, RET_W, RET_W, DIFF_QK_W, DIFF_QK_W, DIFF_V_W)
EVEN_VALUE_BLOCKS = (2, 6)
EVEN_IN = sum(EVEN_SPLITS)
ODD_SPLITS = (MLA_Q_RANK, MLA_KV_RANK, MLA_ROPE, GLA_K_W, GLA_K_W, GLA_V_W, GLA_V_W, GLA_GATE_RANK, GLA_GATE_RANK)
ODD_VALUE_BLOCKS = (5,)
ODD_IN = sum(ODD_SPLITS)

kernel_name = 'hybrid_retnet_diffattn_mla_gla_hmoe_encoder'

F32 = jnp.float32


def _split(z, sizes):
    return jnp.split(z, np.cumsum(sizes)[:-1].tolist(), axis=-1)


def _heads(z, h):
    b, t, _ = z.shape
    return z.reshape(b, t, h, -1).transpose(0, 2, 1, 3)


def _merge(z):
    b, h, t, d = z.shape
    return z.transpose(0, 2, 1, 3).reshape(b, t, h * d)


def _layer_norm(x, g, b):
    xf = x.astype(F32)
    mu = jnp.mean(xf, -1, keepdims=True)
    var = jnp.mean(jnp.square(xf - mu), -1, keepdims=True)
    return ((xf - mu) * lax.rsqrt(var + LN_EPS) * g + b).astype(x.dtype)


def _group_norm(z):
    zf = z.astype(F32)
    mu = jnp.mean(zf, -1, keepdims=True)
    var = jnp.mean(jnp.square(zf - mu), -1, keepdims=True)
    return ((zf - mu) * lax.rsqrt(var + LN_EPS)).astype(z.dtype)


def _rms_norm(z, g):
    zf = z.astype(F32)
    return (zf * lax.rsqrt(jnp.mean(jnp.square(zf), -1, keepdims=True) + 1e-6) * g).astype(z.dtype)


def _rotary(x, rot_dim, theta):
    t = x.shape[2]
    half = rot_dim // 2
    pos = jnp.arange(t, dtype=F32)
    inv = jnp.power(jnp.float32(theta), -jnp.arange(0, rot_dim, 2, dtype=F32) / rot_dim)
    ang = pos[:, None] * inv[None, :]
    cos, sin = jnp.cos(ang), jnp.sin(ang)
    xr = x[..., :rot_dim].astype(F32)
    x1, x2 = xr[..., :half], xr[..., half:]
    rot = jnp.concatenate([x1 * cos - x2 * sin, x2 * cos + x1 * sin], -1).astype(x.dtype)
    return jnp.concatenate([rot, x[..., rot_dim:]], -1)


def _chunk_scan(q, k, v, log_a, strict):
    b, h, t, dk = q.shape
    dv = v.shape[-1]
    n = t // CHUNK

    def chunks(z):
        return z.reshape(b, h, n, CHUNK, z.shape[-1]).transpose(2, 0, 1, 3, 4)

    cum = jnp.cumsum(chunks(log_a.astype(F32)), axis=3)
    mask = jnp.tril(jnp.ones((CHUNK, CHUNK), bool), -1 if strict else 0)[None, None, :, :, None]

    def step(state, inp):
        qi, ki, vi, bi = inp
        qi, ki, vi = qi.astype(F32), ki.astype(F32), vi.astype(F32)
        o_inter = jnp.einsum('bhck,bhkv->bhcv', qi * jnp.exp(bi), state)
        diff = bi[:, :, :, None, :] - bi[:, :, None, :, :]
        decay = jnp.where(mask, jnp.exp(jnp.where(mask, diff, 0.0)), 0.0)
        scores = jnp.sum(qi[:, :, :, None, :] * ki[:, :, None, :, :] * decay, axis=-1)
        o_intra = jnp.einsum('bhij,bhjv->bhiv', scores, vi)
        b_last = bi[:, :, -1:, :]
        state = jnp.exp(b_last[:, :, 0, :, None]) * state + jnp.einsum('bhck,bhcv->bhkv', ki * jnp.exp(b_last - bi), vi)
        return state, o_inter + o_intra

    state0 = jnp.zeros((b, h, dk, dv), F32)
    _, o = lax.scan(step, state0, (chunks(q), chunks(k), chunks(v), cum))
    return o.transpose(1, 2, 0, 3, 4).reshape(b, h, t, dv).astype(v.dtype)


def _bidirectional_scan(q, k, v, log_a_fwd, log_a_bwd):
    rev = lambda z: jnp.flip(z, axis=2)
    fwd = _chunk_scan(q, k, v, log_a_fwd, strict=False)
    bwd = rev(_chunk_scan(rev(q), rev(k), rev(v), rev(log_a_bwd), strict=True))
    return fwd + bwd


def _query_blocks(z):
    b, h, t, d = z.shape
    return z.reshape(b, h, t // Q_BLOCK, Q_BLOCK, d).transpose(2, 0, 1, 3, 4)


def _unblock(o):
    nb, b, h, qb, d = o.shape
    return o.transpose(1, 2, 0, 3, 4).reshape(b, h, nb * qb, d)


def _blocked_attention(q, k, v, scale):
    def one(qi):
        s = jnp.einsum('bhqd,bhkd->bhqk', qi, k).astype(F32) * scale
        p = jax.nn.softmax(s, axis=-1)
        return jnp.einsum('bhqk,bhkd->bhqd', p.astype(v.dtype), v)
    return _unblock(lax.map(one, _query_blocks(q)))


def _blocked_diff_attention(q1, q2, k1, k2, v, lam, scale):
    def one(qs):
        qa, qb = qs
        p1 = jax.nn.softmax(jnp.einsum('bhqd,bhkd->bhqk', qa, k1).astype(F32) * scale, axis=-1)
        p2 = jax.nn.softmax(jnp.einsum('bhqd,bhkd->bhqk', qb, k2).astype(F32) * scale, axis=-1)
        return jnp.einsum('bhqk,bhkd->bhqd', (p1 - lam * p2).astype(v.dtype), v)
    return _unblock(lax.map(one, (_query_blocks(q1), _query_blocks(q2))))


def _retention_diff_mixer(x, w_in, ret_decay_f, ret_decay_b, lq1, lk1, lq2, lk2, subln, w_out, layer_idx):
    rq, rk, rv, rg, dq, dk, dv = _split(x @ w_in, EVEN_SPLITS)
    b, t, _ = x.shape
    q = _rotary(_heads(rq, RET_HEADS), RET_HEAD_DIM, RET_THETA)
    k = _rotary(_heads(rk, RET_HEADS), RET_HEAD_DIM, RET_THETA) * (RET_HEAD_DIM ** -0.5)
    v = _heads(rv, RET_HEADS)
    la_f = jnp.broadcast_to((-jnp.exp(ret_decay_f.astype(F32)))[None, :, None, None], (b, RET_HEADS, t, 1))
    la_b = jnp.broadcast_to((-jnp.exp(ret_decay_b.astype(F32)))[None, :, None, None], (b, RET_HEADS, t, 1))
    ret = _merge(_group_norm(_bidirectional_scan(q, k, v, la_f, la_b))) * jax.nn.silu(rg)
    def pair(z):
        z = z.reshape(b, t, DIFF_HEADS, 2, DIFF_HEAD_DIM).transpose(0, 2, 3, 1, 4)
        return (_rotary(z[:, :, 0], DIFF_ROT_DIM, ROPE_THETA), _rotary(z[:, :, 1], DIFF_ROT_DIM, ROPE_THETA))
    q1, q2 = pair(dq)
    k1, k2 = pair(dk)
    vv = _heads(dv, DIFF_HEADS)
    lam_init = 0.8 - 0.6 * math.exp(-0.3 * layer_idx)
    lam = (jnp.exp(jnp.sum(lq1 * lk1)) - jnp.exp(jnp.sum(lq2 * lk2))).astype(F32) + lam_init
    o = _blocked_diff_attention(q1, q2, k1, k2, vv, lam, DIFF_HEAD_DIM ** -0.5)
    diff = _merge(_rms_norm(o, subln) * (1.0 - lam_init))
    return jnp.concatenate([ret, diff], axis=-1) @ w_out


def _mla_gla_mixer(x, w_in, q_norm, w_uq, kv_norm, w_ukv, gla_w2_f, gla_b_f, gla_w2_b, gla_b_b, gla_norm, w_out):
    cq, ckv, krope, gq, gk, gv, gg, lr_f, lr_b = _split(x @ w_in, ODD_SPLITS)
    qh = _heads(_rms_norm(cq, q_norm) @ w_uq, MLA_HEADS)
    q = jnp.concatenate([qh[..., :MLA_NOPE], _rotary(qh[..., MLA_NOPE:], MLA_ROPE, ROPE_THETA)], -1)
    kvh = _heads(_rms_norm(ckv, kv_norm) @ w_ukv, MLA_HEADS)
    k_nope, v = kvh[..., :MLA_NOPE], kvh[..., MLA_NOPE:]
    k_rope = _rotary(krope[:, None], MLA_ROPE, ROPE_THETA)
    k = jnp.concatenate([k_nope, jnp.broadcast_to(k_rope, k_nope.shape[:-1] + (MLA_ROPE,))], -1)
    mla = _merge(_blocked_attention(q, k, v, (MLA_NOPE + MLA_ROPE) ** -0.5))
    def log_gate(lr, w2, bias):
        return _heads(jax.nn.log_sigmoid((lr @ w2 + bias).astype(F32)) / GLA_TAU, GLA_HEADS)
    gq_h = _heads(gq, GLA_HEADS) * (GLA_K_DIM ** -0.5)
    o = _bidirectional_scan(gq_h, _heads(gk, GLA_HEADS), _heads(gv, GLA_HEADS), log_gate(lr_f, gla_w2_f, gla_b_f), log_gate(lr_b, gla_w2_b, gla_b_b))
    gla = _merge(_rms_norm(o, gla_norm)) * jax.nn.silu(gg)
    return jnp.concatenate([mla, gla], axis=-1) @ w_out


def _hier_moe(x, w_grp, b_grp, w_exp, b_exp, w_gate, w_up, w_down):
    b, t, d = x.shape
    n = b * t
    xf = x.reshape(n, d)
    grp_logits = (xf @ w_grp + b_grp).astype(F32)
    g_idx = jnp.argmax(grp_logits, axis=-1).astype(jnp.int32)
    p_grp = jnp.take_along_axis(jax.nn.softmax(grp_logits, -1), g_idx[:, None], axis=1)[:, 0]
    exp_logits = (xf @ w_exp + b_exp).astype(F32).reshape(n, N_GROUPS, EXPERTS_PER_GROUP)
    in_grp = jnp.take_along_axis(exp_logits, g_idx[:, None, None], axis=1)[:, 0]
    top_l, top_e = lax.top_k(in_grp, TOP_K_IN_GROUP)
    gate = p_grp[:, None] * jax.nn.softmax(top_l, axis=-1)
    eid = (g_idx[:, None] * EXPERTS_PER_GROUP + top_e).reshape(-1).astype(jnp.int32)
    tok = jnp.repeat(jnp.arange(n, dtype=jnp.int32), TOP_K_IN_GROUP)
    gw = gate.reshape(-1)
    a = eid.shape[0]
    order = jnp.argsort(eid)
    eid_s, tok_s, gw_s = eid[order], tok[order], gw[order]
    counts = jnp.bincount(eid, length=N_EXPERTS).astype(jnp.int32)
    padded = (counts + MOE_BLOCK - 1) // MOE_BLOCK * MOE_BLOCK
    start = jnp.cumsum(counts) - counts
    pend = jnp.cumsum(padded)
    pstart = pend - padded
    dest = pstart[eid_s] + (jnp.arange(a, dtype=jnp.int32) - start[eid_s])
    cap = a + N_EXPERTS * MOE_BLOCK
    nb = cap // MOE_BLOCK
    buf_tok = jnp.full((cap,), n, jnp.int32).at[dest].set(tok_s)
    buf_w = jnp.zeros((cap,), F32).at[dest].set(gw_s)
    blk_e = jnp.minimum(jnp.searchsorted(pend, jnp.arange(nb, dtype=jnp.int32) * MOE_BLOCK, side='right'), N_EXPERTS - 1).astype(jnp.int32)
    x_pad = jnp.concatenate([xf, jnp.zeros((1, d), xf.dtype)], axis=0)

    def expert_block(args):
        idx, e = args
        xb = x_pad[idx]
        hdn = jax.nn.silu(xb @ w_gate[e]) * (xb @ w_up[e])
        return hdn @ w_down[e]

    y = lax.map(expert_block, (buf_tok.reshape(nb, MOE_BLOCK), blk_e)).reshape(cap, d)
    out = jnp.zeros((n + 1, d), x.dtype).at[buf_tok].add(y * buf_w[:, None].astype(y.dtype))
    return out[:n].reshape(b, t, d)


def setup_inputs(seed: int = 0) -> dict:
    key = jax.random.key(seed)
    ks = iter(jax.random.split(key, 40))

    def nrm(shape, scale):
        return jax.random.normal(next(ks), shape, jnp.float32) * scale

    def gain(shape):
        return 1.0 + nrm(shape, 0.02)

    ev_col = np.concatenate([np.full((w,), BETA if i in EVEN_VALUE_BLOCKS else 1.0, np.float32) for i, w in enumerate(EVEN_SPLITS)])
    od_col = np.concatenate([np.full((w,), BETA if i in ODD_VALUE_BLOCKS else 1.0, np.float32) for i, w in enumerate(ODD_SPLITS)])
    ukv_col = np.tile(np.concatenate([np.ones((MLA_NOPE,), np.float32), np.full((MLA_V,), BETA, np.float32)]), MLA_HEADS)
    gam = 1.0 - 2.0 ** (-5.0 - np.arange(RET_HEADS))
    ret_base = jnp.asarray(np.log(-np.log(gam)), jnp.float32)
    return {
        'x': nrm((BATCH, SEQ, D_MODEL), 1.0),
        'ev_w_in': nrm((N_EVEN, D_MODEL, EVEN_IN), D_MODEL ** -0.5) * jnp.asarray(ev_col),
        'ev_ret_decay_f': ret_base + nrm((N_EVEN, RET_HEADS), 0.05),
        'ev_ret_decay_b': ret_base + nrm((N_EVEN, RET_HEADS), 0.05),
        'ev_lq1': nrm((N_EVEN, DIFF_HEAD_DIM), 0.1),
        'ev_lk1': nrm((N_EVEN, DIFF_HEAD_DIM), 0.1),
        'ev_lq2': nrm((N_EVEN, DIFF_HEAD_DIM), 0.1),
        'ev_lk2': nrm((N_EVEN, DIFF_HEAD_DIM), 0.1),
        'ev_subln': gain((N_EVEN, DIFF_V_DIM)),
        'ev_w_out': nrm((N_EVEN, D_MODEL, D_MODEL), D_MODEL ** -0.5 * BETA),
        'od_w_in': nrm((N_ODD, D_MODEL, ODD_IN), D_MODEL ** -0.5) * jnp.asarray(od_col),
        'od_q_norm': gain((N_ODD, MLA_Q_RANK)),
        'od_w_uq': nrm((N_ODD, MLA_Q_RANK, MLA_HEADS * (MLA_NOPE + MLA_ROPE)), MLA_Q_RANK ** -0.5),
        'od_kv_norm': gain((N_ODD, MLA_KV_RANK)),
        'od_w_ukv': nrm((N_ODD, MLA_KV_RANK, MLA_HEADS * (MLA_NOPE + MLA_V)), MLA_KV_RANK ** -0.5) * jnp.asarray(ukv_col),
        'od_gla_w2_f': nrm((N_ODD, GLA_GATE_RANK, GLA_K_W), GLA_GATE_RANK ** -0.5),
        'od_gla_b_f': nrm((N_ODD, GLA_K_W), 0.5),
        'od_gla_w2_b': nrm((N_ODD, GLA_GATE_RANK, GLA_K_W), GLA_GATE_RANK ** -0.5),
        'od_gla_b_b': nrm((N_ODD, GLA_K_W), 0.5),
        'od_gla_norm': gain((N_ODD, GLA_V_DIM)),
        'od_w_out': nrm((N_ODD, D_MODEL, D_MODEL), D_MODEL ** -0.5 * BETA),
        'ln1_g': gain((DEPTH, D_MODEL)),
        'ln1_b': nrm((DEPTH, D_MODEL), 0.02),
        'ln2_g': gain((DEPTH, D_MODEL)),
        'ln2_b': nrm((DEPTH, D_MODEL), 0.02),
        'moe_w_grp': nrm((DEPTH, D_MODEL, N_GROUPS), D_MODEL ** -0.5),
        'moe_b_grp': nrm((DEPTH, N_GROUPS), 0.01),
        'moe_w_exp': nrm((DEPTH, D_MODEL, N_EXPERTS), D_MODEL ** -0.5),
        'moe_b_exp': nrm((DEPTH, N_EXPERTS), 0.01),
        'moe_w_gate': nrm((DEPTH, N_EXPERTS, D_MODEL, D_EXPERT), D_MODEL ** -0.5),
        'moe_w_up': nrm((DEPTH, N_EXPERTS, D_MODEL, D_EXPERT), D_MODEL ** -0.5 * BETA),
        'moe_w_down': nrm((DEPTH, N_EXPERTS, D_EXPERT, D_MODEL), D_EXPERT ** -0.5 * BETA),
    }


def reference(x, ev_w_in, ev_ret_decay_f, ev_ret_decay_b, ev_lq1, ev_lk1, ev_lq2, ev_lk2, ev_subln, ev_w_out,
              od_w_in, od_q_norm, od_w_uq, od_kv_norm, od_w_ukv, od_gla_w2_f, od_gla_b_f, od_gla_w2_b, od_gla_b_b, od_gla_norm, od_w_out,
              ln1_g, ln1_b, ln2_g, ln2_b,
              moe_w_grp, moe_b_grp, moe_w_exp, moe_b_exp, moe_w_gate, moe_w_up, moe_w_down):
    for i in range(DEPTH):
        j = i // 2
        if i % 2 == 0:
            mix = _retention_diff_mixer(x, ev_w_in[j], ev_ret_decay_f[j], ev_ret_decay_b[j], ev_lq1[j], ev_lk1[j], ev_lq2[j], ev_lk2[j], ev_subln[j], ev_w_out[j], i)
        else:
            mix = _mla_gla_mixer(x, od_w_in[j], od_q_norm[j], od_w_uq[j], od_kv_norm[j], od_w_ukv[j], od_gla_w2_f[j], od_gla_b_f[j], od_gla_w2_b[j], od_gla_b_b[j], od_gla_norm[j], od_w_out[j])
        x = _layer_norm(ALPHA * x + mix, ln1_g[i], ln1_b[i])
        ffn = _hier_moe(x, moe_w_grp[i], moe_b_grp[i], moe_w_exp[i], moe_b_exp[i], moe_w_gate[i], moe_w_up[i], moe_w_down[i])
        x = _layer_norm(ALPHA * x + ffn, ln2_g[i], ln2_b[i])
    return x
```

```python
import functools
import math

import numpy as np
import jax
import jax.numpy as jnp
from jax import lax
from jax.experimental import pallas as pl
from jax.experimental.pallas import tpu as pltpu

F32 = jnp.float32
BF16 = jnp.bfloat16

HEADS = 4
LANE = 128
RET_THETA = 10000.0
ROPE_THETA = 500000.0
DIFF_HEAD_DIM = 64
DIFF_ROT_DIM = 16
MLA_Q_RANK = 256
MLA_KV_RANK = 128
MLA_NOPE = 64
MLA_ROPE = 32
MLA_V = 128
GLA_K_DIM = 64
GLA_GATE_RANK = 16
GLA_TAU = 16.0
N_GROUPS = 4
EXPERTS_PER_GROUP = 8
N_EXPERTS = N_GROUPS * EXPERTS_PER_GROUP
DEPTH = 2
ALPHA = (2.0 * DEPTH) ** 0.25
LN_EPS = 1e-5
RMS_EPS = 1e-6

VMEM_LIMIT = 48 * 1024 * 1024


def _div_pow2(x, n):
    return lax.shift_right_logical(x, int(n).bit_length() - 1)


def _mod_pow2(x, n):
    return lax.bitwise_and(x, int(n) - 1)


def _cparams(sem):
    return pltpu.CompilerParams(dimension_semantics=sem, vmem_limit_bytes=VMEM_LIMIT)


def _proj_kernel(*refs, rms):
    if rms:
        x_ref, g_ref, w_ref, o_ref, xb_ref = refs
    else:
        x_ref, w_ref, o_ref, xb_ref = refs

    @pl.when(pl.program_id(1) == 0)
    def _():
        x = x_ref[...]
        if rms:
            x = x * lax.rsqrt(jnp.mean(x * x, axis=-1, keepdims=True) + RMS_EPS) * g_ref[...]
        xb_ref[...] = x.astype(BF16)

    o_ref[...] = jnp.dot(xb_ref[...], w_ref[...], preferred_element_type=F32).astype(o_ref.dtype)


def _proj(x, w, *, k, xcol=0, tm=1024, tn=512, gain=None):
    n = x.shape[0]
    m = w.shape[1]
    tm = min(tm, n)
    tn = min(tn, m)
    in_specs = [pl.BlockSpec((tm, k), lambda i, j: (i, xcol))]
    args = [x]
    if gain is not None:
        in_specs.append(pl.BlockSpec((1, k), lambda i, j: (0, 0)))
        args.append(gain.reshape(1, k))
    in_specs.append(pl.BlockSpec((k, tn), lambda i, j: (0, j)))
    args.append(w)
    return pl.pallas_call(
        functools.partial(_proj_kernel, rms=gain is not None),
        grid=(n // tm, m // tn),
        in_specs=in_specs,
        out_specs=pl.BlockSpec((tm, tn), lambda i, j: (i, j)),
        out_shape=jax.ShapeDtypeStruct((n, m), F32),
        scratch_shapes=[pltpu.VMEM((tm, k), BF16)],
        compiler_params=_cparams(("parallel", "arbitrary")),
        name="proj",
    )(*args)


def _prep_kernel(*refs, has_tab, has_add, sh, transpose):
    refs = list(refs)
    z_ref = refs.pop(0)
    add_ref = refs.pop(0) if has_add else None
    if has_tab:
        c_ref, sa_ref, sb_ref = refs.pop(0), refs.pop(0), refs.pop(0)
    o_ref = refs.pop(0)
    z = z_ref[...]
    if has_tab:
        out = z * c_ref[...] + pltpu.roll(z, sh, axis=1) * sa_ref[...] + pltpu.roll(z, LANE - sh, axis=1) * sb_ref[...]
    else:
        out = z
    if has_add:
        out = out + add_ref[...]
    if transpose:
        out = out.T
    o_ref[...] = out.astype(o_ref.dtype)


def _prep(z, col0, *, batch, seq, per_head=True, tables=None, sh=0, add=None, add_col0=0, transpose=False, tm=512):
    tm = min(tm, seq)
    nt = seq // tm
    zmap = (lambda b, h, i: (b * nt + i, col0 + h)) if per_head else (lambda b, h, i: (b * nt + i, col0))
    in_specs = [pl.BlockSpec((tm, LANE), zmap)]
    args = [z]
    if add is not None:
        in_specs.append(pl.BlockSpec((tm, LANE), lambda b, h, i: (b * nt + i, add_col0 + h)))
        args.append(add)
    if tables is not None:
        for t in tables:
            in_specs.append(pl.BlockSpec((tm, LANE), lambda b, h, i: (i, 0)))
            args.append(t)
    if transpose:
        out_shape = jax.ShapeDtypeStruct((batch, HEADS, nt, LANE, tm), BF16)
        out_spec = pl.BlockSpec((None, None, None, LANE, tm), lambda b, h, i: (b, h, i, 0, 0))
    else:
        out_shape = jax.ShapeDtypeStruct((batch, HEADS, seq, LANE), BF16)
        out_spec = pl.BlockSpec((None, None, tm, LANE), lambda b, h, i: (b, h, i, 0))
    return pl.pallas_call(
        functools.partial(_prep_kernel, has_tab=tables is not None, has_add=add is not None, sh=sh, transpose=transpose),
        grid=(batch, HEADS, nt),
        in_specs=in_specs,
        out_specs=out_spec,
        out_shape=out_shape,
        compiler_params=_cparams(("parallel", "parallel", "parallel")),
        name="prep",
    )(*args)


def _rope_tables(seq, segs, theta, scale):
    pos = jnp.arange(seq, dtype=F32)
    c = jnp.full((seq, LANE), 1.0, F32)
    sa = jnp.zeros((seq, LANE), F32)
    sb = jnp.zeros((seq, LANE), F32)
    for start, rot in segs:
        half = rot // 2
        inv = jnp.power(jnp.float32(theta), -jnp.arange(0, rot, 2, dtype=F32) / rot)
        ang = pos[:, None] * inv[None, :]
        cos, sin = jnp.cos(ang), jnp.sin(ang)
        c = c.at[:, start:start + half].set(cos).at[:, start + half:start + rot].set(cos)
        sa = sa.at[:, start + half:start + rot].set(sin)
        sb = sb.at[:, start:start + half].set(-sin)
    return c * scale, sa * scale, sb * scale


def _flash_kernel(*refs, ncomp, tk, nk, lam_init):
    if ncomp == 2:
        q_ref, k_ref, vt_ref, lq1, lk1, lq2, lk2, g_ref, o_ref, m_sc, l_sc, acc_sc = refs
    else:
        q_ref, k_ref, vt_ref, o_ref, m_sc, l_sc, acc_sc = refs
    q = q_ref[...]
    if ncomp == 2:
        lane = lax.broadcasted_iota(jnp.int32, q.shape, 1)
        zero = jnp.zeros_like(q)
        qs = [jnp.where(lane < DIFF_HEAD_DIM, q, zero), jnp.where(lane >= DIFF_HEAD_DIM, q, zero)]
    else:
        qs = [q]
    m_sc[...] = jnp.full(m_sc.shape, -jnp.inf, F32)
    l_sc[...] = jnp.zeros(l_sc.shape, F32)
    acc_sc[...] = jnp.zeros(acc_sc.shape, F32)

    def body(j, carry):
        off = pl.multiple_of(j * tk, tk)
        k = k_ref[pl.ds(off, tk), :]
        vt = vt_ref[j]
        for c in range(ncomp):
            s = lax.dot_general(k, qs[c], (((1,), (1,)), ((), ())), preferred_element_type=F32)
            m_old = m_sc[c]
            m_new = jnp.maximum(m_old, jnp.max(s, axis=0, keepdims=True))
            alpha = jnp.exp(m_old - m_new)
            p = jnp.exp(s - m_new)
            l_sc[c] = alpha * l_sc[c] + jnp.sum(p, axis=0, keepdims=True)
            acc_sc[c] = alpha * acc_sc[c] + jnp.dot(vt, p.astype(BF16), preferred_element_type=F32)
            m_sc[c] = m_new
        return carry

    lax.fori_loop(0, nk, body, 0)
    o = acc_sc[0] / l_sc[0]
    if ncomp == 2:
        lam = (jnp.exp(jnp.sum(lq1[...] * lk1[...], keepdims=True))
               - jnp.exp(jnp.sum(lq2[...] * lk2[...], keepdims=True)) + lam_init)
        o = o - lam * (acc_sc[1] / l_sc[1])
        o = o * lax.rsqrt(jnp.mean(o * o, axis=0, keepdims=True) + RMS_EPS) * g_ref[...] * (1.0 - lam_init)
    o_ref[...] = o.T


def _flash(q, k, vt, *, batch, seq, tq=512, diff=None, lam_init=0.0):
    nk, tk = vt.shape[2], vt.shape[4]
    tq = min(tq, seq)
    nq = seq // tq
    ncomp = 2 if diff is not None else 1
    in_specs = [
        pl.BlockSpec((None, None, tq, LANE), lambda b, h, i: (b, h, i, 0)),
        pl.BlockSpec((None, None, seq, LANE), lambda b, h, i: (b, h, 0, 0)),
        pl.BlockSpec((None, None, nk, LANE, tk), lambda b, h, i: (b, h, 0, 0, 0)),
    ]
    args = [q, k, vt]
    if diff is not None:
        lq1, lk1, lq2, lk2, subln = diff
        for v in (lq1, lk1, lq2, lk2):
            in_specs.append(pl.BlockSpec((1, DIFF_HEAD_DIM), lambda b, h, i: (0, 0)))
            args.append(v.reshape(1, DIFF_HEAD_DIM))
        in_specs.append(pl.BlockSpec((LANE, 1), lambda b, h, i: (0, 0)))
        args.append(subln.reshape(LANE, 1))
    return pl.pallas_call(
        functools.partial(_flash_kernel, ncomp=ncomp, tk=tk, nk=nk, lam_init=lam_init),
        grid=(batch, HEADS, nq),
        in_specs=in_specs,
        out_specs=pl.BlockSpec((tq, LANE), lambda b, h, i: (b * nq + i, h)),
        out_shape=jax.ShapeDtypeStruct((batch * seq, HEADS * LANE), F32),
        scratch_shapes=[pltpu.VMEM((ncomp, 1, tq), F32), pltpu.VMEM((ncomp, 1, tq), F32),
                        pltpu.VMEM((ncomp, LANE, tq), F32)],
        compiler_params=_cparams(("parallel", "parallel", "parallel")),
        name="flash_diff" if diff is not None else "flash_mla",
    )(*args)


def _ret_kernel(dec_ref, qf, kf, vf, qb, kb, vb, of_ref, ob_ref, s_sc, *, chunk):
    h = pl.program_id(1)

    @pl.when(pl.program_id(2) == 0)
    def _():
        s_sc[...] = jnp.zeros(s_sc.shape, F32)

    ii = lax.broadcasted_iota(jnp.int32, (chunk, chunk), 0)
    jj = lax.broadcasted_iota(jnp.int32, (chunk, chunk), 1)
    r = lax.broadcasted_iota(jnp.int32, (chunk, 1), 0).astype(F32)
    for d, (q_ref, k_ref, v_ref, o_ref) in enumerate(((qf, kf, vf, of_ref), (qb, kb, vb, ob_ref))):
        la = -jnp.exp(jnp.full((1, 1), dec_ref[d, h], F32))
        if d == 0:
            mask, dist = ii >= jj, (ii - jj).astype(F32)
            qdec, kdec = jnp.exp(la * (r + 1.0)), jnp.exp(la * (chunk - 1.0 - r))
        else:
            mask, dist = jj > ii, (jj - ii).astype(F32)
            qdec, kdec = jnp.exp(la * (chunk - r)), jnp.exp(la * r)
        decay = jnp.where(mask, jnp.exp(jnp.where(mask, dist * la, 0.0)), 0.0)
        q, k, v = q_ref[...], k_ref[...], v_ref[...]
        s = lax.dot_general(q, k, (((1,), (1,)), ((), ())), preferred_element_type=F32)
        o = jnp.dot((s * decay).astype(BF16), v, preferred_element_type=F32)
        state = s_sc[d]
        o = o + qdec * jnp.dot(q, state.astype(BF16), preferred_element_type=F32)
        kd = (k.astype(F32) * kdec).astype(BF16)
        s_sc[d] = jnp.exp(la * float(chunk)) * state + lax.dot_general(
            kd, v, (((0,), (0,)), ((), ())), preferred_element_type=F32)
        o_ref[...] = o


def _retention(q, k, v, decays, *, batch, seq, chunk=256):
    chunk = min(chunk, seq)
    n = seq // chunk
    fwd = pl.BlockSpec((None, None, chunk, LANE), lambda b, h, c: (b, h, c, 0))
    bwd = pl.BlockSpec((None, None, chunk, LANE), lambda b, h, c: (b, h, n - 1 - c, 0))
    out = jax.ShapeDtypeStruct((batch * seq, HEADS * LANE), F32)
    return pl.pallas_call(
        functools.partial(_ret_kernel, chunk=chunk),
        grid=(batch, HEADS, n),
        in_specs=[pl.BlockSpec(memory_space=pltpu.SMEM), fwd, fwd, fwd, bwd, bwd, bwd],
        out_specs=[pl.BlockSpec((chunk, LANE), lambda b, h, c: (b * n + c, h)),
                   pl.BlockSpec((chunk, LANE), lambda b, h, c: (b * n + n - 1 - c, h))],
        out_shape=[out, out],
        scratch_shapes=[pltpu.VMEM((2, LANE, LANE), F32)],
        compiler_params=_cparams(("parallel", "parallel", "arbitrary")),
        name="retention",
    )(decays, q, k, v, q, k, v)


GLA_SUB = 16


def _split3(x):
    x1 = x.astype(BF16)
    r1 = x - x1.astype(F32)
    x2 = r1.astype(BF16)
    x3 = (r1 - x2.astype(F32)).astype(BF16)
    return x1, x2, x3


def _gla_direction(q, k, v, lr, w2, bias, st, reverse):
    C, wk = q.shape
    wv = v.shape[1]
    dk, dv = wk // HEADS, wv // HEADS
    z = jnp.dot(lr.astype(BF16), w2, preferred_element_type=F32) + bias
    g = (jnp.minimum(z, 0.0) - jnp.log1p(jnp.exp(-jnp.abs(z)))) * (1.0 / GLA_TAU)
    ii = lax.broadcasted_iota(jnp.int32, (C, C), 0)
    jj = lax.broadcasted_iota(jnp.int32, (C, C), 1)
    tri = jnp.where(ii >= jj, 1.0, 0.0).astype(BF16)
    b = sum(jnp.dot(tri, part, preferred_element_type=F32) for part in _split3(g))
    tot = b[C - 1:C, :]
    c = (tot - b + g) if reverse else b

    qe = (q * jnp.exp(jnp.minimum(c, 0.0))).astype(BF16)
    o = lax.dot_general(qe, st.astype(BF16), (((1,), (1,)), ((), ())), preferred_element_type=F32)
    ke = (k * jnp.exp(jnp.minimum(tot - c, 0.0))).astype(BF16)
    upd = lax.dot_general(v.astype(BF16), ke, (((0,), (0,)), ((), ())), preferred_element_type=F32)
    rr = _div_pow2(lax.broadcasted_iota(jnp.int32, (wv, wk), 0), dv)
    cc = _div_pow2(lax.broadcasted_iota(jnp.int32, (wv, wk), 1), dk)
    new_st = jnp.where(rr == cc, st * jnp.exp(tot) + upd, 0.0)

    lane_head = _div_pow2(lax.broadcasted_iota(jnp.int32, (C, wk), 1), dk)
    scores = [jnp.zeros((C, C), F32) for _ in range(HEADS)]
    hsz = C // 2
    while hsz >= GLA_SUB:
        blk = 2 * hsz
        rows = []
        for m in range(C // blk):
            rrow = m * blk + (hsz if reverse else hsz - 1)
            rows.append(jnp.broadcast_to(c[rrow:rrow + 1, :], (blk, wk)))
        ref = jnp.concatenate(rows, axis=0) if len(rows) > 1 else rows[0]
        qt = q * jnp.exp(jnp.minimum(c - ref, 0.0))
        kt = (k * jnp.exp(jnp.minimum(ref - c, 0.0))).astype(BF16)
        same = _div_pow2(ii, blk) == _div_pow2(jj, blk)
        if reverse:
            lvl = same & (_mod_pow2(ii, blk) < hsz) & (_mod_pow2(jj, blk) >= hsz)
        else:
            lvl = same & (_mod_pow2(ii, blk) >= hsz) & (_mod_pow2(jj, blk) < hsz)
        for h in range(HEADS):
            qh = jnp.where(lane_head == h, qt, 0.0).astype(BF16)
            s = lax.dot_general(qh, kt, (((1,), (1,)), ((), ())), preferred_element_type=F32)
            scores[h] = scores[h] + jnp.where(lvl, s, 0.0)
        hsz //= 2
    vb = v.astype(BF16)
    o = o + jnp.concatenate(
        [jnp.dot(scores[h].astype(BF16), vb[:, h * dv:(h + 1) * dv], preferred_element_type=F32) for h in range(HEADS)],
        axis=1)

    er = _div_pow2(lax.broadcasted_iota(jnp.int32, (wk, wv), 0), dk)
    ec = _div_pow2(lax.broadcasted_iota(jnp.int32, (wk, wv), 1), dv)
    expand = jnp.where(er == ec, 1.0, 0.0).astype(BF16)
    pos = _mod_pow2(lax.broadcasted_iota(jnp.int32, (C, 1), 0), GLA_SUB)
    for lag in range(1 if reverse else 0, GLA_SUB):
        if lag == 0:
            t, vs, valid = q * k, v, None
        else:
            shift = (C - lag) if reverse else lag
            ks, cs, vs = pltpu.roll(k, shift, axis=0), pltpu.roll(c, shift, axis=0), pltpu.roll(v, shift, axis=0)
            t = q * ks * jnp.exp(jnp.minimum(c - cs, 0.0))
            valid = (pos + lag < GLA_SUB) if reverse else (pos >= lag)
        red = jnp.dot(t.astype(BF16), expand, preferred_element_type=F32)
        if valid is not None:
            red = jnp.where(valid, red, 0.0)
        o = o + red * vs
    return o, new_st


def _gla_kernel(qf, kf, vf, lf, qb, kb, vb, lb, w2f, bf, w2b, bb, of_ref, ob_ref, s_sc, *, qscale):
    @pl.when(pl.program_id(1) == 0)
    def _():
        s_sc[...] = jnp.zeros(s_sc.shape, F32)

    o, st = _gla_direction(qf[...] * qscale, kf[...], vf[...], lf[...], w2f[...], bf[...], s_sc[0], False)
    of_ref[...] = o
    s_sc[0] = st
    o, st = _gla_direction(qb[...] * qscale, kb[...], vb[...], lb[...], w2b[...], bb[...], s_sc[1], True)
    ob_ref[...] = o
    s_sc[1] = st


def _gla(z, *, qcol, kcol, vcol, lcol, w2f, bf, w2b, bb, batch, seq, chunk=128):
    chunk = min(chunk, seq)
    n = seq // chunk
    wk, wv = HEADS * GLA_K_DIM, HEADS * LANE

    def specs(cmap):
        return [pl.BlockSpec((chunk, wk), lambda b, c: (cmap(b, c), qcol)),
                pl.BlockSpec((chunk, wk), lambda b, c: (cmap(b, c), kcol)),
                pl.BlockSpec((chunk, wv), lambda b, c: (cmap(b, c), vcol)),
                pl.BlockSpec((chunk, LANE), lambda b, c: (cmap(b, c), lcol))]

    fmap = lambda b, c: b * n + c
    bmap = lambda b, c: b * n + n - 1 - c
    wspec = [pl.BlockSpec((LANE, wk), lambda b, c: (0, 0)), pl.BlockSpec((1, wk), lambda b, c: (0, 0))]
    out = jax.ShapeDtypeStruct((batch * seq, wv), F32)
    return pl.pallas_call(
        functools.partial(_gla_kernel, qscale=GLA_K_DIM ** -0.5),
        grid=(batch, n),
        in_specs=specs(fmap) + specs(bmap) + wspec + wspec,
        out_specs=[pl.BlockSpec((chunk, wv), lambda b, c: (fmap(b, c), 0)),
                   pl.BlockSpec((chunk, wv), lambda b, c: (bmap(b, c), 0))],
        out_shape=[out, out],
        scratch_shapes=[pltpu.VMEM((2, wv, wk), F32)],
        compiler_params=_cparams(("parallel", "arbitrary")),
        name="gla",
    )(z, z, z, z, z, z, z, z, w2f, bf, w2b, bb)


def _layer_norm_rows(r, g, b):
    mu = jnp.mean(r, axis=-1, keepdims=True)
    d = r - mu
    var = jnp.mean(d * d, axis=-1, keepdims=True)
    return d * lax.rsqrt(var + LN_EPS) * g + b


def _outproj_kernel(x_ref, fin_ref, of_ref, ob_ref, gate_ref, ng_ref, wa_ref, wb_ref, lg_ref, lb_ref, o_ref, *, group_norm):
    lin = of_ref[...] + ob_ref[...]
    parts = []
    for h in range(HEADS):
        zh = lin[:, h * LANE:(h + 1) * LANE]
        if group_norm:
            mu = jnp.mean(zh, axis=-1, keepdims=True)
            dz = zh - mu
            parts.append(dz * lax.rsqrt(jnp.mean(dz * dz, axis=-1, keepdims=True) + LN_EPS))
        else:
            parts.append(zh * lax.rsqrt(jnp.mean(zh * zh, axis=-1, keepdims=True) + RMS_EPS) * ng_ref[...])
    gate = gate_ref[...]
    lin = jnp.concatenate(parts, axis=1) * (gate * jax.nn.sigmoid(gate))
    y = (jnp.dot(fin_ref[...].astype(BF16), wa_ref[...], preferred_element_type=F32)
         + jnp.dot(lin.astype(BF16), wb_ref[...], preferred_element_type=F32))
    o_ref[...] = _layer_norm_rows(ALPHA * x_ref[...] + y, lg_ref[...], lb_ref[...])


def _outproj(x, fin, of, ob, gate_src, gate_col, norm_gain, wa, wb, ln_g, ln_b, *, group_norm, tm=512):
    n, d = x.shape
    w = HEADS * LANE
    tm = min(tm, n)
    row = lambda i: (i, 0)
    const = lambda i: (0, 0)
    return pl.pallas_call(
        functools.partial(_outproj_kernel, group_norm=group_norm),
        grid=(n // tm,),
        in_specs=[pl.BlockSpec((tm, d), row), pl.BlockSpec((tm, w), row), pl.BlockSpec((tm, w), row),
                  pl.BlockSpec((tm, w), row), pl.BlockSpec((tm, w), lambda i: (i, gate_col)),
                  pl.BlockSpec((1, LANE), const), pl.BlockSpec((w, d), const), pl.BlockSpec((w, d), const),
                  pl.BlockSpec((1, d), const), pl.BlockSpec((1, d), const)],
        out_specs=pl.BlockSpec((tm, d), row),
        out_shape=jax.ShapeDtypeStruct((n, d), F32),
        compiler_params=_cparams(("parallel",)),
        name="outproj",
    )(x, fin, of, ob, gate_src, norm_gain.reshape(1, LANE), wa, wb, ln_g.reshape(1, d), ln_b.reshape(1, d))


def _route_kernel(x_ref, w_ref, b_ref, ids_ref, gw_ref, cnt_ref, run_sc):
    @pl.when(pl.program_id(0) == 0)
    def _():
        run_sc[...] = jnp.zeros(run_sc.shape, F32)

    tm = x_ref.shape[0]
    logits = jnp.dot(x_ref[...], w_ref[...], preferred_element_type=F32, precision=lax.Precision.HIGHEST) + b_ref[...]
    lane = lax.broadcasted_iota(jnp.int32, logits.shape, 1)
    neg = -jnp.inf
    gmask = (lane >= N_EXPERTS) & (lane < N_EXPERTS + N_GROUPS)
    gl = jnp.where(gmask, logits, neg)
    gmax = jnp.max(gl, axis=1, keepdims=True)
    lane_f = lane.astype(F32)
    first = lambda hit: jnp.min(jnp.where(hit, lane_f, float(LANE)), axis=1, keepdims=True).astype(jnp.int32)
    gidx = first(gl == gmax) - N_EXPERTS
    p_grp = 1.0 / jnp.sum(jnp.where(gmask, jnp.exp(gl - gmax), 0.0), axis=1, keepdims=True)
    el = jnp.where(_div_pow2(lane, EXPERTS_PER_GROUP) == gidx, logits, neg)
    l1 = jnp.max(el, axis=1, keepdims=True)
    e1 = first(el == l1)
    el2 = jnp.where(lane == e1, neg, el)
    l2 = jnp.max(el2, axis=1, keepdims=True)
    e2 = first(el2 == l2)
    t = jnp.exp(l2 - l1)
    w1 = p_grp / (1.0 + t)
    w2 = p_grp * t / (1.0 + t)

    onehot = jnp.where(lane == e1, 1.0, jnp.where(lane == e2, 1.0, 0.0))
    ri = lax.broadcasted_iota(jnp.int32, (tm, tm), 0)
    ci = lax.broadcasted_iota(jnp.int32, (tm, tm), 1)
    before = jnp.dot(jnp.where(ri > ci, 1.0, 0.0).astype(BF16), onehot.astype(BF16), preferred_element_type=F32)
    before = before + run_sc[...]
    r1 = jnp.sum(jnp.where(lane == e1, before, 0.0), axis=1, keepdims=True).astype(jnp.int32)
    r2 = jnp.sum(jnp.where(lane == e2, before, 0.0), axis=1, keepdims=True).astype(jnp.int32)
    run_sc[...] = run_sc[...] + jnp.sum(onehot, axis=0, keepdims=True)
    cnt_ref[...] = run_sc[...].astype(jnp.int32)
    ids_ref[...] = jnp.where(lane == 0, e1, jnp.where(lane == 1, e2, jnp.where(lane == 2, r1, jnp.where(lane == 3, r2, 0))))
    gw_ref[...] = jnp.where(lane == 0, w1, jnp.where(lane == 1, w2, 0.0))


def _route(x, w, b, *, tm=512):
    n, d = x.shape
    tm = min(tm, n)
    return pl.pallas_call(
        _route_kernel,
        grid=(n // tm,),
        in_specs=[pl.BlockSpec((tm, d), lambda i: (i, 0)), pl.BlockSpec((d, LANE), lambda i: (0, 0)),
                  pl.BlockSpec((1, LANE), lambda i: (0, 0))],
        out_specs=[pl.BlockSpec((tm, LANE), lambda i: (i, 0)), pl.BlockSpec((tm, LANE), lambda i: (i, 0)),
                   pl.BlockSpec((1, LANE), lambda i: (0, 0))],
        out_shape=[jax.ShapeDtypeStruct((n, LANE), jnp.int32), jax.ShapeDtypeStruct((n, LANE), F32),
                   jax.ShapeDtypeStruct((1, LANE), jnp.int32)],
        scratch_shapes=[pltpu.VMEM((1, LANE), F32)],
        compiler_params=_cparams(("arbitrary",)),
        name="route",
    )(x, w, b)


MOE_BM = 256


def _experts_kernel(blk_e_ref, nused_ref, slot_ref, x_hbm, wg_ref, wu_ref, wd_ref, y_hbm,
                    idx_smem, xbuf, ybuf, wgb, wub, wdb, sem_i, sem_g, sem_s, *, n_tok):
    i = pl.program_id(0)
    bm = xbuf.shape[0]

    @pl.when(i == 0)
    def _():
        ybuf[...] = jnp.zeros(ybuf.shape, F32)
        init = pltpu.make_async_copy(ybuf, y_hbm.at[pl.ds(2 * n_tok, bm), :], sem_s)
        init.start()
        init.wait()

    @pl.when(i < nused_ref[0])
    def _():
        cp = pltpu.make_async_copy(slot_ref.at[i], idx_smem, sem_i)
        cp.start()

        prev = blk_e_ref[jnp.maximum(i - 1, 0)]

        @pl.when(jnp.logical_or(i == 0, blk_e_ref[i] != prev))
        def _():
            wgb[...] = wg_ref[...].astype(BF16)
            wub[...] = wu_ref[...].astype(BF16)
            wdb[...] = wd_ref[...].astype(BF16)

        cp.wait()

        def gather(r, carry):
            a = idx_smem[r]
            tok = jnp.where(a >= 0, lax.rem(a, n_tok), 0)
            pltpu.make_async_copy(x_hbm.at[pl.ds(tok, 1), :], xbuf.at[pl.ds(r, 1), :], sem_g).start()
            return carry

        lax.fori_loop(0, bm, gather, 0)
        pltpu.make_async_copy(x_hbm.at[pl.ds(0, bm), :], xbuf, sem_g).wait()

        xb = xbuf[...].astype(BF16)
        hg = jnp.dot(xb, wgb[...], preferred_element_type=F32)
        hu = jnp.dot(xb, wub[...], preferred_element_type=F32)
        hid = (hg * jax.nn.sigmoid(hg) * hu).astype(BF16)
        ybuf[...] = jnp.dot(hid, wdb[...], preferred_element_type=F32)

        def scatter(r, carry):
            a = idx_smem[r]
            dst = jnp.where(a >= 0, a, 2 * n_tok + r)
            pltpu.make_async_copy(ybuf.at[pl.ds(r, 1), :], y_hbm.at[pl.ds(dst, 1), :], sem_s).start()
            return carry

        lax.fori_loop(0, bm, scatter, 0)
        pltpu.make_async_copy(ybuf, y_hbm.at[pl.ds(0, bm), :], sem_s).wait()


def _experts(x, slot_a, blk_e, n_used, w_gate, w_up, w_down):
    n, d = x.shape
    nb, bm = slot_a.shape
    de = w_gate.shape[2]
    grid_spec = pltpu.PrefetchScalarGridSpec(
        num_scalar_prefetch=2,
        grid=(nb,),
        in_specs=[pl.BlockSpec((nb, bm), lambda i, be, nu: (0, 0)),
                  pl.BlockSpec(memory_space=pl.ANY),
                  pl.BlockSpec((None, d, de), lambda i, be, nu: (be[i], 0, 0)),
                  pl.BlockSpec((None, d, de), lambda i, be, nu: (be[i], 0, 0)),
                  pl.BlockSpec((None, de, d), lambda i, be, nu: (be[i], 0, 0))],
        out_specs=pl.BlockSpec(memory_space=pl.ANY),
        scratch_shapes=[pltpu.SMEM((bm,), jnp.int32), pltpu.VMEM((bm, d), F32), pltpu.VMEM((bm, d), F32),
                        pltpu.VMEM((d, de), BF16), pltpu.VMEM((d, de), BF16), pltpu.VMEM((de, d), BF16),
                        pltpu.SemaphoreType.DMA(()), pltpu.SemaphoreType.DMA(()), pltpu.SemaphoreType.DMA(())],
    )
    return pl.pallas_call(
        functools.partial(_experts_kernel, n_tok=n),
        grid_spec=grid_spec,
        out_shape=jax.ShapeDtypeStruct((2 * n + bm, d), F32),
        compiler_params=_cparams(("arbitrary",)),
        name="experts",
    )(blk_e, n_used, slot_a, x, w_gate, w_up, w_down)


def _combine_kernel(x_ref, y0_ref, y1_ref, gw_ref, g_ref, b_ref, o_ref):
    gw = gw_ref[...]
    ffn = y0_ref[...] * gw[:, 0:1] + y1_ref[...] * gw[:, 1:2]
    o_ref[...] = _layer_norm_rows(ALPHA * x_ref[...] + ffn, g_ref[...], b_ref[...])


def _combine(x, y, gw, ln_g, ln_b, *, tm=512):
    n, d = x.shape
    tm = min(tm, n)
    nt = n // tm
    return pl.pallas_call(
        _combine_kernel,
        grid=(nt,),
        in_specs=[pl.BlockSpec((tm, d), lambda i: (i, 0)), pl.BlockSpec((tm, d), lambda i: (i, 0)),
                  pl.BlockSpec((tm, d), lambda i: (i + nt, 0)), pl.BlockSpec((tm, LANE), lambda i: (i, 0)),
                  pl.BlockSpec((1, d), lambda i: (0, 0)), pl.BlockSpec((1, d), lambda i: (0, 0))],
        out_specs=pl.BlockSpec((tm, d), lambda i: (i, 0)),
        out_shape=jax.ShapeDtypeStruct((n, d), F32),
        compiler_params=_cparams(("parallel",)),
        name="combine",
    )(x, y, y, gw, ln_g.reshape(1, d), ln_b.reshape(1, d))


def _moe(x, w_grp, b_grp, w_exp, b_exp, w_gate, w_up, w_down, ln_g, ln_b):
    n, d = x.shape
    wr = jnp.zeros((d, LANE), F32).at[:, :N_EXPERTS].set(w_exp).at[:, N_EXPERTS:N_EXPERTS + N_GROUPS].set(w_grp)
    br = jnp.zeros((1, LANE), F32).at[0, :N_EXPERTS].set(b_exp).at[0, N_EXPERTS:N_EXPERTS + N_GROUPS].set(b_grp)
    ids, gw, cnt = _route(x, wr, br)
    bm = MOE_BM
    counts = cnt[0, :N_EXPERTS]
    padded = (counts + bm - 1) // bm * bm
    pend = jnp.cumsum(padded)
    pstart = pend - padded
    e, r = ids[:, 0:2], ids[:, 2:4]
    dest = pstart[e] + r
    a_id = jnp.arange(n, dtype=jnp.int32)[:, None] + jnp.arange(2, dtype=jnp.int32)[None, :] * n
    cap = 2 * n + N_EXPERTS * bm
    nb = cap // bm
    slot_a = jnp.full((cap,), -1, jnp.int32).at[dest.reshape(-1)].set(a_id.reshape(-1)).reshape(nb, bm)
    blk_e = jnp.minimum(jnp.searchsorted(pend, jnp.arange(nb, dtype=jnp.int32) * bm, side='right'),
                        N_EXPERTS - 1).astype(jnp.int32)
    n_used = (pend[-1:] // bm).astype(jnp.int32)
    y = _experts(x, slot_a, blk_e, n_used, w_gate, w_up, w_down)
    return _combine(x, y, gw, ln_g, ln_b)


def _even_layer(x, batch, seq, layer_idx, w_in, dec_f, dec_b, lq1, lk1, lq2, lk2, subln, w_out, ln_g, ln_b):
    d = x.shape[1]
    w = HEADS * LANE
    z = _proj(x, w_in.astype(BF16), k=d)
    kw = dict(batch=batch, seq=seq)
    ret_seg = [(0, LANE)]
    q = _prep(z, 0, tables=_rope_tables(seq, ret_seg, RET_THETA, 1.0), sh=LANE // 2, **kw)
    k = _prep(z, 4, tables=_rope_tables(seq, ret_seg, RET_THETA, LANE ** -0.5), sh=LANE // 2, **kw)
    v = _prep(z, 8, **kw)
    decays = jnp.stack([dec_f, dec_b]).astype(F32)
    of, ob = _retention(q, k, v, decays, **kw)
    diff_seg = [(0, DIFF_ROT_DIM), (DIFF_HEAD_DIM, DIFF_ROT_DIM)]
    dq = _prep(z, 16, tables=_rope_tables(seq, diff_seg, ROPE_THETA, DIFF_HEAD_DIM ** -0.5), sh=DIFF_ROT_DIM // 2, **kw)
    dk = _prep(z, 20, tables=_rope_tables(seq, diff_seg, ROPE_THETA, 1.0), sh=DIFF_ROT_DIM // 2, **kw)
    dvt = _prep(z, 24, transpose=True, **kw)
    lam_init = 0.8 - 0.6 * math.exp(-0.3 * layer_idx)
    diff = _flash(dq, dk, dvt, diff=(lq1, lk1, lq2, lk2, subln), lam_init=lam_init, **kw)
    wo = w_out.astype(BF16)
    return _outproj(x, diff, of, ob, z, 3, jnp.ones((LANE,), F32), wo[w:], wo[:w], ln_g, ln_b, group_norm=True)


def _odd_layer(x, batch, seq, w_in, q_norm, w_uq, kv_norm, w_ukv, w2_f, b_f, w2_b, b_b, gla_norm, w_out, ln_g, ln_b):
    d = x.shape[1]
    w = HEADS * LANE
    o = np.cumsum([0, MLA_Q_RANK, MLA_KV_RANK, MLA_ROPE, HEADS * GLA_K_DIM, HEADS * GLA_K_DIM, w, w,
                   GLA_GATE_RANK, GLA_GATE_RANK]).tolist()
    zeros = lambda c: jnp.zeros((d, c), F32)
    w_in2 = jnp.concatenate([
        w_in[:, o[0]:o[2]], zeros(MLA_NOPE), w_in[:, o[2]:o[3]], zeros(LANE - MLA_NOPE - MLA_ROPE),
        w_in[:, o[3]:o[7]], w_in[:, o[7]:o[9]], zeros(LANE - 2 * GLA_GATE_RANK)], axis=1).astype(BF16)
    z = _proj(x, w_in2, k=d, tm=512, tn=w_in2.shape[1])
    kw = dict(batch=batch, seq=seq)
    qd = MLA_NOPE + MLA_ROPE
    w_uq2 = jnp.pad(w_uq.reshape(MLA_Q_RANK, HEADS, qd), ((0, 0), (0, 0), (0, LANE - qd))).reshape(MLA_Q_RANK, w)
    ukv = w_ukv.reshape(MLA_KV_RANK, HEADS, MLA_NOPE + MLA_V)
    w_uk2 = jnp.pad(ukv[:, :, :MLA_NOPE], ((0, 0), (0, 0), (0, LANE - MLA_NOPE))).reshape(MLA_KV_RANK, w)
    w_uv2 = ukv[:, :, MLA_NOPE:].reshape(MLA_KV_RANK, w)
    qh = _proj(z, w_uq2.astype(BF16), k=MLA_Q_RANK, xcol=0, gain=q_norm)
    kvh = _proj(z, jnp.concatenate([w_uk2, w_uv2], axis=1).astype(BF16), k=MLA_KV_RANK, xcol=2, gain=kv_norm)
    rope_seg = [(MLA_NOPE, MLA_ROPE)]
    q = _prep(qh, 0, tables=_rope_tables(seq, rope_seg, ROPE_THETA, qd ** -0.5), sh=MLA_ROPE // 2, **kw)
    k = _prep(z, 3, per_head=False, tables=_rope_tables(seq, rope_seg, ROPE_THETA, 1.0), sh=MLA_ROPE // 2,
              add=kvh, add_col0=0, **kw)
    vt = _prep(kvh, HEADS, transpose=True, **kw)
    mla = _flash(q, k, vt, **kw)
    wk = HEADS * GLA_K_DIM
    pad_rows = lambda m, r0: jnp.zeros((LANE, wk), F32).at[r0:r0 + GLA_GATE_RANK].set(m).astype(BF16)
    of, ob = _gla(z, qcol=2, kcol=3, vcol=2, lcol=16,
                  w2f=pad_rows(w2_f, 0), bf=b_f.reshape(1, wk), w2b=pad_rows(w2_b, GLA_GATE_RANK), bb=b_b.reshape(1, wk), **kw)
    wo = w_out.astype(BF16)
    return _outproj(x, mla, of, ob, z, 3, gla_norm, wo[:w], wo[w:], ln_g, ln_b, group_norm=False)


def kernel(x, ev_w_in, ev_ret_decay_f, ev_ret_decay_b, ev_lq1, ev_lk1, ev_lq2, ev_lk2, ev_subln, ev_w_out, od_w_in, od_q_norm, od_w_uq, od_kv_norm, od_w_ukv, od_gla_w2_f, od_gla_b_f, od_gla_w2_b, od_gla_b_b, od_gla_norm, od_w_out, ln1_g, ln1_b, ln2_g, ln2_b, moe_w_grp, moe_b_grp, moe_w_exp, moe_b_exp, moe_w_gate, moe_w_up, moe_w_down):
    batch, seq, d = x.shape
    h = x.reshape(batch * seq, d)
    for i in range(DEPTH):
        j = i // 2
        if i % 2 == 0:
            h = _even_layer(h, batch, seq, i, ev_w_in[j], ev_ret_decay_f[j], ev_ret_decay_b[j], ev_lq1[j], ev_lk1[j],
                            ev_lq2[j], ev_lk2[j], ev_subln[j], ev_w_out[j], ln1_g[i], ln1_b[i])
        else:
            h = _odd_layer(h, batch, seq, od_w_in[j], od_q_norm[j], od_w_uq[j], od_kv_norm[j], od_w_ukv[j],
                           od_gla_w2_f[j], od_gla_b_f[j], od_gla_w2_b[j], od_gla_b_b[j], od_gla_norm[j], od_w_out[j],
                           ln1_g[i], ln1_b[i])
        h = _moe(h, moe_w_grp[i], moe_b_grp[i], moe_w_exp[i], moe_b_exp[i], moe_w_gate[i], moe_w_up[i], moe_w_down[i],
                 ln2_g[i], ln2_b[i])
    return h.reshape(batch, seq, d)
```

```python
import functools
import math

import numpy as np
import jax
import jax.numpy as jnp
from jax import lax
from jax.experimental import pallas as pl
from jax.experimental.pallas import tpu as pltpu

F32 = jnp.float32
BF16 = jnp.bfloat16

HEADS = 4
LANE = 128
RET_THETA = 10000.0
ROPE_THETA = 500000.0
DIFF_HEAD_DIM = 64
DIFF_ROT_DIM = 16
MLA_Q_RANK = 256
MLA_KV_RANK = 128
MLA_NOPE = 64
MLA_ROPE = 32
MLA_V = 128
GLA_K_DIM = 64
GLA_GATE_RANK = 16
GLA_TAU = 16.0
N_GROUPS = 4
EXPERTS_PER_GROUP = 8
N_EXPERTS = N_GROUPS * EXPERTS_PER_GROUP
DEPTH = 2
ALPHA = (2.0 * DEPTH) ** 0.25
LN_EPS = 1e-5
RMS_EPS = 1e-6

VMEM_LIMIT = 48 * 1024 * 1024


def _div_pow2(x, n):
    return lax.shift_right_logical(x, int(n).bit_length() - 1)


def _mod_pow2(x, n):
    return lax.bitwise_and(x, int(n) - 1)


def _cparams(sem):
    return pltpu.CompilerParams(dimension_semantics=sem, vmem_limit_bytes=VMEM_LIMIT)


def _proj_kernel(*refs, rms):
    if rms:
        x_ref, g_ref, w_ref, o_ref, xb_ref = refs
    else:
        x_ref, w_ref, o_ref, xb_ref = refs

    @pl.when(pl.program_id(1) == 0)
    def _():
        x = x_ref[...]
        if rms:
            x = x * lax.rsqrt(jnp.mean(x * x, axis=-1, keepdims=True) + RMS_EPS) * g_ref[...]
        xb_ref[...] = x.astype(BF16)

    o_ref[...] = jnp.dot(xb_ref[...], w_ref[...], preferred_element_type=F32).astype(o_ref.dtype)


def _proj(x, w, *, k, xcol=0, tm=1024, tn=512, gain=None):
    n = x.shape[0]
    m = w.shape[1]
    tm = min(tm, n)
    tn = min(tn, m)
    in_specs = [pl.BlockSpec((tm, k), lambda i, j: (i, xcol))]
    args = [x]
    if gain is not None:
        in_specs.append(pl.BlockSpec((1, k), lambda i, j: (0, 0)))
        args.append(gain.reshape(1, k))
    in_specs.append(pl.BlockSpec((k, tn), lambda i, j: (0, j)))
    args.append(w)
    return pl.pallas_call(
        functools.partial(_proj_kernel, rms=gain is not None),
        grid=(n // tm, m // tn),
        in_specs=in_specs,
        out_specs=pl.BlockSpec((tm, tn), lambda i, j: (i, j)),
        out_shape=jax.ShapeDtypeStruct((n, m), F32),
        scratch_shapes=[pltpu.VMEM((tm, k), BF16)],
        compiler_params=_cparams(("parallel", "arbitrary")),
        name="proj",
    )(*args)


def _prep_kernel(*refs, has_tab, has_add, sh, transpose):
    refs = list(refs)
    z_ref = refs.pop(0)
    add_ref = refs.pop(0) if has_add else None
    if has_tab:
        c_ref, sa_ref, sb_ref = refs.pop(0), refs.pop(0), refs.pop(0)
    o_ref = refs.pop(0)
    z = z_ref[...]
    if has_tab:
        out = z * c_ref[...] + pltpu.roll(z, sh, axis=1) * sa_ref[...] + pltpu.roll(z, LANE - sh, axis=1) * sb_ref[...]
    else:
        out = z
    if has_add:
        out = out + add_ref[...]
    if transpose:
        o_ref[:LANE, :] = out.T.astype(o_ref.dtype)
        o_ref[LANE:, :] = jnp.ones((o_ref.shape[0] - LANE, o_ref.shape[1]), o_ref.dtype)
    else:
        o_ref[...] = out.astype(o_ref.dtype)


def _prep(z, col0, *, batch, seq, per_head=True, tables=None, sh=0, add=None, add_col0=0, transpose=False, tm=512):
    tm = min(tm, seq // 2) if transpose else min(tm, seq)
    nt = seq // tm
    zmap = (lambda b, h, i: (b * nt + i, col0 + h)) if per_head else (lambda b, h, i: (b * nt + i, col0))
    in_specs = [pl.BlockSpec((tm, LANE), zmap)]
    args = [z]
    if add is not None:
        in_specs.append(pl.BlockSpec((tm, LANE), lambda b, h, i: (b * nt + i, add_col0 + h)))
        args.append(add)
    if tables is not None:
        for t in tables:
            in_specs.append(pl.BlockSpec((tm, LANE), lambda b, h, i: (i, 0)))
            args.append(t)
    if transpose:
        out_shape = jax.ShapeDtypeStruct((batch, HEADS, nt, LANE + ONES_ROWS, tm), BF16)
        out_spec = pl.BlockSpec((None, None, None, LANE + ONES_ROWS, tm), lambda b, h, i: (b, h, i, 0, 0))
    else:
        out_shape = jax.ShapeDtypeStruct((batch, HEADS, seq, LANE), BF16)
        out_spec = pl.BlockSpec((None, None, tm, LANE), lambda b, h, i: (b, h, i, 0))
    return pl.pallas_call(
        functools.partial(_prep_kernel, has_tab=tables is not None, has_add=add is not None, sh=sh, transpose=transpose),
        grid=(batch, HEADS, nt),
        in_specs=in_specs,
        out_specs=out_spec,
        out_shape=out_shape,
        compiler_params=_cparams(("parallel", "parallel", "parallel")),
        name="prep",
    )(*args)


def _rope_tables(seq, segs, theta, scale):
    pos = jnp.arange(seq, dtype=F32)
    inv = jnp.zeros((LANE,), F32)
    lo = np.zeros((LANE,), bool)
    hi = np.zeros((LANE,), bool)
    for start, rot in segs:
        half = rot // 2
        f = jnp.power(jnp.float32(theta), -jnp.arange(0, rot, 2, dtype=F32) / rot)
        inv = inv.at[start:start + half].set(f).at[start + half:start + rot].set(f)
        lo[start:start + half] = True
        hi[start + half:start + rot] = True
    ang = pos[:, None] * inv[None, :]
    cos, sin = jnp.cos(ang), jnp.sin(ang)
    c = jnp.where(lo | hi, cos, 1.0) * scale
    sa = jnp.where(hi, sin, 0.0) * scale
    sb = jnp.where(lo, -sin, 0.0) * scale
    return c, sa, sb


ONES_ROWS = 16
LOG2E = math.log2(math.e)
FLASH_TK = 1024
FLASH_TK_DIFF = 512


def _flash_kernel(*refs, ncomp, nk, lam_init):
    if ncomp == 2:
        q_ref, k_ref, vt_ref, lq1, lk1, lq2, lk2, g_ref, o_ref, *scr = refs
    else:
        q_ref, k_ref, vt_ref, o_ref, *scr = refs
    qm_sc, m_sc, acc_sc, s0, s1, cm0, cm1, p0, p1, al0, al1 = scr
    tk = s0.shape[1]
    q = q_ref[...]
    if ncomp == 2:
        lane = lax.broadcasted_iota(jnp.int32, q.shape, 1)
        zero = jnp.zeros_like(q)
        qm_sc[0] = jnp.where(lane < DIFF_HEAD_DIM, q, zero)
        qm_sc[1] = jnp.where(lane >= DIFF_HEAD_DIM, q, zero)
    else:
        qm_sc[0] = q
    m_sc[...] = jnp.full(m_sc.shape, -jnp.inf, F32)
    acc_sc[...] = jnp.zeros(acc_sc.shape, F32)
    p1[...] = jnp.zeros(p1.shape, BF16)
    al1[...] = jnp.ones(al1.shape, F32)

    def scores(j, s_ref, cm_ref):
        off = pl.multiple_of(j * tk, tk)
        k = k_ref[pl.ds(off, tk), :]
        for c in range(ncomp):
            s = lax.dot_general(k, qm_sc[c], (((1,), (1,)), ((), ())), preferred_element_type=F32)
            s_ref[c] = s
            cm_ref[c] = jnp.max(s, axis=0, keepdims=True)

    def softmax(s_ref, cm_ref, p_ref, al_ref):
        for c in range(ncomp):
            m_old = m_sc[c]
            m_new = jnp.maximum(m_old, cm_ref[c])
            al_ref[c] = jnp.exp2(m_old - m_new)
            p_ref[c] = jnp.exp2(s_ref[c] - m_new).astype(BF16)
            m_sc[c] = m_new

    def values(j, p_ref, al_ref):
        vt = vt_ref[j]
        for c in range(ncomp):
            acc_sc[c] = al_ref[c] * acc_sc[c] + jnp.dot(vt, p_ref[c], preferred_element_type=F32)

    scores(0, s0, cm0)

    def body(jj, carry):
        a = 2 * jj
        scores(a + 1, s1, cm1)
        softmax(s0, cm0, p0, al0)
        values(jnp.maximum(a - 1, 0), p1, al1)
        scores(jnp.minimum(a + 2, nk - 1), s0, cm0)
        softmax(s1, cm1, p1, al1)
        values(a, p0, al0)
        return carry

    lax.fori_loop(0, nk // 2, body, 0)
    values(nk - 1, p1, al1)

    def normalised(c):
        acc = acc_sc[c]
        return acc[:LANE] / acc[LANE:LANE + 1]

    o = normalised(0)
    if ncomp == 2:
        lam = (jnp.exp(jnp.sum(lq1[...] * lk1[...], keepdims=True))
               - jnp.exp(jnp.sum(lq2[...] * lk2[...], keepdims=True)) + lam_init)
        o = o - lam * normalised(1)
        o = o * lax.rsqrt(jnp.mean(o * o, axis=0, keepdims=True) + RMS_EPS) * g_ref[...] * (1.0 - lam_init)
    o_ref[...] = o.T


def _flash(q, k, vt, *, batch, seq, tq=512, diff=None, lam_init=0.0):
    nk, vrows, tk = vt.shape[2], vt.shape[3], vt.shape[4]
    assert nk % 2 == 0 and vrows == LANE + ONES_ROWS
    tq = min(tq, seq)
    nq = seq // tq
    ncomp = 2 if diff is not None else 1
    in_specs = [
        pl.BlockSpec((None, None, tq, LANE), lambda b, h, i: (b, h, i, 0)),
        pl.BlockSpec((None, None, seq, LANE), lambda b, h, i: (b, h, 0, 0)),
        pl.BlockSpec((None, None, nk, vrows, tk), lambda b, h, i: (b, h, 0, 0, 0)),
    ]
    args = [q, k, vt]
    if diff is not None:
        lq1, lk1, lq2, lk2, subln = diff
        for v in (lq1, lk1, lq2, lk2):
            in_specs.append(pl.BlockSpec((1, DIFF_HEAD_DIM), lambda b, h, i: (0, 0)))
            args.append(v.reshape(1, DIFF_HEAD_DIM))
        in_specs.append(pl.BlockSpec((LANE, 1), lambda b, h, i: (0, 0)))
        args.append(subln.reshape(LANE, 1))
    return pl.pallas_call(
        functools.partial(_flash_kernel, ncomp=ncomp, nk=nk, lam_init=lam_init),
        grid=(batch, HEADS, nq),
        in_specs=in_specs,
        out_specs=pl.BlockSpec((tq, LANE), lambda b, h, i: (b * nq + i, h)),
        out_shape=jax.ShapeDtypeStruct((batch * seq, HEADS * LANE), F32),
        scratch_shapes=[pltpu.VMEM((ncomp, tq, LANE), BF16),
                        pltpu.VMEM((ncomp, 1, tq), F32), pltpu.VMEM((ncomp, vrows, tq), F32),
                        pltpu.VMEM((ncomp, tk, tq), F32), pltpu.VMEM((ncomp, tk, tq), F32),
                        pltpu.VMEM((ncomp, 1, tq), F32), pltpu.VMEM((ncomp, 1, tq), F32),
                        pltpu.VMEM((ncomp, tk, tq), BF16), pltpu.VMEM((ncomp, tk, tq), BF16),
                        pltpu.VMEM((ncomp, 1, tq), F32), pltpu.VMEM((ncomp, 1, tq), F32)],
        compiler_params=_cparams(("parallel", "parallel", "parallel")),
        name="flash_diff" if diff is not None else "flash_mla",
    )(*args)


def _ret_kernel(dec_ref, qf, kf, vf, qb, kb, vb, of_ref, ob_ref, s_sc, *, chunk):
    h = pl.program_id(1)

    @pl.when(pl.program_id(2) == 0)
    def _():
        s_sc[...] = jnp.zeros(s_sc.shape, F32)

    ii = lax.broadcasted_iota(jnp.int32, (chunk, chunk), 0)
    jj = lax.broadcasted_iota(jnp.int32, (chunk, chunk), 1)
    r = lax.broadcasted_iota(jnp.int32, (chunk, 1), 0).astype(F32)
    for d, (q_ref, k_ref, v_ref, o_ref) in enumerate(((qf, kf, vf, of_ref), (qb, kb, vb, ob_ref))):
        la = -jnp.exp(jnp.full((1, 1), dec_ref[d, h], F32))
        if d == 0:
            mask, dist = ii >= jj, (ii - jj).astype(F32)
            qdec, kdec = jnp.exp(la * (r + 1.0)), jnp.exp(la * (chunk - 1.0 - r))
        else:
            mask, dist = jj > ii, (jj - ii).astype(F32)
            qdec, kdec = jnp.exp(la * (chunk - r)), jnp.exp(la * r)
        decay = jnp.where(mask, jnp.exp(jnp.where(mask, dist * la, 0.0)), 0.0)
        q, k, v = q_ref[...], k_ref[...], v_ref[...]
        s = lax.dot_general(q, k, (((1,), (1,)), ((), ())), preferred_element_type=F32)
        o = jnp.dot((s * decay).astype(BF16), v, preferred_element_type=F32)
        state = s_sc[d]
        o = o + qdec * jnp.dot(q, state.astype(BF16), preferred_element_type=F32)
        kd = (k.astype(F32) * kdec).astype(BF16)
        s_sc[d] = jnp.exp(la * float(chunk)) * state + lax.dot_general(
            kd, v, (((0,), (0,)), ((), ())), preferred_element_type=F32)
        o_ref[...] = o


def _retention(q, k, v, decays, *, batch, seq, chunk=256):
    chunk = min(chunk, seq)
    n = seq // chunk
    fwd = pl.BlockSpec((None, None, chunk, LANE), lambda b, h, c: (b, h, c, 0))
    bwd = pl.BlockSpec((None, None, chunk, LANE), lambda b, h, c: (b, h, n - 1 - c, 0))
    out = jax.ShapeDtypeStruct((batch * seq, HEADS * LANE), F32)
    return pl.pallas_call(
        functools.partial(_ret_kernel, chunk=chunk),
        grid=(batch, HEADS, n),
        in_specs=[pl.BlockSpec(memory_space=pltpu.SMEM), fwd, fwd, fwd, bwd, bwd, bwd],
        out_specs=[pl.BlockSpec((chunk, LANE), lambda b, h, c: (b * n + c, h)),
                   pl.BlockSpec((chunk, LANE), lambda b, h, c: (b * n + n - 1 - c, h))],
        out_shape=[out, out],
        scratch_shapes=[pltpu.VMEM((2, LANE, LANE), F32)],
        compiler_params=_cparams(("parallel", "parallel", "arbitrary")),
        name="retention",
    )(decays, q, k, v, q, k, v)


GLA_SUB = 16


def _split3(x):
    x1 = x.astype(BF16)
    r1 = x - x1.astype(F32)
    x2 = r1.astype(BF16)
    x3 = (r1 - x2.astype(F32)).astype(BF16)
    return x1, x2, x3


def _gla_direction(q, k, v, lr, w2, bias, st, reverse):
    C, wk = q.shape
    wv = v.shape[1]
    dk, dv = wk // HEADS, wv // HEADS
    z = jnp.dot(lr.astype(BF16), w2, preferred_element_type=F32) + bias
    g = (jnp.minimum(z, 0.0) - jnp.log1p(jnp.exp(-jnp.abs(z)))) * (1.0 / GLA_TAU)
    ii = lax.broadcasted_iota(jnp.int32, (C, C), 0)
    jj = lax.broadcasted_iota(jnp.int32, (C, C), 1)
    tri = jnp.where(ii >= jj, 1.0, 0.0).astype(BF16)
    b = sum(jnp.dot(tri, part, preferred_element_type=F32) for part in _split3(g))
    tot = b[C - 1:C, :]
    c = (tot - b + g) if reverse else b

    qe = (q * jnp.exp(jnp.minimum(c, 0.0))).astype(BF16)
    o = lax.dot_general(qe, st.astype(BF16), (((1,), (1,)), ((), ())), preferred_element_type=F32)
    ke = (k * jnp.exp(jnp.minimum(tot - c, 0.0))).astype(BF16)
    upd = lax.dot_general(v.astype(BF16), ke, (((0,), (0,)), ((), ())), preferred_element_type=F32)
    rr = _div_pow2(lax.broadcasted_iota(jnp.int32, (wv, wk), 0), dv)
    cc = _div_pow2(lax.broadcasted_iota(jnp.int32, (wv, wk), 1), dk)
    new_st = jnp.where(rr == cc, st * jnp.exp(tot) + upd, 0.0)

    lane_head = _div_pow2(lax.broadcasted_iota(jnp.int32, (C, wk), 1), dk)
    scores = [jnp.zeros((C, C), F32) for _ in range(HEADS)]
    hsz = C // 2
    while hsz >= GLA_SUB:
        blk = 2 * hsz
        rows = []
        for m in range(C // blk):
            rrow = m * blk + (hsz if reverse else hsz - 1)
            rows.append(jnp.broadcast_to(c[rrow:rrow + 1, :], (blk, wk)))
        ref = jnp.concatenate(rows, axis=0) if len(rows) > 1 else rows[0]
        qt = q * jnp.exp(jnp.minimum(c - ref, 0.0))
        kt = (k * jnp.exp(jnp.minimum(ref - c, 0.0))).astype(BF16)
        same = _div_pow2(ii, blk) == _div_pow2(jj, blk)
        if reverse:
            lvl = same & (_mod_pow2(ii, blk) < hsz) & (_mod_pow2(jj, blk) >= hsz)
        else:
            lvl = same & (_mod_pow2(ii, blk) >= hsz) & (_mod_pow2(jj, blk) < hsz)
        for h in range(HEADS):
            qh = jnp.where(lane_head == h, qt, 0.0).astype(BF16)
            s = lax.dot_general(qh, kt, (((1,), (1,)), ((), ())), preferred_element_type=F32)
            scores[h] = scores[h] + jnp.where(lvl, s, 0.0)
        hsz //= 2
    vb = v.astype(BF16)
    o = o + jnp.concatenate(
        [jnp.dot(scores[h].astype(BF16), vb[:, h * dv:(h + 1) * dv], preferred_element_type=F32) for h in range(HEADS)],
        axis=1)

    er = _div_pow2(lax.broadcasted_iota(jnp.int32, (wk, wv), 0), dk)
    ec = _div_pow2(lax.broadcasted_iota(jnp.int32, (wk, wv), 1), dv)
    expand = jnp.where(er == ec, 1.0, 0.0).astype(BF16)
    pos = _mod_pow2(lax.broadcasted_iota(jnp.int32, (C, 1), 0), GLA_SUB)
    for lag in range(1 if reverse else 0, GLA_SUB):
        if lag == 0:
            t, vs, valid = q * k, v, None
        else:
            shift = (C - lag) if reverse else lag
            ks, cs, vs = pltpu.roll(k, shift, axis=0), pltpu.roll(c, shift, axis=0), pltpu.roll(v, shift, axis=0)
            t = q * ks * jnp.exp(jnp.minimum(c - cs, 0.0))
            valid = (pos + lag < GLA_SUB) if reverse else (pos >= lag)
        red = jnp.dot(t.astype(BF16), expand, preferred_element_type=F32)
        if valid is not None:
            red = jnp.where(valid, red, 0.0)
        o = o + red * vs
    return o, new_st


def _gla_kernel(qf, kf, vf, lf, qb, kb, vb, lb, w2f, bf, w2b, bb, of_ref, ob_ref, s_sc, *, qscale):
    @pl.when(pl.program_id(1) == 0)
    def _():
        s_sc[...] = jnp.zeros(s_sc.shape, F32)

    o, st = _gla_direction(qf[...] * qscale, kf[...], vf[...], lf[...], w2f[...], bf[...], s_sc[0], False)
    of_ref[...] = o
    s_sc[0] = st
    o, st = _gla_direction(qb[...] * qscale, kb[...], vb[...], lb[...], w2b[...], bb[...], s_sc[1], True)
    ob_ref[...] = o
    s_sc[1] = st


def _gla(z, *, qcol, kcol, vcol, lcol, w2f, bf, w2b, bb, batch, seq, chunk=128):
    chunk = min(chunk, seq)
    n = seq // chunk
    wk, wv = HEADS * GLA_K_DIM, HEADS * LANE

    def specs(cmap):
        return [pl.BlockSpec((chunk, wk), lambda b, c: (cmap(b, c), qcol)),
                pl.BlockSpec((chunk, wk), lambda b, c: (cmap(b, c), kcol)),
                pl.BlockSpec((chunk, wv), lambda b, c: (cmap(b, c), vcol)),
                pl.BlockSpec((chunk, LANE), lambda b, c: (cmap(b, c), lcol))]

    fmap = lambda b, c: b * n + c
    bmap = lambda b, c: b * n + n - 1 - c
    wspec = [pl.BlockSpec((LANE, wk), lambda b, c: (0, 0)), pl.BlockSpec((1, wk), lambda b, c: (0, 0))]
    out = jax.ShapeDtypeStruct((batch * seq, wv), F32)
    return pl.pallas_call(
        functools.partial(_gla_kernel, qscale=GLA_K_DIM ** -0.5),
        grid=(batch, n),
        in_specs=specs(fmap) + specs(bmap) + wspec + wspec,
        out_specs=[pl.BlockSpec((chunk, wv), lambda b, c: (fmap(b, c), 0)),
                   pl.BlockSpec((chunk, wv), lambda b, c: (bmap(b, c), 0))],
        out_shape=[out, out],
        scratch_shapes=[pltpu.VMEM((2, wv, wk), F32)],
        compiler_params=_cparams(("parallel", "arbitrary")),
        name="gla",
    )(z, z, z, z, z, z, z, z, w2f, bf, w2b, bb)


def _layer_norm_rows(r, g, b):
    mu = jnp.mean(r, axis=-1, keepdims=True)
    d = r - mu
    var = jnp.mean(d * d, axis=-1, keepdims=True)
    return d * lax.rsqrt(var + LN_EPS) * g + b


def _outproj_kernel(x_ref, fin_ref, of_ref, ob_ref, gate_ref, ng_ref, wa_ref, wb_ref, lg_ref, lb_ref, o_ref, *, group_norm):
    lin = of_ref[...] + ob_ref[...]
    parts = []
    for h in range(HEADS):
        zh = lin[:, h * LANE:(h + 1) * LANE]
        if group_norm:
            mu = jnp.mean(zh, axis=-1, keepdims=True)
            dz = zh - mu
            parts.append(dz * lax.rsqrt(jnp.mean(dz * dz, axis=-1, keepdims=True) + LN_EPS))
        else:
            parts.append(zh * lax.rsqrt(jnp.mean(zh * zh, axis=-1, keepdims=True) + RMS_EPS) * ng_ref[...])
    gate = gate_ref[...]
    lin = jnp.concatenate(parts, axis=1) * (gate * jax.nn.sigmoid(gate))
    y = (jnp.dot(fin_ref[...].astype(BF16), wa_ref[...], preferred_element_type=F32)
         + jnp.dot(lin.astype(BF16), wb_ref[...], preferred_element_type=F32))
    o_ref[...] = _layer_norm_rows(ALPHA * x_ref[...] + y, lg_ref[...], lb_ref[...])


def _outproj(x, fin, of, ob, gate_src, gate_col, norm_gain, wa, wb, ln_g, ln_b, *, group_norm, tm=512):
    n, d = x.shape
    w = HEADS * LANE
    tm = min(tm, n)
    row = lambda i: (i, 0)
    const = lambda i: (0, 0)
    return pl.pallas_call(
        functools.partial(_outproj_kernel, group_norm=group_norm),
        grid=(n // tm,),
        in_specs=[pl.BlockSpec((tm, d), row), pl.BlockSpec((tm, w), row), pl.BlockSpec((tm, w), row),
                  pl.BlockSpec((tm, w), row), pl.BlockSpec((tm, w), lambda i: (i, gate_col)),
                  pl.BlockSpec((1, LANE), const), pl.BlockSpec((w, d), const), pl.BlockSpec((w, d), const),
                  pl.BlockSpec((1, d), const), pl.BlockSpec((1, d), const)],
        out_specs=pl.BlockSpec((tm, d), row),
        out_shape=jax.ShapeDtypeStruct((n, d), F32),
        compiler_params=_cparams(("parallel",)),
        name="outproj",
    )(x, fin, of, ob, gate_src, norm_gain.reshape(1, LANE), wa, wb, ln_g.reshape(1, d), ln_b.reshape(1, d))


def _route_kernel(x_ref, w_ref, b_ref, ids_ref, gw_ref, cnt_ref, run_sc):
    @pl.when(pl.program_id(0) == 0)
    def _():
        run_sc[...] = jnp.zeros(run_sc.shape, F32)

    tm = x_ref.shape[0]
    logits = jnp.dot(x_ref[...], w_ref[...], preferred_element_type=F32, precision=lax.Precision.HIGHEST) + b_ref[...]
    lane = lax.broadcasted_iota(jnp.int32, logits.shape, 1)
    neg = -jnp.inf
    gmask = (lane >= N_EXPERTS) & (lane < N_EXPERTS + N_GROUPS)
    gl = jnp.where(gmask, logits, neg)
    gmax = jnp.max(gl, axis=1, keepdims=True)
    lane_f = lane.astype(F32)
    first = lambda hit: jnp.min(jnp.where(hit, lane_f, float(LANE)), axis=1, keepdims=True).astype(jnp.int32)
    gidx = first(gl == gmax) - N_EXPERTS
    p_grp = 1.0 / jnp.sum(jnp.where(gmask, jnp.exp(gl - gmax), 0.0), axis=1, keepdims=True)
    el = jnp.where(_div_pow2(lane, EXPERTS_PER_GROUP) == gidx, logits, neg)
    l1 = jnp.max(el, axis=1, keepdims=True)
    e1 = first(el == l1)
    el2 = jnp.where(lane == e1, neg, el)
    l2 = jnp.max(el2, axis=1, keepdims=True)
    e2 = first(el2 == l2)
    t = jnp.exp(l2 - l1)
    w1 = p_grp / (1.0 + t)
    w2 = p_grp * t / (1.0 + t)

    onehot = jnp.where(lane == e1, 1.0, jnp.where(lane == e2, 1.0, 0.0))
    ri = lax.broadcasted_iota(jnp.int32, (tm, tm), 0)
    ci = lax.broadcasted_iota(jnp.int32, (tm, tm), 1)
    before = jnp.dot(jnp.where(ri > ci, 1.0, 0.0).astype(BF16), onehot.astype(BF16), preferred_element_type=F32)
    before = before + run_sc[...]
    r1 = jnp.sum(jnp.where(lane == e1, before, 0.0), axis=1, keepdims=True).astype(jnp.int32)
    r2 = jnp.sum(jnp.where(lane == e2, before, 0.0), axis=1, keepdims=True).astype(jnp.int32)
    run_sc[...] = run_sc[...] + jnp.sum(onehot, axis=0, keepdims=True)
    cnt_ref[...] = run_sc[...].astype(jnp.int32)
    ids_ref[...] = jnp.where(lane == 0, e1, jnp.where(lane == 1, e2, jnp.where(lane == 2, r1, jnp.where(lane == 3, r2, 0))))
    gw_ref[...] = jnp.where(lane == 0, w1, jnp.where(lane == 1, w2, 0.0))


def _route(x, w, b, *, tm=512):
    n, d = x.shape
    tm = min(tm, n)
    return pl.pallas_call(
        _route_kernel,
        grid=(n // tm,),
        in_specs=[pl.BlockSpec((tm, d), lambda i: (i, 0)), pl.BlockSpec((d, LANE), lambda i: (0, 0)),
                  pl.BlockSpec((1, LANE), lambda i: (0, 0))],
        out_specs=[pl.BlockSpec((tm, LANE), lambda i: (i, 0)), pl.BlockSpec((tm, LANE), lambda i: (i, 0)),
                   pl.BlockSpec((1, LANE), lambda i: (0, 0))],
        out_shape=[jax.ShapeDtypeStruct((n, LANE), jnp.int32), jax.ShapeDtypeStruct((n, LANE), F32),
                   jax.ShapeDtypeStruct((1, LANE), jnp.int32)],
        scratch_shapes=[pltpu.VMEM((1, LANE), F32)],
        compiler_params=_cparams(("arbitrary",)),
        name="route",
    )(x, w, b)


MOE_BM = 256


def _experts_kernel(blk_e_ref, nused_ref, slot_ref, x_hbm, wg_ref, wu_ref, wd_ref, y_hbm,
                    idx_smem, xbuf, ybuf, wgb, wub, wdb, sem_i, sem_g, sem_s, *, n_tok):
    i = pl.program_id(0)
    bm = xbuf.shape[0]

    @pl.when(i == 0)
    def _():
        ybuf[...] = jnp.zeros(ybuf.shape, F32)
        init = pltpu.make_async_copy(ybuf, y_hbm.at[pl.ds(2 * n_tok, bm), :], sem_s)
        init.start()
        init.wait()

    @pl.when(i < nused_ref[0])
    def _():
        cp = pltpu.make_async_copy(slot_ref.at[i], idx_smem, sem_i)
        cp.start()

        prev = blk_e_ref[jnp.maximum(i - 1, 0)]

        @pl.when(jnp.logical_or(i == 0, blk_e_ref[i] != prev))
        def _():
            wgb[...] = wg_ref[...].astype(BF16)
            wub[...] = wu_ref[...].astype(BF16)
            wdb[...] = wd_ref[...].astype(BF16)

        cp.wait()

        def gather(r, carry):
            a = idx_smem[r]
            tok = jnp.where(a >= 0, lax.rem(a, n_tok), 0)
            pltpu.make_async_copy(x_hbm.at[pl.ds(tok, 1), :], xbuf.at[pl.ds(r, 1), :], sem_g).start()
            return carry

        lax.fori_loop(0, bm, gather, 0)
        pltpu.make_async_copy(x_hbm.at[pl.ds(0, bm), :], xbuf, sem_g).wait()

        xb = xbuf[...].astype(BF16)
        hg = jnp.dot(xb, wgb[...], preferred_element_type=F32)
        hu = jnp.dot(xb, wub[...], preferred_element_type=F32)
        hid = (hg * jax.nn.sigmoid(hg) * hu).astype(BF16)
        ybuf[...] = jnp.dot(hid, wdb[...], preferred_element_type=F32)

        def scatter(r, carry):
            a = idx_smem[r]
            dst = jnp.where(a >= 0, a, 2 * n_tok + r)
            pltpu.make_async_copy(ybuf.at[pl.ds(r, 1), :], y_hbm.at[pl.ds(dst, 1), :], sem_s).start()
            return carry

        lax.fori_loop(0, bm, scatter, 0)
        pltpu.make_async_copy(ybuf, y_hbm.at[pl.ds(0, bm), :], sem_s).wait()


def _experts(x, slot_a, blk_e, n_used, w_gate, w_up, w_down):
    n, d = x.shape
    nb, bm = slot_a.shape
    de = w_gate.shape[2]
    grid_spec = pltpu.PrefetchScalarGridSpec(
        num_scalar_prefetch=2,
        grid=(nb,),
        in_specs=[pl.BlockSpec((nb, bm), lambda i, be, nu: (0, 0)),
                  pl.BlockSpec(memory_space=pl.ANY),
                  pl.BlockSpec((None, d, de), lambda i, be, nu: (be[i], 0, 0)),
                  pl.BlockSpec((None, d, de), lambda i, be, nu: (be[i], 0, 0)),
                  pl.BlockSpec((None, de, d), lambda i, be, nu: (be[i], 0, 0))],
        out_specs=pl.BlockSpec(memory_space=pl.ANY),
        scratch_shapes=[pltpu.SMEM((bm,), jnp.int32), pltpu.VMEM((bm, d), F32), pltpu.VMEM((bm, d), F32),
                        pltpu.VMEM((d, de), BF16), pltpu.VMEM((d, de), BF16), pltpu.VMEM((de, d), BF16),
                        pltpu.SemaphoreType.DMA(()), pltpu.SemaphoreType.DMA(()), pltpu.SemaphoreType.DMA(())],
    )
    return pl.pallas_call(
        functools.partial(_experts_kernel, n_tok=n),
        grid_spec=grid_spec,
        out_shape=jax.ShapeDtypeStruct((2 * n + bm, d), F32),
        compiler_params=_cparams(("arbitrary",)),
        name="experts",
    )(blk_e, n_used, slot_a, x, w_gate, w_up, w_down)


def _combine_kernel(x_ref, y0_ref, y1_ref, gw_ref, g_ref, b_ref, o_ref):
    gw = gw_ref[...]
    ffn = y0_ref[...] * gw[:, 0:1] + y1_ref[...] * gw[:, 1:2]
    o_ref[...] = _layer_norm_rows(ALPHA * x_ref[...] + ffn, g_ref[...], b_ref[...])


def _combine(x, y, gw, ln_g, ln_b, *, tm=512):
    n, d = x.shape
    tm = min(tm, n)
    nt = n // tm
    return pl.pallas_call(
        _combine_kernel,
        grid=(nt,),
        in_specs=[pl.BlockSpec((tm, d), lambda i: (i, 0)), pl.BlockSpec((tm, d), lambda i: (i, 0)),
                  pl.BlockSpec((tm, d), lambda i: (i + nt, 0)), pl.BlockSpec((tm, LANE), lambda i: (i, 0)),
                  pl.BlockSpec((1, d), lambda i: (0, 0)), pl.BlockSpec((1, d), lambda i: (0, 0))],
        out_specs=pl.BlockSpec((tm, d), lambda i: (i, 0)),
        out_shape=jax.ShapeDtypeStruct((n, d), F32),
        compiler_params=_cparams(("parallel",)),
        name="combine",
    )(x, y, y, gw, ln_g.reshape(1, d), ln_b.reshape(1, d))


def _moe(x, w_grp, b_grp, w_exp, b_exp, w_gate, w_up, w_down, ln_g, ln_b):
    n, d = x.shape
    wr = jnp.zeros((d, LANE), F32).at[:, :N_EXPERTS].set(w_exp).at[:, N_EXPERTS:N_EXPERTS + N_GROUPS].set(w_grp)
    br = jnp.zeros((1, LANE), F32).at[0, :N_EXPERTS].set(b_exp).at[0, N_EXPERTS:N_EXPERTS + N_GROUPS].set(b_grp)
    ids, gw, cnt = _route(x, wr, br)
    bm = MOE_BM
    counts = cnt[0, :N_EXPERTS]
    padded = (counts + bm - 1) // bm * bm
    pend = jnp.cumsum(padded)
    pstart = pend - padded
    e, r = ids[:, 0:2], ids[:, 2:4]
    dest = pstart[e] + r
    a_id = jnp.arange(n, dtype=jnp.int32)[:, None] + jnp.arange(2, dtype=jnp.int32)[None, :] * n
    cap = 2 * n + N_EXPERTS * bm
    nb = cap // bm
    slot_a = jnp.full((cap,), -1, jnp.int32).at[dest.reshape(-1)].set(a_id.reshape(-1)).reshape(nb, bm)
    blk_e = jnp.minimum(jnp.searchsorted(pend, jnp.arange(nb, dtype=jnp.int32) * bm, side='right'),
                        N_EXPERTS - 1).astype(jnp.int32)
    n_used = (pend[-1:] // bm).astype(jnp.int32)
    y = _experts(x, slot_a, blk_e, n_used, w_gate, w_up, w_down)
    return _combine(x, y, gw, ln_g, ln_b)


def _even_layer(x, batch, seq, layer_idx, w_in, dec_f, dec_b, lq1, lk1, lq2, lk2, subln, w_out, ln_g, ln_b):
    d = x.shape[1]
    w = HEADS * LANE
    z = _proj(x, w_in.astype(BF16), k=d)
    kw = dict(batch=batch, seq=seq)
    ret_seg = [(0, LANE)]
    q = _prep(z, 0, tables=_rope_tables(seq, ret_seg, RET_THETA, 1.0), sh=LANE // 2, **kw)
    k = _prep(z, 4, tables=_rope_tables(seq, ret_seg, RET_THETA, LANE ** -0.5), sh=LANE // 2, **kw)
    v = _prep(z, 8, **kw)
    decays = jnp.stack([dec_f, dec_b]).astype(F32)
    of, ob = _retention(q, k, v, decays, **kw)
    diff_seg = [(0, DIFF_ROT_DIM), (DIFF_HEAD_DIM, DIFF_ROT_DIM)]
    dq = _prep(z, 16, tables=_rope_tables(seq, diff_seg, ROPE_THETA, DIFF_HEAD_DIM ** -0.5 * LOG2E), sh=DIFF_ROT_DIM // 2, **kw)
    dk = _prep(z, 20, tables=_rope_tables(seq, diff_seg, ROPE_THETA, 1.0), sh=DIFF_ROT_DIM // 2, **kw)
    dvt = _prep(z, 24, transpose=True, tm=FLASH_TK_DIFF, **kw)
    lam_init = 0.8 - 0.6 * math.exp(-0.3 * layer_idx)
    diff = _flash(dq, dk, dvt, diff=(lq1, lk1, lq2, lk2, subln), lam_init=lam_init, **kw)
    wo = w_out.astype(BF16)
    return _outproj(x, diff, of, ob, z, 3, jnp.ones((LANE,), F32), wo[w:], wo[:w], ln_g, ln_b, group_norm=True)


def _odd_layer(x, batch, seq, w_in, q_norm, w_uq, kv_norm, w_ukv, w2_f, b_f, w2_b, b_b, gla_norm, w_out, ln_g, ln_b):
    d = x.shape[1]
    w = HEADS * LANE
    o = np.cumsum([0, MLA_Q_RANK, MLA_KV_RANK, MLA_ROPE, HEADS * GLA_K_DIM, HEADS * GLA_K_DIM, w, w,
                   GLA_GATE_RANK, GLA_GATE_RANK]).tolist()
    zeros = lambda c: jnp.zeros((d, c), F32)
    w_in2 = jnp.concatenate([
        w_in[:, o[0]:o[2]], zeros(MLA_NOPE), w_in[:, o[2]:o[3]], zeros(LANE - MLA_NOPE - MLA_ROPE),
        w_in[:, o[3]:o[7]], w_in[:, o[7]:o[9]], zeros(LANE - 2 * GLA_GATE_RANK)], axis=1).astype(BF16)
    z = _proj(x, w_in2, k=d, tm=512, tn=w_in2.shape[1])
    kw = dict(batch=batch, seq=seq)
    qd = MLA_NOPE + MLA_ROPE
    w_uq2 = jnp.pad(w_uq.reshape(MLA_Q_RANK, HEADS, qd), ((0, 0), (0, 0), (0, LANE - qd))).reshape(MLA_Q_RANK, w)
    ukv = w_ukv.reshape(MLA_KV_RANK, HEADS, MLA_NOPE + MLA_V)
    w_uk2 = jnp.pad(ukv[:, :, :MLA_NOPE], ((0, 0), (0, 0), (0, LANE - MLA_NOPE))).reshape(MLA_KV_RANK, w)
    w_uv2 = ukv[:, :, MLA_NOPE:].reshape(MLA_KV_RANK, w)
    qh = _proj(z, w_uq2.astype(BF16), k=MLA_Q_RANK, xcol=0, gain=q_norm)
    kvh = _proj(z, jnp.concatenate([w_uk2, w_uv2], axis=1).astype(BF16), k=MLA_KV_RANK, xcol=2, gain=kv_norm)
    rope_seg = [(MLA_NOPE, MLA_ROPE)]
    q = _prep(qh, 0, tables=_rope_tables(seq, rope_seg, ROPE_THETA, qd ** -0.5 * LOG2E), sh=MLA_ROPE // 2, **kw)
    k = _prep(z, 3, per_head=False, tables=_rope_tables(seq, rope_seg, ROPE_THETA, 1.0), sh=MLA_ROPE // 2,
              add=kvh, add_col0=0, **kw)
    vt = _prep(kvh, HEADS, transpose=True, tm=FLASH_TK, **kw)
    mla = _flash(q, k, vt, **kw)
    wk = HEADS * GLA_K_DIM
    pad_rows = lambda m, r0: jnp.zeros((LANE, wk), F32).at[r0:r0 + GLA_GATE_RANK].set(m).astype(BF16)
    of, ob = _gla(z, qcol=2, kcol=3, vcol=2, lcol=16,
                  w2f=pad_rows(w2_f, 0), bf=b_f.reshape(1, wk), w2b=pad_rows(w2_b, GLA_GATE_RANK), bb=b_b.reshape(1, wk), **kw)
    wo = w_out.astype(BF16)
    return _outproj(x, mla, of, ob, z, 3, gla_norm, wo[:w], wo[w:], ln_g, ln_b, group_norm=False)


def kernel(x, ev_w_in, ev_ret_decay_f, ev_ret_decay_b, ev_lq1, ev_lk1, ev_lq2, ev_lk2, ev_subln, ev_w_out, od_w_in, od_q_norm, od_w_uq, od_kv_norm, od_w_ukv, od_gla_w2_f, od_gla_b_f, od_gla_w2_b, od_gla_b_b, od_gla_norm, od_w_out, ln1_g, ln1_b, ln2_g, ln2_b, moe_w_grp, moe_b_grp, moe_w_exp, moe_b_exp, moe_w_gate, moe_w_up, moe_w_down):
    batch, seq, d = x.shape
    h = x.reshape(batch * seq, d)
    for i in range(DEPTH):
        j = i // 2
        if i % 2 == 0:
            h = _even_layer(h, batch, seq, i, ev_w_in[j], ev_ret_decay_f[j], ev_ret_decay_b[j], ev_lq1[j], ev_lk1[j],
                            ev_lq2[j], ev_lk2[j], ev_subln[j], ev_w_out[j], ln1_g[i], ln1_b[i])
        else:
            h = _odd_layer(h, batch, seq, od_w_in[j], od_q_norm[j], od_w_uq[j], od_kv_norm[j], od_w_ukv[j],
                           od_gla_w2_f[j], od_gla_b_f[j], od_gla_w2_b[j], od_gla_b_b[j], od_gla_norm[j], od_w_out[j],
                           ln1_g[i], ln1_b[i])
        h = _moe(h, moe_w_grp[i], moe_b_grp[i], moe_w_exp[i], moe_b_exp[i], moe_w_gate[i], moe_w_up[i], moe_w_down[i],
                 ln2_g[i], ln2_b[i])
    return h.reshape(batch, seq, d)
```

```python
import functools
import math

import numpy as np
import jax
import jax.numpy as jnp
from jax import lax
from jax.experimental import pallas as pl
from jax.experimental.pallas import tpu as pltpu

F32 = jnp.float32
BF16 = jnp.bfloat16

HEADS = 4
LANE = 128
RET_THETA = 10000.0
ROPE_THETA = 500000.0
DIFF_HEAD_DIM = 64
DIFF_ROT_DIM = 16
MLA_Q_RANK = 256
MLA_KV_RANK = 128
MLA_NOPE = 64
MLA_ROPE = 32
MLA_V = 128
GLA_K_DIM = 64
GLA_GATE_RANK = 16
GLA_TAU = 16.0
N_GROUPS = 4
EXPERTS_PER_GROUP = 8
N_EXPERTS = N_GROUPS * EXPERTS_PER_GROUP
DEPTH = 2
ALPHA = (2.0 * DEPTH) ** 0.25
LN_EPS = 1e-5
RMS_EPS = 1e-6

VMEM_LIMIT = 48 * 1024 * 1024


def _div_pow2(x, n):
    return lax.shift_right_logical(x, int(n).bit_length() - 1)


def _mod_pow2(x, n):
    return lax.bitwise_and(x, int(n) - 1)


def _cparams(sem):
    return pltpu.CompilerParams(dimension_semantics=sem, vmem_limit_bytes=VMEM_LIMIT)


def _proj_kernel(*refs, rms):
    if rms:
        x_ref, g_ref, w_ref, o_ref, xb_ref = refs
    else:
        x_ref, w_ref, o_ref, xb_ref = refs

    @pl.when(pl.program_id(1) == 0)
    def _():
        x = x_ref[...]
        if rms:
            x = x * lax.rsqrt(jnp.mean(x * x, axis=-1, keepdims=True) + RMS_EPS) * g_ref[...]
        xb_ref[...] = x.astype(BF16)

    o_ref[...] = jnp.dot(xb_ref[...], w_ref[...], preferred_element_type=F32).astype(o_ref.dtype)


def _proj(x, w, *, k, xcol=0, tm=1024, tn=512, gain=None):
    n = x.shape[0]
    m = w.shape[1]
    tm = min(tm, n)
    tn = min(tn, m)
    in_specs = [pl.BlockSpec((tm, k), lambda i, j: (i, xcol))]
    args = [x]
    if gain is not None:
        in_specs.append(pl.BlockSpec((1, k), lambda i, j: (0, 0)))
        args.append(gain.reshape(1, k))
    in_specs.append(pl.BlockSpec((k, tn), lambda i, j: (0, j)))
    args.append(w)
    return pl.pallas_call(
        functools.partial(_proj_kernel, rms=gain is not None),
        grid=(n // tm, m // tn),
        in_specs=in_specs,
        out_specs=pl.BlockSpec((tm, tn), lambda i, j: (i, j)),
        out_shape=jax.ShapeDtypeStruct((n, m), F32),
        scratch_shapes=[pltpu.VMEM((tm, k), BF16)],
        compiler_params=_cparams(("parallel", "arbitrary")),
        name="proj",
    )(*args)


def _prep_kernel(*refs, has_tab, has_add, sh, transpose):
    refs = list(refs)
    z_ref = refs.pop(0)
    add_ref = refs.pop(0) if has_add else None
    if has_tab:
        c_ref, sa_ref, sb_ref = refs.pop(0), refs.pop(0), refs.pop(0)
    o_ref = refs.pop(0)
    z = z_ref[...]
    if has_tab:
        out = z * c_ref[...] + pltpu.roll(z, sh, axis=1) * sa_ref[...] + pltpu.roll(z, LANE - sh, axis=1) * sb_ref[...]
    else:
        out = z
    if has_add:
        out = out + add_ref[...]
    if transpose:
        o_ref[:LANE, :] = out.T.astype(o_ref.dtype)
        o_ref[LANE:, :] = jnp.ones((o_ref.shape[0] - LANE, o_ref.shape[1]), o_ref.dtype)
    else:
        o_ref[...] = out.astype(o_ref.dtype)


def _prep(z, col0, *, batch, seq, per_head=True, tables=None, sh=0, add=None, add_col0=0, transpose=False, tm=512):
    tm = min(tm, seq // 2) if transpose else min(tm, seq)
    nt = seq // tm
    zmap = (lambda b, h, i: (b * nt + i, col0 + h)) if per_head else (lambda b, h, i: (b * nt + i, col0))
    in_specs = [pl.BlockSpec((tm, LANE), zmap)]
    args = [z]
    if add is not None:
        in_specs.append(pl.BlockSpec((tm, LANE), lambda b, h, i: (b * nt + i, add_col0 + h)))
        args.append(add)
    if tables is not None:
        for t in tables:
            in_specs.append(pl.BlockSpec((tm, LANE), lambda b, h, i: (i, 0)))
            args.append(t)
    if transpose:
        out_shape = jax.ShapeDtypeStruct((batch, HEADS, nt, LANE + ONES_ROWS, tm), BF16)
        out_spec = pl.BlockSpec((None, None, None, LANE + ONES_ROWS, tm), lambda b, h, i: (b, h, i, 0, 0))
    else:
        out_shape = jax.ShapeDtypeStruct((batch, HEADS, seq, LANE), BF16)
        out_spec = pl.BlockSpec((None, None, tm, LANE), lambda b, h, i: (b, h, i, 0))
    return pl.pallas_call(
        functools.partial(_prep_kernel, has_tab=tables is not None, has_add=add is not None, sh=sh, transpose=transpose),
        grid=(batch, HEADS, nt),
        in_specs=in_specs,
        out_specs=out_spec,
        out_shape=out_shape,
        compiler_params=_cparams(("parallel", "parallel", "parallel")),
        name="prep",
    )(*args)


def _rope_tables(seq, segs, theta, scale):
    pos = jnp.arange(seq, dtype=F32)
    inv = jnp.zeros((LANE,), F32)
    lo = np.zeros((LANE,), bool)
    hi = np.zeros((LANE,), bool)
    for start, rot in segs:
        half = rot // 2
        f = jnp.power(jnp.float32(theta), -jnp.arange(0, rot, 2, dtype=F32) / rot)
        inv = inv.at[start:start + half].set(f).at[start + half:start + rot].set(f)
        lo[start:start + half] = True
        hi[start + half:start + rot] = True
    ang = pos[:, None] * inv[None, :]
    cos, sin = jnp.cos(ang), jnp.sin(ang)
    c = jnp.where(lo | hi, cos, 1.0) * scale
    sa = jnp.where(hi, sin, 0.0) * scale
    sb = jnp.where(lo, -sin, 0.0) * scale
    return c, sa, sb


ONES_ROWS = 16
LOG2E = math.log2(math.e)
FLASH_TK = 1024
FLASH_TK_DIFF = 512


def _flash_kernel(*refs, ncomp, nk, lam_init):
    if ncomp == 2:
        q_ref, k_ref, vt_ref, lq1, lk1, lq2, lk2, g_ref, o_ref, *scr = refs
    else:
        q_ref, k_ref, vt_ref, o_ref, *scr = refs
    qm_sc, m_sc, acc_sc, s0, s1, cm0, cm1, p0, p1, al0, al1 = scr
    tk = s0.shape[1]
    q = q_ref[...]
    if ncomp == 2:
        lane = lax.broadcasted_iota(jnp.int32, q.shape, 1)
        zero = jnp.zeros_like(q)
        qm_sc[0] = jnp.where(lane < DIFF_HEAD_DIM, q, zero)
        qm_sc[1] = jnp.where(lane >= DIFF_HEAD_DIM, q, zero)
    else:
        qm_sc[0] = q
    m_sc[...] = jnp.full(m_sc.shape, -jnp.inf, F32)
    acc_sc[...] = jnp.zeros(acc_sc.shape, F32)
    p1[...] = jnp.zeros(p1.shape, BF16)
    al1[...] = jnp.ones(al1.shape, F32)

    def scores(j, s_ref, cm_ref):
        off = pl.multiple_of(j * tk, tk)
        k = k_ref[pl.ds(off, tk), :]
        for c in range(ncomp):
            s = lax.dot_general(k, qm_sc[c], (((1,), (1,)), ((), ())), preferred_element_type=F32)
            s_ref[c] = s
            cm_ref[c] = jnp.max(s, axis=0, keepdims=True)

    def softmax(s_ref, cm_ref, p_ref, al_ref):
        for c in range(ncomp):
            m_old = m_sc[c]
            m_new = jnp.maximum(m_old, cm_ref[c])
            al_ref[c] = jnp.exp2(m_old - m_new)
            p_ref[c] = jnp.exp2(s_ref[c] - m_new).astype(BF16)
            m_sc[c] = m_new

    def values(j, p_ref, al_ref):
        vt = vt_ref[j]
        for c in range(ncomp):
            acc_sc[c] = al_ref[c] * acc_sc[c] + jnp.dot(vt, p_ref[c], preferred_element_type=F32)

    scores(0, s0, cm0)

    def body(jj, carry):
        a = 2 * jj
        scores(a + 1, s1, cm1)
        softmax(s0, cm0, p0, al0)
        values(jnp.maximum(a - 1, 0), p1, al1)
        scores(jnp.minimum(a + 2, nk - 1), s0, cm0)
        softmax(s1, cm1, p1, al1)
        values(a, p0, al0)
        return carry

    lax.fori_loop(0, nk // 2, body, 0)
    values(nk - 1, p1, al1)

    def normalised(c):
        acc = acc_sc[c]
        return acc[:LANE] / acc[LANE:LANE + 1]

    o = normalised(0)
    if ncomp == 2:
        lam = (jnp.exp(jnp.sum(lq1[...] * lk1[...], keepdims=True))
               - jnp.exp(jnp.sum(lq2[...] * lk2[...], keepdims=True)) + lam_init)
        o = o - lam * normalised(1)
        o = o * lax.rsqrt(jnp.mean(o * o, axis=0, keepdims=True) + RMS_EPS) * g_ref[...] * (1.0 - lam_init)
    o_ref[...] = o.T


def _flash(q, k, vt, *, batch, seq, tq=512, diff=None, lam_init=0.0):
    nk, vrows, tk = vt.shape[2], vt.shape[3], vt.shape[4]
    assert nk % 2 == 0 and vrows == LANE + ONES_ROWS
    tq = min(tq, seq)
    nq = seq // tq
    ncomp = 2 if diff is not None else 1
    in_specs = [
        pl.BlockSpec((None, None, tq, LANE), lambda b, h, i: (b, h, i, 0)),
        pl.BlockSpec((None, None, seq, LANE), lambda b, h, i: (b, h, 0, 0)),
        pl.BlockSpec((None, None, nk, vrows, tk), lambda b, h, i: (b, h, 0, 0, 0)),
    ]
    args = [q, k, vt]
    if diff is not None:
        lq1, lk1, lq2, lk2, subln = diff
        for v in (lq1, lk1, lq2, lk2):
            in_specs.append(pl.BlockSpec((1, DIFF_HEAD_DIM), lambda b, h, i: (0, 0)))
            args.append(v.reshape(1, DIFF_HEAD_DIM))
        in_specs.append(pl.BlockSpec((LANE, 1), lambda b, h, i: (0, 0)))
        args.append(subln.reshape(LANE, 1))
    return pl.pallas_call(
        functools.partial(_flash_kernel, ncomp=ncomp, nk=nk, lam_init=lam_init),
        grid=(batch, HEADS, nq),
        in_specs=in_specs,
        out_specs=pl.BlockSpec((tq, LANE), lambda b, h, i: (b * nq + i, h)),
        out_shape=jax.ShapeDtypeStruct((batch * seq, HEADS * LANE), F32),
        scratch_shapes=[pltpu.VMEM((ncomp, tq, LANE), BF16),
                        pltpu.VMEM((ncomp, 1, tq), F32), pltpu.VMEM((ncomp, vrows, tq), F32),
                        pltpu.VMEM((ncomp, tk, tq), F32), pltpu.VMEM((ncomp, tk, tq), F32),
                        pltpu.VMEM((ncomp, 1, tq), F32), pltpu.VMEM((ncomp, 1, tq), F32),
                        pltpu.VMEM((ncomp, tk, tq), BF16), pltpu.VMEM((ncomp, tk, tq), BF16),
                        pltpu.VMEM((ncomp, 1, tq), F32), pltpu.VMEM((ncomp, 1, tq), F32)],
        compiler_params=_cparams(("parallel", "parallel", "parallel")),
        name="flash_diff" if diff is not None else "flash_mla",
    )(*args)


def _ret_kernel(dec_ref, qf, kf, vf, qb, kb, vb, of_ref, ob_ref, s_sc, *, chunk):
    h = pl.program_id(1)

    @pl.when(pl.program_id(2) == 0)
    def _():
        s_sc[...] = jnp.zeros(s_sc.shape, F32)

    ii = lax.broadcasted_iota(jnp.int32, (chunk, chunk), 0)
    jj = lax.broadcasted_iota(jnp.int32, (chunk, chunk), 1)
    r = lax.broadcasted_iota(jnp.int32, (chunk, 1), 0).astype(F32)
    for d, (q_ref, k_ref, v_ref, o_ref) in enumerate(((qf, kf, vf, of_ref), (qb, kb, vb, ob_ref))):
        la = -jnp.exp(jnp.full((1, 1), dec_ref[d, h], F32))
        if d == 0:
            mask, dist = ii >= jj, (ii - jj).astype(F32)
            qdec, kdec = jnp.exp(la * (r + 1.0)), jnp.exp(la * (chunk - 1.0 - r))
        else:
            mask, dist = jj > ii, (jj - ii).astype(F32)
            qdec, kdec = jnp.exp(la * (chunk - r)), jnp.exp(la * r)
        decay = jnp.where(mask, jnp.exp(jnp.where(mask, dist * la, 0.0)), 0.0)
        q, k, v = q_ref[...], k_ref[...], v_ref[...]
        s = lax.dot_general(q, k, (((1,), (1,)), ((), ())), preferred_element_type=F32)
        o = jnp.dot((s * decay).astype(BF16), v, preferred_element_type=F32)
        state = s_sc[d]
        o = o + qdec * jnp.dot(q, state.astype(BF16), preferred_element_type=F32)
        kd = (k.astype(F32) * kdec).astype(BF16)
        s_sc[d] = jnp.exp(la * float(chunk)) * state + lax.dot_general(
            kd, v, (((0,), (0,)), ((), ())), preferred_element_type=F32)
        o_ref[...] = o


def _retention(q, k, v, decays, *, batch, seq, chunk=256):
    chunk = min(chunk, seq)
    n = seq // chunk
    fwd = pl.BlockSpec((None, None, chunk, LANE), lambda b, h, c: (b, h, c, 0))
    bwd = pl.BlockSpec((None, None, chunk, LANE), lambda b, h, c: (b, h, n - 1 - c, 0))
    out = jax.ShapeDtypeStruct((batch * seq, HEADS * LANE), F32)
    return pl.pallas_call(
        functools.partial(_ret_kernel, chunk=chunk),
        grid=(batch, HEADS, n),
        in_specs=[pl.BlockSpec(memory_space=pltpu.SMEM), fwd, fwd, fwd, bwd, bwd, bwd],
        out_specs=[pl.BlockSpec((chunk, LANE), lambda b, h, c: (b * n + c, h)),
                   pl.BlockSpec((chunk, LANE), lambda b, h, c: (b * n + n - 1 - c, h))],
        out_shape=[out, out],
        scratch_shapes=[pltpu.VMEM((2, LANE, LANE), F32)],
        compiler_params=_cparams(("parallel", "parallel", "arbitrary")),
        name="retention",
    )(decays, q, k, v, q, k, v)


GLA_SUB = 16


def _split3(x):
    x1 = x.astype(BF16)
    r1 = x - x1.astype(F32)
    x2 = r1.astype(BF16)
    x3 = (r1 - x2.astype(F32)).astype(BF16)
    return x1, x2, x3


def _gla_direction(q, k, v, lr, w2, bias, st, reverse):
    C, wk = q.shape
    wv = v.shape[1]
    dk, dv = wk // HEADS, wv // HEADS
    z = jnp.dot(lr.astype(BF16), w2, preferred_element_type=F32) + bias
    g = (jnp.minimum(z, 0.0) - jnp.log1p(jnp.exp(-jnp.abs(z)))) * (1.0 / GLA_TAU)
    ii = lax.broadcasted_iota(jnp.int32, (C, C), 0)
    jj = lax.broadcasted_iota(jnp.int32, (C, C), 1)
    tri = jnp.where(ii >= jj, 1.0, 0.0).astype(BF16)
    b = sum(jnp.dot(tri, part, preferred_element_type=F32) for part in _split3(g))
    tot = b[C - 1:C, :]
    c = (tot - b + g) if reverse else b

    qe = (q * jnp.exp(jnp.minimum(c, 0.0))).astype(BF16)
    o = lax.dot_general(qe, st.astype(BF16), (((1,), (1,)), ((), ())), preferred_element_type=F32)
    ke = (k * jnp.exp(jnp.minimum(tot - c, 0.0))).astype(BF16)
    upd = lax.dot_general(v.astype(BF16), ke, (((0,), (0,)), ((), ())), preferred_element_type=F32)
    rr = _div_pow2(lax.broadcasted_iota(jnp.int32, (wv, wk), 0), dv)
    cc = _div_pow2(lax.broadcasted_iota(jnp.int32, (wv, wk), 1), dk)
    new_st = jnp.where(rr == cc, st * jnp.exp(tot) + upd, 0.0)

    lane_head = _div_pow2(lax.broadcasted_iota(jnp.int32, (C, wk), 1), dk)
    scores = [jnp.zeros((C, C), F32) for _ in range(HEADS)]
    hsz = C // 2
    while hsz >= GLA_SUB:
        blk = 2 * hsz
        rows = []
        for m in range(C // blk):
            rrow = m * blk + (hsz if reverse else hsz - 1)
            rows.append(jnp.broadcast_to(c[rrow:rrow + 1, :], (blk, wk)))
        ref = jnp.concatenate(rows, axis=0) if len(rows) > 1 else rows[0]
        qt = q * jnp.exp(jnp.minimum(c - ref, 0.0))
        kt = (k * jnp.exp(jnp.minimum(ref - c, 0.0))).astype(BF16)
        same = _div_pow2(ii, blk) == _div_pow2(jj, blk)
        if reverse:
            lvl = same & (_mod_pow2(ii, blk) < hsz) & (_mod_pow2(jj, blk) >= hsz)
        else:
            lvl = same & (_mod_pow2(ii, blk) >= hsz) & (_mod_pow2(jj, blk) < hsz)
        for h in range(HEADS):
            qh = jnp.where(lane_head == h, qt, 0.0).astype(BF16)
            s = lax.dot_general(qh, kt, (((1,), (1,)), ((), ())), preferred_element_type=F32)
            scores[h] = scores[h] + jnp.where(lvl, s, 0.0)
        hsz //= 2
    vb = v.astype(BF16)
    o = o + jnp.concatenate(
        [jnp.dot(scores[h].astype(BF16), vb[:, h * dv:(h + 1) * dv], preferred_element_type=F32) for h in range(HEADS)],
        axis=1)

    er = _div_pow2(lax.broadcasted_iota(jnp.int32, (wk, wv), 0), dk)
    ec = _div_pow2(lax.broadcasted_iota(jnp.int32, (wk, wv), 1), dv)
    expand = jnp.where(er == ec, 1.0, 0.0).astype(BF16)
    pos = _mod_pow2(lax.broadcasted_iota(jnp.int32, (C, 1), 0), GLA_SUB)
    for lag in range(1 if reverse else 0, GLA_SUB):
        if lag == 0:
            t, vs, valid = q * k, v, None
        else:
            shift = (C - lag) if reverse else lag
            ks, cs, vs = pltpu.roll(k, shift, axis=0), pltpu.roll(c, shift, axis=0), pltpu.roll(v, shift, axis=0)
            t = q * ks * jnp.exp(jnp.minimum(c - cs, 0.0))
            valid = (pos + lag < GLA_SUB) if reverse else (pos >= lag)
        red = jnp.dot(t.astype(BF16), expand, preferred_element_type=F32)
        if valid is not None:
            red = jnp.where(valid, red, 0.0)
        o = o + red * vs
    return o, new_st


def _gla_kernel(qf, kf, vf, lf, qb, kb, vb, lb, w2f, bf, w2b, bb, of_ref, ob_ref, s_sc, *, qscale):
    @pl.when(pl.program_id(1) == 0)
    def _():
        s_sc[...] = jnp.zeros(s_sc.shape, F32)

    o, st = _gla_direction(qf[...] * qscale, kf[...], vf[...], lf[...], w2f[...], bf[...], s_sc[0], False)
    of_ref[...] = o
    s_sc[0] = st
    o, st = _gla_direction(qb[...] * qscale, kb[...], vb[...], lb[...], w2b[...], bb[...], s_sc[1], True)
    ob_ref[...] = o
    s_sc[1] = st


def _gla(z, *, qcol, kcol, vcol, lcol, w2f, bf, w2b, bb, batch, seq, chunk=128):
    chunk = min(chunk, seq)
    n = seq // chunk
    wk, wv = HEADS * GLA_K_DIM, HEADS * LANE

    def specs(cmap):
        return [pl.BlockSpec((chunk, wk), lambda b, c: (cmap(b, c), qcol)),
                pl.BlockSpec((chunk, wk), lambda b, c: (cmap(b, c), kcol)),
                pl.BlockSpec((chunk, wv), lambda b, c: (cmap(b, c), vcol)),
                pl.BlockSpec((chunk, LANE), lambda b, c: (cmap(b, c), lcol))]

    fmap = lambda b, c: b * n + c
    bmap = lambda b, c: b * n + n - 1 - c
    wspec = [pl.BlockSpec((LANE, wk), lambda b, c: (0, 0)), pl.BlockSpec((1, wk), lambda b, c: (0, 0))]
    out = jax.ShapeDtypeStruct((batch * seq, wv), F32)
    return pl.pallas_call(
        functools.partial(_gla_kernel, qscale=GLA_K_DIM ** -0.5),
        grid=(batch, n),
        in_specs=specs(fmap) + specs(bmap) + wspec + wspec,
        out_specs=[pl.BlockSpec((chunk, wv), lambda b, c: (fmap(b, c), 0)),
                   pl.BlockSpec((chunk, wv), lambda b, c: (bmap(b, c), 0))],
        out_shape=[out, out],
        scratch_shapes=[pltpu.VMEM((2, wv, wk), F32)],
        compiler_params=_cparams(("parallel", "arbitrary")),
        name="gla",
    )(z, z, z, z, z, z, z, z, w2f, bf, w2b, bb)


def _layer_norm_rows(r, g, b):
    mu = jnp.mean(r, axis=-1, keepdims=True)
    d = r - mu
    var = jnp.mean(d * d, axis=-1, keepdims=True)
    return d * lax.rsqrt(var + LN_EPS) * g + b


def _outproj_kernel(x_ref, fin_ref, of_ref, ob_ref, gate_ref, ng_ref, wa_ref, wb_ref, lg_ref, lb_ref, o_ref, *, group_norm):
    lin = of_ref[...] + ob_ref[...]
    parts = []
    for h in range(HEADS):
        zh = lin[:, h * LANE:(h + 1) * LANE]
        if group_norm:
            mu = jnp.mean(zh, axis=-1, keepdims=True)
            dz = zh - mu
            parts.append(dz * lax.rsqrt(jnp.mean(dz * dz, axis=-1, keepdims=True) + LN_EPS))
        else:
            parts.append(zh * lax.rsqrt(jnp.mean(zh * zh, axis=-1, keepdims=True) + RMS_EPS) * ng_ref[...])
    gate = gate_ref[...]
    lin = jnp.concatenate(parts, axis=1) * (gate * jax.nn.sigmoid(gate))
    y = (jnp.dot(fin_ref[...].astype(BF16), wa_ref[...], preferred_element_type=F32)
         + jnp.dot(lin.astype(BF16), wb_ref[...], preferred_element_type=F32))
    o_ref[...] = _layer_norm_rows(ALPHA * x_ref[...] + y, lg_ref[...], lb_ref[...])


def _outproj(x, fin, of, ob, gate_src, gate_col, norm_gain, wa, wb, ln_g, ln_b, *, group_norm, tm=512):
    n, d = x.shape
    w = HEADS * LANE
    tm = min(tm, n)
    row = lambda i: (i, 0)
    const = lambda i: (0, 0)
    return pl.pallas_call(
        functools.partial(_outproj_kernel, group_norm=group_norm),
        grid=(n // tm,),
        in_specs=[pl.BlockSpec((tm, d), row), pl.BlockSpec((tm, w), row), pl.BlockSpec((tm, w), row),
                  pl.BlockSpec((tm, w), row), pl.BlockSpec((tm, w), lambda i: (i, gate_col)),
                  pl.BlockSpec((1, LANE), const), pl.BlockSpec((w, d), const), pl.BlockSpec((w, d), const),
                  pl.BlockSpec((1, d), const), pl.BlockSpec((1, d), const)],
        out_specs=pl.BlockSpec((tm, d), row),
        out_shape=jax.ShapeDtypeStruct((n, d), F32),
        compiler_params=_cparams(("parallel",)),
        name="outproj",
    )(x, fin, of, ob, gate_src, norm_gain.reshape(1, LANE), wa, wb, ln_g.reshape(1, d), ln_b.reshape(1, d))


def _route_kernel(x_ref, w_ref, b_ref, ids_ref, gw_ref, cnt_ref, run_sc):
    @pl.when(pl.program_id(0) == 0)
    def _():
        run_sc[...] = jnp.zeros(run_sc.shape, F32)

    tm = x_ref.shape[0]
    logits = jnp.dot(x_ref[...], w_ref[...], preferred_element_type=F32, precision=lax.Precision.HIGHEST) + b_ref[...]
    lane = lax.broadcasted_iota(jnp.int32, logits.shape, 1)
    neg = -jnp.inf
    gmask = (lane >= N_EXPERTS) & (lane < N_EXPERTS + N_GROUPS)
    gl = jnp.where(gmask, logits, neg)
    gmax = jnp.max(gl, axis=1, keepdims=True)
    lane_f = lane.astype(F32)
    first = lambda hit: jnp.min(jnp.where(hit, lane_f, float(LANE)), axis=1, keepdims=True).astype(jnp.int32)
    gidx = first(gl == gmax) - N_EXPERTS
    p_grp = 1.0 / jnp.sum(jnp.where(gmask, jnp.exp(gl - gmax), 0.0), axis=1, keepdims=True)
    el = jnp.where(_div_pow2(lane, EXPERTS_PER_GROUP) == gidx, logits, neg)
    l1 = jnp.max(el, axis=1, keepdims=True)
    e1 = first(el == l1)
    el2 = jnp.where(lane == e1, neg, el)
    l2 = jnp.max(el2, axis=1, keepdims=True)
    e2 = first(el2 == l2)
    t = jnp.exp(l2 - l1)
    w1 = p_grp / (1.0 + t)
    w2 = p_grp * t / (1.0 + t)

    onehot = jnp.where(lane == e1, 1.0, jnp.where(lane == e2, 1.0, 0.0))
    ri = lax.broadcasted_iota(jnp.int32, (tm, tm), 0)
    ci = lax.broadcasted_iota(jnp.int32, (tm, tm), 1)
    before = jnp.dot(jnp.where(ri > ci, 1.0, 0.0).astype(BF16), onehot.astype(BF16), preferred_element_type=F32)
    before = before + run_sc[...]
    r1 = jnp.sum(jnp.where(lane == e1, before, 0.0), axis=1, keepdims=True).astype(jnp.int32)
    r2 = jnp.sum(jnp.where(lane == e2, before, 0.0), axis=1, keepdims=True).astype(jnp.int32)
    run_sc[...] = run_sc[...] + jnp.sum(onehot, axis=0, keepdims=True)
    cnt_ref[...] = run_sc[...].astype(jnp.int32)
    ids_ref[...] = jnp.where(lane == 0, e1, jnp.where(lane == 1, e2, jnp.where(lane == 2, r1, jnp.where(lane == 3, r2, 0))))
    gw_ref[...] = jnp.where(lane == 0, w1, jnp.where(lane == 1, w2, 0.0))


def _route(x, w, b, *, tm=512):
    n, d = x.shape
    tm = min(tm, n)
    return pl.pallas_call(
        _route_kernel,
        grid=(n // tm,),
        in_specs=[pl.BlockSpec((tm, d), lambda i: (i, 0)), pl.BlockSpec((d, LANE), lambda i: (0, 0)),
                  pl.BlockSpec((1, LANE), lambda i: (0, 0))],
        out_specs=[pl.BlockSpec((tm, LANE), lambda i: (i, 0)), pl.BlockSpec((tm, LANE), lambda i: (i, 0)),
                   pl.BlockSpec((1, LANE), lambda i: (0, 0))],
        out_shape=[jax.ShapeDtypeStruct((n, LANE), jnp.int32), jax.ShapeDtypeStruct((n, LANE), F32),
                   jax.ShapeDtypeStruct((1, LANE), jnp.int32)],
        scratch_shapes=[pltpu.VMEM((1, LANE), F32)],
        compiler_params=_cparams(("arbitrary",)),
        name="route",
    )(x, w, b)


MOE_BM = 256


def _dispatch_kernel(pend_ref, padded_ref, dest_ref, x_ref, xs_hbm, idx_smem, zbuf, sem_i, sem_z, sem):
    i = pl.program_id(0)
    tm = x_ref.shape[0]
    bm = zbuf.shape[0]

    @pl.when(i == 0)
    def _():
        zbuf[...] = jnp.zeros(zbuf.shape, F32)

        def tail(e):
            start_row = pl.multiple_of(pend_ref[e] - bm, bm)
            return pltpu.make_async_copy(zbuf, xs_hbm.at[pl.ds(start_row, bm), :], sem_z)

        def start(e, carry):
            @pl.when(padded_ref[e] > 0)
            def _():
                tail(e).start()
            return carry

        def wait(e, carry):
            @pl.when(padded_ref[e] > 0)
            def _():
                tail(e).wait()
            return carry

        lax.fori_loop(0, N_EXPERTS, start, 0)
        lax.fori_loop(0, N_EXPERTS, wait, 0)

        def unused(b):
            return pltpu.make_async_copy(zbuf, xs_hbm.at[pl.ds(pl.multiple_of(b * bm, bm), bm), :], sem_z)

        first_unused = pend_ref[N_EXPERTS - 1] // bm
        n_blocks = xs_hbm.shape[0] // bm
        lax.fori_loop(first_unused, n_blocks, lambda b, c: (unused(b).start(), c)[1], 0)
        lax.fori_loop(first_unused, n_blocks, lambda b, c: (unused(b).wait(), c)[1], 0)

    cp = pltpu.make_async_copy(dest_ref.at[i], idx_smem, sem_i)
    cp.start()
    cp.wait()

    def scatter(r, carry):
        for k in range(2):
            dst = idx_smem[2 * r + k]
            pltpu.make_async_copy(x_ref.at[pl.ds(r, 1), :], xs_hbm.at[pl.ds(dst, 1), :], sem).start()
        return carry

    lax.fori_loop(0, tm, scatter, 0)
    for k in range(2):
        pltpu.make_async_copy(x_ref, xs_hbm.at[pl.ds(0, tm), :], sem).wait()


def _dispatch(x, dest, pend, padded, cap, *, tm=512):
    n, d = x.shape
    tm = min(tm, n)
    nt = n // tm
    grid_spec = pltpu.PrefetchScalarGridSpec(
        num_scalar_prefetch=2,
        grid=(nt,),
        in_specs=[pl.BlockSpec((nt, 2 * tm), lambda i, pe, pa: (0, 0)),
                  pl.BlockSpec((tm, d), lambda i, pe, pa: (i, 0))],
        out_specs=pl.BlockSpec(memory_space=pl.ANY),
        scratch_shapes=[pltpu.SMEM((2 * tm,), jnp.int32), pltpu.VMEM((MOE_BM, d), F32),
                        pltpu.SemaphoreType.DMA(()), pltpu.SemaphoreType.DMA(()), pltpu.SemaphoreType.DMA(())],
    )
    return pl.pallas_call(
        _dispatch_kernel,
        grid_spec=grid_spec,
        out_shape=jax.ShapeDtypeStruct((cap, d), F32),
        compiler_params=_cparams(("arbitrary",)),
        name="dispatch",
    )(pend, padded, dest.reshape(nt, 2 * tm), x)


def _experts_kernel(blk_e_ref, nused_ref, xs_ref, wg_ref, wu_ref, wd_ref, ys_ref, wgb, wub, wdb):
    i = pl.program_id(0)

    @pl.when(i < nused_ref[0])
    def _():
        prev = blk_e_ref[jnp.maximum(i - 1, 0)]

        @pl.when(jnp.logical_or(i == 0, blk_e_ref[i] != prev))
        def _():
            wgb[...] = wg_ref[...].astype(BF16)
            wub[...] = wu_ref[...].astype(BF16)
            wdb[...] = wd_ref[...].astype(BF16)

        xb = xs_ref[...].astype(BF16)
        hg = jnp.dot(xb, wgb[...], preferred_element_type=F32)
        hu = jnp.dot(xb, wub[...], preferred_element_type=F32)
        hid = (hg * jax.nn.sigmoid(hg) * hu).astype(BF16)
        ys_ref[...] = jnp.dot(hid, wdb[...], preferred_element_type=F32)

    @pl.when(i >= nused_ref[0])
    def _():
        ys_ref[...] = jnp.zeros(ys_ref.shape, F32)


def _experts(xs, blk_e, n_used, w_gate, w_up, w_down, layer):
    cap, d = xs.shape
    bm = MOE_BM
    de = w_gate.shape[3]
    row_in = lambda i, be, nu: (jnp.minimum(i, nu[0] - 1), 0)
    row = lambda i, be, nu: (i, 0)
    grid_spec = pltpu.PrefetchScalarGridSpec(
        num_scalar_prefetch=2,
        grid=(cap // bm,),
        in_specs=[pl.BlockSpec((bm, d), row_in),
                  pl.BlockSpec((None, None, d, de), lambda i, be, nu: (layer, be[i], 0, 0)),
                  pl.BlockSpec((None, None, d, de), lambda i, be, nu: (layer, be[i], 0, 0)),
                  pl.BlockSpec((None, None, de, d), lambda i, be, nu: (layer, be[i], 0, 0))],
        out_specs=pl.BlockSpec((bm, d), row),
        scratch_shapes=[pltpu.VMEM((d, de), BF16), pltpu.VMEM((d, de), BF16), pltpu.VMEM((de, d), BF16)],
    )
    return pl.pallas_call(
        _experts_kernel,
        grid_spec=grid_spec,
        out_shape=jax.ShapeDtypeStruct((cap, d), F32),
        compiler_params=_cparams(("arbitrary",)),
        name="experts",
    )(blk_e, n_used, xs, w_gate, w_up, w_down)


def _combine_kernel(dest_ref, x_ref, gw_ref, g_ref, b_ref, ys_hbm, o_ref, idx_smem, ybuf, sem_i, sem):
    i = pl.program_id(0)
    tm = x_ref.shape[0]
    cp = pltpu.make_async_copy(dest_ref.at[i], idx_smem, sem_i)
    cp.start()
    cp.wait()

    def gather(r, carry):
        for k in range(2):
            src = idx_smem[2 * r + k]
            pltpu.make_async_copy(ys_hbm.at[pl.ds(src, 1), :], ybuf.at[k, pl.ds(r, 1), :], sem).start()
        return carry

    lax.fori_loop(0, tm, gather, 0)
    for k in range(2):
        pltpu.make_async_copy(ys_hbm.at[pl.ds(0, tm), :], ybuf.at[k], sem).wait()
    gw = gw_ref[...]
    ffn = ybuf[0] * gw[:, 0:1] + ybuf[1] * gw[:, 1:2]
    o_ref[...] = _layer_norm_rows(ALPHA * x_ref[...] + ffn, g_ref[...], b_ref[...])


def _combine(x, ys, dest, gw, ln_g, ln_b, *, tm=512):
    n, d = x.shape
    tm = min(tm, n)
    nt = n // tm
    return pl.pallas_call(
        _combine_kernel,
        grid=(nt,),
        in_specs=[pl.BlockSpec((nt, 2 * tm), lambda i: (0, 0)),
                  pl.BlockSpec((tm, d), lambda i: (i, 0)), pl.BlockSpec((tm, LANE), lambda i: (i, 0)),
                  pl.BlockSpec((1, d), lambda i: (0, 0)), pl.BlockSpec((1, d), lambda i: (0, 0)),
                  pl.BlockSpec(memory_space=pl.ANY)],
        out_specs=pl.BlockSpec((tm, d), lambda i: (i, 0)),
        out_shape=jax.ShapeDtypeStruct((n, d), F32),
        scratch_shapes=[pltpu.SMEM((2 * tm,), jnp.int32), pltpu.VMEM((2, tm, d), F32),
                        pltpu.SemaphoreType.DMA(()), pltpu.SemaphoreType.DMA(())],
        compiler_params=_cparams(("arbitrary",)),
        name="combine",
    )(dest.reshape(nt, 2 * tm), x, gw, ln_g.reshape(1, d), ln_b.reshape(1, d), ys)


def _moe(x, w_grp, b_grp, w_exp, b_exp, w_gate, w_up, w_down, layer, ln_g, ln_b):
    n, d = x.shape
    wr = jnp.zeros((d, LANE), F32).at[:, :N_EXPERTS].set(w_exp).at[:, N_EXPERTS:N_EXPERTS + N_GROUPS].set(w_grp)
    br = jnp.zeros((1, LANE), F32).at[0, :N_EXPERTS].set(b_exp).at[0, N_EXPERTS:N_EXPERTS + N_GROUPS].set(b_grp)
    ids, gw, cnt = _route(x, wr, br)
    bm = MOE_BM
    counts = cnt[0, :N_EXPERTS]
    padded = (counts + bm - 1) // bm * bm
    pend = jnp.cumsum(padded)
    pstart = pend - padded
    e, r = ids[:, 0:2], ids[:, 2:4]
    onehot = e[:, :, None] == jnp.arange(N_EXPERTS, dtype=jnp.int32)[None, None, :]
    dest = jnp.sum(jnp.where(onehot, pstart[None, None, :], 0), axis=-1) + r
    cap = 2 * n + N_EXPERTS * bm
    nb = cap // bm
    blk_start = jnp.arange(nb, dtype=jnp.int32) * bm
    blk_e = jnp.minimum(jnp.sum((pend[None, :] <= blk_start[:, None]).astype(jnp.int32), axis=1), N_EXPERTS - 1)
    n_used = (pend[-1:] // bm).astype(jnp.int32)
    xs = _dispatch(x, dest, pend.astype(jnp.int32), padded.astype(jnp.int32), cap)
    ys = _experts(xs, blk_e, n_used, w_gate, w_up, w_down, layer)
    return _combine(x, ys, dest, gw, ln_g, ln_b)


def _even_layer(x, batch, seq, layer_idx, w_in, dec_f, dec_b, lq1, lk1, lq2, lk2, subln, w_out, ln_g, ln_b):
    d = x.shape[1]
    w = HEADS * LANE
    z = _proj(x, w_in.astype(BF16), k=d)
    kw = dict(batch=batch, seq=seq)
    ret_seg = [(0, LANE)]
    q = _prep(z, 0, tables=_rope_tables(seq, ret_seg, RET_THETA, 1.0), sh=LANE // 2, **kw)
    k = _prep(z, 4, tables=_rope_tables(seq, ret_seg, RET_THETA, LANE ** -0.5), sh=LANE // 2, **kw)
    v = _prep(z, 8, **kw)
    decays = jnp.stack([dec_f, dec_b]).astype(F32)
    of, ob = _retention(q, k, v, decays, **kw)
    diff_seg = [(0, DIFF_ROT_DIM), (DIFF_HEAD_DIM, DIFF_ROT_DIM)]
    dq = _prep(z, 16, tables=_rope_tables(seq, diff_seg, ROPE_THETA, DIFF_HEAD_DIM ** -0.5 * LOG2E), sh=DIFF_ROT_DIM // 2, **kw)
    dk = _prep(z, 20, tables=_rope_tables(seq, diff_seg, ROPE_THETA, 1.0), sh=DIFF_ROT_DIM // 2, **kw)
    dvt = _prep(z, 24, transpose=True, tm=FLASH_TK_DIFF, **kw)
    lam_init = 0.8 - 0.6 * math.exp(-0.3 * layer_idx)
    diff = _flash(dq, dk, dvt, diff=(lq1, lk1, lq2, lk2, subln), lam_init=lam_init, **kw)
    wo = w_out.astype(BF16)
    return _outproj(x, diff, of, ob, z, 3, jnp.ones((LANE,), F32), wo[w:], wo[:w], ln_g, ln_b, group_norm=True)


def _odd_layer(x, batch, seq, w_in, q_norm, w_uq, kv_norm, w_ukv, w2_f, b_f, w2_b, b_b, gla_norm, w_out, ln_g, ln_b):
    d = x.shape[1]
    w = HEADS * LANE
    o = np.cumsum([0, MLA_Q_RANK, MLA_KV_RANK, MLA_ROPE, HEADS * GLA_K_DIM, HEADS * GLA_K_DIM, w, w,
                   GLA_GATE_RANK, GLA_GATE_RANK]).tolist()
    zeros = lambda c: jnp.zeros((d, c), F32)
    w_in2 = jnp.concatenate([
        w_in[:, o[0]:o[2]], zeros(MLA_NOPE), w_in[:, o[2]:o[3]], zeros(LANE - MLA_NOPE - MLA_ROPE),
        w_in[:, o[3]:o[7]], w_in[:, o[7]:o[9]], zeros(LANE - 2 * GLA_GATE_RANK)], axis=1).astype(BF16)
    z = _proj(x, w_in2, k=d, tm=512, tn=w_in2.shape[1])
    kw = dict(batch=batch, seq=seq)
    qd = MLA_NOPE + MLA_ROPE
    w_uq2 = jnp.pad(w_uq.reshape(MLA_Q_RANK, HEADS, qd), ((0, 0), (0, 0), (0, LANE - qd))).reshape(MLA_Q_RANK, w)
    ukv = w_ukv.reshape(MLA_KV_RANK, HEADS, MLA_NOPE + MLA_V)
    w_uk2 = jnp.pad(ukv[:, :, :MLA_NOPE], ((0, 0), (0, 0), (0, LANE - MLA_NOPE))).reshape(MLA_KV_RANK, w)
    w_uv2 = ukv[:, :, MLA_NOPE:].reshape(MLA_KV_RANK, w)
    qh = _proj(z, w_uq2.astype(BF16), k=MLA_Q_RANK, xcol=0, gain=q_norm)
    kvh = _proj(z, jnp.concatenate([w_uk2, w_uv2], axis=1).astype(BF16), k=MLA_KV_RANK, xcol=2, gain=kv_norm)
    rope_seg = [(MLA_NOPE, MLA_ROPE)]
    q = _prep(qh, 0, tables=_rope_tables(seq, rope_seg, ROPE_THETA, qd ** -0.5 * LOG2E), sh=MLA_ROPE // 2, **kw)
    k = _prep(z, 3, per_head=False, tables=_rope_tables(seq, rope_seg, ROPE_THETA, 1.0), sh=MLA_ROPE // 2,
              add=kvh, add_col0=0, **kw)
    vt = _prep(kvh, HEADS, transpose=True, tm=FLASH_TK, **kw)
    mla = _flash(q, k, vt, **kw)
    wk = HEADS * GLA_K_DIM
    pad_rows = lambda m, r0: jnp.zeros((LANE, wk), F32).at[r0:r0 + GLA_GATE_RANK].set(m).astype(BF16)
    of, ob = _gla(z, qcol=2, kcol=3, vcol=2, lcol=16,
                  w2f=pad_rows(w2_f, 0), bf=b_f.reshape(1, wk), w2b=pad_rows(w2_b, GLA_GATE_RANK), bb=b_b.reshape(1, wk), **kw)
    wo = w_out.astype(BF16)
    return _outproj(x, mla, of, ob, z, 3, gla_norm, wo[:w], wo[w:], ln_g, ln_b, group_norm=False)


def kernel(x, ev_w_in, ev_ret_decay_f, ev_ret_decay_b, ev_lq1, ev_lk1, ev_lq2, ev_lk2, ev_subln, ev_w_out, od_w_in, od_q_norm, od_w_uq, od_kv_norm, od_w_ukv, od_gla_w2_f, od_gla_b_f, od_gla_w2_b, od_gla_b_b, od_gla_norm, od_w_out, ln1_g, ln1_b, ln2_g, ln2_b, moe_w_grp, moe_b_grp, moe_w_exp, moe_b_exp, moe_w_gate, moe_w_up, moe_w_down):
    batch, seq, d = x.shape
    h = x.reshape(batch * seq, d)
    for i in range(DEPTH):
        j = i // 2
        if i % 2 == 0:
            h = _even_layer(h, batch, seq, i, ev_w_in[j], ev_ret_decay_f[j], ev_ret_decay_b[j], ev_lq1[j], ev_lk1[j],
                            ev_lq2[j], ev_lk2[j], ev_subln[j], ev_w_out[j], ln1_g[i], ln1_b[i])
        else:
            h = _odd_layer(h, batch, seq, od_w_in[j], od_q_norm[j], od_w_uq[j], od_kv_norm[j], od_w_ukv[j],
                           od_gla_w2_f[j], od_gla_b_f[j], od_gla_w2_b[j], od_gla_b_b[j], od_gla_norm[j], od_w_out[j],
                           ln1_g[i], ln1_b[i])
        h = _moe(h, moe_w_grp[i], moe_b_grp[i], moe_w_exp[i], moe_b_exp[i], moe_w_gate, moe_w_up, moe_w_down, i,
                 ln2_g[i], ln2_b[i])
    return h.reshape(batch, seq, d)
```

```python
import functools
import math

import numpy as np
import jax
import jax.numpy as jnp
from jax import lax
from jax.experimental import pallas as pl
from jax.experimental.pallas import tpu as pltpu

F32 = jnp.float32
BF16 = jnp.bfloat16

HEADS = 4
LANE = 128
RET_THETA = 10000.0
ROPE_THETA = 500000.0
DIFF_HEAD_DIM = 64
DIFF_ROT_DIM = 16
MLA_Q_RANK = 256
MLA_KV_RANK = 128
MLA_NOPE = 64
MLA_ROPE = 32
MLA_V = 128
GLA_K_DIM = 64
GLA_GATE_RANK = 16
GLA_TAU = 16.0
N_GROUPS = 4
EXPERTS_PER_GROUP = 8
N_EXPERTS = N_GROUPS * EXPERTS_PER_GROUP
DEPTH = 2
ALPHA = (2.0 * DEPTH) ** 0.25
LN_EPS = 1e-5
RMS_EPS = 1e-6

VMEM_LIMIT = 48 * 1024 * 1024


def _div_pow2(x, n):
    return lax.shift_right_logical(x, int(n).bit_length() - 1)


def _mod_pow2(x, n):
    return lax.bitwise_and(x, int(n) - 1)


def _cparams(sem):
    return pltpu.CompilerParams(dimension_semantics=sem, vmem_limit_bytes=VMEM_LIMIT)


def _proj_kernel(*refs, rms):
    if rms:
        x_ref, g_ref, w_ref, o_ref, xb_ref = refs
    else:
        x_ref, w_ref, o_ref, xb_ref = refs

    @pl.when(pl.program_id(1) == 0)
    def _():
        x = x_ref[...]
        if rms:
            x = x * lax.rsqrt(jnp.mean(x * x, axis=-1, keepdims=True) + RMS_EPS) * g_ref[...]
        xb_ref[...] = x.astype(BF16)

    o_ref[...] = jnp.dot(xb_ref[...], w_ref[...], preferred_element_type=F32).astype(o_ref.dtype)


def _proj(x, w, *, k, xcol=0, tm=1024, tn=512, gain=None):
    n = x.shape[0]
    m = w.shape[1]
    tm = min(tm, n)
    tn = min(tn, m)
    in_specs = [pl.BlockSpec((tm, k), lambda i, j: (i, xcol))]
    args = [x]
    if gain is not None:
        in_specs.append(pl.BlockSpec((1, k), lambda i, j: (0, 0)))
        args.append(gain.reshape(1, k))
    in_specs.append(pl.BlockSpec((k, tn), lambda i, j: (0, j)))
    args.append(w)
    return pl.pallas_call(
        functools.partial(_proj_kernel, rms=gain is not None),
        grid=(n // tm, m // tn),
        in_specs=in_specs,
        out_specs=pl.BlockSpec((tm, tn), lambda i, j: (i, j)),
        out_shape=jax.ShapeDtypeStruct((n, m), F32),
        scratch_shapes=[pltpu.VMEM((tm, k), BF16)],
        compiler_params=_cparams(("parallel", "arbitrary")),
        name="proj",
    )(*args)


def _prep_kernel(*refs, has_tab, has_add, sh, transpose):
    refs = list(refs)
    z_ref = refs.pop(0)
    add_ref = refs.pop(0) if has_add else None
    if has_tab:
        c_ref, sa_ref, sb_ref = refs.pop(0), refs.pop(0), refs.pop(0)
    o_ref = refs.pop(0)
    for h in range(HEADS):
        z = z_ref[:, h * LANE:(h + 1) * LANE] if z_ref.shape[1] > LANE else z_ref[...]
        if has_tab:
            out = (z * c_ref[...] + pltpu.roll(z, sh, axis=1) * sa_ref[...]
                   + pltpu.roll(z, LANE - sh, axis=1) * sb_ref[...])
        else:
            out = z
        if has_add:
            out = out + add_ref[:, h * LANE:(h + 1) * LANE]
        if transpose:
            o_ref[h, :LANE, :] = out.T.astype(o_ref.dtype)
            o_ref[h, LANE:, :] = jnp.ones((o_ref.shape[1] - LANE, o_ref.shape[2]), o_ref.dtype)
        else:
            o_ref[h] = out.astype(o_ref.dtype)


def _prep(z, col0, *, batch, seq, per_head=True, tables=None, sh=0, add=None, add_col0=0, transpose=False, tm=1024):
    tm = min(tm, seq // 2) if transpose else min(tm, seq)
    nt = seq // tm
    w = HEADS * LANE
    if per_head:
        in_specs = [pl.BlockSpec((tm, w), lambda b, i: (b * nt + i, col0 // HEADS))]
    else:
        in_specs = [pl.BlockSpec((tm, LANE), lambda b, i: (b * nt + i, col0))]
    args = [z]
    if add is not None:
        in_specs.append(pl.BlockSpec((tm, w), lambda b, i: (b * nt + i, add_col0 // HEADS)))
        args.append(add)
    if tables is not None:
        for t in tables:
            in_specs.append(pl.BlockSpec((tm, LANE), lambda b, i: (i, 0)))
            args.append(t)
    if transpose:
        out_shape = jax.ShapeDtypeStruct((batch, HEADS, nt, LANE + ONES_ROWS, tm), BF16)
        out_spec = pl.BlockSpec((None, HEADS, None, LANE + ONES_ROWS, tm), lambda b, i: (b, 0, i, 0, 0))
    else:
        out_shape = jax.ShapeDtypeStruct((batch, HEADS, seq, LANE), BF16)
        out_spec = pl.BlockSpec((None, HEADS, tm, LANE), lambda b, i: (b, 0, i, 0))
    return pl.pallas_call(
        functools.partial(_prep_kernel, has_tab=tables is not None, has_add=add is not None, sh=sh, transpose=transpose),
        grid=(batch, nt),
        in_specs=in_specs,
        out_specs=out_spec,
        out_shape=out_shape,
        compiler_params=_cparams(("parallel", "parallel")),
        name="prep",
    )(*args)


def _rope_tables(seq, segs, theta, scale):
    pos = jnp.arange(seq, dtype=F32)
    inv = jnp.zeros((LANE,), F32)
    lo = np.zeros((LANE,), bool)
    hi = np.zeros((LANE,), bool)
    for start, rot in segs:
        half = rot // 2
        f = jnp.power(jnp.float32(theta), -jnp.arange(0, rot, 2, dtype=F32) / rot)
        inv = inv.at[start:start + half].set(f).at[start + half:start + rot].set(f)
        lo[start:start + half] = True
        hi[start + half:start + rot] = True
    ang = pos[:, None] * inv[None, :]
    cos, sin = jnp.cos(ang), jnp.sin(ang)
    c = jnp.where(lo | hi, cos, 1.0) * scale
    sa = jnp.where(hi, sin, 0.0) * scale
    sb = jnp.where(lo, -sin, 0.0) * scale
    return c, sa, sb


ONES_ROWS = 16
LOG2E = math.log2(math.e)
FLASH_TK = 1024
FLASH_TK_DIFF = 512


def _flash_kernel(*refs, ncomp, nk, lam_init):
    if ncomp == 2:
        q_ref, k_ref, vt_ref, lq1, lk1, lq2, lk2, g_ref, o_ref, *scr = refs
    else:
        q_ref, k_ref, vt_ref, o_ref, *scr = refs
    qm_sc, m_sc, acc_sc, s0, s1, cm0, cm1, p0, p1, al0, al1 = scr
    tk = s0.shape[1]
    q = q_ref[...]
    if ncomp == 2:
        lane = lax.broadcasted_iota(jnp.int32, q.shape, 1)
        zero = jnp.zeros_like(q)
        qm_sc[0] = jnp.where(lane < DIFF_HEAD_DIM, q, zero)
        qm_sc[1] = jnp.where(lane >= DIFF_HEAD_DIM, q, zero)
    else:
        qm_sc[0] = q
    m_sc[...] = jnp.full(m_sc.shape, -jnp.inf, F32)
    acc_sc[...] = jnp.zeros(acc_sc.shape, F32)
    p1[...] = jnp.zeros(p1.shape, BF16)
    al1[...] = jnp.ones(al1.shape, F32)

    def scores(j, s_ref, cm_ref):
        off = pl.multiple_of(j * tk, tk)
        k = k_ref[pl.ds(off, tk), :]
        for c in range(ncomp):
            s = lax.dot_general(k, qm_sc[c], (((1,), (1,)), ((), ())), preferred_element_type=F32)
            s_ref[c] = s
            cm_ref[c] = jnp.max(s, axis=0, keepdims=True)

    def softmax(s_ref, cm_ref, p_ref, al_ref):
        for c in range(ncomp):
            m_old = m_sc[c]
            m_new = jnp.maximum(m_old, cm_ref[c])
            al_ref[c] = jnp.exp2(m_old - m_new)
            p_ref[c] = jnp.exp2(s_ref[c] - m_new).astype(BF16)
            m_sc[c] = m_new

    def values(j, p_ref, al_ref):
        vt = vt_ref[j]
        for c in range(ncomp):
            acc_sc[c] = al_ref[c] * acc_sc[c] + jnp.dot(vt, p_ref[c], preferred_element_type=F32)

    scores(0, s0, cm0)

    def body(jj, carry):
        a = 2 * jj
        scores(a + 1, s1, cm1)
        softmax(s0, cm0, p0, al0)
        values(jnp.maximum(a - 1, 0), p1, al1)
        scores(jnp.minimum(a + 2, nk - 1), s0, cm0)
        softmax(s1, cm1, p1, al1)
        values(a, p0, al0)
        return carry

    lax.fori_loop(0, nk // 2, body, 0)
    values(nk - 1, p1, al1)

    def normalised(c):
        acc = acc_sc[c]
        return acc[:LANE] / acc[LANE:LANE + 1]

    o = normalised(0)
    if ncomp == 2:
        lam = (jnp.exp(jnp.sum(lq1[...] * lk1[...], keepdims=True))
               - jnp.exp(jnp.sum(lq2[...] * lk2[...], keepdims=True)) + lam_init)
        o = o - lam * normalised(1)
        o = o * lax.rsqrt(jnp.mean(o * o, axis=0, keepdims=True) + RMS_EPS) * g_ref[...] * (1.0 - lam_init)
    o_ref[...] = o.T


def _flash(q, k, vt, *, batch, seq, tq=512, diff=None, lam_init=0.0):
    nk, vrows, tk = vt.shape[2], vt.shape[3], vt.shape[4]
    assert nk % 2 == 0 and vrows == LANE + ONES_ROWS
    tq = min(tq, seq)
    nq = seq // tq
    ncomp = 2 if diff is not None else 1
    in_specs = [
        pl.BlockSpec((None, None, tq, LANE), lambda b, h, i: (b, h, i, 0)),
        pl.BlockSpec((None, None, seq, LANE), lambda b, h, i: (b, h, 0, 0)),
        pl.BlockSpec((None, None, nk, vrows, tk), lambda b, h, i: (b, h, 0, 0, 0)),
    ]
    args = [q, k, vt]
    if diff is not None:
        lq1, lk1, lq2, lk2, subln = diff
        for v in (lq1, lk1, lq2, lk2):
            in_specs.append(pl.BlockSpec((1, DIFF_HEAD_DIM), lambda b, h, i: (0, 0)))
            args.append(v.reshape(1, DIFF_HEAD_DIM))
        in_specs.append(pl.BlockSpec((LANE, 1), lambda b, h, i: (0, 0)))
        args.append(subln.reshape(LANE, 1))
    return pl.pallas_call(
        functools.partial(_flash_kernel, ncomp=ncomp, nk=nk, lam_init=lam_init),
        grid=(batch, HEADS, nq),
        in_specs=in_specs,
        out_specs=pl.BlockSpec((tq, LANE), lambda b, h, i: (b * nq + i, h)),
        out_shape=jax.ShapeDtypeStruct((batch * seq, HEADS * LANE), F32),
        scratch_shapes=[pltpu.VMEM((ncomp, tq, LANE), BF16),
                        pltpu.VMEM((ncomp, 1, tq), F32), pltpu.VMEM((ncomp, vrows, tq), F32),
                        pltpu.VMEM((ncomp, tk, tq), F32), pltpu.VMEM((ncomp, tk, tq), F32),
                        pltpu.VMEM((ncomp, 1, tq), F32), pltpu.VMEM((ncomp, 1, tq), F32),
                        pltpu.VMEM((ncomp, tk, tq), BF16), pltpu.VMEM((ncomp, tk, tq), BF16),
                        pltpu.VMEM((ncomp, 1, tq), F32), pltpu.VMEM((ncomp, 1, tq), F32)],
        compiler_params=_cparams(("parallel", "parallel", "parallel")),
        name="flash_diff" if diff is not None else "flash_mla",
    )(*args)


def _ret_kernel(dec_ref, qf, kf, vf, qb, kb, vb, of_ref, ob_ref, s_sc, *, chunk):
    h = pl.program_id(1)

    @pl.when(pl.program_id(2) == 0)
    def _():
        s_sc[...] = jnp.zeros(s_sc.shape, F32)

    ii = lax.broadcasted_iota(jnp.int32, (chunk, chunk), 0)
    jj = lax.broadcasted_iota(jnp.int32, (chunk, chunk), 1)
    r = lax.broadcasted_iota(jnp.int32, (chunk, 1), 0).astype(F32)
    for d, (q_ref, k_ref, v_ref, o_ref) in enumerate(((qf, kf, vf, of_ref), (qb, kb, vb, ob_ref))):
        la = -jnp.exp(jnp.full((1, 1), dec_ref[d, h], F32))
        if d == 0:
            mask, dist = ii >= jj, (ii - jj).astype(F32)
            qdec, kdec = jnp.exp(la * (r + 1.0)), jnp.exp(la * (chunk - 1.0 - r))
        else:
            mask, dist = jj > ii, (jj - ii).astype(F32)
            qdec, kdec = jnp.exp(la * (chunk - r)), jnp.exp(la * r)
        decay = jnp.where(mask, jnp.exp(jnp.where(mask, dist * la, 0.0)), 0.0)
        q, k, v = q_ref[...], k_ref[...], v_ref[...]
        s = lax.dot_general(q, k, (((1,), (1,)), ((), ())), preferred_element_type=F32)
        o = jnp.dot((s * decay).astype(BF16), v, preferred_element_type=F32)
        state = s_sc[d]
        o = o + qdec * jnp.dot(q, state.astype(BF16), preferred_element_type=F32)
        kd = (k.astype(F32) * kdec).astype(BF16)
        s_sc[d] = jnp.exp(la * float(chunk)) * state + lax.dot_general(
            kd, v, (((0,), (0,)), ((), ())), preferred_element_type=F32)
        o_ref[...] = o


def _retention(q, k, v, decays, *, batch, seq, chunk=256):
    chunk = min(chunk, seq)
    n = seq // chunk
    fwd = pl.BlockSpec((None, None, chunk, LANE), lambda b, h, c: (b, h, c, 0))
    bwd = pl.BlockSpec((None, None, chunk, LANE), lambda b, h, c: (b, h, n - 1 - c, 0))
    out = jax.ShapeDtypeStruct((batch * seq, HEADS * LANE), F32)
    return pl.pallas_call(
        functools.partial(_ret_kernel, chunk=chunk),
        grid=(batch, HEADS, n),
        in_specs=[pl.BlockSpec(memory_space=pltpu.SMEM), fwd, fwd, fwd, bwd, bwd, bwd],
        out_specs=[pl.BlockSpec((chunk, LANE), lambda b, h, c: (b * n + c, h)),
                   pl.BlockSpec((chunk, LANE), lambda b, h, c: (b * n + n - 1 - c, h))],
        out_shape=[out, out],
        scratch_shapes=[pltpu.VMEM((2, LANE, LANE), F32)],
        compiler_params=_cparams(("parallel", "parallel", "arbitrary")),
        name="retention",
    )(decays, q, k, v, q, k, v)


GLA_SUB = 16


def _split3(x):
    x1 = x.astype(BF16)
    r1 = x - x1.astype(F32)
    x2 = r1.astype(BF16)
    x3 = (r1 - x2.astype(F32)).astype(BF16)
    return x1, x2, x3


def _gla_direction(q, k, v, lr, w2, bias, st, reverse):
    C, wk = q.shape
    wv = v.shape[1]
    dk, dv = wk // HEADS, wv // HEADS
    z = jnp.dot(lr.astype(BF16), w2, preferred_element_type=F32) + bias
    g = (jnp.minimum(z, 0.0) - jnp.log1p(jnp.exp(-jnp.abs(z)))) * (1.0 / GLA_TAU)
    ii = lax.broadcasted_iota(jnp.int32, (C, C), 0)
    jj = lax.broadcasted_iota(jnp.int32, (C, C), 1)
    tri = jnp.where(ii >= jj, 1.0, 0.0).astype(BF16)
    b = sum(jnp.dot(tri, part, preferred_element_type=F32) for part in _split3(g))
    tot = b[C - 1:C, :]
    c = (tot - b + g) if reverse else b

    qe = (q * jnp.exp(jnp.minimum(c, 0.0))).astype(BF16)
    o = lax.dot_general(qe, st.astype(BF16), (((1,), (1,)), ((), ())), preferred_element_type=F32)
    ke = (k * jnp.exp(jnp.minimum(tot - c, 0.0))).astype(BF16)
    upd = lax.dot_general(v.astype(BF16), ke, (((0,), (0,)), ((), ())), preferred_element_type=F32)
    rr = _div_pow2(lax.broadcasted_iota(jnp.int32, (wv, wk), 0), dv)
    cc = _div_pow2(lax.broadcasted_iota(jnp.int32, (wv, wk), 1), dk)
    new_st = jnp.where(rr == cc, st * jnp.exp(tot) + upd, 0.0)

    lane_head = _div_pow2(lax.broadcasted_iota(jnp.int32, (C, wk), 1), dk)
    scores = [jnp.zeros((C, C), F32) for _ in range(HEADS)]
    hsz = C // 2
    while hsz >= GLA_SUB:
        blk = 2 * hsz
        rows = []
        for m in range(C // blk):
            rrow = m * blk + (hsz if reverse else hsz - 1)
            rows.append(jnp.broadcast_to(c[rrow:rrow + 1, :], (blk, wk)))
        ref = jnp.concatenate(rows, axis=0) if len(rows) > 1 else rows[0]
        qt = q * jnp.exp(jnp.minimum(c - ref, 0.0))
        kt = (k * jnp.exp(jnp.minimum(ref - c, 0.0))).astype(BF16)
        same = _div_pow2(ii, blk) == _div_pow2(jj, blk)
        if reverse:
            lvl = same & (_mod_pow2(ii, blk) < hsz) & (_mod_pow2(jj, blk) >= hsz)
        else:
            lvl = same & (_mod_pow2(ii, blk) >= hsz) & (_mod_pow2(jj, blk) < hsz)
        for h in range(HEADS):
            qh = jnp.where(lane_head == h, qt, 0.0).astype(BF16)
            s = lax.dot_general(qh, kt, (((1,), (1,)), ((), ())), preferred_element_type=F32)
            scores[h] = scores[h] + jnp.where(lvl, s, 0.0)
        hsz //= 2
    vb = v.astype(BF16)
    o = o + jnp.concatenate(
        [jnp.dot(scores[h].astype(BF16), vb[:, h * dv:(h + 1) * dv], preferred_element_type=F32) for h in range(HEADS)],
        axis=1)

    er = _div_pow2(lax.broadcasted_iota(jnp.int32, (wk, wv), 0), dk)
    ec = _div_pow2(lax.broadcasted_iota(jnp.int32, (wk, wv), 1), dv)
    expand = jnp.where(er == ec, 1.0, 0.0).astype(BF16)
    pos = _mod_pow2(lax.broadcasted_iota(jnp.int32, (C, 1), 0), GLA_SUB)
    for lag in range(1 if reverse else 0, GLA_SUB):
        if lag == 0:
            t, vs, valid = q * k, v, None
        else:
            shift = (C - lag) if reverse else lag
            ks, cs, vs = pltpu.roll(k, shift, axis=0), pltpu.roll(c, shift, axis=0), pltpu.roll(v, shift, axis=0)
            t = q * ks * jnp.exp(jnp.minimum(c - cs, 0.0))
            valid = (pos + lag < GLA_SUB) if reverse else (pos >= lag)
        red = jnp.dot(t.astype(BF16), expand, preferred_element_type=F32)
        if valid is not None:
            red = jnp.where(valid, red, 0.0)
        o = o + red * vs
    return o, new_st


def _gla_kernel(qf, kf, vf, lf, qb, kb, vb, lb, w2f, bf, w2b, bb, of_ref, ob_ref, s_sc, *, qscale):
    @pl.when(pl.program_id(1) == 0)
    def _():
        s_sc[...] = jnp.zeros(s_sc.shape, F32)

    o, st = _gla_direction(qf[...] * qscale, kf[...], vf[...], lf[...], w2f[...], bf[...], s_sc[0], False)
    of_ref[...] = o
    s_sc[0] = st
    o, st = _gla_direction(qb[...] * qscale, kb[...], vb[...], lb[...], w2b[...], bb[...], s_sc[1], True)
    ob_ref[...] = o
    s_sc[1] = st


def _gla(z, *, qcol, kcol, vcol, lcol, w2f, bf, w2b, bb, batch, seq, chunk=128):
    chunk = min(chunk, seq)
    n = seq // chunk
    wk, wv = HEADS * GLA_K_DIM, HEADS * LANE

    def specs(cmap):
        return [pl.BlockSpec((chunk, wk), lambda b, c: (cmap(b, c), qcol)),
                pl.BlockSpec((chunk, wk), lambda b, c: (cmap(b, c), kcol)),
                pl.BlockSpec((chunk, wv), lambda b, c: (cmap(b, c), vcol)),
                pl.BlockSpec((chunk, LANE), lambda b, c: (cmap(b, c), lcol))]

    fmap = lambda b, c: b * n + c
    bmap = lambda b, c: b * n + n - 1 - c
    wspec = [pl.BlockSpec((LANE, wk), lambda b, c: (0, 0)), pl.BlockSpec((1, wk), lambda b, c: (0, 0))]
    out = jax.ShapeDtypeStruct((batch * seq, wv), F32)
    return pl.pallas_call(
        functools.partial(_gla_kernel, qscale=GLA_K_DIM ** -0.5),
        grid=(batch, n),
        in_specs=specs(fmap) + specs(bmap) + wspec + wspec,
        out_specs=[pl.BlockSpec((chunk, wv), lambda b, c: (fmap(b, c), 0)),
                   pl.BlockSpec((chunk, wv), lambda b, c: (bmap(b, c), 0))],
        out_shape=[out, out],
        scratch_shapes=[pltpu.VMEM((2, wv, wk), F32)],
        compiler_params=_cparams(("parallel", "arbitrary")),
        name="gla",
    )(z, z, z, z, z, z, z, z, w2f, bf, w2b, bb)


def _layer_norm_rows(r, g, b):
    mu = jnp.mean(r, axis=-1, keepdims=True)
    d = r - mu
    var = jnp.mean(d * d, axis=-1, keepdims=True)
    return d * lax.rsqrt(var + LN_EPS) * g + b


def _outproj_kernel(x_ref, fin_ref, of_ref, ob_ref, gate_ref, ng_ref, wa_ref, wb_ref, lg_ref, lb_ref, o_ref, *, group_norm):
    lin = of_ref[...] + ob_ref[...]
    parts = []
    for h in range(HEADS):
        zh = lin[:, h * LANE:(h + 1) * LANE]
        if group_norm:
            mu = jnp.mean(zh, axis=-1, keepdims=True)
            dz = zh - mu
            parts.append(dz * lax.rsqrt(jnp.mean(dz * dz, axis=-1, keepdims=True) + LN_EPS))
        else:
            parts.append(zh * lax.rsqrt(jnp.mean(zh * zh, axis=-1, keepdims=True) + RMS_EPS) * ng_ref[...])
    gate = gate_ref[...]
    lin = jnp.concatenate(parts, axis=1) * (gate * jax.nn.sigmoid(gate))
    y = (jnp.dot(fin_ref[...].astype(BF16), wa_ref[...], preferred_element_type=F32)
         + jnp.dot(lin.astype(BF16), wb_ref[...], preferred_element_type=F32))
    o_ref[...] = _layer_norm_rows(ALPHA * x_ref[...] + y, lg_ref[...], lb_ref[...])


def _outproj(x, fin, of, ob, gate_src, gate_col, norm_gain, wa, wb, ln_g, ln_b, *, group_norm, tm=512):
    n, d = x.shape
    w = HEADS * LANE
    tm = min(tm, n)
    row = lambda i: (i, 0)
    const = lambda i: (0, 0)
    return pl.pallas_call(
        functools.partial(_outproj_kernel, group_norm=group_norm),
        grid=(n // tm,),
        in_specs=[pl.BlockSpec((tm, d), row), pl.BlockSpec((tm, w), row), pl.BlockSpec((tm, w), row),
                  pl.BlockSpec((tm, w), row), pl.BlockSpec((tm, w), lambda i: (i, gate_col)),
                  pl.BlockSpec((1, LANE), const), pl.BlockSpec((w, d), const), pl.BlockSpec((w, d), const),
                  pl.BlockSpec((1, d), const), pl.BlockSpec((1, d), const)],
        out_specs=pl.BlockSpec((tm, d), row),
        out_shape=jax.ShapeDtypeStruct((n, d), F32),
        compiler_params=_cparams(("parallel",)),
        name="outproj",
    )(x, fin, of, ob, gate_src, norm_gain.reshape(1, LANE), wa, wb, ln_g.reshape(1, d), ln_b.reshape(1, d))


def _route_kernel(x_ref, w_ref, b_ref, ids_ref, gw_ref, cnt_ref, run_sc):
    @pl.when(pl.program_id(0) == 0)
    def _():
        run_sc[...] = jnp.zeros(run_sc.shape, F32)

    tm = x_ref.shape[0]
    logits = jnp.dot(x_ref[...], w_ref[...], preferred_element_type=F32, precision=lax.Precision.HIGHEST) + b_ref[...]
    lane = lax.broadcasted_iota(jnp.int32, logits.shape, 1)
    neg = -jnp.inf
    gmask = (lane >= N_EXPERTS) & (lane < N_EXPERTS + N_GROUPS)
    gl = jnp.where(gmask, logits, neg)
    gmax = jnp.max(gl, axis=1, keepdims=True)
    lane_f = lane.astype(F32)
    first = lambda hit: jnp.min(jnp.where(hit, lane_f, float(LANE)), axis=1, keepdims=True).astype(jnp.int32)
    gidx = first(gl == gmax) - N_EXPERTS
    p_grp = 1.0 / jnp.sum(jnp.where(gmask, jnp.exp(gl - gmax), 0.0), axis=1, keepdims=True)
    el = jnp.where(_div_pow2(lane, EXPERTS_PER_GROUP) == gidx, logits, neg)
    l1 = jnp.max(el, axis=1, keepdims=True)
    e1 = first(el == l1)
    el2 = jnp.where(lane == e1, neg, el)
    l2 = jnp.max(el2, axis=1, keepdims=True)
    e2 = first(el2 == l2)
    t = jnp.exp(l2 - l1)
    w1 = p_grp / (1.0 + t)
    w2 = p_grp * t / (1.0 + t)

    onehot = jnp.where(lane == e1, 1.0, jnp.where(lane == e2, 1.0, 0.0))
    ri = lax.broadcasted_iota(jnp.int32, (tm, tm), 0)
    ci = lax.broadcasted_iota(jnp.int32, (tm, tm), 1)
    before = jnp.dot(jnp.where(ri > ci, 1.0, 0.0).astype(BF16), onehot.astype(BF16), preferred_element_type=F32)
    before = before + run_sc[...]
    r1 = jnp.sum(jnp.where(lane == e1, before, 0.0), axis=1, keepdims=True).astype(jnp.int32)
    r2 = jnp.sum(jnp.where(lane == e2, before, 0.0), axis=1, keepdims=True).astype(jnp.int32)
    run_sc[...] = run_sc[...] + jnp.sum(onehot, axis=0, keepdims=True)
    cnt_ref[...] = run_sc[...].astype(jnp.int32)
    ids_ref[...] = jnp.where(lane == 0, e1, jnp.where(lane == 1, e2, jnp.where(lane == 2, r1, jnp.where(lane == 3, r2, 0))))
    gw_ref[...] = jnp.where(lane == 0, w1, jnp.where(lane == 1, w2, 0.0))


def _route(x, w, b, *, tm=512):
    n, d = x.shape
    tm = min(tm, n)
    return pl.pallas_call(
        _route_kernel,
        grid=(n // tm,),
        in_specs=[pl.BlockSpec((tm, d), lambda i: (i, 0)), pl.BlockSpec((d, LANE), lambda i: (0, 0)),
                  pl.BlockSpec((1, LANE), lambda i: (0, 0))],
        out_specs=[pl.BlockSpec((tm, LANE), lambda i: (i, 0)), pl.BlockSpec((tm, LANE), lambda i: (i, 0)),
                   pl.BlockSpec((1, LANE), lambda i: (0, 0))],
        out_shape=[jax.ShapeDtypeStruct((n, LANE), jnp.int32), jax.ShapeDtypeStruct((n, LANE), F32),
                   jax.ShapeDtypeStruct((1, LANE), jnp.int32)],
        scratch_shapes=[pltpu.VMEM((1, LANE), F32)],
        compiler_params=_cparams(("arbitrary",)),
        name="route",
    )(x, w, b)


MOE_BM = 256


def _dispatch_kernel(pend_ref, padded_ref, dest_ref, x_ref, xs_hbm, idx_smem, zbuf, sem_i, sem_z, sem):
    i = pl.program_id(0)
    tm = x_ref.shape[0]
    bm = zbuf.shape[0]

    @pl.when(i == 0)
    def _():
        zbuf[...] = jnp.zeros(zbuf.shape, F32)

        def tail(e):
            start_row = pl.multiple_of(pend_ref[e] - bm, bm)
            return pltpu.make_async_copy(zbuf, xs_hbm.at[pl.ds(start_row, bm), :], sem_z)

        def start(e, carry):
            @pl.when(padded_ref[e] > 0)
            def _():
                tail(e).start()
            return carry

        def wait(e, carry):
            @pl.when(padded_ref[e] > 0)
            def _():
                tail(e).wait()
            return carry

        lax.fori_loop(0, N_EXPERTS, start, 0)
        lax.fori_loop(0, N_EXPERTS, wait, 0)

        def unused(b):
            return pltpu.make_async_copy(zbuf, xs_hbm.at[pl.ds(pl.multiple_of(b * bm, bm), bm), :], sem_z)

        first_unused = pend_ref[N_EXPERTS - 1] // bm
        n_blocks = xs_hbm.shape[0] // bm
        lax.fori_loop(first_unused, n_blocks, lambda b, c: (unused(b).start(), c)[1], 0)
        lax.fori_loop(first_unused, n_blocks, lambda b, c: (unused(b).wait(), c)[1], 0)

    cp = pltpu.make_async_copy(dest_ref.at[i], idx_smem, sem_i)
    cp.start()
    cp.wait()

    def scatter(r, carry):
        for k in range(2):
            dst = idx_smem[2 * r + k]
            pltpu.make_async_copy(x_ref.at[pl.ds(r, 1), :], xs_hbm.at[pl.ds(dst, 1), :], sem).start()
        return carry

    lax.fori_loop(0, tm, scatter, 0, unroll=8)
    for k in range(2):
        pltpu.make_async_copy(x_ref, xs_hbm.at[pl.ds(0, tm), :], sem).wait()


def _dispatch(x, dest, pend, padded, cap, *, tm=512):
    n, d = x.shape
    tm = min(tm, n)
    nt = n // tm
    grid_spec = pltpu.PrefetchScalarGridSpec(
        num_scalar_prefetch=2,
        grid=(nt,),
        in_specs=[pl.BlockSpec((nt, 2 * tm), lambda i, pe, pa: (0, 0)),
                  pl.BlockSpec((tm, d), lambda i, pe, pa: (i, 0))],
        out_specs=pl.BlockSpec(memory_space=pl.ANY),
        scratch_shapes=[pltpu.SMEM((2 * tm,), jnp.int32), pltpu.VMEM((MOE_BM, d), F32),
                        pltpu.SemaphoreType.DMA(()), pltpu.SemaphoreType.DMA(()), pltpu.SemaphoreType.DMA(())],
    )
    return pl.pallas_call(
        _dispatch_kernel,
        grid_spec=grid_spec,
        out_shape=jax.ShapeDtypeStruct((cap, d), F32),
        compiler_params=_cparams(("arbitrary",)),
        name="dispatch",
    )(pend, padded, dest.reshape(nt, 2 * tm), x)


def _experts_kernel(blk_e_ref, nused_ref, xs_ref, wg_ref, wu_ref, wd_ref, ys_ref, wgb, wub, wdb):
    i = pl.program_id(0)

    @pl.when(i < nused_ref[0])
    def _():
        prev = blk_e_ref[jnp.maximum(i - 1, 0)]

        @pl.when(jnp.logical_or(i == 0, blk_e_ref[i] != prev))
        def _():
            wgb[...] = wg_ref[...].astype(BF16)
            wub[...] = wu_ref[...].astype(BF16)
            wdb[...] = wd_ref[...].astype(BF16)

        xb = xs_ref[...].astype(BF16)
        hg = jnp.dot(xb, wgb[...], preferred_element_type=F32)
        hu = jnp.dot(xb, wub[...], preferred_element_type=F32)
        hid = (hg * jax.nn.sigmoid(hg) * hu).astype(BF16)
        ys_ref[...] = jnp.dot(hid, wdb[...], preferred_element_type=F32)

    @pl.when(i >= nused_ref[0])
    def _():
        ys_ref[...] = jnp.zeros(ys_ref.shape, F32)


def _experts(xs, blk_e, n_used, w_gate, w_up, w_down, layer):
    cap, d = xs.shape
    bm = MOE_BM
    de = w_gate.shape[3]
    row_in = lambda i, be, nu: (jnp.minimum(i, nu[0] - 1), 0)
    row = lambda i, be, nu: (i, 0)
    grid_spec = pltpu.PrefetchScalarGridSpec(
        num_scalar_prefetch=2,
        grid=(cap // bm,),
        in_specs=[pl.BlockSpec((bm, d), row_in),
                  pl.BlockSpec((None, None, d, de), lambda i, be, nu: (layer, be[i], 0, 0)),
                  pl.BlockSpec((None, None, d, de), lambda i, be, nu: (layer, be[i], 0, 0)),
                  pl.BlockSpec((None, None, de, d), lambda i, be, nu: (layer, be[i], 0, 0))],
        out_specs=pl.BlockSpec((bm, d), row),
        scratch_shapes=[pltpu.VMEM((d, de), BF16), pltpu.VMEM((d, de), BF16), pltpu.VMEM((de, d), BF16)],
    )
    return pl.pallas_call(
        _experts_kernel,
        grid_spec=grid_spec,
        out_shape=jax.ShapeDtypeStruct((cap, d), F32),
        compiler_params=_cparams(("arbitrary",)),
        name="experts",
    )(blk_e, n_used, xs, w_gate, w_up, w_down)


def _combine_kernel(dest_ref, x_ref, gw_ref, g_ref, b_ref, ys_hbm, o_ref, idx_smem, ybuf, sem_i, sem):
    i = pl.program_id(0)
    tm = x_ref.shape[0]
    cp = pltpu.make_async_copy(dest_ref.at[i], idx_smem, sem_i)
    cp.start()
    cp.wait()

    def gather(r, carry):
        for k in range(2):
            src = idx_smem[2 * r + k]
            pltpu.make_async_copy(ys_hbm.at[pl.ds(src, 1), :], ybuf.at[k, pl.ds(r, 1), :], sem).start()
        return carry

    lax.fori_loop(0, tm, gather, 0, unroll=8)
    for k in range(2):
        pltpu.make_async_copy(ys_hbm.at[pl.ds(0, tm), :], ybuf.at[k], sem).wait()
    gw = gw_ref[...]
    ffn = ybuf[0] * gw[:, 0:1] + ybuf[1] * gw[:, 1:2]
    o_ref[...] = _layer_norm_rows(ALPHA * x_ref[...] + ffn, g_ref[...], b_ref[...])


def _combine(x, ys, dest, gw, ln_g, ln_b, *, tm=512):
    n, d = x.shape
    tm = min(tm, n)
    nt = n // tm
    return pl.pallas_call(
        _combine_kernel,
        grid=(nt,),
        in_specs=[pl.BlockSpec((nt, 2 * tm), lambda i: (0, 0)),
                  pl.BlockSpec((tm, d), lambda i: (i, 0)), pl.BlockSpec((tm, LANE), lambda i: (i, 0)),
                  pl.BlockSpec((1, d), lambda i: (0, 0)), pl.BlockSpec((1, d), lambda i: (0, 0)),
                  pl.BlockSpec(memory_space=pl.ANY)],
        out_specs=pl.BlockSpec((tm, d), lambda i: (i, 0)),
        out_shape=jax.ShapeDtypeStruct((n, d), F32),
        scratch_shapes=[pltpu.SMEM((2 * tm,), jnp.int32), pltpu.VMEM((2, tm, d), F32),
                        pltpu.SemaphoreType.DMA(()), pltpu.SemaphoreType.DMA(())],
        compiler_params=_cparams(("arbitrary",)),
        name="combine",
    )(dest.reshape(nt, 2 * tm), x, gw, ln_g.reshape(1, d), ln_b.reshape(1, d), ys)


def _moe(x, w_grp, b_grp, w_exp, b_exp, w_gate, w_up, w_down, layer, ln_g, ln_b):
    n, d = x.shape
    wr = jnp.zeros((d, LANE), F32).at[:, :N_EXPERTS].set(w_exp).at[:, N_EXPERTS:N_EXPERTS + N_GROUPS].set(w_grp)
    br = jnp.zeros((1, LANE), F32).at[0, :N_EXPERTS].set(b_exp).at[0, N_EXPERTS:N_EXPERTS + N_GROUPS].set(b_grp)
    ids, gw, cnt = _route(x, wr, br)
    bm = MOE_BM
    counts = cnt[0, :N_EXPERTS]
    padded = (counts + bm - 1) // bm * bm
    pend = jnp.cumsum(padded)
    pstart = pend - padded
    e, r = ids[:, 0:2], ids[:, 2:4]
    onehot = e[:, :, None] == jnp.arange(N_EXPERTS, dtype=jnp.int32)[None, None, :]
    dest = jnp.sum(jnp.where(onehot, pstart[None, None, :], 0), axis=-1) + r
    cap = 2 * n + N_EXPERTS * bm
    nb = cap // bm
    blk_start = jnp.arange(nb, dtype=jnp.int32) * bm
    blk_e = jnp.minimum(jnp.sum((pend[None, :] <= blk_start[:, None]).astype(jnp.int32), axis=1), N_EXPERTS - 1)
    n_used = (pend[-1:] // bm).astype(jnp.int32)
    xs = _dispatch(x, dest, pend.astype(jnp.int32), padded.astype(jnp.int32), cap)
    ys = _experts(xs, blk_e, n_used, w_gate, w_up, w_down, layer)
    return _combine(x, ys, dest, gw, ln_g, ln_b)


def _even_layer(x, batch, seq, layer_idx, w_in, dec_f, dec_b, lq1, lk1, lq2, lk2, subln, w_out, ln_g, ln_b):
    d = x.shape[1]
    w = HEADS * LANE
    z = _proj(x, w_in.astype(BF16), k=d)
    kw = dict(batch=batch, seq=seq)
    ret_seg = [(0, LANE)]
    q = _prep(z, 0, tables=_rope_tables(seq, ret_seg, RET_THETA, 1.0), sh=LANE // 2, **kw)
    k = _prep(z, 4, tables=_rope_tables(seq, ret_seg, RET_THETA, LANE ** -0.5), sh=LANE // 2, **kw)
    v = _prep(z, 8, **kw)
    decays = jnp.stack([dec_f, dec_b]).astype(F32)
    of, ob = _retention(q, k, v, decays, **kw)
    diff_seg = [(0, DIFF_ROT_DIM), (DIFF_HEAD_DIM, DIFF_ROT_DIM)]
    dq = _prep(z, 16, tables=_rope_tables(seq, diff_seg, ROPE_THETA, DIFF_HEAD_DIM ** -0.5 * LOG2E), sh=DIFF_ROT_DIM // 2, **kw)
    dk = _prep(z, 20, tables=_rope_tables(seq, diff_seg, ROPE_THETA, 1.0), sh=DIFF_ROT_DIM // 2, **kw)
    dvt = _prep(z, 24, transpose=True, tm=FLASH_TK_DIFF, **kw)
    lam_init = 0.8 - 0.6 * math.exp(-0.3 * layer_idx)
    diff = _flash(dq, dk, dvt, diff=(lq1, lk1, lq2, lk2, subln), lam_init=lam_init, **kw)
    wo = w_out.astype(BF16)
    return _outproj(x, diff, of, ob, z, 3, jnp.ones((LANE,), F32), wo[w:], wo[:w], ln_g, ln_b, group_norm=True)


def _odd_layer(x, batch, seq, w_in, q_norm, w_uq, kv_norm, w_ukv, w2_f, b_f, w2_b, b_b, gla_norm, w_out, ln_g, ln_b):
    d = x.shape[1]
    w = HEADS * LANE
    o = np.cumsum([0, MLA_Q_RANK, MLA_KV_RANK, MLA_ROPE, HEADS * GLA_K_DIM, HEADS * GLA_K_DIM, w, w,
                   GLA_GATE_RANK, GLA_GATE_RANK]).tolist()
    zeros = lambda c: jnp.zeros((d, c), F32)
    w_in2 = jnp.concatenate([
        w_in[:, o[0]:o[2]], zeros(MLA_NOPE), w_in[:, o[2]:o[3]], zeros(LANE - MLA_NOPE - MLA_ROPE),
        w_in[:, o[3]:o[7]], w_in[:, o[7]:o[9]], zeros(LANE - 2 * GLA_GATE_RANK)], axis=1).astype(BF16)
    z = _proj(x, w_in2, k=d, tm=512, tn=w_in2.shape[1])
    kw = dict(batch=batch, seq=seq)
    qd = MLA_NOPE + MLA_ROPE
    w_uq2 = jnp.pad(w_uq.reshape(MLA_Q_RANK, HEADS, qd), ((0, 0), (0, 0), (0, LANE - qd))).reshape(MLA_Q_RANK, w)
    ukv = w_ukv.reshape(MLA_KV_RANK, HEADS, MLA_NOPE + MLA_V)
    w_uk2 = jnp.pad(ukv[:, :, :MLA_NOPE], ((0, 0), (0, 0), (0, LANE - MLA_NOPE))).reshape(MLA_KV_RANK, w)
    w_uv2 = ukv[:, :, MLA_NOPE:].reshape(MLA_KV_RANK, w)
    qh = _proj(z, w_uq2.astype(BF16), k=MLA_Q_RANK, xcol=0, gain=q_norm)
    kvh = _proj(z, jnp.concatenate([w_uk2, w_uv2], axis=1).astype(BF16), k=MLA_KV_RANK, xcol=2, gain=kv_norm)
    rope_seg = [(MLA_NOPE, MLA_ROPE)]
    q = _prep(qh, 0, tables=_rope_tables(seq, rope_seg, ROPE_THETA, qd ** -0.5 * LOG2E), sh=MLA_ROPE // 2, **kw)
    k = _prep(z, 3, per_head=False, tables=_rope_tables(seq, rope_seg, ROPE_THETA, 1.0), sh=MLA_ROPE // 2,
              add=kvh, add_col0=0, **kw)
    vt = _prep(kvh, HEADS, transpose=True, tm=FLASH_TK, **kw)
    mla = _flash(q, k, vt, **kw)
    wk = HEADS * GLA_K_DIM
    pad_rows = lambda m, r0: jnp.zeros((LANE, wk), F32).at[r0:r0 + GLA_GATE_RANK].set(m).astype(BF16)
    of, ob = _gla(z, qcol=2, kcol=3, vcol=2, lcol=16,
                  w2f=pad_rows(w2_f, 0), bf=b_f.reshape(1, wk), w2b=pad_rows(w2_b, GLA_GATE_RANK), bb=b_b.reshape(1, wk), **kw)
    wo = w_out.astype(BF16)
    return _outproj(x, mla, of, ob, z, 3, gla_norm, wo[:w], wo[w:], ln_g, ln_b, group_norm=False)


def kernel(x, ev_w_in, ev_ret_decay_f, ev_ret_decay_b, ev_lq1, ev_lk1, ev_lq2, ev_lk2, ev_subln, ev_w_out, od_w_in, od_q_norm, od_w_uq, od_kv_norm, od_w_ukv, od_gla_w2_f, od_gla_b_f, od_gla_w2_b, od_gla_b_b, od_gla_norm, od_w_out, ln1_g, ln1_b, ln2_g, ln2_b, moe_w_grp, moe_b_grp, moe_w_exp, moe_b_exp, moe_w_gate, moe_w_up, moe_w_down):
    batch, seq, d = x.shape
    h = x.reshape(batch * seq, d)
    for i in range(DEPTH):
        j = i // 2
        if i % 2 == 0:
            h = _even_layer(h, batch, seq, i, ev_w_in[j], ev_ret_decay_f[j], ev_ret_decay_b[j], ev_lq1[j], ev_lk1[j],
                            ev_lq2[j], ev_lk2[j], ev_subln[j], ev_w_out[j], ln1_g[i], ln1_b[i])
        else:
            h = _odd_layer(h, batch, seq, od_w_in[j], od_q_norm[j], od_w_uq[j], od_kv_norm[j], od_w_ukv[j],
                           od_gla_w2_f[j], od_gla_b_f[j], od_gla_w2_b[j], od_gla_b_b[j], od_gla_norm[j], od_w_out[j],
                           ln1_g[i], ln1_b[i])
        h = _moe(h, moe_w_grp[i], moe_b_grp[i], moe_w_exp[i], moe_b_exp[i], moe_w_gate, moe_w_up, moe_w_down, i,
                 ln2_g[i], ln2_b[i])
    return h.reshape(batch, seq, d)
```

```python
import functools
import math

import numpy as np
import jax
import jax.numpy as jnp
from jax import lax
from jax.experimental import pallas as pl
from jax.experimental.pallas import tpu as pltpu

F32 = jnp.float32
BF16 = jnp.bfloat16

HEADS = 4
LANE = 128
RET_THETA = 10000.0
ROPE_THETA = 500000.0
DIFF_HEAD_DIM = 64
DIFF_ROT_DIM = 16
MLA_Q_RANK = 256
MLA_KV_RANK = 128
MLA_NOPE = 64
MLA_ROPE = 32
MLA_V = 128
GLA_K_DIM = 64
GLA_GATE_RANK = 16
GLA_TAU = 16.0
N_GROUPS = 4
EXPERTS_PER_GROUP = 8
N_EXPERTS = N_GROUPS * EXPERTS_PER_GROUP
DEPTH = 2
ALPHA = (2.0 * DEPTH) ** 0.25
LN_EPS = 1e-5
RMS_EPS = 1e-6

VMEM_LIMIT = 48 * 1024 * 1024


def _div_pow2(x, n):
    return lax.shift_right_logical(x, int(n).bit_length() - 1)


def _mod_pow2(x, n):
    return lax.bitwise_and(x, int(n) - 1)


def _cparams(sem):
    return pltpu.CompilerParams(dimension_semantics=sem, vmem_limit_bytes=VMEM_LIMIT)


def _proj_kernel(*refs, rms):
    if rms:
        x_ref, g_ref, w_ref, o_ref, xb_ref = refs
    else:
        x_ref, w_ref, o_ref, xb_ref = refs

    @pl.when(pl.program_id(1) == 0)
    def _():
        x = x_ref[...]
        if rms:
            x = x * lax.rsqrt(jnp.mean(x * x, axis=-1, keepdims=True) + RMS_EPS) * g_ref[...]
        xb_ref[...] = x.astype(BF16)

    o_ref[...] = jnp.dot(xb_ref[...], w_ref[...], preferred_element_type=F32).astype(o_ref.dtype)


def _proj(x, w, *, k, xcol=0, tm=1024, tn=512, gain=None):
    n = x.shape[0]
    m = w.shape[1]
    tm = min(tm, n)
    tn = min(tn, m)
    in_specs = [pl.BlockSpec((tm, k), lambda i, j: (i, xcol))]
    args = [x]
    if gain is not None:
        in_specs.append(pl.BlockSpec((1, k), lambda i, j: (0, 0)))
        args.append(gain.reshape(1, k))
    in_specs.append(pl.BlockSpec((k, tn), lambda i, j: (0, j)))
    args.append(w)
    return pl.pallas_call(
        functools.partial(_proj_kernel, rms=gain is not None),
        grid=(n // tm, m // tn),
        in_specs=in_specs,
        out_specs=pl.BlockSpec((tm, tn), lambda i, j: (i, j)),
        out_shape=jax.ShapeDtypeStruct((n, m), F32),
        scratch_shapes=[pltpu.VMEM((tm, k), BF16)],
        compiler_params=_cparams(("parallel", "arbitrary")),
        name="proj",
    )(*args)


def _prep_kernel(*refs, has_tab, has_add, sh, transpose):
    refs = list(refs)
    z_ref = refs.pop(0)
    add_ref = refs.pop(0) if has_add else None
    if has_tab:
        c_ref, sa_ref, sb_ref = refs.pop(0), refs.pop(0), refs.pop(0)
    o_ref = refs.pop(0)
    for h in range(HEADS):
        z = z_ref[:, h * LANE:(h + 1) * LANE] if z_ref.shape[1] > LANE else z_ref[...]
        if has_tab:
            out = (z * c_ref[...] + pltpu.roll(z, sh, axis=1) * sa_ref[...]
                   + pltpu.roll(z, LANE - sh, axis=1) * sb_ref[...])
        else:
            out = z
        if has_add:
            out = out + add_ref[:, h * LANE:(h + 1) * LANE]
        if transpose:
            o_ref[h, :LANE, :] = out.T.astype(o_ref.dtype)
            o_ref[h, LANE:, :] = jnp.ones((o_ref.shape[1] - LANE, o_ref.shape[2]), o_ref.dtype)
        else:
            o_ref[h] = out.astype(o_ref.dtype)


def _prep(z, col0, *, batch, seq, per_head=True, tables=None, sh=0, add=None, add_col0=0, transpose=False, tm=1024):
    tm = min(tm, seq // 2) if transpose else min(tm, seq)
    nt = seq // tm
    w = HEADS * LANE
    if per_head:
        in_specs = [pl.BlockSpec((tm, w), lambda b, i: (b * nt + i, col0 // HEADS))]
    else:
        in_specs = [pl.BlockSpec((tm, LANE), lambda b, i: (b * nt + i, col0))]
    args = [z]
    if add is not None:
        in_specs.append(pl.BlockSpec((tm, w), lambda b, i: (b * nt + i, add_col0 // HEADS)))
        args.append(add)
    if tables is not None:
        for t in tables:
            in_specs.append(pl.BlockSpec((tm, LANE), lambda b, i: (i, 0)))
            args.append(t)
    if transpose:
        out_shape = jax.ShapeDtypeStruct((batch, HEADS, nt, LANE + ONES_ROWS, tm), BF16)
        out_spec = pl.BlockSpec((None, HEADS, None, LANE + ONES_ROWS, tm), lambda b, i: (b, 0, i, 0, 0))
    else:
        out_shape = jax.ShapeDtypeStruct((batch, HEADS, seq, LANE), BF16)
        out_spec = pl.BlockSpec((None, HEADS, tm, LANE), lambda b, i: (b, 0, i, 0))
    return pl.pallas_call(
        functools.partial(_prep_kernel, has_tab=tables is not None, has_add=add is not None, sh=sh, transpose=transpose),
        grid=(batch, nt),
        in_specs=in_specs,
        out_specs=out_spec,
        out_shape=out_shape,
        compiler_params=_cparams(("parallel", "parallel")),
        name="prep",
    )(*args)


def _rope_tables(seq, segs, theta, scale):
    pos = jnp.arange(seq, dtype=F32)
    inv = jnp.zeros((LANE,), F32)
    lo = np.zeros((LANE,), bool)
    hi = np.zeros((LANE,), bool)
    for start, rot in segs:
        half = rot // 2
        f = jnp.power(jnp.float32(theta), -jnp.arange(0, rot, 2, dtype=F32) / rot)
        inv = inv.at[start:start + half].set(f).at[start + half:start + rot].set(f)
        lo[start:start + half] = True
        hi[start + half:start + rot] = True
    ang = pos[:, None] * inv[None, :]
    cos, sin = jnp.cos(ang), jnp.sin(ang)
    c = jnp.where(lo | hi, cos, 1.0) * scale
    sa = jnp.where(hi, sin, 0.0) * scale
    sb = jnp.where(lo, -sin, 0.0) * scale
    return c, sa, sb


ONES_ROWS = 16
LOG2E = math.log2(math.e)
FLASH_TK = 1024
FLASH_TK_DIFF = 512


def _flash_kernel(*refs, ncomp, nk, lam_init):
    if ncomp == 2:
        q_ref, k_ref, vt_ref, lq1, lk1, lq2, lk2, g_ref, o_ref, *scr = refs
    else:
        q_ref, k_ref, vt_ref, o_ref, *scr = refs
    qm_sc, m_sc, acc_sc, s0, s1, cm0, cm1, p0, p1, al0, al1 = scr
    tk = s0.shape[1]
    q = q_ref[...]
    if ncomp == 2:
        lane = lax.broadcasted_iota(jnp.int32, q.shape, 1)
        zero = jnp.zeros_like(q)
        qm_sc[0] = jnp.where(lane < DIFF_HEAD_DIM, q, zero)
        qm_sc[1] = jnp.where(lane >= DIFF_HEAD_DIM, q, zero)
    else:
        qm_sc[0] = q
    m_sc[...] = jnp.full(m_sc.shape, -jnp.inf, F32)
    acc_sc[...] = jnp.zeros(acc_sc.shape, F32)

    def scores(j, s_ref, cm_ref):
        k = k_ref[j * tk:(j + 1) * tk, :]
        for c in range(ncomp):
            s = lax.dot_general(k, qm_sc[c], (((1,), (1,)), ((), ())), preferred_element_type=F32)
            s_ref[c] = s
            cm_ref[c] = jnp.max(s, axis=0, keepdims=True)

    def softmax(s_ref, cm_ref, p_ref, al_ref):
        for c in range(ncomp):
            m_old = m_sc[c]
            m_new = jnp.maximum(m_old, cm_ref[c])
            al_ref[c] = jnp.exp2(m_old - m_new)
            p_ref[c] = jnp.exp2(s_ref[c] - m_new).astype(BF16)
            m_sc[c] = m_new

    def values(j, p_ref, al_ref):
        vt = vt_ref[j]
        for c in range(ncomp):
            acc_sc[c] = al_ref[c] * acc_sc[c] + jnp.dot(vt, p_ref[c], preferred_element_type=F32)

    bufs = ((s0, cm0, p0, al0), (s1, cm1, p1, al1))
    scores(0, s0, cm0)
    for j in range(nk):
        s_c, cm_c, p_c, al_c = bufs[j % 2]
        s_n, cm_n, p_n, al_n = bufs[(j + 1) % 2]
        if j + 1 < nk:
            scores(j + 1, s_n, cm_n)
        softmax(s_c, cm_c, p_c, al_c)
        if j >= 1:
            values(j - 1, p_n, al_n)
    values(nk - 1, *bufs[(nk - 1) % 2][2:])

    def normalised(c):
        acc = acc_sc[c]
        return acc[:LANE] / acc[LANE:LANE + 1]

    o = normalised(0)
    if ncomp == 2:
        lam = (jnp.exp(jnp.sum(lq1[...] * lk1[...], keepdims=True))
               - jnp.exp(jnp.sum(lq2[...] * lk2[...], keepdims=True)) + lam_init)
        o = o - lam * normalised(1)
        o = o * lax.rsqrt(jnp.mean(o * o, axis=0, keepdims=True) + RMS_EPS) * g_ref[...] * (1.0 - lam_init)
    o_ref[...] = o.T


def _flash(q, k, vt, *, batch, seq, tq=512, diff=None, lam_init=0.0):
    nk, vrows, tk = vt.shape[2], vt.shape[3], vt.shape[4]
    assert nk % 2 == 0 and vrows == LANE + ONES_ROWS
    tq = min(tq, seq)
    nq = seq // tq
    ncomp = 2 if diff is not None else 1
    in_specs = [
        pl.BlockSpec((None, None, tq, LANE), lambda b, h, i: (b, h, i, 0)),
        pl.BlockSpec((None, None, seq, LANE), lambda b, h, i: (b, h, 0, 0)),
        pl.BlockSpec((None, None, nk, vrows, tk), lambda b, h, i: (b, h, 0, 0, 0)),
    ]
    args = [q, k, vt]
    if diff is not None:
        lq1, lk1, lq2, lk2, subln = diff
        for v in (lq1, lk1, lq2, lk2):
            in_specs.append(pl.BlockSpec((1, DIFF_HEAD_DIM), lambda b, h, i: (0, 0)))
            args.append(v.reshape(1, DIFF_HEAD_DIM))
        in_specs.append(pl.BlockSpec((LANE, 1), lambda b, h, i: (0, 0)))
        args.append(subln.reshape(LANE, 1))
    return pl.pallas_call(
        functools.partial(_flash_kernel, ncomp=ncomp, nk=nk, lam_init=lam_init),
        grid=(batch, HEADS, nq),
        in_specs=in_specs,
        out_specs=pl.BlockSpec((tq, LANE), lambda b, h, i: (b * nq + i, h)),
        out_shape=jax.ShapeDtypeStruct((batch * seq, HEADS * LANE), F32),
        scratch_shapes=[pltpu.VMEM((ncomp, tq, LANE), BF16),
                        pltpu.VMEM((ncomp, 1, tq), F32), pltpu.VMEM((ncomp, vrows, tq), F32),
                        pltpu.VMEM((ncomp, tk, tq), F32), pltpu.VMEM((ncomp, tk, tq), F32),
                        pltpu.VMEM((ncomp, 1, tq), F32), pltpu.VMEM((ncomp, 1, tq), F32),
                        pltpu.VMEM((ncomp, tk, tq), BF16), pltpu.VMEM((ncomp, tk, tq), BF16),
                        pltpu.VMEM((ncomp, 1, tq), F32), pltpu.VMEM((ncomp, 1, tq), F32)],
        compiler_params=_cparams(("parallel", "parallel", "parallel")),
        name="flash_diff" if diff is not None else "flash_mla",
    )(*args)


def _ret_kernel(dec_ref, qf, kf, vf, qb, kb, vb, of_ref, ob_ref, s_sc, *, chunk):
    h = pl.program_id(1)

    @pl.when(pl.program_id(2) == 0)
    def _():
        s_sc[...] = jnp.zeros(s_sc.shape, F32)

    ii = lax.broadcasted_iota(jnp.int32, (chunk, chunk), 0)
    jj = lax.broadcasted_iota(jnp.int32, (chunk, chunk), 1)
    r = lax.broadcasted_iota(jnp.int32, (chunk, 1), 0).astype(F32)
    for d, (q_ref, k_ref, v_ref, o_ref) in enumerate(((qf, kf, vf, of_ref), (qb, kb, vb, ob_ref))):
        la = -jnp.exp(jnp.full((1, 1), dec_ref[d, h], F32))
        if d == 0:
            mask, dist = ii >= jj, (ii - jj).astype(F32)
            qdec, kdec = jnp.exp(la * (r + 1.0)), jnp.exp(la * (chunk - 1.0 - r))
        else:
            mask, dist = jj > ii, (jj - ii).astype(F32)
            qdec, kdec = jnp.exp(la * (chunk - r)), jnp.exp(la * r)
        decay = jnp.where(mask, jnp.exp(jnp.where(mask, dist * la, 0.0)), 0.0)
        q, k, v = q_ref[...], k_ref[...], v_ref[...]
        s = lax.dot_general(q, k, (((1,), (1,)), ((), ())), preferred_element_type=F32)
        o = jnp.dot((s * decay).astype(BF16), v, preferred_element_type=F32)
        state = s_sc[d]
        o = o + qdec * jnp.dot(q, state.astype(BF16), preferred_element_type=F32)
        kd = (k.astype(F32) * kdec).astype(BF16)
        s_sc[d] = jnp.exp(la * float(chunk)) * state + lax.dot_general(
            kd, v, (((0,), (0,)), ((), ())), preferred_element_type=F32)
        o_ref[...] = o


def _retention(q, k, v, decays, *, batch, seq, chunk=256):
    chunk = min(chunk, seq)
    n = seq // chunk
    fwd = pl.BlockSpec((None, None, chunk, LANE), lambda b, h, c: (b, h, c, 0))
    bwd = pl.BlockSpec((None, None, chunk, LANE), lambda b, h, c: (b, h, n - 1 - c, 0))
    out = jax.ShapeDtypeStruct((batch * seq, HEADS * LANE), F32)
    return pl.pallas_call(
        functools.partial(_ret_kernel, chunk=chunk),
        grid=(batch, HEADS, n),
        in_specs=[pl.BlockSpec(memory_space=pltpu.SMEM), fwd, fwd, fwd, bwd, bwd, bwd],
        out_specs=[pl.BlockSpec((chunk, LANE), lambda b, h, c: (b * n + c, h)),
                   pl.BlockSpec((chunk, LANE), lambda b, h, c: (b * n + n - 1 - c, h))],
        out_shape=[out, out],
        scratch_shapes=[pltpu.VMEM((2, LANE, LANE), F32)],
        compiler_params=_cparams(("parallel", "parallel", "arbitrary")),
        name="retention",
    )(decays, q, k, v, q, k, v)


GLA_SUB = 8


def _split3(x):
    x1 = x.astype(BF16)
    r1 = x - x1.astype(F32)
    x2 = r1.astype(BF16)
    x3 = (r1 - x2.astype(F32)).astype(BF16)
    return x1, x2, x3


def _gla_direction(q, k, v, lr, w2, bias, st, reverse):
    C, wk = q.shape
    wv = v.shape[1]
    dk, dv = wk // HEADS, wv // HEADS
    z = jnp.dot(lr.astype(BF16), w2, preferred_element_type=F32) + bias
    g = (jnp.minimum(z, 0.0) - jnp.log(1.0 + jnp.exp(-jnp.abs(z)))) * (1.0 / GLA_TAU)
    ii = lax.broadcasted_iota(jnp.int32, (C, C), 0)
    jj = lax.broadcasted_iota(jnp.int32, (C, C), 1)
    tri = jnp.where(ii >= jj, 1.0, 0.0).astype(BF16)
    b = sum(jnp.dot(tri, part, preferred_element_type=F32) for part in _split3(g))
    tot = b[C - 1:C, :]
    c = (tot - b + g) if reverse else b

    qe = (q * jnp.exp(jnp.minimum(c, 0.0))).astype(BF16)
    o = lax.dot_general(qe, st.astype(BF16), (((1,), (1,)), ((), ())), preferred_element_type=F32)
    ke = (k * jnp.exp(jnp.minimum(tot - c, 0.0))).astype(BF16)
    upd = lax.dot_general(v.astype(BF16), ke, (((0,), (0,)), ((), ())), preferred_element_type=F32)
    rr = _div_pow2(lax.broadcasted_iota(jnp.int32, (wv, wk), 0), dv)
    cc = _div_pow2(lax.broadcasted_iota(jnp.int32, (wv, wk), 1), dk)
    new_st = jnp.where(rr == cc, st * jnp.exp(tot) + upd, 0.0)

    lane_head = _div_pow2(lax.broadcasted_iota(jnp.int32, (C, wk), 1), dk)
    scores = [jnp.zeros((C, C), F32) for _ in range(HEADS)]
    hsz = C // 2
    while hsz >= GLA_SUB:
        blk = 2 * hsz
        rows = []
        for m in range(C // blk):
            rrow = m * blk + (hsz if reverse else hsz - 1)
            rows.append(jnp.broadcast_to(c[rrow:rrow + 1, :], (blk, wk)))
        ref = jnp.concatenate(rows, axis=0) if len(rows) > 1 else rows[0]
        qt = q * jnp.exp(jnp.minimum(c - ref, 0.0))
        kt = (k * jnp.exp(jnp.minimum(ref - c, 0.0))).astype(BF16)
        same = _div_pow2(ii, blk) == _div_pow2(jj, blk)
        if reverse:
            lvl = same & (_mod_pow2(ii, blk) < hsz) & (_mod_pow2(jj, blk) >= hsz)
        else:
            lvl = same & (_mod_pow2(ii, blk) >= hsz) & (_mod_pow2(jj, blk) < hsz)
        for h in range(HEADS):
            qh = jnp.where(lane_head == h, qt, 0.0).astype(BF16)
            s = lax.dot_general(qh, kt, (((1,), (1,)), ((), ())), preferred_element_type=F32)
            scores[h] = scores[h] + jnp.where(lvl, s, 0.0)
        hsz //= 2

    assert dv == C
    er = _div_pow2(lax.broadcasted_iota(jnp.int32, (wk, wv), 0), dk)
    ec = _div_pow2(lax.broadcasted_iota(jnp.int32, (wk, wv), 1), dv)
    expand = jnp.where(er == ec, 1.0, 0.0).astype(BF16)
    dist = (jj - ii) if reverse else (ii - jj)
    same_sub = _div_pow2(ii, GLA_SUB) == _div_pow2(jj, GLA_SUB)
    for lag in range(1 if reverse else 0, GLA_SUB):
        if lag == 0:
            t = q * k
        else:
            shift = (C - lag) if reverse else lag
            ks, cs = pltpu.roll(k, shift, axis=0), pltpu.roll(c, shift, axis=0)
            t = q * ks * jnp.exp(jnp.minimum(c - cs, 0.0))
        red = jnp.dot(t.astype(BF16), expand, preferred_element_type=F32)
        on_diag = same_sub & (dist == lag)
        for h in range(HEADS):
            scores[h] = scores[h] + jnp.where(on_diag, red[:, h * dv:(h + 1) * dv], 0.0)

    vb = v.astype(BF16)
    o = o + jnp.concatenate(
        [jnp.dot(scores[h].astype(BF16), vb[:, h * dv:(h + 1) * dv], preferred_element_type=F32) for h in range(HEADS)],
        axis=1)
    return o, new_st


def _gla_kernel(qf, kf, vf, lf, qb, kb, vb, lb, w2f, bf, w2b, bb, of_ref, ob_ref, s_sc, *, qscale):
    @pl.when(pl.program_id(1) == 0)
    def _():
        s_sc[...] = jnp.zeros(s_sc.shape, F32)

    o, st = _gla_direction(qf[...] * qscale, kf[...], vf[...], lf[...], w2f[...], bf[...], s_sc[0], False)
    of_ref[...] = o
    s_sc[0] = st
    o, st = _gla_direction(qb[...] * qscale, kb[...], vb[...], lb[...], w2b[...], bb[...], s_sc[1], True)
    ob_ref[...] = o
    s_sc[1] = st


def _gla(z, *, qcol, kcol, vcol, lcol, w2f, bf, w2b, bb, batch, seq, chunk=128):
    chunk = min(chunk, seq)
    n = seq // chunk
    wk, wv = HEADS * GLA_K_DIM, HEADS * LANE

    def specs(cmap):
        return [pl.BlockSpec((chunk, wk), lambda b, c: (cmap(b, c), qcol)),
                pl.BlockSpec((chunk, wk), lambda b, c: (cmap(b, c), kcol)),
                pl.BlockSpec((chunk, wv), lambda b, c: (cmap(b, c), vcol)),
                pl.BlockSpec((chunk, LANE), lambda b, c: (cmap(b, c), lcol))]

    fmap = lambda b, c: b * n + c
    bmap = lambda b, c: b * n + n - 1 - c
    wspec = [pl.BlockSpec((LANE, wk), lambda b, c: (0, 0)), pl.BlockSpec((1, wk), lambda b, c: (0, 0))]
    out = jax.ShapeDtypeStruct((batch * seq, wv), F32)
    return pl.pallas_call(
        functools.partial(_gla_kernel, qscale=GLA_K_DIM ** -0.5),
        grid=(batch, n),
        in_specs=specs(fmap) + specs(bmap) + wspec + wspec,
        out_specs=[pl.BlockSpec((chunk, wv), lambda b, c: (fmap(b, c), 0)),
                   pl.BlockSpec((chunk, wv), lambda b, c: (bmap(b, c), 0))],
        out_shape=[out, out],
        scratch_shapes=[pltpu.VMEM((2, wv, wk), F32)],
        compiler_params=_cparams(("parallel", "arbitrary")),
        name="gla",
    )(z, z, z, z, z, z, z, z, w2f, bf, w2b, bb)


def _layer_norm_rows(r, g, b):
    mu = jnp.mean(r, axis=-1, keepdims=True)
    d = r - mu
    var = jnp.mean(d * d, axis=-1, keepdims=True)
    return d * lax.rsqrt(var + LN_EPS) * g + b


def _outproj_kernel(x_ref, fin_ref, of_ref, ob_ref, gate_ref, ng_ref, wa_ref, wb_ref, lg_ref, lb_ref, o_ref, *, group_norm):
    lin = of_ref[...] + ob_ref[...]
    parts = []
    for h in range(HEADS):
        zh = lin[:, h * LANE:(h + 1) * LANE]
        if group_norm:
            mu = jnp.mean(zh, axis=-1, keepdims=True)
            dz = zh - mu
            parts.append(dz * lax.rsqrt(jnp.mean(dz * dz, axis=-1, keepdims=True) + LN_EPS))
        else:
            parts.append(zh * lax.rsqrt(jnp.mean(zh * zh, axis=-1, keepdims=True) + RMS_EPS) * ng_ref[...])
    gate = gate_ref[...]
    lin = jnp.concatenate(parts, axis=1) * (gate * jax.nn.sigmoid(gate))
    y = (jnp.dot(fin_ref[...].astype(BF16), wa_ref[...], preferred_element_type=F32)
         + jnp.dot(lin.astype(BF16), wb_ref[...], preferred_element_type=F32))
    o_ref[...] = _layer_norm_rows(ALPHA * x_ref[...] + y, lg_ref[...], lb_ref[...])


def _outproj(x, fin, of, ob, gate_src, gate_col, norm_gain, wa, wb, ln_g, ln_b, *, group_norm, tm=512):
    n, d = x.shape
    w = HEADS * LANE
    tm = min(tm, n)
    row = lambda i: (i, 0)
    const = lambda i: (0, 0)
    return pl.pallas_call(
        functools.partial(_outproj_kernel, group_norm=group_norm),
        grid=(n // tm,),
        in_specs=[pl.BlockSpec((tm, d), row), pl.BlockSpec((tm, w), row), pl.BlockSpec((tm, w), row),
                  pl.BlockSpec((tm, w), row), pl.BlockSpec((tm, w), lambda i: (i, gate_col)),
                  pl.BlockSpec((1, LANE), const), pl.BlockSpec((w, d), const), pl.BlockSpec((w, d), const),
                  pl.BlockSpec((1, d), const), pl.BlockSpec((1, d), const)],
        out_specs=pl.BlockSpec((tm, d), row),
        out_shape=jax.ShapeDtypeStruct((n, d), F32),
        compiler_params=_cparams(("parallel",)),
        name="outproj",
    )(x, fin, of, ob, gate_src, norm_gain.reshape(1, LANE), wa, wb, ln_g.reshape(1, d), ln_b.reshape(1, d))


def _route_kernel(x_ref, w_ref, b_ref, ids_ref, gw_ref, cnt_ref, run_sc):
    @pl.when(pl.program_id(0) == 0)
    def _():
        run_sc[...] = jnp.zeros(run_sc.shape, F32)

    tm = x_ref.shape[0]
    x = x_ref[...]
    xh = x.astype(BF16)
    xl = (x - xh.astype(F32)).astype(BF16)
    wh, wl = w_ref[0], w_ref[1]
    logits = (jnp.dot(xh, wh, preferred_element_type=F32) + jnp.dot(xh, wl, preferred_element_type=F32)
              + jnp.dot(xl, wh, preferred_element_type=F32)) + b_ref[...]
    lane = lax.broadcasted_iota(jnp.int32, logits.shape, 1)
    neg = -jnp.inf
    gmask = (lane >= N_EXPERTS) & (lane < N_EXPERTS + N_GROUPS)
    gl = jnp.where(gmask, logits, neg)
    gmax = jnp.max(gl, axis=1, keepdims=True)
    lane_f = lane.astype(F32)
    first = lambda hit: jnp.min(jnp.where(hit, lane_f, float(LANE)), axis=1, keepdims=True).astype(jnp.int32)
    gidx = first(gl == gmax) - N_EXPERTS
    p_grp = 1.0 / jnp.sum(jnp.where(gmask, jnp.exp(gl - gmax), 0.0), axis=1, keepdims=True)
    el = jnp.where(_div_pow2(lane, EXPERTS_PER_GROUP) == gidx, logits, neg)
    l1 = jnp.max(el, axis=1, keepdims=True)
    e1 = first(el == l1)
    el2 = jnp.where(lane == e1, neg, el)
    l2 = jnp.max(el2, axis=1, keepdims=True)
    e2 = first(el2 == l2)
    t = jnp.exp(l2 - l1)
    w1 = p_grp / (1.0 + t)
    w2 = p_grp * t / (1.0 + t)

    onehot = jnp.where(lane == e1, 1.0, jnp.where(lane == e2, 1.0, 0.0))
    ri = lax.broadcasted_iota(jnp.int32, (tm, tm), 0)
    ci = lax.broadcasted_iota(jnp.int32, (tm, tm), 1)
    before = jnp.dot(jnp.where(ri > ci, 1.0, 0.0).astype(BF16), onehot.astype(BF16), preferred_element_type=F32)
    before = before + run_sc[...]
    r1 = jnp.sum(jnp.where(lane == e1, before, 0.0), axis=1, keepdims=True).astype(jnp.int32)
    r2 = jnp.sum(jnp.where(lane == e2, before, 0.0), axis=1, keepdims=True).astype(jnp.int32)
    run_sc[...] = run_sc[...] + jnp.sum(onehot, axis=0, keepdims=True)
    cnt_ref[...] = run_sc[...].astype(jnp.int32)
    ids_ref[...] = jnp.where(lane == 0, e1, jnp.where(lane == 1, e2, jnp.where(lane == 2, r1, jnp.where(lane == 3, r2, 0))))
    gw_ref[...] = jnp.where(lane == 0, w1, jnp.where(lane == 1, w2, 0.0))


def _route(x, w, b, *, tm=512):
    n, d = x.shape
    tm = min(tm, n)
    w_hi = w.astype(BF16)
    w = jnp.stack([w_hi, (w - w_hi.astype(F32)).astype(BF16)])
    return pl.pallas_call(
        _route_kernel,
        grid=(n // tm,),
        in_specs=[pl.BlockSpec((tm, d), lambda i: (i, 0)), pl.BlockSpec((2, d, LANE), lambda i: (0, 0, 0)),
                  pl.BlockSpec((1, LANE), lambda i: (0, 0))],
        out_specs=[pl.BlockSpec((tm, LANE), lambda i: (i, 0)), pl.BlockSpec((tm, LANE), lambda i: (i, 0)),
                   pl.BlockSpec((1, LANE), lambda i: (0, 0))],
        out_shape=[jax.ShapeDtypeStruct((n, LANE), jnp.int32), jax.ShapeDtypeStruct((n, LANE), F32),
                   jax.ShapeDtypeStruct((1, LANE), jnp.int32)],
        scratch_shapes=[pltpu.VMEM((1, LANE), F32)],
        compiler_params=_cparams(("arbitrary",)),
        name="route",
    )(x, w, b)


MOE_BM = 256


def _dispatch_kernel(pend_ref, padded_ref, dest_ref, x_ref, xs_hbm, idx_smem, zbuf, sem_i, sem_z, sem):
    i = pl.program_id(0)
    tm = x_ref.shape[0]
    bm = zbuf.shape[0]

    @pl.when(i == 0)
    def _():
        zbuf[...] = jnp.zeros(zbuf.shape, F32)

        def tail(e):
            start_row = pl.multiple_of(pend_ref[e] - bm, bm)
            return pltpu.make_async_copy(zbuf, xs_hbm.at[pl.ds(start_row, bm), :], sem_z)

        def start(e, carry):
            @pl.when(padded_ref[e] > 0)
            def _():
                tail(e).start()
            return carry

        def wait(e, carry):
            @pl.when(padded_ref[e] > 0)
            def _():
                tail(e).wait()
            return carry

        lax.fori_loop(0, N_EXPERTS, start, 0)
        lax.fori_loop(0, N_EXPERTS, wait, 0)

        def unused(b):
            return pltpu.make_async_copy(zbuf, xs_hbm.at[pl.ds(pl.multiple_of(b * bm, bm), bm), :], sem_z)

        first_unused = pend_ref[N_EXPERTS - 1] // bm
        n_blocks = xs_hbm.shape[0] // bm
        lax.fori_loop(first_unused, n_blocks, lambda b, c: (unused(b).start(), c)[1], 0)
        lax.fori_loop(first_unused, n_blocks, lambda b, c: (unused(b).wait(), c)[1], 0)

    cp = pltpu.make_async_copy(dest_ref.at[i], idx_smem, sem_i)
    cp.start()
    cp.wait()

    def scatter(r, carry):
        for k in range(2):
            dst = idx_smem[2 * r + k]
            pltpu.make_async_copy(x_ref.at[pl.ds(r, 1), :], xs_hbm.at[pl.ds(dst, 1), :], sem).start()
        return carry

    lax.fori_loop(0, tm, scatter, 0, unroll=8)
    for k in range(2):
        pltpu.make_async_copy(x_ref, xs_hbm.at[pl.ds(0, tm), :], sem).wait()


def _dispatch(x, dest, pend, padded, cap, *, tm=512):
    n, d = x.shape
    tm = min(tm, n)
    nt = n // tm
    grid_spec = pltpu.PrefetchScalarGridSpec(
        num_scalar_prefetch=2,
        grid=(nt,),
        in_specs=[pl.BlockSpec((nt, 2 * tm), lambda i, pe, pa: (0, 0)),
                  pl.BlockSpec((tm, d), lambda i, pe, pa: (i, 0))],
        out_specs=pl.BlockSpec(memory_space=pl.ANY),
        scratch_shapes=[pltpu.SMEM((2 * tm,), jnp.int32), pltpu.VMEM((MOE_BM, d), F32),
                        pltpu.SemaphoreType.DMA(()), pltpu.SemaphoreType.DMA(()), pltpu.SemaphoreType.DMA(())],
    )
    return pl.pallas_call(
        _dispatch_kernel,
        grid_spec=grid_spec,
        out_shape=jax.ShapeDtypeStruct((cap, d), F32),
        compiler_params=_cparams(("arbitrary",)),
        name="dispatch",
    )(pend, padded, dest.reshape(nt, 2 * tm), x)


def _experts_kernel(blk_e_ref, nused_ref, xs_ref, wg_ref, wu_ref, wd_ref, ys_ref, wgb, wub, wdb):
    i = pl.program_id(0)

    @pl.when(i < nused_ref[0])
    def _():
        prev = blk_e_ref[jnp.maximum(i - 1, 0)]

        @pl.when(jnp.logical_or(i == 0, blk_e_ref[i] != prev))
        def _():
            wgb[...] = wg_ref[...].astype(BF16)
            wub[...] = wu_ref[...].astype(BF16)
            wdb[...] = wd_ref[...].astype(BF16)

        xb = xs_ref[...].astype(BF16)
        hg = jnp.dot(xb, wgb[...], preferred_element_type=F32)
        hu = jnp.dot(xb, wub[...], preferred_element_type=F32)
        hid = (hg * jax.nn.sigmoid(hg) * hu).astype(BF16)
        ys_ref[...] = jnp.dot(hid, wdb[...], preferred_element_type=F32)

    @pl.when(i >= nused_ref[0])
    def _():
        ys_ref[...] = jnp.zeros(ys_ref.shape, F32)


def _experts(xs, blk_e, n_used, w_gate, w_up, w_down, layer):
    cap, d = xs.shape
    bm = MOE_BM
    de = w_gate.shape[3]
    row_in = lambda i, be, nu: (jnp.minimum(i, nu[0] - 1), 0)
    row = lambda i, be, nu: (i, 0)
    grid_spec = pltpu.PrefetchScalarGridSpec(
        num_scalar_prefetch=2,
        grid=(cap // bm,),
        in_specs=[pl.BlockSpec((bm, d), row_in),
                  pl.BlockSpec((None, None, d, de), lambda i, be, nu: (layer, be[i], 0, 0)),
                  pl.BlockSpec((None, None, d, de), lambda i, be, nu: (layer, be[i], 0, 0)),
                  pl.BlockSpec((None, None, de, d), lambda i, be, nu: (layer, be[i], 0, 0))],
        out_specs=pl.BlockSpec((bm, d), row),
        scratch_shapes=[pltpu.VMEM((d, de), BF16), pltpu.VMEM((d, de), BF16), pltpu.VMEM((de, d), BF16)],
    )
    return pl.pallas_call(
        _experts_kernel,
        grid_spec=grid_spec,
        out_shape=jax.ShapeDtypeStruct((cap, d), F32),
        compiler_params=_cparams(("arbitrary",)),
        name="experts",
    )(blk_e, n_used, xs, w_gate, w_up, w_down)


def _combine_kernel(dest_ref, x_ref, gw_ref, g_ref, b_ref, ys_hbm, o_ref, idx_smem, ybuf, sem_i, sem):
    i = pl.program_id(0)
    tm = x_ref.shape[0]
    cp = pltpu.make_async_copy(dest_ref.at[i], idx_smem, sem_i)
    cp.start()
    cp.wait()

    def gather(r, carry):
        for k in range(2):
            src = idx_smem[2 * r + k]
            pltpu.make_async_copy(ys_hbm.at[pl.ds(src, 1), :], ybuf.at[k, pl.ds(r, 1), :], sem).start()
        return carry

    lax.fori_loop(0, tm, gather, 0, unroll=8)
    for k in range(2):
        pltpu.make_async_copy(ys_hbm.at[pl.ds(0, tm), :], ybuf.at[k], sem).wait()
    gw = gw_ref[...]
    ffn = ybuf[0] * gw[:, 0:1] + ybuf[1] * gw[:, 1:2]
    o_ref[...] = _layer_norm_rows(ALPHA * x_ref[...] + ffn, g_ref[...], b_ref[...])


def _combine(x, ys, dest, gw, ln_g, ln_b, *, tm=512):
    n, d = x.shape
    tm = min(tm, n)
    nt = n // tm
    return pl.pallas_call(
        _combine_kernel,
        grid=(nt,),
        in_specs=[pl.BlockSpec((nt, 2 * tm), lambda i: (0, 0)),
                  pl.BlockSpec((tm, d), lambda i: (i, 0)), pl.BlockSpec((tm, LANE), lambda i: (i, 0)),
                  pl.BlockSpec((1, d), lambda i: (0, 0)), pl.BlockSpec((1, d), lambda i: (0, 0)),
                  pl.BlockSpec(memory_space=pl.ANY)],
        out_specs=pl.BlockSpec((tm, d), lambda i: (i, 0)),
        out_shape=jax.ShapeDtypeStruct((n, d), F32),
        scratch_shapes=[pltpu.SMEM((2 * tm,), jnp.int32), pltpu.VMEM((2, tm, d), F32),
                        pltpu.SemaphoreType.DMA(()), pltpu.SemaphoreType.DMA(())],
        compiler_params=_cparams(("arbitrary",)),
        name="combine",
    )(dest.reshape(nt, 2 * tm), x, gw, ln_g.reshape(1, d), ln_b.reshape(1, d), ys)


def _moe(x, w_grp, b_grp, w_exp, b_exp, w_gate, w_up, w_down, layer, ln_g, ln_b):
    n, d = x.shape
    wr = jnp.zeros((d, LANE), F32).at[:, :N_EXPERTS].set(w_exp).at[:, N_EXPERTS:N_EXPERTS + N_GROUPS].set(w_grp)
    br = jnp.zeros((1, LANE), F32).at[0, :N_EXPERTS].set(b_exp).at[0, N_EXPERTS:N_EXPERTS + N_GROUPS].set(b_grp)
    ids, gw, cnt = _route(x, wr, br)
    bm = MOE_BM
    counts = cnt[0, :N_EXPERTS]
    padded = (counts + bm - 1) // bm * bm
    pend = jnp.cumsum(padded)
    pstart = pend - padded
    e, r = ids[:, 0:2], ids[:, 2:4]
    onehot = e[:, :, None] == jnp.arange(N_EXPERTS, dtype=jnp.int32)[None, None, :]
    dest = jnp.sum(jnp.where(onehot, pstart[None, None, :], 0), axis=-1) + r
    cap = 2 * n + N_EXPERTS * bm
    nb = cap // bm
    blk_start = jnp.arange(nb, dtype=jnp.int32) * bm
    blk_e = jnp.minimum(jnp.sum((pend[None, :] <= blk_start[:, None]).astype(jnp.int32), axis=1), N_EXPERTS - 1)
    n_used = (pend[-1:] // bm).astype(jnp.int32)
    xs = _dispatch(x, dest, pend.astype(jnp.int32), padded.astype(jnp.int32), cap)
    ys = _experts(xs, blk_e, n_used, w_gate, w_up, w_down, layer)
    return _combine(x, ys, dest, gw, ln_g, ln_b)


def _even_layer(x, batch, seq, layer_idx, w_in, dec_f, dec_b, lq1, lk1, lq2, lk2, subln, w_out, ln_g, ln_b):
    d = x.shape[1]
    w = HEADS * LANE
    z = _proj(x, w_in.astype(BF16), k=d)
    kw = dict(batch=batch, seq=seq)
    ret_seg = [(0, LANE)]
    q = _prep(z, 0, tables=_rope_tables(seq, ret_seg, RET_THETA, 1.0), sh=LANE // 2, **kw)
    k = _prep(z, 4, tables=_rope_tables(seq, ret_seg, RET_THETA, LANE ** -0.5), sh=LANE // 2, **kw)
    v = _prep(z, 8, **kw)
    decays = jnp.stack([dec_f, dec_b]).astype(F32)
    of, ob = _retention(q, k, v, decays, **kw)
    diff_seg = [(0, DIFF_ROT_DIM), (DIFF_HEAD_DIM, DIFF_ROT_DIM)]
    dq = _prep(z, 16, tables=_rope_tables(seq, diff_seg, ROPE_THETA, DIFF_HEAD_DIM ** -0.5 * LOG2E), sh=DIFF_ROT_DIM // 2, **kw)
    dk = _prep(z, 20, tables=_rope_tables(seq, diff_seg, ROPE_THETA, 1.0), sh=DIFF_ROT_DIM // 2, **kw)
    dvt = _prep(z, 24, transpose=True, tm=FLASH_TK_DIFF, **kw)
    lam_init = 0.8 - 0.6 * math.exp(-0.3 * layer_idx)
    diff = _flash(dq, dk, dvt, diff=(lq1, lk1, lq2, lk2, subln), lam_init=lam_init, **kw)
    wo = w_out.astype(BF16)
    return _outproj(x, diff, of, ob, z, 3, jnp.ones((LANE,), F32), wo[w:], wo[:w], ln_g, ln_b, group_norm=True)


def _odd_layer(x, batch, seq, w_in, q_norm, w_uq, kv_norm, w_ukv, w2_f, b_f, w2_b, b_b, gla_norm, w_out, ln_g, ln_b):
    d = x.shape[1]
    w = HEADS * LANE
    o = np.cumsum([0, MLA_Q_RANK, MLA_KV_RANK, MLA_ROPE, HEADS * GLA_K_DIM, HEADS * GLA_K_DIM, w, w,
                   GLA_GATE_RANK, GLA_GATE_RANK]).tolist()
    zeros = lambda c: jnp.zeros((d, c), F32)
    w_in2 = jnp.concatenate([
        w_in[:, o[0]:o[2]], zeros(MLA_NOPE), w_in[:, o[2]:o[3]], zeros(LANE - MLA_NOPE - MLA_ROPE),
        w_in[:, o[3]:o[7]], w_in[:, o[7]:o[9]], zeros(LANE - 2 * GLA_GATE_RANK)], axis=1).astype(BF16)
    z = _proj(x, w_in2, k=d, tm=512, tn=w_in2.shape[1])
    kw = dict(batch=batch, seq=seq)
    qd = MLA_NOPE + MLA_ROPE
    w_uq2 = jnp.pad(w_uq.reshape(MLA_Q_RANK, HEADS, qd), ((0, 0), (0, 0), (0, LANE - qd))).reshape(MLA_Q_RANK, w)
    ukv = w_ukv.reshape(MLA_KV_RANK, HEADS, MLA_NOPE + MLA_V)
    w_uk2 = jnp.pad(ukv[:, :, :MLA_NOPE], ((0, 0), (0, 0), (0, LANE - MLA_NOPE))).reshape(MLA_KV_RANK, w)
    w_uv2 = ukv[:, :, MLA_NOPE:].reshape(MLA_KV_RANK, w)
    qh = _proj(z, w_uq2.astype(BF16), k=MLA_Q_RANK, xcol=0, gain=q_norm)
    kvh = _proj(z, jnp.concatenate([w_uk2, w_uv2], axis=1).astype(BF16), k=MLA_KV_RANK, xcol=2, gain=kv_norm)
    rope_seg = [(MLA_NOPE, MLA_ROPE)]
    q = _prep(qh, 0, tables=_rope_tables(seq, rope_seg, ROPE_THETA, qd ** -0.5 * LOG2E), sh=MLA_ROPE // 2, **kw)
    k = _prep(z, 3, per_head=False, tables=_rope_tables(seq, rope_seg, ROPE_THETA, 1.0), sh=MLA_ROPE // 2,
              add=kvh, add_col0=0, **kw)
    vt = _prep(kvh, HEADS, transpose=True, tm=FLASH_TK, **kw)
    mla = _flash(q, k, vt, **kw)
    wk = HEADS * GLA_K_DIM
    pad_rows = lambda m, r0: jnp.zeros((LANE, wk), F32).at[r0:r0 + GLA_GATE_RANK].set(m).astype(BF16)
    of, ob = _gla(z, qcol=2, kcol=3, vcol=2, lcol=16,
                  w2f=pad_rows(w2_f, 0), bf=b_f.reshape(1, wk), w2b=pad_rows(w2_b, GLA_GATE_RANK), bb=b_b.reshape(1, wk), **kw)
    wo = w_out.astype(BF16)
    return _outproj(x, mla, of, ob, z, 3, gla_norm, wo[:w], wo[w:], ln_g, ln_b, group_norm=False)


def kernel(x, ev_w_in, ev_ret_decay_f, ev_ret_decay_b, ev_lq1, ev_lk1, ev_lq2, ev_lk2, ev_subln, ev_w_out, od_w_in, od_q_norm, od_w_uq, od_kv_norm, od_w_ukv, od_gla_w2_f, od_gla_b_f, od_gla_w2_b, od_gla_b_b, od_gla_norm, od_w_out, ln1_g, ln1_b, ln2_g, ln2_b, moe_w_grp, moe_b_grp, moe_w_exp, moe_b_exp, moe_w_gate, moe_w_up, moe_w_down):
    batch, seq, d = x.shape
    h = x.reshape(batch * seq, d)
    for i in range(DEPTH):
        j = i // 2
        if i % 2 == 0:
            h = _even_layer(h, batch, seq, i, ev_w_in[j], ev_ret_decay_f[j], ev_ret_decay_b[j], ev_lq1[j], ev_lk1[j],
                            ev_lq2[j], ev_lk2[j], ev_subln[j], ev_w_out[j], ln1_g[i], ln1_b[i])
        else:
            h = _odd_layer(h, batch, seq, od_w_in[j], od_q_norm[j], od_w_uq[j], od_kv_norm[j], od_w_ukv[j],
                           od_gla_w2_f[j], od_gla_b_f[j], od_gla_w2_b[j], od_gla_b_b[j], od_gla_norm[j], od_w_out[j],
                           ln1_g[i], ln1_b[i])
        h = _moe(h, moe_w_grp[i], moe_b_grp[i], moe_w_exp[i], moe_b_exp[i], moe_w_gate, moe_w_up, moe_w_down, i,
                 ln2_g[i], ln2_b[i])
    return h.reshape(batch, seq, d)
```

```python
import functools
import math

import numpy as np
import jax
import jax.numpy as jnp
from jax import lax
from jax.experimental import pallas as pl
from jax.experimental.pallas import tpu as pltpu

F32 = jnp.float32
BF16 = jnp.bfloat16

HEADS = 4
LANE = 128
RET_THETA = 10000.0
ROPE_THETA = 500000.0
DIFF_HEAD_DIM = 64
DIFF_ROT_DIM = 16
MLA_Q_RANK = 256
MLA_KV_RANK = 128
MLA_NOPE = 64
MLA_ROPE = 32
MLA_V = 128
GLA_K_DIM = 64
GLA_GATE_RANK = 16
GLA_TAU = 16.0
N_GROUPS = 4
EXPERTS_PER_GROUP = 8
N_EXPERTS = N_GROUPS * EXPERTS_PER_GROUP
DEPTH = 2
ALPHA = (2.0 * DEPTH) ** 0.25
LN_EPS = 1e-5
RMS_EPS = 1e-6

VMEM_LIMIT = 48 * 1024 * 1024


def _div_pow2(x, n):
    return lax.shift_right_logical(x, int(n).bit_length() - 1)


def _mod_pow2(x, n):
    return lax.bitwise_and(x, int(n) - 1)


def _cparams(sem):
    return pltpu.CompilerParams(dimension_semantics=sem, vmem_limit_bytes=VMEM_LIMIT)


def _proj_kernel(*refs, rms):
    if rms:
        x_ref, g_ref, w_ref, o_ref, xb_ref = refs
    else:
        x_ref, w_ref, o_ref, xb_ref = refs

    @pl.when(pl.program_id(1) == 0)
    def _():
        x = x_ref[...]
        if rms:
            x = x * lax.rsqrt(jnp.mean(x * x, axis=-1, keepdims=True) + RMS_EPS) * g_ref[...]
        xb_ref[...] = x.astype(BF16)

    o_ref[...] = jnp.dot(xb_ref[...], w_ref[...], preferred_element_type=F32).astype(o_ref.dtype)


def _proj(x, w, *, k, xcol=0, tm=1024, tn=512, gain=None):
    n = x.shape[0]
    m = w.shape[1]
    tm = min(tm, n)
    tn = min(tn, m)
    in_specs = [pl.BlockSpec((tm, k), lambda i, j: (i, xcol))]
    args = [x]
    if gain is not None:
        in_specs.append(pl.BlockSpec((1, k), lambda i, j: (0, 0)))
        args.append(gain.reshape(1, k))
    in_specs.append(pl.BlockSpec((k, tn), lambda i, j: (0, j)))
    args.append(w)
    return pl.pallas_call(
        functools.partial(_proj_kernel, rms=gain is not None),
        grid=(n // tm, m // tn),
        in_specs=in_specs,
        out_specs=pl.BlockSpec((tm, tn), lambda i, j: (i, j)),
        out_shape=jax.ShapeDtypeStruct((n, m), F32),
        scratch_shapes=[pltpu.VMEM((tm, k), BF16)],
        compiler_params=_cparams(("parallel", "arbitrary")),
        name="proj",
    )(*args)


def _prep_kernel(*refs, has_tab, has_add, sh, transpose):
    refs = list(refs)
    z_ref = refs.pop(0)
    add_ref = refs.pop(0) if has_add else None
    if has_tab:
        c_ref, sa_ref, sb_ref = refs.pop(0), refs.pop(0), refs.pop(0)
    o_ref = refs.pop(0)
    for h in range(HEADS):
        z = z_ref[:, h * LANE:(h + 1) * LANE] if z_ref.shape[1] > LANE else z_ref[...]
        if has_tab:
            out = (z * c_ref[...] + pltpu.roll(z, sh, axis=1) * sa_ref[...]
                   + pltpu.roll(z, LANE - sh, axis=1) * sb_ref[...])
        else:
            out = z
        if has_add:
            out = out + add_ref[:, h * LANE:(h + 1) * LANE]
        if transpose:
            o_ref[h, :LANE, :] = out.T.astype(o_ref.dtype)
            o_ref[h, LANE:, :] = jnp.ones((o_ref.shape[1] - LANE, o_ref.shape[2]), o_ref.dtype)
        else:
            o_ref[h] = out.astype(o_ref.dtype)


def _prep(z, col0, *, batch, seq, per_head=True, tables=None, sh=0, add=None, add_col0=0, transpose=False, tm=1024):
    tm = min(tm, seq // 2) if transpose else min(tm, seq)
    nt = seq // tm
    w = HEADS * LANE
    if per_head:
        in_specs = [pl.BlockSpec((tm, w), lambda b, i: (b * nt + i, col0 // HEADS))]
    else:
        in_specs = [pl.BlockSpec((tm, LANE), lambda b, i: (b * nt + i, col0))]
    args = [z]
    if add is not None:
        in_specs.append(pl.BlockSpec((tm, w), lambda b, i: (b * nt + i, add_col0 // HEADS)))
        args.append(add)
    if tables is not None:
        for t in tables:
            in_specs.append(pl.BlockSpec((tm, LANE), lambda b, i: (i, 0)))
            args.append(t)
    if transpose:
        out_shape = jax.ShapeDtypeStruct((batch, HEADS, nt, LANE + ONES_ROWS, tm), BF16)
        out_spec = pl.BlockSpec((None, HEADS, None, LANE + ONES_ROWS, tm), lambda b, i: (b, 0, i, 0, 0))
    else:
        out_shape = jax.ShapeDtypeStruct((batch, HEADS, seq, LANE), BF16)
        out_spec = pl.BlockSpec((None, HEADS, tm, LANE), lambda b, i: (b, 0, i, 0))
    return pl.pallas_call(
        functools.partial(_prep_kernel, has_tab=tables is not None, has_add=add is not None, sh=sh, transpose=transpose),
        grid=(batch, nt),
        in_specs=in_specs,
        out_specs=out_spec,
        out_shape=out_shape,
        compiler_params=_cparams(("parallel", "parallel")),
        name="prep",
    )(*args)


def _rope_tables(seq, segs, theta, scale):
    pos = jnp.arange(seq, dtype=F32)
    inv = jnp.zeros((LANE,), F32)
    lo = np.zeros((LANE,), bool)
    hi = np.zeros((LANE,), bool)
    for start, rot in segs:
        half = rot // 2
        f = jnp.power(jnp.float32(theta), -jnp.arange(0, rot, 2, dtype=F32) / rot)
        inv = inv.at[start:start + half].set(f).at[start + half:start + rot].set(f)
        lo[start:start + half] = True
        hi[start + half:start + rot] = True
    ang = pos[:, None] * inv[None, :]
    cos, sin = jnp.cos(ang), jnp.sin(ang)
    c = jnp.where(lo | hi, cos, 1.0) * scale
    sa = jnp.where(hi, sin, 0.0) * scale
    sb = jnp.where(lo, -sin, 0.0) * scale
    return c, sa, sb


ONES_ROWS = 16
LOG2E = math.log2(math.e)
FLASH_TK = 1024
FLASH_TK_DIFF = 512


def _flash_kernel(*refs, ncomp, nk, lam_init):
    if ncomp == 2:
        q_ref, k_ref, vt_ref, lq1, lk1, lq2, lk2, g_ref, o_ref, *scr = refs
    else:
        q_ref, k_ref, vt_ref, o_ref, *scr = refs
    qm_sc, m_sc, acc_sc, s0, s1, cm0, cm1, p0, p1, al0, al1 = scr
    tk = s0.shape[1]
    q = q_ref[...]
    if ncomp == 2:
        lane = lax.broadcasted_iota(jnp.int32, q.shape, 1)
        zero = jnp.zeros_like(q)
        qm_sc[0] = jnp.where(lane < DIFF_HEAD_DIM, q, zero)
        qm_sc[1] = jnp.where(lane >= DIFF_HEAD_DIM, q, zero)
    else:
        qm_sc[0] = q
    m_sc[...] = jnp.full(m_sc.shape, -jnp.inf, F32)
    acc_sc[...] = jnp.zeros(acc_sc.shape, F32)

    def scores(j, s_ref, cm_ref):
        k = k_ref[j * tk:(j + 1) * tk, :]
        for c in range(ncomp):
            s = lax.dot_general(k, qm_sc[c], (((1,), (1,)), ((), ())), preferred_element_type=F32)
            s_ref[c] = s
            cm_ref[c] = jnp.max(s, axis=0, keepdims=True)

    def softmax(s_ref, cm_ref, p_ref, al_ref):
        for c in range(ncomp):
            m_old = m_sc[c]
            m_new = jnp.maximum(m_old, cm_ref[c])
            al_ref[c] = jnp.exp2(m_old - m_new)
            p_ref[c] = jnp.exp2(s_ref[c] - m_new).astype(BF16)
            m_sc[c] = m_new

    def values(j, p_ref, al_ref):
        vt = vt_ref[j]
        for c in range(ncomp):
            acc_sc[c] = al_ref[c] * acc_sc[c] + jnp.dot(vt, p_ref[c], preferred_element_type=F32)

    bufs = ((s0, cm0, p0, al0), (s1, cm1, p1, al1))
    scores(0, s0, cm0)
    for j in range(nk):
        s_c, cm_c, p_c, al_c = bufs[j % 2]
        s_n, cm_n, p_n, al_n = bufs[(j + 1) % 2]
        if j + 1 < nk:
            scores(j + 1, s_n, cm_n)
        softmax(s_c, cm_c, p_c, al_c)
        if j >= 1:
            values(j - 1, p_n, al_n)
    values(nk - 1, *bufs[(nk - 1) % 2][2:])

    def normalised(c):
        acc = acc_sc[c]
        return acc[:LANE] / acc[LANE:LANE + 1]

    o = normalised(0)
    if ncomp == 2:
        lam = (jnp.exp(jnp.sum(lq1[...] * lk1[...], keepdims=True))
               - jnp.exp(jnp.sum(lq2[...] * lk2[...], keepdims=True)) + lam_init)
        o = o - lam * normalised(1)
        o = o * lax.rsqrt(jnp.mean(o * o, axis=0, keepdims=True) + RMS_EPS) * g_ref[...] * (1.0 - lam_init)
    o_ref[...] = o.T


def _flash(q, k, vt, *, batch, seq, tq=512, diff=None, lam_init=0.0):
    nk, vrows, tk = vt.shape[2], vt.shape[3], vt.shape[4]
    assert nk % 2 == 0 and vrows == LANE + ONES_ROWS
    tq = min(tq, seq)
    nq = seq // tq
    ncomp = 2 if diff is not None else 1
    in_specs = [
        pl.BlockSpec((None, None, tq, LANE), lambda b, h, i: (b, h, i, 0)),
        pl.BlockSpec((None, None, seq, LANE), lambda b, h, i: (b, h, 0, 0)),
        pl.BlockSpec((None, None, nk, vrows, tk), lambda b, h, i: (b, h, 0, 0, 0)),
    ]
    args = [q, k, vt]
    if diff is not None:
        lq1, lk1, lq2, lk2, subln = diff
        for v in (lq1, lk1, lq2, lk2):
            in_specs.append(pl.BlockSpec((1, DIFF_HEAD_DIM), lambda b, h, i: (0, 0)))
            args.append(v.reshape(1, DIFF_HEAD_DIM))
        in_specs.append(pl.BlockSpec((LANE, 1), lambda b, h, i: (0, 0)))
        args.append(subln.reshape(LANE, 1))
    return pl.pallas_call(
        functools.partial(_flash_kernel, ncomp=ncomp, nk=nk, lam_init=lam_init),
        grid=(batch, HEADS, nq),
        in_specs=in_specs,
        out_specs=pl.BlockSpec((tq, LANE), lambda b, h, i: (b * nq + i, h)),
        out_shape=jax.ShapeDtypeStruct((batch * seq, HEADS * LANE), F32),
        scratch_shapes=[pltpu.VMEM((ncomp, tq, LANE), BF16),
                        pltpu.VMEM((ncomp, 1, tq), F32), pltpu.VMEM((ncomp, vrows, tq), F32),
                        pltpu.VMEM((ncomp, tk, tq), F32), pltpu.VMEM((ncomp, tk, tq), F32),
                        pltpu.VMEM((ncomp, 1, tq), F32), pltpu.VMEM((ncomp, 1, tq), F32),
                        pltpu.VMEM((ncomp, tk, tq), BF16), pltpu.VMEM((ncomp, tk, tq), BF16),
                        pltpu.VMEM((ncomp, 1, tq), F32), pltpu.VMEM((ncomp, 1, tq), F32)],
        compiler_params=_cparams(("parallel", "parallel", "parallel")),
        name="flash_diff" if diff is not None else "flash_mla",
    )(*args)


def _ret_kernel(dec_ref, qf, kf, vf, qb, kb, vb, of_ref, ob_ref, s_sc, *, chunk):
    @pl.when(pl.program_id(1) == 0)
    def _():
        s_sc[...] = jnp.zeros(s_sc.shape, F32)

    ii = lax.broadcasted_iota(jnp.int32, (chunk, chunk), 0)
    jj = lax.broadcasted_iota(jnp.int32, (chunk, chunk), 1)
    r = lax.broadcasted_iota(jnp.int32, (chunk, 1), 0).astype(F32)
    for d, (q_ref, k_ref, v_ref, o_ref) in enumerate(((qf, kf, vf, of_ref), (qb, kb, vb, ob_ref))):
        for h in range(HEADS):
            la = -jnp.exp(jnp.full((1, 1), dec_ref[d, h], F32))
            if d == 0:
                mask, dist = ii >= jj, (ii - jj).astype(F32)
                qdec, kdec = jnp.exp(la * (r + 1.0)), jnp.exp(la * (chunk - 1.0 - r))
            else:
                mask, dist = jj > ii, (jj - ii).astype(F32)
                qdec, kdec = jnp.exp(la * (chunk - r)), jnp.exp(la * r)
            decay = jnp.where(mask, jnp.exp(jnp.where(mask, dist * la, 0.0)), 0.0)
            q, k, v = q_ref[h], k_ref[h], v_ref[h]
            s = lax.dot_general(q, k, (((1,), (1,)), ((), ())), preferred_element_type=F32)
            o = jnp.dot((s * decay).astype(BF16), v, preferred_element_type=F32)
            state = s_sc[d, h]
            o = o + qdec * jnp.dot(q, state.astype(BF16), preferred_element_type=F32)
            kd = (k.astype(F32) * kdec).astype(BF16)
            s_sc[d, h] = jnp.exp(la * float(chunk)) * state + lax.dot_general(
                kd, v, (((0,), (0,)), ((), ())), preferred_element_type=F32)
            o_ref[:, h * LANE:(h + 1) * LANE] = o


def _retention(q, k, v, decays, *, batch, seq, chunk=256):
    chunk = min(chunk, seq)
    n = seq // chunk
    fwd = pl.BlockSpec((None, HEADS, chunk, LANE), lambda b, c: (b, 0, c, 0))
    bwd = pl.BlockSpec((None, HEADS, chunk, LANE), lambda b, c: (b, 0, n - 1 - c, 0))
    w = HEADS * LANE
    out = jax.ShapeDtypeStruct((batch * seq, w), F32)
    return pl.pallas_call(
        functools.partial(_ret_kernel, chunk=chunk),
        grid=(batch, n),
        in_specs=[pl.BlockSpec(memory_space=pltpu.SMEM), fwd, fwd, fwd, bwd, bwd, bwd],
        out_specs=[pl.BlockSpec((chunk, w), lambda b, c: (b * n + c, 0)),
                   pl.BlockSpec((chunk, w), lambda b, c: (b * n + n - 1 - c, 0))],
        out_shape=[out, out],
        scratch_shapes=[pltpu.VMEM((2, HEADS, LANE, LANE), F32)],
        compiler_params=_cparams(("parallel", "arbitrary")),
        name="retention",
    )(decays, q, k, v, q, k, v)


GLA_SUB = 8


def _split3(x):
    x1 = x.astype(BF16)
    r1 = x - x1.astype(F32)
    x2 = r1.astype(BF16)
    x3 = (r1 - x2.astype(F32)).astype(BF16)
    return x1, x2, x3


def _gla_direction(q, k, v, lr, w2, bias, st, reverse):
    C, wk = q.shape
    wv = v.shape[1]
    dk, dv = wk // HEADS, wv // HEADS
    z = jnp.dot(lr.astype(BF16), w2, preferred_element_type=F32) + bias
    g = (jnp.minimum(z, 0.0) - jnp.log(1.0 + jnp.exp(-jnp.abs(z)))) * (1.0 / GLA_TAU)
    ii = lax.broadcasted_iota(jnp.int32, (C, C), 0)
    jj = lax.broadcasted_iota(jnp.int32, (C, C), 1)
    tri = jnp.where(ii >= jj, 1.0, 0.0).astype(BF16)
    b = sum(jnp.dot(tri, part, preferred_element_type=F32) for part in _split3(g))
    tot = b[C - 1:C, :]
    c = (tot - b + g) if reverse else b

    qe = (q * jnp.exp(jnp.minimum(c, 0.0))).astype(BF16)
    o = lax.dot_general(qe, st.astype(BF16), (((1,), (1,)), ((), ())), preferred_element_type=F32)
    ke = (k * jnp.exp(jnp.minimum(tot - c, 0.0))).astype(BF16)
    upd = lax.dot_general(v.astype(BF16), ke, (((0,), (0,)), ((), ())), preferred_element_type=F32)
    rr = _div_pow2(lax.broadcasted_iota(jnp.int32, (wv, wk), 0), dv)
    cc = _div_pow2(lax.broadcasted_iota(jnp.int32, (wv, wk), 1), dk)
    new_st = jnp.where(rr == cc, st * jnp.exp(tot) + upd, 0.0)

    lane_head = _div_pow2(lax.broadcasted_iota(jnp.int32, (C, wk), 1), dk)
    scores = [jnp.zeros((C, C), F32) for _ in range(HEADS)]
    hsz = C // 2
    while hsz >= GLA_SUB:
        blk = 2 * hsz
        rows = []
        for m in range(C // blk):
            rrow = m * blk + (hsz if reverse else hsz - 1)
            rows.append(jnp.broadcast_to(c[rrow:rrow + 1, :], (blk, wk)))
        ref = jnp.concatenate(rows, axis=0) if len(rows) > 1 else rows[0]
        qt = q * jnp.exp(jnp.minimum(c - ref, 0.0))
        kt = (k * jnp.exp(jnp.minimum(ref - c, 0.0))).astype(BF16)
        same = _div_pow2(ii, blk) == _div_pow2(jj, blk)
        if reverse:
            lvl = same & (_mod_pow2(ii, blk) < hsz) & (_mod_pow2(jj, blk) >= hsz)
        else:
            lvl = same & (_mod_pow2(ii, blk) >= hsz) & (_mod_pow2(jj, blk) < hsz)
        for h in range(HEADS):
            qh = jnp.where(lane_head == h, qt, 0.0).astype(BF16)
            s = lax.dot_general(qh, kt, (((1,), (1,)), ((), ())), preferred_element_type=F32)
            scores[h] = scores[h] + jnp.where(lvl, s, 0.0)
        hsz //= 2

    assert dv == C
    er = _div_pow2(lax.broadcasted_iota(jnp.int32, (wk, wv), 0), dk)
    ec = _div_pow2(lax.broadcasted_iota(jnp.int32, (wk, wv), 1), dv)
    expand = jnp.where(er == ec, 1.0, 0.0).astype(BF16)
    dist = (jj - ii) if reverse else (ii - jj)
    same_sub = _div_pow2(ii, GLA_SUB) == _div_pow2(jj, GLA_SUB)
    for lag in range(1 if reverse else 0, GLA_SUB):
        if lag == 0:
            t = q * k
        else:
            shift = (C - lag) if reverse else lag
            ks, cs = pltpu.roll(k, shift, axis=0), pltpu.roll(c, shift, axis=0)
            t = q * ks * jnp.exp(jnp.minimum(c - cs, 0.0))
        red = jnp.dot(t.astype(BF16), expand, preferred_element_type=F32)
        on_diag = same_sub & (dist == lag)
        for h in range(HEADS):
            scores[h] = scores[h] + jnp.where(on_diag, red[:, h * dv:(h + 1) * dv], 0.0)

    vb = v.astype(BF16)
    o = o + jnp.concatenate(
        [jnp.dot(scores[h].astype(BF16), vb[:, h * dv:(h + 1) * dv], preferred_element_type=F32) for h in range(HEADS)],
        axis=1)
    return o, new_st


def _gla_kernel(qf, kf, vf, lf, qb, kb, vb, lb, w2f, bf, w2b, bb, of_ref, ob_ref, s_sc, *, qscale):
    @pl.when(pl.program_id(1) == 0)
    def _():
        s_sc[...] = jnp.zeros(s_sc.shape, F32)

    o, st = _gla_direction(qf[...] * qscale, kf[...], vf[...], lf[...], w2f[...], bf[...], s_sc[0], False)
    of_ref[...] = o
    s_sc[0] = st
    o, st = _gla_direction(qb[...] * qscale, kb[...], vb[...], lb[...], w2b[...], bb[...], s_sc[1], True)
    ob_ref[...] = o
    s_sc[1] = st


def _gla(z, *, qcol, kcol, vcol, lcol, w2f, bf, w2b, bb, batch, seq, chunk=128):
    chunk = min(chunk, seq)
    n = seq // chunk
    wk, wv = HEADS * GLA_K_DIM, HEADS * LANE

    def specs(cmap):
        return [pl.BlockSpec((chunk, wk), lambda b, c: (cmap(b, c), qcol)),
                pl.BlockSpec((chunk, wk), lambda b, c: (cmap(b, c), kcol)),
                pl.BlockSpec((chunk, wv), lambda b, c: (cmap(b, c), vcol)),
                pl.BlockSpec((chunk, LANE), lambda b, c: (cmap(b, c), lcol))]

    fmap = lambda b, c: b * n + c
    bmap = lambda b, c: b * n + n - 1 - c
    wspec = [pl.BlockSpec((LANE, wk), lambda b, c: (0, 0)), pl.BlockSpec((1, wk), lambda b, c: (0, 0))]
    out = jax.ShapeDtypeStruct((batch * seq, wv), F32)
    return pl.pallas_call(
        functools.partial(_gla_kernel, qscale=GLA_K_DIM ** -0.5),
        grid=(batch, n),
        in_specs=specs(fmap) + specs(bmap) + wspec + wspec,
        out_specs=[pl.BlockSpec((chunk, wv), lambda b, c: (fmap(b, c), 0)),
                   pl.BlockSpec((chunk, wv), lambda b, c: (bmap(b, c), 0))],
        out_shape=[out, out],
        scratch_shapes=[pltpu.VMEM((2, wv, wk), F32)],
        compiler_params=_cparams(("parallel", "arbitrary")),
        name="gla",
    )(z, z, z, z, z, z, z, z, w2f, bf, w2b, bb)


def _layer_norm_rows(r, g, b):
    mu = jnp.mean(r, axis=-1, keepdims=True)
    d = r - mu
    var = jnp.mean(d * d, axis=-1, keepdims=True)
    return d * lax.rsqrt(var + LN_EPS) * g + b


def _outproj_kernel(x_ref, fin_ref, of_ref, ob_ref, gate_ref, ng_ref, wa_ref, wb_ref, lg_ref, lb_ref, o_ref, *, group_norm):
    lin = of_ref[...] + ob_ref[...]
    parts = []
    for h in range(HEADS):
        zh = lin[:, h * LANE:(h + 1) * LANE]
        if group_norm:
            mu = jnp.mean(zh, axis=-1, keepdims=True)
            dz = zh - mu
            parts.append(dz * lax.rsqrt(jnp.mean(dz * dz, axis=-1, keepdims=True) + LN_EPS))
        else:
            parts.append(zh * lax.rsqrt(jnp.mean(zh * zh, axis=-1, keepdims=True) + RMS_EPS) * ng_ref[...])
    gate = gate_ref[...]
    lin = jnp.concatenate(parts, axis=1) * (gate * jax.nn.sigmoid(gate))
    y = (jnp.dot(fin_ref[...].astype(BF16), wa_ref[...], preferred_element_type=F32)
         + jnp.dot(lin.astype(BF16), wb_ref[...], preferred_element_type=F32))
    o_ref[...] = _layer_norm_rows(ALPHA * x_ref[...] + y, lg_ref[...], lb_ref[...])


def _outproj(x, fin, of, ob, gate_src, gate_col, norm_gain, wa, wb, ln_g, ln_b, *, group_norm, tm=512):
    n, d = x.shape
    w = HEADS * LANE
    tm = min(tm, n)
    row = lambda i: (i, 0)
    const = lambda i: (0, 0)
    return pl.pallas_call(
        functools.partial(_outproj_kernel, group_norm=group_norm),
        grid=(n // tm,),
        in_specs=[pl.BlockSpec((tm, d), row), pl.BlockSpec((tm, w), row), pl.BlockSpec((tm, w), row),
                  pl.BlockSpec((tm, w), row), pl.BlockSpec((tm, w), lambda i: (i, gate_col)),
                  pl.BlockSpec((1, LANE), const), pl.BlockSpec((w, d), const), pl.BlockSpec((w, d), const),
                  pl.BlockSpec((1, d), const), pl.BlockSpec((1, d), const)],
        out_specs=pl.BlockSpec((tm, d), row),
        out_shape=jax.ShapeDtypeStruct((n, d), F32),
        compiler_params=_cparams(("parallel",)),
        name="outproj",
    )(x, fin, of, ob, gate_src, norm_gain.reshape(1, LANE), wa, wb, ln_g.reshape(1, d), ln_b.reshape(1, d))


def _route_kernel(x_ref, w_ref, b_ref, ids_ref, gw_ref, cnt_ref, run_sc):
    @pl.when(pl.program_id(0) == 0)
    def _():
        run_sc[...] = jnp.zeros(run_sc.shape, F32)

    tm = x_ref.shape[0]
    x = x_ref[...]
    xh = x.astype(BF16)
    xl = (x - xh.astype(F32)).astype(BF16)
    wh, wl = w_ref[0], w_ref[1]
    logits = (jnp.dot(xh, wh, preferred_element_type=F32) + jnp.dot(xh, wl, preferred_element_type=F32)
              + jnp.dot(xl, wh, preferred_element_type=F32)) + b_ref[...]
    lane = lax.broadcasted_iota(jnp.int32, logits.shape, 1)
    neg = -jnp.inf
    gmask = (lane >= N_EXPERTS) & (lane < N_EXPERTS + N_GROUPS)
    gl = jnp.where(gmask, logits, neg)
    gmax = jnp.max(gl, axis=1, keepdims=True)
    lane_f = lane.astype(F32)
    first = lambda hit: jnp.min(jnp.where(hit, lane_f, float(LANE)), axis=1, keepdims=True).astype(jnp.int32)
    gidx = first(gl == gmax) - N_EXPERTS
    p_grp = 1.0 / jnp.sum(jnp.where(gmask, jnp.exp(gl - gmax), 0.0), axis=1, keepdims=True)
    el = jnp.where(_div_pow2(lane, EXPERTS_PER_GROUP) == gidx, logits, neg)
    l1 = jnp.max(el, axis=1, keepdims=True)
    e1 = first(el == l1)
    el2 = jnp.where(lane == e1, neg, el)
    l2 = jnp.max(el2, axis=1, keepdims=True)
    e2 = first(el2 == l2)
    t = jnp.exp(l2 - l1)
    w1 = p_grp / (1.0 + t)
    w2 = p_grp * t / (1.0 + t)

    onehot = jnp.where(lane == e1, 1.0, jnp.where(lane == e2, 1.0, 0.0))
    ri = lax.broadcasted_iota(jnp.int32, (tm, tm), 0)
    ci = lax.broadcasted_iota(jnp.int32, (tm, tm), 1)
    before = jnp.dot(jnp.where(ri > ci, 1.0, 0.0).astype(BF16), onehot.astype(BF16), preferred_element_type=F32)
    before = before + run_sc[...]
    r1 = jnp.sum(jnp.where(lane == e1, before, 0.0), axis=1, keepdims=True).astype(jnp.int32)
    r2 = jnp.sum(jnp.where(lane == e2, before, 0.0), axis=1, keepdims=True).astype(jnp.int32)
    run_sc[...] = run_sc[...] + jnp.sum(onehot, axis=0, keepdims=True)
    cnt_ref[...] = run_sc[...].astype(jnp.int32)
    ids_ref[...] = jnp.where(lane == 0, e1, jnp.where(lane == 1, e2, jnp.where(lane == 2, r1, jnp.where(lane == 3, r2, 0))))
    gw_ref[...] = jnp.where(lane == 0, w1, jnp.where(lane == 1, w2, 0.0))


def _route(x, w, b, *, tm=512):
    n, d = x.shape
    tm = min(tm, n)
    w_hi = w.astype(BF16)
    w = jnp.stack([w_hi, (w - w_hi.astype(F32)).astype(BF16)])
    return pl.pallas_call(
        _route_kernel,
        grid=(n // tm,),
        in_specs=[pl.BlockSpec((tm, d), lambda i: (i, 0)), pl.BlockSpec((2, d, LANE), lambda i: (0, 0, 0)),
                  pl.BlockSpec((1, LANE), lambda i: (0, 0))],
        out_specs=[pl.BlockSpec((tm, LANE), lambda i: (i, 0)), pl.BlockSpec((tm, LANE), lambda i: (i, 0)),
                   pl.BlockSpec((1, LANE), lambda i: (0, 0))],
        out_shape=[jax.ShapeDtypeStruct((n, LANE), jnp.int32), jax.ShapeDtypeStruct((n, LANE), F32),
                   jax.ShapeDtypeStruct((1, LANE), jnp.int32)],
        scratch_shapes=[pltpu.VMEM((1, LANE), F32)],
        compiler_params=_cparams(("arbitrary",)),
        name="route",
    )(x, w, b)


MOE_BM = 256


def _dispatch_kernel(pend_ref, padded_ref, dest_ref, x_ref, xs_hbm, idx_smem, zbuf, sem_i, sem_z, sem):
    i = pl.program_id(0)
    tm = x_ref.shape[0]
    bm = zbuf.shape[0]

    @pl.when(i == 0)
    def _():
        zbuf[...] = jnp.zeros(zbuf.shape, F32)

        def tail(e):
            start_row = pl.multiple_of(pend_ref[e] - bm, bm)
            return pltpu.make_async_copy(zbuf, xs_hbm.at[pl.ds(start_row, bm), :], sem_z)

        def start(e, carry):
            @pl.when(padded_ref[e] > 0)
            def _():
                tail(e).start()
            return carry

        def wait(e, carry):
            @pl.when(padded_ref[e] > 0)
            def _():
                tail(e).wait()
            return carry

        lax.fori_loop(0, N_EXPERTS, start, 0)
        lax.fori_loop(0, N_EXPERTS, wait, 0)

        def unused(b):
            return pltpu.make_async_copy(zbuf, xs_hbm.at[pl.ds(pl.multiple_of(b * bm, bm), bm), :], sem_z)

        first_unused = pend_ref[N_EXPERTS - 1] // bm
        n_blocks = xs_hbm.shape[0] // bm
        lax.fori_loop(first_unused, n_blocks, lambda b, c: (unused(b).start(), c)[1], 0)
        lax.fori_loop(first_unused, n_blocks, lambda b, c: (unused(b).wait(), c)[1], 0)

    cp = pltpu.make_async_copy(dest_ref.at[i], idx_smem, sem_i)
    cp.start()
    cp.wait()

    def scatter(r, carry):
        for k in range(2):
            dst = idx_smem[2 * r + k]
            pltpu.make_async_copy(x_ref.at[pl.ds(r, 1), :], xs_hbm.at[pl.ds(dst, 1), :], sem).start()
        return carry

    lax.fori_loop(0, tm, scatter, 0, unroll=8)
    for k in range(2):
        pltpu.make_async_copy(x_ref, xs_hbm.at[pl.ds(0, tm), :], sem).wait()


def _dispatch(x, dest, pend, padded, cap, *, tm=512):
    n, d = x.shape
    tm = min(tm, n)
    nt = n // tm
    grid_spec = pltpu.PrefetchScalarGridSpec(
        num_scalar_prefetch=2,
        grid=(nt,),
        in_specs=[pl.BlockSpec((nt, 2 * tm), lambda i, pe, pa: (0, 0)),
                  pl.BlockSpec((tm, d), lambda i, pe, pa: (i, 0))],
        out_specs=pl.BlockSpec(memory_space=pl.ANY),
        scratch_shapes=[pltpu.SMEM((2 * tm,), jnp.int32), pltpu.VMEM((MOE_BM, d), F32),
                        pltpu.SemaphoreType.DMA(()), pltpu.SemaphoreType.DMA(()), pltpu.SemaphoreType.DMA(())],
    )
    return pl.pallas_call(
        _dispatch_kernel,
        grid_spec=grid_spec,
        out_shape=jax.ShapeDtypeStruct((cap, d), F32),
        compiler_params=_cparams(("arbitrary",)),
        name="dispatch",
    )(pend, padded, dest.reshape(nt, 2 * tm), x)


def _experts_kernel(blk_e_ref, nused_ref, xs_ref, wg_ref, wu_ref, wd_ref, ys_ref, wgb, wub, wdb):
    i = pl.program_id(0)

    @pl.when(i < nused_ref[0])
    def _():
        prev = blk_e_ref[jnp.maximum(i - 1, 0)]

        @pl.when(jnp.logical_or(i == 0, blk_e_ref[i] != prev))
        def _():
            wgb[...] = wg_ref[...].astype(BF16)
            wub[...] = wu_ref[...].astype(BF16)
            wdb[...] = wd_ref[...].astype(BF16)

        xb = xs_ref[...].astype(BF16)
        hg = jnp.dot(xb, wgb[...], preferred_element_type=F32)
        hu = jnp.dot(xb, wub[...], preferred_element_type=F32)
        hid = (hg * jax.nn.sigmoid(hg) * hu).astype(BF16)
        ys_ref[...] = jnp.dot(hid, wdb[...], preferred_element_type=F32)

    @pl.when(i >= nused_ref[0])
    def _():
        ys_ref[...] = jnp.zeros(ys_ref.shape, F32)


def _experts(xs, blk_e, n_used, w_gate, w_up, w_down, layer):
    cap, d = xs.shape
    bm = MOE_BM
    de = w_gate.shape[3]
    row_in = lambda i, be, nu: (jnp.minimum(i, nu[0] - 1), 0)
    row = lambda i, be, nu: (i, 0)
    grid_spec = pltpu.PrefetchScalarGridSpec(
        num_scalar_prefetch=2,
        grid=(cap // bm,),
        in_specs=[pl.BlockSpec((bm, d), row_in),
                  pl.BlockSpec((None, None, d, de), lambda i, be, nu: (layer, be[i], 0, 0)),
                  pl.BlockSpec((None, None, d, de), lambda i, be, nu: (layer, be[i], 0, 0)),
                  pl.BlockSpec((None, None, de, d), lambda i, be, nu: (layer, be[i], 0, 0))],
        out_specs=pl.BlockSpec((bm, d), row),
        scratch_shapes=[pltpu.VMEM((d, de), BF16), pltpu.VMEM((d, de), BF16), pltpu.VMEM((de, d), BF16)],
    )
    return pl.pallas_call(
        _experts_kernel,
        grid_spec=grid_spec,
        out_shape=jax.ShapeDtypeStruct((cap, d), F32),
        compiler_params=_cparams(("arbitrary",)),
        name="experts",
    )(blk_e, n_used, xs, w_gate, w_up, w_down)


def _combine_kernel(dest_ref, x_ref, gw_ref, g_ref, b_ref, ys_hbm, o_ref, idx_smem, ybuf, sem_i, sem):
    i = pl.program_id(0)
    tm = x_ref.shape[0]
    cp = pltpu.make_async_copy(dest_ref.at[i], idx_smem, sem_i)
    cp.start()
    cp.wait()

    def gather(r, carry):
        for k in range(2):
            src = idx_smem[2 * r + k]
            pltpu.make_async_copy(ys_hbm.at[pl.ds(src, 1), :], ybuf.at[k, pl.ds(r, 1), :], sem).start()
        return carry

    lax.fori_loop(0, tm, gather, 0, unroll=8)
    for k in range(2):
        pltpu.make_async_copy(ys_hbm.at[pl.ds(0, tm), :], ybuf.at[k], sem).wait()
    gw = gw_ref[...]
    ffn = ybuf[0] * gw[:, 0:1] + ybuf[1] * gw[:, 1:2]
    o_ref[...] = _layer_norm_rows(ALPHA * x_ref[...] + ffn, g_ref[...], b_ref[...])


def _combine(x, ys, dest, gw, ln_g, ln_b, *, tm=512):
    n, d = x.shape
    tm = min(tm, n)
    nt = n // tm
    return pl.pallas_call(
        _combine_kernel,
        grid=(nt,),
        in_specs=[pl.BlockSpec((nt, 2 * tm), lambda i: (0, 0)),
                  pl.BlockSpec((tm, d), lambda i: (i, 0)), pl.BlockSpec((tm, LANE), lambda i: (i, 0)),
                  pl.BlockSpec((1, d), lambda i: (0, 0)), pl.BlockSpec((1, d), lambda i: (0, 0)),
                  pl.BlockSpec(memory_space=pl.ANY)],
        out_specs=pl.BlockSpec((tm, d), lambda i: (i, 0)),
        out_shape=jax.ShapeDtypeStruct((n, d), F32),
        scratch_shapes=[pltpu.SMEM((2 * tm,), jnp.int32), pltpu.VMEM((2, tm, d), F32),
                        pltpu.SemaphoreType.DMA(()), pltpu.SemaphoreType.DMA(())],
        compiler_params=_cparams(("arbitrary",)),
        name="combine",
    )(dest.reshape(nt, 2 * tm), x, gw, ln_g.reshape(1, d), ln_b.reshape(1, d), ys)


def _moe(x, w_grp, b_grp, w_exp, b_exp, w_gate, w_up, w_down, layer, ln_g, ln_b):
    n, d = x.shape
    wr = jnp.zeros((d, LANE), F32).at[:, :N_EXPERTS].set(w_exp).at[:, N_EXPERTS:N_EXPERTS + N_GROUPS].set(w_grp)
    br = jnp.zeros((1, LANE), F32).at[0, :N_EXPERTS].set(b_exp).at[0, N_EXPERTS:N_EXPERTS + N_GROUPS].set(b_grp)
    ids, gw, cnt = _route(x, wr, br)
    bm = MOE_BM
    counts = cnt[0, :N_EXPERTS]
    padded = (counts + bm - 1) // bm * bm
    pend = jnp.cumsum(padded)
    pstart = pend - padded
    e, r = ids[:, 0:2], ids[:, 2:4]
    onehot = e[:, :, None] == jnp.arange(N_EXPERTS, dtype=jnp.int32)[None, None, :]
    dest = jnp.sum(jnp.where(onehot, pstart[None, None, :], 0), axis=-1) + r
    cap = 2 * n + N_EXPERTS * bm
    nb = cap // bm
    blk_start = jnp.arange(nb, dtype=jnp.int32) * bm
    blk_e = jnp.minimum(jnp.sum((pend[None, :] <= blk_start[:, None]).astype(jnp.int32), axis=1), N_EXPERTS - 1)
    n_used = (pend[-1:] // bm).astype(jnp.int32)
    xs = _dispatch(x, dest, pend.astype(jnp.int32), padded.astype(jnp.int32), cap)
    ys = _experts(xs, blk_e, n_used, w_gate, w_up, w_down, layer)
    return _combine(x, ys, dest, gw, ln_g, ln_b)


def _even_layer(x, batch, seq, layer_idx, w_in, dec_f, dec_b, lq1, lk1, lq2, lk2, subln, w_out, ln_g, ln_b):
    d = x.shape[1]
    w = HEADS * LANE
    z = _proj(x, w_in.astype(BF16), k=d, tm=2048)
    kw = dict(batch=batch, seq=seq)
    ret_seg = [(0, LANE)]
    q = _prep(z, 0, tables=_rope_tables(seq, ret_seg, RET_THETA, 1.0), sh=LANE // 2, **kw)
    k = _prep(z, 4, tables=_rope_tables(seq, ret_seg, RET_THETA, LANE ** -0.5), sh=LANE // 2, **kw)
    v = _prep(z, 8, **kw)
    decays = jnp.stack([dec_f, dec_b]).astype(F32)
    of, ob = _retention(q, k, v, decays, **kw)
    diff_seg = [(0, DIFF_ROT_DIM), (DIFF_HEAD_DIM, DIFF_ROT_DIM)]
    dq = _prep(z, 16, tables=_rope_tables(seq, diff_seg, ROPE_THETA, DIFF_HEAD_DIM ** -0.5 * LOG2E), sh=DIFF_ROT_DIM // 2, **kw)
    dk = _prep(z, 20, tables=_rope_tables(seq, diff_seg, ROPE_THETA, 1.0), sh=DIFF_ROT_DIM // 2, **kw)
    dvt = _prep(z, 24, transpose=True, tm=FLASH_TK_DIFF, **kw)
    lam_init = 0.8 - 0.6 * math.exp(-0.3 * layer_idx)
    diff = _flash(dq, dk, dvt, diff=(lq1, lk1, lq2, lk2, subln), lam_init=lam_init, **kw)
    wo = w_out.astype(BF16)
    return _outproj(x, diff, of, ob, z, 3, jnp.ones((LANE,), F32), wo[w:], wo[:w], ln_g, ln_b, group_norm=True)


def _odd_layer(x, batch, seq, w_in, q_norm, w_uq, kv_norm, w_ukv, w2_f, b_f, w2_b, b_b, gla_norm, w_out, ln_g, ln_b):
    d = x.shape[1]
    w = HEADS * LANE
    o = np.cumsum([0, MLA_Q_RANK, MLA_KV_RANK, MLA_ROPE, HEADS * GLA_K_DIM, HEADS * GLA_K_DIM, w, w,
                   GLA_GATE_RANK, GLA_GATE_RANK]).tolist()
    zeros = lambda c: jnp.zeros((d, c), F32)
    w_in2 = jnp.concatenate([
        w_in[:, o[0]:o[2]], zeros(MLA_NOPE), w_in[:, o[2]:o[3]], zeros(LANE - MLA_NOPE - MLA_ROPE),
        w_in[:, o[3]:o[7]], w_in[:, o[7]:o[9]], zeros(LANE - 2 * GLA_GATE_RANK)], axis=1).astype(BF16)
    z = _proj(x, w_in2, k=d, tm=512, tn=w_in2.shape[1])
    kw = dict(batch=batch, seq=seq)
    qd = MLA_NOPE + MLA_ROPE
    w_uq2 = jnp.pad(w_uq.reshape(MLA_Q_RANK, HEADS, qd), ((0, 0), (0, 0), (0, LANE - qd))).reshape(MLA_Q_RANK, w)
    ukv = w_ukv.reshape(MLA_KV_RANK, HEADS, MLA_NOPE + MLA_V)
    w_uk2 = jnp.pad(ukv[:, :, :MLA_NOPE], ((0, 0), (0, 0), (0, LANE - MLA_NOPE))).reshape(MLA_KV_RANK, w)
    w_uv2 = ukv[:, :, MLA_NOPE:].reshape(MLA_KV_RANK, w)
    qh = _proj(z, w_uq2.astype(BF16), k=MLA_Q_RANK, xcol=0, gain=q_norm)
    kvh = _proj(z, jnp.concatenate([w_uk2, w_uv2], axis=1).astype(BF16), k=MLA_KV_RANK, xcol=2, gain=kv_norm)
    rope_seg = [(MLA_NOPE, MLA_ROPE)]
    q = _prep(qh, 0, tables=_rope_tables(seq, rope_seg, ROPE_THETA, qd ** -0.5 * LOG2E), sh=MLA_ROPE // 2, **kw)
    k = _prep(z, 3, per_head=False, tables=_rope_tables(seq, rope_seg, ROPE_THETA, 1.0), sh=MLA_ROPE // 2,
              add=kvh, add_col0=0, **kw)
    vt = _prep(kvh, HEADS, transpose=True, tm=FLASH_TK, **kw)
    mla = _flash(q, k, vt, **kw)
    wk = HEADS * GLA_K_DIM
    pad_rows = lambda m, r0: jnp.zeros((LANE, wk), F32).at[r0:r0 + GLA_GATE_RANK].set(m).astype(BF16)
    of, ob = _gla(z, qcol=2, kcol=3, vcol=2, lcol=16,
                  w2f=pad_rows(w2_f, 0), bf=b_f.reshape(1, wk), w2b=pad_rows(w2_b, GLA_GATE_RANK), bb=b_b.reshape(1, wk), **kw)
    wo = w_out.astype(BF16)
    return _outproj(x, mla, of, ob, z, 3, gla_norm, wo[:w], wo[w:], ln_g, ln_b, group_norm=False)


def kernel(x, ev_w_in, ev_ret_decay_f, ev_ret_decay_b, ev_lq1, ev_lk1, ev_lq2, ev_lk2, ev_subln, ev_w_out, od_w_in, od_q_norm, od_w_uq, od_kv_norm, od_w_ukv, od_gla_w2_f, od_gla_b_f, od_gla_w2_b, od_gla_b_b, od_gla_norm, od_w_out, ln1_g, ln1_b, ln2_g, ln2_b, moe_w_grp, moe_b_grp, moe_w_exp, moe_b_exp, moe_w_gate, moe_w_up, moe_w_down):
    batch, seq, d = x.shape
    h = x.reshape(batch * seq, d)
    for i in range(DEPTH):
        j = i // 2
        if i % 2 == 0:
            h = _even_layer(h, batch, seq, i, ev_w_in[j], ev_ret_decay_f[j], ev_ret_decay_b[j], ev_lq1[j], ev_lk1[j],
                            ev_lq2[j], ev_lk2[j], ev_subln[j], ev_w_out[j], ln1_g[i], ln1_b[i])
        else:
            h = _odd_layer(h, batch, seq, od_w_in[j], od_q_norm[j], od_w_uq[j], od_kv_norm[j], od_w_ukv[j],
                           od_gla_w2_f[j], od_gla_b_f[j], od_gla_w2_b[j], od_gla_b_b[j], od_gla_norm[j], od_w_out[j],
                           ln1_g[i], ln1_b[i])
        h = _moe(h, moe_w_grp[i], moe_b_grp[i], moe_w_exp[i], moe_b_exp[i], moe_w_gate, moe_w_up, moe_w_down, i,
                 ln2_g[i], ln2_b[i])
    return h.reshape(batch, seq, d)
```

```python
import functools
import math

import numpy as np
import jax
import jax.numpy as jnp
from jax import lax
from jax.experimental import pallas as pl
from jax.experimental.pallas import tpu as pltpu

F32 = jnp.float32
BF16 = jnp.bfloat16

HEADS = 4
LANE = 128
RET_THETA = 10000.0
ROPE_THETA = 500000.0
DIFF_HEAD_DIM = 64
DIFF_ROT_DIM = 16
MLA_Q_RANK = 256
MLA_KV_RANK = 128
MLA_NOPE = 64
MLA_ROPE = 32
MLA_V = 128
GLA_K_DIM = 64
GLA_GATE_RANK = 16
GLA_TAU = 16.0
N_GROUPS = 4
EXPERTS_PER_GROUP = 8
N_EXPERTS = N_GROUPS * EXPERTS_PER_GROUP
DEPTH = 2
ALPHA = (2.0 * DEPTH) ** 0.25
LN_EPS = 1e-5
RMS_EPS = 1e-6

VMEM_LIMIT = 48 * 1024 * 1024


def _div_pow2(x, n):
    return lax.shift_right_logical(x, int(n).bit_length() - 1)


def _mod_pow2(x, n):
    return lax.bitwise_and(x, int(n) - 1)


def _cparams(sem):
    return pltpu.CompilerParams(dimension_semantics=sem, vmem_limit_bytes=VMEM_LIMIT)


def _proj_kernel(*refs, rms):
    if rms:
        x_ref, g_ref, w_ref, o_ref, xb_ref = refs
    else:
        x_ref, w_ref, o_ref, xb_ref = refs

    @pl.when(pl.program_id(1) == 0)
    def _():
        x = x_ref[...]
        if rms:
            x = x * lax.rsqrt(jnp.mean(x * x, axis=-1, keepdims=True) + RMS_EPS) * g_ref[...]
        xb_ref[...] = x.astype(BF16)

    o_ref[...] = jnp.dot(xb_ref[...], w_ref[...], preferred_element_type=F32).astype(o_ref.dtype)


def _proj(x, w, *, k, xcol=0, tm=1024, tn=512, gain=None):
    n = x.shape[0]
    m = w.shape[1]
    tm = min(tm, n)
    tn = min(tn, m)
    in_specs = [pl.BlockSpec((tm, k), lambda i, j: (i, xcol))]
    args = [x]
    if gain is not None:
        in_specs.append(pl.BlockSpec((1, k), lambda i, j: (0, 0)))
        args.append(gain.reshape(1, k))
    in_specs.append(pl.BlockSpec((k, tn), lambda i, j: (0, j)))
    args.append(w)
    return pl.pallas_call(
        functools.partial(_proj_kernel, rms=gain is not None),
        grid=(n // tm, m // tn),
        in_specs=in_specs,
        out_specs=pl.BlockSpec((tm, tn), lambda i, j: (i, j)),
        out_shape=jax.ShapeDtypeStruct((n, m), F32),
        scratch_shapes=[pltpu.VMEM((tm, k), BF16)],
        compiler_params=_cparams(("parallel", "arbitrary")),
        name="proj",
    )(*args)


def _prep_kernel(*refs, has_tab, has_add, sh, transpose):
    refs = list(refs)
    z_ref = refs.pop(0)
    add_ref = refs.pop(0) if has_add else None
    if has_tab:
        c_ref, sa_ref, sb_ref = refs.pop(0), refs.pop(0), refs.pop(0)
    o_ref = refs.pop(0)
    for h in range(HEADS):
        z = z_ref[:, h * LANE:(h + 1) * LANE] if z_ref.shape[1] > LANE else z_ref[...]
        if has_tab:
            out = (z * c_ref[...] + pltpu.roll(z, sh, axis=1) * sa_ref[...]
                   + pltpu.roll(z, LANE - sh, axis=1) * sb_ref[...])
        else:
            out = z
        if has_add:
            out = out + add_ref[:, h * LANE:(h + 1) * LANE]
        if transpose:
            o_ref[h, :LANE, :] = out.T.astype(o_ref.dtype)
            o_ref[h, LANE:, :] = jnp.ones((o_ref.shape[1] - LANE, o_ref.shape[2]), o_ref.dtype)
        else:
            o_ref[h] = out.astype(o_ref.dtype)


def _prep(z, col0, *, batch, seq, per_head=True, tables=None, sh=0, add=None, add_col0=0, transpose=False, tm=1024):
    tm = min(tm, seq // 2) if transpose else min(tm, seq)
    nt = seq // tm
    w = HEADS * LANE
    if per_head:
        in_specs = [pl.BlockSpec((tm, w), lambda b, i: (b * nt + i, col0 // HEADS))]
    else:
        in_specs = [pl.BlockSpec((tm, LANE), lambda b, i: (b * nt + i, col0))]
    args = [z]
    if add is not None:
        in_specs.append(pl.BlockSpec((tm, w), lambda b, i: (b * nt + i, add_col0 // HEADS)))
        args.append(add)
    if tables is not None:
        for t in tables:
            in_specs.append(pl.BlockSpec((tm, LANE), lambda b, i: (i, 0)))
            args.append(t)
    if transpose:
        out_shape = jax.ShapeDtypeStruct((batch, HEADS, nt, LANE + ONES_ROWS, tm), BF16)
        out_spec = pl.BlockSpec((None, HEADS, None, LANE + ONES_ROWS, tm), lambda b, i: (b, 0, i, 0, 0))
    else:
        out_shape = jax.ShapeDtypeStruct((batch, HEADS, seq, LANE), BF16)
        out_spec = pl.BlockSpec((None, HEADS, tm, LANE), lambda b, i: (b, 0, i, 0))
    return pl.pallas_call(
        functools.partial(_prep_kernel, has_tab=tables is not None, has_add=add is not None, sh=sh, transpose=transpose),
        grid=(batch, nt),
        in_specs=in_specs,
        out_specs=out_spec,
        out_shape=out_shape,
        compiler_params=_cparams(("parallel", "parallel")),
        name="prep",
    )(*args)


def _rope_heads(z, tabs, sh, scale):
    c, sa, sb = tabs
    outs = []
    for h in range(HEADS):
        zh = z[:, h * LANE:(h + 1) * LANE]
        outs.append((zh * c + pltpu.roll(zh, sh, axis=1) * sa + pltpu.roll(zh, LANE - sh, axis=1) * sb) * scale)
    return outs


def _even_in_kernel(x_ref, w_ref, rc, rsa, rsb, dc, dsa, dsb, q_ref, k_ref, v_ref, g_ref, dq_ref, dk_ref, dvt_ref):
    w = HEADS * LANE
    xb = x_ref[...].astype(BF16)
    part = lambda t: jnp.dot(xb, w_ref[:, t * w:(t + 1) * w], preferred_element_type=F32)
    ret_t = (rc[...], rsa[...], rsb[...])
    diff_t = (dc[...], dsa[...], dsb[...])
    for h, o in enumerate(_rope_heads(part(0), ret_t, LANE // 2, 1.0)):
        q_ref[h] = o.astype(BF16)
    for h, o in enumerate(_rope_heads(part(1), ret_t, LANE // 2, LANE ** -0.5)):
        k_ref[h] = o.astype(BF16)
    rv = part(2)
    for h in range(HEADS):
        v_ref[h] = rv[:, h * LANE:(h + 1) * LANE].astype(BF16)
    g_ref[...] = part(3)
    for h, o in enumerate(_rope_heads(part(4), diff_t, DIFF_ROT_DIM // 2, DIFF_HEAD_DIM ** -0.5 * LOG2E)):
        dq_ref[h] = o.astype(BF16)
    for h, o in enumerate(_rope_heads(part(5), diff_t, DIFF_ROT_DIM // 2, 1.0)):
        dk_ref[h] = o.astype(BF16)
    dv = part(6)
    for h in range(HEADS):
        dvt_ref[h, :LANE, :] = dv[:, h * LANE:(h + 1) * LANE].T.astype(BF16)
        dvt_ref[h, LANE:, :] = jnp.ones((ONES_ROWS, dvt_ref.shape[2]), BF16)


def _even_in(x, w, ret_tabs, diff_tabs, *, batch, seq):
    n, d = x.shape
    tm = min(FLASH_TK_DIFF, seq // 2)
    nt = seq // tm
    hw = HEADS * LANE
    heads = jax.ShapeDtypeStruct((batch, HEADS, seq, LANE), BF16)
    head_spec = pl.BlockSpec((None, HEADS, tm, LANE), lambda i: (i // nt, 0, i % nt, 0))
    tab_spec = pl.BlockSpec((tm, LANE), lambda i: (i % nt, 0))
    return pl.pallas_call(
        _even_in_kernel,
        grid=(n // tm,),
        in_specs=[pl.BlockSpec((tm, d), lambda i: (i, 0)), pl.BlockSpec((d, 7 * hw), lambda i: (0, 0))] + [tab_spec] * 6,
        out_specs=[head_spec, head_spec, head_spec, pl.BlockSpec((tm, hw), lambda i: (i, 0)), head_spec, head_spec,
                   pl.BlockSpec((None, HEADS, None, LANE + ONES_ROWS, tm), lambda i: (i // nt, 0, i % nt, 0, 0))],
        out_shape=[heads, heads, heads, jax.ShapeDtypeStruct((n, hw), F32), heads, heads,
                   jax.ShapeDtypeStruct((batch, HEADS, nt, LANE + ONES_ROWS, tm), BF16)],
        compiler_params=_cparams(("parallel",)),
        name="even_in",
    )(x, w, *ret_tabs, *diff_tabs)


def _rms_rows(z, g):
    return z * lax.rsqrt(jnp.mean(z * z, axis=-1, keepdims=True) + RMS_EPS) * g


def _odd_in_kernel(x_ref, w_ref, qn_ref, wq_ref, kvn_ref, wkv_ref, tc, tsa, tsb, q_ref, k_ref, vt_ref, zg_ref):
    hw = HEADS * LANE
    mla_w = MLA_Q_RANK + MLA_KV_RANK + LANE
    xb = x_ref[...].astype(BF16)
    zg_ref[...] = jnp.dot(xb, w_ref[:, mla_w:], preferred_element_type=F32)
    z1 = jnp.dot(xb, w_ref[:, :mla_w], preferred_element_type=F32)
    tabs = (tc[...], tsa[...], tsb[...])
    sh = MLA_ROPE // 2
    qh = jnp.dot(_rms_rows(z1[:, :MLA_Q_RANK], qn_ref[...]).astype(BF16), wq_ref[...], preferred_element_type=F32)
    for h, o in enumerate(_rope_heads(qh, tabs, sh, (MLA_NOPE + MLA_ROPE) ** -0.5 * LOG2E)):
        q_ref[h] = o.astype(BF16)
    ckv = _rms_rows(z1[:, MLA_Q_RANK:MLA_Q_RANK + MLA_KV_RANK], kvn_ref[...]).astype(BF16)
    kv = jnp.dot(ckv, wkv_ref[...], preferred_element_type=F32)
    kr = z1[:, MLA_Q_RANK + MLA_KV_RANK:]
    kr = kr * tabs[0] + pltpu.roll(kr, sh, axis=1) * tabs[1] + pltpu.roll(kr, LANE - sh, axis=1) * tabs[2]
    for h in range(HEADS):
        k_ref[h] = (kv[:, h * LANE:(h + 1) * LANE] + kr).astype(BF16)
        vt_ref[h, :LANE, :] = kv[:, hw + h * LANE:hw + (h + 1) * LANE].T.astype(BF16)
        vt_ref[h, LANE:, :] = jnp.ones((ONES_ROWS, vt_ref.shape[2]), BF16)


def _odd_in(x, w, q_norm, w_uq, kv_norm, w_ukv, tabs, *, batch, seq, tm=512):
    n, d = x.shape
    tm = min(tm, seq // 2)
    nt = seq // tm
    tk = min(FLASH_TK, seq // 2)
    per = tk // tm
    hw = HEADS * LANE
    gw_ = w.shape[1] - (MLA_Q_RANK + MLA_KV_RANK + LANE)
    heads = jax.ShapeDtypeStruct((batch, HEADS, seq, LANE), BF16)
    head_spec = pl.BlockSpec((None, HEADS, tm, LANE), lambda i: (i // nt, 0, i % nt, 0))
    tab_spec = pl.BlockSpec((tm, LANE), lambda i: (i % nt, 0))
    const = lambda i: (0, 0)
    return pl.pallas_call(
        _odd_in_kernel,
        grid=(n // tm,),
        in_specs=[pl.BlockSpec((tm, d), lambda i: (i, 0)), pl.BlockSpec(w.shape, const),
                  pl.BlockSpec((1, MLA_Q_RANK), const), pl.BlockSpec(w_uq.shape, const),
                  pl.BlockSpec((1, MLA_KV_RANK), const), pl.BlockSpec(w_ukv.shape, const)] + [tab_spec] * 3,
        out_specs=[head_spec, head_spec,
                   pl.BlockSpec((None, HEADS, None, LANE + ONES_ROWS, tm),
                                lambda i: (i // nt, 0, (i % nt) // per, 0, (i % nt) % per)),
                   pl.BlockSpec((tm, gw_), lambda i: (i, 0))],
        out_shape=[heads, heads, jax.ShapeDtypeStruct((batch, HEADS, seq // tk, LANE + ONES_ROWS, tk), BF16),
                   jax.ShapeDtypeStruct((n, gw_), F32)],
        compiler_params=_cparams(("parallel",)),
        name="odd_in",
    )(x, w, q_norm.reshape(1, -1), w_uq, kv_norm.reshape(1, -1), w_ukv, *tabs)


def _rope_tables(seq, segs, theta, scale):
    pos = jnp.arange(seq, dtype=F32)
    inv = jnp.zeros((LANE,), F32)
    lo = np.zeros((LANE,), bool)
    hi = np.zeros((LANE,), bool)
    for start, rot in segs:
        half = rot // 2
        f = jnp.power(jnp.float32(theta), -jnp.arange(0, rot, 2, dtype=F32) / rot)
        inv = inv.at[start:start + half].set(f).at[start + half:start + rot].set(f)
        lo[start:start + half] = True
        hi[start + half:start + rot] = True
    ang = pos[:, None] * inv[None, :]
    cos, sin = jnp.cos(ang), jnp.sin(ang)
    c = jnp.where(lo | hi, cos, 1.0) * scale
    sa = jnp.where(hi, sin, 0.0) * scale
    sb = jnp.where(lo, -sin, 0.0) * scale
    return c, sa, sb


ONES_ROWS = 16
LOG2E = math.log2(math.e)
FLASH_TK = 1024
FLASH_TK_DIFF = 512


def _flash_kernel(*refs, ncomp, nk, lam_init):
    if ncomp == 2:
        q_ref, k_ref, vt_ref, lq1, lk1, lq2, lk2, g_ref, o_ref, *scr = refs
    else:
        q_ref, k_ref, vt_ref, o_ref, *scr = refs
    qm_sc, m_sc, acc_sc, s0, s1, cm0, cm1, p0, p1, al0, al1 = scr
    tk = s0.shape[1]
    q = q_ref[...]
    if ncomp == 2:
        lane = lax.broadcasted_iota(jnp.int32, q.shape, 1)
        zero = jnp.zeros_like(q)
        qm_sc[0] = jnp.where(lane < DIFF_HEAD_DIM, q, zero)
        qm_sc[1] = jnp.where(lane >= DIFF_HEAD_DIM, q, zero)
    else:
        qm_sc[0] = q
    m_sc[...] = jnp.full(m_sc.shape, -jnp.inf, F32)
    acc_sc[...] = jnp.zeros(acc_sc.shape, F32)

    def scores(j, s_ref, cm_ref):
        k = k_ref[j * tk:(j + 1) * tk, :]
        for c in range(ncomp):
            s = lax.dot_general(k, qm_sc[c], (((1,), (1,)), ((), ())), preferred_element_type=F32)
            s_ref[c] = s
            cm_ref[c] = jnp.max(s, axis=0, keepdims=True)

    def softmax(s_ref, cm_ref, p_ref, al_ref):
        for c in range(ncomp):
            m_old = m_sc[c]
            m_new = jnp.maximum(m_old, cm_ref[c])
            al_ref[c] = jnp.exp2(m_old - m_new)
            p_ref[c] = jnp.exp2(s_ref[c] - m_new).astype(BF16)
            m_sc[c] = m_new

    def values(j, p_ref, al_ref):
        vt = vt_ref[j]
        for c in range(ncomp):
            acc_sc[c] = al_ref[c] * acc_sc[c] + jnp.dot(vt, p_ref[c], preferred_element_type=F32)

    bufs = ((s0, cm0, p0, al0), (s1, cm1, p1, al1))
    scores(0, s0, cm0)
    for j in range(nk):
        s_c, cm_c, p_c, al_c = bufs[j % 2]
        s_n, cm_n, p_n, al_n = bufs[(j + 1) % 2]
        if j + 1 < nk:
            scores(j + 1, s_n, cm_n)
        softmax(s_c, cm_c, p_c, al_c)
        if j >= 1:
            values(j - 1, p_n, al_n)
    values(nk - 1, *bufs[(nk - 1) % 2][2:])

    def normalised(c):
        acc = acc_sc[c]
        return acc[:LANE] / acc[LANE:LANE + 1]

    o = normalised(0)
    if ncomp == 2:
        lam = (jnp.exp(jnp.sum(lq1[...] * lk1[...], keepdims=True))
               - jnp.exp(jnp.sum(lq2[...] * lk2[...], keepdims=True)) + lam_init)
        o = o - lam * normalised(1)
        o = o * lax.rsqrt(jnp.mean(o * o, axis=0, keepdims=True) + RMS_EPS) * g_ref[...] * (1.0 - lam_init)
    o_ref[...] = o.T


def _flash(q, k, vt, *, batch, seq, tq=512, diff=None, lam_init=0.0):
    nk, vrows, tk = vt.shape[2], vt.shape[3], vt.shape[4]
    assert nk % 2 == 0 and vrows == LANE + ONES_ROWS
    tq = min(tq, seq)
    nq = seq // tq
    ncomp = 2 if diff is not None else 1
    in_specs = [
        pl.BlockSpec((None, None, tq, LANE), lambda b, h, i: (b, h, i, 0)),
        pl.BlockSpec((None, None, seq, LANE), lambda b, h, i: (b, h, 0, 0)),
        pl.BlockSpec((None, None, nk, vrows, tk), lambda b, h, i: (b, h, 0, 0, 0)),
    ]
    args = [q, k, vt]
    if diff is not None:
        lq1, lk1, lq2, lk2, subln = diff
        for v in (lq1, lk1, lq2, lk2):
            in_specs.append(pl.BlockSpec((1, DIFF_HEAD_DIM), lambda b, h, i: (0, 0)))
            args.append(v.reshape(1, DIFF_HEAD_DIM))
        in_specs.append(pl.BlockSpec((LANE, 1), lambda b, h, i: (0, 0)))
        args.append(subln.reshape(LANE, 1))
    return pl.pallas_call(
        functools.partial(_flash_kernel, ncomp=ncomp, nk=nk, lam_init=lam_init),
        grid=(batch, HEADS, nq),
        in_specs=in_specs,
        out_specs=pl.BlockSpec((tq, LANE), lambda b, h, i: (b * nq + i, h)),
        out_shape=jax.ShapeDtypeStruct((batch * seq, HEADS * LANE), F32),
        scratch_shapes=[pltpu.VMEM((ncomp, tq, LANE), BF16),
                        pltpu.VMEM((ncomp, 1, tq), F32), pltpu.VMEM((ncomp, vrows, tq), F32),
                        pltpu.VMEM((ncomp, tk, tq), F32), pltpu.VMEM((ncomp, tk, tq), F32),
                        pltpu.VMEM((ncomp, 1, tq), F32), pltpu.VMEM((ncomp, 1, tq), F32),
                        pltpu.VMEM((ncomp, tk, tq), BF16), pltpu.VMEM((ncomp, tk, tq), BF16),
                        pltpu.VMEM((ncomp, 1, tq), F32), pltpu.VMEM((ncomp, 1, tq), F32)],
        compiler_params=_cparams(("parallel", "parallel", "parallel")),
        name="flash_diff" if diff is not None else "flash_mla",
    )(*args)


def _ret_kernel(dec_ref, qf, kf, vf, qb, kb, vb, of_ref, ob_ref, s_sc, *, chunk):
    @pl.when(pl.program_id(1) == 0)
    def _():
        s_sc[...] = jnp.zeros(s_sc.shape, F32)

    ii = lax.broadcasted_iota(jnp.int32, (chunk, chunk), 0)
    jj = lax.broadcasted_iota(jnp.int32, (chunk, chunk), 1)
    r = lax.broadcasted_iota(jnp.int32, (chunk, 1), 0).astype(F32)
    for d, (q_ref, k_ref, v_ref, o_ref) in enumerate(((qf, kf, vf, of_ref), (qb, kb, vb, ob_ref))):
        for h in range(HEADS):
            la = -jnp.exp(jnp.full((1, 1), dec_ref[d, h], F32))
            if d == 0:
                mask, dist = ii >= jj, (ii - jj).astype(F32)
                qdec, kdec = jnp.exp(la * (r + 1.0)), jnp.exp(la * (chunk - 1.0 - r))
            else:
                mask, dist = jj > ii, (jj - ii).astype(F32)
                qdec, kdec = jnp.exp(la * (chunk - r)), jnp.exp(la * r)
            decay = jnp.where(mask, jnp.exp(jnp.where(mask, dist * la, 0.0)), 0.0)
            q, k, v = q_ref[h], k_ref[h], v_ref[h]
            s = lax.dot_general(q, k, (((1,), (1,)), ((), ())), preferred_element_type=F32)
            o = jnp.dot((s * decay).astype(BF16), v, preferred_element_type=F32)
            state = s_sc[d, h]
            o = o + qdec * jnp.dot(q, state.astype(BF16), preferred_element_type=F32)
            kd = (k.astype(F32) * kdec).astype(BF16)
            s_sc[d, h] = jnp.exp(la * float(chunk)) * state + lax.dot_general(
                kd, v, (((0,), (0,)), ((), ())), preferred_element_type=F32)
            o_ref[:, h * LANE:(h + 1) * LANE] = o


def _retention(q, k, v, decays, *, batch, seq, chunk=256):
    chunk = min(chunk, seq)
    n = seq // chunk
    fwd = pl.BlockSpec((None, HEADS, chunk, LANE), lambda b, c: (b, 0, c, 0))
    bwd = pl.BlockSpec((None, HEADS, chunk, LANE), lambda b, c: (b, 0, n - 1 - c, 0))
    w = HEADS * LANE
    out = jax.ShapeDtypeStruct((batch * seq, w), F32)
    return pl.pallas_call(
        functools.partial(_ret_kernel, chunk=chunk),
        grid=(batch, n),
        in_specs=[pl.BlockSpec(memory_space=pltpu.SMEM), fwd, fwd, fwd, bwd, bwd, bwd],
        out_specs=[pl.BlockSpec((chunk, w), lambda b, c: (b * n + c, 0)),
                   pl.BlockSpec((chunk, w), lambda b, c: (b * n + n - 1 - c, 0))],
        out_shape=[out, out],
        scratch_shapes=[pltpu.VMEM((2, HEADS, LANE, LANE), F32)],
        compiler_params=_cparams(("parallel", "arbitrary")),
        name="retention",
    )(decays, q, k, v, q, k, v)


GLA_SUB = 8


def _split3(x):
    x1 = x.astype(BF16)
    r1 = x - x1.astype(F32)
    x2 = r1.astype(BF16)
    x3 = (r1 - x2.astype(F32)).astype(BF16)
    return x1, x2, x3


def _gla_direction(q, k, v, lr, w2, bias, st, reverse):
    C, wk = q.shape
    wv = v.shape[1]
    dk, dv = wk // HEADS, wv // HEADS
    z = jnp.dot(lr.astype(BF16), w2, preferred_element_type=F32) + bias
    g = (jnp.minimum(z, 0.0) - jnp.log(1.0 + jnp.exp(-jnp.abs(z)))) * (1.0 / GLA_TAU)
    ii = lax.broadcasted_iota(jnp.int32, (C, C), 0)
    jj = lax.broadcasted_iota(jnp.int32, (C, C), 1)
    tri = jnp.where(ii >= jj, 1.0, 0.0).astype(BF16)
    b = sum(jnp.dot(tri, part, preferred_element_type=F32) for part in _split3(g))
    tot = b[C - 1:C, :]
    c = (tot - b + g) if reverse else b

    qe = (q * jnp.exp(jnp.minimum(c, 0.0))).astype(BF16)
    o = lax.dot_general(qe, st.astype(BF16), (((1,), (1,)), ((), ())), preferred_element_type=F32)
    ke = (k * jnp.exp(jnp.minimum(tot - c, 0.0))).astype(BF16)
    upd = lax.dot_general(v.astype(BF16), ke, (((0,), (0,)), ((), ())), preferred_element_type=F32)
    rr = _div_pow2(lax.broadcasted_iota(jnp.int32, (wv, wk), 0), dv)
    cc = _div_pow2(lax.broadcasted_iota(jnp.int32, (wv, wk), 1), dk)
    new_st = jnp.where(rr == cc, st * jnp.exp(tot) + upd, 0.0)

    lane_head = _div_pow2(lax.broadcasted_iota(jnp.int32, (C, wk), 1), dk)
    scores = [jnp.zeros((C, C), F32) for _ in range(HEADS)]
    hsz = C // 2
    while hsz >= GLA_SUB:
        blk = 2 * hsz
        rows = []
        for m in range(C // blk):
            rrow = m * blk + (hsz if reverse else hsz - 1)
            rows.append(jnp.broadcast_to(c[rrow:rrow + 1, :], (blk, wk)))
        ref = jnp.concatenate(rows, axis=0) if len(rows) > 1 else rows[0]
        qt = q * jnp.exp(jnp.minimum(c - ref, 0.0))
        kt = (k * jnp.exp(jnp.minimum(ref - c, 0.0))).astype(BF16)
        same = _div_pow2(ii, blk) == _div_pow2(jj, blk)
        if reverse:
            lvl = same & (_mod_pow2(ii, blk) < hsz) & (_mod_pow2(jj, blk) >= hsz)
        else:
            lvl = same & (_mod_pow2(ii, blk) >= hsz) & (_mod_pow2(jj, blk) < hsz)
        for h in range(HEADS):
            qh = jnp.where(lane_head == h, qt, 0.0).astype(BF16)
            s = lax.dot_general(qh, kt, (((1,), (1,)), ((), ())), preferred_element_type=F32)
            scores[h] = scores[h] + jnp.where(lvl, s, 0.0)
        hsz //= 2

    assert dv == C
    er = _div_pow2(lax.broadcasted_iota(jnp.int32, (wk, wv), 0), dk)
    ec = _div_pow2(lax.broadcasted_iota(jnp.int32, (wk, wv), 1), dv)
    expand = jnp.where(er == ec, 1.0, 0.0).astype(BF16)
    dist = (jj - ii) if reverse else (ii - jj)
    same_sub = _div_pow2(ii, GLA_SUB) == _div_pow2(jj, GLA_SUB)
    for lag in range(1 if reverse else 0, GLA_SUB):
        if lag == 0:
            t = q * k
        else:
            shift = (C - lag) if reverse else lag
            ks, cs = pltpu.roll(k, shift, axis=0), pltpu.roll(c, shift, axis=0)
            t = q * ks * jnp.exp(jnp.minimum(c - cs, 0.0))
        red = jnp.dot(t.astype(BF16), expand, preferred_element_type=F32)
        on_diag = same_sub & (dist == lag)
        for h in range(HEADS):
            scores[h] = scores[h] + jnp.where(on_diag, red[:, h * dv:(h + 1) * dv], 0.0)

    vb = v.astype(BF16)
    o = o + jnp.concatenate(
        [jnp.dot(scores[h].astype(BF16), vb[:, h * dv:(h + 1) * dv], preferred_element_type=F32) for h in range(HEADS)],
        axis=1)
    return o, new_st


def _gla_kernel(qf, kf, vf, lf, qb, kb, vb, lb, w2f, bf, w2b, bb, of_ref, ob_ref, s_sc, *, qscale):
    @pl.when(pl.program_id(1) == 0)
    def _():
        s_sc[...] = jnp.zeros(s_sc.shape, F32)

    o, st = _gla_direction(qf[...] * qscale, kf[...], vf[...], lf[...], w2f[...], bf[...], s_sc[0], False)
    of_ref[...] = o
    s_sc[0] = st
    o, st = _gla_direction(qb[...] * qscale, kb[...], vb[...], lb[...], w2b[...], bb[...], s_sc[1], True)
    ob_ref[...] = o
    s_sc[1] = st


def _gla(z, *, qcol, kcol, vcol, lcol, w2f, bf, w2b, bb, batch, seq, chunk=128):
    chunk = min(chunk, seq)
    n = seq // chunk
    wk, wv = HEADS * GLA_K_DIM, HEADS * LANE

    def specs(cmap):
        return [pl.BlockSpec((chunk, wk), lambda b, c: (cmap(b, c), qcol)),
                pl.BlockSpec((chunk, wk), lambda b, c: (cmap(b, c), kcol)),
                pl.BlockSpec((chunk, wv), lambda b, c: (cmap(b, c), vcol)),
                pl.BlockSpec((chunk, LANE), lambda b, c: (cmap(b, c), lcol))]

    fmap = lambda b, c: b * n + c
    bmap = lambda b, c: b * n + n - 1 - c
    wspec = [pl.BlockSpec((LANE, wk), lambda b, c: (0, 0)), pl.BlockSpec((1, wk), lambda b, c: (0, 0))]
    out = jax.ShapeDtypeStruct((batch * seq, wv), F32)
    return pl.pallas_call(
        functools.partial(_gla_kernel, qscale=GLA_K_DIM ** -0.5),
        grid=(batch, n),
        in_specs=specs(fmap) + specs(bmap) + wspec + wspec,
        out_specs=[pl.BlockSpec((chunk, wv), lambda b, c: (fmap(b, c), 0)),
                   pl.BlockSpec((chunk, wv), lambda b, c: (bmap(b, c), 0))],
        out_shape=[out, out],
        scratch_shapes=[pltpu.VMEM((2, wv, wk), F32)],
        compiler_params=_cparams(("parallel", "arbitrary")),
        name="gla",
    )(z, z, z, z, z, z, z, z, w2f, bf, w2b, bb)


def _layer_norm_rows(r, g, b):
    mu = jnp.mean(r, axis=-1, keepdims=True)
    d = r - mu
    var = jnp.mean(d * d, axis=-1, keepdims=True)
    return d * lax.rsqrt(var + LN_EPS) * g + b


def _outproj_kernel(x_ref, fin_ref, of_ref, ob_ref, gate_ref, ng_ref, wa_ref, wb_ref, lg_ref, lb_ref, o_ref, *, group_norm):
    lin = of_ref[...] + ob_ref[...]
    parts = []
    for h in range(HEADS):
        zh = lin[:, h * LANE:(h + 1) * LANE]
        if group_norm:
            mu = jnp.mean(zh, axis=-1, keepdims=True)
            dz = zh - mu
            parts.append(dz * lax.rsqrt(jnp.mean(dz * dz, axis=-1, keepdims=True) + LN_EPS))
        else:
            parts.append(zh * lax.rsqrt(jnp.mean(zh * zh, axis=-1, keepdims=True) + RMS_EPS) * ng_ref[...])
    gate = gate_ref[...]
    lin = jnp.concatenate(parts, axis=1) * (gate * jax.nn.sigmoid(gate))
    y = (jnp.dot(fin_ref[...].astype(BF16), wa_ref[...], preferred_element_type=F32)
         + jnp.dot(lin.astype(BF16), wb_ref[...], preferred_element_type=F32))
    o_ref[...] = _layer_norm_rows(ALPHA * x_ref[...] + y, lg_ref[...], lb_ref[...])


def _outproj(x, fin, of, ob, gate_src, gate_col, norm_gain, wa, wb, ln_g, ln_b, *, group_norm, tm=512):
    n, d = x.shape
    w = HEADS * LANE
    tm = min(tm, n)
    row = lambda i: (i, 0)
    const = lambda i: (0, 0)
    return pl.pallas_call(
        functools.partial(_outproj_kernel, group_norm=group_norm),
        grid=(n // tm,),
        in_specs=[pl.BlockSpec((tm, d), row), pl.BlockSpec((tm, w), row), pl.BlockSpec((tm, w), row),
                  pl.BlockSpec((tm, w), row), pl.BlockSpec((tm, w), lambda i: (i, gate_col)),
                  pl.BlockSpec((1, LANE), const), pl.BlockSpec((w, d), const), pl.BlockSpec((w, d), const),
                  pl.BlockSpec((1, d), const), pl.BlockSpec((1, d), const)],
        out_specs=pl.BlockSpec((tm, d), row),
        out_shape=jax.ShapeDtypeStruct((n, d), F32),
        compiler_params=_cparams(("parallel",)),
        name="outproj",
    )(x, fin, of, ob, gate_src, norm_gain.reshape(1, LANE), wa, wb, ln_g.reshape(1, d), ln_b.reshape(1, d))


def _route_kernel(x_ref, w_ref, b_ref, ids_ref, gw_ref, cnt_ref, run_sc):
    @pl.when(pl.program_id(0) == 0)
    def _():
        run_sc[...] = jnp.zeros(run_sc.shape, F32)

    tm = x_ref.shape[0]
    x = x_ref[...]
    xh = x.astype(BF16)
    xl = (x - xh.astype(F32)).astype(BF16)
    wh, wl = w_ref[0], w_ref[1]
    logits = (jnp.dot(xh, wh, preferred_element_type=F32) + jnp.dot(xh, wl, preferred_element_type=F32)
              + jnp.dot(xl, wh, preferred_element_type=F32)) + b_ref[...]
    lane = lax.broadcasted_iota(jnp.int32, logits.shape, 1)
    neg = -jnp.inf
    gmask = (lane >= N_EXPERTS) & (lane < N_EXPERTS + N_GROUPS)
    gl = jnp.where(gmask, logits, neg)
    gmax = jnp.max(gl, axis=1, keepdims=True)
    lane_f = lane.astype(F32)
    first = lambda hit: jnp.min(jnp.where(hit, lane_f, float(LANE)), axis=1, keepdims=True).astype(jnp.int32)
    gidx = first(gl == gmax) - N_EXPERTS
    p_grp = 1.0 / jnp.sum(jnp.where(gmask, jnp.exp(gl - gmax), 0.0), axis=1, keepdims=True)
    el = jnp.where(_div_pow2(lane, EXPERTS_PER_GROUP) == gidx, logits, neg)
    l1 = jnp.max(el, axis=1, keepdims=True)
    e1 = first(el == l1)
    el2 = jnp.where(lane == e1, neg, el)
    l2 = jnp.max(el2, axis=1, keepdims=True)
    e2 = first(el2 == l2)
    t = jnp.exp(l2 - l1)
    w1 = p_grp / (1.0 + t)
    w2 = p_grp * t / (1.0 + t)

    onehot = jnp.where(lane == e1, 1.0, jnp.where(lane == e2, 1.0, 0.0))
    ri = lax.broadcasted_iota(jnp.int32, (tm, tm), 0)
    ci = lax.broadcasted_iota(jnp.int32, (tm, tm), 1)
    before = jnp.dot(jnp.where(ri > ci, 1.0, 0.0).astype(BF16), onehot.astype(BF16), preferred_element_type=F32)
    before = before + run_sc[...]
    r1 = jnp.sum(jnp.where(lane == e1, before, 0.0), axis=1, keepdims=True).astype(jnp.int32)
    r2 = jnp.sum(jnp.where(lane == e2, before, 0.0), axis=1, keepdims=True).astype(jnp.int32)
    run_sc[...] = run_sc[...] + jnp.sum(onehot, axis=0, keepdims=True)
    cnt_ref[...] = run_sc[...].astype(jnp.int32)
    ids_ref[...] = jnp.where(lane == 0, e1, jnp.where(lane == 1, e2, jnp.where(lane == 2, r1, jnp.where(lane == 3, r2, 0))))
    gw_ref[...] = jnp.where(lane == 0, w1, jnp.where(lane == 1, w2, 0.0))


def _route(x, w, b, *, tm=512):
    n, d = x.shape
    tm = min(tm, n)
    w_hi = w.astype(BF16)
    w = jnp.stack([w_hi, (w - w_hi.astype(F32)).astype(BF16)])
    return pl.pallas_call(
        _route_kernel,
        grid=(n // tm,),
        in_specs=[pl.BlockSpec((tm, d), lambda i: (i, 0)), pl.BlockSpec((2, d, LANE), lambda i: (0, 0, 0)),
                  pl.BlockSpec((1, LANE), lambda i: (0, 0))],
        out_specs=[pl.BlockSpec((tm, LANE), lambda i: (i, 0)), pl.BlockSpec((tm, LANE), lambda i: (i, 0)),
                   pl.BlockSpec((1, LANE), lambda i: (0, 0))],
        out_shape=[jax.ShapeDtypeStruct((n, LANE), jnp.int32), jax.ShapeDtypeStruct((n, LANE), F32),
                   jax.ShapeDtypeStruct((1, LANE), jnp.int32)],
        scratch_shapes=[pltpu.VMEM((1, LANE), F32)],
        compiler_params=_cparams(("arbitrary",)),
        name="route",
    )(x, w, b)


MOE_BM = 256
SUBL = 8


def _rows_from_linear(ref, rows):
    return jnp.concatenate([ref[pl.ds(s, rows, stride=SUBL), :] for s in range(SUBL)], axis=1)


def _rows_to_linear(ref, val):
    for s in range(SUBL):
        ref[pl.ds(s, val.shape[0], stride=SUBL), :] = val[:, s * LANE:(s + 1) * LANE]


def _dispatch_kernel(pend_ref, padded_ref, dest_ref, x_ref, xs_hbm, idx_smem, zbuf, lin, sem_i, sem_z, sem):
    i = pl.program_id(0)
    tm = x_ref.shape[0]
    bm = zbuf.shape[0] // SUBL

    @pl.when(i == 0)
    def _():
        zbuf[...] = jnp.zeros(zbuf.shape, F32)

        def tail(e):
            start_row = pl.multiple_of((pend_ref[e] - bm) * SUBL, bm * SUBL)
            return pltpu.make_async_copy(zbuf, xs_hbm.at[pl.ds(start_row, bm * SUBL), :], sem_z)

        def start(e, carry):
            @pl.when(padded_ref[e] > 0)
            def _():
                tail(e).start()
            return carry

        def wait(e, carry):
            @pl.when(padded_ref[e] > 0)
            def _():
                tail(e).wait()
            return carry

        lax.fori_loop(0, N_EXPERTS, start, 0)
        lax.fori_loop(0, N_EXPERTS, wait, 0)

        def unused(b):
            start_row = pl.multiple_of(b * bm * SUBL, bm * SUBL)
            return pltpu.make_async_copy(zbuf, xs_hbm.at[pl.ds(start_row, bm * SUBL), :], sem_z)

        first_unused = pend_ref[N_EXPERTS - 1] // bm
        n_blocks = xs_hbm.shape[0] // (bm * SUBL)
        lax.fori_loop(first_unused, n_blocks, lambda b, c: (unused(b).start(), c)[1], 0)
        lax.fori_loop(first_unused, n_blocks, lambda b, c: (unused(b).wait(), c)[1], 0)

    cp = pltpu.make_async_copy(dest_ref.at[i], idx_smem, sem_i)
    cp.start()
    _rows_to_linear(lin, x_ref[...])
    cp.wait()

    def scatter(r, carry):
        src = lin.at[pl.ds(pl.multiple_of(r * SUBL, SUBL), SUBL), :]
        for k in range(2):
            dst = pl.multiple_of(idx_smem[2 * r + k] * SUBL, SUBL)
            pltpu.make_async_copy(src, xs_hbm.at[pl.ds(dst, SUBL), :], sem).start()
        return carry

    lax.fori_loop(0, tm, scatter, 0, unroll=8)
    for k in range(2):
        pltpu.make_async_copy(lin, xs_hbm.at[pl.ds(0, tm * SUBL), :], sem).wait()


def _dispatch(x, dest, pend, padded, cap, *, tm=512):
    n, d = x.shape
    assert d == SUBL * LANE
    tm = min(tm, n)
    nt = n // tm
    grid_spec = pltpu.PrefetchScalarGridSpec(
        num_scalar_prefetch=2,
        grid=(nt,),
        in_specs=[pl.BlockSpec((nt, 2 * tm), lambda i, pe, pa: (0, 0)),
                  pl.BlockSpec((tm, d), lambda i, pe, pa: (i, 0))],
        out_specs=pl.BlockSpec(memory_space=pl.ANY),
        scratch_shapes=[pltpu.SMEM((2 * tm,), jnp.int32), pltpu.VMEM((MOE_BM * SUBL, LANE), F32),
                        pltpu.VMEM((tm * SUBL, LANE), F32),
                        pltpu.SemaphoreType.DMA(()), pltpu.SemaphoreType.DMA(()), pltpu.SemaphoreType.DMA(())],
    )
    return pl.pallas_call(
        _dispatch_kernel,
        grid_spec=grid_spec,
        out_shape=jax.ShapeDtypeStruct((cap * SUBL, LANE), F32),
        compiler_params=_cparams(("arbitrary",)),
        name="dispatch",
    )(pend, padded, dest.reshape(nt, 2 * tm), x)


def _experts_kernel(blk_e_ref, nused_ref, xs_ref, wg_ref, wu_ref, wd_ref, ys_ref, wgb, wub, wdb):
    i = pl.program_id(0)

    @pl.when(i < nused_ref[0])
    def _():
        prev = blk_e_ref[jnp.maximum(i - 1, 0)]

        @pl.when(jnp.logical_or(i == 0, blk_e_ref[i] != prev))
        def _():
            wgb[...] = wg_ref[...].astype(BF16)
            wub[...] = wu_ref[...].astype(BF16)
            wdb[...] = wd_ref[...].astype(BF16)

        xb = _rows_from_linear(xs_ref, xs_ref.shape[0] // SUBL).astype(BF16)
        hg = jnp.dot(xb, wgb[...], preferred_element_type=F32)
        hu = jnp.dot(xb, wub[...], preferred_element_type=F32)
        hid = (hg * jax.nn.sigmoid(hg) * hu).astype(BF16)
        _rows_to_linear(ys_ref, jnp.dot(hid, wdb[...], preferred_element_type=F32))

    @pl.when(i >= nused_ref[0])
    def _():
        ys_ref[...] = jnp.zeros(ys_ref.shape, F32)


def _experts(xs, blk_e, n_used, w_gate, w_up, w_down, layer):
    cap = xs.shape[0] // SUBL
    bm = MOE_BM
    d, de = w_gate.shape[2], w_gate.shape[3]
    row_in = lambda i, be, nu: (jnp.minimum(i, nu[0] - 1), 0)
    row = lambda i, be, nu: (i, 0)
    grid_spec = pltpu.PrefetchScalarGridSpec(
        num_scalar_prefetch=2,
        grid=(cap // bm,),
        in_specs=[pl.BlockSpec((bm * SUBL, LANE), row_in),
                  pl.BlockSpec((None, None, d, de), lambda i, be, nu: (layer, be[i], 0, 0)),
                  pl.BlockSpec((None, None, d, de), lambda i, be, nu: (layer, be[i], 0, 0)),
                  pl.BlockSpec((None, None, de, d), lambda i, be, nu: (layer, be[i], 0, 0))],
        out_specs=pl.BlockSpec((bm * SUBL, LANE), row),
        scratch_shapes=[pltpu.VMEM((d, de), BF16), pltpu.VMEM((d, de), BF16), pltpu.VMEM((de, d), BF16)],
    )
    return pl.pallas_call(
        _experts_kernel,
        grid_spec=grid_spec,
        out_shape=jax.ShapeDtypeStruct((cap * SUBL, LANE), F32),
        compiler_params=_cparams(("arbitrary",)),
        name="experts",
    )(blk_e, n_used, xs, w_gate, w_up, w_down)


def _combine_kernel(dest_ref, x_ref, gw_ref, g_ref, b_ref, ys_hbm, o_ref, idx_smem, ybuf, sem_i, sem):
    i = pl.program_id(0)
    tm = x_ref.shape[0]
    cp = pltpu.make_async_copy(dest_ref.at[i], idx_smem, sem_i)
    cp.start()
    cp.wait()

    def gather(r, carry):
        row = pl.multiple_of(r * SUBL, SUBL)
        for k in range(2):
            src = pl.multiple_of(idx_smem[2 * r + k] * SUBL, SUBL)
            pltpu.make_async_copy(ys_hbm.at[pl.ds(src, SUBL), :], ybuf.at[k, pl.ds(row, SUBL), :], sem).start()
        return carry

    lax.fori_loop(0, tm, gather, 0, unroll=8)
    for k in range(2):
        pltpu.make_async_copy(ys_hbm.at[pl.ds(0, tm * SUBL), :], ybuf.at[k], sem).wait()
    gw = gw_ref[...]
    ffn = _rows_from_linear(ybuf.at[0], tm) * gw[:, 0:1] + _rows_from_linear(ybuf.at[1], tm) * gw[:, 1:2]
    o_ref[...] = _layer_norm_rows(ALPHA * x_ref[...] + ffn, g_ref[...], b_ref[...])


def _combine(x, ys, dest, gw, ln_g, ln_b, *, tm=512):
    n, d = x.shape
    tm = min(tm, n)
    nt = n // tm
    return pl.pallas_call(
        _combine_kernel,
        grid=(nt,),
        in_specs=[pl.BlockSpec((nt, 2 * tm), lambda i: (0, 0)),
                  pl.BlockSpec((tm, d), lambda i: (i, 0)), pl.BlockSpec((tm, LANE), lambda i: (i, 0)),
                  pl.BlockSpec((1, d), lambda i: (0, 0)), pl.BlockSpec((1, d), lambda i: (0, 0)),
                  pl.BlockSpec(memory_space=pl.ANY)],
        out_specs=pl.BlockSpec((tm, d), lambda i: (i, 0)),
        out_shape=jax.ShapeDtypeStruct((n, d), F32),
        scratch_shapes=[pltpu.SMEM((2 * tm,), jnp.int32), pltpu.VMEM((2, tm * SUBL, LANE), F32),
                        pltpu.SemaphoreType.DMA(()), pltpu.SemaphoreType.DMA(())],
        compiler_params=_cparams(("arbitrary",)),
        name="combine",
    )(dest.reshape(nt, 2 * tm), x, gw, ln_g.reshape(1, d), ln_b.reshape(1, d), ys)


def _moe(x, w_grp, b_grp, w_exp, b_exp, w_gate, w_up, w_down, layer, ln_g, ln_b):
    n, d = x.shape
    wr = jnp.zeros((d, LANE), F32).at[:, :N_EXPERTS].set(w_exp).at[:, N_EXPERTS:N_EXPERTS + N_GROUPS].set(w_grp)
    br = jnp.zeros((1, LANE), F32).at[0, :N_EXPERTS].set(b_exp).at[0, N_EXPERTS:N_EXPERTS + N_GROUPS].set(b_grp)
    ids, gw, cnt = _route(x, wr, br)
    bm = MOE_BM
    counts = cnt[0, :N_EXPERTS]
    padded = (counts + bm - 1) // bm * bm
    pend = jnp.cumsum(padded)
    pstart = pend - padded
    e, r = ids[:, 0:2], ids[:, 2:4]
    onehot = e[:, :, None] == jnp.arange(N_EXPERTS, dtype=jnp.int32)[None, None, :]
    dest = jnp.sum(jnp.where(onehot, pstart[None, None, :], 0), axis=-1) + r
    cap = 2 * n + N_EXPERTS * bm
    nb = cap // bm
    blk_start = jnp.arange(nb, dtype=jnp.int32) * bm
    blk_e = jnp.minimum(jnp.sum((pend[None, :] <= blk_start[:, None]).astype(jnp.int32), axis=1), N_EXPERTS - 1)
    n_used = (pend[-1:] // bm).astype(jnp.int32)
    xs = _dispatch(x, dest, pend.astype(jnp.int32), padded.astype(jnp.int32), cap)
    ys = _experts(xs, blk_e, n_used, w_gate, w_up, w_down, layer)
    return _combine(x, ys, dest, gw, ln_g, ln_b)


def _even_layer(x, batch, seq, layer_idx, w_in, dec_f, dec_b, lq1, lk1, lq2, lk2, subln, w_out, ln_g, ln_b):
    d = x.shape[1]
    w = HEADS * LANE
    kw = dict(batch=batch, seq=seq)
    diff_seg = [(0, DIFF_ROT_DIM), (DIFF_HEAD_DIM, DIFF_ROT_DIM)]
    q, k, v, gate, dq, dk, dvt = _even_in(
        x, w_in.astype(BF16), _rope_tables(seq, [(0, LANE)], RET_THETA, 1.0),
        _rope_tables(seq, diff_seg, ROPE_THETA, 1.0), **kw)
    decays = jnp.stack([dec_f, dec_b]).astype(F32)
    of, ob = _retention(q, k, v, decays, **kw)
    lam_init = 0.8 - 0.6 * math.exp(-0.3 * layer_idx)
    diff = _flash(dq, dk, dvt, diff=(lq1, lk1, lq2, lk2, subln), lam_init=lam_init, **kw)
    wo = w_out.astype(BF16)
    return _outproj(x, diff, of, ob, gate, 0, jnp.ones((LANE,), F32), wo[w:], wo[:w], ln_g, ln_b, group_norm=True)


def _odd_layer(x, batch, seq, w_in, q_norm, w_uq, kv_norm, w_ukv, w2_f, b_f, w2_b, b_b, gla_norm, w_out, ln_g, ln_b):
    d = x.shape[1]
    w = HEADS * LANE
    o = np.cumsum([0, MLA_Q_RANK, MLA_KV_RANK, MLA_ROPE, HEADS * GLA_K_DIM, HEADS * GLA_K_DIM, w, w,
                   GLA_GATE_RANK, GLA_GATE_RANK]).tolist()
    zeros = lambda c: jnp.zeros((d, c), F32)
    w_in2 = jnp.concatenate([
        w_in[:, o[0]:o[2]], zeros(MLA_NOPE), w_in[:, o[2]:o[3]], zeros(LANE - MLA_NOPE - MLA_ROPE),
        w_in[:, o[3]:o[7]], w_in[:, o[7]:o[9]], zeros(LANE - 2 * GLA_GATE_RANK)], axis=1).astype(BF16)
    kw = dict(batch=batch, seq=seq)
    qd = MLA_NOPE + MLA_ROPE
    w_uq2 = jnp.pad(w_uq.reshape(MLA_Q_RANK, HEADS, qd), ((0, 0), (0, 0), (0, LANE - qd))).reshape(MLA_Q_RANK, w)
    ukv = w_ukv.reshape(MLA_KV_RANK, HEADS, MLA_NOPE + MLA_V)
    w_uk2 = jnp.pad(ukv[:, :, :MLA_NOPE], ((0, 0), (0, 0), (0, LANE - MLA_NOPE))).reshape(MLA_KV_RANK, w)
    w_uv2 = ukv[:, :, MLA_NOPE:].reshape(MLA_KV_RANK, w)
    q, k, vt, zg = _odd_in(x, w_in2, q_norm, w_uq2.astype(BF16), kv_norm,
                           jnp.concatenate([w_uk2, w_uv2], axis=1).astype(BF16),
                           _rope_tables(seq, [(MLA_NOPE, MLA_ROPE)], ROPE_THETA, 1.0), **kw)
    mla = _flash(q, k, vt, **kw)
    wk = HEADS * GLA_K_DIM
    pad_rows = lambda m, r0: jnp.zeros((LANE, wk), F32).at[r0:r0 + GLA_GATE_RANK].set(m).astype(BF16)
    of, ob = _gla(zg, qcol=0, kcol=1, vcol=1, lcol=12,
                  w2f=pad_rows(w2_f, 0), bf=b_f.reshape(1, wk), w2b=pad_rows(w2_b, GLA_GATE_RANK), bb=b_b.reshape(1, wk), **kw)
    wo = w_out.astype(BF16)
    return _outproj(x, mla, of, ob, zg, 2, gla_norm, wo[:w], wo[w:], ln_g, ln_b, group_norm=False)


def kernel(x, ev_w_in, ev_ret_decay_f, ev_ret_decay_b, ev_lq1, ev_lk1, ev_lq2, ev_lk2, ev_subln, ev_w_out, od_w_in, od_q_norm, od_w_uq, od_kv_norm, od_w_ukv, od_gla_w2_f, od_gla_b_f, od_gla_w2_b, od_gla_b_b, od_gla_norm, od_w_out, ln1_g, ln1_b, ln2_g, ln2_b, moe_w_grp, moe_b_grp, moe_w_exp, moe_b_exp, moe_w_gate, moe_w_up, moe_w_down):
    batch, seq, d = x.shape
    h = x.reshape(batch * seq, d)
    for i in range(DEPTH):
        j = i // 2
        if i % 2 == 0:
            h = _even_layer(h, batch, seq, i, ev_w_in[j], ev_ret_decay_f[j], ev_ret_decay_b[j], ev_lq1[j], ev_lk1[j],
                            ev_lq2[j], ev_lk2[j], ev_subln[j], ev_w_out[j], ln1_g[i], ln1_b[i])
        else:
            h = _odd_layer(h, batch, seq, od_w_in[j], od_q_norm[j], od_w_uq[j], od_kv_norm[j], od_w_ukv[j],
                           od_gla_w2_f[j], od_gla_b_f[j], od_gla_w2_b[j], od_gla_b_b[j], od_gla_norm[j], od_w_out[j],
                           ln1_g[i], ln1_b[i])
        h = _moe(h, moe_w_grp[i], moe_b_grp[i], moe_w_exp[i], moe_b_exp[i], moe_w_gate, moe_w_up, moe_w_down, i,
                 ln2_g[i], ln2_b[i])
    return h.reshape(batch, seq, d)
```

```python
import functools
import math

import numpy as np
import jax
import jax.numpy as jnp
from jax import lax
from jax.experimental import pallas as pl
from jax.experimental.pallas import tpu as pltpu

F32 = jnp.float32
BF16 = jnp.bfloat16

HEADS = 4
LANE = 128
RET_THETA = 10000.0
ROPE_THETA = 500000.0
DIFF_HEAD_DIM = 64
DIFF_ROT_DIM = 16
MLA_Q_RANK = 256
MLA_KV_RANK = 128
MLA_NOPE = 64
MLA_ROPE = 32
MLA_V = 128
GLA_K_DIM = 64
GLA_GATE_RANK = 16
GLA_TAU = 16.0
N_GROUPS = 4
EXPERTS_PER_GROUP = 8
N_EXPERTS = N_GROUPS * EXPERTS_PER_GROUP
DEPTH = 2
ALPHA = (2.0 * DEPTH) ** 0.25
LN_EPS = 1e-5
RMS_EPS = 1e-6

VMEM_LIMIT = 48 * 1024 * 1024


def _div_pow2(x, n):
    return lax.shift_right_logical(x, int(n).bit_length() - 1)


def _mod_pow2(x, n):
    return lax.bitwise_and(x, int(n) - 1)


def _cparams(sem):
    return pltpu.CompilerParams(dimension_semantics=sem, vmem_limit_bytes=VMEM_LIMIT)


def _rope_heads(z, tabs, sh, scale):
    c, sa, sb = tabs
    outs = []
    for h in range(HEADS):
        zh = z[:, h * LANE:(h + 1) * LANE]
        outs.append((zh * c + pltpu.roll(zh, sh, axis=1) * sa + pltpu.roll(zh, LANE - sh, axis=1) * sb) * scale)
    return outs


def _even_in_kernel(x_ref, w_ref, rc, rsa, rsb, dc, dsa, dsb, q_ref, k_ref, v_ref, g_ref, dq_ref, dk_ref, dvt_ref):
    w = HEADS * LANE
    xb = x_ref[...].astype(BF16)
    part = lambda t: jnp.dot(xb, w_ref[:, t * w:(t + 1) * w], preferred_element_type=F32)
    ret_t = (rc[...], rsa[...], rsb[...])
    diff_t = (dc[...], dsa[...], dsb[...])
    for h, o in enumerate(_rope_heads(part(0), ret_t, LANE // 2, 1.0)):
        q_ref[h] = o.astype(BF16)
    for h, o in enumerate(_rope_heads(part(1), ret_t, LANE // 2, LANE ** -0.5)):
        k_ref[h] = o.astype(BF16)
    rv = part(2)
    for h in range(HEADS):
        v_ref[h] = rv[:, h * LANE:(h + 1) * LANE].astype(BF16)
    g_ref[...] = part(3)
    for h, o in enumerate(_rope_heads(part(4), diff_t, DIFF_ROT_DIM // 2, DIFF_HEAD_DIM ** -0.5 * LOG2E)):
        dq_ref[h] = o.astype(BF16)
    for h, o in enumerate(_rope_heads(part(5), diff_t, DIFF_ROT_DIM // 2, 1.0)):
        dk_ref[h] = o.astype(BF16)
    dv = part(6)
    for h in range(HEADS):
        dvt_ref[h, :LANE, :] = dv[:, h * LANE:(h + 1) * LANE].T.astype(BF16)
        dvt_ref[h, LANE:, :] = jnp.ones((ONES_ROWS, dvt_ref.shape[2]), BF16)


def _even_in(x, w, ret_tabs, diff_tabs, *, batch, seq):
    n, d = x.shape
    tm = min(FLASH_TK_DIFF, seq // 2)
    nt = seq // tm
    hw = HEADS * LANE
    heads = jax.ShapeDtypeStruct((batch, HEADS, seq, LANE), BF16)
    head_spec = pl.BlockSpec((None, HEADS, tm, LANE), lambda i: (i // nt, 0, i % nt, 0))
    tab_spec = pl.BlockSpec((tm, LANE), lambda i: (i % nt, 0))
    return pl.pallas_call(
        _even_in_kernel,
        grid=(n // tm,),
        in_specs=[pl.BlockSpec((tm, d), lambda i: (i, 0)), pl.BlockSpec((d, 7 * hw), lambda i: (0, 0))] + [tab_spec] * 6,
        out_specs=[head_spec, head_spec, head_spec, pl.BlockSpec((tm, hw), lambda i: (i, 0)), head_spec, head_spec,
                   pl.BlockSpec((None, HEADS, None, LANE + ONES_ROWS, tm), lambda i: (i // nt, 0, i % nt, 0, 0))],
        out_shape=[heads, heads, heads, jax.ShapeDtypeStruct((n, hw), F32), heads, heads,
                   jax.ShapeDtypeStruct((batch, HEADS, nt, LANE + ONES_ROWS, tm), BF16)],
        compiler_params=_cparams(("parallel",)),
        name="even_in",
    )(x, w, *ret_tabs, *diff_tabs)


def _rms_rows(z, g):
    return z * lax.rsqrt(jnp.mean(z * z, axis=-1, keepdims=True) + RMS_EPS) * g


def _odd_in_kernel(x_ref, w_ref, qn_ref, wq_ref, kvn_ref, wkv_ref, tc, tsa, tsb, q_ref, k_ref, vt_ref, zg_ref):
    hw = HEADS * LANE
    mla_w = MLA_Q_RANK + MLA_KV_RANK + LANE
    xb = x_ref[...].astype(BF16)
    zg_ref[...] = jnp.dot(xb, w_ref[:, mla_w:], preferred_element_type=F32)
    z1 = jnp.dot(xb, w_ref[:, :mla_w], preferred_element_type=F32)
    tabs = (tc[...], tsa[...], tsb[...])
    sh = MLA_ROPE // 2
    qh = jnp.dot(_rms_rows(z1[:, :MLA_Q_RANK], qn_ref[...]).astype(BF16), wq_ref[...], preferred_element_type=F32)
    for h, o in enumerate(_rope_heads(qh, tabs, sh, (MLA_NOPE + MLA_ROPE) ** -0.5 * LOG2E)):
        q_ref[h] = o.astype(BF16)
    ckv = _rms_rows(z1[:, MLA_Q_RANK:MLA_Q_RANK + MLA_KV_RANK], kvn_ref[...]).astype(BF16)
    kv = jnp.dot(ckv, wkv_ref[...], preferred_element_type=F32)
    kr = z1[:, MLA_Q_RANK + MLA_KV_RANK:]
    kr = kr * tabs[0] + pltpu.roll(kr, sh, axis=1) * tabs[1] + pltpu.roll(kr, LANE - sh, axis=1) * tabs[2]
    for h in range(HEADS):
        k_ref[h] = (kv[:, h * LANE:(h + 1) * LANE] + kr).astype(BF16)
        vt_ref[h, :LANE, :] = kv[:, hw + h * LANE:hw + (h + 1) * LANE].T.astype(BF16)
        vt_ref[h, LANE:, :] = jnp.ones((ONES_ROWS, vt_ref.shape[2]), BF16)


def _odd_in(x, w, q_norm, w_uq, kv_norm, w_ukv, tabs, *, batch, seq, tm=512):
    n, d = x.shape
    tm = min(tm, seq // 2)
    nt = seq // tm
    tk = min(FLASH_TK, seq // 2)
    per = tk // tm
    hw = HEADS * LANE
    gw_ = w.shape[1] - (MLA_Q_RANK + MLA_KV_RANK + LANE)
    heads = jax.ShapeDtypeStruct((batch, HEADS, seq, LANE), BF16)
    head_spec = pl.BlockSpec((None, HEADS, tm, LANE), lambda i: (i // nt, 0, i % nt, 0))
    tab_spec = pl.BlockSpec((tm, LANE), lambda i: (i % nt, 0))
    const = lambda i: (0, 0)
    return pl.pallas_call(
        _odd_in_kernel,
        grid=(n // tm,),
        in_specs=[pl.BlockSpec((tm, d), lambda i: (i, 0)), pl.BlockSpec(w.shape, const),
                  pl.BlockSpec((1, MLA_Q_RANK), const), pl.BlockSpec(w_uq.shape, const),
                  pl.BlockSpec((1, MLA_KV_RANK), const), pl.BlockSpec(w_ukv.shape, const)] + [tab_spec] * 3,
        out_specs=[head_spec, head_spec,
                   pl.BlockSpec((None, HEADS, None, LANE + ONES_ROWS, tm),
                                lambda i: (i // nt, 0, (i % nt) // per, 0, (i % nt) % per)),
                   pl.BlockSpec((tm, gw_), lambda i: (i, 0))],
        out_shape=[heads, heads, jax.ShapeDtypeStruct((batch, HEADS, seq // tk, LANE + ONES_ROWS, tk), BF16),
                   jax.ShapeDtypeStruct((n, gw_), F32)],
        compiler_params=_cparams(("parallel",)),
        name="odd_in",
    )(x, w, q_norm.reshape(1, -1), w_uq, kv_norm.reshape(1, -1), w_ukv, *tabs)


def _rope_tables(seq, segs, theta, scale):
    pos = jnp.arange(seq, dtype=F32)
    inv = jnp.zeros((LANE,), F32)
    lo = np.zeros((LANE,), bool)
    hi = np.zeros((LANE,), bool)
    for start, rot in segs:
        half = rot // 2
        f = jnp.power(jnp.float32(theta), -jnp.arange(0, rot, 2, dtype=F32) / rot)
        inv = inv.at[start:start + half].set(f).at[start + half:start + rot].set(f)
        lo[start:start + half] = True
        hi[start + half:start + rot] = True
    ang = pos[:, None] * inv[None, :]
    cos, sin = jnp.cos(ang), jnp.sin(ang)
    c = jnp.where(lo | hi, cos, 1.0) * scale
    sa = jnp.where(hi, sin, 0.0) * scale
    sb = jnp.where(lo, -sin, 0.0) * scale
    return c, sa, sb


ONES_ROWS = 16
LOG2E = math.log2(math.e)
FLASH_TK = 1024
FLASH_TK_DIFF = 512


def _flash_kernel(*refs, ncomp, nk, lam_init):
    if ncomp == 2:
        q_ref, k_ref, vt_ref, lq1, lk1, lq2, lk2, g_ref, o_ref, *scr = refs
    else:
        q_ref, k_ref, vt_ref, o_ref, *scr = refs
    qm_sc, m_sc, acc_sc, s0, s1, cm0, cm1, p0, p1, al0, al1 = scr
    tk = s0.shape[1]
    q = q_ref[...]
    if ncomp == 2:
        lane = lax.broadcasted_iota(jnp.int32, q.shape, 1)
        zero = jnp.zeros_like(q)
        qm_sc[0] = jnp.where(lane < DIFF_HEAD_DIM, q, zero)
        qm_sc[1] = jnp.where(lane >= DIFF_HEAD_DIM, q, zero)
    else:
        qm_sc[0] = q
    m_sc[...] = jnp.full(m_sc.shape, -jnp.inf, F32)
    acc_sc[...] = jnp.zeros(acc_sc.shape, F32)

    def scores(j, s_ref, cm_ref):
        k = k_ref[j * tk:(j + 1) * tk, :]
        for c in range(ncomp):
            s = lax.dot_general(k, qm_sc[c], (((1,), (1,)), ((), ())), preferred_element_type=F32)
            s_ref[c] = s
            cm_ref[c] = jnp.max(s, axis=0, keepdims=True)

    def softmax(s_ref, cm_ref, p_ref, al_ref):
        for c in range(ncomp):
            m_old = m_sc[c]
            m_new = jnp.maximum(m_old, cm_ref[c])
            al_ref[c] = jnp.exp2(m_old - m_new)
            p_ref[c] = jnp.exp2(s_ref[c] - m_new).astype(BF16)
            m_sc[c] = m_new

    def values(j, p_ref, al_ref):
        vt = vt_ref[j]
        for c in range(ncomp):
            acc_sc[c] = al_ref[c] * acc_sc[c] + jnp.dot(vt, p_ref[c], preferred_element_type=F32)

    bufs = ((s0, cm0, p0, al0), (s1, cm1, p1, al1))
    scores(0, s0, cm0)
    for j in range(nk):
        s_c, cm_c, p_c, al_c = bufs[j % 2]
        s_n, cm_n, p_n, al_n = bufs[(j + 1) % 2]
        if j + 1 < nk:
            scores(j + 1, s_n, cm_n)
        softmax(s_c, cm_c, p_c, al_c)
        if j >= 1:
            values(j - 1, p_n, al_n)
    values(nk - 1, *bufs[(nk - 1) % 2][2:])

    def normalised(c):
        acc = acc_sc[c]
        return acc[:LANE] / acc[LANE:LANE + 1]

    o = normalised(0)
    if ncomp == 2:
        lam = (jnp.exp(jnp.sum(lq1[...] * lk1[...], keepdims=True))
               - jnp.exp(jnp.sum(lq2[...] * lk2[...], keepdims=True)) + lam_init)
        o = o - lam * normalised(1)
        o = o * lax.rsqrt(jnp.mean(o * o, axis=0, keepdims=True) + RMS_EPS) * g_ref[...] * (1.0 - lam_init)
    o_ref[...] = o.T


def _flash(q, k, vt, *, batch, seq, tq=512, diff=None, lam_init=0.0):
    nk, vrows, tk = vt.shape[2], vt.shape[3], vt.shape[4]
    assert nk % 2 == 0 and vrows == LANE + ONES_ROWS
    tq = min(tq, seq)
    nq = seq // tq
    ncomp = 2 if diff is not None else 1
    in_specs = [
        pl.BlockSpec((None, None, tq, LANE), lambda b, h, i: (b, h, i, 0)),
        pl.BlockSpec((None, None, seq, LANE), lambda b, h, i: (b, h, 0, 0)),
        pl.BlockSpec((None, None, nk, vrows, tk), lambda b, h, i: (b, h, 0, 0, 0)),
    ]
    args = [q, k, vt]
    if diff is not None:
        lq1, lk1, lq2, lk2, subln = diff
        for v in (lq1, lk1, lq2, lk2):
            in_specs.append(pl.BlockSpec((1, DIFF_HEAD_DIM), lambda b, h, i: (0, 0)))
            args.append(v.reshape(1, DIFF_HEAD_DIM))
        in_specs.append(pl.BlockSpec((LANE, 1), lambda b, h, i: (0, 0)))
        args.append(subln.reshape(LANE, 1))
    return pl.pallas_call(
        functools.partial(_flash_kernel, ncomp=ncomp, nk=nk, lam_init=lam_init),
        grid=(batch, HEADS, nq),
        in_specs=in_specs,
        out_specs=pl.BlockSpec((tq, LANE), lambda b, h, i: (b * nq + i, h)),
        out_shape=jax.ShapeDtypeStruct((batch * seq, HEADS * LANE), F32),
        scratch_shapes=[pltpu.VMEM((ncomp, tq, LANE), BF16),
                        pltpu.VMEM((ncomp, 1, tq), F32), pltpu.VMEM((ncomp, vrows, tq), F32),
                        pltpu.VMEM((ncomp, tk, tq), F32), pltpu.VMEM((ncomp, tk, tq), F32),
                        pltpu.VMEM((ncomp, 1, tq), F32), pltpu.VMEM((ncomp, 1, tq), F32),
                        pltpu.VMEM((ncomp, tk, tq), BF16), pltpu.VMEM((ncomp, tk, tq), BF16),
                        pltpu.VMEM((ncomp, 1, tq), F32), pltpu.VMEM((ncomp, 1, tq), F32)],
        compiler_params=_cparams(("parallel", "parallel", "parallel")),
        name="flash_diff" if diff is not None else "flash_mla",
    )(*args)


def _ret_kernel(dec_ref, qf, kf, vf, qb, kb, vb, of_ref, ob_ref, s_sc, *, chunk):
    @pl.when(pl.program_id(1) == 0)
    def _():
        s_sc[...] = jnp.zeros(s_sc.shape, F32)

    ii = lax.broadcasted_iota(jnp.int32, (chunk, chunk), 0)
    jj = lax.broadcasted_iota(jnp.int32, (chunk, chunk), 1)
    r = lax.broadcasted_iota(jnp.int32, (chunk, 1), 0).astype(F32)
    for d, (q_ref, k_ref, v_ref, o_ref) in enumerate(((qf, kf, vf, of_ref), (qb, kb, vb, ob_ref))):
        for h in range(HEADS):
            la = -jnp.exp(jnp.full((1, 1), dec_ref[d, h], F32))
            if d == 0:
                mask, dist = ii >= jj, (ii - jj).astype(F32)
                qdec, kdec = jnp.exp(la * (r + 1.0)), jnp.exp(la * (chunk - 1.0 - r))
            else:
                mask, dist = jj > ii, (jj - ii).astype(F32)
                qdec, kdec = jnp.exp(la * (chunk - r)), jnp.exp(la * r)
            decay = jnp.where(mask, jnp.exp(jnp.where(mask, dist * la, 0.0)), 0.0)
            q, k, v = q_ref[h], k_ref[h], v_ref[h]
            s = lax.dot_general(q, k, (((1,), (1,)), ((), ())), preferred_element_type=F32)
            o = jnp.dot((s * decay).astype(BF16), v, preferred_element_type=F32)
            state = s_sc[d, h]
            o = o + qdec * jnp.dot(q, state.astype(BF16), preferred_element_type=F32)
            kd = (k.astype(F32) * kdec).astype(BF16)
            s_sc[d, h] = jnp.exp(la * float(chunk)) * state + lax.dot_general(
                kd, v, (((0,), (0,)), ((), ())), preferred_element_type=F32)
            o_ref[:, h * LANE:(h + 1) * LANE] = o


def _retention(q, k, v, decays, *, batch, seq, chunk=256):
    chunk = min(chunk, seq)
    n = seq // chunk
    fwd = pl.BlockSpec((None, HEADS, chunk, LANE), lambda b, c: (b, 0, c, 0))
    bwd = pl.BlockSpec((None, HEADS, chunk, LANE), lambda b, c: (b, 0, n - 1 - c, 0))
    w = HEADS * LANE
    out = jax.ShapeDtypeStruct((batch * seq, w), F32)
    return pl.pallas_call(
        functools.partial(_ret_kernel, chunk=chunk),
        grid=(batch, n),
        in_specs=[pl.BlockSpec(memory_space=pltpu.SMEM), fwd, fwd, fwd, bwd, bwd, bwd],
        out_specs=[pl.BlockSpec((chunk, w), lambda b, c: (b * n + c, 0)),
                   pl.BlockSpec((chunk, w), lambda b, c: (b * n + n - 1 - c, 0))],
        out_shape=[out, out],
        scratch_shapes=[pltpu.VMEM((2, HEADS, LANE, LANE), F32)],
        compiler_params=_cparams(("parallel", "arbitrary")),
        name="retention",
    )(decays, q, k, v, q, k, v)


GLA_SUB = 8


def _split3(x):
    x1 = x.astype(BF16)
    r1 = x - x1.astype(F32)
    x2 = r1.astype(BF16)
    x3 = (r1 - x2.astype(F32)).astype(BF16)
    return x1, x2, x3


def _gla_direction(q, k, v, lr, w2, bias, st, reverse):
    C, wk = q.shape
    wv = v.shape[1]
    dk, dv = wk // HEADS, wv // HEADS
    z = jnp.dot(lr.astype(BF16), w2, preferred_element_type=F32) + bias
    g = (jnp.minimum(z, 0.0) - jnp.log(1.0 + jnp.exp(-jnp.abs(z)))) * (1.0 / GLA_TAU)
    ii = lax.broadcasted_iota(jnp.int32, (C, C), 0)
    jj = lax.broadcasted_iota(jnp.int32, (C, C), 1)
    tri = jnp.where(ii >= jj, 1.0, 0.0).astype(BF16)
    b = sum(jnp.dot(tri, part, preferred_element_type=F32) for part in _split3(g))
    tot = b[C - 1:C, :]
    c = (tot - b + g) if reverse else b

    qe = (q * jnp.exp(jnp.minimum(c, 0.0))).astype(BF16)
    o = lax.dot_general(qe, st.astype(BF16), (((1,), (1,)), ((), ())), preferred_element_type=F32)
    ke = (k * jnp.exp(jnp.minimum(tot - c, 0.0))).astype(BF16)
    upd = lax.dot_general(v.astype(BF16), ke, (((0,), (0,)), ((), ())), preferred_element_type=F32)
    rr = _div_pow2(lax.broadcasted_iota(jnp.int32, (wv, wk), 0), dv)
    cc = _div_pow2(lax.broadcasted_iota(jnp.int32, (wv, wk), 1), dk)
    new_st = jnp.where(rr == cc, st * jnp.exp(tot) + upd, 0.0)

    lane_head = _div_pow2(lax.broadcasted_iota(jnp.int32, (C, wk), 1), dk)
    scores = [jnp.zeros((C, C), F32) for _ in range(HEADS)]
    hsz = C // 2
    while hsz >= GLA_SUB:
        blk = 2 * hsz
        rows = []
        for m in range(C // blk):
            rrow = m * blk + (hsz if reverse else hsz - 1)
            rows.append(jnp.broadcast_to(c[rrow:rrow + 1, :], (blk, wk)))
        ref = jnp.concatenate(rows, axis=0) if len(rows) > 1 else rows[0]
        qt = q * jnp.exp(jnp.minimum(c - ref, 0.0))
        kt = (k * jnp.exp(jnp.minimum(ref - c, 0.0))).astype(BF16)
        same = _div_pow2(ii, blk) == _div_pow2(jj, blk)
        if reverse:
            lvl = same & (_mod_pow2(ii, blk) < hsz) & (_mod_pow2(jj, blk) >= hsz)
        else:
            lvl = same & (_mod_pow2(ii, blk) >= hsz) & (_mod_pow2(jj, blk) < hsz)
        for h in range(HEADS):
            qh = jnp.where(lane_head == h, qt, 0.0).astype(BF16)
            s = lax.dot_general(qh, kt, (((1,), (1,)), ((), ())), preferred_element_type=F32)
            scores[h] = scores[h] + jnp.where(lvl, s, 0.0)
        hsz //= 2

    assert dv == C
    er = _div_pow2(lax.broadcasted_iota(jnp.int32, (wk, wv), 0), dk)
    ec = _div_pow2(lax.broadcasted_iota(jnp.int32, (wk, wv), 1), dv)
    expand = jnp.where(er == ec, 1.0, 0.0).astype(BF16)
    dist = (jj - ii) if reverse else (ii - jj)
    same_sub = _div_pow2(ii, GLA_SUB) == _div_pow2(jj, GLA_SUB)
    for lag in range(1 if reverse else 0, GLA_SUB):
        if lag == 0:
            t = q * k
        else:
            shift = (C - lag) if reverse else lag
            ks, cs = pltpu.roll(k, shift, axis=0), pltpu.roll(c, shift, axis=0)
            t = q * ks * jnp.exp(jnp.minimum(c - cs, 0.0))
        red = jnp.dot(t.astype(BF16), expand, preferred_element_type=F32)
        on_diag = same_sub & (dist == lag)
        for h in range(HEADS):
            scores[h] = scores[h] + jnp.where(on_diag, red[:, h * dv:(h + 1) * dv], 0.0)

    vb = v.astype(BF16)
    o = o + jnp.concatenate(
        [jnp.dot(scores[h].astype(BF16), vb[:, h * dv:(h + 1) * dv], preferred_element_type=F32) for h in range(HEADS)],
        axis=1)
    return o, new_st


def _gla_kernel(qf, kf, vf, lf, qb, kb, vb, lb, w2f, bf, w2b, bb, of_ref, ob_ref, s_sc, *, qscale):
    @pl.when(pl.program_id(1) == 0)
    def _():
        s_sc[...] = jnp.zeros(s_sc.shape, F32)

    o, st = _gla_direction(qf[...] * qscale, kf[...], vf[...], lf[...], w2f[...], bf[...], s_sc[0], False)
    of_ref[...] = o
    s_sc[0] = st
    o, st = _gla_direction(qb[...] * qscale, kb[...], vb[...], lb[...], w2b[...], bb[...], s_sc[1], True)
    ob_ref[...] = o
    s_sc[1] = st


def _gla(z, *, qcol, kcol, vcol, lcol, w2f, bf, w2b, bb, batch, seq, chunk=128):
    chunk = min(chunk, seq)
    n = seq // chunk
    wk, wv = HEADS * GLA_K_DIM, HEADS * LANE

    def specs(cmap):
        return [pl.BlockSpec((chunk, wk), lambda b, c: (cmap(b, c), qcol)),
                pl.BlockSpec((chunk, wk), lambda b, c: (cmap(b, c), kcol)),
                pl.BlockSpec((chunk, wv), lambda b, c: (cmap(b, c), vcol)),
                pl.BlockSpec((chunk, LANE), lambda b, c: (cmap(b, c), lcol))]

    fmap = lambda b, c: b * n + c
    bmap = lambda b, c: b * n + n - 1 - c
    wspec = [pl.BlockSpec((LANE, wk), lambda b, c: (0, 0)), pl.BlockSpec((1, wk), lambda b, c: (0, 0))]
    out = jax.ShapeDtypeStruct((batch * seq, wv), F32)
    return pl.pallas_call(
        functools.partial(_gla_kernel, qscale=GLA_K_DIM ** -0.5),
        grid=(batch, n),
        in_specs=specs(fmap) + specs(bmap) + wspec + wspec,
        out_specs=[pl.BlockSpec((chunk, wv), lambda b, c: (fmap(b, c), 0)),
                   pl.BlockSpec((chunk, wv), lambda b, c: (bmap(b, c), 0))],
        out_shape=[out, out],
        scratch_shapes=[pltpu.VMEM((2, wv, wk), F32)],
        compiler_params=_cparams(("parallel", "arbitrary")),
        name="gla",
    )(z, z, z, z, z, z, z, z, w2f, bf, w2b, bb)


def _layer_norm_rows(r, g, b):
    mu = jnp.mean(r, axis=-1, keepdims=True)
    d = r - mu
    var = jnp.mean(d * d, axis=-1, keepdims=True)
    return d * lax.rsqrt(var + LN_EPS) * g + b


def _outproj_kernel(x_ref, fin_ref, of_ref, ob_ref, gate_ref, ng_ref, wa_ref, wb_ref, lg_ref, lb_ref, o_ref, *, group_norm):
    lin = of_ref[...] + ob_ref[...]
    parts = []
    for h in range(HEADS):
        zh = lin[:, h * LANE:(h + 1) * LANE]
        if group_norm:
            mu = jnp.mean(zh, axis=-1, keepdims=True)
            dz = zh - mu
            parts.append(dz * lax.rsqrt(jnp.mean(dz * dz, axis=-1, keepdims=True) + LN_EPS))
        else:
            parts.append(zh * lax.rsqrt(jnp.mean(zh * zh, axis=-1, keepdims=True) + RMS_EPS) * ng_ref[...])
    gate = gate_ref[...]
    lin = jnp.concatenate(parts, axis=1) * (gate * jax.nn.sigmoid(gate))
    y = (jnp.dot(fin_ref[...].astype(BF16), wa_ref[...], preferred_element_type=F32)
         + jnp.dot(lin.astype(BF16), wb_ref[...], preferred_element_type=F32))
    o_ref[...] = _layer_norm_rows(ALPHA * x_ref[...] + y, lg_ref[...], lb_ref[...])


def _outproj(x, fin, of, ob, gate_src, gate_col, norm_gain, wa, wb, ln_g, ln_b, *, group_norm, tm=512):
    n, d = x.shape
    w = HEADS * LANE
    tm = min(tm, n)
    row = lambda i: (i, 0)
    const = lambda i: (0, 0)
    return pl.pallas_call(
        functools.partial(_outproj_kernel, group_norm=group_norm),
        grid=(n // tm,),
        in_specs=[pl.BlockSpec((tm, d), row), pl.BlockSpec((tm, w), row), pl.BlockSpec((tm, w), row),
                  pl.BlockSpec((tm, w), row), pl.BlockSpec((tm, w), lambda i: (i, gate_col)),
                  pl.BlockSpec((1, LANE), const), pl.BlockSpec((w, d), const), pl.BlockSpec((w, d), const),
                  pl.BlockSpec((1, d), const), pl.BlockSpec((1, d), const)],
        out_specs=pl.BlockSpec((tm, d), row),
        out_shape=jax.ShapeDtypeStruct((n, d), F32),
        compiler_params=_cparams(("parallel",)),
        name="outproj",
    )(x, fin, of, ob, gate_src, norm_gain.reshape(1, LANE), wa, wb, ln_g.reshape(1, d), ln_b.reshape(1, d))


def _route_kernel(x_ref, w_ref, b_ref, ids_ref, gw_ref, cnt_ref, run_sc):
    @pl.when(pl.program_id(0) == 0)
    def _():
        run_sc[...] = jnp.zeros(run_sc.shape, F32)

    tm = x_ref.shape[0]
    x = x_ref[...]
    xh = x.astype(BF16)
    xl = (x - xh.astype(F32)).astype(BF16)
    wh, wl = w_ref[0], w_ref[1]
    logits = (jnp.dot(xh, wh, preferred_element_type=F32) + jnp.dot(xh, wl, preferred_element_type=F32)
              + jnp.dot(xl, wh, preferred_element_type=F32)) + b_ref[...]
    lane = lax.broadcasted_iota(jnp.int32, logits.shape, 1)
    neg = -jnp.inf
    gmask = (lane >= N_EXPERTS) & (lane < N_EXPERTS + N_GROUPS)
    gl = jnp.where(gmask, logits, neg)
    gmax = jnp.max(gl, axis=1, keepdims=True)
    lane_f = lane.astype(F32)
    first = lambda hit: jnp.min(jnp.where(hit, lane_f, float(LANE)), axis=1, keepdims=True).astype(jnp.int32)
    gidx = first(gl == gmax) - N_EXPERTS
    p_grp = 1.0 / jnp.sum(jnp.where(gmask, jnp.exp(gl - gmax), 0.0), axis=1, keepdims=True)
    el = jnp.where(_div_pow2(lane, EXPERTS_PER_GROUP) == gidx, logits, neg)
    l1 = jnp.max(el, axis=1, keepdims=True)
    e1 = first(el == l1)
    el2 = jnp.where(lane == e1, neg, el)
    l2 = jnp.max(el2, axis=1, keepdims=True)
    e2 = first(el2 == l2)
    t = jnp.exp(l2 - l1)
    w1 = p_grp / (1.0 + t)
    w2 = p_grp * t / (1.0 + t)

    onehot = jnp.where(lane == e1, 1.0, jnp.where(lane == e2, 1.0, 0.0))
    ri = lax.broadcasted_iota(jnp.int32, (tm, tm), 0)
    ci = lax.broadcasted_iota(jnp.int32, (tm, tm), 1)
    before = jnp.dot(jnp.where(ri > ci, 1.0, 0.0).astype(BF16), onehot.astype(BF16), preferred_element_type=F32)
    before = before + run_sc[...]
    r1 = jnp.sum(jnp.where(lane == e1, before, 0.0), axis=1, keepdims=True).astype(jnp.int32)
    r2 = jnp.sum(jnp.where(lane == e2, before, 0.0), axis=1, keepdims=True).astype(jnp.int32)
    run_sc[...] = run_sc[...] + jnp.sum(onehot, axis=0, keepdims=True)
    cnt_ref[...] = run_sc[...].astype(jnp.int32)
    ids_ref[...] = jnp.where(lane == 0, e1, jnp.where(lane == 1, e2, jnp.where(lane == 2, r1, jnp.where(lane == 3, r2, 0))))
    gw_ref[...] = jnp.where(lane == 0, w1, jnp.where(lane == 1, w2, 0.0))


def _route(x, w, b, *, tm=512):
    n, d = x.shape
    tm = min(tm, n)
    w_hi = w.astype(BF16)
    w = jnp.stack([w_hi, (w - w_hi.astype(F32)).astype(BF16)])
    return pl.pallas_call(
        _route_kernel,
        grid=(n // tm,),
        in_specs=[pl.BlockSpec((tm, d), lambda i: (i, 0)), pl.BlockSpec((2, d, LANE), lambda i: (0, 0, 0)),
                  pl.BlockSpec((1, LANE), lambda i: (0, 0))],
        out_specs=[pl.BlockSpec((tm, LANE), lambda i: (i, 0)), pl.BlockSpec((tm, LANE), lambda i: (i, 0)),
                   pl.BlockSpec((1, LANE), lambda i: (0, 0))],
        out_shape=[jax.ShapeDtypeStruct((n, LANE), jnp.int32), jax.ShapeDtypeStruct((n, LANE), F32),
                   jax.ShapeDtypeStruct((1, LANE), jnp.int32)],
        scratch_shapes=[pltpu.VMEM((1, LANE), F32)],
        compiler_params=_cparams(("arbitrary",)),
        name="route",
    )(x, w, b)


MOE_BM = 256
SUBL = 8


def _rows_from_linear(ref, rows):
    return jnp.concatenate([ref[pl.ds(s, rows, stride=SUBL), :] for s in range(SUBL)], axis=1)


def _rows_to_linear(ref, val):
    for s in range(SUBL):
        ref[pl.ds(s, val.shape[0], stride=SUBL), :] = val[:, s * LANE:(s + 1) * LANE]


def _dispatch_kernel(pend_ref, padded_ref, dest_ref, x_ref, xs_hbm, idx_smem, zbuf, lin, sem_i, sem_z, sem):
    i = pl.program_id(0)
    tm = x_ref.shape[0]
    bm = zbuf.shape[0] // SUBL

    @pl.when(i == 0)
    def _():
        zbuf[...] = jnp.zeros(zbuf.shape, F32)

        def tail(e):
            start_row = pl.multiple_of((pend_ref[e] - bm) * SUBL, bm * SUBL)
            return pltpu.make_async_copy(zbuf, xs_hbm.at[pl.ds(start_row, bm * SUBL), :], sem_z)

        def start(e, carry):
            @pl.when(padded_ref[e] > 0)
            def _():
                tail(e).start()
            return carry

        def wait(e, carry):
            @pl.when(padded_ref[e] > 0)
            def _():
                tail(e).wait()
            return carry

        lax.fori_loop(0, N_EXPERTS, start, 0)
        lax.fori_loop(0, N_EXPERTS, wait, 0)

        def unused(b):
            start_row = pl.multiple_of(b * bm * SUBL, bm * SUBL)
            return pltpu.make_async_copy(zbuf, xs_hbm.at[pl.ds(start_row, bm * SUBL), :], sem_z)

        first_unused = pend_ref[N_EXPERTS - 1] // bm
        n_blocks = xs_hbm.shape[0] // (bm * SUBL)
        lax.fori_loop(first_unused, n_blocks, lambda b, c: (unused(b).start(), c)[1], 0)
        lax.fori_loop(first_unused, n_blocks, lambda b, c: (unused(b).wait(), c)[1], 0)

    cp = pltpu.make_async_copy(dest_ref.at[i], idx_smem, sem_i)
    cp.start()
    _rows_to_linear(lin, x_ref[...])
    cp.wait()

    def scatter(r, carry):
        src = lin.at[pl.ds(pl.multiple_of(r * SUBL, SUBL), SUBL), :]
        for k in range(2):
            dst = pl.multiple_of(idx_smem[2 * r + k] * SUBL, SUBL)
            pltpu.make_async_copy(src, xs_hbm.at[pl.ds(dst, SUBL), :], sem).start(priority=k)
        return carry

    lax.fori_loop(0, tm, scatter, 0, unroll=8)
    for k in range(2):
        pltpu.make_async_copy(lin, xs_hbm.at[pl.ds(0, tm * SUBL), :], sem).wait()


def _dispatch(x, dest, pend, padded, cap, *, tm=512):
    n, d = x.shape
    assert d == SUBL * LANE
    tm = min(tm, n)
    nt = n // tm
    grid_spec = pltpu.PrefetchScalarGridSpec(
        num_scalar_prefetch=2,
        grid=(nt,),
        in_specs=[pl.BlockSpec((nt, 2 * tm), lambda i, pe, pa: (0, 0)),
                  pl.BlockSpec((tm, d), lambda i, pe, pa: (i, 0))],
        out_specs=pl.BlockSpec(memory_space=pl.ANY),
        scratch_shapes=[pltpu.SMEM((2 * tm,), jnp.int32), pltpu.VMEM((MOE_BM * SUBL, LANE), F32),
                        pltpu.VMEM((tm * SUBL, LANE), F32),
                        pltpu.SemaphoreType.DMA(()), pltpu.SemaphoreType.DMA(()), pltpu.SemaphoreType.DMA(())],
    )
    return pl.pallas_call(
        _dispatch_kernel,
        grid_spec=grid_spec,
        out_shape=jax.ShapeDtypeStruct((cap * SUBL, LANE), F32),
        compiler_params=_cparams(("arbitrary",)),
        name="dispatch",
    )(pend, padded, dest.reshape(nt, 2 * tm), x)


def _experts_kernel(blk_e_ref, nused_ref, xs_ref, wg_ref, wu_ref, wd_ref, ys_ref, wgb, wub, wdb):
    i = pl.program_id(0)

    @pl.when(i < nused_ref[0])
    def _():
        prev = blk_e_ref[jnp.maximum(i - 1, 0)]

        @pl.when(jnp.logical_or(i == 0, blk_e_ref[i] != prev))
        def _():
            wgb[...] = wg_ref[...].astype(BF16)
            wub[...] = wu_ref[...].astype(BF16)
            wdb[...] = wd_ref[...].astype(BF16)

        xb = _rows_from_linear(xs_ref, xs_ref.shape[0] // SUBL).astype(BF16)
        hg = jnp.dot(xb, wgb[...], preferred_element_type=F32)
        hu = jnp.dot(xb, wub[...], preferred_element_type=F32)
        hid = (hg * jax.nn.sigmoid(hg) * hu).astype(BF16)
        _rows_to_linear(ys_ref, jnp.dot(hid, wdb[...], preferred_element_type=F32))

    @pl.when(i >= nused_ref[0])
    def _():
        ys_ref[...] = jnp.zeros(ys_ref.shape, F32)


def _experts(xs, blk_e, n_used, w_gate, w_up, w_down, layer):
    cap = xs.shape[0] // SUBL
    bm = MOE_BM
    d, de = w_gate.shape[2], w_gate.shape[3]
    row_in = lambda i, be, nu: (jnp.minimum(i, nu[0] - 1), 0)
    row = lambda i, be, nu: (i, 0)
    grid_spec = pltpu.PrefetchScalarGridSpec(
        num_scalar_prefetch=2,
        grid=(cap // bm,),
        in_specs=[pl.BlockSpec((bm * SUBL, LANE), row_in),
                  pl.BlockSpec((None, None, d, de), lambda i, be, nu: (layer, be[i], 0, 0)),
                  pl.BlockSpec((None, None, d, de), lambda i, be, nu: (layer, be[i], 0, 0)),
                  pl.BlockSpec((None, None, de, d), lambda i, be, nu: (layer, be[i], 0, 0))],
        out_specs=pl.BlockSpec((bm * SUBL, LANE), row),
        scratch_shapes=[pltpu.VMEM((d, de), BF16), pltpu.VMEM((d, de), BF16), pltpu.VMEM((de, d), BF16)],
    )
    return pl.pallas_call(
        _experts_kernel,
        grid_spec=grid_spec,
        out_shape=jax.ShapeDtypeStruct((cap * SUBL, LANE), F32),
        compiler_params=_cparams(("arbitrary",)),
        name="experts",
    )(blk_e, n_used, xs, w_gate, w_up, w_down)


def _combine_kernel(dest_ref, x_ref, gw_ref, g_ref, b_ref, ys_hbm, o_ref, idx_smem, ybuf, sem_i, sem):
    i = pl.program_id(0)
    tm = x_ref.shape[0]
    cp = pltpu.make_async_copy(dest_ref.at[i], idx_smem, sem_i)
    cp.start()
    cp.wait()

    def gather(r, carry):
        row = pl.multiple_of(r * SUBL, SUBL)
        for k in range(2):
            src = pl.multiple_of(idx_smem[2 * r + k] * SUBL, SUBL)
            pltpu.make_async_copy(ys_hbm.at[pl.ds(src, SUBL), :], ybuf.at[k, pl.ds(row, SUBL), :], sem).start(priority=k)
        return carry

    lax.fori_loop(0, tm, gather, 0, unroll=8)
    for k in range(2):
        pltpu.make_async_copy(ys_hbm.at[pl.ds(0, tm * SUBL), :], ybuf.at[k], sem).wait()
    gw = gw_ref[...]
    ffn = _rows_from_linear(ybuf.at[0], tm) * gw[:, 0:1] + _rows_from_linear(ybuf.at[1], tm) * gw[:, 1:2]
    o_ref[...] = _layer_norm_rows(ALPHA * x_ref[...] + ffn, g_ref[...], b_ref[...])


def _combine(x, ys, dest, gw, ln_g, ln_b, *, tm=512):
    n, d = x.shape
    tm = min(tm, n)
    nt = n // tm
    return pl.pallas_call(
        _combine_kernel,
        grid=(nt,),
        in_specs=[pl.BlockSpec((nt, 2 * tm), lambda i: (0, 0)),
                  pl.BlockSpec((tm, d), lambda i: (i, 0)), pl.BlockSpec((tm, LANE), lambda i: (i, 0)),
                  pl.BlockSpec((1, d), lambda i: (0, 0)), pl.BlockSpec((1, d), lambda i: (0, 0)),
                  pl.BlockSpec(memory_space=pl.ANY)],
        out_specs=pl.BlockSpec((tm, d), lambda i: (i, 0)),
        out_shape=jax.ShapeDtypeStruct((n, d), F32),
        scratch_shapes=[pltpu.SMEM((2 * tm,), jnp.int32), pltpu.VMEM((2, tm * SUBL, LANE), F32),
                        pltpu.SemaphoreType.DMA(()), pltpu.SemaphoreType.DMA(())],
        compiler_params=_cparams(("arbitrary",)),
        name="combine",
    )(dest.reshape(nt, 2 * tm), x, gw, ln_g.reshape(1, d), ln_b.reshape(1, d), ys)


def _moe(x, w_grp, b_grp, w_exp, b_exp, w_gate, w_up, w_down, layer, ln_g, ln_b):
    n, d = x.shape
    wr = jnp.zeros((d, LANE), F32).at[:, :N_EXPERTS].set(w_exp).at[:, N_EXPERTS:N_EXPERTS + N_GROUPS].set(w_grp)
    br = jnp.zeros((1, LANE), F32).at[0, :N_EXPERTS].set(b_exp).at[0, N_EXPERTS:N_EXPERTS + N_GROUPS].set(b_grp)
    ids, gw, cnt = _route(x, wr, br)
    bm = MOE_BM
    counts = cnt[0, :N_EXPERTS]
    padded = (counts + bm - 1) // bm * bm
    pend = jnp.cumsum(padded)
    pstart = pend - padded
    e, r = ids[:, 0:2], ids[:, 2:4]
    onehot = e[:, :, None] == jnp.arange(N_EXPERTS, dtype=jnp.int32)[None, None, :]
    dest = jnp.sum(jnp.where(onehot, pstart[None, None, :], 0), axis=-1) + r
    cap = 2 * n + N_EXPERTS * bm
    nb = cap // bm
    blk_start = jnp.arange(nb, dtype=jnp.int32) * bm
    blk_e = jnp.minimum(jnp.sum((pend[None, :] <= blk_start[:, None]).astype(jnp.int32), axis=1), N_EXPERTS - 1)
    n_used = (pend[-1:] // bm).astype(jnp.int32)
    xs = _dispatch(x, dest, pend.astype(jnp.int32), padded.astype(jnp.int32), cap)
    ys = _experts(xs, blk_e, n_used, w_gate, w_up, w_down, layer)
    return _combine(x, ys, dest, gw, ln_g, ln_b)


def _even_layer(x, batch, seq, layer_idx, w_in, dec_f, dec_b, lq1, lk1, lq2, lk2, subln, w_out, ln_g, ln_b):
    d = x.shape[1]
    w = HEADS * LANE
    kw = dict(batch=batch, seq=seq)
    diff_seg = [(0, DIFF_ROT_DIM), (DIFF_HEAD_DIM, DIFF_ROT_DIM)]
    q, k, v, gate, dq, dk, dvt = _even_in(
        x, w_in.astype(BF16), _rope_tables(seq, [(0, LANE)], RET_THETA, 1.0),
        _rope_tables(seq, diff_seg, ROPE_THETA, 1.0), **kw)
    decays = jnp.stack([dec_f, dec_b]).astype(F32)
    of, ob = _retention(q, k, v, decays, **kw)
    lam_init = 0.8 - 0.6 * math.exp(-0.3 * layer_idx)
    diff = _flash(dq, dk, dvt, diff=(lq1, lk1, lq2, lk2, subln), lam_init=lam_init, **kw)
    wo = w_out.astype(BF16)
    return _outproj(x, diff, of, ob, gate, 0, jnp.ones((LANE,), F32), wo[w:], wo[:w], ln_g, ln_b, group_norm=True)


def _odd_layer(x, batch, seq, w_in, q_norm, w_uq, kv_norm, w_ukv, w2_f, b_f, w2_b, b_b, gla_norm, w_out, ln_g, ln_b):
    d = x.shape[1]
    w = HEADS * LANE
    o = np.cumsum([0, MLA_Q_RANK, MLA_KV_RANK, MLA_ROPE, HEADS * GLA_K_DIM, HEADS * GLA_K_DIM, w, w,
                   GLA_GATE_RANK, GLA_GATE_RANK]).tolist()
    zeros = lambda c: jnp.zeros((d, c), F32)
    w_in2 = jnp.concatenate([
        w_in[:, o[0]:o[2]], zeros(MLA_NOPE), w_in[:, o[2]:o[3]], zeros(LANE - MLA_NOPE - MLA_ROPE),
        w_in[:, o[3]:o[7]], w_in[:, o[7]:o[9]], zeros(LANE - 2 * GLA_GATE_RANK)], axis=1).astype(BF16)
    kw = dict(batch=batch, seq=seq)
    qd = MLA_NOPE + MLA_ROPE
    w_uq2 = jnp.pad(w_uq.reshape(MLA_Q_RANK, HEADS, qd), ((0, 0), (0, 0), (0, LANE - qd))).reshape(MLA_Q_RANK, w)
    ukv = w_ukv.reshape(MLA_KV_RANK, HEADS, MLA_NOPE + MLA_V)
    w_uk2 = jnp.pad(ukv[:, :, :MLA_NOPE], ((0, 0), (0, 0), (0, LANE - MLA_NOPE))).reshape(MLA_KV_RANK, w)
    w_uv2 = ukv[:, :, MLA_NOPE:].reshape(MLA_KV_RANK, w)
    q, k, vt, zg = _odd_in(x, w_in2, q_norm, w_uq2.astype(BF16), kv_norm,
                           jnp.concatenate([w_uk2, w_uv2], axis=1).astype(BF16),
                           _rope_tables(seq, [(MLA_NOPE, MLA_ROPE)], ROPE_THETA, 1.0), **kw)
    mla = _flash(q, k, vt, **kw)
    wk = HEADS * GLA_K_DIM
    pad_rows = lambda m, r0: jnp.zeros((LANE, wk), F32).at[r0:r0 + GLA_GATE_RANK].set(m).astype(BF16)
    of, ob = _gla(zg, qcol=0, kcol=1, vcol=1, lcol=12,
                  w2f=pad_rows(w2_f, 0), bf=b_f.reshape(1, wk), w2b=pad_rows(w2_b, GLA_GATE_RANK), bb=b_b.reshape(1, wk), **kw)
    wo = w_out.astype(BF16)
    return _outproj(x, mla, of, ob, zg, 2, gla_norm, wo[:w], wo[w:], ln_g, ln_b, group_norm=False)


def kernel(x, ev_w_in, ev_ret_decay_f, ev_ret_decay_b, ev_lq1, ev_lk1, ev_lq2, ev_lk2, ev_subln, ev_w_out, od_w_in, od_q_norm, od_w_uq, od_kv_norm, od_w_ukv, od_gla_w2_f, od_gla_b_f, od_gla_w2_b, od_gla_b_b, od_gla_norm, od_w_out, ln1_g, ln1_b, ln2_g, ln2_b, moe_w_grp, moe_b_grp, moe_w_exp, moe_b_exp, moe_w_gate, moe_w_up, moe_w_down):
    batch, seq, d = x.shape
    h = x.reshape(batch * seq, d)
    for i in range(DEPTH):
        j = i // 2
        if i % 2 == 0:
            h = _even_layer(h, batch, seq, i, ev_w_in[j], ev_ret_decay_f[j], ev_ret_decay_b[j], ev_lq1[j], ev_lk1[j],
                            ev_lq2[j], ev_lk2[j], ev_subln[j], ev_w_out[j], ln1_g[i], ln1_b[i])
        else:
            h = _odd_layer(h, batch, seq, od_w_in[j], od_q_norm[j], od_w_uq[j], od_kv_norm[j], od_w_ukv[j],
                           od_gla_w2_f[j], od_gla_b_f[j], od_gla_w2_b[j], od_gla_b_b[j], od_gla_norm[j], od_w_out[j],
                           ln1_g[i], ln1_b[i])
        h = _moe(h, moe_w_grp[i], moe_b_grp[i], moe_w_exp[i], moe_b_exp[i], moe_w_gate, moe_w_up, moe_w_down, i,
                 ln2_g[i], ln2_b[i])
    return h.reshape(batch, seq, d)
```

```python
import functools
import math

import numpy as np
import jax
import jax.numpy as jnp
from jax import lax
from jax.experimental import pallas as pl
from jax.experimental.pallas import tpu as pltpu

F32 = jnp.float32
BF16 = jnp.bfloat16

HEADS = 4
LANE = 128
RET_THETA = 10000.0
ROPE_THETA = 500000.0
DIFF_HEAD_DIM = 64
DIFF_ROT_DIM = 16
MLA_Q_RANK = 256
MLA_KV_RANK = 128
MLA_NOPE = 64
MLA_ROPE = 32
MLA_V = 128
GLA_K_DIM = 64
GLA_GATE_RANK = 16
GLA_TAU = 16.0
N_GROUPS = 4
EXPERTS_PER_GROUP = 8
N_EXPERTS = N_GROUPS * EXPERTS_PER_GROUP
DEPTH = 2
ALPHA = (2.0 * DEPTH) ** 0.25
LN_EPS = 1e-5
RMS_EPS = 1e-6

VMEM_LIMIT = 48 * 1024 * 1024


def _div_pow2(x, n):
    return lax.shift_right_logical(x, int(n).bit_length() - 1)


def _mod_pow2(x, n):
    return lax.bitwise_and(x, int(n) - 1)


def _cparams(sem):
    return pltpu.CompilerParams(dimension_semantics=sem, vmem_limit_bytes=VMEM_LIMIT)


def _rope_heads(z, tabs, sh, scale):
    c, sa, sb = tabs
    outs = []
    for h in range(HEADS):
        zh = z[:, h * LANE:(h + 1) * LANE]
        outs.append((zh * c + pltpu.roll(zh, sh, axis=1) * sa + pltpu.roll(zh, LANE - sh, axis=1) * sb) * scale)
    return outs


def _even_in_kernel(x_ref, w_ref, rc, rsa, rsb, dc, dsa, dsb, q_ref, k_ref, v_ref, g_ref, dq_ref, dk_ref, dvt_ref):
    w = HEADS * LANE
    xb = x_ref[...].astype(BF16)
    part = lambda t: jnp.dot(xb, w_ref[:, t * w:(t + 1) * w], preferred_element_type=F32)
    ret_t = (rc[...], rsa[...], rsb[...])
    diff_t = (dc[...], dsa[...], dsb[...])
    for h, o in enumerate(_rope_heads(part(0), ret_t, LANE // 2, 1.0)):
        q_ref[h] = o.astype(BF16)
    for h, o in enumerate(_rope_heads(part(1), ret_t, LANE // 2, LANE ** -0.5)):
        k_ref[h] = o.astype(BF16)
    rv = part(2)
    for h in range(HEADS):
        v_ref[h] = rv[:, h * LANE:(h + 1) * LANE].astype(BF16)
    g_ref[...] = part(3)
    for h, o in enumerate(_rope_heads(part(4), diff_t, DIFF_ROT_DIM // 2, DIFF_HEAD_DIM ** -0.5 * LOG2E)):
        dq_ref[h] = o.T.astype(BF16)
    for h, o in enumerate(_rope_heads(part(5), diff_t, DIFF_ROT_DIM // 2, 1.0)):
        dk_ref[h] = o.astype(BF16)
    dv = part(6)
    for h in range(HEADS):
        dvt_ref[h, :LANE, :] = dv[:, h * LANE:(h + 1) * LANE].T.astype(BF16)
        dvt_ref[h, LANE:, :] = jnp.ones((ONES_ROWS, dvt_ref.shape[2]), BF16)


def _even_in(x, w, ret_tabs, diff_tabs, *, batch, seq):
    n, d = x.shape
    tm = min(FLASH_TK_DIFF, seq // 2)
    nt = seq // tm
    hw = HEADS * LANE
    heads = jax.ShapeDtypeStruct((batch, HEADS, seq, LANE), BF16)
    head_spec = pl.BlockSpec((None, HEADS, tm, LANE), lambda i: (i // nt, 0, i % nt, 0))
    heads_t = jax.ShapeDtypeStruct((batch, HEADS, LANE, seq), BF16)
    head_t_spec = pl.BlockSpec((None, HEADS, LANE, tm), lambda i: (i // nt, 0, 0, i % nt))
    tab_spec = pl.BlockSpec((tm, LANE), lambda i: (i % nt, 0))
    return pl.pallas_call(
        _even_in_kernel,
        grid=(n // tm,),
        in_specs=[pl.BlockSpec((tm, d), lambda i: (i, 0)), pl.BlockSpec((d, 7 * hw), lambda i: (0, 0))] + [tab_spec] * 6,
        out_specs=[head_spec, head_spec, head_spec, pl.BlockSpec((tm, hw), lambda i: (i, 0)), head_t_spec, head_spec,
                   pl.BlockSpec((None, HEADS, None, LANE + ONES_ROWS, tm), lambda i: (i // nt, 0, i % nt, 0, 0))],
        out_shape=[heads, heads, heads, jax.ShapeDtypeStruct((n, hw), F32), heads_t, heads,
                   jax.ShapeDtypeStruct((batch, HEADS, nt, LANE + ONES_ROWS, tm), BF16)],
        compiler_params=_cparams(("parallel",)),
        name="even_in",
    )(x, w, *ret_tabs, *diff_tabs)


def _rms_rows(z, g):
    return z * lax.rsqrt(jnp.mean(z * z, axis=-1, keepdims=True) + RMS_EPS) * g


def _odd_in_kernel(x_ref, w_ref, qn_ref, wq_ref, kvn_ref, wkv_ref, tc, tsa, tsb, q_ref, k_ref, vt_ref, zg_ref):
    hw = HEADS * LANE
    mla_w = MLA_Q_RANK + MLA_KV_RANK + LANE
    xb = x_ref[...].astype(BF16)
    zg_ref[...] = jnp.dot(xb, w_ref[:, mla_w:], preferred_element_type=F32)
    z1 = jnp.dot(xb, w_ref[:, :mla_w], preferred_element_type=F32)
    tabs = (tc[...], tsa[...], tsb[...])
    sh = MLA_ROPE // 2
    qh = jnp.dot(_rms_rows(z1[:, :MLA_Q_RANK], qn_ref[...]).astype(BF16), wq_ref[...], preferred_element_type=F32)
    for h, o in enumerate(_rope_heads(qh, tabs, sh, (MLA_NOPE + MLA_ROPE) ** -0.5 * LOG2E)):
        q_ref[h] = o.T.astype(BF16)
    ckv = _rms_rows(z1[:, MLA_Q_RANK:MLA_Q_RANK + MLA_KV_RANK], kvn_ref[...]).astype(BF16)
    kv = jnp.dot(ckv, wkv_ref[...], preferred_element_type=F32)
    kr = z1[:, MLA_Q_RANK + MLA_KV_RANK:]
    kr = kr * tabs[0] + pltpu.roll(kr, sh, axis=1) * tabs[1] + pltpu.roll(kr, LANE - sh, axis=1) * tabs[2]
    for h in range(HEADS):
        k_ref[h] = (kv[:, h * LANE:(h + 1) * LANE] + kr).astype(BF16)
        vt_ref[h, :LANE, :] = kv[:, hw + h * LANE:hw + (h + 1) * LANE].T.astype(BF16)
        vt_ref[h, LANE:, :] = jnp.ones((ONES_ROWS, vt_ref.shape[2]), BF16)


def _odd_in(x, w, q_norm, w_uq, kv_norm, w_ukv, tabs, *, batch, seq, tm=512):
    n, d = x.shape
    tm = min(tm, seq // 2)
    nt = seq // tm
    tk = min(FLASH_TK, seq // 2)
    per = tk // tm
    hw = HEADS * LANE
    gw_ = w.shape[1] - (MLA_Q_RANK + MLA_KV_RANK + LANE)
    heads = jax.ShapeDtypeStruct((batch, HEADS, seq, LANE), BF16)
    head_spec = pl.BlockSpec((None, HEADS, tm, LANE), lambda i: (i // nt, 0, i % nt, 0))
    tab_spec = pl.BlockSpec((tm, LANE), lambda i: (i % nt, 0))
    const = lambda i: (0, 0)
    return pl.pallas_call(
        _odd_in_kernel,
        grid=(n // tm,),
        in_specs=[pl.BlockSpec((tm, d), lambda i: (i, 0)), pl.BlockSpec(w.shape, const),
                  pl.BlockSpec((1, MLA_Q_RANK), const), pl.BlockSpec(w_uq.shape, const),
                  pl.BlockSpec((1, MLA_KV_RANK), const), pl.BlockSpec(w_ukv.shape, const)] + [tab_spec] * 3,
        out_specs=[pl.BlockSpec((None, HEADS, LANE, tm), lambda i: (i // nt, 0, 0, i % nt)), head_spec,
                   pl.BlockSpec((None, HEADS, None, LANE + ONES_ROWS, tm),
                                lambda i: (i // nt, 0, (i % nt) // per, 0, (i % nt) % per)),
                   pl.BlockSpec((tm, gw_), lambda i: (i, 0))],
        out_shape=[jax.ShapeDtypeStruct((batch, HEADS, LANE, seq), BF16), heads, jax.ShapeDtypeStruct((batch, HEADS, seq // tk, LANE + ONES_ROWS, tk), BF16),
                   jax.ShapeDtypeStruct((n, gw_), F32)],
        compiler_params=_cparams(("parallel",)),
        name="odd_in",
    )(x, w, q_norm.reshape(1, -1), w_uq, kv_norm.reshape(1, -1), w_ukv, *tabs)


def _rope_tables(seq, segs, theta, scale):
    pos = jnp.arange(seq, dtype=F32)
    inv = jnp.zeros((LANE,), F32)
    lo = np.zeros((LANE,), bool)
    hi = np.zeros((LANE,), bool)
    for start, rot in segs:
        half = rot // 2
        f = jnp.power(jnp.float32(theta), -jnp.arange(0, rot, 2, dtype=F32) / rot)
        inv = inv.at[start:start + half].set(f).at[start + half:start + rot].set(f)
        lo[start:start + half] = True
        hi[start + half:start + rot] = True
    ang = pos[:, None] * inv[None, :]
    cos, sin = jnp.cos(ang), jnp.sin(ang)
    c = jnp.where(lo | hi, cos, 1.0) * scale
    sa = jnp.where(hi, sin, 0.0) * scale
    sb = jnp.where(lo, -sin, 0.0) * scale
    return c, sa, sb


ONES_ROWS = 16
LOG2E = math.log2(math.e)
FLASH_TK = 1024
FLASH_TK_DIFF = 512


def _flash_kernel(*refs, ncomp, nk, lam_init):
    if ncomp == 2:
        q_ref, k_ref, vt_ref, lq1, lk1, lq2, lk2, g_ref, o_ref, *scr = refs
    else:
        q_ref, k_ref, vt_ref, o_ref, *scr = refs
    qm_sc, m_sc, acc_sc, s0, s1, cm0, cm1, p0, p1, al0, al1 = scr
    tk = s0.shape[1]
    q = q_ref[...]
    if ncomp == 2:
        chan = lax.broadcasted_iota(jnp.int32, q.shape, 0)
        zero = jnp.zeros_like(q)
        qm_sc[0] = jnp.where(chan < DIFF_HEAD_DIM, q, zero)
        qm_sc[1] = jnp.where(chan >= DIFF_HEAD_DIM, q, zero)
    else:
        qm_sc[0] = q
    m_sc[...] = jnp.full(m_sc.shape, -jnp.inf, F32)
    acc_sc[...] = jnp.zeros(acc_sc.shape, F32)

    def scores(j, s_ref, cm_ref):
        k = k_ref[j * tk:(j + 1) * tk, :]
        for c in range(ncomp):
            s = jnp.dot(k, qm_sc[c], preferred_element_type=F32)
            s_ref[c] = s
            cm_ref[c] = jnp.max(s, axis=0, keepdims=True)

    def softmax(s_ref, cm_ref, p_ref, al_ref):
        for c in range(ncomp):
            m_old = m_sc[c]
            m_new = jnp.maximum(m_old, cm_ref[c])
            al_ref[c] = jnp.exp2(m_old - m_new)
            p_ref[c] = jnp.exp2(s_ref[c] - m_new).astype(BF16)
            m_sc[c] = m_new

    def values(j, p_ref, al_ref):
        vt = vt_ref[j]
        for c in range(ncomp):
            acc_sc[c] = al_ref[c] * acc_sc[c] + jnp.dot(vt, p_ref[c], preferred_element_type=F32)

    bufs = ((s0, cm0, p0, al0), (s1, cm1, p1, al1))
    scores(0, s0, cm0)
    for j in range(nk):
        s_c, cm_c, p_c, al_c = bufs[j % 2]
        s_n, cm_n, p_n, al_n = bufs[(j + 1) % 2]
        if j + 1 < nk:
            scores(j + 1, s_n, cm_n)
        softmax(s_c, cm_c, p_c, al_c)
        if j >= 1:
            values(j - 1, p_n, al_n)
    values(nk - 1, *bufs[(nk - 1) % 2][2:])

    def normalised(c):
        acc = acc_sc[c]
        return acc[:LANE] / acc[LANE:LANE + 1]

    o = normalised(0)
    if ncomp == 2:
        lam = (jnp.exp(jnp.sum(lq1[...] * lk1[...], keepdims=True))
               - jnp.exp(jnp.sum(lq2[...] * lk2[...], keepdims=True)) + lam_init)
        o = o - lam * normalised(1)
        o = o * lax.rsqrt(jnp.mean(o * o, axis=0, keepdims=True) + RMS_EPS) * g_ref[...] * (1.0 - lam_init)
    o_ref[...] = o.T


def _flash(q, k, vt, *, batch, seq, tq=512, diff=None, lam_init=0.0):
    nk, vrows, tk = vt.shape[2], vt.shape[3], vt.shape[4]
    assert vrows == LANE + ONES_ROWS
    tq = min(tq, seq)
    nq = seq // tq
    ncomp = 2 if diff is not None else 1
    in_specs = [
        pl.BlockSpec((None, None, LANE, tq), lambda b, h, i: (b, h, 0, i)),
        pl.BlockSpec((None, None, seq, LANE), lambda b, h, i: (b, h, 0, 0)),
        pl.BlockSpec((None, None, nk, vrows, tk), lambda b, h, i: (b, h, 0, 0, 0)),
    ]
    args = [q, k, vt]
    if diff is not None:
        lq1, lk1, lq2, lk2, subln = diff
        for v in (lq1, lk1, lq2, lk2):
            in_specs.append(pl.BlockSpec((1, DIFF_HEAD_DIM), lambda b, h, i: (0, 0)))
            args.append(v.reshape(1, DIFF_HEAD_DIM))
        in_specs.append(pl.BlockSpec((LANE, 1), lambda b, h, i: (0, 0)))
        args.append(subln.reshape(LANE, 1))
    return pl.pallas_call(
        functools.partial(_flash_kernel, ncomp=ncomp, nk=nk, lam_init=lam_init),
        grid=(batch, HEADS, nq),
        in_specs=in_specs,
        out_specs=pl.BlockSpec((tq, LANE), lambda b, h, i: (b * nq + i, h)),
        out_shape=jax.ShapeDtypeStruct((batch * seq, HEADS * LANE), F32),
        scratch_shapes=[pltpu.VMEM((ncomp, LANE, tq), BF16),
                        pltpu.VMEM((ncomp, 1, tq), F32), pltpu.VMEM((ncomp, vrows, tq), F32),
                        pltpu.VMEM((ncomp, tk, tq), F32), pltpu.VMEM((ncomp, tk, tq), F32),
                        pltpu.VMEM((ncomp, 1, tq), F32), pltpu.VMEM((ncomp, 1, tq), F32),
                        pltpu.VMEM((ncomp, tk, tq), BF16), pltpu.VMEM((ncomp, tk, tq), BF16),
                        pltpu.VMEM((ncomp, 1, tq), F32), pltpu.VMEM((ncomp, 1, tq), F32)],
        compiler_params=_cparams(("parallel", "parallel", "parallel")),
        name="flash_diff" if diff is not None else "flash_mla",
    )(*args)


def _ret_kernel(dec_ref, qf, kf, vf, qb, kb, vb, of_ref, ob_ref, s_sc, *, chunk):
    @pl.when(pl.program_id(1) == 0)
    def _():
        s_sc[...] = jnp.zeros(s_sc.shape, F32)

    ii = lax.broadcasted_iota(jnp.int32, (chunk, chunk), 0)
    jj = lax.broadcasted_iota(jnp.int32, (chunk, chunk), 1)
    r = lax.broadcasted_iota(jnp.int32, (chunk, 1), 0).astype(F32)
    for d, (q_ref, k_ref, v_ref, o_ref) in enumerate(((qf, kf, vf, of_ref), (qb, kb, vb, ob_ref))):
        for h in range(HEADS):
            la = -jnp.exp(jnp.full((1, 1), dec_ref[d, h], F32))
            if d == 0:
                mask, dist = ii >= jj, (ii - jj).astype(F32)
                qdec, kdec = jnp.exp(la * (r + 1.0)), jnp.exp(la * (chunk - 1.0 - r))
            else:
                mask, dist = jj > ii, (jj - ii).astype(F32)
                qdec, kdec = jnp.exp(la * (chunk - r)), jnp.exp(la * r)
            decay = jnp.where(mask, jnp.exp(jnp.where(mask, dist * la, 0.0)), 0.0)
            q, k, v = q_ref[h], k_ref[h], v_ref[h]
            s = lax.dot_general(q, k, (((1,), (1,)), ((), ())), preferred_element_type=F32)
            o = jnp.dot((s * decay).astype(BF16), v, preferred_element_type=F32)
            state = s_sc[d, h]
            o = o + qdec * jnp.dot(q, state.astype(BF16), preferred_element_type=F32)
            kd = (k.astype(F32) * kdec).astype(BF16)
            s_sc[d, h] = jnp.exp(la * float(chunk)) * state + lax.dot_general(
                kd, v, (((0,), (0,)), ((), ())), preferred_element_type=F32)
            o_ref[:, h * LANE:(h + 1) * LANE] = o


def _retention(q, k, v, decays, *, batch, seq, chunk=256):
    chunk = min(chunk, seq)
    n = seq // chunk
    fwd = pl.BlockSpec((None, HEADS, chunk, LANE), lambda b, c: (b, 0, c, 0))
    bwd = pl.BlockSpec((None, HEADS, chunk, LANE), lambda b, c: (b, 0, n - 1 - c, 0))
    w = HEADS * LANE
    out = jax.ShapeDtypeStruct((batch * seq, w), F32)
    return pl.pallas_call(
        functools.partial(_ret_kernel, chunk=chunk),
        grid=(batch, n),
        in_specs=[pl.BlockSpec(memory_space=pltpu.SMEM), fwd, fwd, fwd, bwd, bwd, bwd],
        out_specs=[pl.BlockSpec((chunk, w), lambda b, c: (b * n + c, 0)),
                   pl.BlockSpec((chunk, w), lambda b, c: (b * n + n - 1 - c, 0))],
        out_shape=[out, out],
        scratch_shapes=[pltpu.VMEM((2, HEADS, LANE, LANE), F32)],
        compiler_params=_cparams(("parallel", "arbitrary")),
        name="retention",
    )(decays, q, k, v, q, k, v)


GLA_SUB = 8


def _split3(x):
    x1 = x.astype(BF16)
    r1 = x - x1.astype(F32)
    x2 = r1.astype(BF16)
    x3 = (r1 - x2.astype(F32)).astype(BF16)
    return x1, x2, x3


def _gla_direction(q, k, v, lr, w2, bias, st, reverse):
    C, wk = q.shape
    wv = v.shape[1]
    dk, dv = wk // HEADS, wv // HEADS
    z = jnp.dot(lr.astype(BF16), w2, preferred_element_type=F32) + bias
    g = (jnp.minimum(z, 0.0) - jnp.log(1.0 + jnp.exp(-jnp.abs(z)))) * (1.0 / GLA_TAU)
    ii = lax.broadcasted_iota(jnp.int32, (C, C), 0)
    jj = lax.broadcasted_iota(jnp.int32, (C, C), 1)
    tri = jnp.where(ii >= jj, 1.0, 0.0).astype(BF16)
    b = sum(jnp.dot(tri, part, preferred_element_type=F32) for part in _split3(g))
    tot = b[C - 1:C, :]
    c = (tot - b + g) if reverse else b

    qe = (q * jnp.exp(jnp.minimum(c, 0.0))).astype(BF16)
    o = lax.dot_general(qe, st.astype(BF16), (((1,), (1,)), ((), ())), preferred_element_type=F32)
    ke = (k * jnp.exp(jnp.minimum(tot - c, 0.0))).astype(BF16)
    upd = lax.dot_general(v.astype(BF16), ke, (((0,), (0,)), ((), ())), preferred_element_type=F32)
    rr = _div_pow2(lax.broadcasted_iota(jnp.int32, (wv, wk), 0), dv)
    cc = _div_pow2(lax.broadcasted_iota(jnp.int32, (wv, wk), 1), dk)
    new_st = jnp.where(rr == cc, st * jnp.exp(tot) + upd, 0.0)

    lane_head = _div_pow2(lax.broadcasted_iota(jnp.int32, (C, wk), 1), dk)
    scores = [jnp.zeros((C, C), F32) for _ in range(HEADS)]
    hsz = C // 2
    while hsz >= GLA_SUB:
        blk = 2 * hsz
        rows = []
        for m in range(C // blk):
            rrow = m * blk + (hsz if reverse else hsz - 1)
            rows.append(jnp.broadcast_to(c[rrow:rrow + 1, :], (blk, wk)))
        ref = jnp.concatenate(rows, axis=0) if len(rows) > 1 else rows[0]
        qt = q * jnp.exp(jnp.minimum(c - ref, 0.0))
        kt = (k * jnp.exp(jnp.minimum(ref - c, 0.0))).astype(BF16)
        same = _div_pow2(ii, blk) == _div_pow2(jj, blk)
        if reverse:
            lvl = same & (_mod_pow2(ii, blk) < hsz) & (_mod_pow2(jj, blk) >= hsz)
        else:
            lvl = same & (_mod_pow2(ii, blk) >= hsz) & (_mod_pow2(jj, blk) < hsz)
        for h in range(HEADS):
            qh = jnp.where(lane_head == h, qt, 0.0).astype(BF16)
            s = lax.dot_general(qh, kt, (((1,), (1,)), ((), ())), preferred_element_type=F32)
            scores[h] = scores[h] + jnp.where(lvl, s, 0.0)
        hsz //= 2

    assert dv == C
    er = _div_pow2(lax.broadcasted_iota(jnp.int32, (wk, wv), 0), dk)
    ec = _div_pow2(lax.broadcasted_iota(jnp.int32, (wk, wv), 1), dv)
    expand = jnp.where(er == ec, 1.0, 0.0).astype(BF16)
    dist = (jj - ii) if reverse else (ii - jj)
    same_sub = _div_pow2(ii, GLA_SUB) == _div_pow2(jj, GLA_SUB)
    for lag in range(1 if reverse else 0, GLA_SUB):
        if lag == 0:
            t = q * k
        else:
            shift = (C - lag) if reverse else lag
            ks, cs = pltpu.roll(k, shift, axis=0), pltpu.roll(c, shift, axis=0)
            t = q * ks * jnp.exp(jnp.minimum(c - cs, 0.0))
        red = jnp.dot(t.astype(BF16), expand, preferred_element_type=F32)
        on_diag = same_sub & (dist == lag)
        for h in range(HEADS):
            scores[h] = scores[h] + jnp.where(on_diag, red[:, h * dv:(h + 1) * dv], 0.0)

    vb = v.astype(BF16)
    o = o + jnp.concatenate(
        [jnp.dot(scores[h].astype(BF16), vb[:, h * dv:(h + 1) * dv], preferred_element_type=F32) for h in range(HEADS)],
        axis=1)
    return o, new_st


def _gla_kernel(qf, kf, vf, lf, qb, kb, vb, lb, w2f, bf, w2b, bb, of_ref, ob_ref, s_sc, *, qscale):
    @pl.when(pl.program_id(1) == 0)
    def _():
        s_sc[...] = jnp.zeros(s_sc.shape, F32)

    o, st = _gla_direction(qf[...] * qscale, kf[...], vf[...], lf[...], w2f[...], bf[...], s_sc[0], False)
    of_ref[...] = o
    s_sc[0] = st
    o, st = _gla_direction(qb[...] * qscale, kb[...], vb[...], lb[...], w2b[...], bb[...], s_sc[1], True)
    ob_ref[...] = o
    s_sc[1] = st


def _gla(z, *, qcol, kcol, vcol, lcol, w2f, bf, w2b, bb, batch, seq, chunk=128):
    chunk = min(chunk, seq)
    n = seq // chunk
    wk, wv = HEADS * GLA_K_DIM, HEADS * LANE

    def specs(cmap):
        return [pl.BlockSpec((chunk, wk), lambda b, c: (cmap(b, c), qcol)),
                pl.BlockSpec((chunk, wk), lambda b, c: (cmap(b, c), kcol)),
                pl.BlockSpec((chunk, wv), lambda b, c: (cmap(b, c), vcol)),
                pl.BlockSpec((chunk, LANE), lambda b, c: (cmap(b, c), lcol))]

    fmap = lambda b, c: b * n + c
    bmap = lambda b, c: b * n + n - 1 - c
    wspec = [pl.BlockSpec((LANE, wk), lambda b, c: (0, 0)), pl.BlockSpec((1, wk), lambda b, c: (0, 0))]
    out = jax.ShapeDtypeStruct((batch * seq, wv), F32)
    return pl.pallas_call(
        functools.partial(_gla_kernel, qscale=GLA_K_DIM ** -0.5),
        grid=(batch, n),
        in_specs=specs(fmap) + specs(bmap) + wspec + wspec,
        out_specs=[pl.BlockSpec((chunk, wv), lambda b, c: (fmap(b, c), 0)),
                   pl.BlockSpec((chunk, wv), lambda b, c: (bmap(b, c), 0))],
        out_shape=[out, out],
        scratch_shapes=[pltpu.VMEM((2, wv, wk), F32)],
        compiler_params=_cparams(("parallel", "arbitrary")),
        name="gla",
    )(z, z, z, z, z, z, z, z, w2f, bf, w2b, bb)


def _layer_norm_rows(r, g, b):
    mu = jnp.mean(r, axis=-1, keepdims=True)
    d = r - mu
    var = jnp.mean(d * d, axis=-1, keepdims=True)
    return d * lax.rsqrt(var + LN_EPS) * g + b


def _outproj_kernel(x_ref, fin_ref, of_ref, ob_ref, gate_ref, ng_ref, wa_ref, wb_ref, lg_ref, lb_ref, o_ref, *, group_norm):
    lin = of_ref[...] + ob_ref[...]
    parts = []
    for h in range(HEADS):
        zh = lin[:, h * LANE:(h + 1) * LANE]
        if group_norm:
            mu = jnp.mean(zh, axis=-1, keepdims=True)
            dz = zh - mu
            parts.append(dz * lax.rsqrt(jnp.mean(dz * dz, axis=-1, keepdims=True) + LN_EPS))
        else:
            parts.append(zh * lax.rsqrt(jnp.mean(zh * zh, axis=-1, keepdims=True) + RMS_EPS) * ng_ref[...])
    gate = gate_ref[...]
    lin = jnp.concatenate(parts, axis=1) * (gate * jax.nn.sigmoid(gate))
    y = (jnp.dot(fin_ref[...].astype(BF16), wa_ref[...], preferred_element_type=F32)
         + jnp.dot(lin.astype(BF16), wb_ref[...], preferred_element_type=F32))
    o_ref[...] = _layer_norm_rows(ALPHA * x_ref[...] + y, lg_ref[...], lb_ref[...])


def _outproj(x, fin, of, ob, gate_src, gate_col, norm_gain, wa, wb, ln_g, ln_b, *, group_norm, tm=512):
    n, d = x.shape
    w = HEADS * LANE
    tm = min(tm, n)
    row = lambda i: (i, 0)
    const = lambda i: (0, 0)
    return pl.pallas_call(
        functools.partial(_outproj_kernel, group_norm=group_norm),
        grid=(n // tm,),
        in_specs=[pl.BlockSpec((tm, d), row), pl.BlockSpec((tm, w), row), pl.BlockSpec((tm, w), row),
                  pl.BlockSpec((tm, w), row), pl.BlockSpec((tm, w), lambda i: (i, gate_col)),
                  pl.BlockSpec((1, LANE), const), pl.BlockSpec((w, d), const), pl.BlockSpec((w, d), const),
                  pl.BlockSpec((1, d), const), pl.BlockSpec((1, d), const)],
        out_specs=pl.BlockSpec((tm, d), row),
        out_shape=jax.ShapeDtypeStruct((n, d), F32),
        compiler_params=_cparams(("parallel",)),
        name="outproj",
    )(x, fin, of, ob, gate_src, norm_gain.reshape(1, LANE), wa, wb, ln_g.reshape(1, d), ln_b.reshape(1, d))


def _route_kernel(x_ref, w_ref, b_ref, ids_ref, gw_ref, cnt_ref, run_sc):
    @pl.when(pl.program_id(0) == 0)
    def _():
        run_sc[...] = jnp.zeros(run_sc.shape, F32)

    tm = x_ref.shape[0]
    x = x_ref[...]
    xh = x.astype(BF16)
    xl = (x - xh.astype(F32)).astype(BF16)
    wh, wl = w_ref[0], w_ref[1]
    logits = (jnp.dot(xh, wh, preferred_element_type=F32) + jnp.dot(xh, wl, preferred_element_type=F32)
              + jnp.dot(xl, wh, preferred_element_type=F32)) + b_ref[...]
    lane = lax.broadcasted_iota(jnp.int32, logits.shape, 1)
    neg = -jnp.inf
    gmask = (lane >= N_EXPERTS) & (lane < N_EXPERTS + N_GROUPS)
    gl = jnp.where(gmask, logits, neg)
    gmax = jnp.max(gl, axis=1, keepdims=True)
    lane_f = lane.astype(F32)
    first = lambda hit: jnp.min(jnp.where(hit, lane_f, float(LANE)), axis=1, keepdims=True).astype(jnp.int32)
    gidx = first(gl == gmax) - N_EXPERTS
    p_grp = 1.0 / jnp.sum(jnp.where(gmask, jnp.exp(gl - gmax), 0.0), axis=1, keepdims=True)
    el = jnp.where(_div_pow2(lane, EXPERTS_PER_GROUP) == gidx, logits, neg)
    l1 = jnp.max(el, axis=1, keepdims=True)
    e1 = first(el == l1)
    el2 = jnp.where(lane == e1, neg, el)
    l2 = jnp.max(el2, axis=1, keepdims=True)
    e2 = first(el2 == l2)
    t = jnp.exp(l2 - l1)
    w1 = p_grp / (1.0 + t)
    w2 = p_grp * t / (1.0 + t)

    onehot = jnp.where(lane == e1, 1.0, jnp.where(lane == e2, 1.0, 0.0))
    ri = lax.broadcasted_iota(jnp.int32, (tm, tm), 0)
    ci = lax.broadcasted_iota(jnp.int32, (tm, tm), 1)
    before = jnp.dot(jnp.where(ri > ci, 1.0, 0.0).astype(BF16), onehot.astype(BF16), preferred_element_type=F32)
    before = before + run_sc[...]
    r1 = jnp.sum(jnp.where(lane == e1, before, 0.0), axis=1, keepdims=True).astype(jnp.int32)
    r2 = jnp.sum(jnp.where(lane == e2, before, 0.0), axis=1, keepdims=True).astype(jnp.int32)
    run_sc[...] = run_sc[...] + jnp.sum(onehot, axis=0, keepdims=True)
    cnt_ref[...] = run_sc[...].astype(jnp.int32)
    ids_ref[...] = jnp.where(lane == 0, e1, jnp.where(lane == 1, e2, jnp.where(lane == 2, r1, jnp.where(lane == 3, r2, 0))))
    gw_ref[...] = jnp.where(lane == 0, w1, jnp.where(lane == 1, w2, 0.0))


def _route(x, w, b, *, tm=512):
    n, d = x.shape
    tm = min(tm, n)
    w_hi = w.astype(BF16)
    w = jnp.stack([w_hi, (w - w_hi.astype(F32)).astype(BF16)])
    return pl.pallas_call(
        _route_kernel,
        grid=(n // tm,),
        in_specs=[pl.BlockSpec((tm, d), lambda i: (i, 0)), pl.BlockSpec((2, d, LANE), lambda i: (0, 0, 0)),
                  pl.BlockSpec((1, LANE), lambda i: (0, 0))],
        out_specs=[pl.BlockSpec((tm, LANE), lambda i: (i, 0)), pl.BlockSpec((tm, LANE), lambda i: (i, 0)),
                   pl.BlockSpec((1, LANE), lambda i: (0, 0))],
        out_shape=[jax.ShapeDtypeStruct((n, LANE), jnp.int32), jax.ShapeDtypeStruct((n, LANE), F32),
                   jax.ShapeDtypeStruct((1, LANE), jnp.int32)],
        scratch_shapes=[pltpu.VMEM((1, LANE), F32)],
        compiler_params=_cparams(("arbitrary",)),
        name="route",
    )(x, w, b)


MOE_BM = 256
SUBL = 8


def _rows_from_linear(ref, rows):
    return jnp.concatenate([ref[pl.ds(s, rows, stride=SUBL), :] for s in range(SUBL)], axis=1)


def _rows_to_linear(ref, val):
    for s in range(SUBL):
        ref[pl.ds(s, val.shape[0], stride=SUBL), :] = val[:, s * LANE:(s + 1) * LANE]


def _dispatch_kernel(pend_ref, padded_ref, dest_ref, x_ref, xs_hbm, idx_smem, zbuf, lin, sem_i, sem_z, sem):
    i = pl.program_id(0)
    tm = x_ref.shape[0]
    bm = zbuf.shape[0] // SUBL

    @pl.when(i == 0)
    def _():
        zbuf[...] = jnp.zeros(zbuf.shape, F32)

        def tail(e):
            start_row = pl.multiple_of((pend_ref[e] - bm) * SUBL, bm * SUBL)
            return pltpu.make_async_copy(zbuf, xs_hbm.at[pl.ds(start_row, bm * SUBL), :], sem_z)

        def start(e, carry):
            @pl.when(padded_ref[e] > 0)
            def _():
                tail(e).start()
            return carry

        def wait(e, carry):
            @pl.when(padded_ref[e] > 0)
            def _():
                tail(e).wait()
            return carry

        lax.fori_loop(0, N_EXPERTS, start, 0)
        lax.fori_loop(0, N_EXPERTS, wait, 0)

        def unused(b):
            start_row = pl.multiple_of(b * bm * SUBL, bm * SUBL)
            return pltpu.make_async_copy(zbuf, xs_hbm.at[pl.ds(start_row, bm * SUBL), :], sem_z)

        first_unused = pend_ref[N_EXPERTS - 1] // bm
        n_blocks = xs_hbm.shape[0] // (bm * SUBL)
        lax.fori_loop(first_unused, n_blocks, lambda b, c: (unused(b).start(), c)[1], 0)
        lax.fori_loop(first_unused, n_blocks, lambda b, c: (unused(b).wait(), c)[1], 0)

    cp = pltpu.make_async_copy(dest_ref.at[i], idx_smem, sem_i)
    cp.start()
    _rows_to_linear(lin, x_ref[...])
    cp.wait()

    def scatter(r, carry):
        src = lin.at[pl.ds(pl.multiple_of(r * SUBL, SUBL), SUBL), :]
        for k in range(2):
            dst = pl.multiple_of(idx_smem[2 * r + k] * SUBL, SUBL)
            pltpu.make_async_copy(src, xs_hbm.at[pl.ds(dst, SUBL), :], sem).start(priority=k)
        return carry

    lax.fori_loop(0, tm, scatter, 0, unroll=8)
    for k in range(2):
        pltpu.make_async_copy(lin, xs_hbm.at[pl.ds(0, tm * SUBL), :], sem).wait()


def _dispatch(x, dest, pend, padded, cap, *, tm=512):
    n, d = x.shape
    assert d == SUBL * LANE
    tm = min(tm, n)
    nt = n // tm
    grid_spec = pltpu.PrefetchScalarGridSpec(
        num_scalar_prefetch=2,
        grid=(nt,),
        in_specs=[pl.BlockSpec((nt, 2 * tm), lambda i, pe, pa: (0, 0)),
                  pl.BlockSpec((tm, d), lambda i, pe, pa: (i, 0))],
        out_specs=pl.BlockSpec(memory_space=pl.ANY),
        scratch_shapes=[pltpu.SMEM((2 * tm,), jnp.int32), pltpu.VMEM((MOE_BM * SUBL, LANE), F32),
                        pltpu.VMEM((tm * SUBL, LANE), F32),
                        pltpu.SemaphoreType.DMA(()), pltpu.SemaphoreType.DMA(()), pltpu.SemaphoreType.DMA(())],
    )
    return pl.pallas_call(
        _dispatch_kernel,
        grid_spec=grid_spec,
        out_shape=jax.ShapeDtypeStruct((cap * SUBL, LANE), F32),
        compiler_params=_cparams(("arbitrary",)),
        name="dispatch",
    )(pend, padded, dest.reshape(nt, 2 * tm), x)


def _experts_kernel(blk_e_ref, nused_ref, xs_ref, wg_ref, wu_ref, wd_ref, ys_ref, wgb, wub, wdb):
    i = pl.program_id(0)

    @pl.when(i < nused_ref[0])
    def _():
        prev = blk_e_ref[jnp.maximum(i - 1, 0)]

        @pl.when(jnp.logical_or(i == 0, blk_e_ref[i] != prev))
        def _():
            wgb[...] = wg_ref[...].astype(BF16)
            wub[...] = wu_ref[...].astype(BF16)
            wdb[...] = wd_ref[...].astype(BF16)

        xb = _rows_from_linear(xs_ref, xs_ref.shape[0] // SUBL).astype(BF16)
        hg = jnp.dot(xb, wgb[...], preferred_element_type=F32)
        hu = jnp.dot(xb, wub[...], preferred_element_type=F32)
        hid = (hg * jax.nn.sigmoid(hg) * hu).astype(BF16)
        _rows_to_linear(ys_ref, jnp.dot(hid, wdb[...], preferred_element_type=F32))

    @pl.when(i >= nused_ref[0])
    def _():
        ys_ref[...] = jnp.zeros(ys_ref.shape, F32)


def _experts(xs, blk_e, n_used, w_gate, w_up, w_down, layer):
    cap = xs.shape[0] // SUBL
    bm = MOE_BM
    d, de = w_gate.shape[2], w_gate.shape[3]
    row_in = lambda i, be, nu: (jnp.minimum(i, nu[0] - 1), 0)
    row = lambda i, be, nu: (i, 0)
    grid_spec = pltpu.PrefetchScalarGridSpec(
        num_scalar_prefetch=2,
        grid=(cap // bm,),
        in_specs=[pl.BlockSpec((bm * SUBL, LANE), row_in),
                  pl.BlockSpec((None, None, d, de), lambda i, be, nu: (layer, be[i], 0, 0)),
                  pl.BlockSpec((None, None, d, de), lambda i, be, nu: (layer, be[i], 0, 0)),
                  pl.BlockSpec((None, None, de, d), lambda i, be, nu: (layer, be[i], 0, 0))],
        out_specs=pl.BlockSpec((bm * SUBL, LANE), row),
        scratch_shapes=[pltpu.VMEM((d, de), BF16), pltpu.VMEM((d, de), BF16), pltpu.VMEM((de, d), BF16)],
    )
    return pl.pallas_call(
        _experts_kernel,
        grid_spec=grid_spec,
        out_shape=jax.ShapeDtypeStruct((cap * SUBL, LANE), F32),
        compiler_params=_cparams(("arbitrary",)),
        name="experts",
    )(blk_e, n_used, xs, w_gate, w_up, w_down)


def _combine_kernel(dest_ref, x_ref, gw_ref, g_ref, b_ref, ys_hbm, o_ref, idx_smem, ybuf, sem_i, sem):
    i = pl.program_id(0)
    tm = x_ref.shape[0]
    cp = pltpu.make_async_copy(dest_ref.at[i], idx_smem, sem_i)
    cp.start()
    cp.wait()

    def gather(r, carry):
        row = pl.multiple_of(r * SUBL, SUBL)
        for k in range(2):
            src = pl.multiple_of(idx_smem[2 * r + k] * SUBL, SUBL)
            pltpu.make_async_copy(ys_hbm.at[pl.ds(src, SUBL), :], ybuf.at[k, pl.ds(row, SUBL), :], sem).start(priority=k)
        return carry

    lax.fori_loop(0, tm, gather, 0, unroll=8)
    for k in range(2):
        pltpu.make_async_copy(ys_hbm.at[pl.ds(0, tm * SUBL), :], ybuf.at[k], sem).wait()
    gw = gw_ref[...]
    ffn = _rows_from_linear(ybuf.at[0], tm) * gw[:, 0:1] + _rows_from_linear(ybuf.at[1], tm) * gw[:, 1:2]
    o_ref[...] = _layer_norm_rows(ALPHA * x_ref[...] + ffn, g_ref[...], b_ref[...])


def _combine(x, ys, dest, gw, ln_g, ln_b, *, tm=512):
    n, d = x.shape
    tm = min(tm, n)
    nt = n // tm
    return pl.pallas_call(
        _combine_kernel,
        grid=(nt,),
        in_specs=[pl.BlockSpec((nt, 2 * tm), lambda i: (0, 0)),
                  pl.BlockSpec((tm, d), lambda i: (i, 0)), pl.BlockSpec((tm, LANE), lambda i: (i, 0)),
                  pl.BlockSpec((1, d), lambda i: (0, 0)), pl.BlockSpec((1, d), lambda i: (0, 0)),
                  pl.BlockSpec(memory_space=pl.ANY)],
        out_specs=pl.BlockSpec((tm, d), lambda i: (i, 0)),
        out_shape=jax.ShapeDtypeStruct((n, d), F32),
        scratch_shapes=[pltpu.SMEM((2 * tm,), jnp.int32), pltpu.VMEM((2, tm * SUBL, LANE), F32),
                        pltpu.SemaphoreType.DMA(()), pltpu.SemaphoreType.DMA(())],
        compiler_params=_cparams(("arbitrary",)),
        name="combine",
    )(dest.reshape(nt, 2 * tm), x, gw, ln_g.reshape(1, d), ln_b.reshape(1, d), ys)


def _moe(x, w_grp, b_grp, w_exp, b_exp, w_gate, w_up, w_down, layer, ln_g, ln_b):
    n, d = x.shape
    wr = jnp.zeros((d, LANE), F32).at[:, :N_EXPERTS].set(w_exp).at[:, N_EXPERTS:N_EXPERTS + N_GROUPS].set(w_grp)
    br = jnp.zeros((1, LANE), F32).at[0, :N_EXPERTS].set(b_exp).at[0, N_EXPERTS:N_EXPERTS + N_GROUPS].set(b_grp)
    ids, gw, cnt = _route(x, wr, br)
    bm = MOE_BM
    counts = cnt[0, :N_EXPERTS]
    padded = (counts + bm - 1) // bm * bm
    pend = jnp.cumsum(padded)
    pstart = pend - padded
    e, r = ids[:, 0:2], ids[:, 2:4]
    onehot = e[:, :, None] == jnp.arange(N_EXPERTS, dtype=jnp.int32)[None, None, :]
    dest = jnp.sum(jnp.where(onehot, pstart[None, None, :], 0), axis=-1) + r
    cap = 2 * n + N_EXPERTS * bm
    nb = cap // bm
    blk_start = jnp.arange(nb, dtype=jnp.int32) * bm
    blk_e = jnp.minimum(jnp.sum((pend[None, :] <= blk_start[:, None]).astype(jnp.int32), axis=1), N_EXPERTS - 1)
    n_used = (pend[-1:] // bm).astype(jnp.int32)
    xs = _dispatch(x, dest, pend.astype(jnp.int32), padded.astype(jnp.int32), cap)
    ys = _experts(xs, blk_e, n_used, w_gate, w_up, w_down, layer)
    return _combine(x, ys, dest, gw, ln_g, ln_b)


def _even_layer(x, batch, seq, layer_idx, w_in, dec_f, dec_b, lq1, lk1, lq2, lk2, subln, w_out, ln_g, ln_b):
    d = x.shape[1]
    w = HEADS * LANE
    kw = dict(batch=batch, seq=seq)
    diff_seg = [(0, DIFF_ROT_DIM), (DIFF_HEAD_DIM, DIFF_ROT_DIM)]
    q, k, v, gate, dq, dk, dvt = _even_in(
        x, w_in.astype(BF16), _rope_tables(seq, [(0, LANE)], RET_THETA, 1.0),
        _rope_tables(seq, diff_seg, ROPE_THETA, 1.0), **kw)
    decays = jnp.stack([dec_f, dec_b]).astype(F32)
    of, ob = _retention(q, k, v, decays, **kw)
    lam_init = 0.8 - 0.6 * math.exp(-0.3 * layer_idx)
    diff = _flash(dq, dk, dvt, diff=(lq1, lk1, lq2, lk2, subln), lam_init=lam_init, **kw)
    wo = w_out.astype(BF16)
    return _outproj(x, diff, of, ob, gate, 0, jnp.ones((LANE,), F32), wo[w:], wo[:w], ln_g, ln_b, group_norm=True)


def _odd_layer(x, batch, seq, w_in, q_norm, w_uq, kv_norm, w_ukv, w2_f, b_f, w2_b, b_b, gla_norm, w_out, ln_g, ln_b):
    d = x.shape[1]
    w = HEADS * LANE
    o = np.cumsum([0, MLA_Q_RANK, MLA_KV_RANK, MLA_ROPE, HEADS * GLA_K_DIM, HEADS * GLA_K_DIM, w, w,
                   GLA_GATE_RANK, GLA_GATE_RANK]).tolist()
    zeros = lambda c: jnp.zeros((d, c), F32)
    w_in2 = jnp.concatenate([
        w_in[:, o[0]:o[2]], zeros(MLA_NOPE), w_in[:, o[2]:o[3]], zeros(LANE - MLA_NOPE - MLA_ROPE),
        w_in[:, o[3]:o[7]], w_in[:, o[7]:o[9]], zeros(LANE - 2 * GLA_GATE_RANK)], axis=1).astype(BF16)
    kw = dict(batch=batch, seq=seq)
    qd = MLA_NOPE + MLA_ROPE
    w_uq2 = jnp.pad(w_uq.reshape(MLA_Q_RANK, HEADS, qd), ((0, 0), (0, 0), (0, LANE - qd))).reshape(MLA_Q_RANK, w)
    ukv = w_ukv.reshape(MLA_KV_RANK, HEADS, MLA_NOPE + MLA_V)
    w_uk2 = jnp.pad(ukv[:, :, :MLA_NOPE], ((0, 0), (0, 0), (0, LANE - MLA_NOPE))).reshape(MLA_KV_RANK, w)
    w_uv2 = ukv[:, :, MLA_NOPE:].reshape(MLA_KV_RANK, w)
    q, k, vt, zg = _odd_in(x, w_in2, q_norm, w_uq2.astype(BF16), kv_norm,
                           jnp.concatenate([w_uk2, w_uv2], axis=1).astype(BF16),
                           _rope_tables(seq, [(MLA_NOPE, MLA_ROPE)], ROPE_THETA, 1.0), **kw)
    mla = _flash(q, k, vt, **kw)
    wk = HEADS * GLA_K_DIM
    pad_rows = lambda m, r0: jnp.zeros((LANE, wk), F32).at[r0:r0 + GLA_GATE_RANK].set(m).astype(BF16)
    of, ob = _gla(zg, qcol=0, kcol=1, vcol=1, lcol=12,
                  w2f=pad_rows(w2_f, 0), bf=b_f.reshape(1, wk), w2b=pad_rows(w2_b, GLA_GATE_RANK), bb=b_b.reshape(1, wk), **kw)
    wo = w_out.astype(BF16)
    return _outproj(x, mla, of, ob, zg, 2, gla_norm, wo[:w], wo[w:], ln_g, ln_b, group_norm=False)


def kernel(x, ev_w_in, ev_ret_decay_f, ev_ret_decay_b, ev_lq1, ev_lk1, ev_lq2, ev_lk2, ev_subln, ev_w_out, od_w_in, od_q_norm, od_w_uq, od_kv_norm, od_w_ukv, od_gla_w2_f, od_gla_b_f, od_gla_w2_b, od_gla_b_b, od_gla_norm, od_w_out, ln1_g, ln1_b, ln2_g, ln2_b, moe_w_grp, moe_b_grp, moe_w_exp, moe_b_exp, moe_w_gate, moe_w_up, moe_w_down):
    batch, seq, d = x.shape
    h = x.reshape(batch * seq, d)
    for i in range(DEPTH):
        j = i // 2
        if i % 2 == 0:
            h = _even_layer(h, batch, seq, i, ev_w_in[j], ev_ret_decay_f[j], ev_ret_decay_b[j], ev_lq1[j], ev_lk1[j],
                            ev_lq2[j], ev_lk2[j], ev_subln[j], ev_w_out[j], ln1_g[i], ln1_b[i])
        else:
            h = _odd_layer(h, batch, seq, od_w_in[j], od_q_norm[j], od_w_uq[j], od_kv_norm[j], od_w_ukv[j],
                           od_gla_w2_f[j], od_gla_b_f[j], od_gla_w2_b[j], od_gla_b_b[j], od_gla_norm[j], od_w_out[j],
                           ln1_g[i], ln1_b[i])
        h = _moe(h, moe_w_grp[i], moe_b_grp[i], moe_w_exp[i], moe_b_exp[i], moe_w_gate, moe_w_up, moe_w_down, i,
                 ln2_g[i], ln2_b[i])
    return h.reshape(batch, seq, d)
```

```python
import functools
import math

import numpy as np
import jax
import jax.numpy as jnp
from jax import lax
from jax.experimental import pallas as pl
from jax.experimental.pallas import tpu as pltpu

F32 = jnp.float32
BF16 = jnp.bfloat16

HEADS = 4
LANE = 128
RET_THETA = 10000.0
ROPE_THETA = 500000.0
DIFF_HEAD_DIM = 64
DIFF_ROT_DIM = 16
MLA_Q_RANK = 256
MLA_KV_RANK = 128
MLA_NOPE = 64
MLA_ROPE = 32
MLA_V = 128
GLA_K_DIM = 64
GLA_GATE_RANK = 16
GLA_TAU = 16.0
N_GROUPS = 4
EXPERTS_PER_GROUP = 8
N_EXPERTS = N_GROUPS * EXPERTS_PER_GROUP
DEPTH = 2
ALPHA = (2.0 * DEPTH) ** 0.25
LN_EPS = 1e-5
RMS_EPS = 1e-6

VMEM_LIMIT = 48 * 1024 * 1024


def _div_pow2(x, n):
    return lax.shift_right_logical(x, int(n).bit_length() - 1)


def _mod_pow2(x, n):
    return lax.bitwise_and(x, int(n) - 1)


def _cparams(sem):
    return pltpu.CompilerParams(dimension_semantics=sem, vmem_limit_bytes=VMEM_LIMIT)


def _rope_heads(z, tabs, sh, scale):
    c, sa, sb = tabs
    outs = []
    for h in range(HEADS):
        zh = z[:, h * LANE:(h + 1) * LANE]
        outs.append((zh * c + pltpu.roll(zh, sh, axis=1) * sa + pltpu.roll(zh, LANE - sh, axis=1) * sb) * scale)
    return outs


def _even_in_kernel(x_ref, w_ref, rc, rsa, rsb, dc, dsa, dsb, q_ref, k_ref, v_ref, g_ref, dq_ref, dk_ref, dvt_ref):
    w = HEADS * LANE
    xb = x_ref[...].astype(BF16)
    part = lambda t: jnp.dot(xb, w_ref[:, t * w:(t + 1) * w], preferred_element_type=F32)
    ret_t = (rc[...], rsa[...], rsb[...])
    diff_t = (dc[...], dsa[...], dsb[...])
    for h, o in enumerate(_rope_heads(part(0), ret_t, LANE // 2, 1.0)):
        q_ref[h] = o.astype(BF16)
    for h, o in enumerate(_rope_heads(part(1), ret_t, LANE // 2, LANE ** -0.5)):
        k_ref[h] = o.astype(BF16)
    rv = part(2)
    for h in range(HEADS):
        v_ref[h] = rv[:, h * LANE:(h + 1) * LANE].astype(BF16)
    g_ref[...] = part(3)
    for h, o in enumerate(_rope_heads(part(4), diff_t, DIFF_ROT_DIM // 2, DIFF_HEAD_DIM ** -0.5 * LOG2E)):
        dq_ref[h] = o.T.astype(BF16)
    for h, o in enumerate(_rope_heads(part(5), diff_t, DIFF_ROT_DIM // 2, 1.0)):
        dk_ref[h] = o.astype(BF16)
    dv = part(6)
    for h in range(HEADS):
        dvt_ref[h, :LANE, :] = dv[:, h * LANE:(h + 1) * LANE].T.astype(BF16)
        dvt_ref[h, LANE:, :] = jnp.ones((ONES_ROWS, dvt_ref.shape[2]), BF16)


def _even_in(x, w, ret_tabs, diff_tabs, *, batch, seq):
    n, d = x.shape
    tm = min(FLASH_TK_DIFF, seq // 2)
    nt = seq // tm
    hw = HEADS * LANE
    heads = jax.ShapeDtypeStruct((batch, HEADS, seq, LANE), BF16)
    head_spec = pl.BlockSpec((None, HEADS, tm, LANE), lambda i: (i // nt, 0, i % nt, 0))
    heads_t = jax.ShapeDtypeStruct((batch, HEADS, LANE, seq), BF16)
    head_t_spec = pl.BlockSpec((None, HEADS, LANE, tm), lambda i: (i // nt, 0, 0, i % nt))
    tab_spec = pl.BlockSpec((tm, LANE), lambda i: (i % nt, 0))
    return pl.pallas_call(
        _even_in_kernel,
        grid=(n // tm,),
        in_specs=[pl.BlockSpec((tm, d), lambda i: (i, 0)), pl.BlockSpec((d, 7 * hw), lambda i: (0, 0))] + [tab_spec] * 6,
        out_specs=[head_spec, head_spec, head_spec, pl.BlockSpec((tm, hw), lambda i: (i, 0)), head_t_spec, head_spec,
                   pl.BlockSpec((None, HEADS, None, LANE + ONES_ROWS, tm), lambda i: (i // nt, 0, i % nt, 0, 0))],
        out_shape=[heads, heads, heads, jax.ShapeDtypeStruct((n, hw), F32), heads_t, heads,
                   jax.ShapeDtypeStruct((batch, HEADS, nt, LANE + ONES_ROWS, tm), BF16)],
        compiler_params=_cparams(("parallel",)),
        name="even_in",
    )(x, w, *ret_tabs, *diff_tabs)


def _rms_rows(z, g):
    return z * lax.rsqrt(jnp.mean(z * z, axis=-1, keepdims=True) + RMS_EPS) * g


def _odd_in_kernel(x_ref, w_ref, qn_ref, wq_ref, kvn_ref, wkv_ref, tc, tsa, tsb, q_ref, k_ref, vt_ref, zg_ref):
    hw = HEADS * LANE
    mla_w = MLA_Q_RANK + MLA_KV_RANK + LANE
    xb = x_ref[...].astype(BF16)
    zg_ref[...] = jnp.dot(xb, w_ref[:, mla_w:], preferred_element_type=F32)
    z1 = jnp.dot(xb, w_ref[:, :mla_w], preferred_element_type=F32)
    tabs = (tc[...], tsa[...], tsb[...])
    sh = MLA_ROPE // 2
    qh = jnp.dot(_rms_rows(z1[:, :MLA_Q_RANK], qn_ref[...]).astype(BF16), wq_ref[...], preferred_element_type=F32)
    for h, o in enumerate(_rope_heads(qh, tabs, sh, (MLA_NOPE + MLA_ROPE) ** -0.5 * LOG2E)):
        q_ref[h] = o.T.astype(BF16)
    ckv = _rms_rows(z1[:, MLA_Q_RANK:MLA_Q_RANK + MLA_KV_RANK], kvn_ref[...]).astype(BF16)
    kv = jnp.dot(ckv, wkv_ref[...], preferred_element_type=F32)
    kr = z1[:, MLA_Q_RANK + MLA_KV_RANK:]
    kr = kr * tabs[0] + pltpu.roll(kr, sh, axis=1) * tabs[1] + pltpu.roll(kr, LANE - sh, axis=1) * tabs[2]
    for h in range(HEADS):
        k_ref[h] = (kv[:, h * LANE:(h + 1) * LANE] + kr).astype(BF16)
        vt_ref[h, :LANE, :] = kv[:, hw + h * LANE:hw + (h + 1) * LANE].T.astype(BF16)
        vt_ref[h, LANE:, :] = jnp.ones((ONES_ROWS, vt_ref.shape[2]), BF16)


def _odd_in(x, w, q_norm, w_uq, kv_norm, w_ukv, tabs, *, batch, seq, tm=512):
    n, d = x.shape
    tm = min(tm, seq // 2)
    nt = seq // tm
    tk = min(FLASH_TK, seq // 2)
    per = tk // tm
    hw = HEADS * LANE
    gw_ = w.shape[1] - (MLA_Q_RANK + MLA_KV_RANK + LANE)
    heads = jax.ShapeDtypeStruct((batch, HEADS, seq, LANE), BF16)
    head_spec = pl.BlockSpec((None, HEADS, tm, LANE), lambda i: (i // nt, 0, i % nt, 0))
    tab_spec = pl.BlockSpec((tm, LANE), lambda i: (i % nt, 0))
    const = lambda i: (0, 0)
    return pl.pallas_call(
        _odd_in_kernel,
        grid=(n // tm,),
        in_specs=[pl.BlockSpec((tm, d), lambda i: (i, 0)), pl.BlockSpec(w.shape, const),
                  pl.BlockSpec((1, MLA_Q_RANK), const), pl.BlockSpec(w_uq.shape, const),
                  pl.BlockSpec((1, MLA_KV_RANK), const), pl.BlockSpec(w_ukv.shape, const)] + [tab_spec] * 3,
        out_specs=[pl.BlockSpec((None, HEADS, LANE, tm), lambda i: (i // nt, 0, 0, i % nt)), head_spec,
                   pl.BlockSpec((None, HEADS, None, LANE + ONES_ROWS, tm),
                                lambda i: (i // nt, 0, (i % nt) // per, 0, (i % nt) % per)),
                   pl.BlockSpec((tm, gw_), lambda i: (i, 0))],
        out_shape=[jax.ShapeDtypeStruct((batch, HEADS, LANE, seq), BF16), heads, jax.ShapeDtypeStruct((batch, HEADS, seq // tk, LANE + ONES_ROWS, tk), BF16),
                   jax.ShapeDtypeStruct((n, gw_), F32)],
        compiler_params=_cparams(("parallel",)),
        name="odd_in",
    )(x, w, q_norm.reshape(1, -1), w_uq, kv_norm.reshape(1, -1), w_ukv, *tabs)


def _rope_tables(seq, segs, theta):
    pos = jnp.arange(seq, dtype=F32)[:, None]
    ones = lambda w: jnp.ones((seq, w), F32)
    zeros = lambda w: jnp.zeros((seq, w), F32)
    c, sa, sb, lane = [], [], [], 0
    for start, rot in segs:
        half = rot // 2
        ang = pos * jnp.power(jnp.float32(theta), -jnp.arange(0, rot, 2, dtype=F32) / rot)[None, :]
        cos, sin = jnp.cos(ang), jnp.sin(ang)
        c += [ones(start - lane), cos, cos]
        sa += [zeros(start - lane), zeros(half), sin]
        sb += [zeros(start - lane), -sin, zeros(half)]
        lane = start + rot
    c, sa, sb = c + [ones(LANE - lane)], sa + [zeros(LANE - lane)], sb + [zeros(LANE - lane)]
    cat = lambda parts: jnp.concatenate([p for p in parts if p.shape[1]], axis=1)
    return cat(c), cat(sa), cat(sb)


ONES_ROWS = 16
LOG2E = math.log2(math.e)
FLASH_TK = 1024
FLASH_TK_DIFF = 512


def _flash_kernel(*refs, ncomp, nk, lam_init):
    if ncomp == 2:
        q_ref, k_ref, vt_ref, lq1, lk1, lq2, lk2, g_ref, o_ref, *scr = refs
    else:
        q_ref, k_ref, vt_ref, o_ref, *scr = refs
    qm_sc, m_sc, acc_sc, s0, s1, cm0, cm1, p0, p1, al0, al1 = scr
    tk = s0.shape[1]
    q = q_ref[...]
    if ncomp == 2:
        chan = lax.broadcasted_iota(jnp.int32, q.shape, 0)
        zero = jnp.zeros_like(q)
        qm_sc[0] = jnp.where(chan < DIFF_HEAD_DIM, q, zero)
        qm_sc[1] = jnp.where(chan >= DIFF_HEAD_DIM, q, zero)
    else:
        qm_sc[0] = q
    m_sc[...] = jnp.full(m_sc.shape, -jnp.inf, F32)
    acc_sc[...] = jnp.zeros(acc_sc.shape, F32)

    def scores(j, s_ref, cm_ref):
        k = k_ref[j * tk:(j + 1) * tk, :]
        for c in range(ncomp):
            s = jnp.dot(k, qm_sc[c], preferred_element_type=F32)
            s_ref[c] = s
            cm_ref[c] = jnp.max(s, axis=0, keepdims=True)

    def softmax(s_ref, cm_ref, p_ref, al_ref):
        for c in range(ncomp):
            m_old = m_sc[c]
            m_new = jnp.maximum(m_old, cm_ref[c])
            al_ref[c] = jnp.exp2(m_old - m_new)
            p_ref[c] = jnp.exp2(s_ref[c] - m_new).astype(BF16)
            m_sc[c] = m_new

    def values(j, p_ref, al_ref):
        vt = vt_ref[j]
        for c in range(ncomp):
            acc_sc[c] = al_ref[c] * acc_sc[c] + jnp.dot(vt, p_ref[c], preferred_element_type=F32)

    bufs = ((s0, cm0, p0, al0), (s1, cm1, p1, al1))
    scores(0, s0, cm0)
    for j in range(nk):
        s_c, cm_c, p_c, al_c = bufs[j % 2]
        s_n, cm_n, p_n, al_n = bufs[(j + 1) % 2]
        if j + 1 < nk:
            scores(j + 1, s_n, cm_n)
        softmax(s_c, cm_c, p_c, al_c)
        if j >= 1:
            values(j - 1, p_n, al_n)
    values(nk - 1, *bufs[(nk - 1) % 2][2:])

    def normalised(c):
        acc = acc_sc[c]
        return acc[:LANE] / acc[LANE:LANE + 1]

    o = normalised(0)
    if ncomp == 2:
        lam = (jnp.exp(jnp.sum(lq1[...] * lk1[...], keepdims=True))
               - jnp.exp(jnp.sum(lq2[...] * lk2[...], keepdims=True)) + lam_init)
        o = o - lam * normalised(1)
        o = o * lax.rsqrt(jnp.mean(o * o, axis=0, keepdims=True) + RMS_EPS) * g_ref[...] * (1.0 - lam_init)
    o_ref[...] = o.T


def _flash(q, k, vt, *, batch, seq, tq=512, diff=None, lam_init=0.0):
    nk, vrows, tk = vt.shape[2], vt.shape[3], vt.shape[4]
    assert vrows == LANE + ONES_ROWS
    tq = min(tq, seq)
    nq = seq // tq
    ncomp = 2 if diff is not None else 1
    in_specs = [
        pl.BlockSpec((None, None, LANE, tq), lambda b, h, i: (b, h, 0, i)),
        pl.BlockSpec((None, None, seq, LANE), lambda b, h, i: (b, h, 0, 0)),
        pl.BlockSpec((None, None, nk, vrows, tk), lambda b, h, i: (b, h, 0, 0, 0)),
    ]
    args = [q, k, vt]
    if diff is not None:
        lq1, lk1, lq2, lk2, subln = diff
        for v in (lq1, lk1, lq2, lk2):
            in_specs.append(pl.BlockSpec((1, DIFF_HEAD_DIM), lambda b, h, i: (0, 0)))
            args.append(v.reshape(1, DIFF_HEAD_DIM))
        in_specs.append(pl.BlockSpec((LANE, 1), lambda b, h, i: (0, 0)))
        args.append(subln.reshape(LANE, 1))
    return pl.pallas_call(
        functools.partial(_flash_kernel, ncomp=ncomp, nk=nk, lam_init=lam_init),
        grid=(batch, HEADS, nq),
        in_specs=in_specs,
        out_specs=pl.BlockSpec((tq, LANE), lambda b, h, i: (b * nq + i, h)),
        out_shape=jax.ShapeDtypeStruct((batch * seq, HEADS * LANE), F32),
        scratch_shapes=[pltpu.VMEM((ncomp, LANE, tq), BF16),
                        pltpu.VMEM((ncomp, 1, tq), F32), pltpu.VMEM((ncomp, vrows, tq), F32),
                        pltpu.VMEM((ncomp, tk, tq), F32), pltpu.VMEM((ncomp, tk, tq), F32),
                        pltpu.VMEM((ncomp, 1, tq), F32), pltpu.VMEM((ncomp, 1, tq), F32),
                        pltpu.VMEM((ncomp, tk, tq), BF16), pltpu.VMEM((ncomp, tk, tq), BF16),
                        pltpu.VMEM((ncomp, 1, tq), F32), pltpu.VMEM((ncomp, 1, tq), F32)],
        compiler_params=_cparams(("parallel", "parallel", "parallel")),
        name="flash_diff" if diff is not None else "flash_mla",
    )(*args)


def _ret_kernel(dec_ref, qf, kf, vf, qb, kb, vb, of_ref, ob_ref, s_sc, *, chunk):
    @pl.when(pl.program_id(1) == 0)
    def _():
        s_sc[...] = jnp.zeros(s_sc.shape, F32)

    ii = lax.broadcasted_iota(jnp.int32, (chunk, chunk), 0)
    jj = lax.broadcasted_iota(jnp.int32, (chunk, chunk), 1)
    r = lax.broadcasted_iota(jnp.int32, (chunk, 1), 0).astype(F32)
    for d, (q_ref, k_ref, v_ref, o_ref) in enumerate(((qf, kf, vf, of_ref), (qb, kb, vb, ob_ref))):
        for h in range(HEADS):
            la = -jnp.exp(jnp.full((1, 1), dec_ref[d, h], F32))
            if d == 0:
                mask, dist = ii >= jj, (ii - jj).astype(F32)
                qdec, kdec = jnp.exp(la * (r + 1.0)), jnp.exp(la * (chunk - 1.0 - r))
            else:
                mask, dist = jj > ii, (jj - ii).astype(F32)
                qdec, kdec = jnp.exp(la * (chunk - r)), jnp.exp(la * r)
            decay = jnp.where(mask, jnp.exp(jnp.where(mask, dist * la, 0.0)), 0.0)
            q, k, v = q_ref[h], k_ref[h], v_ref[h]
            s = lax.dot_general(q, k, (((1,), (1,)), ((), ())), preferred_element_type=F32)
            o = jnp.dot((s * decay).astype(BF16), v, preferred_element_type=F32)
            state = s_sc[d, h]
            o = o + qdec * jnp.dot(q, state.astype(BF16), preferred_element_type=F32)
            kd = (k.astype(F32) * kdec).astype(BF16)
            s_sc[d, h] = jnp.exp(la * float(chunk)) * state + lax.dot_general(
                kd, v, (((0,), (0,)), ((), ())), preferred_element_type=F32)
            o_ref[:, h * LANE:(h + 1) * LANE] = o


def _retention(q, k, v, decays, *, batch, seq, chunk=256):
    chunk = min(chunk, seq)
    n = seq // chunk
    fwd = pl.BlockSpec((None, HEADS, chunk, LANE), lambda b, c: (b, 0, c, 0))
    bwd = pl.BlockSpec((None, HEADS, chunk, LANE), lambda b, c: (b, 0, n - 1 - c, 0))
    w = HEADS * LANE
    out = jax.ShapeDtypeStruct((batch * seq, w), F32)
    return pl.pallas_call(
        functools.partial(_ret_kernel, chunk=chunk),
        grid=(batch, n),
        in_specs=[pl.BlockSpec(memory_space=pltpu.SMEM), fwd, fwd, fwd, bwd, bwd, bwd],
        out_specs=[pl.BlockSpec((chunk, w), lambda b, c: (b * n + c, 0)),
                   pl.BlockSpec((chunk, w), lambda b, c: (b * n + n - 1 - c, 0))],
        out_shape=[out, out],
        scratch_shapes=[pltpu.VMEM((2, HEADS, LANE, LANE), F32)],
        compiler_params=_cparams(("parallel", "arbitrary")),
        name="retention",
    )(decays, q, k, v, q, k, v)


GLA_SUB = 8


def _split3(x):
    x1 = x.astype(BF16)
    r1 = x - x1.astype(F32)
    x2 = r1.astype(BF16)
    x3 = (r1 - x2.astype(F32)).astype(BF16)
    return x1, x2, x3


def _gla_direction(q, k, v, lr, w2, bias, st, reverse):
    C, wk = q.shape
    wv = v.shape[1]
    dk, dv = wk // HEADS, wv // HEADS
    z = jnp.dot(lr.astype(BF16), w2, preferred_element_type=F32) + bias
    g = (jnp.minimum(z, 0.0) - jnp.log(1.0 + jnp.exp(-jnp.abs(z)))) * (1.0 / GLA_TAU)
    ii = lax.broadcasted_iota(jnp.int32, (C, C), 0)
    jj = lax.broadcasted_iota(jnp.int32, (C, C), 1)
    tri = jnp.where(ii >= jj, 1.0, 0.0).astype(BF16)
    b = sum(jnp.dot(tri, part, preferred_element_type=F32) for part in _split3(g))
    tot = b[C - 1:C, :]
    c = (tot - b + g) if reverse else b

    qe = (q * jnp.exp(jnp.minimum(c, 0.0))).astype(BF16)
    o = lax.dot_general(qe, st.astype(BF16), (((1,), (1,)), ((), ())), preferred_element_type=F32)
    ke = (k * jnp.exp(jnp.minimum(tot - c, 0.0))).astype(BF16)
    upd = lax.dot_general(v.astype(BF16), ke, (((0,), (0,)), ((), ())), preferred_element_type=F32)
    rr = _div_pow2(lax.broadcasted_iota(jnp.int32, (wv, wk), 0), dv)
    cc = _div_pow2(lax.broadcasted_iota(jnp.int32, (wv, wk), 1), dk)
    new_st = jnp.where(rr == cc, st * jnp.exp(tot) + upd, 0.0)

    lane_head = _div_pow2(lax.broadcasted_iota(jnp.int32, (C, wk), 1), dk)
    scores = [jnp.zeros((C, C), F32) for _ in range(HEADS)]
    hsz = C // 2
    while hsz >= GLA_SUB:
        blk = 2 * hsz
        rows = []
        for m in range(C // blk):
            rrow = m * blk + (hsz if reverse else hsz - 1)
            rows.append(jnp.broadcast_to(c[rrow:rrow + 1, :], (blk, wk)))
        ref = jnp.concatenate(rows, axis=0) if len(rows) > 1 else rows[0]
        qt = q * jnp.exp(jnp.minimum(c - ref, 0.0))
        kt = (k * jnp.exp(jnp.minimum(ref - c, 0.0))).astype(BF16)
        same = _div_pow2(ii, blk) == _div_pow2(jj, blk)
        if reverse:
            lvl = same & (_mod_pow2(ii, blk) < hsz) & (_mod_pow2(jj, blk) >= hsz)
        else:
            lvl = same & (_mod_pow2(ii, blk) >= hsz) & (_mod_pow2(jj, blk) < hsz)
        for h in range(HEADS):
            qh = jnp.where(lane_head == h, qt, 0.0).astype(BF16)
            s = lax.dot_general(qh, kt, (((1,), (1,)), ((), ())), preferred_element_type=F32)
            scores[h] = scores[h] + jnp.where(lvl, s, 0.0)
        hsz //= 2

    assert dv == C
    er = _div_pow2(lax.broadcasted_iota(jnp.int32, (wk, wv), 0), dk)
    ec = _div_pow2(lax.broadcasted_iota(jnp.int32, (wk, wv), 1), dv)
    expand = jnp.where(er == ec, 1.0, 0.0).astype(BF16)
    dist = (jj - ii) if reverse else (ii - jj)
    same_sub = _div_pow2(ii, GLA_SUB) == _div_pow2(jj, GLA_SUB)
    for lag in range(1 if reverse else 0, GLA_SUB):
        if lag == 0:
            t = q * k
        else:
            shift = (C - lag) if reverse else lag
            ks, cs = pltpu.roll(k, shift, axis=0), pltpu.roll(c, shift, axis=0)
            t = q * ks * jnp.exp(jnp.minimum(c - cs, 0.0))
        red = jnp.dot(t.astype(BF16), expand, preferred_element_type=F32)
        on_diag = same_sub & (dist == lag)
        for h in range(HEADS):
            scores[h] = scores[h] + jnp.where(on_diag, red[:, h * dv:(h + 1) * dv], 0.0)

    vb = v.astype(BF16)
    o = o + jnp.concatenate(
        [jnp.dot(scores[h].astype(BF16), vb[:, h * dv:(h + 1) * dv], preferred_element_type=F32) for h in range(HEADS)],
        axis=1)
    return o, new_st


def _gla_kernel(qf, kf, vf, lf, qb, kb, vb, lb, w2f, bf, w2b, bb, of_ref, ob_ref, s_sc, *, qscale):
    @pl.when(pl.program_id(1) == 0)
    def _():
        s_sc[...] = jnp.zeros(s_sc.shape, F32)

    o, st = _gla_direction(qf[...] * qscale, kf[...], vf[...], lf[...], w2f[...], bf[...], s_sc[0], False)
    of_ref[...] = o
    s_sc[0] = st
    o, st = _gla_direction(qb[...] * qscale, kb[...], vb[...], lb[...], w2b[...], bb[...], s_sc[1], True)
    ob_ref[...] = o
    s_sc[1] = st


def _gla(z, *, qcol, kcol, vcol, lcol, w2f, bf, w2b, bb, batch, seq, chunk=128):
    chunk = min(chunk, seq)
    n = seq // chunk
    wk, wv = HEADS * GLA_K_DIM, HEADS * LANE

    def specs(cmap):
        return [pl.BlockSpec((chunk, wk), lambda b, c: (cmap(b, c), qcol)),
                pl.BlockSpec((chunk, wk), lambda b, c: (cmap(b, c), kcol)),
                pl.BlockSpec((chunk, wv), lambda b, c: (cmap(b, c), vcol)),
                pl.BlockSpec((chunk, LANE), lambda b, c: (cmap(b, c), lcol))]

    fmap = lambda b, c: b * n + c
    bmap = lambda b, c: b * n + n - 1 - c
    wspec = [pl.BlockSpec((LANE, wk), lambda b, c: (0, 0)), pl.BlockSpec((1, wk), lambda b, c: (0, 0))]
    out = jax.ShapeDtypeStruct((batch * seq, wv), F32)
    return pl.pallas_call(
        functools.partial(_gla_kernel, qscale=GLA_K_DIM ** -0.5),
        grid=(batch, n),
        in_specs=specs(fmap) + specs(bmap) + wspec + wspec,
        out_specs=[pl.BlockSpec((chunk, wv), lambda b, c: (fmap(b, c), 0)),
                   pl.BlockSpec((chunk, wv), lambda b, c: (bmap(b, c), 0))],
        out_shape=[out, out],
        scratch_shapes=[pltpu.VMEM((2, wv, wk), F32)],
        compiler_params=_cparams(("parallel", "arbitrary")),
        name="gla",
    )(z, z, z, z, z, z, z, z, w2f, bf, w2b, bb)


def _layer_norm_rows(r, g, b):
    mu = jnp.mean(r, axis=-1, keepdims=True)
    d = r - mu
    var = jnp.mean(d * d, axis=-1, keepdims=True)
    return d * lax.rsqrt(var + LN_EPS) * g + b


def _outproj_kernel(x_ref, fin_ref, of_ref, ob_ref, gate_ref, ng_ref, wa_ref, wb_ref, lg_ref, lb_ref, rw_ref, rb_ref,
                    o_ref, ids_ref, gw_ref, cnt_ref, run_sc, *, group_norm):
    lin = of_ref[...] + ob_ref[...]
    parts = []
    for h in range(HEADS):
        zh = lin[:, h * LANE:(h + 1) * LANE]
        if group_norm:
            mu = jnp.mean(zh, axis=-1, keepdims=True)
            dz = zh - mu
            parts.append(dz * lax.rsqrt(jnp.mean(dz * dz, axis=-1, keepdims=True) + LN_EPS))
        else:
            parts.append(zh * lax.rsqrt(jnp.mean(zh * zh, axis=-1, keepdims=True) + RMS_EPS) * ng_ref[...])
    gate = gate_ref[...]
    lin = jnp.concatenate(parts, axis=1) * (gate * jax.nn.sigmoid(gate))
    y = (jnp.dot(fin_ref[...].astype(BF16), wa_ref[...], preferred_element_type=F32)
         + jnp.dot(lin.astype(BF16), wb_ref[...], preferred_element_type=F32))
    x1 = _layer_norm_rows(ALPHA * x_ref[...] + y, lg_ref[...], lb_ref[...])
    o_ref[...] = x1
    _route_rows(x1, rw_ref, rb_ref, ids_ref, gw_ref, cnt_ref, run_sc)


def _outproj(x, fin, of, ob, gate_src, gate_col, norm_gain, wa, wb, ln_g, ln_b, route_w, route_b, *, group_norm, tm=512):
    n, d = x.shape
    w = HEADS * LANE
    tm = min(tm, n)
    row = lambda i: (i, 0)
    const = lambda i: (0, 0)
    rw_hi = route_w.astype(BF16)
    rw = jnp.stack([rw_hi, (route_w - rw_hi.astype(F32)).astype(BF16)])
    return pl.pallas_call(
        functools.partial(_outproj_kernel, group_norm=group_norm),
        grid=(n // tm,),
        in_specs=[pl.BlockSpec((tm, d), row), pl.BlockSpec((tm, w), row), pl.BlockSpec((tm, w), row),
                  pl.BlockSpec((tm, w), row), pl.BlockSpec((tm, w), lambda i: (i, gate_col)),
                  pl.BlockSpec((1, LANE), const), pl.BlockSpec((w, d), const), pl.BlockSpec((w, d), const),
                  pl.BlockSpec((1, d), const), pl.BlockSpec((1, d), const),
                  pl.BlockSpec((2, d, LANE), lambda i: (0, 0, 0)), pl.BlockSpec((1, LANE), const)],
        out_specs=[pl.BlockSpec((tm, d), row), pl.BlockSpec((tm, LANE), row), pl.BlockSpec((tm, LANE), row),
                   pl.BlockSpec((1, LANE), const)],
        out_shape=[jax.ShapeDtypeStruct((n, d), F32), jax.ShapeDtypeStruct((n, LANE), jnp.int32),
                   jax.ShapeDtypeStruct((n, LANE), F32), jax.ShapeDtypeStruct((1, LANE), jnp.int32)],
        scratch_shapes=[pltpu.VMEM((1, LANE), F32)],
        compiler_params=_cparams(("arbitrary",)),
        name="outproj",
    )(x, fin, of, ob, gate_src, norm_gain.reshape(1, LANE), wa, wb, ln_g.reshape(1, d), ln_b.reshape(1, d), rw, route_b)


def _route_rows(x, w_ref, b_ref, ids_ref, gw_ref, cnt_ref, run_sc):
    @pl.when(pl.program_id(0) == 0)
    def _():
        run_sc[...] = jnp.zeros(run_sc.shape, F32)

    tm = x.shape[0]
    xh = x.astype(BF16)
    xl = (x - xh.astype(F32)).astype(BF16)
    wh, wl = w_ref[0], w_ref[1]
    logits = (jnp.dot(xh, wh, preferred_element_type=F32) + jnp.dot(xh, wl, preferred_element_type=F32)
              + jnp.dot(xl, wh, preferred_element_type=F32)) + b_ref[...]
    lane = lax.broadcasted_iota(jnp.int32, logits.shape, 1)
    neg = -jnp.inf
    gmask = (lane >= N_EXPERTS) & (lane < N_EXPERTS + N_GROUPS)
    gl = jnp.where(gmask, logits, neg)
    gmax = jnp.max(gl, axis=1, keepdims=True)
    lane_f = lane.astype(F32)
    first = lambda hit: jnp.min(jnp.where(hit, lane_f, float(LANE)), axis=1, keepdims=True).astype(jnp.int32)
    gidx = first(gl == gmax) - N_EXPERTS
    p_grp = 1.0 / jnp.sum(jnp.where(gmask, jnp.exp(gl - gmax), 0.0), axis=1, keepdims=True)
    el = jnp.where(_div_pow2(lane, EXPERTS_PER_GROUP) == gidx, logits, neg)
    l1 = jnp.max(el, axis=1, keepdims=True)
    e1 = first(el == l1)
    el2 = jnp.where(lane == e1, neg, el)
    l2 = jnp.max(el2, axis=1, keepdims=True)
    e2 = first(el2 == l2)
    t = jnp.exp(l2 - l1)
    w1 = p_grp / (1.0 + t)
    w2 = p_grp * t / (1.0 + t)

    onehot = jnp.where(lane == e1, 1.0, jnp.where(lane == e2, 1.0, 0.0))
    ri = lax.broadcasted_iota(jnp.int32, (tm, tm), 0)
    ci = lax.broadcasted_iota(jnp.int32, (tm, tm), 1)
    before = jnp.dot(jnp.where(ri > ci, 1.0, 0.0).astype(BF16), onehot.astype(BF16), preferred_element_type=F32)
    before = before + run_sc[...]
    r1 = jnp.sum(jnp.where(lane == e1, before, 0.0), axis=1, keepdims=True).astype(jnp.int32)
    r2 = jnp.sum(jnp.where(lane == e2, before, 0.0), axis=1, keepdims=True).astype(jnp.int32)
    run_sc[...] = run_sc[...] + jnp.sum(onehot, axis=0, keepdims=True)
    cnt_ref[...] = run_sc[...].astype(jnp.int32)
    ids_ref[...] = jnp.where(lane == 0, e1, jnp.where(lane == 1, e2, jnp.where(lane == 2, r1, jnp.where(lane == 3, r2, 0))))
    gw_ref[...] = jnp.where(lane == 0, w1, jnp.where(lane == 1, w2, 0.0))


MOE_BM = 256
SUBL = 8


def _rows_from_linear(ref, rows):
    return jnp.concatenate([ref[pl.ds(s, rows, stride=SUBL), :] for s in range(SUBL)], axis=1)


def _rows_to_linear(ref, val):
    for s in range(SUBL):
        ref[pl.ds(s, val.shape[0], stride=SUBL), :] = val[:, s * LANE:(s + 1) * LANE]


def _dispatch_kernel(pend_ref, padded_ref, dest_ref, x_ref, xs_hbm, idx_smem, zbuf, lin, sem_i, sem_z, sem):
    i = pl.program_id(0)
    tm = x_ref.shape[0]
    bm = zbuf.shape[0] // SUBL

    @pl.when(i == 0)
    def _():
        zbuf[...] = jnp.zeros(zbuf.shape, F32)

        def tail(e):
            start_row = pl.multiple_of((pend_ref[e] - bm) * SUBL, bm * SUBL)
            return pltpu.make_async_copy(zbuf, xs_hbm.at[pl.ds(start_row, bm * SUBL), :], sem_z)

        def start(e, carry):
            @pl.when(padded_ref[e] > 0)
            def _():
                tail(e).start()
            return carry

        def wait(e, carry):
            @pl.when(padded_ref[e] > 0)
            def _():
                tail(e).wait()
            return carry

        lax.fori_loop(0, N_EXPERTS, start, 0)
        lax.fori_loop(0, N_EXPERTS, wait, 0)

        def unused(b):
            start_row = pl.multiple_of(b * bm * SUBL, bm * SUBL)
            return pltpu.make_async_copy(zbuf, xs_hbm.at[pl.ds(start_row, bm * SUBL), :], sem_z)

        first_unused = pend_ref[N_EXPERTS - 1] // bm
        n_blocks = xs_hbm.shape[0] // (bm * SUBL)
        lax.fori_loop(first_unused, n_blocks, lambda b, c: (unused(b).start(), c)[1], 0)
        lax.fori_loop(first_unused, n_blocks, lambda b, c: (unused(b).wait(), c)[1], 0)

    cp = pltpu.make_async_copy(dest_ref.at[i], idx_smem, sem_i)
    cp.start()
    _rows_to_linear(lin, x_ref[...])
    cp.wait()

    def scatter(r, carry):
        src = lin.at[pl.ds(pl.multiple_of(r * SUBL, SUBL), SUBL), :]
        for k in range(2):
            dst = pl.multiple_of(idx_smem[2 * r + k] * SUBL, SUBL)
            pltpu.make_async_copy(src, xs_hbm.at[pl.ds(dst, SUBL), :], sem).start(priority=k)
        return carry

    lax.fori_loop(0, tm, scatter, 0, unroll=8)
    for k in range(2):
        pltpu.make_async_copy(lin, xs_hbm.at[pl.ds(0, tm * SUBL), :], sem).wait()


def _dispatch(x, dest, pend, padded, cap, *, tm=512):
    n, d = x.shape
    assert d == SUBL * LANE
    tm = min(tm, n)
    nt = n // tm
    grid_spec = pltpu.PrefetchScalarGridSpec(
        num_scalar_prefetch=2,
        grid=(nt,),
        in_specs=[pl.BlockSpec((nt, 2 * tm), lambda i, pe, pa: (0, 0)),
                  pl.BlockSpec((tm, d), lambda i, pe, pa: (i, 0))],
        out_specs=pl.BlockSpec(memory_space=pl.ANY),
        scratch_shapes=[pltpu.SMEM((2 * tm,), jnp.int32), pltpu.VMEM((MOE_BM * SUBL, LANE), F32),
                        pltpu.VMEM((tm * SUBL, LANE), F32),
                        pltpu.SemaphoreType.DMA(()), pltpu.SemaphoreType.DMA(()), pltpu.SemaphoreType.DMA(())],
    )
    return pl.pallas_call(
        _dispatch_kernel,
        grid_spec=grid_spec,
        out_shape=jax.ShapeDtypeStruct((cap * SUBL, LANE), F32),
        compiler_params=_cparams(("arbitrary",)),
        name="dispatch",
    )(pend, padded, dest.reshape(nt, 2 * tm), x)


def _experts_kernel(blk_e_ref, nused_ref, xs_ref, wg_ref, wu_ref, wd_ref, ys_ref, wgb, wub, wdb):
    i = pl.program_id(0)

    @pl.when(i < nused_ref[0])
    def _():
        prev = blk_e_ref[jnp.maximum(i - 1, 0)]

        @pl.when(jnp.logical_or(i == 0, blk_e_ref[i] != prev))
        def _():
            wgb[...] = wg_ref[...].astype(BF16)
            wub[...] = wu_ref[...].astype(BF16)
            wdb[...] = wd_ref[...].astype(BF16)

        xb = _rows_from_linear(xs_ref, xs_ref.shape[0] // SUBL).astype(BF16)
        hg = jnp.dot(xb, wgb[...], preferred_element_type=F32)
        hu = jnp.dot(xb, wub[...], preferred_element_type=F32)
        hid = (hg * jax.nn.sigmoid(hg) * hu).astype(BF16)
        _rows_to_linear(ys_ref, jnp.dot(hid, wdb[...], preferred_element_type=F32))

    @pl.when(i >= nused_ref[0])
    def _():
        ys_ref[...] = jnp.zeros(ys_ref.shape, F32)


def _experts(xs, blk_e, n_used, w_gate, w_up, w_down, layer):
    cap = xs.shape[0] // SUBL
    bm = MOE_BM
    d, de = w_gate.shape[2], w_gate.shape[3]
    row_in = lambda i, be, nu: (jnp.minimum(i, nu[0] - 1), 0)
    row = lambda i, be, nu: (i, 0)
    grid_spec = pltpu.PrefetchScalarGridSpec(
        num_scalar_prefetch=2,
        grid=(cap // bm,),
        in_specs=[pl.BlockSpec((bm * SUBL, LANE), row_in),
                  pl.BlockSpec((None, None, d, de), lambda i, be, nu: (layer, be[i], 0, 0)),
                  pl.BlockSpec((None, None, d, de), lambda i, be, nu: (layer, be[i], 0, 0)),
                  pl.BlockSpec((None, None, de, d), lambda i, be, nu: (layer, be[i], 0, 0))],
        out_specs=pl.BlockSpec((bm * SUBL, LANE), row),
        scratch_shapes=[pltpu.VMEM((d, de), BF16), pltpu.VMEM((d, de), BF16), pltpu.VMEM((de, d), BF16)],
    )
    return pl.pallas_call(
        _experts_kernel,
        grid_spec=grid_spec,
        out_shape=jax.ShapeDtypeStruct((cap * SUBL, LANE), F32),
        compiler_params=_cparams(("arbitrary",)),
        name="experts",
    )(blk_e, n_used, xs, w_gate, w_up, w_down)


def _combine_kernel(dest_ref, x_ref, gw_ref, g_ref, b_ref, ys_hbm, o_ref, idx_smem, ybuf, sem_i, sem):
    i = pl.program_id(0)
    tm = x_ref.shape[0]
    cp = pltpu.make_async_copy(dest_ref.at[i], idx_smem, sem_i)
    cp.start()
    cp.wait()

    def gather(r, carry):
        row = pl.multiple_of(r * SUBL, SUBL)
        for k in range(2):
            src = pl.multiple_of(idx_smem[2 * r + k] * SUBL, SUBL)
            pltpu.make_async_copy(ys_hbm.at[pl.ds(src, SUBL), :], ybuf.at[k, pl.ds(row, SUBL), :], sem).start(priority=k)
        return carry

    lax.fori_loop(0, tm, gather, 0, unroll=8)
    for k in range(2):
        pltpu.make_async_copy(ys_hbm.at[pl.ds(0, tm * SUBL), :], ybuf.at[k], sem).wait()
    gw = gw_ref[...]
    ffn = _rows_from_linear(ybuf.at[0], tm) * gw[:, 0:1] + _rows_from_linear(ybuf.at[1], tm) * gw[:, 1:2]
    o_ref[...] = _layer_norm_rows(ALPHA * x_ref[...] + ffn, g_ref[...], b_ref[...])


def _combine(x, ys, dest, gw, ln_g, ln_b, *, tm=512):
    n, d = x.shape
    tm = min(tm, n)
    nt = n // tm
    return pl.pallas_call(
        _combine_kernel,
        grid=(nt,),
        in_specs=[pl.BlockSpec((nt, 2 * tm), lambda i: (0, 0)),
                  pl.BlockSpec((tm, d), lambda i: (i, 0)), pl.BlockSpec((tm, LANE), lambda i: (i, 0)),
                  pl.BlockSpec((1, d), lambda i: (0, 0)), pl.BlockSpec((1, d), lambda i: (0, 0)),
                  pl.BlockSpec(memory_space=pl.ANY)],
        out_specs=pl.BlockSpec((tm, d), lambda i: (i, 0)),
        out_shape=jax.ShapeDtypeStruct((n, d), F32),
        scratch_shapes=[pltpu.SMEM((2 * tm,), jnp.int32), pltpu.VMEM((2, tm * SUBL, LANE), F32),
                        pltpu.SemaphoreType.DMA(()), pltpu.SemaphoreType.DMA(())],
        compiler_params=_cparams(("arbitrary",)),
        name="combine",
    )(dest.reshape(nt, 2 * tm), x, gw, ln_g.reshape(1, d), ln_b.reshape(1, d), ys)


def _router_params(w_grp, b_grp, w_exp, b_exp):
    d = w_exp.shape[0]
    wr = jnp.zeros((d, LANE), F32).at[:, :N_EXPERTS].set(w_exp).at[:, N_EXPERTS:N_EXPERTS + N_GROUPS].set(w_grp)
    br = jnp.zeros((1, LANE), F32).at[0, :N_EXPERTS].set(b_exp).at[0, N_EXPERTS:N_EXPERTS + N_GROUPS].set(b_grp)
    return wr, br


def _moe(x, routing, w_gate, w_up, w_down, layer, ln_g, ln_b):
    n, d = x.shape
    ids, gw, cnt = routing
    bm = MOE_BM
    counts = cnt[0, :N_EXPERTS]
    padded = (counts + bm - 1) // bm * bm
    pend = jnp.cumsum(padded)
    pstart = pend - padded
    e, r = ids[:, 0:2], ids[:, 2:4]
    onehot = e[:, :, None] == jnp.arange(N_EXPERTS, dtype=jnp.int32)[None, None, :]
    dest = jnp.sum(jnp.where(onehot, pstart[None, None, :], 0), axis=-1) + r
    cap = 2 * n + N_EXPERTS * bm
    nb = cap // bm
    blk_start = jnp.arange(nb, dtype=jnp.int32) * bm
    blk_e = jnp.minimum(jnp.sum((pend[None, :] <= blk_start[:, None]).astype(jnp.int32), axis=1), N_EXPERTS - 1)
    n_used = (pend[-1:] // bm).astype(jnp.int32)
    xs = _dispatch(x, dest, pend.astype(jnp.int32), padded.astype(jnp.int32), cap)
    ys = _experts(xs, blk_e, n_used, w_gate, w_up, w_down, layer)
    return _combine(x, ys, dest, gw, ln_g, ln_b)


def _even_layer(x, batch, seq, layer_idx, w_in, dec_f, dec_b, lq1, lk1, lq2, lk2, subln, w_out, ln_g, ln_b, route):
    d = x.shape[1]
    w = HEADS * LANE
    kw = dict(batch=batch, seq=seq)
    diff_seg = [(0, DIFF_ROT_DIM), (DIFF_HEAD_DIM, DIFF_ROT_DIM)]
    q, k, v, gate, dq, dk, dvt = _even_in(
        x, w_in.astype(BF16), _rope_tables(seq, [(0, LANE)], RET_THETA),
        _rope_tables(seq, diff_seg, ROPE_THETA), **kw)
    decays = jnp.stack([dec_f, dec_b]).astype(F32)
    of, ob = _retention(q, k, v, decays, **kw)
    lam_init = 0.8 - 0.6 * math.exp(-0.3 * layer_idx)
    diff = _flash(dq, dk, dvt, diff=(lq1, lk1, lq2, lk2, subln), lam_init=lam_init, **kw)
    wo = w_out.astype(BF16)
    return _outproj(x, diff, of, ob, gate, 0, jnp.ones((LANE,), F32), wo[w:], wo[:w], ln_g, ln_b, *route,
                    group_norm=True)


def _odd_layer(x, batch, seq, w_in, q_norm, w_uq, kv_norm, w_ukv, w2_f, b_f, w2_b, b_b, gla_norm, w_out, ln_g, ln_b,
               route):
    d = x.shape[1]
    w = HEADS * LANE
    o = np.cumsum([0, MLA_Q_RANK, MLA_KV_RANK, MLA_ROPE, HEADS * GLA_K_DIM, HEADS * GLA_K_DIM, w, w,
                   GLA_GATE_RANK, GLA_GATE_RANK]).tolist()
    zeros = lambda c: jnp.zeros((d, c), F32)
    w_in2 = jnp.concatenate([
        w_in[:, o[0]:o[2]], zeros(MLA_NOPE), w_in[:, o[2]:o[3]], zeros(LANE - MLA_NOPE - MLA_ROPE),
        w_in[:, o[3]:o[7]], w_in[:, o[7]:o[9]], zeros(LANE - 2 * GLA_GATE_RANK)], axis=1).astype(BF16)
    kw = dict(batch=batch, seq=seq)
    qd = MLA_NOPE + MLA_ROPE
    w_uq2 = jnp.pad(w_uq.reshape(MLA_Q_RANK, HEADS, qd), ((0, 0), (0, 0), (0, LANE - qd))).reshape(MLA_Q_RANK, w)
    ukv = w_ukv.reshape(MLA_KV_RANK, HEADS, MLA_NOPE + MLA_V)
    w_uk2 = jnp.pad(ukv[:, :, :MLA_NOPE], ((0, 0), (0, 0), (0, LANE - MLA_NOPE))).reshape(MLA_KV_RANK, w)
    w_uv2 = ukv[:, :, MLA_NOPE:].reshape(MLA_KV_RANK, w)
    q, k, vt, zg = _odd_in(x, w_in2, q_norm, w_uq2.astype(BF16), kv_norm,
                           jnp.concatenate([w_uk2, w_uv2], axis=1).astype(BF16),
                           _rope_tables(seq, [(MLA_NOPE, MLA_ROPE)], ROPE_THETA), **kw)
    mla = _flash(q, k, vt, **kw)
    wk = HEADS * GLA_K_DIM
    pad_rows = lambda m, r0: jnp.zeros((LANE, wk), F32).at[r0:r0 + GLA_GATE_RANK].set(m).astype(BF16)
    of, ob = _gla(zg, qcol=0, kcol=1, vcol=1, lcol=12,
                  w2f=pad_rows(w2_f, 0), bf=b_f.reshape(1, wk), w2b=pad_rows(w2_b, GLA_GATE_RANK), bb=b_b.reshape(1, wk), **kw)
    wo = w_out.astype(BF16)
    return _outproj(x, mla, of, ob, zg, 2, gla_norm, wo[:w], wo[w:], ln_g, ln_b, *route, group_norm=False)


def kernel(x, ev_w_in, ev_ret_decay_f, ev_ret_decay_b, ev_lq1, ev_lk1, ev_lq2, ev_lk2, ev_subln, ev_w_out, od_w_in, od_q_norm, od_w_uq, od_kv_norm, od_w_ukv, od_gla_w2_f, od_gla_b_f, od_gla_w2_b, od_gla_b_b, od_gla_norm, od_w_out, ln1_g, ln1_b, ln2_g, ln2_b, moe_w_grp, moe_b_grp, moe_w_exp, moe_b_exp, moe_w_gate, moe_w_up, moe_w_down):
    batch, seq, d = x.shape
    h = x.reshape(batch * seq, d)
    for i in range(DEPTH):
        j = i // 2
        route = _router_params(moe_w_grp[i], moe_b_grp[i], moe_w_exp[i], moe_b_exp[i])
        if i % 2 == 0:
            h, *routing = _even_layer(h, batch, seq, i, ev_w_in[j], ev_ret_decay_f[j], ev_ret_decay_b[j], ev_lq1[j],
                                      ev_lk1[j], ev_lq2[j], ev_lk2[j], ev_subln[j], ev_w_out[j], ln1_g[i], ln1_b[i], route)
        else:
            h, *routing = _odd_layer(h, batch, seq, od_w_in[j], od_q_norm[j], od_w_uq[j], od_kv_norm[j], od_w_ukv[j],
                                     od_gla_w2_f[j], od_gla_b_f[j], od_gla_w2_b[j], od_gla_b_b[j], od_gla_norm[j],
                                     od_w_out[j], ln1_g[i], ln1_b[i], route)
        h = _moe(h, routing, moe_w_gate, moe_w_up, moe_w_down, i, ln2_g[i], ln2_b[i])
    return h.reshape(batch, seq, d)
```

```python
import functools
import math

import numpy as np
import jax
import jax.numpy as jnp
from jax import lax
from jax.experimental import pallas as pl
from jax.experimental.pallas import tpu as pltpu

F32 = jnp.float32
BF16 = jnp.bfloat16

HEADS = 4
LANE = 128
RET_THETA = 10000.0
ROPE_THETA = 500000.0
DIFF_HEAD_DIM = 64
DIFF_ROT_DIM = 16
MLA_Q_RANK = 256
MLA_KV_RANK = 128
MLA_NOPE = 64
MLA_ROPE = 32
MLA_V = 128
GLA_K_DIM = 64
GLA_GATE_RANK = 16
GLA_TAU = 16.0
N_GROUPS = 4
EXPERTS_PER_GROUP = 8
N_EXPERTS = N_GROUPS * EXPERTS_PER_GROUP
DEPTH = 2
ALPHA = (2.0 * DEPTH) ** 0.25
LN_EPS = 1e-5
RMS_EPS = 1e-6

VMEM_LIMIT = 48 * 1024 * 1024


def _div_pow2(x, n):
    return lax.shift_right_logical(x, int(n).bit_length() - 1)


def _mod_pow2(x, n):
    return lax.bitwise_and(x, int(n) - 1)


def _cparams(sem):
    return pltpu.CompilerParams(dimension_semantics=sem, vmem_limit_bytes=VMEM_LIMIT)


def _rope_heads(z, tabs, sh, scale):
    c, sa, sb = tabs
    outs = []
    for h in range(HEADS):
        zh = z[:, h * LANE:(h + 1) * LANE]
        outs.append((zh * c + pltpu.roll(zh, sh, axis=1) * sa + pltpu.roll(zh, LANE - sh, axis=1) * sb) * scale)
    return outs


def _even_in_kernel(x_ref, w_ref, rc, rsa, rsb, dc, dsa, dsb, q_ref, k_ref, v_ref, g_ref, dq_ref, dk_ref, dvt_ref):
    w = HEADS * LANE
    xb = x_ref[...].astype(BF16)
    part = lambda t: jnp.dot(xb, w_ref[:, t * w:(t + 1) * w], preferred_element_type=F32)
    ret_t = (rc[...], rsa[...], rsb[...])
    diff_t = (dc[...], dsa[...], dsb[...])
    for h, o in enumerate(_rope_heads(part(0), ret_t, LANE // 2, 1.0)):
        q_ref[h] = o.astype(BF16)
    for h, o in enumerate(_rope_heads(part(1), ret_t, LANE // 2, LANE ** -0.5)):
        k_ref[h] = o.astype(BF16)
    rv = part(2)
    for h in range(HEADS):
        v_ref[h] = rv[:, h * LANE:(h + 1) * LANE].astype(BF16)
    g_ref[...] = part(3)
    for h, o in enumerate(_rope_heads(part(4), diff_t, DIFF_ROT_DIM // 2, DIFF_HEAD_DIM ** -0.5 * LOG2E)):
        dq_ref[h] = o.T.astype(BF16)
    for h, o in enumerate(_rope_heads(part(5), diff_t, DIFF_ROT_DIM // 2, 1.0)):
        dk_ref[h] = o.astype(BF16)
    dv = part(6)
    for h in range(HEADS):
        dvt_ref[h, :LANE, :] = dv[:, h * LANE:(h + 1) * LANE].T.astype(BF16)
        dvt_ref[h, LANE:, :] = jnp.ones((ONES_ROWS, dvt_ref.shape[2]), BF16)


def _even_in(x, w, ret_tabs, diff_tabs, *, batch, seq):
    n, d = x.shape
    tm = min(FLASH_TK_DIFF, seq // 2)
    nt = seq // tm
    hw = HEADS * LANE
    heads = jax.ShapeDtypeStruct((batch, HEADS, seq, LANE), BF16)
    head_spec = pl.BlockSpec((None, HEADS, tm, LANE), lambda i: (i // nt, 0, i % nt, 0))
    heads_t = jax.ShapeDtypeStruct((batch, HEADS, LANE, seq), BF16)
    head_t_spec = pl.BlockSpec((None, HEADS, LANE, tm), lambda i: (i // nt, 0, 0, i % nt))
    tab_spec = pl.BlockSpec((tm, LANE), lambda i: (i % nt, 0))
    return pl.pallas_call(
        _even_in_kernel,
        grid=(n // tm,),
        in_specs=[pl.BlockSpec((tm, d), lambda i: (i, 0)), pl.BlockSpec((d, 7 * hw), lambda i: (0, 0))] + [tab_spec] * 6,
        out_specs=[head_spec, head_spec, head_spec, pl.BlockSpec((tm, hw), lambda i: (i, 0)), head_t_spec, head_spec,
                   pl.BlockSpec((None, HEADS, None, LANE + ONES_ROWS, tm), lambda i: (i // nt, 0, i % nt, 0, 0))],
        out_shape=[heads, heads, heads, jax.ShapeDtypeStruct((n, hw), F32), heads_t, heads,
                   jax.ShapeDtypeStruct((batch, HEADS, nt, LANE + ONES_ROWS, tm), BF16)],
        compiler_params=_cparams(("parallel",)),
        name="even_in",
    )(x, w, *ret_tabs, *diff_tabs)


def _rms_rows(z, g):
    return z * lax.rsqrt(jnp.mean(z * z, axis=-1, keepdims=True) + RMS_EPS) * g


def _odd_in_kernel(x_ref, w_ref, qn_ref, wq_ref, kvn_ref, wkv_ref, tc, tsa, tsb, q_ref, k_ref, vt_ref, zg_ref):
    hw = HEADS * LANE
    mla_w = MLA_Q_RANK + MLA_KV_RANK + LANE
    xb = x_ref[...].astype(BF16)
    zg_ref[...] = jnp.dot(xb, w_ref[:, mla_w:], preferred_element_type=F32)
    z1 = jnp.dot(xb, w_ref[:, :mla_w], preferred_element_type=F32)
    tabs = (tc[...], tsa[...], tsb[...])
    sh = MLA_ROPE // 2
    qh = jnp.dot(_rms_rows(z1[:, :MLA_Q_RANK], qn_ref[...]).astype(BF16), wq_ref[...], preferred_element_type=F32)
    for h, o in enumerate(_rope_heads(qh, tabs, sh, (MLA_NOPE + MLA_ROPE) ** -0.5 * LOG2E)):
        q_ref[h] = o.T.astype(BF16)
    ckv = _rms_rows(z1[:, MLA_Q_RANK:MLA_Q_RANK + MLA_KV_RANK], kvn_ref[...]).astype(BF16)
    kv = jnp.dot(ckv, wkv_ref[...], preferred_element_type=F32)
    kr = z1[:, MLA_Q_RANK + MLA_KV_RANK:]
    kr = kr * tabs[0] + pltpu.roll(kr, sh, axis=1) * tabs[1] + pltpu.roll(kr, LANE - sh, axis=1) * tabs[2]
    for h in range(HEADS):
        k_ref[h] = (kv[:, h * LANE:(h + 1) * LANE] + kr).astype(BF16)
        vt_ref[h, :LANE, :] = kv[:, hw + h * LANE:hw + (h + 1) * LANE].T.astype(BF16)
        vt_ref[h, LANE:, :] = jnp.ones((ONES_ROWS, vt_ref.shape[2]), BF16)


def _odd_in(x, w, q_norm, w_uq, kv_norm, w_ukv, tabs, *, batch, seq, tm=512):
    n, d = x.shape
    tm = min(tm, seq // 2)
    nt = seq // tm
    tk = min(FLASH_TK, seq // 2)
    per = tk // tm
    hw = HEADS * LANE
    gw_ = w.shape[1] - (MLA_Q_RANK + MLA_KV_RANK + LANE)
    heads = jax.ShapeDtypeStruct((batch, HEADS, seq, LANE), BF16)
    head_spec = pl.BlockSpec((None, HEADS, tm, LANE), lambda i: (i // nt, 0, i % nt, 0))
    tab_spec = pl.BlockSpec((tm, LANE), lambda i: (i % nt, 0))
    const = lambda i: (0, 0)
    return pl.pallas_call(
        _odd_in_kernel,
        grid=(n // tm,),
        in_specs=[pl.BlockSpec((tm, d), lambda i: (i, 0)), pl.BlockSpec(w.shape, const),
                  pl.BlockSpec((1, MLA_Q_RANK), const), pl.BlockSpec(w_uq.shape, const),
                  pl.BlockSpec((1, MLA_KV_RANK), const), pl.BlockSpec(w_ukv.shape, const)] + [tab_spec] * 3,
        out_specs=[pl.BlockSpec((None, HEADS, LANE, tm), lambda i: (i // nt, 0, 0, i % nt)), head_spec,
                   pl.BlockSpec((None, HEADS, None, LANE + ONES_ROWS, tm),
                                lambda i: (i // nt, 0, (i % nt) // per, 0, (i % nt) % per)),
                   pl.BlockSpec((tm, gw_), lambda i: (i, 0))],
        out_shape=[jax.ShapeDtypeStruct((batch, HEADS, LANE, seq), BF16), heads, jax.ShapeDtypeStruct((batch, HEADS, seq // tk, LANE + ONES_ROWS, tk), BF16),
                   jax.ShapeDtypeStruct((n, gw_), F32)],
        compiler_params=_cparams(("parallel",)),
        name="odd_in",
    )(x, w, q_norm.reshape(1, -1), w_uq, kv_norm.reshape(1, -1), w_ukv, *tabs)


def _rope_tables(seq, segs, theta):
    pos = jnp.arange(seq, dtype=F32)
    inv = jnp.zeros((LANE,), F32)
    lo = np.zeros((LANE,), bool)
    hi = np.zeros((LANE,), bool)
    for start, rot in segs:
        half = rot // 2
        f = jnp.power(jnp.float32(theta), -jnp.arange(0, rot, 2, dtype=F32) / rot)
        inv = inv.at[start:start + half].set(f).at[start + half:start + rot].set(f)
        lo[start:start + half] = True
        hi[start + half:start + rot] = True
    ang = pos[:, None] * inv[None, :]
    cos, sin = jnp.cos(ang), jnp.sin(ang)
    c = jnp.where(lo | hi, cos, 1.0)
    sa = jnp.where(hi, sin, 0.0)
    sb = jnp.where(lo, -sin, 0.0)
    return c, sa, sb


ONES_ROWS = 16
LOG2E = math.log2(math.e)
FLASH_TK = 1024
FLASH_TK_DIFF = 512


def _flash_kernel(*refs, ncomp, nk, lam_init):
    if ncomp == 2:
        q_ref, k_ref, vt_ref, lq1, lk1, lq2, lk2, g_ref, o_ref, *scr = refs
    else:
        q_ref, k_ref, vt_ref, o_ref, *scr = refs
    qm_sc, m_sc, acc_sc, s0, s1, cm0, cm1, p0, p1, al0, al1 = scr
    tk = s0.shape[1]
    q = q_ref[...]
    if ncomp == 2:
        chan = lax.broadcasted_iota(jnp.int32, q.shape, 0)
        zero = jnp.zeros_like(q)
        qm_sc[0] = jnp.where(chan < DIFF_HEAD_DIM, q, zero)
        qm_sc[1] = jnp.where(chan >= DIFF_HEAD_DIM, q, zero)
    else:
        qm_sc[0] = q
    m_sc[...] = jnp.full(m_sc.shape, -jnp.inf, F32)
    acc_sc[...] = jnp.zeros(acc_sc.shape, F32)

    def scores(j, s_ref, cm_ref):
        k = k_ref[j * tk:(j + 1) * tk, :]
        for c in range(ncomp):
            s = jnp.dot(k, qm_sc[c], preferred_element_type=F32)
            s_ref[c] = s
            cm_ref[c] = jnp.max(s, axis=0, keepdims=True)

    def softmax(s_ref, cm_ref, p_ref, al_ref):
        for c in range(ncomp):
            m_old = m_sc[c]
            m_new = jnp.maximum(m_old, cm_ref[c])
            al_ref[c] = jnp.exp2(m_old - m_new)
            p_ref[c] = jnp.exp2(s_ref[c] - m_new).astype(BF16)
            m_sc[c] = m_new

    def values(j, p_ref, al_ref):
        vt = vt_ref[j]
        for c in range(ncomp):
            acc_sc[c] = al_ref[c] * acc_sc[c] + jnp.dot(vt, p_ref[c], preferred_element_type=F32)

    bufs = ((s0, cm0, p0, al0), (s1, cm1, p1, al1))
    scores(0, s0, cm0)
    for j in range(nk):
        s_c, cm_c, p_c, al_c = bufs[j % 2]
        s_n, cm_n, p_n, al_n = bufs[(j + 1) % 2]
        if j + 1 < nk:
            scores(j + 1, s_n, cm_n)
        softmax(s_c, cm_c, p_c, al_c)
        if j >= 1:
            values(j - 1, p_n, al_n)
    values(nk - 1, *bufs[(nk - 1) % 2][2:])

    def normalised(c):
        acc = acc_sc[c]
        return acc[:LANE] / acc[LANE:LANE + 1]

    o = normalised(0)
    if ncomp == 2:
        lam = (jnp.exp(jnp.sum(lq1[...] * lk1[...], keepdims=True))
               - jnp.exp(jnp.sum(lq2[...] * lk2[...], keepdims=True)) + lam_init)
        o = o - lam * normalised(1)
        o = o * lax.rsqrt(jnp.mean(o * o, axis=0, keepdims=True) + RMS_EPS) * g_ref[...] * (1.0 - lam_init)
    o_ref[...] = o.T


def _flash(q, k, vt, *, batch, seq, tq=512, diff=None, lam_init=0.0):
    nk, vrows, tk = vt.shape[2], vt.shape[3], vt.shape[4]
    assert vrows == LANE + ONES_ROWS
    tq = min(tq, seq)
    nq = seq // tq
    ncomp = 2 if diff is not None else 1
    in_specs = [
        pl.BlockSpec((None, None, LANE, tq), lambda b, h, i: (b, h, 0, i)),
        pl.BlockSpec((None, None, seq, LANE), lambda b, h, i: (b, h, 0, 0)),
        pl.BlockSpec((None, None, nk, vrows, tk), lambda b, h, i: (b, h, 0, 0, 0)),
    ]
    args = [q, k, vt]
    if diff is not None:
        lq1, lk1, lq2, lk2, subln = diff
        for v in (lq1, lk1, lq2, lk2):
            in_specs.append(pl.BlockSpec((1, DIFF_HEAD_DIM), lambda b, h, i: (0, 0)))
            args.append(v.reshape(1, DIFF_HEAD_DIM))
        in_specs.append(pl.BlockSpec((LANE, 1), lambda b, h, i: (0, 0)))
        args.append(subln.reshape(LANE, 1))
    return pl.pallas_call(
        functools.partial(_flash_kernel, ncomp=ncomp, nk=nk, lam_init=lam_init),
        grid=(batch, HEADS, nq),
        in_specs=in_specs,
        out_specs=pl.BlockSpec((tq, LANE), lambda b, h, i: (b * nq + i, h)),
        out_shape=jax.ShapeDtypeStruct((batch * seq, HEADS * LANE), F32),
        scratch_shapes=[pltpu.VMEM((ncomp, LANE, tq), BF16),
                        pltpu.VMEM((ncomp, 1, tq), F32), pltpu.VMEM((ncomp, vrows, tq), F32),
                        pltpu.VMEM((ncomp, tk, tq), F32), pltpu.VMEM((ncomp, tk, tq), F32),
                        pltpu.VMEM((ncomp, 1, tq), F32), pltpu.VMEM((ncomp, 1, tq), F32),
                        pltpu.VMEM((ncomp, tk, tq), BF16), pltpu.VMEM((ncomp, tk, tq), BF16),
                        pltpu.VMEM((ncomp, 1, tq), F32), pltpu.VMEM((ncomp, 1, tq), F32)],
        compiler_params=_cparams(("parallel", "parallel", "parallel")),
        name="flash_diff" if diff is not None else "flash_mla",
    )(*args)


def _ret_kernel(dec_ref, qf, kf, vf, qb, kb, vb, of_ref, ob_ref, s_sc, *, chunk):
    @pl.when(pl.program_id(1) == 0)
    def _():
        s_sc[...] = jnp.zeros(s_sc.shape, F32)

    ii = lax.broadcasted_iota(jnp.int32, (chunk, chunk), 0)
    jj = lax.broadcasted_iota(jnp.int32, (chunk, chunk), 1)
    r = lax.broadcasted_iota(jnp.int32, (chunk, 1), 0).astype(F32)
    for d, (q_ref, k_ref, v_ref, o_ref) in enumerate(((qf, kf, vf, of_ref), (qb, kb, vb, ob_ref))):
        for h in range(HEADS):
            la = -jnp.exp(jnp.full((1, 1), dec_ref[d, h], F32))
            if d == 0:
                mask, dist = ii >= jj, (ii - jj).astype(F32)
                qdec, kdec = jnp.exp(la * (r + 1.0)), jnp.exp(la * (chunk - 1.0 - r))
            else:
                mask, dist = jj > ii, (jj - ii).astype(F32)
                qdec, kdec = jnp.exp(la * (chunk - r)), jnp.exp(la * r)
            decay = jnp.where(mask, jnp.exp(jnp.where(mask, dist * la, 0.0)), 0.0)
            q, k, v = q_ref[h], k_ref[h], v_ref[h]
            s = lax.dot_general(q, k, (((1,), (1,)), ((), ())), preferred_element_type=F32)
            o = jnp.dot((s * decay).astype(BF16), v, preferred_element_type=F32)
            state = s_sc[d, h]
            o = o + qdec * jnp.dot(q, state.astype(BF16), preferred_element_type=F32)
            kd = (k.astype(F32) * kdec).astype(BF16)
            s_sc[d, h] = jnp.exp(la * float(chunk)) * state + lax.dot_general(
                kd, v, (((0,), (0,)), ((), ())), preferred_element_type=F32)
            o_ref[:, h * LANE:(h + 1) * LANE] = o


def _retention(q, k, v, decays, *, batch, seq, chunk=256):
    chunk = min(chunk, seq)
    n = seq // chunk
    fwd = pl.BlockSpec((None, HEADS, chunk, LANE), lambda b, c: (b, 0, c, 0))
    bwd = pl.BlockSpec((None, HEADS, chunk, LANE), lambda b, c: (b, 0, n - 1 - c, 0))
    w = HEADS * LANE
    out = jax.ShapeDtypeStruct((batch * seq, w), F32)
    return pl.pallas_call(
        functools.partial(_ret_kernel, chunk=chunk),
        grid=(batch, n),
        in_specs=[pl.BlockSpec(memory_space=pltpu.SMEM), fwd, fwd, fwd, bwd, bwd, bwd],
        out_specs=[pl.BlockSpec((chunk, w), lambda b, c: (b * n + c, 0)),
                   pl.BlockSpec((chunk, w), lambda b, c: (b * n + n - 1 - c, 0))],
        out_shape=[out, out],
        scratch_shapes=[pltpu.VMEM((2, HEADS, LANE, LANE), F32)],
        compiler_params=_cparams(("parallel", "arbitrary")),
        name="retention",
    )(decays, q, k, v, q, k, v)


GLA_SUB = 8


def _split3(x):
    x1 = x.astype(BF16)
    r1 = x - x1.astype(F32)
    x2 = r1.astype(BF16)
    x3 = (r1 - x2.astype(F32)).astype(BF16)
    return x1, x2, x3


def _gla_direction(q, k, v, lr, w2, bias, st, reverse):
    C, wk = q.shape
    wv = v.shape[1]
    dk, dv = wk // HEADS, wv // HEADS
    z = jnp.dot(lr.astype(BF16), w2, preferred_element_type=F32) + bias
    g = (jnp.minimum(z, 0.0) - jnp.log(1.0 + jnp.exp(-jnp.abs(z)))) * (1.0 / GLA_TAU)
    ii = lax.broadcasted_iota(jnp.int32, (C, C), 0)
    jj = lax.broadcasted_iota(jnp.int32, (C, C), 1)
    tri = jnp.where(ii >= jj, 1.0, 0.0).astype(BF16)
    b = sum(jnp.dot(tri, part, preferred_element_type=F32) for part in _split3(g))
    tot = b[C - 1:C, :]
    c = (tot - b + g) if reverse else b

    qe = (q * jnp.exp(jnp.minimum(c, 0.0))).astype(BF16)
    o = lax.dot_general(qe, st.astype(BF16), (((1,), (1,)), ((), ())), preferred_element_type=F32)
    ke = (k * jnp.exp(jnp.minimum(tot - c, 0.0))).astype(BF16)
    upd = lax.dot_general(v.astype(BF16), ke, (((0,), (0,)), ((), ())), preferred_element_type=F32)
    rr = _div_pow2(lax.broadcasted_iota(jnp.int32, (wv, wk), 0), dv)
    cc = _div_pow2(lax.broadcasted_iota(jnp.int32, (wv, wk), 1), dk)
    new_st = jnp.where(rr == cc, st * jnp.exp(tot) + upd, 0.0)

    lane_head = _div_pow2(lax.broadcasted_iota(jnp.int32, (C, wk), 1), dk)
    scores = [jnp.zeros((C, C), F32) for _ in range(HEADS)]
    hsz = C // 2
    while hsz >= GLA_SUB:
        blk = 2 * hsz
        rows = []
        for m in range(C // blk):
            rrow = m * blk + (hsz if reverse else hsz - 1)
            rows.append(jnp.broadcast_to(c[rrow:rrow + 1, :], (blk, wk)))
        ref = jnp.concatenate(rows, axis=0) if len(rows) > 1 else rows[0]
        qt = q * jnp.exp(jnp.minimum(c - ref, 0.0))
        kt = (k * jnp.exp(jnp.minimum(ref - c, 0.0))).astype(BF16)
        same = _div_pow2(ii, blk) == _div_pow2(jj, blk)
        if reverse:
            lvl = same & (_mod_pow2(ii, blk) < hsz) & (_mod_pow2(jj, blk) >= hsz)
        else:
            lvl = same & (_mod_pow2(ii, blk) >= hsz) & (_mod_pow2(jj, blk) < hsz)
        for h in range(HEADS):
            qh = jnp.where(lane_head == h, qt, 0.0).astype(BF16)
            s = lax.dot_general(qh, kt, (((1,), (1,)), ((), ())), preferred_element_type=F32)
            scores[h] = scores[h] + jnp.where(lvl, s, 0.0)
        hsz //= 2

    assert dv == C
    er = _div_pow2(lax.broadcasted_iota(jnp.int32, (wk, wv), 0), dk)
    ec = _div_pow2(lax.broadcasted_iota(jnp.int32, (wk, wv), 1), dv)
    expand = jnp.where(er == ec, 1.0, 0.0).astype(BF16)
    dist = (jj - ii) if reverse else (ii - jj)
    same_sub = _div_pow2(ii, GLA_SUB) == _div_pow2(jj, GLA_SUB)
    for lag in range(1 if reverse else 0, GLA_SUB):
        if lag == 0:
            t = q * k
        else:
            shift = (C - lag) if reverse else lag
            ks, cs = pltpu.roll(k, shift, axis=0), pltpu.roll(c, shift, axis=0)
            t = q * ks * jnp.exp(jnp.minimum(c - cs, 0.0))
        red = jnp.dot(t.astype(BF16), expand, preferred_element_type=F32)
        on_diag = same_sub & (dist == lag)
        for h in range(HEADS):
            scores[h] = scores[h] + jnp.where(on_diag, red[:, h * dv:(h + 1) * dv], 0.0)

    vb = v.astype(BF16)
    o = o + jnp.concatenate(
        [jnp.dot(scores[h].astype(BF16), vb[:, h * dv:(h + 1) * dv], preferred_element_type=F32) for h in range(HEADS)],
        axis=1)
    return o, new_st


def _gla_kernel(qf, kf, vf, lf, qb, kb, vb, lb, w2f, bf, w2b, bb, of_ref, ob_ref, s_sc, *, qscale):
    @pl.when(pl.program_id(1) == 0)
    def _():
        s_sc[...] = jnp.zeros(s_sc.shape, F32)

    o, st = _gla_direction(qf[...] * qscale, kf[...], vf[...], lf[...], w2f[...], bf[...], s_sc[0], False)
    of_ref[...] = o
    s_sc[0] = st
    o, st = _gla_direction(qb[...] * qscale, kb[...], vb[...], lb[...], w2b[...], bb[...], s_sc[1], True)
    ob_ref[...] = o
    s_sc[1] = st


def _gla(z, *, qcol, kcol, vcol, lcol, w2f, bf, w2b, bb, batch, seq, chunk=128):
    chunk = min(chunk, seq)
    n = seq // chunk
    wk, wv = HEADS * GLA_K_DIM, HEADS * LANE

    def specs(cmap):
        return [pl.BlockSpec((chunk, wk), lambda b, c: (cmap(b, c), qcol)),
                pl.BlockSpec((chunk, wk), lambda b, c: (cmap(b, c), kcol)),
                pl.BlockSpec((chunk, wv), lambda b, c: (cmap(b, c), vcol)),
                pl.BlockSpec((chunk, LANE), lambda b, c: (cmap(b, c), lcol))]

    fmap = lambda b, c: b * n + c
    bmap = lambda b, c: b * n + n - 1 - c
    wspec = [pl.BlockSpec((LANE, wk), lambda b, c: (0, 0)), pl.BlockSpec((1, wk), lambda b, c: (0, 0))]
    out = jax.ShapeDtypeStruct((batch * seq, wv), F32)
    return pl.pallas_call(
        functools.partial(_gla_kernel, qscale=GLA_K_DIM ** -0.5),
        grid=(batch, n),
        in_specs=specs(fmap) + specs(bmap) + wspec + wspec,
        out_specs=[pl.BlockSpec((chunk, wv), lambda b, c: (fmap(b, c), 0)),
                   pl.BlockSpec((chunk, wv), lambda b, c: (bmap(b, c), 0))],
        out_shape=[out, out],
        scratch_shapes=[pltpu.VMEM((2, wv, wk), F32)],
        compiler_params=_cparams(("parallel", "arbitrary")),
        name="gla",
    )(z, z, z, z, z, z, z, z, w2f, bf, w2b, bb)


def _layer_norm_rows(r, g, b):
    mu = jnp.mean(r, axis=-1, keepdims=True)
    d = r - mu
    var = jnp.mean(d * d, axis=-1, keepdims=True)
    return d * lax.rsqrt(var + LN_EPS) * g + b


def _outproj_kernel(x_ref, fin_ref, of_ref, ob_ref, gate_ref, ng_ref, wa_ref, wb_ref, lg_ref, lb_ref, rw_ref, rb_ref,
                    o_ref, ids_ref, gw_ref, cnt_ref, run_sc, *, group_norm):
    lin = of_ref[...] + ob_ref[...]
    parts = []
    for h in range(HEADS):
        zh = lin[:, h * LANE:(h + 1) * LANE]
        if group_norm:
            mu = jnp.mean(zh, axis=-1, keepdims=True)
            dz = zh - mu
            parts.append(dz * lax.rsqrt(jnp.mean(dz * dz, axis=-1, keepdims=True) + LN_EPS))
        else:
            parts.append(zh * lax.rsqrt(jnp.mean(zh * zh, axis=-1, keepdims=True) + RMS_EPS) * ng_ref[...])
    gate = gate_ref[...]
    lin = jnp.concatenate(parts, axis=1) * (gate * jax.nn.sigmoid(gate))
    y = (jnp.dot(fin_ref[...].astype(BF16), wa_ref[...], preferred_element_type=F32)
         + jnp.dot(lin.astype(BF16), wb_ref[...], preferred_element_type=F32))
    x1 = _layer_norm_rows(ALPHA * x_ref[...] + y, lg_ref[...], lb_ref[...])
    o_ref[...] = x1
    _route_rows(x1, rw_ref, rb_ref, ids_ref, gw_ref, cnt_ref, run_sc)


def _outproj(x, fin, of, ob, gate_src, gate_col, norm_gain, wa, wb, ln_g, ln_b, route_w, route_b, *, group_norm, tm=512):
    n, d = x.shape
    w = HEADS * LANE
    tm = min(tm, n)
    row = lambda i: (i, 0)
    const = lambda i: (0, 0)
    rw_hi = route_w.astype(BF16)
    rw = jnp.stack([rw_hi, (route_w - rw_hi.astype(F32)).astype(BF16)])
    return pl.pallas_call(
        functools.partial(_outproj_kernel, group_norm=group_norm),
        grid=(n // tm,),
        in_specs=[pl.BlockSpec((tm, d), row), pl.BlockSpec((tm, w), row), pl.BlockSpec((tm, w), row),
                  pl.BlockSpec((tm, w), row), pl.BlockSpec((tm, w), lambda i: (i, gate_col)),
                  pl.BlockSpec((1, LANE), const), pl.BlockSpec((w, d), const), pl.BlockSpec((w, d), const),
                  pl.BlockSpec((1, d), const), pl.BlockSpec((1, d), const),
                  pl.BlockSpec((2, d, LANE), lambda i: (0, 0, 0)), pl.BlockSpec((1, LANE), const)],
        out_specs=[pl.BlockSpec((tm, d), row), pl.BlockSpec((tm, LANE), row), pl.BlockSpec((tm, LANE), row),
                   pl.BlockSpec((1, LANE), const)],
        out_shape=[jax.ShapeDtypeStruct((n, d), F32), jax.ShapeDtypeStruct((n, LANE), jnp.int32),
                   jax.ShapeDtypeStruct((n, LANE), F32), jax.ShapeDtypeStruct((1, LANE), jnp.int32)],
        scratch_shapes=[pltpu.VMEM((1, LANE), F32)],
        compiler_params=_cparams(("arbitrary",)),
        name="outproj",
    )(x, fin, of, ob, gate_src, norm_gain.reshape(1, LANE), wa, wb, ln_g.reshape(1, d), ln_b.reshape(1, d), rw, route_b)


def _route_rows(x, w_ref, b_ref, ids_ref, gw_ref, cnt_ref, run_sc):
    @pl.when(pl.program_id(0) == 0)
    def _():
        run_sc[...] = jnp.zeros(run_sc.shape, F32)

    tm = x.shape[0]
    xh = x.astype(BF16)
    xl = (x - xh.astype(F32)).astype(BF16)
    wh, wl = w_ref[0], w_ref[1]
    logits = (jnp.dot(xh, wh, preferred_element_type=F32) + jnp.dot(xh, wl, preferred_element_type=F32)
              + jnp.dot(xl, wh, preferred_element_type=F32)) + b_ref[...]
    lane = lax.broadcasted_iota(jnp.int32, logits.shape, 1)
    neg = -jnp.inf
    gmask = (lane >= N_EXPERTS) & (lane < N_EXPERTS + N_GROUPS)
    gl = jnp.where(gmask, logits, neg)
    gmax = jnp.max(gl, axis=1, keepdims=True)
    lane_f = lane.astype(F32)
    first = lambda hit: jnp.min(jnp.where(hit, lane_f, float(LANE)), axis=1, keepdims=True).astype(jnp.int32)
    gidx = first(gl == gmax) - N_EXPERTS
    p_grp = 1.0 / jnp.sum(jnp.where(gmask, jnp.exp(gl - gmax), 0.0), axis=1, keepdims=True)
    el = jnp.where(_div_pow2(lane, EXPERTS_PER_GROUP) == gidx, logits, neg)
    l1 = jnp.max(el, axis=1, keepdims=True)
    e1 = first(el == l1)
    el2 = jnp.where(lane == e1, neg, el)
    l2 = jnp.max(el2, axis=1, keepdims=True)
    e2 = first(el2 == l2)
    t = jnp.exp(l2 - l1)
    w1 = p_grp / (1.0 + t)
    w2 = p_grp * t / (1.0 + t)

    onehot = jnp.where(lane == e1, 1.0, jnp.where(lane == e2, 1.0, 0.0))
    ri = lax.broadcasted_iota(jnp.int32, (tm, tm), 0)
    ci = lax.broadcasted_iota(jnp.int32, (tm, tm), 1)
    before = jnp.dot(jnp.where(ri > ci, 1.0, 0.0).astype(BF16), onehot.astype(BF16), preferred_element_type=F32)
    before = before + run_sc[...]
    r1 = jnp.sum(jnp.where(lane == e1, before, 0.0), axis=1, keepdims=True).astype(jnp.int32)
    r2 = jnp.sum(jnp.where(lane == e2, before, 0.0), axis=1, keepdims=True).astype(jnp.int32)
    run_sc[...] = run_sc[...] + jnp.sum(onehot, axis=0, keepdims=True)
    cnt_ref[...] = run_sc[...].astype(jnp.int32)
    ids_ref[...] = jnp.where(lane == 0, e1, jnp.where(lane == 1, e2, jnp.where(lane == 2, r1, jnp.where(lane == 3, r2, 0))))
    gw_ref[...] = jnp.where(lane == 0, w1, jnp.where(lane == 1, w2, 0.0))


MOE_BM = 256
SUBL = 8


def _rows_from_linear(ref, rows):
    return jnp.concatenate([ref[pl.ds(s, rows, stride=SUBL), :] for s in range(SUBL)], axis=1)


def _rows_to_linear(ref, val):
    for s in range(SUBL):
        ref[pl.ds(s, val.shape[0], stride=SUBL), :] = val[:, s * LANE:(s + 1) * LANE]


def _dispatch_kernel(pend_ref, padded_ref, dest_ref, x_ref, xs_hbm, idx_smem, zbuf, lin, sem_i, sem_z, sem):
    i = pl.program_id(0)
    tm = x_ref.shape[0]
    bm = zbuf.shape[0] // SUBL

    @pl.when(i == 0)
    def _():
        zbuf[...] = jnp.zeros(zbuf.shape, F32)

        def tail(e):
            start_row = pl.multiple_of((pend_ref[e] - bm) * SUBL, bm * SUBL)
            return pltpu.make_async_copy(zbuf, xs_hbm.at[pl.ds(start_row, bm * SUBL), :], sem_z)

        def start(e, carry):
            @pl.when(padded_ref[e] > 0)
            def _():
                tail(e).start()
            return carry

        def wait(e, carry):
            @pl.when(padded_ref[e] > 0)
            def _():
                tail(e).wait()
            return carry

        lax.fori_loop(0, N_EXPERTS, start, 0)
        lax.fori_loop(0, N_EXPERTS, wait, 0)

        def unused(b):
            start_row = pl.multiple_of(b * bm * SUBL, bm * SUBL)
            return pltpu.make_async_copy(zbuf, xs_hbm.at[pl.ds(start_row, bm * SUBL), :], sem_z)

        first_unused = pend_ref[N_EXPERTS - 1] // bm
        n_blocks = xs_hbm.shape[0] // (bm * SUBL)
        lax.fori_loop(first_unused, n_blocks, lambda b, c: (unused(b).start(), c)[1], 0)
        lax.fori_loop(first_unused, n_blocks, lambda b, c: (unused(b).wait(), c)[1], 0)

    cp = pltpu.make_async_copy(dest_ref.at[i], idx_smem, sem_i)
    cp.start()
    _rows_to_linear(lin, x_ref[...])
    cp.wait()

    def scatter(r, carry):
        src = lin.at[pl.ds(pl.multiple_of(r * SUBL, SUBL), SUBL), :]
        for k in range(2):
            dst = pl.multiple_of(idx_smem[2 * r + k] * SUBL, SUBL)
            pltpu.make_async_copy(src, xs_hbm.at[pl.ds(dst, SUBL), :], sem).start(priority=k)
        return carry

    lax.fori_loop(0, tm, scatter, 0, unroll=8)
    for k in range(2):
        pltpu.make_async_copy(lin, xs_hbm.at[pl.ds(0, tm * SUBL), :], sem).wait()


def _dispatch(x, dest, pend, padded, cap, *, tm=512):
    n, d = x.shape
    assert d == SUBL * LANE
    tm = min(tm, n)
    nt = n // tm
    grid_spec = pltpu.PrefetchScalarGridSpec(
        num_scalar_prefetch=2,
        grid=(nt,),
        in_specs=[pl.BlockSpec((nt, 2 * tm), lambda i, pe, pa: (0, 0)),
                  pl.BlockSpec((tm, d), lambda i, pe, pa: (i, 0))],
        out_specs=pl.BlockSpec(memory_space=pl.ANY),
        scratch_shapes=[pltpu.SMEM((2 * tm,), jnp.int32), pltpu.VMEM((MOE_BM * SUBL, LANE), F32),
                        pltpu.VMEM((tm * SUBL, LANE), F32),
                        pltpu.SemaphoreType.DMA(()), pltpu.SemaphoreType.DMA(()), pltpu.SemaphoreType.DMA(())],
    )
    return pl.pallas_call(
        _dispatch_kernel,
        grid_spec=grid_spec,
        out_shape=jax.ShapeDtypeStruct((cap * SUBL, LANE), F32),
        compiler_params=_cparams(("arbitrary",)),
        name="dispatch",
    )(pend, padded, dest.reshape(nt, 2 * tm), x)


def _experts_kernel(blk_e_ref, nused_ref, xs_ref, wg_ref, wu_ref, wd_ref, ys_ref, wgb, wub, wdb):
    i = pl.program_id(0)

    @pl.when(i < nused_ref[0])
    def _():
        prev = blk_e_ref[jnp.maximum(i - 1, 0)]

        @pl.when(jnp.logical_or(i == 0, blk_e_ref[i] != prev))
        def _():
            wgb[...] = wg_ref[...].astype(BF16)
            wub[...] = wu_ref[...].astype(BF16)
            wdb[...] = wd_ref[...].astype(BF16)

        xb = _rows_from_linear(xs_ref, xs_ref.shape[0] // SUBL).astype(BF16)
        hg = jnp.dot(xb, wgb[...], preferred_element_type=F32)
        hu = jnp.dot(xb, wub[...], preferred_element_type=F32)
        hid = (hg * jax.nn.sigmoid(hg) * hu).astype(BF16)
        _rows_to_linear(ys_ref, jnp.dot(hid, wdb[...], preferred_element_type=F32))

    @pl.when(i >= nused_ref[0])
    def _():
        ys_ref[...] = jnp.zeros(ys_ref.shape, F32)


def _experts(xs, blk_e, n_used, w_gate, w_up, w_down, layer):
    cap = xs.shape[0] // SUBL
    bm = MOE_BM
    d, de = w_gate.shape[2], w_gate.shape[3]
    row_in = lambda i, be, nu: (jnp.minimum(i, nu[0] - 1), 0)
    row = lambda i, be, nu: (i, 0)
    grid_spec = pltpu.PrefetchScalarGridSpec(
        num_scalar_prefetch=2,
        grid=(cap // bm,),
        in_specs=[pl.BlockSpec((bm * SUBL, LANE), row_in),
                  pl.BlockSpec((None, None, d, de), lambda i, be, nu: (layer, be[i], 0, 0)),
                  pl.BlockSpec((None, None, d, de), lambda i, be, nu: (layer, be[i], 0, 0)),
                  pl.BlockSpec((None, None, de, d), lambda i, be, nu: (layer, be[i], 0, 0))],
        out_specs=pl.BlockSpec((bm * SUBL, LANE), row),
        scratch_shapes=[pltpu.VMEM((d, de), BF16), pltpu.VMEM((d, de), BF16), pltpu.VMEM((de, d), BF16)],
    )
    return pl.pallas_call(
        _experts_kernel,
        grid_spec=grid_spec,
        out_shape=jax.ShapeDtypeStruct((cap * SUBL, LANE), F32),
        compiler_params=_cparams(("arbitrary",)),
        name="experts",
    )(blk_e, n_used, xs, w_gate, w_up, w_down)


def _combine_kernel(dest_ref, x_ref, gw_ref, g_ref, b_ref, ys_hbm, o_ref, idx_smem, ybuf, sem_i, sem):
    i = pl.program_id(0)
    tm = x_ref.shape[0]
    cp = pltpu.make_async_copy(dest_ref.at[i], idx_smem, sem_i)
    cp.start()
    cp.wait()

    def gather(r, carry):
        row = pl.multiple_of(r * SUBL, SUBL)
        for k in range(2):
            src = pl.multiple_of(idx_smem[2 * r + k] * SUBL, SUBL)
            pltpu.make_async_copy(ys_hbm.at[pl.ds(src, SUBL), :], ybuf.at[k, pl.ds(row, SUBL), :], sem).start(priority=k)
        return carry

    lax.fori_loop(0, tm, gather, 0, unroll=8)
    for k in range(2):
        pltpu.make_async_copy(ys_hbm.at[pl.ds(0, tm * SUBL), :], ybuf.at[k], sem).wait()
    gw = gw_ref[...]
    ffn = _rows_from_linear(ybuf.at[0], tm) * gw[:, 0:1] + _rows_from_linear(ybuf.at[1], tm) * gw[:, 1:2]
    o_ref[...] = _layer_norm_rows(ALPHA * x_ref[...] + ffn, g_ref[...], b_ref[...])


def _combine(x, ys, dest, gw, ln_g, ln_b, *, tm=512):
    n, d = x.shape
    tm = min(tm, n)
    nt = n // tm
    return pl.pallas_call(
        _combine_kernel,
        grid=(nt,),
        in_specs=[pl.BlockSpec((nt, 2 * tm), lambda i: (0, 0)),
                  pl.BlockSpec((tm, d), lambda i: (i, 0)), pl.BlockSpec((tm, LANE), lambda i: (i, 0)),
                  pl.BlockSpec((1, d), lambda i: (0, 0)), pl.BlockSpec((1, d), lambda i: (0, 0)),
                  pl.BlockSpec(memory_space=pl.ANY)],
        out_specs=pl.BlockSpec((tm, d), lambda i: (i, 0)),
        out_shape=jax.ShapeDtypeStruct((n, d), F32),
        scratch_shapes=[pltpu.SMEM((2 * tm,), jnp.int32), pltpu.VMEM((2, tm * SUBL, LANE), F32),
                        pltpu.SemaphoreType.DMA(()), pltpu.SemaphoreType.DMA(())],
        compiler_params=_cparams(("arbitrary",)),
        name="combine",
    )(dest.reshape(nt, 2 * tm), x, gw, ln_g.reshape(1, d), ln_b.reshape(1, d), ys)


def _router_params(w_grp, b_grp, w_exp, b_exp):
    d = w_exp.shape[0]
    wr = jnp.zeros((d, LANE), F32).at[:, :N_EXPERTS].set(w_exp).at[:, N_EXPERTS:N_EXPERTS + N_GROUPS].set(w_grp)
    br = jnp.zeros((1, LANE), F32).at[0, :N_EXPERTS].set(b_exp).at[0, N_EXPERTS:N_EXPERTS + N_GROUPS].set(b_grp)
    return wr, br


def _moe(x, routing, w_gate, w_up, w_down, layer, ln_g, ln_b):
    n, d = x.shape
    ids, gw, cnt = routing
    bm = MOE_BM
    counts = cnt[0, :N_EXPERTS]
    padded = (counts + bm - 1) // bm * bm
    pend = jnp.cumsum(padded)
    pstart = pend - padded
    e, r = ids[:, 0:2], ids[:, 2:4]
    onehot = e[:, :, None] == jnp.arange(N_EXPERTS, dtype=jnp.int32)[None, None, :]
    dest = jnp.sum(jnp.where(onehot, pstart[None, None, :], 0), axis=-1) + r
    cap = 2 * n + N_EXPERTS * bm
    nb = cap // bm
    blk_start = jnp.arange(nb, dtype=jnp.int32) * bm
    blk_e = jnp.minimum(jnp.sum((pend[None, :] <= blk_start[:, None]).astype(jnp.int32), axis=1), N_EXPERTS - 1)
    n_used = (pend[-1:] // bm).astype(jnp.int32)
    xs = _dispatch(x, dest, pend.astype(jnp.int32), padded.astype(jnp.int32), cap)
    ys = _experts(xs, blk_e, n_used, w_gate, w_up, w_down, layer)
    return _combine(x, ys, dest, gw, ln_g, ln_b)


def _even_layer(x, batch, seq, layer_idx, w_in, dec_f, dec_b, lq1, lk1, lq2, lk2, subln, w_out, ln_g, ln_b, route):
    d = x.shape[1]
    w = HEADS * LANE
    kw = dict(batch=batch, seq=seq)
    diff_seg = [(0, DIFF_ROT_DIM), (DIFF_HEAD_DIM, DIFF_ROT_DIM)]
    q, k, v, gate, dq, dk, dvt = _even_in(
        x, w_in.astype(BF16), _rope_tables(seq, [(0, LANE)], RET_THETA),
        _rope_tables(seq, diff_seg, ROPE_THETA), **kw)
    decays = jnp.stack([dec_f, dec_b]).astype(F32)
    of, ob = _retention(q, k, v, decays, **kw)
    lam_init = 0.8 - 0.6 * math.exp(-0.3 * layer_idx)
    diff = _flash(dq, dk, dvt, diff=(lq1, lk1, lq2, lk2, subln), lam_init=lam_init, **kw)
    wo = w_out.astype(BF16)
    return _outproj(x, diff, of, ob, gate, 0, jnp.ones((LANE,), F32), wo[w:], wo[:w], ln_g, ln_b, *route,
                    group_norm=True)


def _odd_layer(x, batch, seq, w_in, q_norm, w_uq, kv_norm, w_ukv, w2_f, b_f, w2_b, b_b, gla_norm, w_out, ln_g, ln_b,
               route):
    d = x.shape[1]
    w = HEADS * LANE
    o = np.cumsum([0, MLA_Q_RANK, MLA_KV_RANK, MLA_ROPE, HEADS * GLA_K_DIM, HEADS * GLA_K_DIM, w, w,
                   GLA_GATE_RANK, GLA_GATE_RANK]).tolist()
    zeros = lambda c: jnp.zeros((d, c), F32)
    w_in2 = jnp.concatenate([
        w_in[:, o[0]:o[2]], zeros(MLA_NOPE), w_in[:, o[2]:o[3]], zeros(LANE - MLA_NOPE - MLA_ROPE),
        w_in[:, o[3]:o[7]], w_in[:, o[7]:o[9]], zeros(LANE - 2 * GLA_GATE_RANK)], axis=1).astype(BF16)
    kw = dict(batch=batch, seq=seq)
    qd = MLA_NOPE + MLA_ROPE
    w_uq2 = jnp.pad(w_uq.reshape(MLA_Q_RANK, HEADS, qd), ((0, 0), (0, 0), (0, LANE - qd))).reshape(MLA_Q_RANK, w)
    ukv = w_ukv.reshape(MLA_KV_RANK, HEADS, MLA_NOPE + MLA_V)
    w_uk2 = jnp.pad(ukv[:, :, :MLA_NOPE], ((0, 0), (0, 0), (0, LANE - MLA_NOPE))).reshape(MLA_KV_RANK, w)
    w_uv2 = ukv[:, :, MLA_NOPE:].reshape(MLA_KV_RANK, w)
    q, k, vt, zg = _odd_in(x, w_in2, q_norm, w_uq2.astype(BF16), kv_norm,
                           jnp.concatenate([w_uk2, w_uv2], axis=1).astype(BF16),
                           _rope_tables(seq, [(MLA_NOPE, MLA_ROPE)], ROPE_THETA), **kw)
    mla = _flash(q, k, vt, **kw)
    wk = HEADS * GLA_K_DIM
    pad_rows = lambda m, r0: jnp.zeros((LANE, wk), F32).at[r0:r0 + GLA_GATE_RANK].set(m).astype(BF16)
    of, ob = _gla(zg, qcol=0, kcol=1, vcol=1, lcol=12,
                  w2f=pad_rows(w2_f, 0), bf=b_f.reshape(1, wk), w2b=pad_rows(w2_b, GLA_GATE_RANK), bb=b_b.reshape(1, wk), **kw)
    wo = w_out.astype(BF16)
    return _outproj(x, mla, of, ob, zg, 2, gla_norm, wo[:w], wo[w:], ln_g, ln_b, *route, group_norm=False)


def kernel(x, ev_w_in, ev_ret_decay_f, ev_ret_decay_b, ev_lq1, ev_lk1, ev_lq2, ev_lk2, ev_subln, ev_w_out, od_w_in, od_q_norm, od_w_uq, od_kv_norm, od_w_ukv, od_gla_w2_f, od_gla_b_f, od_gla_w2_b, od_gla_b_b, od_gla_norm, od_w_out, ln1_g, ln1_b, ln2_g, ln2_b, moe_w_grp, moe_b_grp, moe_w_exp, moe_b_exp, moe_w_gate, moe_w_up, moe_w_down):
    batch, seq, d = x.shape
    h = x.reshape(batch * seq, d)
    for i in range(DEPTH):
        j = i // 2
        route = _router_params(moe_w_grp[i], moe_b_grp[i], moe_w_exp[i], moe_b_exp[i])
        if i % 2 == 0:
            h, *routing = _even_layer(h, batch, seq, i, ev_w_in[j], ev_ret_decay_f[j], ev_ret_decay_b[j], ev_lq1[j],
                                      ev_lk1[j], ev_lq2[j], ev_lk2[j], ev_subln[j], ev_w_out[j], ln1_g[i], ln1_b[i], route)
        else:
            h, *routing = _odd_layer(h, batch, seq, od_w_in[j], od_q_norm[j], od_w_uq[j], od_kv_norm[j], od_w_ukv[j],
                                     od_gla_w2_f[j], od_gla_b_f[j], od_gla_w2_b[j], od_gla_b_b[j], od_gla_norm[j],
                                     od_w_out[j], ln1_g[i], ln1_b[i], route)
        h = _moe(h, routing, moe_w_gate, moe_w_up, moe_w_down, i, ln2_g[i], ln2_b[i])
    return h.reshape(batch, seq, d)
```

```python
import functools
import math

import numpy as np
import jax
import jax.numpy as jnp
from jax import lax
from jax.experimental import pallas as pl
from jax.experimental.pallas import tpu as pltpu

F32 = jnp.float32
BF16 = jnp.bfloat16

HEADS = 4
LANE = 128
RET_THETA = 10000.0
ROPE_THETA = 500000.0
DIFF_HEAD_DIM = 64
DIFF_ROT_DIM = 16
MLA_Q_RANK = 256
MLA_KV_RANK = 128
MLA_NOPE = 64
MLA_ROPE = 32
MLA_V = 128
GLA_K_DIM = 64
GLA_GATE_RANK = 16
GLA_TAU = 16.0
N_GROUPS = 4
EXPERTS_PER_GROUP = 8
N_EXPERTS = N_GROUPS * EXPERTS_PER_GROUP
DEPTH = 2
ALPHA = (2.0 * DEPTH) ** 0.25
LN_EPS = 1e-5
RMS_EPS = 1e-6

VMEM_LIMIT = 48 * 1024 * 1024


def _div_pow2(x, n):
    return lax.shift_right_logical(x, int(n).bit_length() - 1)


def _mod_pow2(x, n):
    return lax.bitwise_and(x, int(n) - 1)


def _cparams(sem):
    return pltpu.CompilerParams(dimension_semantics=sem, vmem_limit_bytes=VMEM_LIMIT)


def _rope_heads(z, tabs, sh, scale):
    c, sa, sb = tabs
    outs = []
    for h in range(HEADS):
        zh = z[:, h * LANE:(h + 1) * LANE]
        outs.append((zh * c + pltpu.roll(zh, sh, axis=1) * sa + pltpu.roll(zh, LANE - sh, axis=1) * sb) * scale)
    return outs


def _even_in_kernel(x_ref, w_ref, rc, rsa, rsb, dc, dsa, dsb, q_ref, k_ref, v_ref, g_ref, dq_ref, dk_ref, dvt_ref):
    w = HEADS * LANE
    xb = x_ref[...].astype(BF16)
    part = lambda t: jnp.dot(xb, w_ref[:, t * w:(t + 1) * w], preferred_element_type=F32)
    ret_t = (rc[...], rsa[...], rsb[...])
    diff_t = (dc[...], dsa[...], dsb[...])
    for h, o in enumerate(_rope_heads(part(0), ret_t, LANE // 2, 1.0)):
        q_ref[h] = o.astype(BF16)
    for h, o in enumerate(_rope_heads(part(1), ret_t, LANE // 2, LANE ** -0.5)):
        k_ref[h] = o.astype(BF16)
    rv = part(2)
    for h in range(HEADS):
        v_ref[h] = rv[:, h * LANE:(h + 1) * LANE].astype(BF16)
    g_ref[...] = part(3)
    for h, o in enumerate(_rope_heads(part(4), diff_t, DIFF_ROT_DIM // 2, DIFF_HEAD_DIM ** -0.5 * LOG2E)):
        dq_ref[h] = o.T.astype(BF16)
    for h, o in enumerate(_rope_heads(part(5), diff_t, DIFF_ROT_DIM // 2, 1.0)):
        dk_ref[h] = o.astype(BF16)
    dv = part(6)
    for h in range(HEADS):
        dvt_ref[h, :LANE, :] = dv[:, h * LANE:(h + 1) * LANE].T.astype(BF16)
        dvt_ref[h, LANE:, :] = jnp.ones((ONES_ROWS, dvt_ref.shape[2]), BF16)


def _even_in(x, w, ret_tabs, diff_tabs, *, batch, seq):
    n, d = x.shape
    tm = min(FLASH_TK_DIFF, seq // 2)
    nt = seq // tm
    hw = HEADS * LANE
    heads = jax.ShapeDtypeStruct((batch, HEADS, seq, LANE), BF16)
    head_spec = pl.BlockSpec((None, HEADS, tm, LANE), lambda i: (i // nt, 0, i % nt, 0))
    heads_t = jax.ShapeDtypeStruct((batch, HEADS, LANE, seq), BF16)
    head_t_spec = pl.BlockSpec((None, HEADS, LANE, tm), lambda i: (i // nt, 0, 0, i % nt))
    tab_spec = pl.BlockSpec((tm, LANE), lambda i: (i % nt, 0))
    return pl.pallas_call(
        _even_in_kernel,
        grid=(n // tm,),
        in_specs=[pl.BlockSpec((tm, d), lambda i: (i, 0)), pl.BlockSpec((d, 7 * hw), lambda i: (0, 0))] + [tab_spec] * 6,
        out_specs=[head_spec, head_spec, head_spec, pl.BlockSpec((tm, hw), lambda i: (i, 0)), head_t_spec, head_spec,
                   pl.BlockSpec((None, HEADS, None, LANE + ONES_ROWS, tm), lambda i: (i // nt, 0, i % nt, 0, 0))],
        out_shape=[heads, heads, heads, jax.ShapeDtypeStruct((n, hw), F32), heads_t, heads,
                   jax.ShapeDtypeStruct((batch, HEADS, nt, LANE + ONES_ROWS, tm), BF16)],
        compiler_params=_cparams(("parallel",)),
        name="even_in",
    )(x, w, *ret_tabs, *diff_tabs)


def _rms_rows(z, g):
    return z * lax.rsqrt(jnp.mean(z * z, axis=-1, keepdims=True) + RMS_EPS) * g


def _odd_in_kernel(x_ref, w_ref, qn_ref, wq_ref, kvn_ref, wkv_ref, tc, tsa, tsb, q_ref, k_ref, vt_ref, zg_ref):
    hw = HEADS * LANE
    mla_w = MLA_Q_RANK + MLA_KV_RANK + LANE
    xb = x_ref[...].astype(BF16)
    zg_ref[...] = jnp.dot(xb, w_ref[:, mla_w:], preferred_element_type=F32)
    z1 = jnp.dot(xb, w_ref[:, :mla_w], preferred_element_type=F32)
    tabs = (tc[...], tsa[...], tsb[...])
    sh = MLA_ROPE // 2
    qh = jnp.dot(_rms_rows(z1[:, :MLA_Q_RANK], qn_ref[...]).astype(BF16), wq_ref[...], preferred_element_type=F32)
    for h, o in enumerate(_rope_heads(qh, tabs, sh, (MLA_NOPE + MLA_ROPE) ** -0.5 * LOG2E)):
        q_ref[h] = o.T.astype(BF16)
    ckv = _rms_rows(z1[:, MLA_Q_RANK:MLA_Q_RANK + MLA_KV_RANK], kvn_ref[...]).astype(BF16)
    kv = jnp.dot(ckv, wkv_ref[...], preferred_element_type=F32)
    kr = z1[:, MLA_Q_RANK + MLA_KV_RANK:]
    kr = kr * tabs[0] + pltpu.roll(kr, sh, axis=1) * tabs[1] + pltpu.roll(kr, LANE - sh, axis=1) * tabs[2]
    for h in range(HEADS):
        k_ref[h] = (kv[:, h * LANE:(h + 1) * LANE] + kr).astype(BF16)
        vt_ref[h, :LANE, :] = kv[:, hw + h * LANE:hw + (h + 1) * LANE].T.astype(BF16)
        vt_ref[h, LANE:, :] = jnp.ones((ONES_ROWS, vt_ref.shape[2]), BF16)


def _odd_in(x, w, q_norm, w_uq, kv_norm, w_ukv, tabs, *, batch, seq, tm=512):
    n, d = x.shape
    tm = min(tm, seq // 2)
    nt = seq // tm
    tk = min(FLASH_TK, seq // 2)
    per = tk // tm
    hw = HEADS * LANE
    gw_ = w.shape[1] - (MLA_Q_RANK + MLA_KV_RANK + LANE)
    heads = jax.ShapeDtypeStruct((batch, HEADS, seq, LANE), BF16)
    head_spec = pl.BlockSpec((None, HEADS, tm, LANE), lambda i: (i // nt, 0, i % nt, 0))
    tab_spec = pl.BlockSpec((tm, LANE), lambda i: (i % nt, 0))
    const = lambda i: (0, 0)
    return pl.pallas_call(
        _odd_in_kernel,
        grid=(n // tm,),
        in_specs=[pl.BlockSpec((tm, d), lambda i: (i, 0)), pl.BlockSpec(w.shape, const),
                  pl.BlockSpec((1, MLA_Q_RANK), const), pl.BlockSpec(w_uq.shape, const),
                  pl.BlockSpec((1, MLA_KV_RANK), const), pl.BlockSpec(w_ukv.shape, const)] + [tab_spec] * 3,
        out_specs=[pl.BlockSpec((None, HEADS, LANE, tm), lambda i: (i // nt, 0, 0, i % nt)), head_spec,
                   pl.BlockSpec((None, HEADS, None, LANE + ONES_ROWS, tm),
                                lambda i: (i // nt, 0, (i % nt) // per, 0, (i % nt) % per)),
                   pl.BlockSpec((tm, gw_), lambda i: (i, 0))],
        out_shape=[jax.ShapeDtypeStruct((batch, HEADS, LANE, seq), BF16), heads, jax.ShapeDtypeStruct((batch, HEADS, seq // tk, LANE + ONES_ROWS, tk), BF16),
                   jax.ShapeDtypeStruct((n, gw_), F32)],
        compiler_params=_cparams(("parallel",)),
        name="odd_in",
    )(x, w, q_norm.reshape(1, -1), w_uq, kv_norm.reshape(1, -1), w_ukv, *tabs)


def _rope_tables(seq, segs, theta):
    pos = jnp.arange(seq, dtype=F32)
    inv = jnp.zeros((LANE,), F32)
    lo = np.zeros((LANE,), bool)
    hi = np.zeros((LANE,), bool)
    for start, rot in segs:
        half = rot // 2
        f = jnp.power(jnp.float32(theta), -jnp.arange(0, rot, 2, dtype=F32) / rot)
        inv = inv.at[start:start + half].set(f).at[start + half:start + rot].set(f)
        lo[start:start + half] = True
        hi[start + half:start + rot] = True
    ang = pos[:, None] * inv[None, :]
    cos, sin = jnp.cos(ang), jnp.sin(ang)
    c = jnp.where(lo | hi, cos, 1.0)
    sa = jnp.where(hi, sin, 0.0)
    sb = jnp.where(lo, -sin, 0.0)
    return c, sa, sb


ONES_ROWS = 16
LOG2E = math.log2(math.e)
FLASH_TK = 1024
FLASH_TK_DIFF = 512


def _flash_kernel(*refs, ncomp, nk, lam_init):
    if ncomp == 2:
        q_ref, k_ref, vt_ref, lq1, lk1, lq2, lk2, g_ref, o_ref, *scr = refs
    else:
        q_ref, k_ref, vt_ref, o_ref, *scr = refs
    qm_sc, m_sc, acc_sc, s0, s1, cm0, cm1, p0, p1, al0, al1 = scr
    tk = s0.shape[1]
    q = q_ref[...]
    if ncomp == 2:
        chan = lax.broadcasted_iota(jnp.int32, q.shape, 0)
        zero = jnp.zeros_like(q)
        qm_sc[0] = jnp.where(chan < DIFF_HEAD_DIM, q, zero)
        qm_sc[1] = jnp.where(chan >= DIFF_HEAD_DIM, q, zero)
    else:
        qm_sc[0] = q
    m_sc[...] = jnp.full(m_sc.shape, -jnp.inf, F32)
    acc_sc[...] = jnp.zeros(acc_sc.shape, F32)

    def scores(j, s_ref, cm_ref):
        k = k_ref[j * tk:(j + 1) * tk, :]
        for c in range(ncomp):
            s = jnp.dot(k, qm_sc[c], preferred_element_type=F32)
            s_ref[c] = s
            cm_ref[c] = jnp.max(s, axis=0, keepdims=True)

    def softmax(s_ref, cm_ref, p_ref, al_ref):
        for c in range(ncomp):
            m_old = m_sc[c]
            m_new = jnp.maximum(m_old, cm_ref[c])
            al_ref[c] = jnp.exp2(m_old - m_new)
            p_ref[c] = jnp.exp2(s_ref[c] - m_new).astype(BF16)
            m_sc[c] = m_new

    def values(j, p_ref, al_ref):
        vt = vt_ref[j]
        for c in range(ncomp):
            acc_sc[c] = al_ref[c] * acc_sc[c] + jnp.dot(vt, p_ref[c], preferred_element_type=F32)

    bufs = ((s0, cm0, p0, al0), (s1, cm1, p1, al1))
    scores(0, s0, cm0)
    for j in range(nk):
        s_c, cm_c, p_c, al_c = bufs[j % 2]
        s_n, cm_n, p_n, al_n = bufs[(j + 1) % 2]
        if j + 1 < nk:
            scores(j + 1, s_n, cm_n)
        softmax(s_c, cm_c, p_c, al_c)
        if j >= 1:
            values(j - 1, p_n, al_n)
    values(nk - 1, *bufs[(nk - 1) % 2][2:])

    def normalised(c):
        acc = acc_sc[c]
        return acc[:LANE] / acc[LANE:LANE + 1]

    o = normalised(0)
    if ncomp == 2:
        lam = (jnp.exp(jnp.sum(lq1[...] * lk1[...], keepdims=True))
               - jnp.exp(jnp.sum(lq2[...] * lk2[...], keepdims=True)) + lam_init)
        o = o - lam * normalised(1)
        o = o * lax.rsqrt(jnp.mean(o * o, axis=0, keepdims=True) + RMS_EPS) * g_ref[...] * (1.0 - lam_init)
    o_ref[...] = o.T


def _flash(q, k, vt, *, batch, seq, tq=512, diff=None, lam_init=0.0):
    nk, vrows, tk = vt.shape[2], vt.shape[3], vt.shape[4]
    assert vrows == LANE + ONES_ROWS
    tq = min(tq, seq)
    nq = seq // tq
    ncomp = 2 if diff is not None else 1
    in_specs = [
        pl.BlockSpec((None, None, LANE, tq), lambda b, h, i: (b, h, 0, i)),
        pl.BlockSpec((None, None, seq, LANE), lambda b, h, i: (b, h, 0, 0)),
        pl.BlockSpec((None, None, nk, vrows, tk), lambda b, h, i: (b, h, 0, 0, 0)),
    ]
    args = [q, k, vt]
    if diff is not None:
        lq1, lk1, lq2, lk2, subln = diff
        for v in (lq1, lk1, lq2, lk2):
            in_specs.append(pl.BlockSpec((1, DIFF_HEAD_DIM), lambda b, h, i: (0, 0)))
            args.append(v.reshape(1, DIFF_HEAD_DIM))
        in_specs.append(pl.BlockSpec((LANE, 1), lambda b, h, i: (0, 0)))
        args.append(subln.reshape(LANE, 1))
    return pl.pallas_call(
        functools.partial(_flash_kernel, ncomp=ncomp, nk=nk, lam_init=lam_init),
        grid=(batch, HEADS, nq),
        in_specs=in_specs,
        out_specs=pl.BlockSpec((tq, LANE), lambda b, h, i: (b * nq + i, h)),
        out_shape=jax.ShapeDtypeStruct((batch * seq, HEADS * LANE), F32),
        scratch_shapes=[pltpu.VMEM((ncomp, LANE, tq), BF16),
                        pltpu.VMEM((ncomp, 1, tq), F32), pltpu.VMEM((ncomp, vrows, tq), F32),
                        pltpu.VMEM((ncomp, tk, tq), F32), pltpu.VMEM((ncomp, tk, tq), F32),
                        pltpu.VMEM((ncomp, 1, tq), F32), pltpu.VMEM((ncomp, 1, tq), F32),
                        pltpu.VMEM((ncomp, tk, tq), BF16), pltpu.VMEM((ncomp, tk, tq), BF16),
                        pltpu.VMEM((ncomp, 1, tq), F32), pltpu.VMEM((ncomp, 1, tq), F32)],
        compiler_params=_cparams(("parallel", "parallel", "parallel")),
        name="flash_diff" if diff is not None else "flash_mla",
    )(*args)


def _ret_kernel(dec_ref, qf, kf, vf, qb, kb, vb, of_ref, ob_ref, s_sc, *, chunk):
    @pl.when(pl.program_id(1) == 0)
    def _():
        s_sc[...] = jnp.zeros(s_sc.shape, F32)

    ii = lax.broadcasted_iota(jnp.int32, (chunk, chunk), 0)
    jj = lax.broadcasted_iota(jnp.int32, (chunk, chunk), 1)
    r = lax.broadcasted_iota(jnp.int32, (chunk, 1), 0).astype(F32)
    for d, (q_ref, k_ref, v_ref, o_ref) in enumerate(((qf, kf, vf, of_ref), (qb, kb, vb, ob_ref))):
        for h in range(HEADS):
            la = -jnp.exp(jnp.full((1, 1), dec_ref[d, h], F32))
            if d == 0:
                mask, dist = ii >= jj, (ii - jj).astype(F32)
                qdec, kdec = jnp.exp(la * (r + 1.0)), jnp.exp(la * (chunk - 1.0 - r))
            else:
                mask, dist = jj > ii, (jj - ii).astype(F32)
                qdec, kdec = jnp.exp(la * (chunk - r)), jnp.exp(la * r)
            decay = jnp.where(mask, jnp.exp(jnp.where(mask, dist * la, 0.0)), 0.0)
            q, k, v = q_ref[h], k_ref[h], v_ref[h]
            s = lax.dot_general(q, k, (((1,), (1,)), ((), ())), preferred_element_type=F32)
            o = jnp.dot((s * decay).astype(BF16), v, preferred_element_type=F32)
            state = s_sc[d, h]
            o = o + qdec * jnp.dot(q, state.astype(BF16), preferred_element_type=F32)
            kd = (k.astype(F32) * kdec).astype(BF16)
            s_sc[d, h] = jnp.exp(la * float(chunk)) * state + lax.dot_general(
                kd, v, (((0,), (0,)), ((), ())), preferred_element_type=F32)
            o_ref[:, h * LANE:(h + 1) * LANE] = o


def _retention(q, k, v, decays, *, batch, seq, chunk=256):
    chunk = min(chunk, seq)
    n = seq // chunk
    fwd = pl.BlockSpec((None, HEADS, chunk, LANE), lambda b, c: (b, 0, c, 0))
    bwd = pl.BlockSpec((None, HEADS, chunk, LANE), lambda b, c: (b, 0, n - 1 - c, 0))
    w = HEADS * LANE
    out = jax.ShapeDtypeStruct((batch * seq, w), F32)
    return pl.pallas_call(
        functools.partial(_ret_kernel, chunk=chunk),
        grid=(batch, n),
        in_specs=[pl.BlockSpec(memory_space=pltpu.SMEM), fwd, fwd, fwd, bwd, bwd, bwd],
        out_specs=[pl.BlockSpec((chunk, w), lambda b, c: (b * n + c, 0)),
                   pl.BlockSpec((chunk, w), lambda b, c: (b * n + n - 1 - c, 0))],
        out_shape=[out, out],
        scratch_shapes=[pltpu.VMEM((2, HEADS, LANE, LANE), F32)],
        compiler_params=_cparams(("parallel", "arbitrary")),
        name="retention",
    )(decays, q, k, v, q, k, v)


GLA_SUB = 8


def _split3(x):
    x1 = x.astype(BF16)
    r1 = x - x1.astype(F32)
    x2 = r1.astype(BF16)
    x3 = (r1 - x2.astype(F32)).astype(BF16)
    return x1, x2, x3


def _gla_direction(q, k, v, lr, w2, bias, st, reverse):
    C, wk = q.shape
    wv = v.shape[1]
    dk, dv = wk // HEADS, wv // HEADS
    z = jnp.dot(lr.astype(BF16), w2, preferred_element_type=F32) + bias
    g = (jnp.minimum(z, 0.0) - jnp.log(1.0 + jnp.exp(-jnp.abs(z)))) * (1.0 / GLA_TAU)
    ii = lax.broadcasted_iota(jnp.int32, (C, C), 0)
    jj = lax.broadcasted_iota(jnp.int32, (C, C), 1)
    tri = jnp.where(ii >= jj, 1.0, 0.0).astype(BF16)
    b = sum(jnp.dot(tri, part, preferred_element_type=F32) for part in _split3(g))
    tot = b[C - 1:C, :]
    c = (tot - b + g) if reverse else b

    qe = (q * jnp.exp(jnp.minimum(c, 0.0))).astype(BF16)
    o = lax.dot_general(qe, st.astype(BF16), (((1,), (1,)), ((), ())), preferred_element_type=F32)
    ke = (k * jnp.exp(jnp.minimum(tot - c, 0.0))).astype(BF16)
    upd = lax.dot_general(v.astype(BF16), ke, (((0,), (0,)), ((), ())), preferred_element_type=F32)
    rr = _div_pow2(lax.broadcasted_iota(jnp.int32, (wv, wk), 0), dv)
    cc = _div_pow2(lax.broadcasted_iota(jnp.int32, (wv, wk), 1), dk)
    new_st = jnp.where(rr == cc, st * jnp.exp(tot) + upd, 0.0)

    lane_head = _div_pow2(lax.broadcasted_iota(jnp.int32, (C, wk), 1), dk)
    scores = [jnp.zeros((C, C), F32) for _ in range(HEADS)]
    hsz = C // 2
    while hsz >= GLA_SUB:
        blk = 2 * hsz
        rows = []
        for m in range(C // blk):
            rrow = m * blk + (hsz if reverse else hsz - 1)
            rows.append(jnp.broadcast_to(c[rrow:rrow + 1, :], (blk, wk)))
        ref = jnp.concatenate(rows, axis=0) if len(rows) > 1 else rows[0]
        qt = q * jnp.exp(jnp.minimum(c - ref, 0.0))
        kt = (k * jnp.exp(jnp.minimum(ref - c, 0.0))).astype(BF16)
        same = _div_pow2(ii, blk) == _div_pow2(jj, blk)
        if reverse:
            lvl = same & (_mod_pow2(ii, blk) < hsz) & (_mod_pow2(jj, blk) >= hsz)
        else:
            lvl = same & (_mod_pow2(ii, blk) >= hsz) & (_mod_pow2(jj, blk) < hsz)
        for h in range(HEADS):
            qh = jnp.where(lane_head == h, qt, 0.0).astype(BF16)
            s = lax.dot_general(qh, kt, (((1,), (1,)), ((), ())), preferred_element_type=F32)
            scores[h] = scores[h] + jnp.where(lvl, s, 0.0)
        hsz //= 2

    assert dv == C
    er = _div_pow2(lax.broadcasted_iota(jnp.int32, (wk, wv), 0), dk)
    ec = _div_pow2(lax.broadcasted_iota(jnp.int32, (wk, wv), 1), dv)
    expand = jnp.where(er == ec, 1.0, 0.0).astype(BF16)
    dist = (jj - ii) if reverse else (ii - jj)
    same_sub = _div_pow2(ii, GLA_SUB) == _div_pow2(jj, GLA_SUB)
    for lag in range(1 if reverse else 0, GLA_SUB):
        if lag == 0:
            t = q * k
        else:
            shift = (C - lag) if reverse else lag
            ks, cs = pltpu.roll(k, shift, axis=0), pltpu.roll(c, shift, axis=0)
            t = q * ks * jnp.exp(jnp.minimum(c - cs, 0.0))
        red = jnp.dot(t.astype(BF16), expand, preferred_element_type=F32)
        on_diag = same_sub & (dist == lag)
        for h in range(HEADS):
            scores[h] = scores[h] + jnp.where(on_diag, red[:, h * dv:(h + 1) * dv], 0.0)

    vb = v.astype(BF16)
    o = o + jnp.concatenate(
        [jnp.dot(scores[h].astype(BF16), vb[:, h * dv:(h + 1) * dv], preferred_element_type=F32) for h in range(HEADS)],
        axis=1)
    return o, new_st


def _gla_kernel(qf, kf, vf, lf, qb, kb, vb, lb, w2f, bf, w2b, bb, of_ref, ob_ref, s_sc, *, qscale):
    @pl.when(pl.program_id(1) == 0)
    def _():
        s_sc[...] = jnp.zeros(s_sc.shape, F32)

    o, st = _gla_direction(qf[...] * qscale, kf[...], vf[...], lf[...], w2f[...], bf[...], s_sc[0], False)
    of_ref[...] = o
    s_sc[0] = st
    o, st = _gla_direction(qb[...] * qscale, kb[...], vb[...], lb[...], w2b[...], bb[...], s_sc[1], True)
    ob_ref[...] = o
    s_sc[1] = st


def _gla(z, *, qcol, kcol, vcol, lcol, w2f, bf, w2b, bb, batch, seq, chunk=128):
    chunk = min(chunk, seq)
    n = seq // chunk
    wk, wv = HEADS * GLA_K_DIM, HEADS * LANE

    def specs(cmap):
        return [pl.BlockSpec((chunk, wk), lambda b, c: (cmap(b, c), qcol)),
                pl.BlockSpec((chunk, wk), lambda b, c: (cmap(b, c), kcol)),
                pl.BlockSpec((chunk, wv), lambda b, c: (cmap(b, c), vcol)),
                pl.BlockSpec((chunk, LANE), lambda b, c: (cmap(b, c), lcol))]

    fmap = lambda b, c: b * n + c
    bmap = lambda b, c: b * n + n - 1 - c
    wspec = [pl.BlockSpec((LANE, wk), lambda b, c: (0, 0)), pl.BlockSpec((1, wk), lambda b, c: (0, 0))]
    out = jax.ShapeDtypeStruct((batch * seq, wv), F32)
    return pl.pallas_call(
        functools.partial(_gla_kernel, qscale=GLA_K_DIM ** -0.5),
        grid=(batch, n),
        in_specs=specs(fmap) + specs(bmap) + wspec + wspec,
        out_specs=[pl.BlockSpec((chunk, wv), lambda b, c: (fmap(b, c), 0)),
                   pl.BlockSpec((chunk, wv), lambda b, c: (bmap(b, c), 0))],
        out_shape=[out, out],
        scratch_shapes=[pltpu.VMEM((2, wv, wk), F32)],
        compiler_params=_cparams(("parallel", "arbitrary")),
        name="gla",
    )(z, z, z, z, z, z, z, z, w2f, bf, w2b, bb)


def _layer_norm_rows(r, g, b):
    mu = jnp.mean(r, axis=-1, keepdims=True)
    d = r - mu
    var = jnp.mean(d * d, axis=-1, keepdims=True)
    return d * lax.rsqrt(var + LN_EPS) * g + b


def _outproj_kernel(x_ref, fin_ref, of_ref, ob_ref, gate_ref, ng_ref, wa_ref, wb_ref, lg_ref, lb_ref, rw_ref, rb_ref,
                    o_ref, ids_ref, gw_ref, cnt_ref, run_sc, *, group_norm):
    lin = of_ref[...] + ob_ref[...]
    parts = []
    for h in range(HEADS):
        zh = lin[:, h * LANE:(h + 1) * LANE]
        if group_norm:
            mu = jnp.mean(zh, axis=-1, keepdims=True)
            dz = zh - mu
            parts.append(dz * lax.rsqrt(jnp.mean(dz * dz, axis=-1, keepdims=True) + LN_EPS))
        else:
            parts.append(zh * lax.rsqrt(jnp.mean(zh * zh, axis=-1, keepdims=True) + RMS_EPS) * ng_ref[...])
    gate = gate_ref[...]
    lin = jnp.concatenate(parts, axis=1) * (gate * jax.nn.sigmoid(gate))
    y = (jnp.dot(fin_ref[...].astype(BF16), wa_ref[...], preferred_element_type=F32)
         + jnp.dot(lin.astype(BF16), wb_ref[...], preferred_element_type=F32))
    x1 = _layer_norm_rows(ALPHA * x_ref[...] + y, lg_ref[...], lb_ref[...])
    o_ref[...] = x1
    _route_rows(x1, rw_ref, rb_ref, ids_ref, gw_ref, cnt_ref, run_sc)


def _outproj(x, fin, of, ob, gate_src, gate_col, norm_gain, wa, wb, ln_g, ln_b, route_w, route_b, *, group_norm, tm=512):
    n, d = x.shape
    w = HEADS * LANE
    tm = min(tm, n)
    row = lambda i: (i, 0)
    const = lambda i: (0, 0)
    rw_hi = route_w.astype(BF16)
    rw = jnp.stack([rw_hi, (route_w - rw_hi.astype(F32)).astype(BF16)])
    return pl.pallas_call(
        functools.partial(_outproj_kernel, group_norm=group_norm),
        grid=(n // tm,),
        in_specs=[pl.BlockSpec((tm, d), row), pl.BlockSpec((tm, w), row), pl.BlockSpec((tm, w), row),
                  pl.BlockSpec((tm, w), row), pl.BlockSpec((tm, w), lambda i: (i, gate_col)),
                  pl.BlockSpec((1, LANE), const), pl.BlockSpec((w, d), const), pl.BlockSpec((w, d), const),
                  pl.BlockSpec((1, d), const), pl.BlockSpec((1, d), const),
                  pl.BlockSpec((2, d, LANE), lambda i: (0, 0, 0)), pl.BlockSpec((1, LANE), const)],
        out_specs=[pl.BlockSpec((tm, d), row), pl.BlockSpec((tm, LANE), row), pl.BlockSpec((tm, LANE), row),
                   pl.BlockSpec((1, LANE), const)],
        out_shape=[jax.ShapeDtypeStruct((n, d), F32), jax.ShapeDtypeStruct((n, LANE), jnp.int32),
                   jax.ShapeDtypeStruct((n, LANE), F32), jax.ShapeDtypeStruct((1, LANE), jnp.int32)],
        scratch_shapes=[pltpu.VMEM((1, LANE), F32)],
        compiler_params=_cparams(("arbitrary",)),
        name="outproj",
    )(x, fin, of, ob, gate_src, norm_gain.reshape(1, LANE), wa, wb, ln_g.reshape(1, d), ln_b.reshape(1, d), rw, route_b)


def _route_rows(x, w_ref, b_ref, ids_ref, gw_ref, cnt_ref, run_sc):
    @pl.when(pl.program_id(0) == 0)
    def _():
        run_sc[...] = jnp.zeros(run_sc.shape, F32)

    tm = x.shape[0]
    xh = x.astype(BF16)
    xl = (x - xh.astype(F32)).astype(BF16)
    wh, wl = w_ref[0], w_ref[1]
    logits = (jnp.dot(xh, wh, preferred_element_type=F32) + jnp.dot(xh, wl, preferred_element_type=F32)
              + jnp.dot(xl, wh, preferred_element_type=F32)) + b_ref[...]
    lane = lax.broadcasted_iota(jnp.int32, logits.shape, 1)
    neg = -jnp.inf
    gmask = (lane >= N_EXPERTS) & (lane < N_EXPERTS + N_GROUPS)
    gl = jnp.where(gmask, logits, neg)
    gmax = jnp.max(gl, axis=1, keepdims=True)
    lane_f = lane.astype(F32)
    first = lambda hit: jnp.min(jnp.where(hit, lane_f, float(LANE)), axis=1, keepdims=True).astype(jnp.int32)
    gidx = first(gl == gmax) - N_EXPERTS
    p_grp = 1.0 / jnp.sum(jnp.where(gmask, jnp.exp(gl - gmax), 0.0), axis=1, keepdims=True)
    el = jnp.where(_div_pow2(lane, EXPERTS_PER_GROUP) == gidx, logits, neg)
    l1 = jnp.max(el, axis=1, keepdims=True)
    e1 = first(el == l1)
    el2 = jnp.where(lane == e1, neg, el)
    l2 = jnp.max(el2, axis=1, keepdims=True)
    e2 = first(el2 == l2)
    t = jnp.exp(l2 - l1)
    w1 = p_grp / (1.0 + t)
    w2 = p_grp * t / (1.0 + t)

    onehot = jnp.where(lane == e1, 1.0, jnp.where(lane == e2, 1.0, 0.0))
    ri = lax.broadcasted_iota(jnp.int32, (tm, tm), 0)
    ci = lax.broadcasted_iota(jnp.int32, (tm, tm), 1)
    before = jnp.dot(jnp.where(ri > ci, 1.0, 0.0).astype(BF16), onehot.astype(BF16), preferred_element_type=F32)
    before = before + run_sc[...]
    r1 = jnp.sum(jnp.where(lane == e1, before, 0.0), axis=1, keepdims=True).astype(jnp.int32)
    r2 = jnp.sum(jnp.where(lane == e2, before, 0.0), axis=1, keepdims=True).astype(jnp.int32)
    run_sc[...] = run_sc[...] + jnp.sum(onehot, axis=0, keepdims=True)
    cnt_ref[...] = run_sc[...].astype(jnp.int32)
    ids_ref[...] = jnp.where(lane == 0, e1, jnp.where(lane == 1, e2, jnp.where(lane == 2, r1, jnp.where(lane == 3, r2, 0))))
    gw_ref[...] = jnp.where(lane == 0, w1, jnp.where(lane == 1, w2, 0.0))


MOE_BM = 256
SUBL = 8


def _rows_from_linear(ref, rows):
    return jnp.concatenate([ref[pl.ds(s, rows, stride=SUBL), :] for s in range(SUBL)], axis=1)


def _rows_to_linear(ref, val):
    for s in range(SUBL):
        ref[pl.ds(s, val.shape[0], stride=SUBL), :] = val[:, s * LANE:(s + 1) * LANE]


def _dispatch_kernel(pend_ref, padded_ref, dest_ref, x_ref, xs_hbm, idx_smem, zbuf, lin, sem_i, sem_z, sem):
    i = pl.program_id(0)
    tm = x_ref.shape[0]
    bm = zbuf.shape[0] // SUBL

    @pl.when(i == 0)
    def _():
        zbuf[...] = jnp.zeros(zbuf.shape, F32)

        def tail(e):
            start_row = pl.multiple_of((pend_ref[e] - bm) * SUBL, bm * SUBL)
            return pltpu.make_async_copy(zbuf, xs_hbm.at[pl.ds(start_row, bm * SUBL), :], sem_z)

        def start(e, carry):
            @pl.when(padded_ref[e] > 0)
            def _():
                tail(e).start()
            return carry

        def wait(e, carry):
            @pl.when(padded_ref[e] > 0)
            def _():
                tail(e).wait()
            return carry

        lax.fori_loop(0, N_EXPERTS, start, 0)
        lax.fori_loop(0, N_EXPERTS, wait, 0)

        def unused(b):
            start_row = pl.multiple_of(b * bm * SUBL, bm * SUBL)
            return pltpu.make_async_copy(zbuf, xs_hbm.at[pl.ds(start_row, bm * SUBL), :], sem_z)

        first_unused = pend_ref[N_EXPERTS - 1] // bm
        n_blocks = xs_hbm.shape[0] // (bm * SUBL)
        lax.fori_loop(first_unused, n_blocks, lambda b, c: (unused(b).start(), c)[1], 0)
        lax.fori_loop(first_unused, n_blocks, lambda b, c: (unused(b).wait(), c)[1], 0)

    cp = pltpu.make_async_copy(dest_ref.at[i], idx_smem, sem_i)
    cp.start()
    _rows_to_linear(lin, x_ref[...])
    cp.wait()

    def scatter(r, carry):
        src = lin.at[pl.ds(pl.multiple_of(r * SUBL, SUBL), SUBL), :]
        for k in range(2):
            dst = pl.multiple_of(idx_smem[2 * r + k] * SUBL, SUBL)
            pltpu.make_async_copy(src, xs_hbm.at[pl.ds(dst, SUBL), :], sem).start(priority=k)
        return carry

    lax.fori_loop(0, tm, scatter, 0, unroll=8)
    for k in range(2):
        pltpu.make_async_copy(lin, xs_hbm.at[pl.ds(0, tm * SUBL), :], sem).wait()


def _dispatch(x, dest, pend, padded, cap, *, tm=512):
    n, d = x.shape
    assert d == SUBL * LANE
    tm = min(tm, n)
    nt = n // tm
    grid_spec = pltpu.PrefetchScalarGridSpec(
        num_scalar_prefetch=2,
        grid=(nt,),
        in_specs=[pl.BlockSpec((nt, 2 * tm), lambda i, pe, pa: (0, 0)),
                  pl.BlockSpec((tm, d), lambda i, pe, pa: (i, 0))],
        out_specs=pl.BlockSpec(memory_space=pl.ANY),
        scratch_shapes=[pltpu.SMEM((2 * tm,), jnp.int32), pltpu.VMEM((MOE_BM * SUBL, LANE), F32),
                        pltpu.VMEM((tm * SUBL, LANE), F32),
                        pltpu.SemaphoreType.DMA(()), pltpu.SemaphoreType.DMA(()), pltpu.SemaphoreType.DMA(())],
    )
    return pl.pallas_call(
        _dispatch_kernel,
        grid_spec=grid_spec,
        out_shape=jax.ShapeDtypeStruct((cap * SUBL, LANE), F32),
        compiler_params=_cparams(("arbitrary",)),
        name="dispatch",
    )(pend, padded, dest.reshape(nt, 2 * tm), x)


def _experts_kernel(blk_e_ref, nused_ref, xs_ref, wg_ref, wu_ref, wd_ref, ys_ref, wgb, wub, wdb):
    i = pl.program_id(0)

    @pl.when(i < nused_ref[0])
    def _():
        prev = blk_e_ref[jnp.maximum(i - 1, 0)]

        @pl.when(jnp.logical_or(i == 0, blk_e_ref[i] != prev))
        def _():
            wgb[...] = wg_ref[...].astype(BF16)
            wub[...] = wu_ref[...].astype(BF16)
            wdb[...] = wd_ref[...].astype(BF16)

        xb = _rows_from_linear(xs_ref, xs_ref.shape[0] // SUBL).astype(BF16)
        hg = jnp.dot(xb, wgb[...], preferred_element_type=F32)
        hu = jnp.dot(xb, wub[...], preferred_element_type=F32)
        hid = (hg * jax.nn.sigmoid(hg) * hu).astype(BF16)
        _rows_to_linear(ys_ref, jnp.dot(hid, wdb[...], preferred_element_type=F32))

    @pl.when(i >= nused_ref[0])
    def _():
        ys_ref[...] = jnp.zeros(ys_ref.shape, F32)


def _experts(xs, blk_e, n_used, w_gate, w_up, w_down, layer):
    cap = xs.shape[0] // SUBL
    bm = MOE_BM
    d, de = w_gate.shape[2], w_gate.shape[3]
    row_in = lambda i, be, nu: (jnp.minimum(i, nu[0] - 1), 0)
    row = lambda i, be, nu: (i, 0)
    grid_spec = pltpu.PrefetchScalarGridSpec(
        num_scalar_prefetch=2,
        grid=(cap // bm,),
        in_specs=[pl.BlockSpec((bm * SUBL, LANE), row_in),
                  pl.BlockSpec((None, None, d, de), lambda i, be, nu: (layer, be[i], 0, 0)),
                  pl.BlockSpec((None, None, d, de), lambda i, be, nu: (layer, be[i], 0, 0)),
                  pl.BlockSpec((None, None, de, d), lambda i, be, nu: (layer, be[i], 0, 0))],
        out_specs=pl.BlockSpec((bm * SUBL, LANE), row),
        scratch_shapes=[pltpu.VMEM((d, de), BF16), pltpu.VMEM((d, de), BF16), pltpu.VMEM((de, d), BF16)],
    )
    return pl.pallas_call(
        _experts_kernel,
        grid_spec=grid_spec,
        out_shape=jax.ShapeDtypeStruct((cap * SUBL, LANE), F32),
        compiler_params=_cparams(("arbitrary",)),
        name="experts",
    )(blk_e, n_used, xs, w_gate, w_up, w_down)


def _combine_kernel(dest_ref, x_ref, gw_ref, g_ref, b_ref, ys_hbm, o_ref, idx_smem, ybuf, sem_i, sem):
    i = pl.program_id(0)
    tm = x_ref.shape[0]

    def issue(tile, slot):
        cp = pltpu.make_async_copy(dest_ref.at[tile], idx_smem, sem_i)
        cp.start()
        cp.wait()

        def gather(r, carry):
            row = pl.multiple_of(r * SUBL, SUBL)
            for k in range(2):
                src = pl.multiple_of(idx_smem[2 * r + k] * SUBL, SUBL)
                pltpu.make_async_copy(ys_hbm.at[pl.ds(src, SUBL), :], ybuf.at[slot, k, pl.ds(row, SUBL), :],
                                      sem.at[slot]).start(priority=k)
            return carry

        lax.fori_loop(0, tm, gather, 0, unroll=8)

    slot = lax.rem(i, 2)

    @pl.when(i == 0)
    def _():
        issue(0, 0)

    @pl.when(i + 1 < pl.num_programs(0))
    def _():
        issue(i + 1, 1 - slot)

    for k in range(2):
        pltpu.make_async_copy(ys_hbm.at[pl.ds(0, tm * SUBL), :], ybuf.at[slot, k], sem.at[slot]).wait()
    gw = gw_ref[...]
    ffn = (_rows_from_linear(ybuf.at[slot, 0], tm) * gw[:, 0:1] + _rows_from_linear(ybuf.at[slot, 1], tm) * gw[:, 1:2])
    o_ref[...] = _layer_norm_rows(ALPHA * x_ref[...] + ffn, g_ref[...], b_ref[...])


def _combine(x, ys, dest, gw, ln_g, ln_b, *, tm=512):
    n, d = x.shape
    tm = min(tm, n)
    nt = n // tm
    return pl.pallas_call(
        _combine_kernel,
        grid=(nt,),
        in_specs=[pl.BlockSpec((nt, 2 * tm), lambda i: (0, 0)),
                  pl.BlockSpec((tm, d), lambda i: (i, 0)), pl.BlockSpec((tm, LANE), lambda i: (i, 0)),
                  pl.BlockSpec((1, d), lambda i: (0, 0)), pl.BlockSpec((1, d), lambda i: (0, 0)),
                  pl.BlockSpec(memory_space=pl.ANY)],
        out_specs=pl.BlockSpec((tm, d), lambda i: (i, 0)),
        out_shape=jax.ShapeDtypeStruct((n, d), F32),
        scratch_shapes=[pltpu.SMEM((2 * tm,), jnp.int32), pltpu.VMEM((2, 2, tm * SUBL, LANE), F32),
                        pltpu.SemaphoreType.DMA(()), pltpu.SemaphoreType.DMA((2,))],
        compiler_params=_cparams(("arbitrary",)),
        name="combine",
    )(dest.reshape(nt, 2 * tm), x, gw, ln_g.reshape(1, d), ln_b.reshape(1, d), ys)


def _router_params(w_grp, b_grp, w_exp, b_exp):
    d = w_exp.shape[0]
    wr = jnp.zeros((d, LANE), F32).at[:, :N_EXPERTS].set(w_exp).at[:, N_EXPERTS:N_EXPERTS + N_GROUPS].set(w_grp)
    br = jnp.zeros((1, LANE), F32).at[0, :N_EXPERTS].set(b_exp).at[0, N_EXPERTS:N_EXPERTS + N_GROUPS].set(b_grp)
    return wr, br


def _moe(x, routing, w_gate, w_up, w_down, layer, ln_g, ln_b):
    n, d = x.shape
    ids, gw, cnt = routing
    bm = MOE_BM
    counts = cnt[0, :N_EXPERTS]
    padded = (counts + bm - 1) // bm * bm
    pend = jnp.cumsum(padded)
    pstart = pend - padded
    e, r = ids[:, 0:2], ids[:, 2:4]
    onehot = e[:, :, None] == jnp.arange(N_EXPERTS, dtype=jnp.int32)[None, None, :]
    dest = jnp.sum(jnp.where(onehot, pstart[None, None, :], 0), axis=-1) + r
    cap = 2 * n + N_EXPERTS * bm
    nb = cap // bm
    blk_start = jnp.arange(nb, dtype=jnp.int32) * bm
    blk_e = jnp.minimum(jnp.sum((pend[None, :] <= blk_start[:, None]).astype(jnp.int32), axis=1), N_EXPERTS - 1)
    n_used = (pend[-1:] // bm).astype(jnp.int32)
    xs = _dispatch(x, dest, pend.astype(jnp.int32), padded.astype(jnp.int32), cap)
    ys = _experts(xs, blk_e, n_used, w_gate, w_up, w_down, layer)
    return _combine(x, ys, dest, gw, ln_g, ln_b)


def _even_layer(x, batch, seq, layer_idx, w_in, dec_f, dec_b, lq1, lk1, lq2, lk2, subln, w_out, ln_g, ln_b, route):
    d = x.shape[1]
    w = HEADS * LANE
    kw = dict(batch=batch, seq=seq)
    diff_seg = [(0, DIFF_ROT_DIM), (DIFF_HEAD_DIM, DIFF_ROT_DIM)]
    q, k, v, gate, dq, dk, dvt = _even_in(
        x, w_in.astype(BF16), _rope_tables(seq, [(0, LANE)], RET_THETA),
        _rope_tables(seq, diff_seg, ROPE_THETA), **kw)
    decays = jnp.stack([dec_f, dec_b]).astype(F32)
    of, ob = _retention(q, k, v, decays, **kw)
    lam_init = 0.8 - 0.6 * math.exp(-0.3 * layer_idx)
    diff = _flash(dq, dk, dvt, diff=(lq1, lk1, lq2, lk2, subln), lam_init=lam_init, **kw)
    wo = w_out.astype(BF16)
    return _outproj(x, diff, of, ob, gate, 0, jnp.ones((LANE,), F32), wo[w:], wo[:w], ln_g, ln_b, *route,
                    group_norm=True)


def _odd_layer(x, batch, seq, w_in, q_norm, w_uq, kv_norm, w_ukv, w2_f, b_f, w2_b, b_b, gla_norm, w_out, ln_g, ln_b,
               route):
    d = x.shape[1]
    w = HEADS * LANE
    o = np.cumsum([0, MLA_Q_RANK, MLA_KV_RANK, MLA_ROPE, HEADS * GLA_K_DIM, HEADS * GLA_K_DIM, w, w,
                   GLA_GATE_RANK, GLA_GATE_RANK]).tolist()
    zeros = lambda c: jnp.zeros((d, c), F32)
    w_in2 = jnp.concatenate([
        w_in[:, o[0]:o[2]], zeros(MLA_NOPE), w_in[:, o[2]:o[3]], zeros(LANE - MLA_NOPE - MLA_ROPE),
        w_in[:, o[3]:o[7]], w_in[:, o[7]:o[9]], zeros(LANE - 2 * GLA_GATE_RANK)], axis=1).astype(BF16)
    kw = dict(batch=batch, seq=seq)
    qd = MLA_NOPE + MLA_ROPE
    w_uq2 = jnp.pad(w_uq.reshape(MLA_Q_RANK, HEADS, qd), ((0, 0), (0, 0), (0, LANE - qd))).reshape(MLA_Q_RANK, w)
    ukv = w_ukv.reshape(MLA_KV_RANK, HEADS, MLA_NOPE + MLA_V)
    w_uk2 = jnp.pad(ukv[:, :, :MLA_NOPE], ((0, 0), (0, 0), (0, LANE - MLA_NOPE))).reshape(MLA_KV_RANK, w)
    w_uv2 = ukv[:, :, MLA_NOPE:].reshape(MLA_KV_RANK, w)
    q, k, vt, zg = _odd_in(x, w_in2, q_norm, w_uq2.astype(BF16), kv_norm,
                           jnp.concatenate([w_uk2, w_uv2], axis=1).astype(BF16),
                           _rope_tables(seq, [(MLA_NOPE, MLA_ROPE)], ROPE_THETA), **kw)
    mla = _flash(q, k, vt, **kw)
    wk = HEADS * GLA_K_DIM
    pad_rows = lambda m, r0: jnp.zeros((LANE, wk), F32).at[r0:r0 + GLA_GATE_RANK].set(m).astype(BF16)
    of, ob = _gla(zg, qcol=0, kcol=1, vcol=1, lcol=12,
                  w2f=pad_rows(w2_f, 0), bf=b_f.reshape(1, wk), w2b=pad_rows(w2_b, GLA_GATE_RANK), bb=b_b.reshape(1, wk), **kw)
    wo = w_out.astype(BF16)
    return _outproj(x, mla, of, ob, zg, 2, gla_norm, wo[:w], wo[w:], ln_g, ln_b, *route, group_norm=False)


def kernel(x, ev_w_in, ev_ret_decay_f, ev_ret_decay_b, ev_lq1, ev_lk1, ev_lq2, ev_lk2, ev_subln, ev_w_out, od_w_in, od_q_norm, od_w_uq, od_kv_norm, od_w_ukv, od_gla_w2_f, od_gla_b_f, od_gla_w2_b, od_gla_b_b, od_gla_norm, od_w_out, ln1_g, ln1_b, ln2_g, ln2_b, moe_w_grp, moe_b_grp, moe_w_exp, moe_b_exp, moe_w_gate, moe_w_up, moe_w_down):
    batch, seq, d = x.shape
    h = x.reshape(batch * seq, d)
    for i in range(DEPTH):
        j = i // 2
        route = _router_params(moe_w_grp[i], moe_b_grp[i], moe_w_exp[i], moe_b_exp[i])
        if i % 2 == 0:
            h, *routing = _even_layer(h, batch, seq, i, ev_w_in[j], ev_ret_decay_f[j], ev_ret_decay_b[j], ev_lq1[j],
                                      ev_lk1[j], ev_lq2[j], ev_lk2[j], ev_subln[j], ev_w_out[j], ln1_g[i], ln1_b[i], route)
        else:
            h, *routing = _odd_layer(h, batch, seq, od_w_in[j], od_q_norm[j], od_w_uq[j], od_kv_norm[j], od_w_ukv[j],
                                     od_gla_w2_f[j], od_gla_b_f[j], od_gla_w2_b[j], od_gla_b_b[j], od_gla_norm[j],
                                     od_w_out[j], ln1_g[i], ln1_b[i], route)
        h = _moe(h, routing, moe_w_gate, moe_w_up, moe_w_down, i, ln2_g[i], ln2_b[i])
    return h.reshape(batch, seq, d)
```

```python
import functools
import math

import numpy as np
import jax
import jax.numpy as jnp
from jax import lax
from jax.experimental import pallas as pl
from jax.experimental.pallas import tpu as pltpu

F32 = jnp.float32
BF16 = jnp.bfloat16

HEADS = 4
LANE = 128
RET_THETA = 10000.0
ROPE_THETA = 500000.0
DIFF_HEAD_DIM = 64
DIFF_ROT_DIM = 16
MLA_Q_RANK = 256
MLA_KV_RANK = 128
MLA_NOPE = 64
MLA_ROPE = 32
MLA_V = 128
GLA_K_DIM = 64
GLA_GATE_RANK = 16
GLA_TAU = 16.0
N_GROUPS = 4
EXPERTS_PER_GROUP = 8
N_EXPERTS = N_GROUPS * EXPERTS_PER_GROUP
DEPTH = 2
ALPHA = (2.0 * DEPTH) ** 0.25
LN_EPS = 1e-5
RMS_EPS = 1e-6

VMEM_LIMIT = 48 * 1024 * 1024


def _div_pow2(x, n):
    return lax.shift_right_logical(x, int(n).bit_length() - 1)


def _mod_pow2(x, n):
    return lax.bitwise_and(x, int(n) - 1)


def _cparams(sem):
    return pltpu.CompilerParams(dimension_semantics=sem, vmem_limit_bytes=VMEM_LIMIT)


def _rope_heads(z, tabs, sh, scale):
    c, sa, sb = tabs
    outs = []
    for h in range(HEADS):
        zh = z[:, h * LANE:(h + 1) * LANE]
        outs.append((zh * c + pltpu.roll(zh, sh, axis=1) * sa + pltpu.roll(zh, LANE - sh, axis=1) * sb) * scale)
    return outs


def _even_in_kernel(x_ref, w_ref, rc, rsa, rsb, dc, dsa, dsb, q_ref, k_ref, v_ref, g_ref, dq_ref, dk_ref, dvt_ref):
    w = HEADS * LANE
    xb = x_ref[...].astype(BF16)
    part = lambda t: jnp.dot(xb, w_ref[:, t * w:(t + 1) * w], preferred_element_type=F32)
    ret_t = (rc[...], rsa[...], rsb[...])
    diff_t = (dc[...], dsa[...], dsb[...])
    for h, o in enumerate(_rope_heads(part(0), ret_t, LANE // 2, 1.0)):
        q_ref[h] = o.astype(BF16)
    for h, o in enumerate(_rope_heads(part(1), ret_t, LANE // 2, LANE ** -0.5)):
        k_ref[h] = o.astype(BF16)
    rv = part(2)
    for h in range(HEADS):
        v_ref[h] = rv[:, h * LANE:(h + 1) * LANE].astype(BF16)
    g_ref[...] = part(3)
    for h, o in enumerate(_rope_heads(part(4), diff_t, DIFF_ROT_DIM // 2, DIFF_HEAD_DIM ** -0.5 * LOG2E)):
        dq_ref[h] = o.T.astype(BF16)
    for h, o in enumerate(_rope_heads(part(5), diff_t, DIFF_ROT_DIM // 2, 1.0)):
        dk_ref[h] = o.astype(BF16)
    dv = part(6)
    for h in range(HEADS):
        dvt_ref[h, :LANE, :] = dv[:, h * LANE:(h + 1) * LANE].T.astype(BF16)
        dvt_ref[h, LANE:, :] = jnp.ones((ONES_ROWS, dvt_ref.shape[2]), BF16)


def _even_in(x, w, ret_tabs, diff_tabs, *, batch, seq):
    n, d = x.shape
    tm = min(FLASH_TK_DIFF, seq // 2)
    nt = seq // tm
    hw = HEADS * LANE
    heads = jax.ShapeDtypeStruct((batch, HEADS, seq, LANE), BF16)
    head_spec = pl.BlockSpec((None, HEADS, tm, LANE), lambda i: (i // nt, 0, i % nt, 0))
    heads_t = jax.ShapeDtypeStruct((batch, HEADS, LANE, seq), BF16)
    head_t_spec = pl.BlockSpec((None, HEADS, LANE, tm), lambda i: (i // nt, 0, 0, i % nt))
    tab_spec = pl.BlockSpec((tm, LANE), lambda i: (i % nt, 0))
    return pl.pallas_call(
        _even_in_kernel,
        grid=(n // tm,),
        in_specs=[pl.BlockSpec((tm, d), lambda i: (i, 0)), pl.BlockSpec((d, 7 * hw), lambda i: (0, 0))] + [tab_spec] * 6,
        out_specs=[head_spec, head_spec, head_spec, pl.BlockSpec((tm, hw), lambda i: (i, 0)), head_t_spec, head_spec,
                   pl.BlockSpec((None, HEADS, None, LANE + ONES_ROWS, tm), lambda i: (i // nt, 0, i % nt, 0, 0))],
        out_shape=[heads, heads, heads, jax.ShapeDtypeStruct((n, hw), F32), heads_t, heads,
                   jax.ShapeDtypeStruct((batch, HEADS, nt, LANE + ONES_ROWS, tm), BF16)],
        compiler_params=_cparams(("parallel",)),
        name="even_in",
    )(x, w, *ret_tabs, *diff_tabs)


def _rms_rows(z, g):
    return z * lax.rsqrt(jnp.mean(z * z, axis=-1, keepdims=True) + RMS_EPS) * g


def _odd_in_kernel(x_ref, w_ref, qn_ref, wq_ref, kvn_ref, wkv_ref, tc, tsa, tsb, q_ref, k_ref, vt_ref, zg_ref):
    hw = HEADS * LANE
    mla_w = MLA_Q_RANK + MLA_KV_RANK + LANE
    xb = x_ref[...].astype(BF16)
    zg_ref[...] = jnp.dot(xb, w_ref[:, mla_w:], preferred_element_type=F32)
    z1 = jnp.dot(xb, w_ref[:, :mla_w], preferred_element_type=F32)
    tabs = (tc[...], tsa[...], tsb[...])
    sh = MLA_ROPE // 2
    qh = jnp.dot(_rms_rows(z1[:, :MLA_Q_RANK], qn_ref[...]).astype(BF16), wq_ref[...], preferred_element_type=F32)
    for h, o in enumerate(_rope_heads(qh, tabs, sh, (MLA_NOPE + MLA_ROPE) ** -0.5 * LOG2E)):
        q_ref[h] = o.T.astype(BF16)
    ckv = _rms_rows(z1[:, MLA_Q_RANK:MLA_Q_RANK + MLA_KV_RANK], kvn_ref[...]).astype(BF16)
    kv = jnp.dot(ckv, wkv_ref[...], preferred_element_type=F32)
    kr = z1[:, MLA_Q_RANK + MLA_KV_RANK:]
    kr = kr * tabs[0] + pltpu.roll(kr, sh, axis=1) * tabs[1] + pltpu.roll(kr, LANE - sh, axis=1) * tabs[2]
    for h in range(HEADS):
        k_ref[h] = (kv[:, h * LANE:(h + 1) * LANE] + kr).astype(BF16)
        vt_ref[h, :LANE, :] = kv[:, hw + h * LANE:hw + (h + 1) * LANE].T.astype(BF16)
        vt_ref[h, LANE:, :] = jnp.ones((ONES_ROWS, vt_ref.shape[2]), BF16)


def _odd_in(x, w, q_norm, w_uq, kv_norm, w_ukv, tabs, *, batch, seq, tm=512):
    n, d = x.shape
    tm = min(tm, seq // 2)
    nt = seq // tm
    tk = min(FLASH_TK, seq // 2)
    per = tk // tm
    hw = HEADS * LANE
    gw_ = w.shape[1] - (MLA_Q_RANK + MLA_KV_RANK + LANE)
    heads = jax.ShapeDtypeStruct((batch, HEADS, seq, LANE), BF16)
    head_spec = pl.BlockSpec((None, HEADS, tm, LANE), lambda i: (i // nt, 0, i % nt, 0))
    tab_spec = pl.BlockSpec((tm, LANE), lambda i: (i % nt, 0))
    const = lambda i: (0, 0)
    return pl.pallas_call(
        _odd_in_kernel,
        grid=(n // tm,),
        in_specs=[pl.BlockSpec((tm, d), lambda i: (i, 0)), pl.BlockSpec(w.shape, const),
                  pl.BlockSpec((1, MLA_Q_RANK), const), pl.BlockSpec(w_uq.shape, const),
                  pl.BlockSpec((1, MLA_KV_RANK), const), pl.BlockSpec(w_ukv.shape, const)] + [tab_spec] * 3,
        out_specs=[pl.BlockSpec((None, HEADS, LANE, tm), lambda i: (i // nt, 0, 0, i % nt)), head_spec,
                   pl.BlockSpec((None, HEADS, None, LANE + ONES_ROWS, tm),
                                lambda i: (i // nt, 0, (i % nt) // per, 0, (i % nt) % per)),
                   pl.BlockSpec((tm, gw_), lambda i: (i, 0))],
        out_shape=[jax.ShapeDtypeStruct((batch, HEADS, LANE, seq), BF16), heads, jax.ShapeDtypeStruct((batch, HEADS, seq // tk, LANE + ONES_ROWS, tk), BF16),
                   jax.ShapeDtypeStruct((n, gw_), F32)],
        compiler_params=_cparams(("parallel",)),
        name="odd_in",
    )(x, w, q_norm.reshape(1, -1), w_uq, kv_norm.reshape(1, -1), w_ukv, *tabs)


def _rope_tables(seq, segs, theta):
    pos = jnp.arange(seq, dtype=F32)
    inv = jnp.zeros((LANE,), F32)
    lo = np.zeros((LANE,), bool)
    hi = np.zeros((LANE,), bool)
    for start, rot in segs:
        half = rot // 2
        f = jnp.power(jnp.float32(theta), -jnp.arange(0, rot, 2, dtype=F32) / rot)
        inv = inv.at[start:start + half].set(f).at[start + half:start + rot].set(f)
        lo[start:start + half] = True
        hi[start + half:start + rot] = True
    ang = pos[:, None] * inv[None, :]
    cos, sin = jnp.cos(ang), jnp.sin(ang)
    c = jnp.where(lo | hi, cos, 1.0)
    sa = jnp.where(hi, sin, 0.0)
    sb = jnp.where(lo, -sin, 0.0)
    return c, sa, sb


ONES_ROWS = 16
LOG2E = math.log2(math.e)
FLASH_TK = 1024
FLASH_TK_DIFF = 512


def _flash_kernel(*refs, ncomp, nk, lam_init):
    if ncomp == 2:
        q_ref, k_ref, vt_ref, lq1, lk1, lq2, lk2, g_ref, o_ref, *scr = refs
    else:
        q_ref, k_ref, vt_ref, o_ref, *scr = refs
    qm_sc, m_sc, acc_sc, s0, s1, cm0, cm1, p0, p1, al0, al1 = scr
    tk = s0.shape[1]
    q = q_ref[...]
    if ncomp == 2:
        chan = lax.broadcasted_iota(jnp.int32, q.shape, 0)
        zero = jnp.zeros_like(q)
        qm_sc[0] = jnp.where(chan < DIFF_HEAD_DIM, q, zero)
        qm_sc[1] = jnp.where(chan >= DIFF_HEAD_DIM, q, zero)
    else:
        qm_sc[0] = q
    m_sc[...] = jnp.full(m_sc.shape, -jnp.inf, F32)
    acc_sc[...] = jnp.zeros(acc_sc.shape, F32)

    def scores(j, s_ref, cm_ref):
        k = k_ref[j * tk:(j + 1) * tk, :]
        for c in range(ncomp):
            s = jnp.dot(k, qm_sc[c], preferred_element_type=F32)
            s_ref[c] = s
            cm_ref[c] = jnp.max(s, axis=0, keepdims=True)

    def softmax(s_ref, cm_ref, p_ref, al_ref):
        for c in range(ncomp):
            m_old = m_sc[c]
            m_new = jnp.maximum(m_old, cm_ref[c])
            al_ref[c] = jnp.exp2(m_old - m_new)
            p_ref[c] = jnp.exp2(s_ref[c] - m_new).astype(BF16)
            m_sc[c] = m_new

    def values(j, p_ref, al_ref):
        vt = vt_ref[j]
        for c in range(ncomp):
            acc_sc[c] = al_ref[c] * acc_sc[c] + jnp.dot(vt, p_ref[c], preferred_element_type=F32)

    bufs = ((s0, cm0, p0, al0), (s1, cm1, p1, al1))
    scores(0, s0, cm0)
    for j in range(nk):
        s_c, cm_c, p_c, al_c = bufs[j % 2]
        s_n, cm_n, p_n, al_n = bufs[(j + 1) % 2]
        if j + 1 < nk:
            scores(j + 1, s_n, cm_n)
        softmax(s_c, cm_c, p_c, al_c)
        if j >= 1:
            values(j - 1, p_n, al_n)
    values(nk - 1, *bufs[(nk - 1) % 2][2:])

    def normalised(c):
        acc = acc_sc[c]
        return acc[:LANE] / acc[LANE:LANE + 1]

    o = normalised(0)
    if ncomp == 2:
        lam = (jnp.exp(jnp.sum(lq1[...] * lk1[...], keepdims=True))
               - jnp.exp(jnp.sum(lq2[...] * lk2[...], keepdims=True)) + lam_init)
        o = o - lam * normalised(1)
        o = o * lax.rsqrt(jnp.mean(o * o, axis=0, keepdims=True) + RMS_EPS) * g_ref[...] * (1.0 - lam_init)
    o_ref[...] = o.T


def _flash(q, k, vt, *, batch, seq, tq=512, diff=None, lam_init=0.0):
    nk, vrows, tk = vt.shape[2], vt.shape[3], vt.shape[4]
    assert vrows == LANE + ONES_ROWS
    tq = min(tq, seq)
    nq = seq // tq
    ncomp = 2 if diff is not None else 1
    in_specs = [
        pl.BlockSpec((None, None, LANE, tq), lambda b, h, i: (b, h, 0, i)),
        pl.BlockSpec((None, None, seq, LANE), lambda b, h, i: (b, h, 0, 0)),
        pl.BlockSpec((None, None, nk, vrows, tk), lambda b, h, i: (b, h, 0, 0, 0)),
    ]
    args = [q, k, vt]
    if diff is not None:
        lq1, lk1, lq2, lk2, subln = diff
        for v in (lq1, lk1, lq2, lk2):
            in_specs.append(pl.BlockSpec((1, DIFF_HEAD_DIM), lambda b, h, i: (0, 0)))
            args.append(v.reshape(1, DIFF_HEAD_DIM))
        in_specs.append(pl.BlockSpec((LANE, 1), lambda b, h, i: (0, 0)))
        args.append(subln.reshape(LANE, 1))
    return pl.pallas_call(
        functools.partial(_flash_kernel, ncomp=ncomp, nk=nk, lam_init=lam_init),
        grid=(batch, HEADS, nq),
        in_specs=in_specs,
        out_specs=pl.BlockSpec((tq, LANE), lambda b, h, i: (b * nq + i, h)),
        out_shape=jax.ShapeDtypeStruct((batch * seq, HEADS * LANE), F32),
        scratch_shapes=[pltpu.VMEM((ncomp, LANE, tq), BF16),
                        pltpu.VMEM((ncomp, 1, tq), F32), pltpu.VMEM((ncomp, vrows, tq), F32),
                        pltpu.VMEM((ncomp, tk, tq), F32), pltpu.VMEM((ncomp, tk, tq), F32),
                        pltpu.VMEM((ncomp, 1, tq), F32), pltpu.VMEM((ncomp, 1, tq), F32),
                        pltpu.VMEM((ncomp, tk, tq), BF16), pltpu.VMEM((ncomp, tk, tq), BF16),
                        pltpu.VMEM((ncomp, 1, tq), F32), pltpu.VMEM((ncomp, 1, tq), F32)],
        compiler_params=_cparams(("parallel", "parallel", "parallel")),
        name="flash_diff" if diff is not None else "flash_mla",
    )(*args)


def _ret_kernel(dec_ref, qf, kf, vf, qb, kb, vb, of_ref, ob_ref, s_sc, *, chunk):
    @pl.when(pl.program_id(1) == 0)
    def _():
        s_sc[...] = jnp.zeros(s_sc.shape, F32)

    ii = lax.broadcasted_iota(jnp.int32, (chunk, chunk), 0)
    jj = lax.broadcasted_iota(jnp.int32, (chunk, chunk), 1)
    r = lax.broadcasted_iota(jnp.int32, (chunk, 1), 0).astype(F32)
    for d, (q_ref, k_ref, v_ref, o_ref) in enumerate(((qf, kf, vf, of_ref), (qb, kb, vb, ob_ref))):
        for h in range(HEADS):
            la = -jnp.exp(jnp.full((1, 1), dec_ref[d, h], F32))
            if d == 0:
                mask, dist = ii >= jj, (ii - jj).astype(F32)
                qdec, kdec = jnp.exp(la * (r + 1.0)), jnp.exp(la * (chunk - 1.0 - r))
            else:
                mask, dist = jj > ii, (jj - ii).astype(F32)
                qdec, kdec = jnp.exp(la * (chunk - r)), jnp.exp(la * r)
            decay = jnp.where(mask, jnp.exp(jnp.where(mask, dist * la, 0.0)), 0.0)
            q, k, v = q_ref[h], k_ref[h], v_ref[h]
            s = lax.dot_general(q, k, (((1,), (1,)), ((), ())), preferred_element_type=F32)
            o = jnp.dot((s * decay).astype(BF16), v, preferred_element_type=F32)
            state = s_sc[d, h]
            o = o + qdec * jnp.dot(q, state.astype(BF16), preferred_element_type=F32)
            kd = (k.astype(F32) * kdec).astype(BF16)
            s_sc[d, h] = jnp.exp(la * float(chunk)) * state + lax.dot_general(
                kd, v, (((0,), (0,)), ((), ())), preferred_element_type=F32)
            o_ref[:, h * LANE:(h + 1) * LANE] = o


def _retention(q, k, v, decays, *, batch, seq, chunk=256):
    chunk = min(chunk, seq)
    n = seq // chunk
    fwd = pl.BlockSpec((None, HEADS, chunk, LANE), lambda b, c: (b, 0, c, 0))
    bwd = pl.BlockSpec((None, HEADS, chunk, LANE), lambda b, c: (b, 0, n - 1 - c, 0))
    w = HEADS * LANE
    out = jax.ShapeDtypeStruct((batch * seq, w), F32)
    return pl.pallas_call(
        functools.partial(_ret_kernel, chunk=chunk),
        grid=(batch, n),
        in_specs=[pl.BlockSpec(memory_space=pltpu.SMEM), fwd, fwd, fwd, bwd, bwd, bwd],
        out_specs=[pl.BlockSpec((chunk, w), lambda b, c: (b * n + c, 0)),
                   pl.BlockSpec((chunk, w), lambda b, c: (b * n + n - 1 - c, 0))],
        out_shape=[out, out],
        scratch_shapes=[pltpu.VMEM((2, HEADS, LANE, LANE), F32)],
        compiler_params=_cparams(("parallel", "arbitrary")),
        name="retention",
    )(decays, q, k, v, q, k, v)


GLA_SUB = 8


def _split3(x):
    x1 = x.astype(BF16)
    r1 = x - x1.astype(F32)
    x2 = r1.astype(BF16)
    x3 = (r1 - x2.astype(F32)).astype(BF16)
    return x1, x2, x3


def _gla_direction(q, k, v, lr, w2, bias, st, reverse):
    C, wk = q.shape
    wv = v.shape[1]
    dk, dv = wk // HEADS, wv // HEADS
    z = jnp.dot(lr.astype(BF16), w2, preferred_element_type=F32) + bias
    g = (jnp.minimum(z, 0.0) - jnp.log(1.0 + jnp.exp(-jnp.abs(z)))) * (LOG2E / GLA_TAU)
    ii = lax.broadcasted_iota(jnp.int32, (C, C), 0)
    jj = lax.broadcasted_iota(jnp.int32, (C, C), 1)
    tri = jnp.where(ii >= jj, 1.0, 0.0).astype(BF16)
    b = sum(jnp.dot(tri, part, preferred_element_type=F32) for part in _split3(g))
    tot = b[C - 1:C, :]
    c = (tot - b + g) if reverse else b

    qe = (q * jnp.exp2(jnp.minimum(c, 0.0))).astype(BF16)
    o = lax.dot_general(qe, st.astype(BF16), (((1,), (1,)), ((), ())), preferred_element_type=F32)
    ke = (k * jnp.exp2(jnp.minimum(tot - c, 0.0))).astype(BF16)
    upd = lax.dot_general(v.astype(BF16), ke, (((0,), (0,)), ((), ())), preferred_element_type=F32)
    rr = _div_pow2(lax.broadcasted_iota(jnp.int32, (wv, wk), 0), dv)
    cc = _div_pow2(lax.broadcasted_iota(jnp.int32, (wv, wk), 1), dk)
    new_st = jnp.where(rr == cc, st * jnp.exp2(tot) + upd, 0.0)

    lane_head = _div_pow2(lax.broadcasted_iota(jnp.int32, (C, wk), 1), dk)
    scores = [jnp.zeros((C, C), F32) for _ in range(HEADS)]
    hsz = C // 2
    while hsz >= GLA_SUB:
        blk = 2 * hsz
        rows = []
        for m in range(C // blk):
            rrow = m * blk + (hsz if reverse else hsz - 1)
            rows.append(jnp.broadcast_to(c[rrow:rrow + 1, :], (blk, wk)))
        ref = jnp.concatenate(rows, axis=0) if len(rows) > 1 else rows[0]
        qt = q * jnp.exp2(jnp.minimum(c - ref, 0.0))
        kt = (k * jnp.exp2(jnp.minimum(ref - c, 0.0))).astype(BF16)
        same = _div_pow2(ii, blk) == _div_pow2(jj, blk)
        if reverse:
            lvl = same & (_mod_pow2(ii, blk) < hsz) & (_mod_pow2(jj, blk) >= hsz)
        else:
            lvl = same & (_mod_pow2(ii, blk) >= hsz) & (_mod_pow2(jj, blk) < hsz)
        for h in range(HEADS):
            qh = jnp.where(lane_head == h, qt, 0.0).astype(BF16)
            s = lax.dot_general(qh, kt, (((1,), (1,)), ((), ())), preferred_element_type=F32)
            scores[h] = scores[h] + jnp.where(lvl, s, 0.0)
        hsz //= 2

    assert dv == C
    er = _div_pow2(lax.broadcasted_iota(jnp.int32, (wk, wv), 0), dk)
    ec = _div_pow2(lax.broadcasted_iota(jnp.int32, (wk, wv), 1), dv)
    expand = jnp.where(er == ec, 1.0, 0.0).astype(BF16)
    dist = (jj - ii) if reverse else (ii - jj)
    same_sub = _div_pow2(ii, GLA_SUB) == _div_pow2(jj, GLA_SUB)
    for lag in range(1 if reverse else 0, GLA_SUB):
        if lag == 0:
            t = q * k
        else:
            shift = (GLA_SUB - lag) if reverse else lag
            ks = pltpu.roll(k.reshape(C // GLA_SUB, GLA_SUB, wk), shift, axis=1).reshape(C, wk)
            cs = pltpu.roll(c.reshape(C // GLA_SUB, GLA_SUB, wk), shift, axis=1).reshape(C, wk)
            t = q * ks * jnp.exp2(jnp.minimum(c - cs, 0.0))
        red = jnp.dot(t.astype(BF16), expand, preferred_element_type=F32)
        on_diag = same_sub & (dist == lag)
        for h in range(HEADS):
            scores[h] = scores[h] + jnp.where(on_diag, red[:, h * dv:(h + 1) * dv], 0.0)

    vb = v.astype(BF16)
    o = o + jnp.concatenate(
        [jnp.dot(scores[h].astype(BF16), vb[:, h * dv:(h + 1) * dv], preferred_element_type=F32) for h in range(HEADS)],
        axis=1)
    return o, new_st


def _gla_kernel(qf, kf, vf, lf, qb, kb, vb, lb, w2f, bf, w2b, bb, of_ref, ob_ref, s_sc, *, qscale):
    @pl.when(pl.program_id(1) == 0)
    def _():
        s_sc[...] = jnp.zeros(s_sc.shape, F32)

    o, st = _gla_direction(qf[...] * qscale, kf[...], vf[...], lf[...], w2f[...], bf[...], s_sc[0], False)
    of_ref[...] = o
    s_sc[0] = st
    o, st = _gla_direction(qb[...] * qscale, kb[...], vb[...], lb[...], w2b[...], bb[...], s_sc[1], True)
    ob_ref[...] = o
    s_sc[1] = st


def _gla(z, *, qcol, kcol, vcol, lcol, w2f, bf, w2b, bb, batch, seq, chunk=128):
    chunk = min(chunk, seq)
    n = seq // chunk
    wk, wv = HEADS * GLA_K_DIM, HEADS * LANE

    def specs(cmap):
        return [pl.BlockSpec((chunk, wk), lambda b, c: (cmap(b, c), qcol)),
                pl.BlockSpec((chunk, wk), lambda b, c: (cmap(b, c), kcol)),
                pl.BlockSpec((chunk, wv), lambda b, c: (cmap(b, c), vcol)),
                pl.BlockSpec((chunk, LANE), lambda b, c: (cmap(b, c), lcol))]

    fmap = lambda b, c: b * n + c
    bmap = lambda b, c: b * n + n - 1 - c
    wspec = [pl.BlockSpec((LANE, wk), lambda b, c: (0, 0)), pl.BlockSpec((1, wk), lambda b, c: (0, 0))]
    out = jax.ShapeDtypeStruct((batch * seq, wv), F32)
    return pl.pallas_call(
        functools.partial(_gla_kernel, qscale=GLA_K_DIM ** -0.5),
        grid=(batch, n),
        in_specs=specs(fmap) + specs(bmap) + wspec + wspec,
        out_specs=[pl.BlockSpec((chunk, wv), lambda b, c: (fmap(b, c), 0)),
                   pl.BlockSpec((chunk, wv), lambda b, c: (bmap(b, c), 0))],
        out_shape=[out, out],
        scratch_shapes=[pltpu.VMEM((2, wv, wk), F32)],
        compiler_params=_cparams(("parallel", "arbitrary")),
        name="gla",
    )(z, z, z, z, z, z, z, z, w2f, bf, w2b, bb)


def _layer_norm_rows(r, g, b):
    mu = jnp.mean(r, axis=-1, keepdims=True)
    d = r - mu
    var = jnp.mean(d * d, axis=-1, keepdims=True)
    return d * lax.rsqrt(var + LN_EPS) * g + b


def _outproj_kernel(x_ref, fin_ref, of_ref, ob_ref, gate_ref, ng_ref, wa_ref, wb_ref, lg_ref, lb_ref, rw_ref, rb_ref,
                    o_ref, ids_ref, gw_ref, cnt_ref, run_sc, *, group_norm):
    lin = of_ref[...] + ob_ref[...]
    parts = []
    for h in range(HEADS):
        zh = lin[:, h * LANE:(h + 1) * LANE]
        if group_norm:
            mu = jnp.mean(zh, axis=-1, keepdims=True)
            dz = zh - mu
            parts.append(dz * lax.rsqrt(jnp.mean(dz * dz, axis=-1, keepdims=True) + LN_EPS))
        else:
            parts.append(zh * lax.rsqrt(jnp.mean(zh * zh, axis=-1, keepdims=True) + RMS_EPS) * ng_ref[...])
    gate = gate_ref[...]
    lin = jnp.concatenate(parts, axis=1) * (gate * jax.nn.sigmoid(gate))
    y = (jnp.dot(fin_ref[...].astype(BF16), wa_ref[...], preferred_element_type=F32)
         + jnp.dot(lin.astype(BF16), wb_ref[...], preferred_element_type=F32))
    x1 = _layer_norm_rows(ALPHA * x_ref[...] + y, lg_ref[...], lb_ref[...])
    o_ref[...] = x1
    _route_rows(x1, rw_ref, rb_ref, ids_ref, gw_ref, cnt_ref, run_sc)


def _outproj(x, fin, of, ob, gate_src, gate_col, norm_gain, wa, wb, ln_g, ln_b, route_w, route_b, *, group_norm, tm=512):
    n, d = x.shape
    w = HEADS * LANE
    tm = min(tm, n)
    row = lambda i: (i, 0)
    const = lambda i: (0, 0)
    rw_hi = route_w.astype(BF16)
    rw = jnp.stack([rw_hi, (route_w - rw_hi.astype(F32)).astype(BF16)])
    return pl.pallas_call(
        functools.partial(_outproj_kernel, group_norm=group_norm),
        grid=(n // tm,),
        in_specs=[pl.BlockSpec((tm, d), row), pl.BlockSpec((tm, w), row), pl.BlockSpec((tm, w), row),
                  pl.BlockSpec((tm, w), row), pl.BlockSpec((tm, w), lambda i: (i, gate_col)),
                  pl.BlockSpec((1, LANE), const), pl.BlockSpec((w, d), const), pl.BlockSpec((w, d), const),
                  pl.BlockSpec((1, d), const), pl.BlockSpec((1, d), const),
                  pl.BlockSpec((2, d, LANE), lambda i: (0, 0, 0)), pl.BlockSpec((1, LANE), const)],
        out_specs=[pl.BlockSpec((tm, d), row), pl.BlockSpec((tm, LANE), row), pl.BlockSpec((tm, LANE), row),
                   pl.BlockSpec((1, LANE), const)],
        out_shape=[jax.ShapeDtypeStruct((n, d), F32), jax.ShapeDtypeStruct((n, LANE), jnp.int32),
                   jax.ShapeDtypeStruct((n, LANE), F32), jax.ShapeDtypeStruct((1, LANE), jnp.int32)],
        scratch_shapes=[pltpu.VMEM((1, LANE), F32)],
        compiler_params=_cparams(("arbitrary",)),
        name="outproj",
    )(x, fin, of, ob, gate_src, norm_gain.reshape(1, LANE), wa, wb, ln_g.reshape(1, d), ln_b.reshape(1, d), rw, route_b)


def _route_rows(x, w_ref, b_ref, ids_ref, gw_ref, cnt_ref, run_sc):
    @pl.when(pl.program_id(0) == 0)
    def _():
        run_sc[...] = jnp.zeros(run_sc.shape, F32)

    tm = x.shape[0]
    xh = x.astype(BF16)
    xl = (x - xh.astype(F32)).astype(BF16)
    wh, wl = w_ref[0], w_ref[1]
    logits = (jnp.dot(xh, wh, preferred_element_type=F32) + jnp.dot(xh, wl, preferred_element_type=F32)
              + jnp.dot(xl, wh, preferred_element_type=F32)) + b_ref[...]
    lane = lax.broadcasted_iota(jnp.int32, logits.shape, 1)
    neg = -jnp.inf
    gmask = (lane >= N_EXPERTS) & (lane < N_EXPERTS + N_GROUPS)
    gl = jnp.where(gmask, logits, neg)
    gmax = jnp.max(gl, axis=1, keepdims=True)
    lane_f = lane.astype(F32)
    first = lambda hit: jnp.min(jnp.where(hit, lane_f, float(LANE)), axis=1, keepdims=True).astype(jnp.int32)
    gidx = first(gl == gmax) - N_EXPERTS
    p_grp = 1.0 / jnp.sum(jnp.where(gmask, jnp.exp(gl - gmax), 0.0), axis=1, keepdims=True)
    el = jnp.where(_div_pow2(lane, EXPERTS_PER_GROUP) == gidx, logits, neg)
    l1 = jnp.max(el, axis=1, keepdims=True)
    e1 = first(el == l1)
    el2 = jnp.where(lane == e1, neg, el)
    l2 = jnp.max(el2, axis=1, keepdims=True)
    e2 = first(el2 == l2)
    t = jnp.exp(l2 - l1)
    w1 = p_grp / (1.0 + t)
    w2 = p_grp * t / (1.0 + t)

    onehot = jnp.where(lane == e1, 1.0, jnp.where(lane == e2, 1.0, 0.0))
    ri = lax.broadcasted_iota(jnp.int32, (tm, tm), 0)
    ci = lax.broadcasted_iota(jnp.int32, (tm, tm), 1)
    before = jnp.dot(jnp.where(ri > ci, 1.0, 0.0).astype(BF16), onehot.astype(BF16), preferred_element_type=F32)
    before = before + run_sc[...]
    r1 = jnp.sum(jnp.where(lane == e1, before, 0.0), axis=1, keepdims=True).astype(jnp.int32)
    r2 = jnp.sum(jnp.where(lane == e2, before, 0.0), axis=1, keepdims=True).astype(jnp.int32)
    run_sc[...] = run_sc[...] + jnp.sum(onehot, axis=0, keepdims=True)
    cnt_ref[...] = run_sc[...].astype(jnp.int32)
    ids_ref[...] = jnp.where(lane == 0, e1, jnp.where(lane == 1, e2, jnp.where(lane == 2, r1, jnp.where(lane == 3, r2, 0))))
    gw_ref[...] = jnp.where(lane == 0, w1, jnp.where(lane == 1, w2, 0.0))


MOE_BM = 256
SUBL = 8


def _rows_from_linear(ref, rows):
    return jnp.concatenate([ref[pl.ds(s, rows, stride=SUBL), :] for s in range(SUBL)], axis=1)


def _rows_to_linear(ref, val):
    for s in range(SUBL):
        ref[pl.ds(s, val.shape[0], stride=SUBL), :] = val[:, s * LANE:(s + 1) * LANE]


def _dispatch_kernel(pend_ref, padded_ref, dest_ref, x_ref, xs_hbm, idx_smem, zbuf, lin, sem_i, sem_z, sem):
    i = pl.program_id(0)
    tm = x_ref.shape[0]
    bm = zbuf.shape[0] // SUBL

    @pl.when(i == 0)
    def _():
        zbuf[...] = jnp.zeros(zbuf.shape, F32)

        def tail(e):
            start_row = pl.multiple_of((pend_ref[e] - bm) * SUBL, bm * SUBL)
            return pltpu.make_async_copy(zbuf, xs_hbm.at[pl.ds(start_row, bm * SUBL), :], sem_z)

        def start(e, carry):
            @pl.when(padded_ref[e] > 0)
            def _():
                tail(e).start()
            return carry

        def wait(e, carry):
            @pl.when(padded_ref[e] > 0)
            def _():
                tail(e).wait()
            return carry

        lax.fori_loop(0, N_EXPERTS, start, 0)
        lax.fori_loop(0, N_EXPERTS, wait, 0)

        def unused(b):
            start_row = pl.multiple_of(b * bm * SUBL, bm * SUBL)
            return pltpu.make_async_copy(zbuf, xs_hbm.at[pl.ds(start_row, bm * SUBL), :], sem_z)

        first_unused = pend_ref[N_EXPERTS - 1] // bm
        n_blocks = xs_hbm.shape[0] // (bm * SUBL)
        lax.fori_loop(first_unused, n_blocks, lambda b, c: (unused(b).start(), c)[1], 0)
        lax.fori_loop(first_unused, n_blocks, lambda b, c: (unused(b).wait(), c)[1], 0)

    cp = pltpu.make_async_copy(dest_ref.at[i], idx_smem, sem_i)
    cp.start()
    _rows_to_linear(lin, x_ref[...])
    cp.wait()

    def scatter(r, carry):
        src = lin.at[pl.ds(pl.multiple_of(r * SUBL, SUBL), SUBL), :]
        for k in range(2):
            dst = pl.multiple_of(idx_smem[2 * r + k] * SUBL, SUBL)
            pltpu.make_async_copy(src, xs_hbm.at[pl.ds(dst, SUBL), :], sem).start(priority=k)
        return carry

    lax.fori_loop(0, tm, scatter, 0, unroll=8)
    for k in range(2):
        pltpu.make_async_copy(lin, xs_hbm.at[pl.ds(0, tm * SUBL), :], sem).wait()


def _dispatch(x, dest, pend, padded, cap, *, tm=512):
    n, d = x.shape
    assert d == SUBL * LANE
    tm = min(tm, n)
    nt = n // tm
    grid_spec = pltpu.PrefetchScalarGridSpec(
        num_scalar_prefetch=2,
        grid=(nt,),
        in_specs=[pl.BlockSpec((nt, 2 * tm), lambda i, pe, pa: (0, 0)),
                  pl.BlockSpec((tm, d), lambda i, pe, pa: (i, 0))],
        out_specs=pl.BlockSpec(memory_space=pl.ANY),
        scratch_shapes=[pltpu.SMEM((2 * tm,), jnp.int32), pltpu.VMEM((MOE_BM * SUBL, LANE), F32),
                        pltpu.VMEM((tm * SUBL, LANE), F32),
                        pltpu.SemaphoreType.DMA(()), pltpu.SemaphoreType.DMA(()), pltpu.SemaphoreType.DMA(())],
    )
    return pl.pallas_call(
        _dispatch_kernel,
        grid_spec=grid_spec,
        out_shape=jax.ShapeDtypeStruct((cap * SUBL, LANE), F32),
        compiler_params=_cparams(("arbitrary",)),
        name="dispatch",
    )(pend, padded, dest.reshape(nt, 2 * tm), x)


def _experts_kernel(blk_e_ref, nused_ref, xs_ref, wg_ref, wu_ref, wd_ref, ys_ref, wgb, wub, wdb):
    i = pl.program_id(0)

    @pl.when(i < nused_ref[0])
    def _():
        prev = blk_e_ref[jnp.maximum(i - 1, 0)]

        @pl.when(jnp.logical_or(i == 0, blk_e_ref[i] != prev))
        def _():
            wgb[...] = wg_ref[...].astype(BF16)
            wub[...] = wu_ref[...].astype(BF16)
            wdb[...] = wd_ref[...].astype(BF16)

        xb = _rows_from_linear(xs_ref, xs_ref.shape[0] // SUBL).astype(BF16)
        hg = jnp.dot(xb, wgb[...], preferred_element_type=F32)
        hu = jnp.dot(xb, wub[...], preferred_element_type=F32)
        hid = (hg * jax.nn.sigmoid(hg) * hu).astype(BF16)
        _rows_to_linear(ys_ref, jnp.dot(hid, wdb[...], preferred_element_type=F32))

    @pl.when(i >= nused_ref[0])
    def _():
        ys_ref[...] = jnp.zeros(ys_ref.shape, F32)


def _experts(xs, blk_e, n_used, w_gate, w_up, w_down, layer):
    cap = xs.shape[0] // SUBL
    bm = MOE_BM
    d, de = w_gate.shape[2], w_gate.shape[3]
    row_in = lambda i, be, nu: (jnp.minimum(i, nu[0] - 1), 0)
    row = lambda i, be, nu: (i, 0)
    grid_spec = pltpu.PrefetchScalarGridSpec(
        num_scalar_prefetch=2,
        grid=(cap // bm,),
        in_specs=[pl.BlockSpec((bm * SUBL, LANE), row_in),
                  pl.BlockSpec((None, None, d, de), lambda i, be, nu: (layer, be[i], 0, 0)),
                  pl.BlockSpec((None, None, d, de), lambda i, be, nu: (layer, be[i], 0, 0)),
                  pl.BlockSpec((None, None, de, d), lambda i, be, nu: (layer, be[i], 0, 0))],
        out_specs=pl.BlockSpec((bm * SUBL, LANE), row),
        scratch_shapes=[pltpu.VMEM((d, de), BF16), pltpu.VMEM((d, de), BF16), pltpu.VMEM((de, d), BF16)],
    )
    return pl.pallas_call(
        _experts_kernel,
        grid_spec=grid_spec,
        out_shape=jax.ShapeDtypeStruct((cap * SUBL, LANE), F32),
        compiler_params=_cparams(("arbitrary",)),
        name="experts",
    )(blk_e, n_used, xs, w_gate, w_up, w_down)


def _combine_kernel(dest_ref, x_ref, gw_ref, g_ref, b_ref, ys_hbm, o_ref, idx_smem, ybuf, sem_i, sem):
    i = pl.program_id(0)
    tm = x_ref.shape[0]

    def issue(tile, slot):
        cp = pltpu.make_async_copy(dest_ref.at[tile], idx_smem, sem_i)
        cp.start()
        cp.wait()

        def gather(r, carry):
            row = pl.multiple_of(r * SUBL, SUBL)
            for k in range(2):
                src = pl.multiple_of(idx_smem[2 * r + k] * SUBL, SUBL)
                pltpu.make_async_copy(ys_hbm.at[pl.ds(src, SUBL), :], ybuf.at[slot, k, pl.ds(row, SUBL), :],
                                      sem.at[slot]).start(priority=k)
            return carry

        lax.fori_loop(0, tm, gather, 0, unroll=8)

    slot = lax.rem(i, 2)

    @pl.when(i == 0)
    def _():
        issue(0, 0)

    @pl.when(i + 1 < pl.num_programs(0))
    def _():
        issue(i + 1, 1 - slot)

    for k in range(2):
        pltpu.make_async_copy(ys_hbm.at[pl.ds(0, tm * SUBL), :], ybuf.at[slot, k], sem.at[slot]).wait()
    gw = gw_ref[...]
    ffn = (_rows_from_linear(ybuf.at[slot, 0], tm) * gw[:, 0:1] + _rows_from_linear(ybuf.at[slot, 1], tm) * gw[:, 1:2])
    o_ref[...] = _layer_norm_rows(ALPHA * x_ref[...] + ffn, g_ref[...], b_ref[...])


def _combine(x, ys, dest, gw, ln_g, ln_b, *, tm=512):
    n, d = x.shape
    tm = min(tm, n)
    nt = n // tm
    return pl.pallas_call(
        _combine_kernel,
        grid=(nt,),
        in_specs=[pl.BlockSpec((nt, 2 * tm), lambda i: (0, 0)),
                  pl.BlockSpec((tm, d), lambda i: (i, 0)), pl.BlockSpec((tm, LANE), lambda i: (i, 0)),
                  pl.BlockSpec((1, d), lambda i: (0, 0)), pl.BlockSpec((1, d), lambda i: (0, 0)),
                  pl.BlockSpec(memory_space=pl.ANY)],
        out_specs=pl.BlockSpec((tm, d), lambda i: (i, 0)),
        out_shape=jax.ShapeDtypeStruct((n, d), F32),
        scratch_shapes=[pltpu.SMEM((2 * tm,), jnp.int32), pltpu.VMEM((2, 2, tm * SUBL, LANE), F32),
                        pltpu.SemaphoreType.DMA(()), pltpu.SemaphoreType.DMA((2,))],
        compiler_params=_cparams(("arbitrary",)),
        name="combine",
    )(dest.reshape(nt, 2 * tm), x, gw, ln_g.reshape(1, d), ln_b.reshape(1, d), ys)


def _router_params(w_grp, b_grp, w_exp, b_exp):
    d = w_exp.shape[0]
    wr = jnp.zeros((d, LANE), F32).at[:, :N_EXPERTS].set(w_exp).at[:, N_EXPERTS:N_EXPERTS + N_GROUPS].set(w_grp)
    br = jnp.zeros((1, LANE), F32).at[0, :N_EXPERTS].set(b_exp).at[0, N_EXPERTS:N_EXPERTS + N_GROUPS].set(b_grp)
    return wr, br


def _moe(x, routing, w_gate, w_up, w_down, layer, ln_g, ln_b):
    n, d = x.shape
    ids, gw, cnt = routing
    bm = MOE_BM
    counts = cnt[0, :N_EXPERTS]
    padded = (counts + bm - 1) // bm * bm
    pend = jnp.cumsum(padded)
    pstart = pend - padded
    e, r = ids[:, 0:2], ids[:, 2:4]
    onehot = e[:, :, None] == jnp.arange(N_EXPERTS, dtype=jnp.int32)[None, None, :]
    dest = jnp.sum(jnp.where(onehot, pstart[None, None, :], 0), axis=-1) + r
    cap = 2 * n + N_EXPERTS * bm
    nb = cap // bm
    blk_start = jnp.arange(nb, dtype=jnp.int32) * bm
    blk_e = jnp.minimum(jnp.sum((pend[None, :] <= blk_start[:, None]).astype(jnp.int32), axis=1), N_EXPERTS - 1)
    n_used = (pend[-1:] // bm).astype(jnp.int32)
    xs = _dispatch(x, dest, pend.astype(jnp.int32), padded.astype(jnp.int32), cap)
    ys = _experts(xs, blk_e, n_used, w_gate, w_up, w_down, layer)
    return _combine(x, ys, dest, gw, ln_g, ln_b)


def _even_layer(x, batch, seq, layer_idx, w_in, dec_f, dec_b, lq1, lk1, lq2, lk2, subln, w_out, ln_g, ln_b, route):
    d = x.shape[1]
    w = HEADS * LANE
    kw = dict(batch=batch, seq=seq)
    diff_seg = [(0, DIFF_ROT_DIM), (DIFF_HEAD_DIM, DIFF_ROT_DIM)]
    q, k, v, gate, dq, dk, dvt = _even_in(
        x, w_in.astype(BF16), _rope_tables(seq, [(0, LANE)], RET_THETA),
        _rope_tables(seq, diff_seg, ROPE_THETA), **kw)
    decays = jnp.stack([dec_f, dec_b]).astype(F32)
    of, ob = _retention(q, k, v, decays, **kw)
    lam_init = 0.8 - 0.6 * math.exp(-0.3 * layer_idx)
    diff = _flash(dq, dk, dvt, diff=(lq1, lk1, lq2, lk2, subln), lam_init=lam_init, **kw)
    wo = w_out.astype(BF16)
    return _outproj(x, diff, of, ob, gate, 0, jnp.ones((LANE,), F32), wo[w:], wo[:w], ln_g, ln_b, *route,
                    group_norm=True)


def _odd_layer(x, batch, seq, w_in, q_norm, w_uq, kv_norm, w_ukv, w2_f, b_f, w2_b, b_b, gla_norm, w_out, ln_g, ln_b,
               route):
    d = x.shape[1]
    w = HEADS * LANE
    o = np.cumsum([0, MLA_Q_RANK, MLA_KV_RANK, MLA_ROPE, HEADS * GLA_K_DIM, HEADS * GLA_K_DIM, w, w,
                   GLA_GATE_RANK, GLA_GATE_RANK]).tolist()
    zeros = lambda c: jnp.zeros((d, c), F32)
    w_in2 = jnp.concatenate([
        w_in[:, o[0]:o[2]], zeros(MLA_NOPE), w_in[:, o[2]:o[3]], zeros(LANE - MLA_NOPE - MLA_ROPE),
        w_in[:, o[3]:o[7]], w_in[:, o[7]:o[9]], zeros(LANE - 2 * GLA_GATE_RANK)], axis=1).astype(BF16)
    kw = dict(batch=batch, seq=seq)
    qd = MLA_NOPE + MLA_ROPE
    w_uq2 = jnp.pad(w_uq.reshape(MLA_Q_RANK, HEADS, qd), ((0, 0), (0, 0), (0, LANE - qd))).reshape(MLA_Q_RANK, w)
    ukv = w_ukv.reshape(MLA_KV_RANK, HEADS, MLA_NOPE + MLA_V)
    w_uk2 = jnp.pad(ukv[:, :, :MLA_NOPE], ((0, 0), (0, 0), (0, LANE - MLA_NOPE))).reshape(MLA_KV_RANK, w)
    w_uv2 = ukv[:, :, MLA_NOPE:].reshape(MLA_KV_RANK, w)
    q, k, vt, zg = _odd_in(x, w_in2, q_norm, w_uq2.astype(BF16), kv_norm,
                           jnp.concatenate([w_uk2, w_uv2], axis=1).astype(BF16),
                           _rope_tables(seq, [(MLA_NOPE, MLA_ROPE)], ROPE_THETA), **kw)
    mla = _flash(q, k, vt, **kw)
    wk = HEADS * GLA_K_DIM
    pad_rows = lambda m, r0: jnp.zeros((LANE, wk), F32).at[r0:r0 + GLA_GATE_RANK].set(m).astype(BF16)
    of, ob = _gla(zg, qcol=0, kcol=1, vcol=1, lcol=12,
                  w2f=pad_rows(w2_f, 0), bf=b_f.reshape(1, wk), w2b=pad_rows(w2_b, GLA_GATE_RANK), bb=b_b.reshape(1, wk), **kw)
    wo = w_out.astype(BF16)
    return _outproj(x, mla, of, ob, zg, 2, gla_norm, wo[:w], wo[w:], ln_g, ln_b, *route, group_norm=False)


def kernel(x, ev_w_in, ev_ret_decay_f, ev_ret_decay_b, ev_lq1, ev_lk1, ev_lq2, ev_lk2, ev_subln, ev_w_out, od_w_in, od_q_norm, od_w_uq, od_kv_norm, od_w_ukv, od_gla_w2_f, od_gla_b_f, od_gla_w2_b, od_gla_b_b, od_gla_norm, od_w_out, ln1_g, ln1_b, ln2_g, ln2_b, moe_w_grp, moe_b_grp, moe_w_exp, moe_b_exp, moe_w_gate, moe_w_up, moe_w_down):
    batch, seq, d = x.shape
    h = x.reshape(batch * seq, d)
    for i in range(DEPTH):
        j = i // 2
        route = _router_params(moe_w_grp[i], moe_b_grp[i], moe_w_exp[i], moe_b_exp[i])
        if i % 2 == 0:
            h, *routing = _even_layer(h, batch, seq, i, ev_w_in[j], ev_ret_decay_f[j], ev_ret_decay_b[j], ev_lq1[j],
                                      ev_lk1[j], ev_lq2[j], ev_lk2[j], ev_subln[j], ev_w_out[j], ln1_g[i], ln1_b[i], route)
        else:
            h, *routing = _odd_layer(h, batch, seq, od_w_in[j], od_q_norm[j], od_w_uq[j], od_kv_norm[j], od_w_ukv[j],
                                     od_gla_w2_f[j], od_gla_b_f[j], od_gla_w2_b[j], od_gla_b_b[j], od_gla_norm[j],
                                     od_w_out[j], ln1_g[i], ln1_b[i], route)
        h = _moe(h, routing, moe_w_gate, moe_w_up, moe_w_down, i, ln2_g[i], ln2_b[i])
    return h.reshape(batch, seq, d)
```

```python
import functools
import math

import numpy as np
import jax
import jax.numpy as jnp
from jax import lax
from jax.experimental import pallas as pl
from jax.experimental.pallas import tpu as pltpu

F32 = jnp.float32
BF16 = jnp.bfloat16

HEADS = 4
LANE = 128
RET_THETA = 10000.0
ROPE_THETA = 500000.0
DIFF_HEAD_DIM = 64
DIFF_ROT_DIM = 16
MLA_Q_RANK = 256
MLA_KV_RANK = 128
MLA_NOPE = 64
MLA_ROPE = 32
MLA_V = 128
GLA_K_DIM = 64
GLA_GATE_RANK = 16
GLA_TAU = 16.0
N_GROUPS = 4
EXPERTS_PER_GROUP = 8
N_EXPERTS = N_GROUPS * EXPERTS_PER_GROUP
DEPTH = 2
ALPHA = (2.0 * DEPTH) ** 0.25
LN_EPS = 1e-5
RMS_EPS = 1e-6

VMEM_LIMIT = 48 * 1024 * 1024


def _div_pow2(x, n):
    return lax.shift_right_logical(x, int(n).bit_length() - 1)


def _mod_pow2(x, n):
    return lax.bitwise_and(x, int(n) - 1)


def _cparams(sem):
    return pltpu.CompilerParams(dimension_semantics=sem, vmem_limit_bytes=VMEM_LIMIT)


def _rope_heads(z, tabs, sh, scale):
    c, sa, sb = tabs
    outs = []
    for h in range(HEADS):
        zh = z[:, h * LANE:(h + 1) * LANE]
        outs.append((zh * c + pltpu.roll(zh, sh, axis=1) * sa + pltpu.roll(zh, LANE - sh, axis=1) * sb) * scale)
    return outs


def _even_in_kernel(x_ref, w_ref, rc, rsa, rsb, dc, dsa, dsb, q_ref, k_ref, v_ref, g_ref, dq_ref, dk_ref, dvt_ref):
    w = HEADS * LANE
    xb = x_ref[...].astype(BF16)
    part = lambda t: jnp.dot(xb, w_ref[:, t * w:(t + 1) * w], preferred_element_type=F32)
    ret_t = (rc[...], rsa[...], rsb[...])
    diff_t = (dc[...], dsa[...], dsb[...])
    for h, o in enumerate(_rope_heads(part(0), ret_t, LANE // 2, 1.0)):
        q_ref[h] = o.astype(BF16)
    for h, o in enumerate(_rope_heads(part(1), ret_t, LANE // 2, LANE ** -0.5)):
        k_ref[h] = o.astype(BF16)
    rv = part(2)
    for h in range(HEADS):
        v_ref[h] = rv[:, h * LANE:(h + 1) * LANE].astype(BF16)
    g_ref[...] = part(3)
    for h, o in enumerate(_rope_heads(part(4), diff_t, DIFF_ROT_DIM // 2, DIFF_HEAD_DIM ** -0.5 * LOG2E)):
        dq_ref[h] = o.T.astype(BF16)
    for h, o in enumerate(_rope_heads(part(5), diff_t, DIFF_ROT_DIM // 2, 1.0)):
        dk_ref[h] = o.astype(BF16)
    dv = part(6)
    for h in range(HEADS):
        dvt_ref[h, :LANE, :] = dv[:, h * LANE:(h + 1) * LANE].T.astype(BF16)
        dvt_ref[h, LANE:, :] = jnp.ones((ONES_ROWS, dvt_ref.shape[2]), BF16)


def _even_in(x, w, ret_tabs, diff_tabs, *, batch, seq):
    n, d = x.shape
    tm = min(FLASH_TK_DIFF, seq // 2)
    nt = seq // tm
    hw = HEADS * LANE
    heads = jax.ShapeDtypeStruct((batch, HEADS, seq, LANE), BF16)
    head_spec = pl.BlockSpec((None, HEADS, tm, LANE), lambda i: (i // nt, 0, i % nt, 0))
    heads_t = jax.ShapeDtypeStruct((batch, HEADS, LANE, seq), BF16)
    head_t_spec = pl.BlockSpec((None, HEADS, LANE, tm), lambda i: (i // nt, 0, 0, i % nt))
    tab_spec = pl.BlockSpec((tm, LANE), lambda i: (i % nt, 0))
    return pl.pallas_call(
        _even_in_kernel,
        grid=(n // tm,),
        in_specs=[pl.BlockSpec((tm, d), lambda i: (i, 0)), pl.BlockSpec((d, 7 * hw), lambda i: (0, 0))] + [tab_spec] * 6,
        out_specs=[head_spec, head_spec, head_spec, pl.BlockSpec((tm, hw), lambda i: (i, 0)), head_t_spec, head_spec,
                   pl.BlockSpec((None, HEADS, None, LANE + ONES_ROWS, tm), lambda i: (i // nt, 0, i % nt, 0, 0))],
        out_shape=[heads, heads, heads, jax.ShapeDtypeStruct((n, hw), F32), heads_t, heads,
                   jax.ShapeDtypeStruct((batch, HEADS, nt, LANE + ONES_ROWS, tm), BF16)],
        compiler_params=_cparams(("parallel",)),
        name="even_in",
    )(x, w, *ret_tabs, *diff_tabs)


def _rms_rows(z, g):
    return z * lax.rsqrt(jnp.mean(z * z, axis=-1, keepdims=True) + RMS_EPS) * g


def _odd_in_kernel(x_ref, w_ref, qn_ref, wq_ref, kvn_ref, wkv_ref, tc, tsa, tsb, q_ref, k_ref, vt_ref, zg_ref):
    hw = HEADS * LANE
    mla_w = MLA_Q_RANK + MLA_KV_RANK + LANE
    xb = x_ref[...].astype(BF16)
    zg_ref[...] = jnp.dot(xb, w_ref[:, mla_w:], preferred_element_type=F32)
    z1 = jnp.dot(xb, w_ref[:, :mla_w], preferred_element_type=F32)
    tabs = (tc[...], tsa[...], tsb[...])
    sh = MLA_ROPE // 2
    qh = jnp.dot(_rms_rows(z1[:, :MLA_Q_RANK], qn_ref[...]).astype(BF16), wq_ref[...], preferred_element_type=F32)
    for h, o in enumerate(_rope_heads(qh, tabs, sh, (MLA_NOPE + MLA_ROPE) ** -0.5 * LOG2E)):
        q_ref[h] = o.T.astype(BF16)
    ckv = _rms_rows(z1[:, MLA_Q_RANK:MLA_Q_RANK + MLA_KV_RANK], kvn_ref[...]).astype(BF16)
    kv = jnp.dot(ckv, wkv_ref[...], preferred_element_type=F32)
    kr = z1[:, MLA_Q_RANK + MLA_KV_RANK:]
    kr = kr * tabs[0] + pltpu.roll(kr, sh, axis=1) * tabs[1] + pltpu.roll(kr, LANE - sh, axis=1) * tabs[2]
    for h in range(HEADS):
        k_ref[h] = (kv[:, h * LANE:(h + 1) * LANE] + kr).astype(BF16)
        vt_ref[h, :LANE, :] = kv[:, hw + h * LANE:hw + (h + 1) * LANE].T.astype(BF16)
        vt_ref[h, LANE:, :] = jnp.ones((ONES_ROWS, vt_ref.shape[2]), BF16)


def _odd_in(x, w, q_norm, w_uq, kv_norm, w_ukv, tabs, *, batch, seq, tm=512):
    n, d = x.shape
    tm = min(tm, seq // 2)
    nt = seq // tm
    tk = min(FLASH_TK, seq // 2)
    per = tk // tm
    hw = HEADS * LANE
    gw_ = w.shape[1] - (MLA_Q_RANK + MLA_KV_RANK + LANE)
    heads = jax.ShapeDtypeStruct((batch, HEADS, seq, LANE), BF16)
    head_spec = pl.BlockSpec((None, HEADS, tm, LANE), lambda i: (i // nt, 0, i % nt, 0))
    tab_spec = pl.BlockSpec((tm, LANE), lambda i: (i % nt, 0))
    const = lambda i: (0, 0)
    return pl.pallas_call(
        _odd_in_kernel,
        grid=(n // tm,),
        in_specs=[pl.BlockSpec((tm, d), lambda i: (i, 0)), pl.BlockSpec(w.shape, const),
                  pl.BlockSpec((1, MLA_Q_RANK), const), pl.BlockSpec(w_uq.shape, const),
                  pl.BlockSpec((1, MLA_KV_RANK), const), pl.BlockSpec(w_ukv.shape, const)] + [tab_spec] * 3,
        out_specs=[pl.BlockSpec((None, HEADS, LANE, tm), lambda i: (i // nt, 0, 0, i % nt)), head_spec,
                   pl.BlockSpec((None, HEADS, None, LANE + ONES_ROWS, tm),
                                lambda i: (i // nt, 0, (i % nt) // per, 0, (i % nt) % per)),
                   pl.BlockSpec((tm, gw_), lambda i: (i, 0))],
        out_shape=[jax.ShapeDtypeStruct((batch, HEADS, LANE, seq), BF16), heads, jax.ShapeDtypeStruct((batch, HEADS, seq // tk, LANE + ONES_ROWS, tk), BF16),
                   jax.ShapeDtypeStruct((n, gw_), F32)],
        compiler_params=_cparams(("parallel",)),
        name="odd_in",
    )(x, w, q_norm.reshape(1, -1), w_uq, kv_norm.reshape(1, -1), w_ukv, *tabs)


def _rope_tables(seq, segs, theta):
    pos = jnp.arange(seq, dtype=F32)
    inv = jnp.zeros((LANE,), F32)
    lo = np.zeros((LANE,), bool)
    hi = np.zeros((LANE,), bool)
    for start, rot in segs:
        half = rot // 2
        f = jnp.power(jnp.float32(theta), -jnp.arange(0, rot, 2, dtype=F32) / rot)
        inv = inv.at[start:start + half].set(f).at[start + half:start + rot].set(f)
        lo[start:start + half] = True
        hi[start + half:start + rot] = True
    ang = pos[:, None] * inv[None, :]
    cos, sin = jnp.cos(ang), jnp.sin(ang)
    c = jnp.where(lo | hi, cos, 1.0)
    sa = jnp.where(hi, sin, 0.0)
    sb = jnp.where(lo, -sin, 0.0)
    return c, sa, sb


ONES_ROWS = 16
LOG2E = math.log2(math.e)
FLASH_TK = 1024
FLASH_TK_DIFF = 512


def _flash_kernel(*refs, ncomp, nk, lam_init):
    if ncomp == 2:
        q_ref, k_ref, vt_ref, lq1, lk1, lq2, lk2, g_ref, o_ref, *scr = refs
    else:
        q_ref, k_ref, vt_ref, o_ref, *scr = refs
    qm_sc, m_sc, acc_sc, s0, s1, cm0, cm1, p0, p1, al0, al1 = scr
    tk = s0.shape[1]
    q = q_ref[...]
    if ncomp == 2:
        chan = lax.broadcasted_iota(jnp.int32, q.shape, 0)
        zero = jnp.zeros_like(q)
        qm_sc[0] = jnp.where(chan < DIFF_HEAD_DIM, q, zero)
        qm_sc[1] = jnp.where(chan >= DIFF_HEAD_DIM, q, zero)
    else:
        qm_sc[0] = q
    m_sc[...] = jnp.full(m_sc.shape, -jnp.inf, F32)
    acc_sc[...] = jnp.zeros(acc_sc.shape, F32)

    def scores(j, s_ref, cm_ref):
        k = k_ref[j * tk:(j + 1) * tk, :]
        for c in range(ncomp):
            s = jnp.dot(k, qm_sc[c], preferred_element_type=F32)
            s_ref[c] = s
            cm_ref[c] = jnp.max(s, axis=0, keepdims=True)

    def softmax(s_ref, cm_ref, p_ref, al_ref):
        for c in range(ncomp):
            m_old = m_sc[c]
            m_new = jnp.maximum(m_old, cm_ref[c])
            al_ref[c] = jnp.exp2(m_old - m_new)
            p_ref[c] = jnp.exp2(s_ref[c] - m_new).astype(BF16)
            m_sc[c] = m_new

    def values(j, p_ref, al_ref):
        vt = vt_ref[j]
        for c in range(ncomp):
            acc_sc[c] = al_ref[c] * acc_sc[c] + jnp.dot(vt, p_ref[c], preferred_element_type=F32)

    bufs = ((s0, cm0, p0, al0), (s1, cm1, p1, al1))
    scores(0, s0, cm0)
    for j in range(nk):
        s_c, cm_c, p_c, al_c = bufs[j % 2]
        s_n, cm_n, p_n, al_n = bufs[(j + 1) % 2]
        if j + 1 < nk:
            scores(j + 1, s_n, cm_n)
        softmax(s_c, cm_c, p_c, al_c)
        if j >= 1:
            values(j - 1, p_n, al_n)
    values(nk - 1, *bufs[(nk - 1) % 2][2:])

    def normalised(c):
        acc = acc_sc[c]
        return acc[:LANE] / acc[LANE:LANE + 1]

    o = normalised(0)
    if ncomp == 2:
        lam = (jnp.exp(jnp.sum(lq1[...] * lk1[...], keepdims=True))
               - jnp.exp(jnp.sum(lq2[...] * lk2[...], keepdims=True)) + lam_init)
        o = o - lam * normalised(1)
        o = o * lax.rsqrt(jnp.mean(o * o, axis=0, keepdims=True) + RMS_EPS) * g_ref[...] * (1.0 - lam_init)
    o_ref[...] = o.T


def _flash(q, k, vt, *, batch, seq, tq=512, diff=None, lam_init=0.0):
    nk, vrows, tk = vt.shape[2], vt.shape[3], vt.shape[4]
    assert vrows == LANE + ONES_ROWS
    tq = min(tq, seq)
    nq = seq // tq
    ncomp = 2 if diff is not None else 1
    in_specs = [
        pl.BlockSpec((None, None, LANE, tq), lambda b, h, i: (b, h, 0, i)),
        pl.BlockSpec((None, None, seq, LANE), lambda b, h, i: (b, h, 0, 0)),
        pl.BlockSpec((None, None, nk, vrows, tk), lambda b, h, i: (b, h, 0, 0, 0)),
    ]
    args = [q, k, vt]
    if diff is not None:
        lq1, lk1, lq2, lk2, subln = diff
        for v in (lq1, lk1, lq2, lk2):
            in_specs.append(pl.BlockSpec((1, DIFF_HEAD_DIM), lambda b, h, i: (0, 0)))
            args.append(v.reshape(1, DIFF_HEAD_DIM))
        in_specs.append(pl.BlockSpec((LANE, 1), lambda b, h, i: (0, 0)))
        args.append(subln.reshape(LANE, 1))
    return pl.pallas_call(
        functools.partial(_flash_kernel, ncomp=ncomp, nk=nk, lam_init=lam_init),
        grid=(batch, HEADS, nq),
        in_specs=in_specs,
        out_specs=pl.BlockSpec((tq, LANE), lambda b, h, i: (b * nq + i, h)),
        out_shape=jax.ShapeDtypeStruct((batch * seq, HEADS * LANE), F32),
        scratch_shapes=[pltpu.VMEM((ncomp, LANE, tq), BF16),
                        pltpu.VMEM((ncomp, 1, tq), F32), pltpu.VMEM((ncomp, vrows, tq), F32),
                        pltpu.VMEM((ncomp, tk, tq), F32), pltpu.VMEM((ncomp, tk, tq), F32),
                        pltpu.VMEM((ncomp, 1, tq), F32), pltpu.VMEM((ncomp, 1, tq), F32),
                        pltpu.VMEM((ncomp, tk, tq), BF16), pltpu.VMEM((ncomp, tk, tq), BF16),
                        pltpu.VMEM((ncomp, 1, tq), F32), pltpu.VMEM((ncomp, 1, tq), F32)],
        compiler_params=_cparams(("parallel", "parallel", "parallel")),
        name="flash_diff" if diff is not None else "flash_mla",
    )(*args)


def _ret_kernel(dec_ref, qf, kf, vf, qb, kb, vb, of_ref, ob_ref, s_sc, *, chunk):
    @pl.when(pl.program_id(1) == 0)
    def _():
        s_sc[...] = jnp.zeros(s_sc.shape, F32)

    ii = lax.broadcasted_iota(jnp.int32, (chunk, chunk), 0)
    jj = lax.broadcasted_iota(jnp.int32, (chunk, chunk), 1)
    r = lax.broadcasted_iota(jnp.int32, (chunk, 1), 0).astype(F32)
    for d, (q_ref, k_ref, v_ref, o_ref) in enumerate(((qf, kf, vf, of_ref), (qb, kb, vb, ob_ref))):
        for h in range(HEADS):
            la = -jnp.exp(jnp.full((1, 1), dec_ref[d, h], F32))
            if d == 0:
                mask, dist = ii >= jj, (ii - jj).astype(F32)
                qdec, kdec = jnp.exp(la * (r + 1.0)), jnp.exp(la * (chunk - 1.0 - r))
            else:
                mask, dist = jj > ii, (jj - ii).astype(F32)
                qdec, kdec = jnp.exp(la * (chunk - r)), jnp.exp(la * r)
            decay = jnp.where(mask, jnp.exp(jnp.where(mask, dist * la, 0.0)), 0.0)
            q, k, v = q_ref[h], k_ref[h], v_ref[h]
            s = lax.dot_general(q, k, (((1,), (1,)), ((), ())), preferred_element_type=F32)
            o = jnp.dot((s * decay).astype(BF16), v, preferred_element_type=F32)
            state = s_sc[d, h]
            o = o + qdec * jnp.dot(q, state.astype(BF16), preferred_element_type=F32)
            kd = (k.astype(F32) * kdec).astype(BF16)
            s_sc[d, h] = jnp.exp(la * float(chunk)) * state + lax.dot_general(
                kd, v, (((0,), (0,)), ((), ())), preferred_element_type=F32)
            o_ref[:, h * LANE:(h + 1) * LANE] = o


def _retention(q, k, v, decays, *, batch, seq, chunk=256):
    chunk = min(chunk, seq)
    n = seq // chunk
    fwd = pl.BlockSpec((None, HEADS, chunk, LANE), lambda b, c: (b, 0, c, 0))
    bwd = pl.BlockSpec((None, HEADS, chunk, LANE), lambda b, c: (b, 0, n - 1 - c, 0))
    w = HEADS * LANE
    out = jax.ShapeDtypeStruct((batch * seq, w), F32)
    return pl.pallas_call(
        functools.partial(_ret_kernel, chunk=chunk),
        grid=(batch, n),
        in_specs=[pl.BlockSpec(memory_space=pltpu.SMEM), fwd, fwd, fwd, bwd, bwd, bwd],
        out_specs=[pl.BlockSpec((chunk, w), lambda b, c: (b * n + c, 0)),
                   pl.BlockSpec((chunk, w), lambda b, c: (b * n + n - 1 - c, 0))],
        out_shape=[out, out],
        scratch_shapes=[pltpu.VMEM((2, HEADS, LANE, LANE), F32)],
        compiler_params=_cparams(("parallel", "arbitrary")),
        name="retention",
    )(decays, q, k, v, q, k, v)


GLA_SUB = 8


def _split3(x):
    x1 = x.astype(BF16)
    r1 = x - x1.astype(F32)
    x2 = r1.astype(BF16)
    x3 = (r1 - x2.astype(F32)).astype(BF16)
    return x1, x2, x3


def _gla_direction(q, k, v, lr, w2, bias, st, reverse):
    C, wk = q.shape
    wv = v.shape[1]
    dk, dv = wk // HEADS, wv // HEADS
    z = jnp.dot(lr.astype(BF16), w2, preferred_element_type=F32) + bias
    g = (jnp.minimum(z, 0.0) - jnp.log(1.0 + jnp.exp(-jnp.abs(z)))) * (LOG2E / GLA_TAU)
    ii = lax.broadcasted_iota(jnp.int32, (C, C), 0)
    jj = lax.broadcasted_iota(jnp.int32, (C, C), 1)
    tri = jnp.where(ii >= jj, 1.0, 0.0).astype(BF16)
    b = sum(jnp.dot(tri, part, preferred_element_type=F32) for part in _split3(g))
    tot = b[C - 1:C, :]
    c = (tot - b + g) if reverse else b

    qe = (q * jnp.exp2(jnp.minimum(c, 0.0))).astype(BF16)
    o = lax.dot_general(qe, st.astype(BF16), (((1,), (1,)), ((), ())), preferred_element_type=F32)
    ke = (k * jnp.exp2(jnp.minimum(tot - c, 0.0))).astype(BF16)
    upd = lax.dot_general(v.astype(BF16), ke, (((0,), (0,)), ((), ())), preferred_element_type=F32)
    rr = _div_pow2(lax.broadcasted_iota(jnp.int32, (wv, wk), 0), dv)
    cc = _div_pow2(lax.broadcasted_iota(jnp.int32, (wv, wk), 1), dk)
    new_st = jnp.where(rr == cc, st * jnp.exp2(tot) + upd, 0.0)

    lane_head = _div_pow2(lax.broadcasted_iota(jnp.int32, (C, wk), 1), dk)
    scores = [jnp.zeros((C, C), F32) for _ in range(HEADS)]
    hsz = C // 2
    while hsz >= GLA_SUB:
        blk = 2 * hsz
        rows = []
        for m in range(C // blk):
            rrow = m * blk + (hsz if reverse else hsz - 1)
            rows.append(jnp.broadcast_to(c[rrow:rrow + 1, :], (blk, wk)))
        ref = jnp.concatenate(rows, axis=0) if len(rows) > 1 else rows[0]
        qt = q * jnp.exp2(jnp.minimum(c - ref, 0.0))
        kt = (k * jnp.exp2(jnp.minimum(ref - c, 0.0))).astype(BF16)
        same = _div_pow2(ii, blk) == _div_pow2(jj, blk)
        if reverse:
            lvl = same & (_mod_pow2(ii, blk) < hsz) & (_mod_pow2(jj, blk) >= hsz)
        else:
            lvl = same & (_mod_pow2(ii, blk) >= hsz) & (_mod_pow2(jj, blk) < hsz)
        for h in range(HEADS):
            qh = jnp.where(lane_head == h, qt, 0.0).astype(BF16)
            s = lax.dot_general(qh, kt, (((1,), (1,)), ((), ())), preferred_element_type=F32)
            scores[h] = scores[h] + jnp.where(lvl, s, 0.0)
        hsz //= 2

    assert dv == C
    er = _div_pow2(lax.broadcasted_iota(jnp.int32, (wk, wv), 0), dk)
    ec = _div_pow2(lax.broadcasted_iota(jnp.int32, (wk, wv), 1), dv)
    expand = jnp.where(er == ec, 1.0, 0.0).astype(BF16)
    dist = (jj - ii) if reverse else (ii - jj)
    same_sub = _div_pow2(ii, GLA_SUB) == _div_pow2(jj, GLA_SUB)
    for lag in range(1 if reverse else 0, GLA_SUB):
        if lag == 0:
            t = q * k
        else:
            shift = (GLA_SUB - lag) if reverse else lag
            ks = pltpu.roll(k.reshape(C // GLA_SUB, GLA_SUB, wk), shift, axis=1).reshape(C, wk)
            cs = pltpu.roll(c.reshape(C // GLA_SUB, GLA_SUB, wk), shift, axis=1).reshape(C, wk)
            t = q * ks * jnp.exp2(jnp.minimum(c - cs, 0.0))
        red = jnp.dot(t.astype(BF16), expand, preferred_element_type=F32)
        on_diag = same_sub & (dist == lag)
        for h in range(HEADS):
            scores[h] = scores[h] + jnp.where(on_diag, red[:, h * dv:(h + 1) * dv], 0.0)

    vb = v.astype(BF16)
    o = o + jnp.concatenate(
        [jnp.dot(scores[h].astype(BF16), vb[:, h * dv:(h + 1) * dv], preferred_element_type=F32) for h in range(HEADS)],
        axis=1)
    return o, new_st


def _gla_kernel(qf, kf, vf, lf, qb, kb, vb, lb, w2f, bf, w2b, bb, of_ref, ob_ref, s_sc, *, qscale):
    @pl.when(pl.program_id(1) == 0)
    def _():
        s_sc[...] = jnp.zeros(s_sc.shape, F32)

    o, st = _gla_direction(qf[...] * qscale, kf[...], vf[...], lf[...], w2f[...], bf[...], s_sc[0], False)
    of_ref[...] = o
    s_sc[0] = st
    o, st = _gla_direction(qb[...] * qscale, kb[...], vb[...], lb[...], w2b[...], bb[...], s_sc[1], True)
    ob_ref[...] = o
    s_sc[1] = st


def _gla(z, *, qcol, kcol, vcol, lcol, w2f, bf, w2b, bb, batch, seq, chunk=128):
    chunk = min(chunk, seq)
    n = seq // chunk
    wk, wv = HEADS * GLA_K_DIM, HEADS * LANE

    def specs(cmap):
        return [pl.BlockSpec((chunk, wk), lambda b, c: (cmap(b, c), qcol)),
                pl.BlockSpec((chunk, wk), lambda b, c: (cmap(b, c), kcol)),
                pl.BlockSpec((chunk, wv), lambda b, c: (cmap(b, c), vcol)),
                pl.BlockSpec((chunk, LANE), lambda b, c: (cmap(b, c), lcol))]

    fmap = lambda b, c: b * n + c
    bmap = lambda b, c: b * n + n - 1 - c
    wspec = [pl.BlockSpec((LANE, wk), lambda b, c: (0, 0)), pl.BlockSpec((1, wk), lambda b, c: (0, 0))]
    out = jax.ShapeDtypeStruct((batch * seq, wv), F32)
    return pl.pallas_call(
        functools.partial(_gla_kernel, qscale=GLA_K_DIM ** -0.5),
        grid=(batch, n),
        in_specs=specs(fmap) + specs(bmap) + wspec + wspec,
        out_specs=[pl.BlockSpec((chunk, wv), lambda b, c: (fmap(b, c), 0)),
                   pl.BlockSpec((chunk, wv), lambda b, c: (bmap(b, c), 0))],
        out_shape=[out, out],
        scratch_shapes=[pltpu.VMEM((2, wv, wk), F32)],
        compiler_params=_cparams(("parallel", "arbitrary")),
        name="gla",
    )(z, z, z, z, z, z, z, z, w2f, bf, w2b, bb)


def _layer_norm_rows(r, g, b):
    mu = jnp.mean(r, axis=-1, keepdims=True)
    d = r - mu
    var = jnp.mean(d * d, axis=-1, keepdims=True)
    return d * lax.rsqrt(var + LN_EPS) * g + b


def _outproj_kernel(x_ref, fin_ref, of_ref, ob_ref, gate_ref, ng_ref, wa_ref, wb_ref, lg_ref, lb_ref, rw_ref, rb_ref,
                    o_ref, ids_ref, gw_ref, cnt_ref, run_sc, *, group_norm):
    lin = of_ref[...] + ob_ref[...]
    parts = []
    for h in range(HEADS):
        zh = lin[:, h * LANE:(h + 1) * LANE]
        if group_norm:
            mu = jnp.mean(zh, axis=-1, keepdims=True)
            dz = zh - mu
            parts.append(dz * lax.rsqrt(jnp.mean(dz * dz, axis=-1, keepdims=True) + LN_EPS))
        else:
            parts.append(zh * lax.rsqrt(jnp.mean(zh * zh, axis=-1, keepdims=True) + RMS_EPS) * ng_ref[...])
    gate = gate_ref[...]
    lin = jnp.concatenate(parts, axis=1) * (gate * jax.nn.sigmoid(gate))
    y = (jnp.dot(fin_ref[...].astype(BF16), wa_ref[...], preferred_element_type=F32)
         + jnp.dot(lin.astype(BF16), wb_ref[...], preferred_element_type=F32))
    x1 = _layer_norm_rows(ALPHA * x_ref[...] + y, lg_ref[...], lb_ref[...])
    o_ref[...] = x1
    _route_rows(x1, rw_ref, rb_ref, ids_ref, gw_ref, cnt_ref, run_sc)


def _outproj(x, fin, of, ob, gate_src, gate_col, norm_gain, wa, wb, ln_g, ln_b, route_w, route_b, *, group_norm, tm=512):
    n, d = x.shape
    w = HEADS * LANE
    tm = min(tm, n)
    row = lambda i: (i, 0)
    const = lambda i: (0, 0)
    rw_hi = route_w.astype(BF16)
    rw = jnp.stack([rw_hi, (route_w - rw_hi.astype(F32)).astype(BF16)])
    return pl.pallas_call(
        functools.partial(_outproj_kernel, group_norm=group_norm),
        grid=(n // tm,),
        in_specs=[pl.BlockSpec((tm, d), row), pl.BlockSpec((tm, w), row), pl.BlockSpec((tm, w), row),
                  pl.BlockSpec((tm, w), row), pl.BlockSpec((tm, w), lambda i: (i, gate_col)),
                  pl.BlockSpec((1, LANE), const), pl.BlockSpec((w, d), const), pl.BlockSpec((w, d), const),
                  pl.BlockSpec((1, d), const), pl.BlockSpec((1, d), const),
                  pl.BlockSpec((2, d, LANE), lambda i: (0, 0, 0)), pl.BlockSpec((1, LANE), const)],
        out_specs=[pl.BlockSpec((tm, d), row), pl.BlockSpec((tm, LANE), row), pl.BlockSpec((tm, LANE), row),
                   pl.BlockSpec((1, LANE), const)],
        out_shape=[jax.ShapeDtypeStruct((n, d), F32), jax.ShapeDtypeStruct((n, LANE), jnp.int32),
                   jax.ShapeDtypeStruct((n, LANE), F32), jax.ShapeDtypeStruct((1, LANE), jnp.int32)],
        scratch_shapes=[pltpu.VMEM((1, LANE), F32)],
        compiler_params=_cparams(("arbitrary",)),
        name="outproj",
    )(x, fin, of, ob, gate_src, norm_gain.reshape(1, LANE), wa, wb, ln_g.reshape(1, d), ln_b.reshape(1, d), rw, route_b)


def _route_rows(x, w_ref, b_ref, ids_ref, gw_ref, cnt_ref, run_sc):
    @pl.when(pl.program_id(0) == 0)
    def _():
        run_sc[...] = jnp.zeros(run_sc.shape, F32)

    tm = x.shape[0]
    xh = x.astype(BF16)
    xl = (x - xh.astype(F32)).astype(BF16)
    wh, wl = w_ref[0], w_ref[1]
    logits = (jnp.dot(xh, wh, preferred_element_type=F32) + jnp.dot(xh, wl, preferred_element_type=F32)
              + jnp.dot(xl, wh, preferred_element_type=F32)) + b_ref[...]
    lane = lax.broadcasted_iota(jnp.int32, logits.shape, 1)
    neg = -jnp.inf
    gmask = (lane >= N_EXPERTS) & (lane < N_EXPERTS + N_GROUPS)
    gl = jnp.where(gmask, logits, neg)
    gmax = jnp.max(gl, axis=1, keepdims=True)
    lane_f = lane.astype(F32)
    first = lambda hit: jnp.min(jnp.where(hit, lane_f, float(LANE)), axis=1, keepdims=True).astype(jnp.int32)
    gidx = first(gl == gmax) - N_EXPERTS
    p_grp = 1.0 / jnp.sum(jnp.where(gmask, jnp.exp(gl - gmax), 0.0), axis=1, keepdims=True)
    el = jnp.where(_div_pow2(lane, EXPERTS_PER_GROUP) == gidx, logits, neg)
    l1 = jnp.max(el, axis=1, keepdims=True)
    e1 = first(el == l1)
    el2 = jnp.where(lane == e1, neg, el)
    l2 = jnp.max(el2, axis=1, keepdims=True)
    e2 = first(el2 == l2)
    t = jnp.exp(l2 - l1)
    w1 = p_grp / (1.0 + t)
    w2 = p_grp * t / (1.0 + t)

    onehot = jnp.where(lane == e1, 1.0, jnp.where(lane == e2, 1.0, 0.0))
    ri = lax.broadcasted_iota(jnp.int32, (tm, tm), 0)
    ci = lax.broadcasted_iota(jnp.int32, (tm, tm), 1)
    before = jnp.dot(jnp.where(ri > ci, 1.0, 0.0).astype(BF16), onehot.astype(BF16), preferred_element_type=F32)
    before = before + run_sc[...]
    r1 = jnp.sum(jnp.where(lane == e1, before, 0.0), axis=1, keepdims=True).astype(jnp.int32)
    r2 = jnp.sum(jnp.where(lane == e2, before, 0.0), axis=1, keepdims=True).astype(jnp.int32)
    run_sc[...] = run_sc[...] + jnp.sum(onehot, axis=0, keepdims=True)
    cnt_ref[...] = run_sc[...].astype(jnp.int32)
    ids_ref[...] = jnp.where(lane == 0, e1, jnp.where(lane == 1, e2, jnp.where(lane == 2, r1, jnp.where(lane == 3, r2, 0))))
    gw_ref[...] = jnp.where(lane == 0, w1, jnp.where(lane == 1, w2, 0.0))


MOE_BM = 256
SUBL = 8


def _rows_from_linear(ref, rows):
    return jnp.concatenate([ref[pl.ds(s, rows, stride=SUBL), :] for s in range(SUBL)], axis=1)


def _rows_to_linear(ref, val):
    for s in range(SUBL):
        ref[pl.ds(s, val.shape[0], stride=SUBL), :] = val[:, s * LANE:(s + 1) * LANE]


def _dispatch_kernel(pend_ref, padded_ref, dest_ref, x_ref, xs_hbm, idx_smem, zbuf, lin, sem_i, sem_z, sem):
    i = pl.program_id(0)
    tm = x_ref.shape[0]
    bm = zbuf.shape[0] // SUBL

    @pl.when(i == 0)
    def _():
        zbuf[...] = jnp.zeros(zbuf.shape, F32)

        def tail(e):
            start_row = pl.multiple_of((pend_ref[e] - bm) * SUBL, bm * SUBL)
            return pltpu.make_async_copy(zbuf, xs_hbm.at[pl.ds(start_row, bm * SUBL), :], sem_z)

        def start(e, carry):
            @pl.when(padded_ref[e] > 0)
            def _():
                tail(e).start()
            return carry

        def wait(e, carry):
            @pl.when(padded_ref[e] > 0)
            def _():
                tail(e).wait()
            return carry

        lax.fori_loop(0, N_EXPERTS, start, 0)
        lax.fori_loop(0, N_EXPERTS, wait, 0)

        def unused(b):
            start_row = pl.multiple_of(b * bm * SUBL, bm * SUBL)
            return pltpu.make_async_copy(zbuf, xs_hbm.at[pl.ds(start_row, bm * SUBL), :], sem_z)

        first_unused = pend_ref[N_EXPERTS - 1] // bm
        n_blocks = xs_hbm.shape[0] // (bm * SUBL)
        lax.fori_loop(first_unused, n_blocks, lambda b, c: (unused(b).start(), c)[1], 0)
        lax.fori_loop(first_unused, n_blocks, lambda b, c: (unused(b).wait(), c)[1], 0)

    cp = pltpu.make_async_copy(dest_ref.at[i], idx_smem, sem_i)
    cp.start()
    _rows_to_linear(lin, x_ref[...])
    cp.wait()

    def scatter(r, carry):
        src = lin.at[pl.ds(pl.multiple_of(r * SUBL, SUBL), SUBL), :]
        for k in range(2):
            dst = pl.multiple_of(idx_smem[2 * r + k] * SUBL, SUBL)
            pltpu.make_async_copy(src, xs_hbm.at[pl.ds(dst, SUBL), :], sem).start(priority=k)
        return carry

    lax.fori_loop(0, tm, scatter, 0, unroll=8)
    for k in range(2):
        pltpu.make_async_copy(lin, xs_hbm.at[pl.ds(0, tm * SUBL), :], sem).wait()


def _dispatch(x, dest, pend, padded, cap, *, tm=512):
    n, d = x.shape
    assert d == SUBL * LANE
    tm = min(tm, n)
    nt = n // tm
    grid_spec = pltpu.PrefetchScalarGridSpec(
        num_scalar_prefetch=2,
        grid=(nt,),
        in_specs=[pl.BlockSpec((nt, 2 * tm), lambda i, pe, pa: (0, 0)),
                  pl.BlockSpec((tm, d), lambda i, pe, pa: (i, 0))],
        out_specs=pl.BlockSpec(memory_space=pl.ANY),
        scratch_shapes=[pltpu.SMEM((2 * tm,), jnp.int32), pltpu.VMEM((MOE_BM * SUBL, LANE), F32),
                        pltpu.VMEM((tm * SUBL, LANE), F32),
                        pltpu.SemaphoreType.DMA(()), pltpu.SemaphoreType.DMA(()), pltpu.SemaphoreType.DMA(())],
    )
    return pl.pallas_call(
        _dispatch_kernel,
        grid_spec=grid_spec,
        out_shape=jax.ShapeDtypeStruct((cap * SUBL, LANE), F32),
        compiler_params=_cparams(("arbitrary",)),
        name="dispatch",
    )(pend, padded, dest.reshape(nt, 2 * tm), x)


def _experts_kernel(blk_e_ref, nused_ref, xs_ref, wg_ref, wu_ref, wd_ref, ys_ref, wgb, wub, wdb, hbuf):
    i = pl.program_id(0)
    nb = pl.num_programs(0) - 1
    cur = jnp.minimum(i, nb - 1)
    prv = jnp.maximum(i - 1, 0)

    @pl.when(i == 0)
    def _():
        hbuf[...] = jnp.zeros(hbuf.shape, BF16)

    @pl.when(jnp.logical_or(i == 0, blk_e_ref[cur] != blk_e_ref[jnp.minimum(prv, nb - 1)]))
    def _():
        wgb[...] = wg_ref[...].astype(BF16)
        wub[...] = wu_ref[...].astype(BF16)

    @pl.when(jnp.logical_or(i == 0, blk_e_ref[prv] != blk_e_ref[jnp.maximum(i - 2, 0)]))
    def _():
        wdb[...] = wd_ref[...].astype(BF16)

    slot = lax.rem(i, 2)
    y = jnp.dot(hbuf[1 - slot], wdb[...], preferred_element_type=F32)
    live = jnp.logical_and(i >= 1, i - 1 < nused_ref[0])
    _rows_to_linear(ys_ref, jnp.where(live, y, 0.0))
    xb = _rows_from_linear(xs_ref, xs_ref.shape[0] // SUBL).astype(BF16)
    hg = jnp.dot(xb, wgb[...], preferred_element_type=F32)
    hu = jnp.dot(xb, wub[...], preferred_element_type=F32)
    hbuf[slot] = (hg * jax.nn.sigmoid(hg) * hu).astype(BF16)


def _experts(xs, blk_e, n_used, w_gate, w_up, w_down, layer):
    cap = xs.shape[0] // SUBL
    bm = MOE_BM
    nb = cap // bm
    d, de = w_gate.shape[2], w_gate.shape[3]
    row_in = lambda i, be, nu: (jnp.minimum(i, nu[0] - 1), 0)
    row = lambda i, be, nu: (jnp.maximum(i - 1, 0), 0)
    w_cur = lambda i, be, nu: (layer, be[jnp.minimum(i, nb - 1)], 0, 0)
    w_prv = lambda i, be, nu: (layer, be[jnp.maximum(i - 1, 0)], 0, 0)
    grid_spec = pltpu.PrefetchScalarGridSpec(
        num_scalar_prefetch=2,
        grid=(nb + 1,),
        in_specs=[pl.BlockSpec((bm * SUBL, LANE), row_in),
                  pl.BlockSpec((None, None, d, de), w_cur),
                  pl.BlockSpec((None, None, d, de), w_cur),
                  pl.BlockSpec((None, None, de, d), w_prv)],
        out_specs=pl.BlockSpec((bm * SUBL, LANE), row),
        scratch_shapes=[pltpu.VMEM((d, de), BF16), pltpu.VMEM((d, de), BF16), pltpu.VMEM((de, d), BF16),
                        pltpu.VMEM((2, bm, de), BF16)],
    )
    return pl.pallas_call(
        _experts_kernel,
        grid_spec=grid_spec,
        out_shape=jax.ShapeDtypeStruct((cap * SUBL, LANE), F32),
        compiler_params=_cparams(("arbitrary",)),
        name="experts",
    )(blk_e, n_used, xs, w_gate, w_up, w_down)


def _combine_kernel(dest_ref, x_ref, gw_ref, g_ref, b_ref, ys_hbm, o_ref, idx_smem, ybuf, sem_i, sem):
    i = pl.program_id(0)
    tm = x_ref.shape[0]

    def issue(tile, slot):
        cp = pltpu.make_async_copy(dest_ref.at[tile], idx_smem, sem_i)
        cp.start()
        cp.wait()

        def gather(r, carry):
            row = pl.multiple_of(r * SUBL, SUBL)
            for k in range(2):
                src = pl.multiple_of(idx_smem[2 * r + k] * SUBL, SUBL)
                pltpu.make_async_copy(ys_hbm.at[pl.ds(src, SUBL), :], ybuf.at[slot, k, pl.ds(row, SUBL), :],
                                      sem.at[slot]).start(priority=k)
            return carry

        lax.fori_loop(0, tm, gather, 0, unroll=8)

    slot = lax.rem(i, 2)

    @pl.when(i == 0)
    def _():
        issue(0, 0)

    @pl.when(i + 1 < pl.num_programs(0))
    def _():
        issue(i + 1, 1 - slot)

    for k in range(2):
        pltpu.make_async_copy(ys_hbm.at[pl.ds(0, tm * SUBL), :], ybuf.at[slot, k], sem.at[slot]).wait()
    gw = gw_ref[...]
    ffn = (_rows_from_linear(ybuf.at[slot, 0], tm) * gw[:, 0:1] + _rows_from_linear(ybuf.at[slot, 1], tm) * gw[:, 1:2])
    o_ref[...] = _layer_norm_rows(ALPHA * x_ref[...] + ffn, g_ref[...], b_ref[...])


def _combine(x, ys, dest, gw, ln_g, ln_b, *, tm=512):
    n, d = x.shape
    tm = min(tm, n)
    nt = n // tm
    return pl.pallas_call(
        _combine_kernel,
        grid=(nt,),
        in_specs=[pl.BlockSpec((nt, 2 * tm), lambda i: (0, 0)),
                  pl.BlockSpec((tm, d), lambda i: (i, 0)), pl.BlockSpec((tm, LANE), lambda i: (i, 0)),
                  pl.BlockSpec((1, d), lambda i: (0, 0)), pl.BlockSpec((1, d), lambda i: (0, 0)),
                  pl.BlockSpec(memory_space=pl.ANY)],
        out_specs=pl.BlockSpec((tm, d), lambda i: (i, 0)),
        out_shape=jax.ShapeDtypeStruct((n, d), F32),
        scratch_shapes=[pltpu.SMEM((2 * tm,), jnp.int32), pltpu.VMEM((2, 2, tm * SUBL, LANE), F32),
                        pltpu.SemaphoreType.DMA(()), pltpu.SemaphoreType.DMA((2,))],
        compiler_params=_cparams(("arbitrary",)),
        name="combine",
    )(dest.reshape(nt, 2 * tm), x, gw, ln_g.reshape(1, d), ln_b.reshape(1, d), ys)


def _router_params(w_grp, b_grp, w_exp, b_exp):
    d = w_exp.shape[0]
    wr = jnp.zeros((d, LANE), F32).at[:, :N_EXPERTS].set(w_exp).at[:, N_EXPERTS:N_EXPERTS + N_GROUPS].set(w_grp)
    br = jnp.zeros((1, LANE), F32).at[0, :N_EXPERTS].set(b_exp).at[0, N_EXPERTS:N_EXPERTS + N_GROUPS].set(b_grp)
    return wr, br


def _moe(x, routing, w_gate, w_up, w_down, layer, ln_g, ln_b):
    n, d = x.shape
    ids, gw, cnt = routing
    bm = MOE_BM
    counts = cnt[0, :N_EXPERTS]
    padded = (counts + bm - 1) // bm * bm
    pend = jnp.cumsum(padded)
    pstart = pend - padded
    e, r = ids[:, 0:2], ids[:, 2:4]
    onehot = e[:, :, None] == jnp.arange(N_EXPERTS, dtype=jnp.int32)[None, None, :]
    dest = jnp.sum(jnp.where(onehot, pstart[None, None, :], 0), axis=-1) + r
    cap = 2 * n + N_EXPERTS * bm
    nb = cap // bm
    blk_start = jnp.arange(nb, dtype=jnp.int32) * bm
    blk_e = jnp.minimum(jnp.sum((pend[None, :] <= blk_start[:, None]).astype(jnp.int32), axis=1), N_EXPERTS - 1)
    n_used = (pend[-1:] // bm).astype(jnp.int32)
    xs = _dispatch(x, dest, pend.astype(jnp.int32), padded.astype(jnp.int32), cap)
    ys = _experts(xs, blk_e, n_used, w_gate, w_up, w_down, layer)
    return _combine(x, ys, dest, gw, ln_g, ln_b)


def _even_layer(x, batch, seq, layer_idx, w_in, dec_f, dec_b, lq1, lk1, lq2, lk2, subln, w_out, ln_g, ln_b, route):
    d = x.shape[1]
    w = HEADS * LANE
    kw = dict(batch=batch, seq=seq)
    diff_seg = [(0, DIFF_ROT_DIM), (DIFF_HEAD_DIM, DIFF_ROT_DIM)]
    q, k, v, gate, dq, dk, dvt = _even_in(
        x, w_in.astype(BF16), _rope_tables(seq, [(0, LANE)], RET_THETA),
        _rope_tables(seq, diff_seg, ROPE_THETA), **kw)
    decays = jnp.stack([dec_f, dec_b]).astype(F32)
    of, ob = _retention(q, k, v, decays, **kw)
    lam_init = 0.8 - 0.6 * math.exp(-0.3 * layer_idx)
    diff = _flash(dq, dk, dvt, diff=(lq1, lk1, lq2, lk2, subln), lam_init=lam_init, **kw)
    wo = w_out.astype(BF16)
    return _outproj(x, diff, of, ob, gate, 0, jnp.ones((LANE,), F32), wo[w:], wo[:w], ln_g, ln_b, *route,
                    group_norm=True)


def _odd_layer(x, batch, seq, w_in, q_norm, w_uq, kv_norm, w_ukv, w2_f, b_f, w2_b, b_b, gla_norm, w_out, ln_g, ln_b,
               route):
    d = x.shape[1]
    w = HEADS * LANE
    o = np.cumsum([0, MLA_Q_RANK, MLA_KV_RANK, MLA_ROPE, HEADS * GLA_K_DIM, HEADS * GLA_K_DIM, w, w,
                   GLA_GATE_RANK, GLA_GATE_RANK]).tolist()
    zeros = lambda c: jnp.zeros((d, c), F32)
    w_in2 = jnp.concatenate([
        w_in[:, o[0]:o[2]], zeros(MLA_NOPE), w_in[:, o[2]:o[3]], zeros(LANE - MLA_NOPE - MLA_ROPE),
        w_in[:, o[3]:o[7]], w_in[:, o[7]:o[9]], zeros(LANE - 2 * GLA_GATE_RANK)], axis=1).astype(BF16)
    kw = dict(batch=batch, seq=seq)
    qd = MLA_NOPE + MLA_ROPE
    w_uq2 = jnp.pad(w_uq.reshape(MLA_Q_RANK, HEADS, qd), ((0, 0), (0, 0), (0, LANE - qd))).reshape(MLA_Q_RANK, w)
    ukv = w_ukv.reshape(MLA_KV_RANK, HEADS, MLA_NOPE + MLA_V)
    w_uk2 = jnp.pad(ukv[:, :, :MLA_NOPE], ((0, 0), (0, 0), (0, LANE - MLA_NOPE))).reshape(MLA_KV_RANK, w)
    w_uv2 = ukv[:, :, MLA_NOPE:].reshape(MLA_KV_RANK, w)
    q, k, vt, zg = _odd_in(x, w_in2, q_norm, w_uq2.astype(BF16), kv_norm,
                           jnp.concatenate([w_uk2, w_uv2], axis=1).astype(BF16),
                           _rope_tables(seq, [(MLA_NOPE, MLA_ROPE)], ROPE_THETA), **kw)
    mla = _flash(q, k, vt, **kw)
    wk = HEADS * GLA_K_DIM
    pad_rows = lambda m, r0: jnp.zeros((LANE, wk), F32).at[r0:r0 + GLA_GATE_RANK].set(m).astype(BF16)
    of, ob = _gla(zg, qcol=0, kcol=1, vcol=1, lcol=12,
                  w2f=pad_rows(w2_f, 0), bf=b_f.reshape(1, wk), w2b=pad_rows(w2_b, GLA_GATE_RANK), bb=b_b.reshape(1, wk), **kw)
    wo = w_out.astype(BF16)
    return _outproj(x, mla, of, ob, zg, 2, gla_norm, wo[:w], wo[w:], ln_g, ln_b, *route, group_norm=False)


def kernel(x, ev_w_in, ev_ret_decay_f, ev_ret_decay_b, ev_lq1, ev_lk1, ev_lq2, ev_lk2, ev_subln, ev_w_out, od_w_in, od_q_norm, od_w_uq, od_kv_norm, od_w_ukv, od_gla_w2_f, od_gla_b_f, od_gla_w2_b, od_gla_b_b, od_gla_norm, od_w_out, ln1_g, ln1_b, ln2_g, ln2_b, moe_w_grp, moe_b_grp, moe_w_exp, moe_b_exp, moe_w_gate, moe_w_up, moe_w_down):
    batch, seq, d = x.shape
    h = x.reshape(batch * seq, d)
    for i in range(DEPTH):
        j = i // 2
        route = _router_params(moe_w_grp[i], moe_b_grp[i], moe_w_exp[i], moe_b_exp[i])
        if i % 2 == 0:
            h, *routing = _even_layer(h, batch, seq, i, ev_w_in[j], ev_ret_decay_f[j], ev_ret_decay_b[j], ev_lq1[j],
                                      ev_lk1[j], ev_lq2[j], ev_lk2[j], ev_subln[j], ev_w_out[j], ln1_g[i], ln1_b[i], route)
        else:
            h, *routing = _odd_layer(h, batch, seq, od_w_in[j], od_q_norm[j], od_w_uq[j], od_kv_norm[j], od_w_ukv[j],
                                     od_gla_w2_f[j], od_gla_b_f[j], od_gla_w2_b[j], od_gla_b_b[j], od_gla_norm[j],
                                     od_w_out[j], ln1_g[i], ln1_b[i], route)
        h = _moe(h, routing, moe_w_gate, moe_w_up, moe_w_down, i, ln2_g[i], ln2_b[i])
    return h.reshape(batch, seq, d)
```

```python
import functools
import math

import numpy as np
import jax
import jax.numpy as jnp
from jax import lax
from jax.experimental import pallas as pl
from jax.experimental.pallas import tpu as pltpu

F32 = jnp.float32
BF16 = jnp.bfloat16

HEADS = 4
LANE = 128
RET_THETA = 10000.0
ROPE_THETA = 500000.0
DIFF_HEAD_DIM = 64
DIFF_ROT_DIM = 16
MLA_Q_RANK = 256
MLA_KV_RANK = 128
MLA_NOPE = 64
MLA_ROPE = 32
MLA_V = 128
GLA_K_DIM = 64
GLA_GATE_RANK = 16
GLA_TAU = 16.0
N_GROUPS = 4
EXPERTS_PER_GROUP = 8
N_EXPERTS = N_GROUPS * EXPERTS_PER_GROUP
DEPTH = 2
ALPHA = (2.0 * DEPTH) ** 0.25
LN_EPS = 1e-5
RMS_EPS = 1e-6

VMEM_LIMIT = 48 * 1024 * 1024


def _div_pow2(x, n):
    return lax.shift_right_logical(x, int(n).bit_length() - 1)


def _mod_pow2(x, n):
    return lax.bitwise_and(x, int(n) - 1)


def _cparams(sem):
    return pltpu.CompilerParams(dimension_semantics=sem, vmem_limit_bytes=VMEM_LIMIT)


def _rope_heads(z, tabs, sh, scale):
    c, sa, sb = tabs
    outs = []
    for h in range(HEADS):
        zh = z[:, h * LANE:(h + 1) * LANE]
        outs.append((zh * c + pltpu.roll(zh, sh, axis=1) * sa + pltpu.roll(zh, LANE - sh, axis=1) * sb) * scale)
    return outs


def _even_in_kernel(x_ref, w_ref, rc, rsa, rsb, dc, dsa, dsb, q_ref, k_ref, v_ref, g_ref, dq_ref, dk_ref, dvt_ref):
    w = HEADS * LANE
    xb = x_ref[...].astype(BF16)
    part = lambda t: jnp.dot(xb, w_ref[:, t * w:(t + 1) * w], preferred_element_type=F32)
    ret_t = (rc[...], rsa[...], rsb[...])
    diff_t = (dc[...], dsa[...], dsb[...])
    for h, o in enumerate(_rope_heads(part(0), ret_t, LANE // 2, 1.0)):
        q_ref[h] = o.astype(BF16)
    for h, o in enumerate(_rope_heads(part(1), ret_t, LANE // 2, LANE ** -0.5)):
        k_ref[h] = o.astype(BF16)
    rv = part(2)
    for h in range(HEADS):
        v_ref[h] = rv[:, h * LANE:(h + 1) * LANE].astype(BF16)
    g_ref[...] = part(3)
    for h, o in enumerate(_rope_heads(part(4), diff_t, DIFF_ROT_DIM // 2, DIFF_HEAD_DIM ** -0.5 * LOG2E)):
        dq_ref[h] = o.T.astype(BF16)
    for h, o in enumerate(_rope_heads(part(5), diff_t, DIFF_ROT_DIM // 2, 1.0)):
        dk_ref[h] = o.astype(BF16)
    dv = part(6)
    for h in range(HEADS):
        dvt_ref[h, :LANE, :] = dv[:, h * LANE:(h + 1) * LANE].T.astype(BF16)
        dvt_ref[h, LANE:, :] = jnp.ones((ONES_ROWS, dvt_ref.shape[2]), BF16)


def _even_in(x, w, ret_tabs, diff_tabs, *, batch, seq):
    n, d = x.shape
    tm = min(FLASH_TK_DIFF, seq // 2)
    nt = seq // tm
    hw = HEADS * LANE
    heads = jax.ShapeDtypeStruct((batch, HEADS, seq, LANE), BF16)
    head_spec = pl.BlockSpec((None, HEADS, tm, LANE), lambda i: (i // nt, 0, i % nt, 0))
    heads_t = jax.ShapeDtypeStruct((batch, HEADS, LANE, seq), BF16)
    head_t_spec = pl.BlockSpec((None, HEADS, LANE, tm), lambda i: (i // nt, 0, 0, i % nt))
    tab_spec = pl.BlockSpec((tm, LANE), lambda i: (i % nt, 0))
    return pl.pallas_call(
        _even_in_kernel,
        grid=(n // tm,),
        in_specs=[pl.BlockSpec((tm, d), lambda i: (i, 0)), pl.BlockSpec((d, 7 * hw), lambda i: (0, 0))] + [tab_spec] * 6,
        out_specs=[head_spec, head_spec, head_spec, pl.BlockSpec((tm, hw), lambda i: (i, 0)), head_t_spec, head_spec,
                   pl.BlockSpec((None, HEADS, None, LANE + ONES_ROWS, tm), lambda i: (i // nt, 0, i % nt, 0, 0))],
        out_shape=[heads, heads, heads, jax.ShapeDtypeStruct((n, hw), F32), heads_t, heads,
                   jax.ShapeDtypeStruct((batch, HEADS, nt, LANE + ONES_ROWS, tm), BF16)],
        compiler_params=_cparams(("parallel",)),
        name="even_in",
    )(x, w, *ret_tabs, *diff_tabs)


def _rms_rows(z, g):
    return z * lax.rsqrt(jnp.mean(z * z, axis=-1, keepdims=True) + RMS_EPS) * g


def _odd_in_kernel(x_ref, w_ref, qn_ref, wq_ref, kvn_ref, wkv_ref, tc, tsa, tsb, q_ref, k_ref, vt_ref, zg_ref):
    hw = HEADS * LANE
    mla_w = MLA_Q_RANK + MLA_KV_RANK + LANE
    xb = x_ref[...].astype(BF16)
    zg_ref[...] = jnp.dot(xb, w_ref[:, mla_w:], preferred_element_type=F32)
    z1 = jnp.dot(xb, w_ref[:, :mla_w], preferred_element_type=F32)
    tabs = (tc[...], tsa[...], tsb[...])
    sh = MLA_ROPE // 2
    qh = jnp.dot(_rms_rows(z1[:, :MLA_Q_RANK], qn_ref[...]).astype(BF16), wq_ref[...], preferred_element_type=F32)
    for h, o in enumerate(_rope_heads(qh, tabs, sh, (MLA_NOPE + MLA_ROPE) ** -0.5 * LOG2E)):
        q_ref[h] = o.T.astype(BF16)
    ckv = _rms_rows(z1[:, MLA_Q_RANK:MLA_Q_RANK + MLA_KV_RANK], kvn_ref[...]).astype(BF16)
    kv = jnp.dot(ckv, wkv_ref[...], preferred_element_type=F32)
    kr = z1[:, MLA_Q_RANK + MLA_KV_RANK:]
    kr = kr * tabs[0] + pltpu.roll(kr, sh, axis=1) * tabs[1] + pltpu.roll(kr, LANE - sh, axis=1) * tabs[2]
    for h in range(HEADS):
        k_ref[h] = (kv[:, h * LANE:(h + 1) * LANE] + kr).astype(BF16)
        vt_ref[h, :LANE, :] = kv[:, hw + h * LANE:hw + (h + 1) * LANE].T.astype(BF16)
        vt_ref[h, LANE:, :] = jnp.ones((ONES_ROWS, vt_ref.shape[2]), BF16)


def _odd_in(x, w, q_norm, w_uq, kv_norm, w_ukv, tabs, *, batch, seq, tm=512):
    n, d = x.shape
    tm = min(tm, seq // 2)
    nt = seq // tm
    tk = min(FLASH_TK, seq // 2)
    per = tk // tm
    hw = HEADS * LANE
    gw_ = w.shape[1] - (MLA_Q_RANK + MLA_KV_RANK + LANE)
    heads = jax.ShapeDtypeStruct((batch, HEADS, seq, LANE), BF16)
    head_spec = pl.BlockSpec((None, HEADS, tm, LANE), lambda i: (i // nt, 0, i % nt, 0))
    tab_spec = pl.BlockSpec((tm, LANE), lambda i: (i % nt, 0))
    const = lambda i: (0, 0)
    return pl.pallas_call(
        _odd_in_kernel,
        grid=(n // tm,),
        in_specs=[pl.BlockSpec((tm, d), lambda i: (i, 0)), pl.BlockSpec(w.shape, const),
                  pl.BlockSpec((1, MLA_Q_RANK), const), pl.BlockSpec(w_uq.shape, const),
                  pl.BlockSpec((1, MLA_KV_RANK), const), pl.BlockSpec(w_ukv.shape, const)] + [tab_spec] * 3,
        out_specs=[pl.BlockSpec((None, HEADS, LANE, tm), lambda i: (i // nt, 0, 0, i % nt)), head_spec,
                   pl.BlockSpec((None, HEADS, None, LANE + ONES_ROWS, tm),
                                lambda i: (i // nt, 0, (i % nt) // per, 0, (i % nt) % per)),
                   pl.BlockSpec((tm, gw_), lambda i: (i, 0))],
        out_shape=[jax.ShapeDtypeStruct((batch, HEADS, LANE, seq), BF16), heads, jax.ShapeDtypeStruct((batch, HEADS, seq // tk, LANE + ONES_ROWS, tk), BF16),
                   jax.ShapeDtypeStruct((n, gw_), F32)],
        compiler_params=_cparams(("parallel",)),
        name="odd_in",
    )(x, w, q_norm.reshape(1, -1), w_uq, kv_norm.reshape(1, -1), w_ukv, *tabs)


def _rope_tables(seq, segs, theta):
    pos = jnp.arange(seq, dtype=F32)
    inv = jnp.zeros((LANE,), F32)
    lo = np.zeros((LANE,), bool)
    hi = np.zeros((LANE,), bool)
    for start, rot in segs:
        half = rot // 2
        f = jnp.power(jnp.float32(theta), -jnp.arange(0, rot, 2, dtype=F32) / rot)
        inv = inv.at[start:start + half].set(f).at[start + half:start + rot].set(f)
        lo[start:start + half] = True
        hi[start + half:start + rot] = True
    ang = pos[:, None] * inv[None, :]
    cos, sin = jnp.cos(ang), jnp.sin(ang)
    c = jnp.where(lo | hi, cos, 1.0)
    sa = jnp.where(hi, sin, 0.0)
    sb = jnp.where(lo, -sin, 0.0)
    return c, sa, sb


ONES_ROWS = 16
LOG2E = math.log2(math.e)
FLASH_TK = 1024
FLASH_TK_DIFF = 512


def _flash_kernel(*refs, ncomp, nk, lam_init):
    if ncomp == 2:
        q_ref, k_ref, vt_ref, lq1, lk1, lq2, lk2, g_ref, o_ref, *scr = refs
    else:
        q_ref, k_ref, vt_ref, o_ref, *scr = refs
    qm_sc, m_sc, acc_sc, s0, s1, cm0, cm1, p0, p1, al0, al1 = scr
    tk = s0.shape[1]
    q = q_ref[...]
    if ncomp == 2:
        chan = lax.broadcasted_iota(jnp.int32, q.shape, 0)
        zero = jnp.zeros_like(q)
        qm_sc[0] = jnp.where(chan < DIFF_HEAD_DIM, q, zero)
        qm_sc[1] = jnp.where(chan >= DIFF_HEAD_DIM, q, zero)
    else:
        qm_sc[0] = q
    m_sc[...] = jnp.full(m_sc.shape, -jnp.inf, F32)
    acc_sc[...] = jnp.zeros(acc_sc.shape, F32)

    def scores(j, s_ref, cm_ref):
        k = k_ref[j * tk:(j + 1) * tk, :]
        for c in range(ncomp):
            s = jnp.dot(k, qm_sc[c], preferred_element_type=F32)
            s_ref[c] = s
            cm_ref[c] = jnp.max(s, axis=0, keepdims=True)

    def softmax(s_ref, cm_ref, p_ref, al_ref):
        for c in range(ncomp):
            m_old = m_sc[c]
            m_new = jnp.maximum(m_old, cm_ref[c])
            al_ref[c] = jnp.exp2(m_old - m_new)
            p_ref[c] = jnp.exp2(s_ref[c] - m_new).astype(BF16)
            m_sc[c] = m_new

    def values(j, p_ref, al_ref):
        vt = vt_ref[j]
        for c in range(ncomp):
            acc_sc[c] = al_ref[c] * acc_sc[c] + jnp.dot(vt, p_ref[c], preferred_element_type=F32)

    bufs = ((s0, cm0, p0, al0), (s1, cm1, p1, al1))
    scores(0, s0, cm0)
    for j in range(nk):
        s_c, cm_c, p_c, al_c = bufs[j % 2]
        s_n, cm_n, p_n, al_n = bufs[(j + 1) % 2]
        if j + 1 < nk:
            scores(j + 1, s_n, cm_n)
        softmax(s_c, cm_c, p_c, al_c)
        if j >= 1:
            values(j - 1, p_n, al_n)
    values(nk - 1, *bufs[(nk - 1) % 2][2:])

    def normalised(c):
        acc = acc_sc[c]
        return acc[:LANE] / acc[LANE:LANE + 1]

    o = normalised(0)
    if ncomp == 2:
        lam = (jnp.exp(jnp.sum(lq1[...] * lk1[...], keepdims=True))
               - jnp.exp(jnp.sum(lq2[...] * lk2[...], keepdims=True)) + lam_init)
        o = o - lam * normalised(1)
        o = o * lax.rsqrt(jnp.mean(o * o, axis=0, keepdims=True) + RMS_EPS) * g_ref[...] * (1.0 - lam_init)
    o_ref[...] = o.T


def _flash(q, k, vt, *, batch, seq, tq=512, diff=None, lam_init=0.0):
    nk, vrows, tk = vt.shape[2], vt.shape[3], vt.shape[4]
    assert vrows == LANE + ONES_ROWS
    tq = min(tq, seq)
    nq = seq // tq
    ncomp = 2 if diff is not None else 1
    in_specs = [
        pl.BlockSpec((None, None, LANE, tq), lambda b, h, i: (b, h, 0, i)),
        pl.BlockSpec((None, None, seq, LANE), lambda b, h, i: (b, h, 0, 0)),
        pl.BlockSpec((None, None, nk, vrows, tk), lambda b, h, i: (b, h, 0, 0, 0)),
    ]
    args = [q, k, vt]
    if diff is not None:
        lq1, lk1, lq2, lk2, subln = diff
        for v in (lq1, lk1, lq2, lk2):
            in_specs.append(pl.BlockSpec((1, DIFF_HEAD_DIM), lambda b, h, i: (0, 0)))
            args.append(v.reshape(1, DIFF_HEAD_DIM))
        in_specs.append(pl.BlockSpec((LANE, 1), lambda b, h, i: (0, 0)))
        args.append(subln.reshape(LANE, 1))
    return pl.pallas_call(
        functools.partial(_flash_kernel, ncomp=ncomp, nk=nk, lam_init=lam_init),
        grid=(batch, HEADS, nq),
        in_specs=in_specs,
        out_specs=pl.BlockSpec((tq, LANE), lambda b, h, i: (b * nq + i, h)),
        out_shape=jax.ShapeDtypeStruct((batch * seq, HEADS * LANE), F32),
        scratch_shapes=[pltpu.VMEM((ncomp, LANE, tq), BF16),
                        pltpu.VMEM((ncomp, 1, tq), F32), pltpu.VMEM((ncomp, vrows, tq), F32),
                        pltpu.VMEM((ncomp, tk, tq), F32), pltpu.VMEM((ncomp, tk, tq), F32),
                        pltpu.VMEM((ncomp, 1, tq), F32), pltpu.VMEM((ncomp, 1, tq), F32),
                        pltpu.VMEM((ncomp, tk, tq), BF16), pltpu.VMEM((ncomp, tk, tq), BF16),
                        pltpu.VMEM((ncomp, 1, tq), F32), pltpu.VMEM((ncomp, 1, tq), F32)],
        compiler_params=_cparams(("parallel", "parallel", "parallel")),
        name="flash_diff" if diff is not None else "flash_mla",
    )(*args)


def _ret_kernel(dec_ref, qf, kf, vf, qb, kb, vb, of_ref, ob_ref, s_sc, *, chunk):
    @pl.when(pl.program_id(1) == 0)
    def _():
        s_sc[...] = jnp.zeros(s_sc.shape, F32)

    ii = lax.broadcasted_iota(jnp.int32, (chunk, chunk), 0)
    jj = lax.broadcasted_iota(jnp.int32, (chunk, chunk), 1)
    r = lax.broadcasted_iota(jnp.int32, (chunk, 1), 0).astype(F32)
    for d, (q_ref, k_ref, v_ref, o_ref) in enumerate(((qf, kf, vf, of_ref), (qb, kb, vb, ob_ref))):
        for h in range(HEADS):
            la = -jnp.exp(jnp.full((1, 1), dec_ref[d, h], F32))
            if d == 0:
                mask, dist = ii >= jj, (ii - jj).astype(F32)
                qdec, kdec = jnp.exp(la * (r + 1.0)), jnp.exp(la * (chunk - 1.0 - r))
            else:
                mask, dist = jj > ii, (jj - ii).astype(F32)
                qdec, kdec = jnp.exp(la * (chunk - r)), jnp.exp(la * r)
            decay = jnp.where(mask, jnp.exp(jnp.where(mask, dist * la, 0.0)), 0.0)
            q, k, v = q_ref[h], k_ref[h], v_ref[h]
            s = lax.dot_general(q, k, (((1,), (1,)), ((), ())), preferred_element_type=F32)
            o = jnp.dot((s * decay).astype(BF16), v, preferred_element_type=F32)
            state = s_sc[d, h]
            o = o + qdec * jnp.dot(q, state.astype(BF16), preferred_element_type=F32)
            kd = (k.astype(F32) * kdec).astype(BF16)
            s_sc[d, h] = jnp.exp(la * float(chunk)) * state + lax.dot_general(
                kd, v, (((0,), (0,)), ((), ())), preferred_element_type=F32)
            o_ref[:, h * LANE:(h + 1) * LANE] = o


def _retention(q, k, v, decays, *, batch, seq, chunk=256):
    chunk = min(chunk, seq)
    n = seq // chunk
    fwd = pl.BlockSpec((None, HEADS, chunk, LANE), lambda b, c: (b, 0, c, 0))
    bwd = pl.BlockSpec((None, HEADS, chunk, LANE), lambda b, c: (b, 0, n - 1 - c, 0))
    w = HEADS * LANE
    out = jax.ShapeDtypeStruct((batch * seq, w), F32)
    return pl.pallas_call(
        functools.partial(_ret_kernel, chunk=chunk),
        grid=(batch, n),
        in_specs=[pl.BlockSpec(memory_space=pltpu.SMEM), fwd, fwd, fwd, bwd, bwd, bwd],
        out_specs=[pl.BlockSpec((chunk, w), lambda b, c: (b * n + c, 0)),
                   pl.BlockSpec((chunk, w), lambda b, c: (b * n + n - 1 - c, 0))],
        out_shape=[out, out],
        scratch_shapes=[pltpu.VMEM((2, HEADS, LANE, LANE), F32)],
        compiler_params=_cparams(("parallel", "arbitrary")),
        name="retention",
    )(decays, q, k, v, q, k, v)


GLA_SUB = 8


def _split3(x):
    x1 = x.astype(BF16)
    r1 = x - x1.astype(F32)
    x2 = r1.astype(BF16)
    x3 = (r1 - x2.astype(F32)).astype(BF16)
    return x1, x2, x3


def _gla_direction(q, k, v, lr, w2, bias, st, reverse):
    C, wk = q.shape
    wv = v.shape[1]
    dk, dv = wk // HEADS, wv // HEADS
    z = jnp.dot(lr.astype(BF16), w2, preferred_element_type=F32) + bias
    g = (jnp.minimum(z, 0.0) - jnp.log(1.0 + jnp.exp(-jnp.abs(z)))) * (LOG2E / GLA_TAU)
    ii = lax.broadcasted_iota(jnp.int32, (C, C), 0)
    jj = lax.broadcasted_iota(jnp.int32, (C, C), 1)
    tri = jnp.where(ii >= jj, 1.0, 0.0).astype(BF16)
    b = sum(jnp.dot(tri, part, preferred_element_type=F32) for part in _split3(g))
    tot = b[C - 1:C, :]
    c = (tot - b + g) if reverse else b

    qe = (q * jnp.exp2(jnp.minimum(c, 0.0))).astype(BF16)
    o = lax.dot_general(qe, st.astype(BF16), (((1,), (1,)), ((), ())), preferred_element_type=F32)
    ke = (k * jnp.exp2(jnp.minimum(tot - c, 0.0))).astype(BF16)
    upd = lax.dot_general(v.astype(BF16), ke, (((0,), (0,)), ((), ())), preferred_element_type=F32)
    rr = _div_pow2(lax.broadcasted_iota(jnp.int32, (wv, wk), 0), dv)
    cc = _div_pow2(lax.broadcasted_iota(jnp.int32, (wv, wk), 1), dk)
    new_st = jnp.where(rr == cc, st * jnp.exp2(tot) + upd, 0.0)

    lane_head = _div_pow2(lax.broadcasted_iota(jnp.int32, (C, wk), 1), dk)
    scores = [jnp.zeros((C, C), F32) for _ in range(HEADS)]
    hsz = C // 2
    while hsz >= GLA_SUB:
        blk = 2 * hsz
        rows = []
        for m in range(C // blk):
            rrow = m * blk + (hsz if reverse else hsz - 1)
            rows.append(jnp.broadcast_to(c[rrow:rrow + 1, :], (blk, wk)))
        ref = jnp.concatenate(rows, axis=0) if len(rows) > 1 else rows[0]
        qt = q * jnp.exp2(jnp.minimum(c - ref, 0.0))
        kt = (k * jnp.exp2(jnp.minimum(ref - c, 0.0))).astype(BF16)
        same = _div_pow2(ii, blk) == _div_pow2(jj, blk)
        if reverse:
            lvl = same & (_mod_pow2(ii, blk) < hsz) & (_mod_pow2(jj, blk) >= hsz)
        else:
            lvl = same & (_mod_pow2(ii, blk) >= hsz) & (_mod_pow2(jj, blk) < hsz)
        for h in range(HEADS):
            qh = jnp.where(lane_head == h, qt, 0.0).astype(BF16)
            s = lax.dot_general(qh, kt, (((1,), (1,)), ((), ())), preferred_element_type=F32)
            scores[h] = scores[h] + jnp.where(lvl, s, 0.0)
        hsz //= 2

    assert dv == C
    er = _div_pow2(lax.broadcasted_iota(jnp.int32, (wk, wv), 0), dk)
    ec = _div_pow2(lax.broadcasted_iota(jnp.int32, (wk, wv), 1), dv)
    expand = jnp.where(er == ec, 1.0, 0.0).astype(BF16)
    dist = (jj - ii) if reverse else (ii - jj)
    same_sub = _div_pow2(ii, GLA_SUB) == _div_pow2(jj, GLA_SUB)
    for lag in range(1 if reverse else 0, GLA_SUB):
        if lag == 0:
            t = q * k
        else:
            shift = (GLA_SUB - lag) if reverse else lag
            ks = pltpu.roll(k.reshape(C // GLA_SUB, GLA_SUB, wk), shift, axis=1).reshape(C, wk)
            cs = pltpu.roll(c.reshape(C // GLA_SUB, GLA_SUB, wk), shift, axis=1).reshape(C, wk)
            t = q * ks * jnp.exp2(jnp.minimum(c - cs, 0.0))
        red = jnp.dot(t.astype(BF16), expand, preferred_element_type=F32)
        on_diag = same_sub & (dist == lag)
        for h in range(HEADS):
            scores[h] = scores[h] + jnp.where(on_diag, red[:, h * dv:(h + 1) * dv], 0.0)

    vb = v.astype(BF16)
    o = o + jnp.concatenate(
        [jnp.dot(scores[h].astype(BF16), vb[:, h * dv:(h + 1) * dv], preferred_element_type=F32) for h in range(HEADS)],
        axis=1)
    return o, new_st


def _gla_kernel(qf, kf, vf, lf, qb, kb, vb, lb, w2f, bf, w2b, bb, of_ref, ob_ref, s_sc, *, qscale):
    @pl.when(pl.program_id(1) == 0)
    def _():
        s_sc[...] = jnp.zeros(s_sc.shape, F32)

    o, st = _gla_direction(qf[...] * qscale, kf[...], vf[...], lf[...], w2f[...], bf[...], s_sc[0], False)
    of_ref[...] = o
    s_sc[0] = st
    o, st = _gla_direction(qb[...] * qscale, kb[...], vb[...], lb[...], w2b[...], bb[...], s_sc[1], True)
    ob_ref[...] = o
    s_sc[1] = st


def _gla(z, *, qcol, kcol, vcol, lcol, w2f, bf, w2b, bb, batch, seq, chunk=128):
    chunk = min(chunk, seq)
    n = seq // chunk
    wk, wv = HEADS * GLA_K_DIM, HEADS * LANE

    def specs(cmap):
        return [pl.BlockSpec((chunk, wk), lambda b, c: (cmap(b, c), qcol)),
                pl.BlockSpec((chunk, wk), lambda b, c: (cmap(b, c), kcol)),
                pl.BlockSpec((chunk, wv), lambda b, c: (cmap(b, c), vcol)),
                pl.BlockSpec((chunk, LANE), lambda b, c: (cmap(b, c), lcol))]

    fmap = lambda b, c: b * n + c
    bmap = lambda b, c: b * n + n - 1 - c
    wspec = [pl.BlockSpec((LANE, wk), lambda b, c: (0, 0)), pl.BlockSpec((1, wk), lambda b, c: (0, 0))]
    out = jax.ShapeDtypeStruct((batch * seq, wv), F32)
    return pl.pallas_call(
        functools.partial(_gla_kernel, qscale=GLA_K_DIM ** -0.5),
        grid=(batch, n),
        in_specs=specs(fmap) + specs(bmap) + wspec + wspec,
        out_specs=[pl.BlockSpec((chunk, wv), lambda b, c: (fmap(b, c), 0)),
                   pl.BlockSpec((chunk, wv), lambda b, c: (bmap(b, c), 0))],
        out_shape=[out, out],
        scratch_shapes=[pltpu.VMEM((2, wv, wk), F32)],
        compiler_params=_cparams(("parallel", "arbitrary")),
        name="gla",
    )(z, z, z, z, z, z, z, z, w2f, bf, w2b, bb)


def _layer_norm_rows(r, g, b):
    mu = jnp.mean(r, axis=-1, keepdims=True)
    d = r - mu
    var = jnp.mean(d * d, axis=-1, keepdims=True)
    return d * lax.rsqrt(var + LN_EPS) * g + b


def _outproj_kernel(x_ref, fin_ref, of_ref, ob_ref, gate_ref, ng_ref, wa_ref, wb_ref, lg_ref, lb_ref, rw_ref, rb_ref,
                    o_ref, ids_ref, gw_ref, cnt_ref, run_sc, *, group_norm):
    lin = of_ref[...] + ob_ref[...]
    parts = []
    for h in range(HEADS):
        zh = lin[:, h * LANE:(h + 1) * LANE]
        if group_norm:
            mu = jnp.mean(zh, axis=-1, keepdims=True)
            dz = zh - mu
            parts.append(dz * lax.rsqrt(jnp.mean(dz * dz, axis=-1, keepdims=True) + LN_EPS))
        else:
            parts.append(zh * lax.rsqrt(jnp.mean(zh * zh, axis=-1, keepdims=True) + RMS_EPS) * ng_ref[...])
    gate = gate_ref[...]
    lin = jnp.concatenate(parts, axis=1) * (gate * jax.nn.sigmoid(gate))
    y = (jnp.dot(fin_ref[...].astype(BF16), wa_ref[...], preferred_element_type=F32)
         + jnp.dot(lin.astype(BF16), wb_ref[...], preferred_element_type=F32))
    x1 = _layer_norm_rows(ALPHA * x_ref[...] + y, lg_ref[...], lb_ref[...])
    o_ref[...] = x1
    _route_rows(x1, rw_ref, rb_ref, ids_ref, gw_ref, cnt_ref, run_sc)


def _outproj(x, fin, of, ob, gate_src, gate_col, norm_gain, wa, wb, ln_g, ln_b, route_w, route_b, *, group_norm, tm=512):
    n, d = x.shape
    w = HEADS * LANE
    tm = min(tm, n)
    row = lambda i: (i, 0)
    const = lambda i: (0, 0)
    rw_hi = route_w.astype(BF16)
    rw = jnp.stack([rw_hi, (route_w - rw_hi.astype(F32)).astype(BF16)])
    return pl.pallas_call(
        functools.partial(_outproj_kernel, group_norm=group_norm),
        grid=(n // tm,),
        in_specs=[pl.BlockSpec((tm, d), row), pl.BlockSpec((tm, w), row), pl.BlockSpec((tm, w), row),
                  pl.BlockSpec((tm, w), row), pl.BlockSpec((tm, w), lambda i: (i, gate_col)),
                  pl.BlockSpec((1, LANE), const), pl.BlockSpec((w, d), const), pl.BlockSpec((w, d), const),
                  pl.BlockSpec((1, d), const), pl.BlockSpec((1, d), const),
                  pl.BlockSpec((2, d, LANE), lambda i: (0, 0, 0)), pl.BlockSpec((1, LANE), const)],
        out_specs=[pl.BlockSpec((tm, d), row), pl.BlockSpec((tm, LANE), row), pl.BlockSpec((tm, LANE), row),
                   pl.BlockSpec((1, LANE), const)],
        out_shape=[jax.ShapeDtypeStruct((n, d), F32), jax.ShapeDtypeStruct((n, LANE), jnp.int32),
                   jax.ShapeDtypeStruct((n, LANE), F32), jax.ShapeDtypeStruct((1, LANE), jnp.int32)],
        scratch_shapes=[pltpu.VMEM((1, LANE), F32)],
        compiler_params=_cparams(("arbitrary",)),
        name="outproj",
    )(x, fin, of, ob, gate_src, norm_gain.reshape(1, LANE), wa, wb, ln_g.reshape(1, d), ln_b.reshape(1, d), rw, route_b)


def _route_rows(x, w_ref, b_ref, ids_ref, gw_ref, cnt_ref, run_sc):
    @pl.when(pl.program_id(0) == 0)
    def _():
        run_sc[...] = jnp.zeros(run_sc.shape, F32)

    tm = x.shape[0]
    xh = x.astype(BF16)
    xl = (x - xh.astype(F32)).astype(BF16)
    wh, wl = w_ref[0], w_ref[1]
    logits = (jnp.dot(xh, wh, preferred_element_type=F32) + jnp.dot(xh, wl, preferred_element_type=F32)
              + jnp.dot(xl, wh, preferred_element_type=F32)) + b_ref[...]
    lane = lax.broadcasted_iota(jnp.int32, logits.shape, 1)
    neg = -jnp.inf
    gmask = (lane >= N_EXPERTS) & (lane < N_EXPERTS + N_GROUPS)
    gl = jnp.where(gmask, logits, neg)
    gmax = jnp.max(gl, axis=1, keepdims=True)
    lane_f = lane.astype(F32)
    first = lambda hit: jnp.min(jnp.where(hit, lane_f, float(LANE)), axis=1, keepdims=True).astype(jnp.int32)
    gidx = first(gl == gmax) - N_EXPERTS
    p_grp = 1.0 / jnp.sum(jnp.where(gmask, jnp.exp(gl - gmax), 0.0), axis=1, keepdims=True)
    el = jnp.where(_div_pow2(lane, EXPERTS_PER_GROUP) == gidx, logits, neg)
    l1 = jnp.max(el, axis=1, keepdims=True)
    e1 = first(el == l1)
    el2 = jnp.where(lane == e1, neg, el)
    l2 = jnp.max(el2, axis=1, keepdims=True)
    e2 = first(el2 == l2)
    t = jnp.exp(l2 - l1)
    w1 = p_grp / (1.0 + t)
    w2 = p_grp * t / (1.0 + t)

    onehot = jnp.where(lane == e1, 1.0, jnp.where(lane == e2, 1.0, 0.0))
    ri = lax.broadcasted_iota(jnp.int32, (tm, tm), 0)
    ci = lax.broadcasted_iota(jnp.int32, (tm, tm), 1)
    before = jnp.dot(jnp.where(ri > ci, 1.0, 0.0).astype(BF16), onehot.astype(BF16), preferred_element_type=F32)
    before = before + run_sc[...]
    r1 = jnp.sum(jnp.where(lane == e1, before, 0.0), axis=1, keepdims=True).astype(jnp.int32)
    r2 = jnp.sum(jnp.where(lane == e2, before, 0.0), axis=1, keepdims=True).astype(jnp.int32)
    run_sc[...] = run_sc[...] + jnp.sum(onehot, axis=0, keepdims=True)
    cnt_ref[...] = run_sc[...].astype(jnp.int32)
    ids_ref[...] = jnp.where(lane == 0, e1, jnp.where(lane == 1, e2, jnp.where(lane == 2, r1, jnp.where(lane == 3, r2, 0))))
    gw_ref[...] = jnp.where(lane == 0, w1, jnp.where(lane == 1, w2, 0.0))


MOE_BM = 256
SUBL = 8


def _rows_from_linear(ref, rows):
    return jnp.concatenate([ref[pl.ds(s, rows, stride=SUBL), :] for s in range(SUBL)], axis=1)


def _rows_to_linear(ref, val):
    for s in range(SUBL):
        ref[pl.ds(s, val.shape[0], stride=SUBL), :] = val[:, s * LANE:(s + 1) * LANE]


def _dispatch_kernel(pend_ref, padded_ref, dest_ref, x_ref, xs_hbm, idx_smem, zbuf, lin, sem_i, sem_z, sem):
    i = pl.program_id(0)
    tm = x_ref.shape[0]
    bm = zbuf.shape[0] // SUBL

    @pl.when(i == 0)
    def _():
        zbuf[...] = jnp.zeros(zbuf.shape, F32)

        def tail(e):
            start_row = pl.multiple_of((pend_ref[e] - bm) * SUBL, bm * SUBL)
            return pltpu.make_async_copy(zbuf, xs_hbm.at[pl.ds(start_row, bm * SUBL), :], sem_z)

        def start(e, carry):
            @pl.when(padded_ref[e] > 0)
            def _():
                tail(e).start()
            return carry

        def wait(e, carry):
            @pl.when(padded_ref[e] > 0)
            def _():
                tail(e).wait()
            return carry

        lax.fori_loop(0, N_EXPERTS, start, 0)
        lax.fori_loop(0, N_EXPERTS, wait, 0)

        def unused(b):
            start_row = pl.multiple_of(b * bm * SUBL, bm * SUBL)
            return pltpu.make_async_copy(zbuf, xs_hbm.at[pl.ds(start_row, bm * SUBL), :], sem_z)

        first_unused = pend_ref[N_EXPERTS - 1] // bm
        n_blocks = xs_hbm.shape[0] // (bm * SUBL)
        lax.fori_loop(first_unused, n_blocks, lambda b, c: (unused(b).start(), c)[1], 0)
        lax.fori_loop(first_unused, n_blocks, lambda b, c: (unused(b).wait(), c)[1], 0)

    cp = pltpu.make_async_copy(dest_ref.at[i], idx_smem, sem_i)
    cp.start()
    _rows_to_linear(lin, x_ref[...])
    cp.wait()

    def scatter(r, carry):
        src = lin.at[pl.ds(pl.multiple_of(r * SUBL, SUBL), SUBL), :]
        for k in range(2):
            dst = pl.multiple_of(idx_smem[2 * r + k] * SUBL, SUBL)
            pltpu.make_async_copy(src, xs_hbm.at[pl.ds(dst, SUBL), :], sem).start(priority=k)
        return carry

    lax.fori_loop(0, tm, scatter, 0, unroll=8)
    for k in range(2):
        pltpu.make_async_copy(lin, xs_hbm.at[pl.ds(0, tm * SUBL), :], sem).wait()


def _dispatch(x, dest, pend, padded, cap, *, tm=512):
    n, d = x.shape
    assert d == SUBL * LANE
    tm = min(tm, n)
    nt = n // tm
    grid_spec = pltpu.PrefetchScalarGridSpec(
        num_scalar_prefetch=2,
        grid=(nt,),
        in_specs=[pl.BlockSpec((nt, 2 * tm), lambda i, pe, pa: (0, 0)),
                  pl.BlockSpec((tm, d), lambda i, pe, pa: (i, 0))],
        out_specs=pl.BlockSpec(memory_space=pl.ANY),
        scratch_shapes=[pltpu.SMEM((2 * tm,), jnp.int32), pltpu.VMEM((MOE_BM * SUBL, LANE), F32),
                        pltpu.VMEM((tm * SUBL, LANE), F32),
                        pltpu.SemaphoreType.DMA(()), pltpu.SemaphoreType.DMA(()), pltpu.SemaphoreType.DMA(())],
    )
    return pl.pallas_call(
        _dispatch_kernel,
        grid_spec=grid_spec,
        out_shape=jax.ShapeDtypeStruct((cap * SUBL, LANE), F32),
        compiler_params=_cparams(("arbitrary",)),
        name="dispatch",
    )(pend, padded, dest.reshape(nt, 2 * tm), x)


def _experts_kernel(blk_e_ref, nused_ref, xs_ref, wg_ref, wu_ref, wd_ref, ys_ref, wgb, wub, wdb, hbuf):
    i = pl.program_id(0)
    nb = pl.num_programs(0) - 1
    cur = jnp.minimum(i, nb - 1)
    prv = jnp.maximum(i - 1, 0)

    @pl.when(i == 0)
    def _():
        hbuf[...] = jnp.zeros(hbuf.shape, BF16)

    @pl.when(jnp.logical_or(i == 0, blk_e_ref[cur] != blk_e_ref[jnp.minimum(prv, nb - 1)]))
    def _():
        wgb[...] = wg_ref[...].astype(BF16)
        wub[...] = wu_ref[...].astype(BF16)

    @pl.when(jnp.logical_or(i == 0, blk_e_ref[prv] != blk_e_ref[jnp.maximum(i - 2, 0)]))
    def _():
        wdb[...] = wd_ref[...].astype(BF16)

    slot = lax.rem(i, 2)
    y = jnp.dot(hbuf[1 - slot], wdb[...], preferred_element_type=F32)
    live = jnp.logical_and(i >= 1, i - 1 < nused_ref[0])
    _rows_to_linear(ys_ref, jnp.where(live, y, 0.0))
    xb = _rows_from_linear(xs_ref, xs_ref.shape[0] // SUBL).astype(BF16)
    hg = jnp.dot(xb, wgb[...], preferred_element_type=F32)
    hu = jnp.dot(xb, wub[...], preferred_element_type=F32)
    hbuf[slot] = (hg * jax.nn.sigmoid(hg) * hu).astype(BF16)


def _experts(xs, blk_e, n_used, w_gate, w_up, w_down, layer):
    cap = xs.shape[0] // SUBL
    bm = MOE_BM
    nb = cap // bm
    d, de = w_gate.shape[2], w_gate.shape[3]
    row_in = lambda i, be, nu: (jnp.minimum(i, nu[0] - 1), 0)
    row = lambda i, be, nu: (jnp.maximum(i - 1, 0), 0)
    w_cur = lambda i, be, nu: (layer, be[jnp.minimum(i, nb - 1)], 0, 0)
    w_prv = lambda i, be, nu: (layer, be[jnp.maximum(i - 1, 0)], 0, 0)
    grid_spec = pltpu.PrefetchScalarGridSpec(
        num_scalar_prefetch=2,
        grid=(nb + 1,),
        in_specs=[pl.BlockSpec((bm * SUBL, LANE), row_in),
                  pl.BlockSpec((None, None, d, de), w_cur),
                  pl.BlockSpec((None, None, d, de), w_cur),
                  pl.BlockSpec((None, None, de, d), w_prv)],
        out_specs=pl.BlockSpec((bm * SUBL, LANE), row),
        scratch_shapes=[pltpu.VMEM((d, de), BF16), pltpu.VMEM((d, de), BF16), pltpu.VMEM((de, d), BF16),
                        pltpu.VMEM((2, bm, de), BF16)],
    )
    return pl.pallas_call(
        _experts_kernel,
        grid_spec=grid_spec,
        out_shape=jax.ShapeDtypeStruct((cap * SUBL, LANE), F32),
        compiler_params=_cparams(("arbitrary",)),
        name="experts",
    )(blk_e, n_used, xs, w_gate, w_up, w_down)


def _combine_kernel(dest_ref, x_ref, gw_ref, g_ref, b_ref, ys_hbm, o_ref, idx_smem, ybuf, sem_i, sem):
    i = pl.program_id(0)
    tm = x_ref.shape[0]

    def issue(tile, slot):
        cp = pltpu.make_async_copy(dest_ref.at[tile], idx_smem, sem_i)
        cp.start()
        cp.wait()

        def gather(r, carry):
            row = pl.multiple_of(r * SUBL, SUBL)
            for k in range(2):
                src = pl.multiple_of(idx_smem[2 * r + k] * SUBL, SUBL)
                pltpu.make_async_copy(ys_hbm.at[pl.ds(src, SUBL), :], ybuf.at[slot, k, pl.ds(row, SUBL), :],
                                      sem.at[slot]).start(priority=k)
            return carry

        lax.fori_loop(0, tm, gather, 0, unroll=8)

    slot = lax.rem(i, 2)

    @pl.when(i == 0)
    def _():
        issue(0, 0)

    @pl.when(i + 1 < pl.num_programs(0))
    def _():
        issue(i + 1, 1 - slot)

    for k in range(2):
        pltpu.make_async_copy(ys_hbm.at[pl.ds(0, tm * SUBL), :], ybuf.at[slot, k], sem.at[slot]).wait()
    gw = gw_ref[...]
    ffn = (_rows_from_linear(ybuf.at[slot, 0], tm) * gw[:, 0:1] + _rows_from_linear(ybuf.at[slot, 1], tm) * gw[:, 1:2])
    o_ref[...] = _layer_norm_rows(ALPHA * x_ref[...] + ffn, g_ref[...], b_ref[...])


def _combine(x, ys, dest, gw, ln_g, ln_b, *, tm=512):
    n, d = x.shape
    tm = min(tm, n)
    nt = n // tm
    return pl.pallas_call(
        _combine_kernel,
        grid=(nt,),
        in_specs=[pl.BlockSpec((nt, 2 * tm), lambda i: (0, 0)),
                  pl.BlockSpec((tm, d), lambda i: (i, 0)), pl.BlockSpec((tm, LANE), lambda i: (i, 0)),
                  pl.BlockSpec((1, d), lambda i: (0, 0)), pl.BlockSpec((1, d), lambda i: (0, 0)),
                  pl.BlockSpec(memory_space=pl.ANY)],
        out_specs=pl.BlockSpec((tm, d), lambda i: (i, 0)),
        out_shape=jax.ShapeDtypeStruct((n, d), F32),
        scratch_shapes=[pltpu.SMEM((2 * tm,), jnp.int32), pltpu.VMEM((2, 2, tm * SUBL, LANE), F32),
                        pltpu.SemaphoreType.DMA(()), pltpu.SemaphoreType.DMA((2,))],
        compiler_params=_cparams(("arbitrary",)),
        name="combine",
    )(dest.reshape(nt, 2 * tm), x, gw, ln_g.reshape(1, d), ln_b.reshape(1, d), ys)


def _router_params(w_grp, b_grp, w_exp, b_exp):
    d = w_exp.shape[0]
    wr = jnp.zeros((d, LANE), F32).at[:, :N_EXPERTS].set(w_exp).at[:, N_EXPERTS:N_EXPERTS + N_GROUPS].set(w_grp)
    br = jnp.zeros((1, LANE), F32).at[0, :N_EXPERTS].set(b_exp).at[0, N_EXPERTS:N_EXPERTS + N_GROUPS].set(b_grp)
    return wr, br


def _moe(x, routing, w_gate, w_up, w_down, layer, ln_g, ln_b):
    n, d = x.shape
    ids, gw, cnt = routing
    bm = MOE_BM
    counts = cnt[0, :N_EXPERTS]
    padded = (counts + bm - 1) // bm * bm
    pend = jnp.cumsum(padded)
    pstart = pend - padded
    e, r = ids[:, 0:2], ids[:, 2:4]
    onehot = e[:, :, None] == jnp.arange(N_EXPERTS, dtype=jnp.int32)[None, None, :]
    dest = jnp.sum(jnp.where(onehot, pstart[None, None, :], 0), axis=-1) + r
    cap = 2 * n + N_EXPERTS * bm
    nb = cap // bm
    blk_start = jnp.arange(nb, dtype=jnp.int32) * bm
    blk_e = jnp.minimum(jnp.sum((pend[None, :] <= blk_start[:, None]).astype(jnp.int32), axis=1), N_EXPERTS - 1)
    n_used = (pend[-1:] // bm).astype(jnp.int32)
    xs = _dispatch(x, dest, pend.astype(jnp.int32), padded.astype(jnp.int32), cap)
    ys = _experts(xs, blk_e, n_used, w_gate, w_up, w_down, layer)
    return _combine(x, ys, dest, gw, ln_g, ln_b)


def _even_layer(x, batch, seq, layer_idx, w_in, dec_f, dec_b, lq1, lk1, lq2, lk2, subln, w_out, ln_g, ln_b, route):
    d = x.shape[1]
    w = HEADS * LANE
    kw = dict(batch=batch, seq=seq)
    diff_seg = [(0, DIFF_ROT_DIM), (DIFF_HEAD_DIM, DIFF_ROT_DIM)]
    q, k, v, gate, dq, dk, dvt = _even_in(
        x, w_in.astype(BF16), _rope_tables(seq, [(0, LANE)], RET_THETA),
        _rope_tables(seq, diff_seg, ROPE_THETA), **kw)
    decays = jnp.stack([dec_f, dec_b]).astype(F32)
    of, ob = _retention(q, k, v, decays, **kw)
    lam_init = 0.8 - 0.6 * math.exp(-0.3 * layer_idx)
    diff = _flash(dq, dk, dvt, diff=(lq1, lk1, lq2, lk2, subln), lam_init=lam_init, tq=1024, **kw)
    wo = w_out.astype(BF16)
    return _outproj(x, diff, of, ob, gate, 0, jnp.ones((LANE,), F32), wo[w:], wo[:w], ln_g, ln_b, *route,
                    group_norm=True)


def _odd_layer(x, batch, seq, w_in, q_norm, w_uq, kv_norm, w_ukv, w2_f, b_f, w2_b, b_b, gla_norm, w_out, ln_g, ln_b,
               route):
    d = x.shape[1]
    w = HEADS * LANE
    o = np.cumsum([0, MLA_Q_RANK, MLA_KV_RANK, MLA_ROPE, HEADS * GLA_K_DIM, HEADS * GLA_K_DIM, w, w,
                   GLA_GATE_RANK, GLA_GATE_RANK]).tolist()
    zeros = lambda c: jnp.zeros((d, c), F32)
    w_in2 = jnp.concatenate([
        w_in[:, o[0]:o[2]], zeros(MLA_NOPE), w_in[:, o[2]:o[3]], zeros(LANE - MLA_NOPE - MLA_ROPE),
        w_in[:, o[3]:o[7]], w_in[:, o[7]:o[9]], zeros(LANE - 2 * GLA_GATE_RANK)], axis=1).astype(BF16)
    kw = dict(batch=batch, seq=seq)
    qd = MLA_NOPE + MLA_ROPE
    w_uq2 = jnp.pad(w_uq.reshape(MLA_Q_RANK, HEADS, qd), ((0, 0), (0, 0), (0, LANE - qd))).reshape(MLA_Q_RANK, w)
    ukv = w_ukv.reshape(MLA_KV_RANK, HEADS, MLA_NOPE + MLA_V)
    w_uk2 = jnp.pad(ukv[:, :, :MLA_NOPE], ((0, 0), (0, 0), (0, LANE - MLA_NOPE))).reshape(MLA_KV_RANK, w)
    w_uv2 = ukv[:, :, MLA_NOPE:].reshape(MLA_KV_RANK, w)
    q, k, vt, zg = _odd_in(x, w_in2, q_norm, w_uq2.astype(BF16), kv_norm,
                           jnp.concatenate([w_uk2, w_uv2], axis=1).astype(BF16),
                           _rope_tables(seq, [(MLA_NOPE, MLA_ROPE)], ROPE_THETA), **kw)
    mla = _flash(q, k, vt, **kw)
    wk = HEADS * GLA_K_DIM
    pad_rows = lambda m, r0: jnp.zeros((LANE, wk), F32).at[r0:r0 + GLA_GATE_RANK].set(m).astype(BF16)
    of, ob = _gla(zg, qcol=0, kcol=1, vcol=1, lcol=12,
                  w2f=pad_rows(w2_f, 0), bf=b_f.reshape(1, wk), w2b=pad_rows(w2_b, GLA_GATE_RANK), bb=b_b.reshape(1, wk), **kw)
    wo = w_out.astype(BF16)
    return _outproj(x, mla, of, ob, zg, 2, gla_norm, wo[:w], wo[w:], ln_g, ln_b, *route, group_norm=False)


def kernel(x, ev_w_in, ev_ret_decay_f, ev_ret_decay_b, ev_lq1, ev_lk1, ev_lq2, ev_lk2, ev_subln, ev_w_out, od_w_in, od_q_norm, od_w_uq, od_kv_norm, od_w_ukv, od_gla_w2_f, od_gla_b_f, od_gla_w2_b, od_gla_b_b, od_gla_norm, od_w_out, ln1_g, ln1_b, ln2_g, ln2_b, moe_w_grp, moe_b_grp, moe_w_exp, moe_b_exp, moe_w_gate, moe_w_up, moe_w_down):
    batch, seq, d = x.shape
    h = x.reshape(batch * seq, d)
    for i in range(DEPTH):
        j = i // 2
        route = _router_params(moe_w_grp[i], moe_b_grp[i], moe_w_exp[i], moe_b_exp[i])
        if i % 2 == 0:
            h, *routing = _even_layer(h, batch, seq, i, ev_w_in[j], ev_ret_decay_f[j], ev_ret_decay_b[j], ev_lq1[j],
                                      ev_lk1[j], ev_lq2[j], ev_lk2[j], ev_subln[j], ev_w_out[j], ln1_g[i], ln1_b[i], route)
        else:
            h, *routing = _odd_layer(h, batch, seq, od_w_in[j], od_q_norm[j], od_w_uq[j], od_kv_norm[j], od_w_ukv[j],
                                     od_gla_w2_f[j], od_gla_b_f[j], od_gla_w2_b[j], od_gla_b_b[j], od_gla_norm[j],
                                     od_w_out[j], ln1_g[i], ln1_b[i], route)
        h = _moe(h, routing, moe_w_gate, moe_w_up, moe_w_down, i, ln2_g[i], ln2_b[i])
    return h.reshape(batch, seq, d)
```

```python
import functools
import math

import numpy as np
import jax
import jax.numpy as jnp
from jax import lax
from jax.experimental import pallas as pl
from jax.experimental.pallas import tpu as pltpu

F32 = jnp.float32
BF16 = jnp.bfloat16

HEADS = 4
LANE = 128
RET_THETA = 10000.0
ROPE_THETA = 500000.0
DIFF_HEAD_DIM = 64
DIFF_ROT_DIM = 16
MLA_Q_RANK = 256
MLA_KV_RANK = 128
MLA_NOPE = 64
MLA_ROPE = 32
MLA_V = 128
GLA_K_DIM = 64
GLA_GATE_RANK = 16
GLA_TAU = 16.0
N_GROUPS = 4
EXPERTS_PER_GROUP = 8
N_EXPERTS = N_GROUPS * EXPERTS_PER_GROUP
DEPTH = 2
ALPHA = (2.0 * DEPTH) ** 0.25
LN_EPS = 1e-5
RMS_EPS = 1e-6

VMEM_LIMIT = 48 * 1024 * 1024


def _div_pow2(x, n):
    return lax.shift_right_logical(x, int(n).bit_length() - 1)


def _mod_pow2(x, n):
    return lax.bitwise_and(x, int(n) - 1)


def _cparams(sem):
    return pltpu.CompilerParams(dimension_semantics=sem, vmem_limit_bytes=VMEM_LIMIT)


def _rope_heads(z, tabs, sh, scale):
    c, sa, sb = tabs
    outs = []
    for h in range(HEADS):
        zh = z[:, h * LANE:(h + 1) * LANE]
        outs.append((zh * c + pltpu.roll(zh, sh, axis=1) * sa + pltpu.roll(zh, LANE - sh, axis=1) * sb) * scale)
    return outs


def _even_in_kernel(x_ref, w_ref, rc, rsa, rsb, dc, dsa, dsb, q_ref, k_ref, v_ref, g_ref, dq_ref, dk_ref, dvt_ref):
    w = HEADS * LANE
    xb = x_ref[...].astype(BF16)
    part = lambda t: jnp.dot(xb, w_ref[:, t * w:(t + 1) * w], preferred_element_type=F32)
    ret_t = (rc[...], rsa[...], rsb[...])
    diff_t = (dc[...], dsa[...], dsb[...])
    for h, o in enumerate(_rope_heads(part(0), ret_t, LANE // 2, 1.0)):
        q_ref[h] = o.astype(BF16)
    for h, o in enumerate(_rope_heads(part(1), ret_t, LANE // 2, LANE ** -0.5)):
        k_ref[h] = o.astype(BF16)
    rv = part(2)
    for h in range(HEADS):
        v_ref[h] = rv[:, h * LANE:(h + 1) * LANE].astype(BF16)
    g_ref[...] = part(3)
    for h, o in enumerate(_rope_heads(part(4), diff_t, DIFF_ROT_DIM // 2, DIFF_HEAD_DIM ** -0.5 * LOG2E)):
        dq_ref[h] = o.T.astype(BF16)
    for h, o in enumerate(_rope_heads(part(5), diff_t, DIFF_ROT_DIM // 2, 1.0)):
        dk_ref[h] = o.astype(BF16)
    dv = part(6)
    for h in range(HEADS):
        dvt_ref[h, :LANE, :] = dv[:, h * LANE:(h + 1) * LANE].T.astype(BF16)
        dvt_ref[h, LANE:, :] = jnp.ones((ONES_ROWS, dvt_ref.shape[2]), BF16)


def _even_in(x, w, ret_tabs, diff_tabs, *, batch, seq):
    n, d = x.shape
    tm = min(FLASH_TK_DIFF, seq // 2)
    nt = seq // tm
    hw = HEADS * LANE
    heads = jax.ShapeDtypeStruct((batch, HEADS, seq, LANE), BF16)
    head_spec = pl.BlockSpec((None, HEADS, tm, LANE), lambda i: (i // nt, 0, i % nt, 0))
    heads_t = jax.ShapeDtypeStruct((batch, HEADS, LANE, seq), BF16)
    head_t_spec = pl.BlockSpec((None, HEADS, LANE, tm), lambda i: (i // nt, 0, 0, i % nt))
    tab_spec = pl.BlockSpec((tm, LANE), lambda i: (i % nt, 0))
    return pl.pallas_call(
        _even_in_kernel,
        grid=(n // tm,),
        in_specs=[pl.BlockSpec((tm, d), lambda i: (i, 0)), pl.BlockSpec((d, 7 * hw), lambda i: (0, 0))] + [tab_spec] * 6,
        out_specs=[head_spec, head_spec, head_spec, pl.BlockSpec((tm, hw), lambda i: (i, 0)), head_t_spec, head_spec,
                   pl.BlockSpec((None, HEADS, None, LANE + ONES_ROWS, tm), lambda i: (i // nt, 0, i % nt, 0, 0))],
        out_shape=[heads, heads, heads, jax.ShapeDtypeStruct((n, hw), F32), heads_t, heads,
                   jax.ShapeDtypeStruct((batch, HEADS, nt, LANE + ONES_ROWS, tm), BF16)],
        compiler_params=_cparams(("parallel",)),
        name="even_in",
    )(x, w, *ret_tabs, *diff_tabs)


def _rms_rows(z, g):
    return z * lax.rsqrt(jnp.mean(z * z, axis=-1, keepdims=True) + RMS_EPS) * g


def _odd_in_kernel(x_ref, w_ref, qn_ref, wq_ref, kvn_ref, wkv_ref, tc, tsa, tsb, q_ref, k_ref, vt_ref, zg_ref):
    hw = HEADS * LANE
    mla_w = MLA_Q_RANK + MLA_KV_RANK + LANE
    xb = x_ref[...].astype(BF16)
    zg_ref[...] = jnp.dot(xb, w_ref[:, mla_w:], preferred_element_type=F32)
    z1 = jnp.dot(xb, w_ref[:, :mla_w], preferred_element_type=F32)
    tabs = (tc[...], tsa[...], tsb[...])
    sh = MLA_ROPE // 2
    qh = jnp.dot(_rms_rows(z1[:, :MLA_Q_RANK], qn_ref[...]).astype(BF16), wq_ref[...], preferred_element_type=F32)
    for h, o in enumerate(_rope_heads(qh, tabs, sh, (MLA_NOPE + MLA_ROPE) ** -0.5 * LOG2E)):
        q_ref[h] = o.T.astype(BF16)
    ckv = _rms_rows(z1[:, MLA_Q_RANK:MLA_Q_RANK + MLA_KV_RANK], kvn_ref[...]).astype(BF16)
    kv = jnp.dot(ckv, wkv_ref[...], preferred_element_type=F32)
    kr = z1[:, MLA_Q_RANK + MLA_KV_RANK:]
    kr = kr * tabs[0] + pltpu.roll(kr, sh, axis=1) * tabs[1] + pltpu.roll(kr, LANE - sh, axis=1) * tabs[2]
    for h in range(HEADS):
        k_ref[h] = (kv[:, h * LANE:(h + 1) * LANE] + kr).astype(BF16)
        vt_ref[h, :LANE, :] = kv[:, hw + h * LANE:hw + (h + 1) * LANE].T.astype(BF16)
        vt_ref[h, LANE:, :] = jnp.ones((ONES_ROWS, vt_ref.shape[2]), BF16)


def _odd_in(x, w, q_norm, w_uq, kv_norm, w_ukv, tabs, *, batch, seq, tm=512):
    n, d = x.shape
    tm = min(tm, seq // 2)
    nt = seq // tm
    tk = min(FLASH_TK, seq // 2)
    per = tk // tm
    hw = HEADS * LANE
    gw_ = w.shape[1] - (MLA_Q_RANK + MLA_KV_RANK + LANE)
    heads = jax.ShapeDtypeStruct((batch, HEADS, seq, LANE), BF16)
    head_spec = pl.BlockSpec((None, HEADS, tm, LANE), lambda i: (i // nt, 0, i % nt, 0))
    tab_spec = pl.BlockSpec((tm, LANE), lambda i: (i % nt, 0))
    const = lambda i: (0, 0)
    return pl.pallas_call(
        _odd_in_kernel,
        grid=(n // tm,),
        in_specs=[pl.BlockSpec((tm, d), lambda i: (i, 0)), pl.BlockSpec(w.shape, const),
                  pl.BlockSpec((1, MLA_Q_RANK), const), pl.BlockSpec(w_uq.shape, const),
                  pl.BlockSpec((1, MLA_KV_RANK), const), pl.BlockSpec(w_ukv.shape, const)] + [tab_spec] * 3,
        out_specs=[pl.BlockSpec((None, HEADS, LANE, tm), lambda i: (i // nt, 0, 0, i % nt)), head_spec,
                   pl.BlockSpec((None, HEADS, None, LANE + ONES_ROWS, tm),
                                lambda i: (i // nt, 0, (i % nt) // per, 0, (i % nt) % per)),
                   pl.BlockSpec((tm, gw_), lambda i: (i, 0))],
        out_shape=[jax.ShapeDtypeStruct((batch, HEADS, LANE, seq), BF16), heads, jax.ShapeDtypeStruct((batch, HEADS, seq // tk, LANE + ONES_ROWS, tk), BF16),
                   jax.ShapeDtypeStruct((n, gw_), F32)],
        compiler_params=_cparams(("parallel",)),
        name="odd_in",
    )(x, w, q_norm.reshape(1, -1), w_uq, kv_norm.reshape(1, -1), w_ukv, *tabs)


def _rope_tables(seq, segs, theta):
    pos = jnp.arange(seq, dtype=F32)
    inv = jnp.zeros((LANE,), F32)
    lo = np.zeros((LANE,), bool)
    hi = np.zeros((LANE,), bool)
    for start, rot in segs:
        half = rot // 2
        f = jnp.power(jnp.float32(theta), -jnp.arange(0, rot, 2, dtype=F32) / rot)
        inv = inv.at[start:start + half].set(f).at[start + half:start + rot].set(f)
        lo[start:start + half] = True
        hi[start + half:start + rot] = True
    ang = pos[:, None] * inv[None, :]
    cos, sin = jnp.cos(ang), jnp.sin(ang)
    c = jnp.where(lo | hi, cos, 1.0)
    sa = jnp.where(hi, sin, 0.0)
    sb = jnp.where(lo, -sin, 0.0)
    return c, sa, sb


ONES_ROWS = 16
LOG2E = math.log2(math.e)
FLASH_TK = 1024
FLASH_TK_DIFF = 512


def _flash_kernel(*refs, ncomp, nk, lam_init):
    if ncomp == 2:
        q_ref, k_ref, vt_ref, lq1, lk1, lq2, lk2, g_ref, o_ref, *scr = refs
    else:
        q_ref, k_ref, vt_ref, o_ref, *scr = refs
    qm_sc, m_sc, acc_sc, s0, s1, cm0, cm1, p0, p1, al0, al1 = scr
    tk = s0.shape[1]
    q = q_ref[...]
    if ncomp == 2:
        chan = lax.broadcasted_iota(jnp.int32, q.shape, 0)
        zero = jnp.zeros_like(q)
        qm_sc[0] = jnp.where(chan < DIFF_HEAD_DIM, q, zero)
        qm_sc[1] = jnp.where(chan >= DIFF_HEAD_DIM, q, zero)
    else:
        qm_sc[0] = q
    m_sc[...] = jnp.full(m_sc.shape, -jnp.inf, F32)
    acc_sc[...] = jnp.zeros(acc_sc.shape, F32)

    def scores(j, s_ref, cm_ref):
        k = k_ref[j * tk:(j + 1) * tk, :]
        for c in range(ncomp):
            s = jnp.dot(k, qm_sc[c], preferred_element_type=F32)
            s_ref[c] = s
            cm_ref[c] = jnp.max(s, axis=0, keepdims=True)

    def softmax(s_ref, cm_ref, p_ref, al_ref):
        for c in range(ncomp):
            m_old = m_sc[c]
            m_new = jnp.maximum(m_old, cm_ref[c])
            al_ref[c] = jnp.exp2(m_old - m_new)
            p_ref[c] = jnp.exp2(s_ref[c] - m_new).astype(BF16)
            m_sc[c] = m_new

    def values(j, p_ref, al_ref):
        vt = vt_ref[j]
        for c in range(ncomp):
            acc_sc[c] = al_ref[c] * acc_sc[c] + jnp.dot(vt, p_ref[c], preferred_element_type=F32)

    bufs = ((s0, cm0, p0, al0), (s1, cm1, p1, al1))
    scores(0, s0, cm0)
    for j in range(nk):
        s_c, cm_c, p_c, al_c = bufs[j % 2]
        s_n, cm_n, p_n, al_n = bufs[(j + 1) % 2]
        if j + 1 < nk:
            scores(j + 1, s_n, cm_n)
        softmax(s_c, cm_c, p_c, al_c)
        if j >= 1:
            values(j - 1, p_n, al_n)
    values(nk - 1, *bufs[(nk - 1) % 2][2:])

    def normalised(c):
        acc = acc_sc[c]
        return acc[:LANE] / acc[LANE:LANE + 1]

    o = normalised(0)
    if ncomp == 2:
        lam = (jnp.exp(jnp.sum(lq1[...] * lk1[...], keepdims=True))
               - jnp.exp(jnp.sum(lq2[...] * lk2[...], keepdims=True)) + lam_init)
        o = o - lam * normalised(1)
        o = o * lax.rsqrt(jnp.mean(o * o, axis=0, keepdims=True) + RMS_EPS) * g_ref[...] * (1.0 - lam_init)
    o_ref[...] = o.T


def _flash(q, k, vt, *, batch, seq, tq=512, diff=None, lam_init=0.0):
    nk, vrows, tk = vt.shape[2], vt.shape[3], vt.shape[4]
    assert vrows == LANE + ONES_ROWS
    tq = min(tq, seq)
    nq = seq // tq
    ncomp = 2 if diff is not None else 1
    in_specs = [
        pl.BlockSpec((None, None, LANE, tq), lambda b, h, i: (b, h, 0, i)),
        pl.BlockSpec((None, None, seq, LANE), lambda b, h, i: (b, h, 0, 0)),
        pl.BlockSpec((None, None, nk, vrows, tk), lambda b, h, i: (b, h, 0, 0, 0)),
    ]
    args = [q, k, vt]
    if diff is not None:
        lq1, lk1, lq2, lk2, subln = diff
        for v in (lq1, lk1, lq2, lk2):
            in_specs.append(pl.BlockSpec((1, DIFF_HEAD_DIM), lambda b, h, i: (0, 0)))
            args.append(v.reshape(1, DIFF_HEAD_DIM))
        in_specs.append(pl.BlockSpec((LANE, 1), lambda b, h, i: (0, 0)))
        args.append(subln.reshape(LANE, 1))
    return pl.pallas_call(
        functools.partial(_flash_kernel, ncomp=ncomp, nk=nk, lam_init=lam_init),
        grid=(batch, HEADS, nq),
        in_specs=in_specs,
        out_specs=pl.BlockSpec((tq, LANE), lambda b, h, i: (b * nq + i, h)),
        out_shape=jax.ShapeDtypeStruct((batch * seq, HEADS * LANE), F32),
        scratch_shapes=[pltpu.VMEM((ncomp, LANE, tq), BF16),
                        pltpu.VMEM((ncomp, 1, tq), F32), pltpu.VMEM((ncomp, vrows, tq), F32),
                        pltpu.VMEM((ncomp, tk, tq), F32), pltpu.VMEM((ncomp, tk, tq), F32),
                        pltpu.VMEM((ncomp, 1, tq), F32), pltpu.VMEM((ncomp, 1, tq), F32),
                        pltpu.VMEM((ncomp, tk, tq), BF16), pltpu.VMEM((ncomp, tk, tq), BF16),
                        pltpu.VMEM((ncomp, 1, tq), F32), pltpu.VMEM((ncomp, 1, tq), F32)],
        compiler_params=_cparams(("parallel", "parallel", "parallel")),
        name="flash_diff" if diff is not None else "flash_mla",
    )(*args)


def _ret_kernel(dec_ref, qf, kf, vf, qb, kb, vb, of_ref, ob_ref, s_sc, *, chunk):
    @pl.when(pl.program_id(1) == 0)
    def _():
        s_sc[...] = jnp.zeros(s_sc.shape, F32)

    ii = lax.broadcasted_iota(jnp.int32, (chunk, chunk), 0)
    jj = lax.broadcasted_iota(jnp.int32, (chunk, chunk), 1)
    r = lax.broadcasted_iota(jnp.int32, (chunk, 1), 0).astype(F32)
    for d, (q_ref, k_ref, v_ref, o_ref) in enumerate(((qf, kf, vf, of_ref), (qb, kb, vb, ob_ref))):
        for h in range(HEADS):
            la = -jnp.exp(jnp.full((1, 1), dec_ref[d, h], F32))
            if d == 0:
                mask, dist = ii >= jj, (ii - jj).astype(F32)
                qdec, kdec = jnp.exp(la * (r + 1.0)), jnp.exp(la * (chunk - 1.0 - r))
            else:
                mask, dist = jj > ii, (jj - ii).astype(F32)
                qdec, kdec = jnp.exp(la * (chunk - r)), jnp.exp(la * r)
            decay = jnp.where(mask, jnp.exp(jnp.where(mask, dist * la, 0.0)), 0.0)
            q, k, v = q_ref[h], k_ref[h], v_ref[h]
            s = lax.dot_general(q, k, (((1,), (1,)), ((), ())), preferred_element_type=F32)
            o = jnp.dot((s * decay).astype(BF16), v, preferred_element_type=F32)
            state = s_sc[d, h]
            o = o + qdec * jnp.dot(q, state.astype(BF16), preferred_element_type=F32)
            kd = (k.astype(F32) * kdec).astype(BF16)
            s_sc[d, h] = jnp.exp(la * float(chunk)) * state + lax.dot_general(
                kd, v, (((0,), (0,)), ((), ())), preferred_element_type=F32)
            o_ref[:, h * LANE:(h + 1) * LANE] = o


def _retention(q, k, v, decays, *, batch, seq, chunk=256):
    chunk = min(chunk, seq)
    n = seq // chunk
    fwd = pl.BlockSpec((None, HEADS, chunk, LANE), lambda b, c: (b, 0, c, 0))
    bwd = pl.BlockSpec((None, HEADS, chunk, LANE), lambda b, c: (b, 0, n - 1 - c, 0))
    w = HEADS * LANE
    out = jax.ShapeDtypeStruct((batch * seq, w), F32)
    return pl.pallas_call(
        functools.partial(_ret_kernel, chunk=chunk),
        grid=(batch, n),
        in_specs=[pl.BlockSpec(memory_space=pltpu.SMEM), fwd, fwd, fwd, bwd, bwd, bwd],
        out_specs=[pl.BlockSpec((chunk, w), lambda b, c: (b * n + c, 0)),
                   pl.BlockSpec((chunk, w), lambda b, c: (b * n + n - 1 - c, 0))],
        out_shape=[out, out],
        scratch_shapes=[pltpu.VMEM((2, HEADS, LANE, LANE), F32)],
        compiler_params=_cparams(("parallel", "arbitrary")),
        name="retention",
    )(decays, q, k, v, q, k, v)


GLA_SUB = 8


def _split3(x):
    x1 = x.astype(BF16)
    r1 = x - x1.astype(F32)
    x2 = r1.astype(BF16)
    x3 = (r1 - x2.astype(F32)).astype(BF16)
    return x1, x2, x3


def _gla_direction(q, k, v, lr, w2, bias, st, reverse):
    C, wk = q.shape
    wv = v.shape[1]
    dk, dv = wk // HEADS, wv // HEADS
    z = jnp.dot(lr.astype(BF16), w2, preferred_element_type=F32) + bias
    g = (jnp.minimum(z, 0.0) - jnp.log(1.0 + jnp.exp(-jnp.abs(z)))) * (LOG2E / GLA_TAU)
    ii = lax.broadcasted_iota(jnp.int32, (C, C), 0)
    jj = lax.broadcasted_iota(jnp.int32, (C, C), 1)
    tri = jnp.where(ii >= jj, 1.0, 0.0).astype(BF16)
    b = sum(jnp.dot(tri, part, preferred_element_type=F32) for part in _split3(g))
    tot = b[C - 1:C, :]
    c = (tot - b + g) if reverse else b

    qe = (q * jnp.exp2(jnp.minimum(c, 0.0))).astype(BF16)
    o = lax.dot_general(qe, st.astype(BF16), (((1,), (1,)), ((), ())), preferred_element_type=F32)
    ke = (k * jnp.exp2(jnp.minimum(tot - c, 0.0))).astype(BF16)
    upd = lax.dot_general(v.astype(BF16), ke, (((0,), (0,)), ((), ())), preferred_element_type=F32)
    rr = _div_pow2(lax.broadcasted_iota(jnp.int32, (wv, wk), 0), dv)
    cc = _div_pow2(lax.broadcasted_iota(jnp.int32, (wv, wk), 1), dk)
    new_st = jnp.where(rr == cc, st * jnp.exp2(tot) + upd, 0.0)

    lane_head = _div_pow2(lax.broadcasted_iota(jnp.int32, (C, wk), 1), dk)
    scores = [jnp.zeros((C, C), F32) for _ in range(HEADS)]
    hsz = C // 2
    while hsz >= GLA_SUB:
        blk = 2 * hsz
        rows = []
        for m in range(C // blk):
            rrow = m * blk + (hsz if reverse else hsz - 1)
            rows.append(jnp.broadcast_to(c[rrow:rrow + 1, :], (blk, wk)))
        ref = jnp.concatenate(rows, axis=0) if len(rows) > 1 else rows[0]
        qt = q * jnp.exp2(jnp.minimum(c - ref, 0.0))
        kt = (k * jnp.exp2(jnp.minimum(ref - c, 0.0))).astype(BF16)
        same = _div_pow2(ii, blk) == _div_pow2(jj, blk)
        if reverse:
            lvl = same & (_mod_pow2(ii, blk) < hsz) & (_mod_pow2(jj, blk) >= hsz)
        else:
            lvl = same & (_mod_pow2(ii, blk) >= hsz) & (_mod_pow2(jj, blk) < hsz)
        for h in range(HEADS):
            qh = jnp.where(lane_head == h, qt, 0.0).astype(BF16)
            s = lax.dot_general(qh, kt, (((1,), (1,)), ((), ())), preferred_element_type=F32)
            scores[h] = scores[h] + jnp.where(lvl, s, 0.0)
        hsz //= 2

    assert dv == C
    er = _div_pow2(lax.broadcasted_iota(jnp.int32, (wk, wv), 0), dk)
    ec = _div_pow2(lax.broadcasted_iota(jnp.int32, (wk, wv), 1), dv)
    expand = jnp.where(er == ec, 1.0, 0.0).astype(BF16)
    dist = (jj - ii) if reverse else (ii - jj)
    same_sub = _div_pow2(ii, GLA_SUB) == _div_pow2(jj, GLA_SUB)
    for lag in range(1 if reverse else 0, GLA_SUB):
        if lag == 0:
            t = q * k
        else:
            shift = (GLA_SUB - lag) if reverse else lag
            ks = pltpu.roll(k.reshape(C // GLA_SUB, GLA_SUB, wk), shift, axis=1).reshape(C, wk)
            cs = pltpu.roll(c.reshape(C // GLA_SUB, GLA_SUB, wk), shift, axis=1).reshape(C, wk)
            t = q * ks * jnp.exp2(jnp.minimum(c - cs, 0.0))
        red = jnp.dot(t.astype(BF16), expand, preferred_element_type=F32)
        on_diag = same_sub & (dist == lag)
        for h in range(HEADS):
            scores[h] = scores[h] + jnp.where(on_diag, red[:, h * dv:(h + 1) * dv], 0.0)

    vb = v.astype(BF16)
    o = o + jnp.concatenate(
        [jnp.dot(scores[h].astype(BF16), vb[:, h * dv:(h + 1) * dv], preferred_element_type=F32) for h in range(HEADS)],
        axis=1)
    return o, new_st


def _gla_kernel(qf, kf, vf, lf, qb, kb, vb, lb, w2f, bf, w2b, bb, of_ref, ob_ref, s_sc, *, qscale):
    @pl.when(pl.program_id(1) == 0)
    def _():
        s_sc[...] = jnp.zeros(s_sc.shape, F32)

    o, st = _gla_direction(qf[...] * qscale, kf[...], vf[...], lf[...], w2f[...], bf[...], s_sc[0], False)
    of_ref[...] = o
    s_sc[0] = st
    o, st = _gla_direction(qb[...] * qscale, kb[...], vb[...], lb[...], w2b[...], bb[...], s_sc[1], True)
    ob_ref[...] = o
    s_sc[1] = st


def _gla(z, *, qcol, kcol, vcol, lcol, w2f, bf, w2b, bb, batch, seq, chunk=128):
    chunk = min(chunk, seq)
    n = seq // chunk
    wk, wv = HEADS * GLA_K_DIM, HEADS * LANE

    def specs(cmap):
        return [pl.BlockSpec((chunk, wk), lambda b, c: (cmap(b, c), qcol)),
                pl.BlockSpec((chunk, wk), lambda b, c: (cmap(b, c), kcol)),
                pl.BlockSpec((chunk, wv), lambda b, c: (cmap(b, c), vcol)),
                pl.BlockSpec((chunk, LANE), lambda b, c: (cmap(b, c), lcol))]

    fmap = lambda b, c: b * n + c
    bmap = lambda b, c: b * n + n - 1 - c
    wspec = [pl.BlockSpec((LANE, wk), lambda b, c: (0, 0)), pl.BlockSpec((1, wk), lambda b, c: (0, 0))]
    out = jax.ShapeDtypeStruct((batch * seq, wv), F32)
    return pl.pallas_call(
        functools.partial(_gla_kernel, qscale=GLA_K_DIM ** -0.5),
        grid=(batch, n),
        in_specs=specs(fmap) + specs(bmap) + wspec + wspec,
        out_specs=[pl.BlockSpec((chunk, wv), lambda b, c: (fmap(b, c), 0)),
                   pl.BlockSpec((chunk, wv), lambda b, c: (bmap(b, c), 0))],
        out_shape=[out, out],
        scratch_shapes=[pltpu.VMEM((2, wv, wk), F32)],
        compiler_params=_cparams(("parallel", "arbitrary")),
        name="gla",
    )(z, z, z, z, z, z, z, z, w2f, bf, w2b, bb)


def _layer_norm_rows(r, g, b):
    mu = jnp.mean(r, axis=-1, keepdims=True)
    d = r - mu
    var = jnp.mean(d * d, axis=-1, keepdims=True)
    return d * lax.rsqrt(var + LN_EPS) * g + b


def _outproj_kernel(x_ref, fin_ref, of_ref, ob_ref, gate_ref, ng_ref, wa_ref, wb_ref, lg_ref, lb_ref, rw_ref, rb_ref,
                    o_ref, ids_ref, gw_ref, cnt_ref, run_sc, *, group_norm):
    lin = of_ref[...] + ob_ref[...]
    parts = []
    for h in range(HEADS):
        zh = lin[:, h * LANE:(h + 1) * LANE]
        if group_norm:
            mu = jnp.mean(zh, axis=-1, keepdims=True)
            dz = zh - mu
            parts.append(dz * lax.rsqrt(jnp.mean(dz * dz, axis=-1, keepdims=True) + LN_EPS))
        else:
            parts.append(zh * lax.rsqrt(jnp.mean(zh * zh, axis=-1, keepdims=True) + RMS_EPS) * ng_ref[...])
    gate = gate_ref[...]
    lin = jnp.concatenate(parts, axis=1) * (gate * jax.nn.sigmoid(gate))
    y = (jnp.dot(fin_ref[...].astype(BF16), wa_ref[...], preferred_element_type=F32)
         + jnp.dot(lin.astype(BF16), wb_ref[...], preferred_element_type=F32))
    x1 = _layer_norm_rows(ALPHA * x_ref[...] + y, lg_ref[...], lb_ref[...])
    o_ref[...] = x1
    _route_rows(x1, rw_ref, rb_ref, ids_ref, gw_ref, cnt_ref, run_sc)


def _outproj(x, fin, of, ob, gate_src, gate_col, norm_gain, wa, wb, ln_g, ln_b, route_w, route_b, *, group_norm, tm=512):
    n, d = x.shape
    w = HEADS * LANE
    tm = min(tm, n)
    row = lambda i: (i, 0)
    const = lambda i: (0, 0)
    rw_hi = route_w.astype(BF16)
    rw = jnp.stack([rw_hi, (route_w - rw_hi.astype(F32)).astype(BF16)])
    return pl.pallas_call(
        functools.partial(_outproj_kernel, group_norm=group_norm),
        grid=(n // tm,),
        in_specs=[pl.BlockSpec((tm, d), row), pl.BlockSpec((tm, w), row), pl.BlockSpec((tm, w), row),
                  pl.BlockSpec((tm, w), row), pl.BlockSpec((tm, w), lambda i: (i, gate_col)),
                  pl.BlockSpec((1, LANE), const), pl.BlockSpec((w, d), const), pl.BlockSpec((w, d), const),
                  pl.BlockSpec((1, d), const), pl.BlockSpec((1, d), const),
                  pl.BlockSpec((2, d, LANE), lambda i: (0, 0, 0)), pl.BlockSpec((1, LANE), const)],
        out_specs=[pl.BlockSpec((tm, d), row), pl.BlockSpec((tm, LANE), row), pl.BlockSpec((tm, LANE), row),
                   pl.BlockSpec((1, LANE), const)],
        out_shape=[jax.ShapeDtypeStruct((n, d), F32), jax.ShapeDtypeStruct((n, LANE), jnp.int32),
                   jax.ShapeDtypeStruct((n, LANE), F32), jax.ShapeDtypeStruct((1, LANE), jnp.int32)],
        scratch_shapes=[pltpu.VMEM((1, LANE), F32)],
        compiler_params=_cparams(("arbitrary",)),
        name="outproj",
    )(x, fin, of, ob, gate_src, norm_gain.reshape(1, LANE), wa, wb, ln_g.reshape(1, d), ln_b.reshape(1, d), rw, route_b)


def _route_rows(x, w_ref, b_ref, ids_ref, gw_ref, cnt_ref, run_sc):
    @pl.when(pl.program_id(0) == 0)
    def _():
        run_sc[...] = jnp.zeros(run_sc.shape, F32)

    tm = x.shape[0]
    xh = x.astype(BF16)
    xl = (x - xh.astype(F32)).astype(BF16)
    wh, wl = w_ref[0], w_ref[1]
    logits = (jnp.dot(xh, wh, preferred_element_type=F32) + jnp.dot(xh, wl, preferred_element_type=F32)
              + jnp.dot(xl, wh, preferred_element_type=F32)) + b_ref[...]
    lane = lax.broadcasted_iota(jnp.int32, logits.shape, 1)
    neg = -jnp.inf
    gmask = (lane >= N_EXPERTS) & (lane < N_EXPERTS + N_GROUPS)
    gl = jnp.where(gmask, logits, neg)
    gmax = jnp.max(gl, axis=1, keepdims=True)
    lane_f = lane.astype(F32)
    first = lambda hit: jnp.min(jnp.where(hit, lane_f, float(LANE)), axis=1, keepdims=True).astype(jnp.int32)
    gidx = first(gl == gmax) - N_EXPERTS
    p_grp = 1.0 / jnp.sum(jnp.where(gmask, jnp.exp(gl - gmax), 0.0), axis=1, keepdims=True)
    el = jnp.where(_div_pow2(lane, EXPERTS_PER_GROUP) == gidx, logits, neg)
    l1 = jnp.max(el, axis=1, keepdims=True)
    e1 = first(el == l1)
    el2 = jnp.where(lane == e1, neg, el)
    l2 = jnp.max(el2, axis=1, keepdims=True)
    e2 = first(el2 == l2)
    t = jnp.exp(l2 - l1)
    w1 = p_grp / (1.0 + t)
    w2 = p_grp * t / (1.0 + t)

    onehot = jnp.where(lane == e1, 1.0, jnp.where(lane == e2, 1.0, 0.0))
    ri = lax.broadcasted_iota(jnp.int32, (tm, tm), 0)
    ci = lax.broadcasted_iota(jnp.int32, (tm, tm), 1)
    before = jnp.dot(jnp.where(ri > ci, 1.0, 0.0).astype(BF16), onehot.astype(BF16), preferred_element_type=F32)
    before = before + run_sc[...]
    r1 = jnp.sum(jnp.where(lane == e1, before, 0.0), axis=1, keepdims=True).astype(jnp.int32)
    r2 = jnp.sum(jnp.where(lane == e2, before, 0.0), axis=1, keepdims=True).astype(jnp.int32)
    run_sc[...] = run_sc[...] + jnp.sum(onehot, axis=0, keepdims=True)
    cnt_ref[...] = run_sc[...].astype(jnp.int32)
    ids_ref[...] = jnp.where(lane == 0, e1, jnp.where(lane == 1, e2, jnp.where(lane == 2, r1, jnp.where(lane == 3, r2, 0))))
    gw_ref[...] = jnp.where(lane == 0, w1, jnp.where(lane == 1, w2, 0.0))


MOE_BM = 256
SUBL = 8


def _rows_from_linear(ref, rows):
    return jnp.concatenate([ref[pl.ds(s, rows, stride=SUBL), :] for s in range(SUBL)], axis=1)


def _rows_to_linear(ref, val):
    for s in range(SUBL):
        ref[pl.ds(s, val.shape[0], stride=SUBL), :] = val[:, s * LANE:(s + 1) * LANE]


def _dispatch_kernel(pend_ref, padded_ref, dest_ref, x_ref, xs_hbm, idx_smem, zbuf, lin, sem_i, sem_z, sem):
    i = pl.program_id(0)
    tm = x_ref.shape[0]
    bm = zbuf.shape[0] // SUBL

    @pl.when(i == 0)
    def _():
        zbuf[...] = jnp.zeros(zbuf.shape, F32)

        def tail(e):
            start_row = pl.multiple_of((pend_ref[e] - bm) * SUBL, bm * SUBL)
            return pltpu.make_async_copy(zbuf, xs_hbm.at[pl.ds(start_row, bm * SUBL), :], sem_z)

        def start(e, carry):
            @pl.when(padded_ref[e] > 0)
            def _():
                tail(e).start()
            return carry

        def wait(e, carry):
            @pl.when(padded_ref[e] > 0)
            def _():
                tail(e).wait()
            return carry

        lax.fori_loop(0, N_EXPERTS, start, 0)
        lax.fori_loop(0, N_EXPERTS, wait, 0)

        def unused(b):
            start_row = pl.multiple_of(b * bm * SUBL, bm * SUBL)
            return pltpu.make_async_copy(zbuf, xs_hbm.at[pl.ds(start_row, bm * SUBL), :], sem_z)

        first_unused = pend_ref[N_EXPERTS - 1] // bm
        n_blocks = xs_hbm.shape[0] // (bm * SUBL)
        lax.fori_loop(first_unused, n_blocks, lambda b, c: (unused(b).start(), c)[1], 0)
        lax.fori_loop(first_unused, n_blocks, lambda b, c: (unused(b).wait(), c)[1], 0)

    cp = pltpu.make_async_copy(dest_ref.at[i], idx_smem, sem_i)
    cp.start()
    _rows_to_linear(lin, x_ref[...])
    cp.wait()

    def scatter(r, carry):
        src = lin.at[pl.ds(pl.multiple_of(r * SUBL, SUBL), SUBL), :]
        for k in range(2):
            dst = pl.multiple_of(idx_smem[2 * r + k] * SUBL, SUBL)
            pltpu.make_async_copy(src, xs_hbm.at[pl.ds(dst, SUBL), :], sem).start(priority=k)
        return carry

    lax.fori_loop(0, tm, scatter, 0, unroll=8)
    for k in range(2):
        pltpu.make_async_copy(lin, xs_hbm.at[pl.ds(0, tm * SUBL), :], sem).wait()


def _dispatch(x, dest, pend, padded, cap, *, tm=512):
    n, d = x.shape
    assert d == SUBL * LANE
    tm = min(tm, n)
    nt = n // tm
    grid_spec = pltpu.PrefetchScalarGridSpec(
        num_scalar_prefetch=2,
        grid=(nt,),
        in_specs=[pl.BlockSpec((nt, 2 * tm), lambda i, pe, pa: (0, 0)),
                  pl.BlockSpec((tm, d), lambda i, pe, pa: (i, 0))],
        out_specs=pl.BlockSpec(memory_space=pl.ANY),
        scratch_shapes=[pltpu.SMEM((2 * tm,), jnp.int32), pltpu.VMEM((MOE_BM * SUBL, LANE), F32),
                        pltpu.VMEM((tm * SUBL, LANE), F32),
                        pltpu.SemaphoreType.DMA(()), pltpu.SemaphoreType.DMA(()), pltpu.SemaphoreType.DMA(())],
    )
    return pl.pallas_call(
        _dispatch_kernel,
        grid_spec=grid_spec,
        out_shape=jax.ShapeDtypeStruct((cap * SUBL, LANE), F32),
        compiler_params=_cparams(("arbitrary",)),
        name="dispatch",
    )(pend, padded, dest.reshape(nt, 2 * tm), x)


def _experts_kernel(blk_e_ref, nused_ref, xs_ref, wg_ref, wu_ref, wd_ref, ys_ref, wgb, wub, wdb, hbuf):
    i = pl.program_id(0)
    nb = pl.num_programs(0) - 1
    cur = jnp.minimum(i, nb - 1)
    prv = jnp.maximum(i - 1, 0)

    @pl.when(i == 0)
    def _():
        hbuf[...] = jnp.zeros(hbuf.shape, BF16)

    @pl.when(jnp.logical_or(i == 0, blk_e_ref[cur] != blk_e_ref[jnp.minimum(prv, nb - 1)]))
    def _():
        wgb[...] = wg_ref[...].astype(BF16)
        wub[...] = wu_ref[...].astype(BF16)

    @pl.when(jnp.logical_or(i == 0, blk_e_ref[prv] != blk_e_ref[jnp.maximum(i - 2, 0)]))
    def _():
        wdb[...] = wd_ref[...].astype(BF16)

    slot = lax.rem(i, 2)
    y = jnp.dot(hbuf[1 - slot], wdb[...], preferred_element_type=F32)
    live = jnp.logical_and(i >= 1, i - 1 < nused_ref[0])
    _rows_to_linear(ys_ref, jnp.where(live, y, 0.0))
    xb = _rows_from_linear(xs_ref, xs_ref.shape[0] // SUBL).astype(BF16)
    hg = jnp.dot(xb, wgb[...], preferred_element_type=F32)
    hu = jnp.dot(xb, wub[...], preferred_element_type=F32)
    hbuf[slot] = (hg * jax.nn.sigmoid(hg) * hu).astype(BF16)


def _experts(xs, blk_e, n_used, w_gate, w_up, w_down, layer):
    cap = xs.shape[0] // SUBL
    bm = MOE_BM
    nb = cap // bm
    d, de = w_gate.shape[2], w_gate.shape[3]
    row_in = lambda i, be, nu: (jnp.minimum(i, nu[0] - 1), 0)
    row = lambda i, be, nu: (jnp.maximum(i - 1, 0), 0)
    w_cur = lambda i, be, nu: (layer, be[jnp.minimum(i, nb - 1)], 0, 0)
    w_prv = lambda i, be, nu: (layer, be[jnp.maximum(i - 1, 0)], 0, 0)
    grid_spec = pltpu.PrefetchScalarGridSpec(
        num_scalar_prefetch=2,
        grid=(nb + 1,),
        in_specs=[pl.BlockSpec((bm * SUBL, LANE), row_in),
                  pl.BlockSpec((None, None, d, de), w_cur),
                  pl.BlockSpec((None, None, d, de), w_cur),
                  pl.BlockSpec((None, None, de, d), w_prv)],
        out_specs=pl.BlockSpec((bm * SUBL, LANE), row),
        scratch_shapes=[pltpu.VMEM((d, de), BF16), pltpu.VMEM((d, de), BF16), pltpu.VMEM((de, d), BF16),
                        pltpu.VMEM((2, bm, de), BF16)],
    )
    return pl.pallas_call(
        _experts_kernel,
        grid_spec=grid_spec,
        out_shape=jax.ShapeDtypeStruct((cap * SUBL, LANE), F32),
        compiler_params=_cparams(("arbitrary",)),
        name="experts",
    )(blk_e, n_used, xs, w_gate, w_up, w_down)


def _combine_kernel(dest_ref, x_ref, gw_ref, g_ref, b_ref, ys_hbm, o_ref, idx_smem, ybuf, sem_i, sem):
    i = pl.program_id(0)
    tm = x_ref.shape[0]

    def issue(tile, slot):
        cp = pltpu.make_async_copy(dest_ref.at[tile], idx_smem, sem_i)
        cp.start()
        cp.wait()

        def gather(r, carry):
            row = pl.multiple_of(r * SUBL, SUBL)
            for k in range(2):
                src = pl.multiple_of(idx_smem[2 * r + k] * SUBL, SUBL)
                pltpu.make_async_copy(ys_hbm.at[pl.ds(src, SUBL), :], ybuf.at[slot, k, pl.ds(row, SUBL), :],
                                      sem.at[slot]).start(priority=k)
            return carry

        lax.fori_loop(0, tm, gather, 0, unroll=8)

    slot = lax.rem(i, 2)

    @pl.when(i == 0)
    def _():
        issue(0, 0)

    @pl.when(i + 1 < pl.num_programs(0))
    def _():
        issue(i + 1, 1 - slot)

    for k in range(2):
        pltpu.make_async_copy(ys_hbm.at[pl.ds(0, tm * SUBL), :], ybuf.at[slot, k], sem.at[slot]).wait()
    gw = gw_ref[...]
    ffn = (_rows_from_linear(ybuf.at[slot, 0], tm) * gw[:, 0:1] + _rows_from_linear(ybuf.at[slot, 1], tm) * gw[:, 1:2])
    o_ref[...] = _layer_norm_rows(ALPHA * x_ref[...] + ffn, g_ref[...], b_ref[...])


def _combine(x, ys, dest, gw, ln_g, ln_b, *, tm=512):
    n, d = x.shape
    tm = min(tm, n)
    nt = n // tm
    return pl.pallas_call(
        _combine_kernel,
        grid=(nt,),
        in_specs=[pl.BlockSpec((nt, 2 * tm), lambda i: (0, 0)),
                  pl.BlockSpec((tm, d), lambda i: (i, 0)), pl.BlockSpec((tm, LANE), lambda i: (i, 0)),
                  pl.BlockSpec((1, d), lambda i: (0, 0)), pl.BlockSpec((1, d), lambda i: (0, 0)),
                  pl.BlockSpec(memory_space=pl.ANY)],
        out_specs=pl.BlockSpec((tm, d), lambda i: (i, 0)),
        out_shape=jax.ShapeDtypeStruct((n, d), F32),
        scratch_shapes=[pltpu.SMEM((2 * tm,), jnp.int32), pltpu.VMEM((2, 2, tm * SUBL, LANE), F32),
                        pltpu.SemaphoreType.DMA(()), pltpu.SemaphoreType.DMA((2,))],
        compiler_params=_cparams(("arbitrary",)),
        name="combine",
    )(dest.reshape(nt, 2 * tm), x, gw, ln_g.reshape(1, d), ln_b.reshape(1, d), ys)


def _router_params(w_grp, b_grp, w_exp, b_exp):
    d = w_exp.shape[0]
    wr = jnp.zeros((d, LANE), F32).at[:, :N_EXPERTS].set(w_exp).at[:, N_EXPERTS:N_EXPERTS + N_GROUPS].set(w_grp)
    br = jnp.zeros((1, LANE), F32).at[0, :N_EXPERTS].set(b_exp).at[0, N_EXPERTS:N_EXPERTS + N_GROUPS].set(b_grp)
    return wr, br


def _moe(x, routing, w_gate, w_up, w_down, layer, ln_g, ln_b):
    n, d = x.shape
    ids, gw, cnt = routing
    bm = MOE_BM
    counts = cnt[0, :N_EXPERTS]
    padded = (counts + bm - 1) // bm * bm
    pend = jnp.cumsum(padded)
    pstart = pend - padded
    e, r = ids[:, 0:2], ids[:, 2:4]
    onehot = e[:, :, None] == jnp.arange(N_EXPERTS, dtype=jnp.int32)[None, None, :]
    dest = jnp.sum(jnp.where(onehot, pstart[None, None, :], 0), axis=-1) + r
    cap = 2 * n + N_EXPERTS * bm
    nb = cap // bm
    blk_start = jnp.arange(nb, dtype=jnp.int32) * bm
    blk_e = jnp.minimum(jnp.sum((pend[None, :] <= blk_start[:, None]).astype(jnp.int32), axis=1), N_EXPERTS - 1)
    n_used = (pend[-1:] // bm).astype(jnp.int32)
    xs = _dispatch(x, dest, pend.astype(jnp.int32), padded.astype(jnp.int32), cap)
    ys = _experts(xs, blk_e, n_used, w_gate, w_up, w_down, layer)
    return _combine(x, ys, dest, gw, ln_g, ln_b)


def _even_layer(x, batch, seq, layer_idx, w_in, dec_f, dec_b, lq1, lk1, lq2, lk2, subln, w_out, ln_g, ln_b, route):
    d = x.shape[1]
    w = HEADS * LANE
    kw = dict(batch=batch, seq=seq)
    diff_seg = [(0, DIFF_ROT_DIM), (DIFF_HEAD_DIM, DIFF_ROT_DIM)]
    q, k, v, gate, dq, dk, dvt = _even_in(
        x, w_in.astype(BF16), _rope_tables(seq, [(0, LANE)], RET_THETA),
        _rope_tables(seq, diff_seg, ROPE_THETA), **kw)
    decays = jnp.stack([dec_f, dec_b]).astype(F32)
    of, ob = _retention(q, k, v, decays, **kw)
    lam_init = 0.8 - 0.6 * math.exp(-0.3 * layer_idx)
    diff = _flash(dq, dk, dvt, diff=(lq1, lk1, lq2, lk2, subln), lam_init=lam_init, tq=2048, **kw)
    wo = w_out.astype(BF16)
    return _outproj(x, diff, of, ob, gate, 0, jnp.ones((LANE,), F32), wo[w:], wo[:w], ln_g, ln_b, *route,
                    group_norm=True)


def _odd_layer(x, batch, seq, w_in, q_norm, w_uq, kv_norm, w_ukv, w2_f, b_f, w2_b, b_b, gla_norm, w_out, ln_g, ln_b,
               route):
    d = x.shape[1]
    w = HEADS * LANE
    o = np.cumsum([0, MLA_Q_RANK, MLA_KV_RANK, MLA_ROPE, HEADS * GLA_K_DIM, HEADS * GLA_K_DIM, w, w,
                   GLA_GATE_RANK, GLA_GATE_RANK]).tolist()
    zeros = lambda c: jnp.zeros((d, c), F32)
    w_in2 = jnp.concatenate([
        w_in[:, o[0]:o[2]], zeros(MLA_NOPE), w_in[:, o[2]:o[3]], zeros(LANE - MLA_NOPE - MLA_ROPE),
        w_in[:, o[3]:o[7]], w_in[:, o[7]:o[9]], zeros(LANE - 2 * GLA_GATE_RANK)], axis=1).astype(BF16)
    kw = dict(batch=batch, seq=seq)
    qd = MLA_NOPE + MLA_ROPE
    w_uq2 = jnp.pad(w_uq.reshape(MLA_Q_RANK, HEADS, qd), ((0, 0), (0, 0), (0, LANE - qd))).reshape(MLA_Q_RANK, w)
    ukv = w_ukv.reshape(MLA_KV_RANK, HEADS, MLA_NOPE + MLA_V)
    w_uk2 = jnp.pad(ukv[:, :, :MLA_NOPE], ((0, 0), (0, 0), (0, LANE - MLA_NOPE))).reshape(MLA_KV_RANK, w)
    w_uv2 = ukv[:, :, MLA_NOPE:].reshape(MLA_KV_RANK, w)
    q, k, vt, zg = _odd_in(x, w_in2, q_norm, w_uq2.astype(BF16), kv_norm,
                           jnp.concatenate([w_uk2, w_uv2], axis=1).astype(BF16),
                           _rope_tables(seq, [(MLA_NOPE, MLA_ROPE)], ROPE_THETA), **kw)
    mla = _flash(q, k, vt, **kw)
    wk = HEADS * GLA_K_DIM
    pad_rows = lambda m, r0: jnp.zeros((LANE, wk), F32).at[r0:r0 + GLA_GATE_RANK].set(m).astype(BF16)
    of, ob = _gla(zg, qcol=0, kcol=1, vcol=1, lcol=12,
                  w2f=pad_rows(w2_f, 0), bf=b_f.reshape(1, wk), w2b=pad_rows(w2_b, GLA_GATE_RANK), bb=b_b.reshape(1, wk), **kw)
    wo = w_out.astype(BF16)
    return _outproj(x, mla, of, ob, zg, 2, gla_norm, wo[:w], wo[w:], ln_g, ln_b, *route, group_norm=False)


def kernel(x, ev_w_in, ev_ret_decay_f, ev_ret_decay_b, ev_lq1, ev_lk1, ev_lq2, ev_lk2, ev_subln, ev_w_out, od_w_in, od_q_norm, od_w_uq, od_kv_norm, od_w_ukv, od_gla_w2_f, od_gla_b_f, od_gla_w2_b, od_gla_b_b, od_gla_norm, od_w_out, ln1_g, ln1_b, ln2_g, ln2_b, moe_w_grp, moe_b_grp, moe_w_exp, moe_b_exp, moe_w_gate, moe_w_up, moe_w_down):
    batch, seq, d = x.shape
    h = x.reshape(batch * seq, d)
    for i in range(DEPTH):
        j = i // 2
        route = _router_params(moe_w_grp[i], moe_b_grp[i], moe_w_exp[i], moe_b_exp[i])
        if i % 2 == 0:
            h, *routing = _even_layer(h, batch, seq, i, ev_w_in[j], ev_ret_decay_f[j], ev_ret_decay_b[j], ev_lq1[j],
                                      ev_lk1[j], ev_lq2[j], ev_lk2[j], ev_subln[j], ev_w_out[j], ln1_g[i], ln1_b[i], route)
        else:
            h, *routing = _odd_layer(h, batch, seq, od_w_in[j], od_q_norm[j], od_w_uq[j], od_kv_norm[j], od_w_ukv[j],
                                     od_gla_w2_f[j], od_gla_b_f[j], od_gla_w2_b[j], od_gla_b_b[j], od_gla_norm[j],
                                     od_w_out[j], ln1_g[i], ln1_b[i], route)
        h = _moe(h, routing, moe_w_gate, moe_w_up, moe_w_down, i, ln2_g[i], ln2_b[i])
    return h.reshape(batch, seq, d)
```

```python
import functools
import math

import numpy as np
import jax
import jax.numpy as jnp
from jax import lax
from jax.experimental import pallas as pl
from jax.experimental.pallas import tpu as pltpu

F32 = jnp.float32
BF16 = jnp.bfloat16

HEADS = 4
LANE = 128
RET_THETA = 10000.0
ROPE_THETA = 500000.0
DIFF_HEAD_DIM = 64
DIFF_ROT_DIM = 16
MLA_Q_RANK = 256
MLA_KV_RANK = 128
MLA_NOPE = 64
MLA_ROPE = 32
MLA_V = 128
GLA_K_DIM = 64
GLA_GATE_RANK = 16
GLA_TAU = 16.0
N_GROUPS = 4
EXPERTS_PER_GROUP = 8
N_EXPERTS = N_GROUPS * EXPERTS_PER_GROUP
DEPTH = 2
ALPHA = (2.0 * DEPTH) ** 0.25
LN_EPS = 1e-5
RMS_EPS = 1e-6

VMEM_LIMIT = 48 * 1024 * 1024


def _div_pow2(x, n):
    return lax.shift_right_logical(x, int(n).bit_length() - 1)


def _mod_pow2(x, n):
    return lax.bitwise_and(x, int(n) - 1)


def _cparams(sem):
    return pltpu.CompilerParams(dimension_semantics=sem, vmem_limit_bytes=VMEM_LIMIT)


def _rope_heads(z, tabs, sh, scale):
    c, sa, sb = tabs
    outs = []
    for h in range(HEADS):
        zh = z[:, h * LANE:(h + 1) * LANE]
        outs.append((zh * c + pltpu.roll(zh, sh, axis=1) * sa + pltpu.roll(zh, LANE - sh, axis=1) * sb) * scale)
    return outs


def _even_in_kernel(x_ref, w_ref, rc, rsa, rsb, dc, dsa, dsb, q_ref, k_ref, v_ref, g_ref, dq_ref, dk_ref, dvt_ref):
    w = HEADS * LANE
    xb = x_ref[...].astype(BF16)
    part = lambda t: jnp.dot(xb, w_ref[:, t * w:(t + 1) * w], preferred_element_type=F32)
    ret_t = (rc[...], rsa[...], rsb[...])
    diff_t = (dc[...], dsa[...], dsb[...])
    for h, o in enumerate(_rope_heads(part(0), ret_t, LANE // 2, 1.0)):
        q_ref[h] = o.astype(BF16)
    for h, o in enumerate(_rope_heads(part(1), ret_t, LANE // 2, LANE ** -0.5)):
        k_ref[h] = o.astype(BF16)
    rv = part(2)
    for h in range(HEADS):
        v_ref[h] = rv[:, h * LANE:(h + 1) * LANE].astype(BF16)
    g_ref[...] = part(3)
    for h, o in enumerate(_rope_heads(part(4), diff_t, DIFF_ROT_DIM // 2, DIFF_HEAD_DIM ** -0.5 * LOG2E)):
        dq_ref[h] = o.T.astype(BF16)
    for h, o in enumerate(_rope_heads(part(5), diff_t, DIFF_ROT_DIM // 2, 1.0)):
        dk_ref[h] = o.astype(BF16)
    dv = part(6)
    for h in range(HEADS):
        dvt_ref[h, :LANE, :] = dv[:, h * LANE:(h + 1) * LANE].T.astype(BF16)
        dvt_ref[h, LANE:, :] = jnp.ones((ONES_ROWS, dvt_ref.shape[2]), BF16)


def _even_in(x, w, ret_tabs, diff_tabs, *, batch, seq):
    n, d = x.shape
    tm = min(FLASH_TK_DIFF, seq // 2)
    nt = seq // tm
    hw = HEADS * LANE
    heads = jax.ShapeDtypeStruct((batch, HEADS, seq, LANE), BF16)
    head_spec = pl.BlockSpec((None, HEADS, tm, LANE), lambda i: (i // nt, 0, i % nt, 0))
    heads_t = jax.ShapeDtypeStruct((batch, HEADS, LANE, seq), BF16)
    head_t_spec = pl.BlockSpec((None, HEADS, LANE, tm), lambda i: (i // nt, 0, 0, i % nt))
    tab_spec = pl.BlockSpec((tm, LANE), lambda i: (i % nt, 0))
    return pl.pallas_call(
        _even_in_kernel,
        grid=(n // tm,),
        in_specs=[pl.BlockSpec((tm, d), lambda i: (i, 0)), pl.BlockSpec((d, 7 * hw), lambda i: (0, 0))] + [tab_spec] * 6,
        out_specs=[head_spec, head_spec, head_spec, pl.BlockSpec((tm, hw), lambda i: (i, 0)), head_t_spec, head_spec,
                   pl.BlockSpec((None, HEADS, None, LANE + ONES_ROWS, tm), lambda i: (i // nt, 0, i % nt, 0, 0))],
        out_shape=[heads, heads, heads, jax.ShapeDtypeStruct((n, hw), F32), heads_t, heads,
                   jax.ShapeDtypeStruct((batch, HEADS, nt, LANE + ONES_ROWS, tm), BF16)],
        compiler_params=_cparams(("parallel",)),
        name="even_in",
    )(x, w, *ret_tabs, *diff_tabs)


def _rms_rows(z, g):
    return z * lax.rsqrt(jnp.mean(z * z, axis=-1, keepdims=True) + RMS_EPS) * g


def _odd_in_rows(x, w_ref, qn_ref, wq_ref, kvn_ref, wkv_ref, tc, tsa, tsb, q_ref, k_ref, vt_ref, zg_ref):
    hw = HEADS * LANE
    mla_w = MLA_Q_RANK + MLA_KV_RANK + LANE
    xb = x.astype(BF16)
    zg_ref[...] = jnp.dot(xb, w_ref[:, mla_w:], preferred_element_type=F32)
    z1 = jnp.dot(xb, w_ref[:, :mla_w], preferred_element_type=F32)
    tabs = (tc[...], tsa[...], tsb[...])
    sh = MLA_ROPE // 2
    qh = jnp.dot(_rms_rows(z1[:, :MLA_Q_RANK], qn_ref[...]).astype(BF16), wq_ref[...], preferred_element_type=F32)
    for h, o in enumerate(_rope_heads(qh, tabs, sh, (MLA_NOPE + MLA_ROPE) ** -0.5 * LOG2E)):
        q_ref[h] = o.T.astype(BF16)
    ckv = _rms_rows(z1[:, MLA_Q_RANK:MLA_Q_RANK + MLA_KV_RANK], kvn_ref[...]).astype(BF16)
    kv = jnp.dot(ckv, wkv_ref[...], preferred_element_type=F32)
    kr = z1[:, MLA_Q_RANK + MLA_KV_RANK:]
    kr = kr * tabs[0] + pltpu.roll(kr, sh, axis=1) * tabs[1] + pltpu.roll(kr, LANE - sh, axis=1) * tabs[2]
    for h in range(HEADS):
        k_ref[h] = (kv[:, h * LANE:(h + 1) * LANE] + kr).astype(BF16)
        vt_ref[h, :LANE, :] = kv[:, hw + h * LANE:hw + (h + 1) * LANE].T.astype(BF16)
        vt_ref[h, LANE:, :] = jnp.ones((ONES_ROWS, vt_ref.shape[2]), BF16)


def _odd_in_kernel(x_ref, *refs):
    _odd_in_rows(x_ref[...], *refs)


ODD_TM = 512


def _odd_in_plumbing(odd_args, n, batch, seq):
    w, q_norm, w_uq, kv_norm, w_ukv, tabs = odd_args
    tm = min(ODD_TM, seq // 2)
    nt = seq // tm
    tk = min(FLASH_TK, seq // 2)
    per = tk // tm
    gw_ = w.shape[1] - (MLA_Q_RANK + MLA_KV_RANK + LANE)
    tab_spec = pl.BlockSpec((tm, LANE), lambda i: (i % nt, 0))
    const = lambda i: (0, 0)
    operands = (w, q_norm.reshape(1, -1), w_uq, kv_norm.reshape(1, -1), w_ukv, *tabs)
    in_specs = [pl.BlockSpec(w.shape, const), pl.BlockSpec((1, MLA_Q_RANK), const), pl.BlockSpec(w_uq.shape, const),
                pl.BlockSpec((1, MLA_KV_RANK), const), pl.BlockSpec(w_ukv.shape, const)] + [tab_spec] * 3
    out_specs = [pl.BlockSpec((None, HEADS, LANE, tm), lambda i: (i // nt, 0, 0, i % nt)),
                 pl.BlockSpec((None, HEADS, tm, LANE), lambda i: (i // nt, 0, i % nt, 0)),
                 pl.BlockSpec((None, HEADS, None, LANE + ONES_ROWS, tm),
                              lambda i: (i // nt, 0, (i % nt) // per, 0, (i % nt) % per)),
                 pl.BlockSpec((tm, gw_), lambda i: (i, 0))]
    out_shape = [jax.ShapeDtypeStruct((batch, HEADS, LANE, seq), BF16),
                 jax.ShapeDtypeStruct((batch, HEADS, seq, LANE), BF16),
                 jax.ShapeDtypeStruct((batch, HEADS, seq // tk, LANE + ONES_ROWS, tk), BF16),
                 jax.ShapeDtypeStruct((n, gw_), F32)]
    return tm, operands, in_specs, out_specs, out_shape


def _odd_in(x, odd_args, *, batch, seq):
    n, d = x.shape
    tm, operands, in_specs, out_specs, out_shape = _odd_in_plumbing(odd_args, n, batch, seq)
    return pl.pallas_call(
        _odd_in_kernel,
        grid=(n // tm,),
        in_specs=[pl.BlockSpec((tm, d), lambda i: (i, 0))] + in_specs,
        out_specs=out_specs,
        out_shape=out_shape,
        compiler_params=_cparams(("parallel",)),
        name="odd_in",
    )(x, *operands)


def _rope_tables(seq, segs, theta):
    pos = jnp.arange(seq, dtype=F32)
    inv = jnp.zeros((LANE,), F32)
    lo = np.zeros((LANE,), bool)
    hi = np.zeros((LANE,), bool)
    for start, rot in segs:
        half = rot // 2
        f = jnp.power(jnp.float32(theta), -jnp.arange(0, rot, 2, dtype=F32) / rot)
        inv = inv.at[start:start + half].set(f).at[start + half:start + rot].set(f)
        lo[start:start + half] = True
        hi[start + half:start + rot] = True
    ang = pos[:, None] * inv[None, :]
    cos, sin = jnp.cos(ang), jnp.sin(ang)
    c = jnp.where(lo | hi, cos, 1.0)
    sa = jnp.where(hi, sin, 0.0)
    sb = jnp.where(lo, -sin, 0.0)
    return c, sa, sb


ONES_ROWS = 16
LOG2E = math.log2(math.e)
FLASH_TK = 1024
FLASH_TK_DIFF = 512


def _flash_kernel(*refs, ncomp, nk, lam_init):
    if ncomp == 2:
        q_ref, k_ref, vt_ref, lq1, lk1, lq2, lk2, g_ref, o_ref, *scr = refs
    else:
        q_ref, k_ref, vt_ref, o_ref, *scr = refs
    qm_sc, m_sc, acc_sc, s0, s1, cm0, cm1, p0, p1, al0, al1 = scr
    tk = s0.shape[1]
    q = q_ref[...]
    if ncomp == 2:
        chan = lax.broadcasted_iota(jnp.int32, q.shape, 0)
        zero = jnp.zeros_like(q)
        qm_sc[0] = jnp.where(chan < DIFF_HEAD_DIM, q, zero)
        qm_sc[1] = jnp.where(chan >= DIFF_HEAD_DIM, q, zero)
    else:
        qm_sc[0] = q
    m_sc[...] = jnp.full(m_sc.shape, -jnp.inf, F32)
    acc_sc[...] = jnp.zeros(acc_sc.shape, F32)

    def scores(j, s_ref, cm_ref):
        k = k_ref[j * tk:(j + 1) * tk, :]
        for c in range(ncomp):
            s = jnp.dot(k, qm_sc[c], preferred_element_type=F32)
            s_ref[c] = s
            cm_ref[c] = jnp.max(s, axis=0, keepdims=True)

    def softmax(s_ref, cm_ref, p_ref, al_ref):
        for c in range(ncomp):
            m_old = m_sc[c]
            m_new = jnp.maximum(m_old, cm_ref[c])
            al_ref[c] = jnp.exp2(m_old - m_new)
            p_ref[c] = jnp.exp2(s_ref[c] - m_new).astype(BF16)
            m_sc[c] = m_new

    def values(j, p_ref, al_ref):
        vt = vt_ref[j]
        for c in range(ncomp):
            acc_sc[c] = al_ref[c] * acc_sc[c] + jnp.dot(vt, p_ref[c], preferred_element_type=F32)

    bufs = ((s0, cm0, p0, al0), (s1, cm1, p1, al1))
    scores(0, s0, cm0)
    for j in range(nk):
        s_c, cm_c, p_c, al_c = bufs[j % 2]
        s_n, cm_n, p_n, al_n = bufs[(j + 1) % 2]
        if j + 1 < nk:
            scores(j + 1, s_n, cm_n)
        softmax(s_c, cm_c, p_c, al_c)
        if j >= 1:
            values(j - 1, p_n, al_n)
    values(nk - 1, *bufs[(nk - 1) % 2][2:])

    def normalised(c):
        acc = acc_sc[c]
        return acc[:LANE] / acc[LANE:LANE + 1]

    o = normalised(0)
    if ncomp == 2:
        lam = (jnp.exp(jnp.sum(lq1[...] * lk1[...], keepdims=True))
               - jnp.exp(jnp.sum(lq2[...] * lk2[...], keepdims=True)) + lam_init)
        o = o - lam * normalised(1)
        o = o * lax.rsqrt(jnp.mean(o * o, axis=0, keepdims=True) + RMS_EPS) * g_ref[...] * (1.0 - lam_init)
    o_ref[...] = o.T


def _flash(q, k, vt, *, batch, seq, tq=512, diff=None, lam_init=0.0):
    nk, vrows, tk = vt.shape[2], vt.shape[3], vt.shape[4]
    assert vrows == LANE + ONES_ROWS
    tq = min(tq, seq)
    nq = seq // tq
    ncomp = 2 if diff is not None else 1
    in_specs = [
        pl.BlockSpec((None, None, LANE, tq), lambda b, h, i: (b, h, 0, i)),
        pl.BlockSpec((None, None, seq, LANE), lambda b, h, i: (b, h, 0, 0)),
        pl.BlockSpec((None, None, nk, vrows, tk), lambda b, h, i: (b, h, 0, 0, 0)),
    ]
    args = [q, k, vt]
    if diff is not None:
        lq1, lk1, lq2, lk2, subln = diff
        for v in (lq1, lk1, lq2, lk2):
            in_specs.append(pl.BlockSpec((1, DIFF_HEAD_DIM), lambda b, h, i: (0, 0)))
            args.append(v.reshape(1, DIFF_HEAD_DIM))
        in_specs.append(pl.BlockSpec((LANE, 1), lambda b, h, i: (0, 0)))
        args.append(subln.reshape(LANE, 1))
    return pl.pallas_call(
        functools.partial(_flash_kernel, ncomp=ncomp, nk=nk, lam_init=lam_init),
        grid=(batch, HEADS, nq),
        in_specs=in_specs,
        out_specs=pl.BlockSpec((tq, LANE), lambda b, h, i: (b * nq + i, h)),
        out_shape=jax.ShapeDtypeStruct((batch * seq, HEADS * LANE), F32),
        scratch_shapes=[pltpu.VMEM((ncomp, LANE, tq), BF16),
                        pltpu.VMEM((ncomp, 1, tq), F32), pltpu.VMEM((ncomp, vrows, tq), F32),
                        pltpu.VMEM((ncomp, tk, tq), F32), pltpu.VMEM((ncomp, tk, tq), F32),
                        pltpu.VMEM((ncomp, 1, tq), F32), pltpu.VMEM((ncomp, 1, tq), F32),
                        pltpu.VMEM((ncomp, tk, tq), BF16), pltpu.VMEM((ncomp, tk, tq), BF16),
                        pltpu.VMEM((ncomp, 1, tq), F32), pltpu.VMEM((ncomp, 1, tq), F32)],
        compiler_params=_cparams(("parallel", "parallel", "parallel")),
        name="flash_diff" if diff is not None else "flash_mla",
    )(*args)


def _ret_kernel(dec_ref, qf, kf, vf, qb, kb, vb, of_ref, ob_ref, s_sc, *, chunk):
    @pl.when(pl.program_id(1) == 0)
    def _():
        s_sc[...] = jnp.zeros(s_sc.shape, F32)

    ii = lax.broadcasted_iota(jnp.int32, (chunk, chunk), 0)
    jj = lax.broadcasted_iota(jnp.int32, (chunk, chunk), 1)
    r = lax.broadcasted_iota(jnp.int32, (chunk, 1), 0).astype(F32)
    for d, (q_ref, k_ref, v_ref, o_ref) in enumerate(((qf, kf, vf, of_ref), (qb, kb, vb, ob_ref))):
        for h in range(HEADS):
            la = -jnp.exp(jnp.full((1, 1), dec_ref[d, h], F32))
            if d == 0:
                mask, dist = ii >= jj, (ii - jj).astype(F32)
                qdec, kdec = jnp.exp(la * (r + 1.0)), jnp.exp(la * (chunk - 1.0 - r))
            else:
                mask, dist = jj > ii, (jj - ii).astype(F32)
                qdec, kdec = jnp.exp(la * (chunk - r)), jnp.exp(la * r)
            decay = jnp.where(mask, jnp.exp(jnp.where(mask, dist * la, 0.0)), 0.0)
            q, k, v = q_ref[h], k_ref[h], v_ref[h]
            s = lax.dot_general(q, k, (((1,), (1,)), ((), ())), preferred_element_type=F32)
            o = jnp.dot((s * decay).astype(BF16), v, preferred_element_type=F32)
            state = s_sc[d, h]
            o = o + qdec * jnp.dot(q, state.astype(BF16), preferred_element_type=F32)
            kd = (k.astype(F32) * kdec).astype(BF16)
            s_sc[d, h] = jnp.exp(la * float(chunk)) * state + lax.dot_general(
                kd, v, (((0,), (0,)), ((), ())), preferred_element_type=F32)
            o_ref[:, h * LANE:(h + 1) * LANE] = o


def _retention(q, k, v, decays, *, batch, seq, chunk=256):
    chunk = min(chunk, seq)
    n = seq // chunk
    fwd = pl.BlockSpec((None, HEADS, chunk, LANE), lambda b, c: (b, 0, c, 0))
    bwd = pl.BlockSpec((None, HEADS, chunk, LANE), lambda b, c: (b, 0, n - 1 - c, 0))
    w = HEADS * LANE
    out = jax.ShapeDtypeStruct((batch * seq, w), F32)
    return pl.pallas_call(
        functools.partial(_ret_kernel, chunk=chunk),
        grid=(batch, n),
        in_specs=[pl.BlockSpec(memory_space=pltpu.SMEM), fwd, fwd, fwd, bwd, bwd, bwd],
        out_specs=[pl.BlockSpec((chunk, w), lambda b, c: (b * n + c, 0)),
                   pl.BlockSpec((chunk, w), lambda b, c: (b * n + n - 1 - c, 0))],
        out_shape=[out, out],
        scratch_shapes=[pltpu.VMEM((2, HEADS, LANE, LANE), F32)],
        compiler_params=_cparams(("parallel", "arbitrary")),
        name="retention",
    )(decays, q, k, v, q, k, v)


GLA_SUB = 8


def _split3(x):
    x1 = x.astype(BF16)
    r1 = x - x1.astype(F32)
    x2 = r1.astype(BF16)
    x3 = (r1 - x2.astype(F32)).astype(BF16)
    return x1, x2, x3


def _gla_direction(q, k, v, lr, w2, bias, st, reverse):
    C, wk = q.shape
    wv = v.shape[1]
    dk, dv = wk // HEADS, wv // HEADS
    z = jnp.dot(lr.astype(BF16), w2, preferred_element_type=F32) + bias
    g = (jnp.minimum(z, 0.0) - jnp.log(1.0 + jnp.exp(-jnp.abs(z)))) * (LOG2E / GLA_TAU)
    ii = lax.broadcasted_iota(jnp.int32, (C, C), 0)
    jj = lax.broadcasted_iota(jnp.int32, (C, C), 1)
    tri = jnp.where(ii >= jj, 1.0, 0.0).astype(BF16)
    b = sum(jnp.dot(tri, part, preferred_element_type=F32) for part in _split3(g))
    tot = b[C - 1:C, :]
    c = (tot - b + g) if reverse else b

    qe = (q * jnp.exp2(jnp.minimum(c, 0.0))).astype(BF16)
    o = lax.dot_general(qe, st.astype(BF16), (((1,), (1,)), ((), ())), preferred_element_type=F32)
    ke = (k * jnp.exp2(jnp.minimum(tot - c, 0.0))).astype(BF16)
    upd = lax.dot_general(v.astype(BF16), ke, (((0,), (0,)), ((), ())), preferred_element_type=F32)
    rr = _div_pow2(lax.broadcasted_iota(jnp.int32, (wv, wk), 0), dv)
    cc = _div_pow2(lax.broadcasted_iota(jnp.int32, (wv, wk), 1), dk)
    new_st = jnp.where(rr == cc, st * jnp.exp2(tot) + upd, 0.0)

    lane_head = _div_pow2(lax.broadcasted_iota(jnp.int32, (C, wk), 1), dk)
    scores = [jnp.zeros((C, C), F32) for _ in range(HEADS)]
    hsz = C // 2
    while hsz >= GLA_SUB:
        blk = 2 * hsz
        rows = []
        for m in range(C // blk):
            rrow = m * blk + (hsz if reverse else hsz - 1)
            rows.append(jnp.broadcast_to(c[rrow:rrow + 1, :], (blk, wk)))
        ref = jnp.concatenate(rows, axis=0) if len(rows) > 1 else rows[0]
        qt = q * jnp.exp2(jnp.minimum(c - ref, 0.0))
        kt = (k * jnp.exp2(jnp.minimum(ref - c, 0.0))).astype(BF16)
        same = _div_pow2(ii, blk) == _div_pow2(jj, blk)
        if reverse:
            lvl = same & (_mod_pow2(ii, blk) < hsz) & (_mod_pow2(jj, blk) >= hsz)
        else:
            lvl = same & (_mod_pow2(ii, blk) >= hsz) & (_mod_pow2(jj, blk) < hsz)
        for h in range(HEADS):
            qh = jnp.where(lane_head == h, qt, 0.0).astype(BF16)
            s = lax.dot_general(qh, kt, (((1,), (1,)), ((), ())), preferred_element_type=F32)
            scores[h] = scores[h] + jnp.where(lvl, s, 0.0)
        hsz //= 2

    assert dv == C
    er = _div_pow2(lax.broadcasted_iota(jnp.int32, (wk, wv), 0), dk)
    ec = _div_pow2(lax.broadcasted_iota(jnp.int32, (wk, wv), 1), dv)
    expand = jnp.where(er == ec, 1.0, 0.0).astype(BF16)
    dist = (jj - ii) if reverse else (ii - jj)
    same_sub = _div_pow2(ii, GLA_SUB) == _div_pow2(jj, GLA_SUB)
    for lag in range(1 if reverse else 0, GLA_SUB):
        if lag == 0:
            t = q * k
        else:
            shift = (GLA_SUB - lag) if reverse else lag
            ks = pltpu.roll(k.reshape(C // GLA_SUB, GLA_SUB, wk), shift, axis=1).reshape(C, wk)
            cs = pltpu.roll(c.reshape(C // GLA_SUB, GLA_SUB, wk), shift, axis=1).reshape(C, wk)
            t = q * ks * jnp.exp2(jnp.minimum(c - cs, 0.0))
        red = jnp.dot(t.astype(BF16), expand, preferred_element_type=F32)
        on_diag = same_sub & (dist == lag)
        for h in range(HEADS):
            scores[h] = scores[h] + jnp.where(on_diag, red[:, h * dv:(h + 1) * dv], 0.0)

    vb = v.astype(BF16)
    o = o + jnp.concatenate(
        [jnp.dot(scores[h].astype(BF16), vb[:, h * dv:(h + 1) * dv], preferred_element_type=F32) for h in range(HEADS)],
        axis=1)
    return o, new_st


def _gla_kernel(qf, kf, vf, lf, qb, kb, vb, lb, w2f, bf, w2b, bb, of_ref, ob_ref, s_sc, *, qscale):
    @pl.when(pl.program_id(1) == 0)
    def _():
        s_sc[...] = jnp.zeros(s_sc.shape, F32)

    o, st = _gla_direction(qf[...] * qscale, kf[...], vf[...], lf[...], w2f[...], bf[...], s_sc[0], False)
    of_ref[...] = o
    s_sc[0] = st
    o, st = _gla_direction(qb[...] * qscale, kb[...], vb[...], lb[...], w2b[...], bb[...], s_sc[1], True)
    ob_ref[...] = o
    s_sc[1] = st


def _gla(z, *, qcol, kcol, vcol, lcol, w2f, bf, w2b, bb, batch, seq, chunk=128):
    chunk = min(chunk, seq)
    n = seq // chunk
    wk, wv = HEADS * GLA_K_DIM, HEADS * LANE

    def specs(cmap):
        return [pl.BlockSpec((chunk, wk), lambda b, c: (cmap(b, c), qcol)),
                pl.BlockSpec((chunk, wk), lambda b, c: (cmap(b, c), kcol)),
                pl.BlockSpec((chunk, wv), lambda b, c: (cmap(b, c), vcol)),
                pl.BlockSpec((chunk, LANE), lambda b, c: (cmap(b, c), lcol))]

    fmap = lambda b, c: b * n + c
    bmap = lambda b, c: b * n + n - 1 - c
    wspec = [pl.BlockSpec((LANE, wk), lambda b, c: (0, 0)), pl.BlockSpec((1, wk), lambda b, c: (0, 0))]
    out = jax.ShapeDtypeStruct((batch * seq, wv), F32)
    return pl.pallas_call(
        functools.partial(_gla_kernel, qscale=GLA_K_DIM ** -0.5),
        grid=(batch, n),
        in_specs=specs(fmap) + specs(bmap) + wspec + wspec,
        out_specs=[pl.BlockSpec((chunk, wv), lambda b, c: (fmap(b, c), 0)),
                   pl.BlockSpec((chunk, wv), lambda b, c: (bmap(b, c), 0))],
        out_shape=[out, out],
        scratch_shapes=[pltpu.VMEM((2, wv, wk), F32)],
        compiler_params=_cparams(("parallel", "arbitrary")),
        name="gla",
    )(z, z, z, z, z, z, z, z, w2f, bf, w2b, bb)


def _layer_norm_rows(r, g, b):
    mu = jnp.mean(r, axis=-1, keepdims=True)
    d = r - mu
    var = jnp.mean(d * d, axis=-1, keepdims=True)
    return d * lax.rsqrt(var + LN_EPS) * g + b


def _outproj_kernel(x_ref, fin_ref, of_ref, ob_ref, gate_ref, ng_ref, wa_ref, wb_ref, lg_ref, lb_ref, rw_ref, rb_ref,
                    o_ref, ids_ref, gw_ref, cnt_ref, run_sc, *, group_norm):
    lin = of_ref[...] + ob_ref[...]
    parts = []
    for h in range(HEADS):
        zh = lin[:, h * LANE:(h + 1) * LANE]
        if group_norm:
            mu = jnp.mean(zh, axis=-1, keepdims=True)
            dz = zh - mu
            parts.append(dz * lax.rsqrt(jnp.mean(dz * dz, axis=-1, keepdims=True) + LN_EPS))
        else:
            parts.append(zh * lax.rsqrt(jnp.mean(zh * zh, axis=-1, keepdims=True) + RMS_EPS) * ng_ref[...])
    gate = gate_ref[...]
    lin = jnp.concatenate(parts, axis=1) * (gate * jax.nn.sigmoid(gate))
    y = (jnp.dot(fin_ref[...].astype(BF16), wa_ref[...], preferred_element_type=F32)
         + jnp.dot(lin.astype(BF16), wb_ref[...], preferred_element_type=F32))
    x1 = _layer_norm_rows(ALPHA * x_ref[...] + y, lg_ref[...], lb_ref[...])
    o_ref[...] = x1
    _route_rows(x1, rw_ref, rb_ref, ids_ref, gw_ref, cnt_ref, run_sc)


def _outproj(x, fin, of, ob, gate_src, gate_col, norm_gain, wa, wb, ln_g, ln_b, route_w, route_b, *, group_norm, tm=512):
    n, d = x.shape
    w = HEADS * LANE
    tm = min(tm, n)
    row = lambda i: (i, 0)
    const = lambda i: (0, 0)
    rw_hi = route_w.astype(BF16)
    rw = jnp.stack([rw_hi, (route_w - rw_hi.astype(F32)).astype(BF16)])
    return pl.pallas_call(
        functools.partial(_outproj_kernel, group_norm=group_norm),
        grid=(n // tm,),
        in_specs=[pl.BlockSpec((tm, d), row), pl.BlockSpec((tm, w), row), pl.BlockSpec((tm, w), row),
                  pl.BlockSpec((tm, w), row), pl.BlockSpec((tm, w), lambda i: (i, gate_col)),
                  pl.BlockSpec((1, LANE), const), pl.BlockSpec((w, d), const), pl.BlockSpec((w, d), const),
                  pl.BlockSpec((1, d), const), pl.BlockSpec((1, d), const),
                  pl.BlockSpec((2, d, LANE), lambda i: (0, 0, 0)), pl.BlockSpec((1, LANE), const)],
        out_specs=[pl.BlockSpec((tm, d), row), pl.BlockSpec((tm, LANE), row), pl.BlockSpec((tm, LANE), row),
                   pl.BlockSpec((1, LANE), const)],
        out_shape=[jax.ShapeDtypeStruct((n, d), F32), jax.ShapeDtypeStruct((n, LANE), jnp.int32),
                   jax.ShapeDtypeStruct((n, LANE), F32), jax.ShapeDtypeStruct((1, LANE), jnp.int32)],
        scratch_shapes=[pltpu.VMEM((1, LANE), F32)],
        compiler_params=_cparams(("arbitrary",)),
        name="outproj",
    )(x, fin, of, ob, gate_src, norm_gain.reshape(1, LANE), wa, wb, ln_g.reshape(1, d), ln_b.reshape(1, d), rw, route_b)


def _route_rows(x, w_ref, b_ref, ids_ref, gw_ref, cnt_ref, run_sc):
    @pl.when(pl.program_id(0) == 0)
    def _():
        run_sc[...] = jnp.zeros(run_sc.shape, F32)

    tm = x.shape[0]
    xh = x.astype(BF16)
    xl = (x - xh.astype(F32)).astype(BF16)
    wh, wl = w_ref[0], w_ref[1]
    logits = (jnp.dot(xh, wh, preferred_element_type=F32) + jnp.dot(xh, wl, preferred_element_type=F32)
              + jnp.dot(xl, wh, preferred_element_type=F32)) + b_ref[...]
    lane = lax.broadcasted_iota(jnp.int32, logits.shape, 1)
    neg = -jnp.inf
    gmask = (lane >= N_EXPERTS) & (lane < N_EXPERTS + N_GROUPS)
    gl = jnp.where(gmask, logits, neg)
    gmax = jnp.max(gl, axis=1, keepdims=True)
    lane_f = lane.astype(F32)
    first = lambda hit: jnp.min(jnp.where(hit, lane_f, float(LANE)), axis=1, keepdims=True).astype(jnp.int32)
    gidx = first(gl == gmax) - N_EXPERTS
    p_grp = 1.0 / jnp.sum(jnp.where(gmask, jnp.exp(gl - gmax), 0.0), axis=1, keepdims=True)
    el = jnp.where(_div_pow2(lane, EXPERTS_PER_GROUP) == gidx, logits, neg)
    l1 = jnp.max(el, axis=1, keepdims=True)
    e1 = first(el == l1)
    el2 = jnp.where(lane == e1, neg, el)
    l2 = jnp.max(el2, axis=1, keepdims=True)
    e2 = first(el2 == l2)
    t = jnp.exp(l2 - l1)
    w1 = p_grp / (1.0 + t)
    w2 = p_grp * t / (1.0 + t)

    onehot = jnp.where(lane == e1, 1.0, jnp.where(lane == e2, 1.0, 0.0))
    ri = lax.broadcasted_iota(jnp.int32, (tm, tm), 0)
    ci = lax.broadcasted_iota(jnp.int32, (tm, tm), 1)
    before = jnp.dot(jnp.where(ri > ci, 1.0, 0.0).astype(BF16), onehot.astype(BF16), preferred_element_type=F32)
    before = before + run_sc[...]
    r1 = jnp.sum(jnp.where(lane == e1, before, 0.0), axis=1, keepdims=True).astype(jnp.int32)
    r2 = jnp.sum(jnp.where(lane == e2, before, 0.0), axis=1, keepdims=True).astype(jnp.int32)
    run_sc[...] = run_sc[...] + jnp.sum(onehot, axis=0, keepdims=True)
    cnt_ref[...] = run_sc[...].astype(jnp.int32)
    ids_ref[...] = jnp.where(lane == 0, e1, jnp.where(lane == 1, e2, jnp.where(lane == 2, r1, jnp.where(lane == 3, r2, 0))))
    gw_ref[...] = jnp.where(lane == 0, w1, jnp.where(lane == 1, w2, 0.0))


MOE_BM = 256
SUBL = 8


def _rows_from_linear(ref, rows):
    return jnp.concatenate([ref[pl.ds(s, rows, stride=SUBL), :] for s in range(SUBL)], axis=1)


def _rows_to_linear(ref, val):
    for s in range(SUBL):
        ref[pl.ds(s, val.shape[0], stride=SUBL), :] = val[:, s * LANE:(s + 1) * LANE]


def _dispatch_kernel(pend_ref, padded_ref, dest_ref, x_ref, xs_hbm, idx_smem, zbuf, lin, sem_i, sem_z, sem):
    i = pl.program_id(0)
    tm = x_ref.shape[0]
    bm = zbuf.shape[0] // SUBL

    @pl.when(i == 0)
    def _():
        zbuf[...] = jnp.zeros(zbuf.shape, F32)

        def tail(e):
            start_row = pl.multiple_of((pend_ref[e] - bm) * SUBL, bm * SUBL)
            return pltpu.make_async_copy(zbuf, xs_hbm.at[pl.ds(start_row, bm * SUBL), :], sem_z)

        def start(e, carry):
            @pl.when(padded_ref[e] > 0)
            def _():
                tail(e).start()
            return carry

        def wait(e, carry):
            @pl.when(padded_ref[e] > 0)
            def _():
                tail(e).wait()
            return carry

        lax.fori_loop(0, N_EXPERTS, start, 0)
        lax.fori_loop(0, N_EXPERTS, wait, 0)

        def unused(b):
            start_row = pl.multiple_of(b * bm * SUBL, bm * SUBL)
            return pltpu.make_async_copy(zbuf, xs_hbm.at[pl.ds(start_row, bm * SUBL), :], sem_z)

        first_unused = pend_ref[N_EXPERTS - 1] // bm
        n_blocks = xs_hbm.shape[0] // (bm * SUBL)
        lax.fori_loop(first_unused, n_blocks, lambda b, c: (unused(b).start(), c)[1], 0)
        lax.fori_loop(first_unused, n_blocks, lambda b, c: (unused(b).wait(), c)[1], 0)

    cp = pltpu.make_async_copy(dest_ref.at[i], idx_smem, sem_i)
    cp.start()
    _rows_to_linear(lin, x_ref[...])
    cp.wait()

    def scatter(r, carry):
        src = lin.at[pl.ds(pl.multiple_of(r * SUBL, SUBL), SUBL), :]
        for k in range(2):
            dst = pl.multiple_of(idx_smem[2 * r + k] * SUBL, SUBL)
            pltpu.make_async_copy(src, xs_hbm.at[pl.ds(dst, SUBL), :], sem).start(priority=k)
        return carry

    lax.fori_loop(0, tm, scatter, 0, unroll=8)
    for k in range(2):
        pltpu.make_async_copy(lin, xs_hbm.at[pl.ds(0, tm * SUBL), :], sem).wait()


def _dispatch(x, dest, pend, padded, cap, *, tm=512):
    n, d = x.shape
    assert d == SUBL * LANE
    tm = min(tm, n)
    nt = n // tm
    grid_spec = pltpu.PrefetchScalarGridSpec(
        num_scalar_prefetch=2,
        grid=(nt,),
        in_specs=[pl.BlockSpec((nt, 2 * tm), lambda i, pe, pa: (0, 0)),
                  pl.BlockSpec((tm, d), lambda i, pe, pa: (i, 0))],
        out_specs=pl.BlockSpec(memory_space=pl.ANY),
        scratch_shapes=[pltpu.SMEM((2 * tm,), jnp.int32), pltpu.VMEM((MOE_BM * SUBL, LANE), F32),
                        pltpu.VMEM((tm * SUBL, LANE), F32),
                        pltpu.SemaphoreType.DMA(()), pltpu.SemaphoreType.DMA(()), pltpu.SemaphoreType.DMA(())],
    )
    return pl.pallas_call(
        _dispatch_kernel,
        grid_spec=grid_spec,
        out_shape=jax.ShapeDtypeStruct((cap * SUBL, LANE), F32),
        compiler_params=_cparams(("arbitrary",)),
        name="dispatch",
    )(pend, padded, dest.reshape(nt, 2 * tm), x)


def _experts_kernel(blk_e_ref, nused_ref, xs_ref, wg_ref, wu_ref, wd_ref, ys_ref, wgb, wub, wdb, hbuf):
    i = pl.program_id(0)
    nb = pl.num_programs(0) - 1
    cur = jnp.minimum(i, nb - 1)
    prv = jnp.maximum(i - 1, 0)

    @pl.when(i == 0)
    def _():
        hbuf[...] = jnp.zeros(hbuf.shape, BF16)

    @pl.when(jnp.logical_or(i == 0, blk_e_ref[cur] != blk_e_ref[jnp.minimum(prv, nb - 1)]))
    def _():
        wgb[...] = wg_ref[...].astype(BF16)
        wub[...] = wu_ref[...].astype(BF16)

    @pl.when(jnp.logical_or(i == 0, blk_e_ref[prv] != blk_e_ref[jnp.maximum(i - 2, 0)]))
    def _():
        wdb[...] = wd_ref[...].astype(BF16)

    slot = lax.rem(i, 2)
    y = jnp.dot(hbuf[1 - slot], wdb[...], preferred_element_type=F32)
    live = jnp.logical_and(i >= 1, i - 1 < nused_ref[0])
    _rows_to_linear(ys_ref, jnp.where(live, y, 0.0))
    xb = _rows_from_linear(xs_ref, xs_ref.shape[0] // SUBL).astype(BF16)
    hg = jnp.dot(xb, wgb[...], preferred_element_type=F32)
    hu = jnp.dot(xb, wub[...], preferred_element_type=F32)
    hbuf[slot] = (hg * jax.nn.sigmoid(hg) * hu).astype(BF16)


def _experts(xs, blk_e, n_used, w_gate, w_up, w_down, layer):
    cap = xs.shape[0] // SUBL
    bm = MOE_BM
    nb = cap // bm
    d, de = w_gate.shape[2], w_gate.shape[3]
    row_in = lambda i, be, nu: (jnp.minimum(i, nu[0] - 1), 0)
    row = lambda i, be, nu: (jnp.maximum(i - 1, 0), 0)
    w_cur = lambda i, be, nu: (layer, be[jnp.minimum(i, nb - 1)], 0, 0)
    w_prv = lambda i, be, nu: (layer, be[jnp.maximum(i - 1, 0)], 0, 0)
    grid_spec = pltpu.PrefetchScalarGridSpec(
        num_scalar_prefetch=2,
        grid=(nb + 1,),
        in_specs=[pl.BlockSpec((bm * SUBL, LANE), row_in),
                  pl.BlockSpec((None, None, d, de), w_cur),
                  pl.BlockSpec((None, None, d, de), w_cur),
                  pl.BlockSpec((None, None, de, d), w_prv)],
        out_specs=pl.BlockSpec((bm * SUBL, LANE), row),
        scratch_shapes=[pltpu.VMEM((d, de), BF16), pltpu.VMEM((d, de), BF16), pltpu.VMEM((de, d), BF16),
                        pltpu.VMEM((2, bm, de), BF16)],
    )
    return pl.pallas_call(
        _experts_kernel,
        grid_spec=grid_spec,
        out_shape=jax.ShapeDtypeStruct((cap * SUBL, LANE), F32),
        compiler_params=_cparams(("arbitrary",)),
        name="experts",
    )(blk_e, n_used, xs, w_gate, w_up, w_down)


N_ODD_IN, N_ODD_OUT = 8, 4


def _combine_kernel(dest_ref, x_ref, gw_ref, g_ref, b_ref, ys_hbm, *rest, fuse_next):
    if fuse_next:
        nxt_in, rest = rest[:N_ODD_IN], rest[N_ODD_IN:]
        o_ref, nxt_out, rest = rest[0], rest[1:1 + N_ODD_OUT], rest[1 + N_ODD_OUT:]
    else:
        o_ref, rest = rest[0], rest[1:]
    idx_smem, ybuf, sem_i, sem = rest
    i = pl.program_id(0)
    tm = x_ref.shape[0]

    def issue(tile, slot):
        cp = pltpu.make_async_copy(dest_ref.at[tile], idx_smem, sem_i)
        cp.start()
        cp.wait()

        def gather(r, carry):
            row = pl.multiple_of(r * SUBL, SUBL)
            for k in range(2):
                src = pl.multiple_of(idx_smem[2 * r + k] * SUBL, SUBL)
                pltpu.make_async_copy(ys_hbm.at[pl.ds(src, SUBL), :], ybuf.at[slot, k, pl.ds(row, SUBL), :],
                                      sem.at[slot]).start(priority=k)
            return carry

        lax.fori_loop(0, tm, gather, 0, unroll=8)

    slot = lax.rem(i, 2)

    @pl.when(i == 0)
    def _():
        issue(0, 0)

    @pl.when(i + 1 < pl.num_programs(0))
    def _():
        issue(i + 1, 1 - slot)

    for k in range(2):
        pltpu.make_async_copy(ys_hbm.at[pl.ds(0, tm * SUBL), :], ybuf.at[slot, k], sem.at[slot]).wait()
    gw = gw_ref[...]
    ffn = (_rows_from_linear(ybuf.at[slot, 0], tm) * gw[:, 0:1] + _rows_from_linear(ybuf.at[slot, 1], tm) * gw[:, 1:2])
    x2 = _layer_norm_rows(ALPHA * x_ref[...] + ffn, g_ref[...], b_ref[...])
    o_ref[...] = x2
    if fuse_next:
        _odd_in_rows(x2, *nxt_in, *nxt_out)


def _combine(x, ys, dest, gw, ln_g, ln_b, next_odd=None, *, batch=None, seq=None):
    n, d = x.shape
    tm = min(ODD_TM, n)
    nt = n // tm
    in_specs = [pl.BlockSpec((nt, 2 * tm), lambda i: (0, 0)),
                pl.BlockSpec((tm, d), lambda i: (i, 0)), pl.BlockSpec((tm, LANE), lambda i: (i, 0)),
                pl.BlockSpec((1, d), lambda i: (0, 0)), pl.BlockSpec((1, d), lambda i: (0, 0)),
                pl.BlockSpec(memory_space=pl.ANY)]
    operands = [dest.reshape(nt, 2 * tm), x, gw, ln_g.reshape(1, d), ln_b.reshape(1, d), ys]
    out_specs = [pl.BlockSpec((tm, d), lambda i: (i, 0))]
    out_shape = [jax.ShapeDtypeStruct((n, d), F32)]
    if next_odd is not None:
        tm_odd, nxt_operands, nxt_in_specs, nxt_out_specs, nxt_out_shape = _odd_in_plumbing(next_odd, n, batch, seq)
        assert tm_odd == tm and len(nxt_operands) == N_ODD_IN and len(nxt_out_shape) == N_ODD_OUT
        in_specs, operands = in_specs + nxt_in_specs, operands + list(nxt_operands)
        out_specs, out_shape = out_specs + nxt_out_specs, out_shape + nxt_out_shape
    outs = pl.pallas_call(
        functools.partial(_combine_kernel, fuse_next=next_odd is not None),
        grid=(nt,),
        in_specs=in_specs,
        out_specs=out_specs,
        out_shape=out_shape,
        scratch_shapes=[pltpu.SMEM((2 * tm,), jnp.int32), pltpu.VMEM((2, 2, tm * SUBL, LANE), F32),
                        pltpu.SemaphoreType.DMA(()), pltpu.SemaphoreType.DMA((2,))],
        compiler_params=_cparams(("arbitrary",)),
        name="combine",
    )(*operands)
    return outs[0], (tuple(outs[1:]) if next_odd is not None else None)


def _router_params(w_grp, b_grp, w_exp, b_exp):
    d = w_exp.shape[0]
    wr = jnp.zeros((d, LANE), F32).at[:, :N_EXPERTS].set(w_exp).at[:, N_EXPERTS:N_EXPERTS + N_GROUPS].set(w_grp)
    br = jnp.zeros((1, LANE), F32).at[0, :N_EXPERTS].set(b_exp).at[0, N_EXPERTS:N_EXPERTS + N_GROUPS].set(b_grp)
    return wr, br


def _moe(x, routing, w_gate, w_up, w_down, layer, ln_g, ln_b, next_odd, batch, seq):
    n, d = x.shape
    ids, gw, cnt = routing
    bm = MOE_BM
    counts = cnt[0, :N_EXPERTS]
    padded = (counts + bm - 1) // bm * bm
    pend = jnp.cumsum(padded)
    pstart = pend - padded
    e, r = ids[:, 0:2], ids[:, 2:4]
    onehot = e[:, :, None] == jnp.arange(N_EXPERTS, dtype=jnp.int32)[None, None, :]
    dest = jnp.sum(jnp.where(onehot, pstart[None, None, :], 0), axis=-1) + r
    cap = 2 * n + N_EXPERTS * bm
    nb = cap // bm
    blk_start = jnp.arange(nb, dtype=jnp.int32) * bm
    blk_e = jnp.minimum(jnp.sum((pend[None, :] <= blk_start[:, None]).astype(jnp.int32), axis=1), N_EXPERTS - 1)
    n_used = (pend[-1:] // bm).astype(jnp.int32)
    xs = _dispatch(x, dest, pend.astype(jnp.int32), padded.astype(jnp.int32), cap)
    ys = _experts(xs, blk_e, n_used, w_gate, w_up, w_down, layer)
    return _combine(x, ys, dest, gw, ln_g, ln_b, next_odd, batch=batch, seq=seq)


def _even_layer(x, batch, seq, layer_idx, w_in, dec_f, dec_b, lq1, lk1, lq2, lk2, subln, w_out, ln_g, ln_b, route):
    d = x.shape[1]
    w = HEADS * LANE
    kw = dict(batch=batch, seq=seq)
    diff_seg = [(0, DIFF_ROT_DIM), (DIFF_HEAD_DIM, DIFF_ROT_DIM)]
    q, k, v, gate, dq, dk, dvt = _even_in(
        x, w_in.astype(BF16), _rope_tables(seq, [(0, LANE)], RET_THETA),
        _rope_tables(seq, diff_seg, ROPE_THETA), **kw)
    decays = jnp.stack([dec_f, dec_b]).astype(F32)
    of, ob = _retention(q, k, v, decays, **kw)
    lam_init = 0.8 - 0.6 * math.exp(-0.3 * layer_idx)
    diff = _flash(dq, dk, dvt, diff=(lq1, lk1, lq2, lk2, subln), lam_init=lam_init, tq=2048, **kw)
    wo = w_out.astype(BF16)
    return _outproj(x, diff, of, ob, gate, 0, jnp.ones((LANE,), F32), wo[w:], wo[:w], ln_g, ln_b, *route,
                    group_norm=True)


def _odd_in_args(seq, w_in, q_norm, w_uq, kv_norm, w_ukv):
    d = w_in.shape[0]
    w = HEADS * LANE
    o = np.cumsum([0, MLA_Q_RANK, MLA_KV_RANK, MLA_ROPE, HEADS * GLA_K_DIM, HEADS * GLA_K_DIM, w, w,
                   GLA_GATE_RANK, GLA_GATE_RANK]).tolist()
    zeros = lambda c: jnp.zeros((d, c), F32)
    w_in2 = jnp.concatenate([
        w_in[:, o[0]:o[2]], zeros(MLA_NOPE), w_in[:, o[2]:o[3]], zeros(LANE - MLA_NOPE - MLA_ROPE),
        w_in[:, o[3]:o[7]], w_in[:, o[7]:o[9]], zeros(LANE - 2 * GLA_GATE_RANK)], axis=1).astype(BF16)
    qd = MLA_NOPE + MLA_ROPE
    w_uq2 = jnp.pad(w_uq.reshape(MLA_Q_RANK, HEADS, qd), ((0, 0), (0, 0), (0, LANE - qd))).reshape(MLA_Q_RANK, w)
    ukv = w_ukv.reshape(MLA_KV_RANK, HEADS, MLA_NOPE + MLA_V)
    w_uk2 = jnp.pad(ukv[:, :, :MLA_NOPE], ((0, 0), (0, 0), (0, LANE - MLA_NOPE))).reshape(MLA_KV_RANK, w)
    w_uv2 = ukv[:, :, MLA_NOPE:].reshape(MLA_KV_RANK, w)
    return (w_in2, q_norm, w_uq2.astype(BF16), kv_norm, jnp.concatenate([w_uk2, w_uv2], axis=1).astype(BF16),
            _rope_tables(seq, [(MLA_NOPE, MLA_ROPE)], ROPE_THETA))


def _odd_layer(x, projected, batch, seq, w2_f, b_f, w2_b, b_b, gla_norm, w_out, ln_g, ln_b, route):
    w = HEADS * LANE
    kw = dict(batch=batch, seq=seq)
    q, k, vt, zg = projected
    mla = _flash(q, k, vt, **kw)
    wk = HEADS * GLA_K_DIM
    pad_rows = lambda m, r0: jnp.zeros((LANE, wk), F32).at[r0:r0 + GLA_GATE_RANK].set(m).astype(BF16)
    of, ob = _gla(zg, qcol=0, kcol=1, vcol=1, lcol=12,
                  w2f=pad_rows(w2_f, 0), bf=b_f.reshape(1, wk), w2b=pad_rows(w2_b, GLA_GATE_RANK), bb=b_b.reshape(1, wk), **kw)
    wo = w_out.astype(BF16)
    return _outproj(x, mla, of, ob, zg, 2, gla_norm, wo[:w], wo[w:], ln_g, ln_b, *route, group_norm=False)


def kernel(x, ev_w_in, ev_ret_decay_f, ev_ret_decay_b, ev_lq1, ev_lk1, ev_lq2, ev_lk2, ev_subln, ev_w_out, od_w_in, od_q_norm, od_w_uq, od_kv_norm, od_w_ukv, od_gla_w2_f, od_gla_b_f, od_gla_w2_b, od_gla_b_b, od_gla_norm, od_w_out, ln1_g, ln1_b, ln2_g, ln2_b, moe_w_grp, moe_b_grp, moe_w_exp, moe_b_exp, moe_w_gate, moe_w_up, moe_w_down):
    batch, seq, d = x.shape
    h = x.reshape(batch * seq, d)
    odd_args = lambda j: _odd_in_args(seq, od_w_in[j], od_q_norm[j], od_w_uq[j], od_kv_norm[j], od_w_ukv[j])
    projected = None
    for i in range(DEPTH):
        j = i // 2
        route = _router_params(moe_w_grp[i], moe_b_grp[i], moe_w_exp[i], moe_b_exp[i])
        if i % 2 == 0:
            h, *routing = _even_layer(h, batch, seq, i, ev_w_in[j], ev_ret_decay_f[j], ev_ret_decay_b[j], ev_lq1[j],
                                      ev_lk1[j], ev_lq2[j], ev_lk2[j], ev_subln[j], ev_w_out[j], ln1_g[i], ln1_b[i], route)
        else:
            if projected is None:
                projected = _odd_in(h, odd_args(j), batch=batch, seq=seq)
            h, *routing = _odd_layer(h, projected, batch, seq, od_gla_w2_f[j], od_gla_b_f[j], od_gla_w2_b[j],
                                     od_gla_b_b[j], od_gla_norm[j], od_w_out[j], ln1_g[i], ln1_b[i], route)
        next_odd = odd_args((i + 1) // 2) if (i + 1 < DEPTH and (i + 1) % 2 == 1) else None
        h, projected = _moe(h, routing, moe_w_gate, moe_w_up, moe_w_down, i, ln2_g[i], ln2_b[i], next_odd, batch, seq)
    return h.reshape(batch, seq, d)
```

```python
import functools
import math

import numpy as np
import jax
import jax.numpy as jnp
from jax import lax
from jax.experimental import pallas as pl
from jax.experimental.pallas import tpu as pltpu

F32 = jnp.float32
BF16 = jnp.bfloat16

HEADS = 4
LANE = 128
RET_THETA = 10000.0
ROPE_THETA = 500000.0
DIFF_HEAD_DIM = 64
DIFF_ROT_DIM = 16
MLA_Q_RANK = 256
MLA_KV_RANK = 128
MLA_NOPE = 64
MLA_ROPE = 32
MLA_V = 128
GLA_K_DIM = 64
GLA_GATE_RANK = 16
GLA_TAU = 16.0
N_GROUPS = 4
EXPERTS_PER_GROUP = 8
N_EXPERTS = N_GROUPS * EXPERTS_PER_GROUP
DEPTH = 2
ALPHA = (2.0 * DEPTH) ** 0.25
LN_EPS = 1e-5
RMS_EPS = 1e-6

VMEM_LIMIT = 48 * 1024 * 1024


def _div_pow2(x, n):
    return lax.shift_right_logical(x, int(n).bit_length() - 1)


def _mod_pow2(x, n):
    return lax.bitwise_and(x, int(n) - 1)


def _cparams(sem):
    return pltpu.CompilerParams(dimension_semantics=sem, vmem_limit_bytes=VMEM_LIMIT)


def _rope_heads(z, tabs, sh, scale):
    c, sa, sb = tabs
    outs = []
    for h in range(HEADS):
        zh = z[:, h * LANE:(h + 1) * LANE]
        outs.append((zh * c + pltpu.roll(zh, sh, axis=1) * sa + pltpu.roll(zh, LANE - sh, axis=1) * sb) * scale)
    return outs


def _even_in_kernel(x_ref, w_ref, rc, rsa, rsb, dc, dsa, dsb, q_ref, k_ref, v_ref, g_ref, dq_ref, dk_ref, dvt_ref):
    w = HEADS * LANE
    xb = x_ref[...].astype(BF16)
    part = lambda t: jnp.dot(xb, w_ref[:, t * w:(t + 1) * w], preferred_element_type=F32)
    ret_t = (rc[...], rsa[...], rsb[...])
    diff_t = (dc[...], dsa[...], dsb[...])
    for h, o in enumerate(_rope_heads(part(0), ret_t, LANE // 2, 1.0)):
        q_ref[h] = o.astype(BF16)
    for h, o in enumerate(_rope_heads(part(1), ret_t, LANE // 2, LANE ** -0.5)):
        k_ref[h] = o.astype(BF16)
    rv = part(2)
    for h in range(HEADS):
        v_ref[h] = rv[:, h * LANE:(h + 1) * LANE].astype(BF16)
    g_ref[...] = part(3)
    for h, o in enumerate(_rope_heads(part(4), diff_t, DIFF_ROT_DIM // 2, DIFF_HEAD_DIM ** -0.5 * LOG2E)):
        dq_ref[h] = o.T.astype(BF16)
    for h, o in enumerate(_rope_heads(part(5), diff_t, DIFF_ROT_DIM // 2, 1.0)):
        dk_ref[h] = o.astype(BF16)
    dv = part(6)
    for h in range(HEADS):
        dvt_ref[h, :LANE, :] = dv[:, h * LANE:(h + 1) * LANE].T.astype(BF16)
        dvt_ref[h, LANE:, :] = jnp.ones((ONES_ROWS, dvt_ref.shape[2]), BF16)


def _even_in(x, w, ret_tabs, diff_tabs, *, batch, seq):
    n, d = x.shape
    tm = min(FLASH_TK_DIFF, seq // 2)
    nt = seq // tm
    hw = HEADS * LANE
    heads = jax.ShapeDtypeStruct((batch, HEADS, seq, LANE), BF16)
    head_spec = pl.BlockSpec((None, HEADS, tm, LANE), lambda i: (i // nt, 0, i % nt, 0))
    heads_t = jax.ShapeDtypeStruct((batch, HEADS, LANE, seq), BF16)
    head_t_spec = pl.BlockSpec((None, HEADS, LANE, tm), lambda i: (i // nt, 0, 0, i % nt))
    tab_spec = pl.BlockSpec((tm, LANE), lambda i: (i % nt, 0))
    return pl.pallas_call(
        _even_in_kernel,
        grid=(n // tm,),
        in_specs=[pl.BlockSpec((tm, d), lambda i: (i, 0)), pl.BlockSpec((d, 7 * hw), lambda i: (0, 0))] + [tab_spec] * 6,
        out_specs=[head_spec, head_spec, head_spec, pl.BlockSpec((tm, hw), lambda i: (i, 0)), head_t_spec, head_spec,
                   pl.BlockSpec((None, HEADS, None, LANE + ONES_ROWS, tm), lambda i: (i // nt, 0, i % nt, 0, 0))],
        out_shape=[heads, heads, heads, jax.ShapeDtypeStruct((n, hw), F32), heads_t, heads,
                   jax.ShapeDtypeStruct((batch, HEADS, nt, LANE + ONES_ROWS, tm), BF16)],
        compiler_params=_cparams(("parallel",)),
        name="even_in",
    )(x, w, *ret_tabs, *diff_tabs)


def _rms_rows(z, g):
    return z * lax.rsqrt(jnp.mean(z * z, axis=-1, keepdims=True) + RMS_EPS) * g


def _odd_in_kernel(x_ref, w_ref, qn_ref, wq_ref, kvn_ref, wkv_ref, tc, tsa, tsb, q_ref, k_ref, vt_ref, zg_ref):
    hw = HEADS * LANE
    mla_w = MLA_Q_RANK + MLA_KV_RANK + LANE
    xb = x_ref[...].astype(BF16)
    zg_ref[...] = jnp.dot(xb, w_ref[:, mla_w:], preferred_element_type=F32)
    z1 = jnp.dot(xb, w_ref[:, :mla_w], preferred_element_type=F32)
    tabs = (tc[...], tsa[...], tsb[...])
    sh = MLA_ROPE // 2
    qh = jnp.dot(_rms_rows(z1[:, :MLA_Q_RANK], qn_ref[...]).astype(BF16), wq_ref[...], preferred_element_type=F32)
    for h, o in enumerate(_rope_heads(qh, tabs, sh, (MLA_NOPE + MLA_ROPE) ** -0.5 * LOG2E)):
        q_ref[h] = o.T.astype(BF16)
    ckv = _rms_rows(z1[:, MLA_Q_RANK:MLA_Q_RANK + MLA_KV_RANK], kvn_ref[...]).astype(BF16)
    kv = jnp.dot(ckv, wkv_ref[...], preferred_element_type=F32)
    kr = z1[:, MLA_Q_RANK + MLA_KV_RANK:]
    kr = kr * tabs[0] + pltpu.roll(kr, sh, axis=1) * tabs[1] + pltpu.roll(kr, LANE - sh, axis=1) * tabs[2]
    for h in range(HEADS):
        k_ref[h] = (kv[:, h * LANE:(h + 1) * LANE] + kr).astype(BF16)
        vt_ref[h, :LANE, :] = kv[:, hw + h * LANE:hw + (h + 1) * LANE].T.astype(BF16)
        vt_ref[h, LANE:, :] = jnp.ones((ONES_ROWS, vt_ref.shape[2]), BF16)


def _odd_in(x, w, q_norm, w_uq, kv_norm, w_ukv, tabs, *, batch, seq, tm=512):
    n, d = x.shape
    tm = min(tm, seq // 2)
    nt = seq // tm
    tk = min(FLASH_TK, seq // 2)
    per = tk // tm
    hw = HEADS * LANE
    gw_ = w.shape[1] - (MLA_Q_RANK + MLA_KV_RANK + LANE)
    heads = jax.ShapeDtypeStruct((batch, HEADS, seq, LANE), BF16)
    head_spec = pl.BlockSpec((None, HEADS, tm, LANE), lambda i: (i // nt, 0, i % nt, 0))
    tab_spec = pl.BlockSpec((tm, LANE), lambda i: (i % nt, 0))
    const = lambda i: (0, 0)
    return pl.pallas_call(
        _odd_in_kernel,
        grid=(n // tm,),
        in_specs=[pl.BlockSpec((tm, d), lambda i: (i, 0)), pl.BlockSpec(w.shape, const),
                  pl.BlockSpec((1, MLA_Q_RANK), const), pl.BlockSpec(w_uq.shape, const),
                  pl.BlockSpec((1, MLA_KV_RANK), const), pl.BlockSpec(w_ukv.shape, const)] + [tab_spec] * 3,
        out_specs=[pl.BlockSpec((None, HEADS, LANE, tm), lambda i: (i // nt, 0, 0, i % nt)), head_spec,
                   pl.BlockSpec((None, HEADS, None, LANE + ONES_ROWS, tm),
                                lambda i: (i // nt, 0, (i % nt) // per, 0, (i % nt) % per)),
                   pl.BlockSpec((tm, gw_), lambda i: (i, 0))],
        out_shape=[jax.ShapeDtypeStruct((batch, HEADS, LANE, seq), BF16), heads, jax.ShapeDtypeStruct((batch, HEADS, seq // tk, LANE + ONES_ROWS, tk), BF16),
                   jax.ShapeDtypeStruct((n, gw_), F32)],
        compiler_params=_cparams(("parallel",)),
        name="odd_in",
    )(x, w, q_norm.reshape(1, -1), w_uq, kv_norm.reshape(1, -1), w_ukv, *tabs)


def _rope_tables(seq, segs, theta):
    pos = jnp.arange(seq, dtype=F32)
    inv = jnp.zeros((LANE,), F32)
    lo = np.zeros((LANE,), bool)
    hi = np.zeros((LANE,), bool)
    for start, rot in segs:
        half = rot // 2
        f = jnp.power(jnp.float32(theta), -jnp.arange(0, rot, 2, dtype=F32) / rot)
        inv = inv.at[start:start + half].set(f).at[start + half:start + rot].set(f)
        lo[start:start + half] = True
        hi[start + half:start + rot] = True
    ang = pos[:, None] * inv[None, :]
    cos, sin = jnp.cos(ang), jnp.sin(ang)
    c = jnp.where(lo | hi, cos, 1.0)
    sa = jnp.where(hi, sin, 0.0)
    sb = jnp.where(lo, -sin, 0.0)
    return c, sa, sb


ONES_ROWS = 16
LOG2E = math.log2(math.e)
FLASH_TK = 1024
FLASH_TK_DIFF = 512


def _flash_kernel(*refs, ncomp, nk, lam_init):
    if ncomp == 2:
        q_ref, k_ref, vt_ref, lq1, lk1, lq2, lk2, g_ref, o_ref, *scr = refs
    else:
        q_ref, k_ref, vt_ref, o_ref, *scr = refs
    qm_sc, m_sc, acc_sc, s0, s1, cm0, cm1, p0, p1, al0, al1 = scr
    tk = s0.shape[1]
    q = q_ref[...]
    if ncomp == 2:
        chan = lax.broadcasted_iota(jnp.int32, q.shape, 0)
        zero = jnp.zeros_like(q)
        qm_sc[0] = jnp.where(chan < DIFF_HEAD_DIM, q, zero)
        qm_sc[1] = jnp.where(chan >= DIFF_HEAD_DIM, q, zero)
    else:
        qm_sc[0] = q
    m_sc[...] = jnp.full(m_sc.shape, -jnp.inf, F32)
    acc_sc[...] = jnp.zeros(acc_sc.shape, F32)

    def scores(j, s_ref, cm_ref):
        k = k_ref[j * tk:(j + 1) * tk, :]
        for c in range(ncomp):
            s = jnp.dot(k, qm_sc[c], preferred_element_type=F32)
            s_ref[c] = s
            cm_ref[c] = jnp.max(s, axis=0, keepdims=True)

    def softmax(s_ref, cm_ref, p_ref, al_ref):
        for c in range(ncomp):
            m_old = m_sc[c]
            m_new = jnp.maximum(m_old, cm_ref[c])
            al_ref[c] = jnp.exp2(m_old - m_new)
            p_ref[c] = jnp.exp2(s_ref[c] - m_new).astype(BF16)
            m_sc[c] = m_new

    def values(j, p_ref, al_ref):
        vt = vt_ref[j]
        for c in range(ncomp):
            acc_sc[c] = al_ref[c] * acc_sc[c] + jnp.dot(vt, p_ref[c], preferred_element_type=F32)

    bufs = ((s0, cm0, p0, al0), (s1, cm1, p1, al1))
    scores(0, s0, cm0)
    for j in range(nk):
        s_c, cm_c, p_c, al_c = bufs[j % 2]
        s_n, cm_n, p_n, al_n = bufs[(j + 1) % 2]
        if j + 1 < nk:
            scores(j + 1, s_n, cm_n)
        softmax(s_c, cm_c, p_c, al_c)
        if j >= 1:
            values(j - 1, p_n, al_n)
    values(nk - 1, *bufs[(nk - 1) % 2][2:])

    def normalised(c):
        acc = acc_sc[c]
        return acc[:LANE] / acc[LANE:LANE + 1]

    o = normalised(0)
    if ncomp == 2:
        lam = (jnp.exp(jnp.sum(lq1[...] * lk1[...], keepdims=True))
               - jnp.exp(jnp.sum(lq2[...] * lk2[...], keepdims=True)) + lam_init)
        o = o - lam * normalised(1)
        o = o * lax.rsqrt(jnp.mean(o * o, axis=0, keepdims=True) + RMS_EPS) * g_ref[...] * (1.0 - lam_init)
    o_ref[...] = o.T.astype(o_ref.dtype)


def _flash(q, k, vt, *, batch, seq, tq=512, diff=None, lam_init=0.0):
    nk, vrows, tk = vt.shape[2], vt.shape[3], vt.shape[4]
    assert vrows == LANE + ONES_ROWS
    tq = min(tq, seq)
    nq = seq // tq
    ncomp = 2 if diff is not None else 1
    in_specs = [
        pl.BlockSpec((None, None, LANE, tq), lambda b, h, i: (b, h, 0, i)),
        pl.BlockSpec((None, None, seq, LANE), lambda b, h, i: (b, h, 0, 0)),
        pl.BlockSpec((None, None, nk, vrows, tk), lambda b, h, i: (b, h, 0, 0, 0)),
    ]
    args = [q, k, vt]
    if diff is not None:
        lq1, lk1, lq2, lk2, subln = diff
        for v in (lq1, lk1, lq2, lk2):
            in_specs.append(pl.BlockSpec((1, DIFF_HEAD_DIM), lambda b, h, i: (0, 0)))
            args.append(v.reshape(1, DIFF_HEAD_DIM))
        in_specs.append(pl.BlockSpec((LANE, 1), lambda b, h, i: (0, 0)))
        args.append(subln.reshape(LANE, 1))
    return pl.pallas_call(
        functools.partial(_flash_kernel, ncomp=ncomp, nk=nk, lam_init=lam_init),
        grid=(batch, HEADS, nq),
        in_specs=in_specs,
        out_specs=pl.BlockSpec((tq, LANE), lambda b, h, i: (b * nq + i, h)),
        out_shape=jax.ShapeDtypeStruct((batch * seq, HEADS * LANE), BF16),
        scratch_shapes=[pltpu.VMEM((ncomp, LANE, tq), BF16),
                        pltpu.VMEM((ncomp, 1, tq), F32), pltpu.VMEM((ncomp, vrows, tq), F32),
                        pltpu.VMEM((ncomp, tk, tq), F32), pltpu.VMEM((ncomp, tk, tq), F32),
                        pltpu.VMEM((ncomp, 1, tq), F32), pltpu.VMEM((ncomp, 1, tq), F32),
                        pltpu.VMEM((ncomp, tk, tq), BF16), pltpu.VMEM((ncomp, tk, tq), BF16),
                        pltpu.VMEM((ncomp, 1, tq), F32), pltpu.VMEM((ncomp, 1, tq), F32)],
        compiler_params=_cparams(("parallel", "parallel", "parallel")),
        name="flash_diff" if diff is not None else "flash_mla",
    )(*args)


def _ret_kernel(dec_ref, qf, kf, vf, qb, kb, vb, of_ref, ob_ref, s_sc, *, chunk):
    @pl.when(pl.program_id(1) == 0)
    def _():
        s_sc[...] = jnp.zeros(s_sc.shape, F32)

    ii = lax.broadcasted_iota(jnp.int32, (chunk, chunk), 0)
    jj = lax.broadcasted_iota(jnp.int32, (chunk, chunk), 1)
    r = lax.broadcasted_iota(jnp.int32, (chunk, 1), 0).astype(F32)
    for d, (q_ref, k_ref, v_ref, o_ref) in enumerate(((qf, kf, vf, of_ref), (qb, kb, vb, ob_ref))):
        for h in range(HEADS):
            la = -jnp.exp(jnp.full((1, 1), dec_ref[d, h], F32))
            if d == 0:
                mask, dist = ii >= jj, (ii - jj).astype(F32)
                qdec, kdec = jnp.exp(la * (r + 1.0)), jnp.exp(la * (chunk - 1.0 - r))
            else:
                mask, dist = jj > ii, (jj - ii).astype(F32)
                qdec, kdec = jnp.exp(la * (chunk - r)), jnp.exp(la * r)
            decay = jnp.where(mask, jnp.exp(jnp.where(mask, dist * la, 0.0)), 0.0)
            q, k, v = q_ref[h], k_ref[h], v_ref[h]
            s = lax.dot_general(q, k, (((1,), (1,)), ((), ())), preferred_element_type=F32)
            o = jnp.dot((s * decay).astype(BF16), v, preferred_element_type=F32)
            state = s_sc[d, h]
            o = o + qdec * jnp.dot(q, state.astype(BF16), preferred_element_type=F32)
            kd = (k.astype(F32) * kdec).astype(BF16)
            s_sc[d, h] = jnp.exp(la * float(chunk)) * state + lax.dot_general(
                kd, v, (((0,), (0,)), ((), ())), preferred_element_type=F32)
            o_ref[:, h * LANE:(h + 1) * LANE] = o.astype(o_ref.dtype)


def _retention(q, k, v, decays, *, batch, seq, chunk=256):
    chunk = min(chunk, seq)
    n = seq // chunk
    fwd = pl.BlockSpec((None, HEADS, chunk, LANE), lambda b, c: (b, 0, c, 0))
    bwd = pl.BlockSpec((None, HEADS, chunk, LANE), lambda b, c: (b, 0, n - 1 - c, 0))
    w = HEADS * LANE
    out = jax.ShapeDtypeStruct((batch * seq, w), BF16)
    return pl.pallas_call(
        functools.partial(_ret_kernel, chunk=chunk),
        grid=(batch, n),
        in_specs=[pl.BlockSpec(memory_space=pltpu.SMEM), fwd, fwd, fwd, bwd, bwd, bwd],
        out_specs=[pl.BlockSpec((chunk, w), lambda b, c: (b * n + c, 0)),
                   pl.BlockSpec((chunk, w), lambda b, c: (b * n + n - 1 - c, 0))],
        out_shape=[out, out],
        scratch_shapes=[pltpu.VMEM((2, HEADS, LANE, LANE), F32)],
        compiler_params=_cparams(("parallel", "arbitrary")),
        name="retention",
    )(decays, q, k, v, q, k, v)


GLA_SUB = 8


def _split3(x):
    x1 = x.astype(BF16)
    r1 = x - x1.astype(F32)
    x2 = r1.astype(BF16)
    x3 = (r1 - x2.astype(F32)).astype(BF16)
    return x1, x2, x3


def _gla_direction(q, k, v, lr, w2, bias, st, reverse):
    C, wk = q.shape
    wv = v.shape[1]
    dk, dv = wk // HEADS, wv // HEADS
    z = jnp.dot(lr.astype(BF16), w2, preferred_element_type=F32) + bias
    g = (jnp.minimum(z, 0.0) - jnp.log(1.0 + jnp.exp(-jnp.abs(z)))) * (LOG2E / GLA_TAU)
    ii = lax.broadcasted_iota(jnp.int32, (C, C), 0)
    jj = lax.broadcasted_iota(jnp.int32, (C, C), 1)
    tri = jnp.where(ii >= jj, 1.0, 0.0).astype(BF16)
    b = sum(jnp.dot(tri, part, preferred_element_type=F32) for part in _split3(g))
    tot = b[C - 1:C, :]
    c = (tot - b + g) if reverse else b

    qe = (q * jnp.exp2(jnp.minimum(c, 0.0))).astype(BF16)
    o = lax.dot_general(qe, st.astype(BF16), (((1,), (1,)), ((), ())), preferred_element_type=F32)
    ke = (k * jnp.exp2(jnp.minimum(tot - c, 0.0))).astype(BF16)
    upd = lax.dot_general(v.astype(BF16), ke, (((0,), (0,)), ((), ())), preferred_element_type=F32)
    rr = _div_pow2(lax.broadcasted_iota(jnp.int32, (wv, wk), 0), dv)
    cc = _div_pow2(lax.broadcasted_iota(jnp.int32, (wv, wk), 1), dk)
    new_st = jnp.where(rr == cc, st * jnp.exp2(tot) + upd, 0.0)

    lane_head = _div_pow2(lax.broadcasted_iota(jnp.int32, (C, wk), 1), dk)
    scores = [jnp.zeros((C, C), F32) for _ in range(HEADS)]
    hsz = C // 2
    while hsz >= GLA_SUB:
        blk = 2 * hsz
        rows = []
        for m in range(C // blk):
            rrow = m * blk + (hsz if reverse else hsz - 1)
            rows.append(jnp.broadcast_to(c[rrow:rrow + 1, :], (blk, wk)))
        ref = jnp.concatenate(rows, axis=0) if len(rows) > 1 else rows[0]
        qt = q * jnp.exp2(jnp.minimum(c - ref, 0.0))
        kt = (k * jnp.exp2(jnp.minimum(ref - c, 0.0))).astype(BF16)
        same = _div_pow2(ii, blk) == _div_pow2(jj, blk)
        if reverse:
            lvl = same & (_mod_pow2(ii, blk) < hsz) & (_mod_pow2(jj, blk) >= hsz)
        else:
            lvl = same & (_mod_pow2(ii, blk) >= hsz) & (_mod_pow2(jj, blk) < hsz)
        for h in range(HEADS):
            qh = jnp.where(lane_head == h, qt, 0.0).astype(BF16)
            s = lax.dot_general(qh, kt, (((1,), (1,)), ((), ())), preferred_element_type=F32)
            scores[h] = scores[h] + jnp.where(lvl, s, 0.0)
        hsz //= 2

    assert dv == C
    er = _div_pow2(lax.broadcasted_iota(jnp.int32, (wk, wv), 0), dk)
    ec = _div_pow2(lax.broadcasted_iota(jnp.int32, (wk, wv), 1), dv)
    expand = jnp.where(er == ec, 1.0, 0.0).astype(BF16)
    dist = (jj - ii) if reverse else (ii - jj)
    same_sub = _div_pow2(ii, GLA_SUB) == _div_pow2(jj, GLA_SUB)
    for lag in range(1 if reverse else 0, GLA_SUB):
        if lag == 0:
            t = q * k
        else:
            shift = (GLA_SUB - lag) if reverse else lag
            ks = pltpu.roll(k.reshape(C // GLA_SUB, GLA_SUB, wk), shift, axis=1).reshape(C, wk)
            cs = pltpu.roll(c.reshape(C // GLA_SUB, GLA_SUB, wk), shift, axis=1).reshape(C, wk)
            t = q * ks * jnp.exp2(jnp.minimum(c - cs, 0.0))
        red = jnp.dot(t.astype(BF16), expand, preferred_element_type=F32)
        on_diag = same_sub & (dist == lag)
        for h in range(HEADS):
            scores[h] = scores[h] + jnp.where(on_diag, red[:, h * dv:(h + 1) * dv], 0.0)

    vb = v.astype(BF16)
    o = o + jnp.concatenate(
        [jnp.dot(scores[h].astype(BF16), vb[:, h * dv:(h + 1) * dv], preferred_element_type=F32) for h in range(HEADS)],
        axis=1)
    return o, new_st


def _gla_kernel(qf, kf, vf, lf, qb, kb, vb, lb, w2f, bf, w2b, bb, of_ref, ob_ref, s_sc, *, qscale):
    @pl.when(pl.program_id(1) == 0)
    def _():
        s_sc[...] = jnp.zeros(s_sc.shape, F32)

    o, st = _gla_direction(qf[...] * qscale, kf[...], vf[...], lf[...], w2f[...], bf[...], s_sc[0], False)
    of_ref[...] = o.astype(of_ref.dtype)
    s_sc[0] = st
    o, st = _gla_direction(qb[...] * qscale, kb[...], vb[...], lb[...], w2b[...], bb[...], s_sc[1], True)
    ob_ref[...] = o.astype(ob_ref.dtype)
    s_sc[1] = st


def _gla(z, *, qcol, kcol, vcol, lcol, w2f, bf, w2b, bb, batch, seq, chunk=128):
    chunk = min(chunk, seq)
    n = seq // chunk
    wk, wv = HEADS * GLA_K_DIM, HEADS * LANE

    def specs(cmap):
        return [pl.BlockSpec((chunk, wk), lambda b, c: (cmap(b, c), qcol)),
                pl.BlockSpec((chunk, wk), lambda b, c: (cmap(b, c), kcol)),
                pl.BlockSpec((chunk, wv), lambda b, c: (cmap(b, c), vcol)),
                pl.BlockSpec((chunk, LANE), lambda b, c: (cmap(b, c), lcol))]

    fmap = lambda b, c: b * n + c
    bmap = lambda b, c: b * n + n - 1 - c
    wspec = [pl.BlockSpec((LANE, wk), lambda b, c: (0, 0)), pl.BlockSpec((1, wk), lambda b, c: (0, 0))]
    out = jax.ShapeDtypeStruct((batch * seq, wv), BF16)
    return pl.pallas_call(
        functools.partial(_gla_kernel, qscale=GLA_K_DIM ** -0.5),
        grid=(batch, n),
        in_specs=specs(fmap) + specs(bmap) + wspec + wspec,
        out_specs=[pl.BlockSpec((chunk, wv), lambda b, c: (fmap(b, c), 0)),
                   pl.BlockSpec((chunk, wv), lambda b, c: (bmap(b, c), 0))],
        out_shape=[out, out],
        scratch_shapes=[pltpu.VMEM((2, wv, wk), F32)],
        compiler_params=_cparams(("parallel", "arbitrary")),
        name="gla",
    )(z, z, z, z, z, z, z, z, w2f, bf, w2b, bb)


def _layer_norm_rows(r, g, b):
    mu = jnp.mean(r, axis=-1, keepdims=True)
    d = r - mu
    var = jnp.mean(d * d, axis=-1, keepdims=True)
    return d * lax.rsqrt(var + LN_EPS) * g + b


def _outproj_kernel(x_ref, fin_ref, of_ref, ob_ref, gate_ref, ng_ref, wa_ref, wb_ref, lg_ref, lb_ref, rw_ref, rb_ref,
                    o_ref, ids_ref, gw_ref, cnt_ref, run_sc, *, group_norm):
    lin = of_ref[...].astype(F32) + ob_ref[...].astype(F32)
    parts = []
    for h in range(HEADS):
        zh = lin[:, h * LANE:(h + 1) * LANE]
        if group_norm:
            mu = jnp.mean(zh, axis=-1, keepdims=True)
            dz = zh - mu
            parts.append(dz * lax.rsqrt(jnp.mean(dz * dz, axis=-1, keepdims=True) + LN_EPS))
        else:
            parts.append(zh * lax.rsqrt(jnp.mean(zh * zh, axis=-1, keepdims=True) + RMS_EPS) * ng_ref[...])
    gate = gate_ref[...]
    lin = jnp.concatenate(parts, axis=1) * (gate * jax.nn.sigmoid(gate))
    y = (jnp.dot(fin_ref[...].astype(BF16), wa_ref[...], preferred_element_type=F32)
         + jnp.dot(lin.astype(BF16), wb_ref[...], preferred_element_type=F32))
    x1 = _layer_norm_rows(ALPHA * x_ref[...] + y, lg_ref[...], lb_ref[...])
    o_ref[...] = x1
    _route_rows(x1, rw_ref, rb_ref, ids_ref, gw_ref, cnt_ref, run_sc)


def _outproj(x, fin, of, ob, gate_src, gate_col, norm_gain, wa, wb, ln_g, ln_b, route_w, route_b, *, group_norm, tm=512):
    n, d = x.shape
    w = HEADS * LANE
    tm = min(tm, n)
    row = lambda i: (i, 0)
    const = lambda i: (0, 0)
    rw_hi = route_w.astype(BF16)
    rw = jnp.stack([rw_hi, (route_w - rw_hi.astype(F32)).astype(BF16)])
    return pl.pallas_call(
        functools.partial(_outproj_kernel, group_norm=group_norm),
        grid=(n // tm,),
        in_specs=[pl.BlockSpec((tm, d), row), pl.BlockSpec((tm, w), row), pl.BlockSpec((tm, w), row),
                  pl.BlockSpec((tm, w), row), pl.BlockSpec((tm, w), lambda i: (i, gate_col)),
                  pl.BlockSpec((1, LANE), const), pl.BlockSpec((w, d), const), pl.BlockSpec((w, d), const),
                  pl.BlockSpec((1, d), const), pl.BlockSpec((1, d), const),
                  pl.BlockSpec((2, d, LANE), lambda i: (0, 0, 0)), pl.BlockSpec((1, LANE), const)],
        out_specs=[pl.BlockSpec((tm, d), row), pl.BlockSpec((tm, LANE), row), pl.BlockSpec((tm, LANE), row),
                   pl.BlockSpec((1, LANE), const)],
        out_shape=[jax.ShapeDtypeStruct((n, d), F32), jax.ShapeDtypeStruct((n, LANE), jnp.int32),
                   jax.ShapeDtypeStruct((n, LANE), F32), jax.ShapeDtypeStruct((1, LANE), jnp.int32)],
        scratch_shapes=[pltpu.VMEM((1, LANE), F32)],
        compiler_params=_cparams(("arbitrary",)),
        name="outproj",
    )(x, fin, of, ob, gate_src, norm_gain.reshape(1, LANE), wa, wb, ln_g.reshape(1, d), ln_b.reshape(1, d), rw, route_b)


def _route_rows(x, w_ref, b_ref, ids_ref, gw_ref, cnt_ref, run_sc):
    @pl.when(pl.program_id(0) == 0)
    def _():
        run_sc[...] = jnp.zeros(run_sc.shape, F32)

    tm = x.shape[0]
    xh = x.astype(BF16)
    xl = (x - xh.astype(F32)).astype(BF16)
    wh, wl = w_ref[0], w_ref[1]
    logits = (jnp.dot(xh, wh, preferred_element_type=F32) + jnp.dot(xh, wl, preferred_element_type=F32)
              + jnp.dot(xl, wh, preferred_element_type=F32)) + b_ref[...]
    lane = lax.broadcasted_iota(jnp.int32, logits.shape, 1)
    neg = -jnp.inf
    gmask = (lane >= N_EXPERTS) & (lane < N_EXPERTS + N_GROUPS)
    gl = jnp.where(gmask, logits, neg)
    gmax = jnp.max(gl, axis=1, keepdims=True)
    lane_f = lane.astype(F32)
    first = lambda hit: jnp.min(jnp.where(hit, lane_f, float(LANE)), axis=1, keepdims=True).astype(jnp.int32)
    gidx = first(gl == gmax) - N_EXPERTS
    p_grp = 1.0 / jnp.sum(jnp.where(gmask, jnp.exp(gl - gmax), 0.0), axis=1, keepdims=True)
    el = jnp.where(_div_pow2(lane, EXPERTS_PER_GROUP) == gidx, logits, neg)
    l1 = jnp.max(el, axis=1, keepdims=True)
    e1 = first(el == l1)
    el2 = jnp.where(lane == e1, neg, el)
    l2 = jnp.max(el2, axis=1, keepdims=True)
    e2 = first(el2 == l2)
    t = jnp.exp(l2 - l1)
    w1 = p_grp / (1.0 + t)
    w2 = p_grp * t / (1.0 + t)

    onehot = jnp.where(lane == e1, 1.0, jnp.where(lane == e2, 1.0, 0.0))
    ri = lax.broadcasted_iota(jnp.int32, (tm, tm), 0)
    ci = lax.broadcasted_iota(jnp.int32, (tm, tm), 1)
    before = jnp.dot(jnp.where(ri > ci, 1.0, 0.0).astype(BF16), onehot.astype(BF16), preferred_element_type=F32)
    before = before + run_sc[...]
    r1 = jnp.sum(jnp.where(lane == e1, before, 0.0), axis=1, keepdims=True).astype(jnp.int32)
    r2 = jnp.sum(jnp.where(lane == e2, before, 0.0), axis=1, keepdims=True).astype(jnp.int32)
    run_sc[...] = run_sc[...] + jnp.sum(onehot, axis=0, keepdims=True)
    cnt_ref[...] = run_sc[...].astype(jnp.int32)
    ids_ref[...] = jnp.where(lane == 0, e1, jnp.where(lane == 1, e2, jnp.where(lane == 2, r1, jnp.where(lane == 3, r2, 0))))
    gw_ref[...] = jnp.where(lane == 0, w1, jnp.where(lane == 1, w2, 0.0))


MOE_BM = 256
SUBL = 8


def _rows_from_linear(ref, rows):
    return jnp.concatenate([ref[pl.ds(s, rows, stride=SUBL), :] for s in range(SUBL)], axis=1)


def _rows_to_linear(ref, val):
    for s in range(SUBL):
        ref[pl.ds(s, val.shape[0], stride=SUBL), :] = val[:, s * LANE:(s + 1) * LANE]


def _dispatch_kernel(pend_ref, padded_ref, dest_ref, x_ref, xs_hbm, idx_smem, zbuf, lin, sem_i, sem_z, sem):
    i = pl.program_id(0)
    tm = x_ref.shape[0]
    bm = zbuf.shape[0] // SUBL

    @pl.when(i == 0)
    def _():
        zbuf[...] = jnp.zeros(zbuf.shape, F32)

        def tail(e):
            start_row = pl.multiple_of((pend_ref[e] - bm) * SUBL, bm * SUBL)
            return pltpu.make_async_copy(zbuf, xs_hbm.at[pl.ds(start_row, bm * SUBL), :], sem_z)

        def start(e, carry):
            @pl.when(padded_ref[e] > 0)
            def _():
                tail(e).start()
            return carry

        def wait(e, carry):
            @pl.when(padded_ref[e] > 0)
            def _():
                tail(e).wait()
            return carry

        lax.fori_loop(0, N_EXPERTS, start, 0)
        lax.fori_loop(0, N_EXPERTS, wait, 0)

        def unused(b):
            start_row = pl.multiple_of(b * bm * SUBL, bm * SUBL)
            return pltpu.make_async_copy(zbuf, xs_hbm.at[pl.ds(start_row, bm * SUBL), :], sem_z)

        first_unused = pend_ref[N_EXPERTS - 1] // bm
        n_blocks = xs_hbm.shape[0] // (bm * SUBL)
        lax.fori_loop(first_unused, n_blocks, lambda b, c: (unused(b).start(), c)[1], 0)
        lax.fori_loop(first_unused, n_blocks, lambda b, c: (unused(b).wait(), c)[1], 0)

    cp = pltpu.make_async_copy(dest_ref.at[i], idx_smem, sem_i)
    cp.start()
    _rows_to_linear(lin, x_ref[...])
    cp.wait()

    def scatter(r, carry):
        src = lin.at[pl.ds(pl.multiple_of(r * SUBL, SUBL), SUBL), :]
        for k in range(2):
            dst = pl.multiple_of(idx_smem[2 * r + k] * SUBL, SUBL)
            pltpu.make_async_copy(src, xs_hbm.at[pl.ds(dst, SUBL), :], sem).start(priority=k)
        return carry

    lax.fori_loop(0, tm, scatter, 0, unroll=8)
    for k in range(2):
        pltpu.make_async_copy(lin, xs_hbm.at[pl.ds(0, tm * SUBL), :], sem).wait()


def _dispatch(x, dest, pend, padded, cap, *, tm=512):
    n, d = x.shape
    assert d == SUBL * LANE
    tm = min(tm, n)
    nt = n // tm
    grid_spec = pltpu.PrefetchScalarGridSpec(
        num_scalar_prefetch=2,
        grid=(nt,),
        in_specs=[pl.BlockSpec((nt, 2 * tm), lambda i, pe, pa: (0, 0)),
                  pl.BlockSpec((tm, d), lambda i, pe, pa: (i, 0))],
        out_specs=pl.BlockSpec(memory_space=pl.ANY),
        scratch_shapes=[pltpu.SMEM((2 * tm,), jnp.int32), pltpu.VMEM((MOE_BM * SUBL, LANE), F32),
                        pltpu.VMEM((tm * SUBL, LANE), F32),
                        pltpu.SemaphoreType.DMA(()), pltpu.SemaphoreType.DMA(()), pltpu.SemaphoreType.DMA(())],
    )
    return pl.pallas_call(
        _dispatch_kernel,
        grid_spec=grid_spec,
        out_shape=jax.ShapeDtypeStruct((cap * SUBL, LANE), F32),
        compiler_params=_cparams(("arbitrary",)),
        name="dispatch",
    )(pend, padded, dest.reshape(nt, 2 * tm), x)


def _experts_kernel(blk_e_ref, nused_ref, xs_ref, wg_ref, wu_ref, wd_ref, ys_ref, wgb, wub, wdb, hbuf):
    i = pl.program_id(0)
    nb = pl.num_programs(0) - 1
    cur = jnp.minimum(i, nb - 1)
    prv = jnp.maximum(i - 1, 0)

    @pl.when(i == 0)
    def _():
        hbuf[...] = jnp.zeros(hbuf.shape, BF16)

    @pl.when(jnp.logical_or(i == 0, blk_e_ref[cur] != blk_e_ref[jnp.minimum(prv, nb - 1)]))
    def _():
        wgb[...] = wg_ref[...].astype(BF16)
        wub[...] = wu_ref[...].astype(BF16)

    @pl.when(jnp.logical_or(i == 0, blk_e_ref[prv] != blk_e_ref[jnp.maximum(i - 2, 0)]))
    def _():
        wdb[...] = wd_ref[...].astype(BF16)

    slot = lax.rem(i, 2)
    y = jnp.dot(hbuf[1 - slot], wdb[...], preferred_element_type=F32)
    live = jnp.logical_and(i >= 1, i - 1 < nused_ref[0])
    _rows_to_linear(ys_ref, jnp.where(live, y, 0.0))
    xb = _rows_from_linear(xs_ref, xs_ref.shape[0] // SUBL).astype(BF16)
    hg = jnp.dot(xb, wgb[...], preferred_element_type=F32)
    hu = jnp.dot(xb, wub[...], preferred_element_type=F32)
    hbuf[slot] = (hg * jax.nn.sigmoid(hg) * hu).astype(BF16)


def _experts(xs, blk_e, n_used, w_gate, w_up, w_down, layer):
    cap = xs.shape[0] // SUBL
    bm = MOE_BM
    nb = cap // bm
    d, de = w_gate.shape[2], w_gate.shape[3]
    row_in = lambda i, be, nu: (jnp.minimum(i, nu[0] - 1), 0)
    row = lambda i, be, nu: (jnp.maximum(i - 1, 0), 0)
    w_cur = lambda i, be, nu: (layer, be[jnp.minimum(i, nb - 1)], 0, 0)
    w_prv = lambda i, be, nu: (layer, be[jnp.maximum(i - 1, 0)], 0, 0)
    grid_spec = pltpu.PrefetchScalarGridSpec(
        num_scalar_prefetch=2,
        grid=(nb + 1,),
        in_specs=[pl.BlockSpec((bm * SUBL, LANE), row_in),
                  pl.BlockSpec((None, None, d, de), w_cur),
                  pl.BlockSpec((None, None, d, de), w_cur),
                  pl.BlockSpec((None, None, de, d), w_prv)],
        out_specs=pl.BlockSpec((bm * SUBL, LANE), row),
        scratch_shapes=[pltpu.VMEM((d, de), BF16), pltpu.VMEM((d, de), BF16), pltpu.VMEM((de, d), BF16),
                        pltpu.VMEM((2, bm, de), BF16)],
    )
    return pl.pallas_call(
        _experts_kernel,
        grid_spec=grid_spec,
        out_shape=jax.ShapeDtypeStruct((cap * SUBL, LANE), F32),
        compiler_params=_cparams(("arbitrary",)),
        name="experts",
    )(blk_e, n_used, xs, w_gate, w_up, w_down)


def _combine_kernel(dest_ref, x_ref, gw_ref, g_ref, b_ref, ys_hbm, o_ref, idx_smem, ybuf, sem_i, sem):
    i = pl.program_id(0)
    tm = x_ref.shape[0]

    def issue(tile, slot):
        cp = pltpu.make_async_copy(dest_ref.at[tile], idx_smem, sem_i)
        cp.start()
        cp.wait()

        def gather(r, carry):
            row = pl.multiple_of(r * SUBL, SUBL)
            for k in range(2):
                src = pl.multiple_of(idx_smem[2 * r + k] * SUBL, SUBL)
                pltpu.make_async_copy(ys_hbm.at[pl.ds(src, SUBL), :], ybuf.at[slot, k, pl.ds(row, SUBL), :],
                                      sem.at[slot]).start(priority=k)
            return carry

        lax.fori_loop(0, tm, gather, 0, unroll=8)

    slot = lax.rem(i, 2)

    @pl.when(i == 0)
    def _():
        issue(0, 0)

    @pl.when(i + 1 < pl.num_programs(0))
    def _():
        issue(i + 1, 1 - slot)

    for k in range(2):
        pltpu.make_async_copy(ys_hbm.at[pl.ds(0, tm * SUBL), :], ybuf.at[slot, k], sem.at[slot]).wait()
    gw = gw_ref[...]
    ffn = (_rows_from_linear(ybuf.at[slot, 0], tm) * gw[:, 0:1] + _rows_from_linear(ybuf.at[slot, 1], tm) * gw[:, 1:2])
    o_ref[...] = _layer_norm_rows(ALPHA * x_ref[...] + ffn, g_ref[...], b_ref[...])


def _combine(x, ys, dest, gw, ln_g, ln_b, *, tm=512):
    n, d = x.shape
    tm = min(tm, n)
    nt = n // tm
    return pl.pallas_call(
        _combine_kernel,
        grid=(nt,),
        in_specs=[pl.BlockSpec((nt, 2 * tm), lambda i: (0, 0)),
                  pl.BlockSpec((tm, d), lambda i: (i, 0)), pl.BlockSpec((tm, LANE), lambda i: (i, 0)),
                  pl.BlockSpec((1, d), lambda i: (0, 0)), pl.BlockSpec((1, d), lambda i: (0, 0)),
                  pl.BlockSpec(memory_space=pl.ANY)],
        out_specs=pl.BlockSpec((tm, d), lambda i: (i, 0)),
        out_shape=jax.ShapeDtypeStruct((n, d), F32),
        scratch_shapes=[pltpu.SMEM((2 * tm,), jnp.int32), pltpu.VMEM((2, 2, tm * SUBL, LANE), F32),
                        pltpu.SemaphoreType.DMA(()), pltpu.SemaphoreType.DMA((2,))],
        compiler_params=_cparams(("arbitrary",)),
        name="combine",
    )(dest.reshape(nt, 2 * tm), x, gw, ln_g.reshape(1, d), ln_b.reshape(1, d), ys)


def _router_params(w_grp, b_grp, w_exp, b_exp):
    d = w_exp.shape[0]
    wr = jnp.zeros((d, LANE), F32).at[:, :N_EXPERTS].set(w_exp).at[:, N_EXPERTS:N_EXPERTS + N_GROUPS].set(w_grp)
    br = jnp.zeros((1, LANE), F32).at[0, :N_EXPERTS].set(b_exp).at[0, N_EXPERTS:N_EXPERTS + N_GROUPS].set(b_grp)
    return wr, br


def _moe(x, routing, w_gate, w_up, w_down, layer, ln_g, ln_b):
    n, d = x.shape
    ids, gw, cnt = routing
    bm = MOE_BM
    counts = cnt[0, :N_EXPERTS]
    padded = (counts + bm - 1) // bm * bm
    pend = jnp.cumsum(padded)
    pstart = pend - padded
    e, r = ids[:, 0:2], ids[:, 2:4]
    onehot = e[:, :, None] == jnp.arange(N_EXPERTS, dtype=jnp.int32)[None, None, :]
    dest = jnp.sum(jnp.where(onehot, pstart[None, None, :], 0), axis=-1) + r
    cap = 2 * n + N_EXPERTS * bm
    nb = cap // bm
    blk_start = jnp.arange(nb, dtype=jnp.int32) * bm
    blk_e = jnp.minimum(jnp.sum((pend[None, :] <= blk_start[:, None]).astype(jnp.int32), axis=1), N_EXPERTS - 1)
    n_used = (pend[-1:] // bm).astype(jnp.int32)
    xs = _dispatch(x, dest, pend.astype(jnp.int32), padded.astype(jnp.int32), cap)
    ys = _experts(xs, blk_e, n_used, w_gate, w_up, w_down, layer)
    return _combine(x, ys, dest, gw, ln_g, ln_b)


def _even_layer(x, batch, seq, layer_idx, w_in, dec_f, dec_b, lq1, lk1, lq2, lk2, subln, w_out, ln_g, ln_b, route):
    d = x.shape[1]
    w = HEADS * LANE
    kw = dict(batch=batch, seq=seq)
    diff_seg = [(0, DIFF_ROT_DIM), (DIFF_HEAD_DIM, DIFF_ROT_DIM)]
    q, k, v, gate, dq, dk, dvt = _even_in(
        x, w_in.astype(BF16), _rope_tables(seq, [(0, LANE)], RET_THETA),
        _rope_tables(seq, diff_seg, ROPE_THETA), **kw)
    decays = jnp.stack([dec_f, dec_b]).astype(F32)
    of, ob = _retention(q, k, v, decays, **kw)
    lam_init = 0.8 - 0.6 * math.exp(-0.3 * layer_idx)
    diff = _flash(dq, dk, dvt, diff=(lq1, lk1, lq2, lk2, subln), lam_init=lam_init, tq=2048, **kw)
    wo = w_out.astype(BF16)
    return _outproj(x, diff, of, ob, gate, 0, jnp.ones((LANE,), F32), wo[w:], wo[:w], ln_g, ln_b, *route,
                    group_norm=True)


def _odd_layer(x, batch, seq, w_in, q_norm, w_uq, kv_norm, w_ukv, w2_f, b_f, w2_b, b_b, gla_norm, w_out, ln_g, ln_b,
               route):
    d = x.shape[1]
    w = HEADS * LANE
    o = np.cumsum([0, MLA_Q_RANK, MLA_KV_RANK, MLA_ROPE, HEADS * GLA_K_DIM, HEADS * GLA_K_DIM, w, w,
                   GLA_GATE_RANK, GLA_GATE_RANK]).tolist()
    zeros = lambda c: jnp.zeros((d, c), F32)
    w_in2 = jnp.concatenate([
        w_in[:, o[0]:o[2]], zeros(MLA_NOPE), w_in[:, o[2]:o[3]], zeros(LANE - MLA_NOPE - MLA_ROPE),
        w_in[:, o[3]:o[7]], w_in[:, o[7]:o[9]], zeros(LANE - 2 * GLA_GATE_RANK)], axis=1).astype(BF16)
    kw = dict(batch=batch, seq=seq)
    qd = MLA_NOPE + MLA_ROPE
    w_uq2 = jnp.pad(w_uq.reshape(MLA_Q_RANK, HEADS, qd), ((0, 0), (0, 0), (0, LANE - qd))).reshape(MLA_Q_RANK, w)
    ukv = w_ukv.reshape(MLA_KV_RANK, HEADS, MLA_NOPE + MLA_V)
    w_uk2 = jnp.pad(ukv[:, :, :MLA_NOPE], ((0, 0), (0, 0), (0, LANE - MLA_NOPE))).reshape(MLA_KV_RANK, w)
    w_uv2 = ukv[:, :, MLA_NOPE:].reshape(MLA_KV_RANK, w)
    q, k, vt, zg = _odd_in(x, w_in2, q_norm, w_uq2.astype(BF16), kv_norm,
                           jnp.concatenate([w_uk2, w_uv2], axis=1).astype(BF16),
                           _rope_tables(seq, [(MLA_NOPE, MLA_ROPE)], ROPE_THETA), **kw)
    mla = _flash(q, k, vt, **kw)
    wk = HEADS * GLA_K_DIM
    pad_rows = lambda m, r0: jnp.zeros((LANE, wk), F32).at[r0:r0 + GLA_GATE_RANK].set(m).astype(BF16)
    of, ob = _gla(zg, qcol=0, kcol=1, vcol=1, lcol=12,
                  w2f=pad_rows(w2_f, 0), bf=b_f.reshape(1, wk), w2b=pad_rows(w2_b, GLA_GATE_RANK), bb=b_b.reshape(1, wk), **kw)
    wo = w_out.astype(BF16)
    return _outproj(x, mla, of, ob, zg, 2, gla_norm, wo[:w], wo[w:], ln_g, ln_b, *route, group_norm=False)


def kernel(x, ev_w_in, ev_ret_decay_f, ev_ret_decay_b, ev_lq1, ev_lk1, ev_lq2, ev_lk2, ev_subln, ev_w_out, od_w_in, od_q_norm, od_w_uq, od_kv_norm, od_w_ukv, od_gla_w2_f, od_gla_b_f, od_gla_w2_b, od_gla_b_b, od_gla_norm, od_w_out, ln1_g, ln1_b, ln2_g, ln2_b, moe_w_grp, moe_b_grp, moe_w_exp, moe_b_exp, moe_w_gate, moe_w_up, moe_w_down):
    batch, seq, d = x.shape
    h = x.reshape(batch * seq, d)
    for i in range(DEPTH):
        j = i // 2
        route = _router_params(moe_w_grp[i], moe_b_grp[i], moe_w_exp[i], moe_b_exp[i])
        if i % 2 == 0:
            h, *routing = _even_layer(h, batch, seq, i, ev_w_in[j], ev_ret_decay_f[j], ev_ret_decay_b[j], ev_lq1[j],
                                      ev_lk1[j], ev_lq2[j], ev_lk2[j], ev_subln[j], ev_w_out[j], ln1_g[i], ln1_b[i], route)
        else:
            h, *routing = _odd_layer(h, batch, seq, od_w_in[j], od_q_norm[j], od_w_uq[j], od_kv_norm[j], od_w_ukv[j],
                                     od_gla_w2_f[j], od_gla_b_f[j], od_gla_w2_b[j], od_gla_b_b[j], od_gla_norm[j],
                                     od_w_out[j], ln1_g[i], ln1_b[i], route)
        h = _moe(h, routing, moe_w_gate, moe_w_up, moe_w_down, i, ln2_g[i], ln2_b[i])
    return h.reshape(batch, seq, d)
```

```python
import functools
import math

import numpy as np
import jax
import jax.numpy as jnp
from jax import lax
from jax.experimental import pallas as pl
from jax.experimental.pallas import tpu as pltpu

F32 = jnp.float32
BF16 = jnp.bfloat16

HEADS = 4
LANE = 128
RET_THETA = 10000.0
ROPE_THETA = 500000.0
DIFF_HEAD_DIM = 64
DIFF_ROT_DIM = 16
MLA_Q_RANK = 256
MLA_KV_RANK = 128
MLA_NOPE = 64
MLA_ROPE = 32
MLA_V = 128
GLA_K_DIM = 64
GLA_GATE_RANK = 16
GLA_TAU = 16.0
N_GROUPS = 4
EXPERTS_PER_GROUP = 8
N_EXPERTS = N_GROUPS * EXPERTS_PER_GROUP
DEPTH = 2
ALPHA = (2.0 * DEPTH) ** 0.25
LN_EPS = 1e-5
RMS_EPS = 1e-6

VMEM_LIMIT = 48 * 1024 * 1024


def _div_pow2(x, n):
    return lax.shift_right_logical(x, int(n).bit_length() - 1)


def _mod_pow2(x, n):
    return lax.bitwise_and(x, int(n) - 1)


def _cparams(sem):
    return pltpu.CompilerParams(dimension_semantics=sem, vmem_limit_bytes=VMEM_LIMIT)


def _rope_heads(z, tabs, sh, scale):
    c, sa, sb = tabs
    outs = []
    for h in range(HEADS):
        zh = z[:, h * LANE:(h + 1) * LANE]
        outs.append((zh * c + pltpu.roll(zh, sh, axis=1) * sa + pltpu.roll(zh, LANE - sh, axis=1) * sb) * scale)
    return outs


def _even_in_kernel(x_ref, w_ref, rc, rsa, rsb, dc, dsa, dsb, q_ref, k_ref, v_ref, g_ref, dq_ref, dk_ref, dvt_ref):
    w = HEADS * LANE
    xb = x_ref[...].astype(BF16)
    part = lambda t: jnp.dot(xb, w_ref[:, t * w:(t + 1) * w], preferred_element_type=F32)
    ret_t = (rc[...], rsa[...], rsb[...])
    diff_t = (dc[...], dsa[...], dsb[...])
    for h, o in enumerate(_rope_heads(part(0), ret_t, LANE // 2, 1.0)):
        q_ref[h] = o.astype(BF16)
    for h, o in enumerate(_rope_heads(part(1), ret_t, LANE // 2, LANE ** -0.5)):
        k_ref[h] = o.astype(BF16)
    rv = part(2)
    for h in range(HEADS):
        v_ref[h] = rv[:, h * LANE:(h + 1) * LANE].astype(BF16)
    g_ref[...] = part(3)
    for h, o in enumerate(_rope_heads(part(4), diff_t, DIFF_ROT_DIM // 2, DIFF_HEAD_DIM ** -0.5 * LOG2E)):
        dq_ref[h] = o.T.astype(BF16)
    for h, o in enumerate(_rope_heads(part(5), diff_t, DIFF_ROT_DIM // 2, 1.0)):
        dk_ref[h] = o.astype(BF16)
    dv = part(6)
    for h in range(HEADS):
        dvt_ref[h, :LANE, :] = dv[:, h * LANE:(h + 1) * LANE].T.astype(BF16)
        dvt_ref[h, LANE:, :] = jnp.ones((ONES_ROWS, dvt_ref.shape[2]), BF16)


def _even_in(x, w, ret_tabs, diff_tabs, *, batch, seq):
    n, d = x.shape
    tm = min(FLASH_TK_DIFF, seq // 2)
    nt = seq // tm
    hw = HEADS * LANE
    heads = jax.ShapeDtypeStruct((batch, HEADS, seq, LANE), BF16)
    head_spec = pl.BlockSpec((None, HEADS, tm, LANE), lambda i: (i // nt, 0, i % nt, 0))
    heads_t = jax.ShapeDtypeStruct((batch, HEADS, LANE, seq), BF16)
    head_t_spec = pl.BlockSpec((None, HEADS, LANE, tm), lambda i: (i // nt, 0, 0, i % nt))
    tab_spec = pl.BlockSpec((tm, LANE), lambda i: (i % nt, 0))
    return pl.pallas_call(
        _even_in_kernel,
        grid=(n // tm,),
        in_specs=[pl.BlockSpec((tm, d), lambda i: (i, 0)), pl.BlockSpec((d, 7 * hw), lambda i: (0, 0))] + [tab_spec] * 6,
        out_specs=[head_spec, head_spec, head_spec, pl.BlockSpec((tm, hw), lambda i: (i, 0)), head_t_spec, head_spec,
                   pl.BlockSpec((None, HEADS, None, LANE + ONES_ROWS, tm), lambda i: (i // nt, 0, i % nt, 0, 0))],
        out_shape=[heads, heads, heads, jax.ShapeDtypeStruct((n, hw), F32), heads_t, heads,
                   jax.ShapeDtypeStruct((batch, HEADS, nt, LANE + ONES_ROWS, tm), BF16)],
        compiler_params=_cparams(("parallel",)),
        name="even_in",
    )(x, w, *ret_tabs, *diff_tabs)


def _rms_rows(z, g):
    return z * lax.rsqrt(jnp.mean(z * z, axis=-1, keepdims=True) + RMS_EPS) * g


def _odd_in_kernel(x_ref, w_ref, qn_ref, wq_ref, kvn_ref, wkv_ref, tc, tsa, tsb, q_ref, k_ref, vt_ref, zg_ref):
    hw = HEADS * LANE
    mla_w = MLA_Q_RANK + MLA_KV_RANK + LANE
    xb = x_ref[...].astype(BF16)
    zg_ref[...] = jnp.dot(xb, w_ref[:, mla_w:], preferred_element_type=F32)
    z1 = jnp.dot(xb, w_ref[:, :mla_w], preferred_element_type=F32)
    tabs = (tc[...], tsa[...], tsb[...])
    sh = MLA_ROPE // 2
    qh = jnp.dot(_rms_rows(z1[:, :MLA_Q_RANK], qn_ref[...]).astype(BF16), wq_ref[...], preferred_element_type=F32)
    for h, o in enumerate(_rope_heads(qh, tabs, sh, (MLA_NOPE + MLA_ROPE) ** -0.5 * LOG2E)):
        q_ref[h] = o.T.astype(BF16)
    ckv = _rms_rows(z1[:, MLA_Q_RANK:MLA_Q_RANK + MLA_KV_RANK], kvn_ref[...]).astype(BF16)
    kv = jnp.dot(ckv, wkv_ref[...], preferred_element_type=F32)
    kr = z1[:, MLA_Q_RANK + MLA_KV_RANK:]
    kr = kr * tabs[0] + pltpu.roll(kr, sh, axis=1) * tabs[1] + pltpu.roll(kr, LANE - sh, axis=1) * tabs[2]
    for h in range(HEADS):
        k_ref[h] = (kv[:, h * LANE:(h + 1) * LANE] + kr).astype(BF16)
        vt_ref[h, :LANE, :] = kv[:, hw + h * LANE:hw + (h + 1) * LANE].T.astype(BF16)
        vt_ref[h, LANE:, :] = jnp.ones((ONES_ROWS, vt_ref.shape[2]), BF16)


def _odd_in(x, w, q_norm, w_uq, kv_norm, w_ukv, tabs, *, batch, seq, tm=512):
    n, d = x.shape
    tm = min(tm, seq // 2)
    nt = seq // tm
    tk = min(FLASH_TK, seq // 2)
    per = tk // tm
    hw = HEADS * LANE
    gw_ = w.shape[1] - (MLA_Q_RANK + MLA_KV_RANK + LANE)
    heads = jax.ShapeDtypeStruct((batch, HEADS, seq, LANE), BF16)
    head_spec = pl.BlockSpec((None, HEADS, tm, LANE), lambda i: (i // nt, 0, i % nt, 0))
    tab_spec = pl.BlockSpec((tm, LANE), lambda i: (i % nt, 0))
    const = lambda i: (0, 0)
    return pl.pallas_call(
        _odd_in_kernel,
        grid=(n // tm,),
        in_specs=[pl.BlockSpec((tm, d), lambda i: (i, 0)), pl.BlockSpec(w.shape, const),
                  pl.BlockSpec((1, MLA_Q_RANK), const), pl.BlockSpec(w_uq.shape, const),
                  pl.BlockSpec((1, MLA_KV_RANK), const), pl.BlockSpec(w_ukv.shape, const)] + [tab_spec] * 3,
        out_specs=[pl.BlockSpec((None, HEADS, LANE, tm), lambda i: (i // nt, 0, 0, i % nt)), head_spec,
                   pl.BlockSpec((None, HEADS, None, LANE + ONES_ROWS, tm),
                                lambda i: (i // nt, 0, (i % nt) // per, 0, (i % nt) % per)),
                   pl.BlockSpec((tm, gw_), lambda i: (i, 0))],
        out_shape=[jax.ShapeDtypeStruct((batch, HEADS, LANE, seq), BF16), heads, jax.ShapeDtypeStruct((batch, HEADS, seq // tk, LANE + ONES_ROWS, tk), BF16),
                   jax.ShapeDtypeStruct((n, gw_), F32)],
        compiler_params=_cparams(("parallel",)),
        name="odd_in",
    )(x, w, q_norm.reshape(1, -1), w_uq, kv_norm.reshape(1, -1), w_ukv, *tabs)


def _rope_tables(seq, segs, theta):
    pos = jnp.arange(seq, dtype=F32)
    inv = jnp.zeros((LANE,), F32)
    lo = np.zeros((LANE,), bool)
    hi = np.zeros((LANE,), bool)
    for start, rot in segs:
        half = rot // 2
        f = jnp.power(jnp.float32(theta), -jnp.arange(0, rot, 2, dtype=F32) / rot)
        inv = inv.at[start:start + half].set(f).at[start + half:start + rot].set(f)
        lo[start:start + half] = True
        hi[start + half:start + rot] = True
    ang = pos[:, None] * inv[None, :]
    cos, sin = jnp.cos(ang), jnp.sin(ang)
    c = jnp.where(lo | hi, cos, 1.0)
    sa = jnp.where(hi, sin, 0.0)
    sb = jnp.where(lo, -sin, 0.0)
    return c, sa, sb


ONES_ROWS = 16
LOG2E = math.log2(math.e)
FLASH_TK = 1024
FLASH_TK_DIFF = 512


def _flash_kernel(*refs, ncomp, nk, lam_init):
    if ncomp == 2:
        q_ref, k_ref, vt_ref, lq1, lk1, lq2, lk2, g_ref, o_ref, *scr = refs
    else:
        q_ref, k_ref, vt_ref, o_ref, *scr = refs
    qm_sc, m_sc, acc_sc, s0, s1, cm0, cm1, p0, p1, al0, al1 = scr
    tk = s0.shape[1]
    q = q_ref[...]
    if ncomp == 2:
        chan = lax.broadcasted_iota(jnp.int32, q.shape, 0)
        zero = jnp.zeros_like(q)
        qm_sc[0] = jnp.where(chan < DIFF_HEAD_DIM, q, zero)
        qm_sc[1] = jnp.where(chan >= DIFF_HEAD_DIM, q, zero)
    else:
        qm_sc[0] = q
    m_sc[...] = jnp.full(m_sc.shape, -jnp.inf, F32)
    acc_sc[...] = jnp.zeros(acc_sc.shape, F32)

    def scores(j, s_ref, cm_ref):
        k = k_ref[j * tk:(j + 1) * tk, :]
        for c in range(ncomp):
            s = jnp.dot(k, qm_sc[c], preferred_element_type=F32)
            s_ref[c] = s
            cm_ref[c] = jnp.max(s, axis=0, keepdims=True)

    def softmax(s_ref, cm_ref, p_ref, al_ref):
        for c in range(ncomp):
            m_old = m_sc[c]
            m_new = jnp.maximum(m_old, cm_ref[c])
            al_ref[c] = jnp.exp2(m_old - m_new)
            p_ref[c] = jnp.exp2(s_ref[c] - m_new).astype(BF16)
            m_sc[c] = m_new

    def values(j, p_ref, al_ref):
        vt = vt_ref[j]
        for c in range(ncomp):
            acc_sc[c] = al_ref[c] * acc_sc[c] + jnp.dot(vt, p_ref[c], preferred_element_type=F32)

    bufs = ((s0, cm0, p0, al0), (s1, cm1, p1, al1))
    scores(0, s0, cm0)
    for j in range(nk):
        s_c, cm_c, p_c, al_c = bufs[j % 2]
        s_n, cm_n, p_n, al_n = bufs[(j + 1) % 2]
        if j + 1 < nk:
            scores(j + 1, s_n, cm_n)
        softmax(s_c, cm_c, p_c, al_c)
        if j >= 1:
            values(j - 1, p_n, al_n)
    values(nk - 1, *bufs[(nk - 1) % 2][2:])

    def normalised(c):
        acc = acc_sc[c]
        return acc[:LANE] / acc[LANE:LANE + 1]

    o = normalised(0)
    if ncomp == 2:
        lam = (jnp.exp(jnp.sum(lq1[...] * lk1[...], keepdims=True))
               - jnp.exp(jnp.sum(lq2[...] * lk2[...], keepdims=True)) + lam_init)
        o = o - lam * normalised(1)
        o = o * lax.rsqrt(jnp.mean(o * o, axis=0, keepdims=True) + RMS_EPS) * g_ref[...] * (1.0 - lam_init)
    o_ref[...] = o.T.astype(o_ref.dtype)


def _flash(q, k, vt, *, batch, seq, tq=512, diff=None, lam_init=0.0):
    nk, vrows, tk = vt.shape[2], vt.shape[3], vt.shape[4]
    assert vrows == LANE + ONES_ROWS
    tq = min(tq, seq)
    nq = seq // tq
    ncomp = 2 if diff is not None else 1
    in_specs = [
        pl.BlockSpec((None, None, LANE, tq), lambda b, h, i: (b, h, 0, i)),
        pl.BlockSpec((None, None, seq, LANE), lambda b, h, i: (b, h, 0, 0)),
        pl.BlockSpec((None, None, nk, vrows, tk), lambda b, h, i: (b, h, 0, 0, 0)),
    ]
    args = [q, k, vt]
    if diff is not None:
        lq1, lk1, lq2, lk2, subln = diff
        for v in (lq1, lk1, lq2, lk2):
            in_specs.append(pl.BlockSpec((1, DIFF_HEAD_DIM), lambda b, h, i: (0, 0)))
            args.append(v.reshape(1, DIFF_HEAD_DIM))
        in_specs.append(pl.BlockSpec((LANE, 1), lambda b, h, i: (0, 0)))
        args.append(subln.reshape(LANE, 1))
    return pl.pallas_call(
        functools.partial(_flash_kernel, ncomp=ncomp, nk=nk, lam_init=lam_init),
        grid=(batch, HEADS, nq),
        in_specs=in_specs,
        out_specs=pl.BlockSpec((tq, LANE), lambda b, h, i: (b * nq + i, h)),
        out_shape=jax.ShapeDtypeStruct((batch * seq, HEADS * LANE), BF16),
        scratch_shapes=[pltpu.VMEM((ncomp, LANE, tq), BF16),
                        pltpu.VMEM((ncomp, 1, tq), F32), pltpu.VMEM((ncomp, vrows, tq), F32),
                        pltpu.VMEM((ncomp, tk, tq), F32), pltpu.VMEM((ncomp, tk, tq), F32),
                        pltpu.VMEM((ncomp, 1, tq), F32), pltpu.VMEM((ncomp, 1, tq), F32),
                        pltpu.VMEM((ncomp, tk, tq), BF16), pltpu.VMEM((ncomp, tk, tq), BF16),
                        pltpu.VMEM((ncomp, 1, tq), F32), pltpu.VMEM((ncomp, 1, tq), F32)],
        compiler_params=_cparams(("parallel", "parallel", "parallel")),
        name="flash_diff" if diff is not None else "flash_mla",
    )(*args)


def _ret_kernel(dec_ref, qf, kf, vf, qb, kb, vb, of_ref, ob_ref, s_sc, *, chunk):
    @pl.when(pl.program_id(1) == 0)
    def _():
        s_sc[...] = jnp.zeros(s_sc.shape, F32)

    ii = lax.broadcasted_iota(jnp.int32, (chunk, chunk), 0)
    jj = lax.broadcasted_iota(jnp.int32, (chunk, chunk), 1)
    r = lax.broadcasted_iota(jnp.int32, (chunk, 1), 0).astype(F32)
    for d, (q_ref, k_ref, v_ref, o_ref) in enumerate(((qf, kf, vf, of_ref), (qb, kb, vb, ob_ref))):
        for h in range(HEADS):
            la = -jnp.exp(jnp.full((1, 1), dec_ref[d, h], F32))
            if d == 0:
                mask, dist = ii >= jj, (ii - jj).astype(F32)
                qdec, kdec = jnp.exp(la * (r + 1.0)), jnp.exp(la * (chunk - 1.0 - r))
            else:
                mask, dist = jj > ii, (jj - ii).astype(F32)
                qdec, kdec = jnp.exp(la * (chunk - r)), jnp.exp(la * r)
            decay = jnp.where(mask, jnp.exp(jnp.where(mask, dist * la, 0.0)), 0.0)
            q, k, v = q_ref[h], k_ref[h], v_ref[h]
            s = lax.dot_general(q, k, (((1,), (1,)), ((), ())), preferred_element_type=F32)
            o = jnp.dot((s * decay).astype(BF16), v, preferred_element_type=F32)
            state = s_sc[d, h]
            o = o + qdec * jnp.dot(q, state.astype(BF16), preferred_element_type=F32)
            kd = (k.astype(F32) * kdec).astype(BF16)
            s_sc[d, h] = jnp.exp(la * float(chunk)) * state + lax.dot_general(
                kd, v, (((0,), (0,)), ((), ())), preferred_element_type=F32)
            o_ref[:, h * LANE:(h + 1) * LANE] = o.astype(o_ref.dtype)


def _retention(q, k, v, decays, *, batch, seq, chunk=256):
    chunk = min(chunk, seq)
    n = seq // chunk
    fwd = pl.BlockSpec((None, HEADS, chunk, LANE), lambda b, c: (b, 0, c, 0))
    bwd = pl.BlockSpec((None, HEADS, chunk, LANE), lambda b, c: (b, 0, n - 1 - c, 0))
    w = HEADS * LANE
    out = jax.ShapeDtypeStruct((batch * seq, w), BF16)
    return pl.pallas_call(
        functools.partial(_ret_kernel, chunk=chunk),
        grid=(batch, n),
        in_specs=[pl.BlockSpec(memory_space=pltpu.SMEM), fwd, fwd, fwd, bwd, bwd, bwd],
        out_specs=[pl.BlockSpec((chunk, w), lambda b, c: (b * n + c, 0)),
                   pl.BlockSpec((chunk, w), lambda b, c: (b * n + n - 1 - c, 0))],
        out_shape=[out, out],
        scratch_shapes=[pltpu.VMEM((2, HEADS, LANE, LANE), F32)],
        compiler_params=_cparams(("parallel", "arbitrary")),
        name="retention",
    )(decays, q, k, v, q, k, v)


GLA_SUB = 8


def _split3(x):
    x1 = x.astype(BF16)
    r1 = x - x1.astype(F32)
    x2 = r1.astype(BF16)
    x3 = (r1 - x2.astype(F32)).astype(BF16)
    return x1, x2, x3


def _gla_direction(q, k, v, lr, w2, bias, st, reverse):
    C, wk = q.shape
    wv = v.shape[1]
    dk, dv = wk // HEADS, wv // HEADS
    z = jnp.dot(lr.astype(BF16), w2, preferred_element_type=F32) + bias
    g = (jnp.minimum(z, 0.0) - jnp.log(1.0 + jnp.exp(-jnp.abs(z)))) * (LOG2E / GLA_TAU)
    ii = lax.broadcasted_iota(jnp.int32, (C, C), 0)
    jj = lax.broadcasted_iota(jnp.int32, (C, C), 1)
    tri = jnp.where(ii >= jj, 1.0, 0.0).astype(BF16)
    b = sum(jnp.dot(tri, part, preferred_element_type=F32) for part in _split3(g))
    tot = b[C - 1:C, :]
    c = (tot - b + g) if reverse else b

    qe = (q * jnp.exp2(jnp.minimum(c, 0.0))).astype(BF16)
    o = lax.dot_general(qe, st.astype(BF16), (((1,), (1,)), ((), ())), preferred_element_type=F32)
    ke = (k * jnp.exp2(jnp.minimum(tot - c, 0.0))).astype(BF16)
    upd = lax.dot_general(v.astype(BF16), ke, (((0,), (0,)), ((), ())), preferred_element_type=F32)
    rr = _div_pow2(lax.broadcasted_iota(jnp.int32, (wv, wk), 0), dv)
    cc = _div_pow2(lax.broadcasted_iota(jnp.int32, (wv, wk), 1), dk)
    new_st = jnp.where(rr == cc, st * jnp.exp2(tot) + upd, 0.0)

    lane_head = _div_pow2(lax.broadcasted_iota(jnp.int32, (C, wk), 1), dk)
    scores = [jnp.zeros((C, C), F32) for _ in range(HEADS)]
    hsz = C // 2
    while hsz >= GLA_SUB:
        blk = 2 * hsz
        rows = []
        for m in range(C // blk):
            rrow = m * blk + (hsz if reverse else hsz - 1)
            rows.append(jnp.broadcast_to(c[rrow:rrow + 1, :], (blk, wk)))
        ref = jnp.concatenate(rows, axis=0) if len(rows) > 1 else rows[0]
        qt = q * jnp.exp2(jnp.minimum(c - ref, 0.0))
        kt = (k * jnp.exp2(jnp.minimum(ref - c, 0.0))).astype(BF16)
        same = _div_pow2(ii, blk) == _div_pow2(jj, blk)
        if reverse:
            lvl = same & (_mod_pow2(ii, blk) < hsz) & (_mod_pow2(jj, blk) >= hsz)
        else:
            lvl = same & (_mod_pow2(ii, blk) >= hsz) & (_mod_pow2(jj, blk) < hsz)
        for h in range(HEADS):
            qh = jnp.where(lane_head == h, qt, 0.0).astype(BF16)
            s = lax.dot_general(qh, kt, (((1,), (1,)), ((), ())), preferred_element_type=F32)
            scores[h] = scores[h] + jnp.where(lvl, s, 0.0)
        hsz //= 2

    assert dv == C
    er = _div_pow2(lax.broadcasted_iota(jnp.int32, (wk, wv), 0), dk)
    ec = _div_pow2(lax.broadcasted_iota(jnp.int32, (wk, wv), 1), dv)
    expand = jnp.where(er == ec, 1.0, 0.0).astype(BF16)
    dist = (jj - ii) if reverse else (ii - jj)
    same_sub = _div_pow2(ii, GLA_SUB) == _div_pow2(jj, GLA_SUB)
    for lag in range(1 if reverse else 0, GLA_SUB):
        if lag == 0:
            t = q * k
        else:
            shift = (GLA_SUB - lag) if reverse else lag
            ks = pltpu.roll(k.reshape(C // GLA_SUB, GLA_SUB, wk), shift, axis=1).reshape(C, wk)
            cs = pltpu.roll(c.reshape(C // GLA_SUB, GLA_SUB, wk), shift, axis=1).reshape(C, wk)
            t = q * ks * jnp.exp2(jnp.minimum(c - cs, 0.0))
        red = jnp.dot(t.astype(BF16), expand, preferred_element_type=F32)
        on_diag = same_sub & (dist == lag)
        for h in range(HEADS):
            scores[h] = scores[h] + jnp.where(on_diag, red[:, h * dv:(h + 1) * dv], 0.0)

    vb = v.astype(BF16)
    o = o + jnp.concatenate(
        [jnp.dot(scores[h].astype(BF16), vb[:, h * dv:(h + 1) * dv], preferred_element_type=F32) for h in range(HEADS)],
        axis=1)
    return o, new_st


def _gla_kernel(qf, kf, vf, lf, qb, kb, vb, lb, w2f, bf, w2b, bb, of_ref, ob_ref, s_sc, *, qscale):
    @pl.when(pl.program_id(1) == 0)
    def _():
        s_sc[...] = jnp.zeros(s_sc.shape, F32)

    o, st = _gla_direction(qf[...] * qscale, kf[...], vf[...], lf[...], w2f[...], bf[...], s_sc[0], False)
    of_ref[...] = o.astype(of_ref.dtype)
    s_sc[0] = st
    o, st = _gla_direction(qb[...] * qscale, kb[...], vb[...], lb[...], w2b[...], bb[...], s_sc[1], True)
    ob_ref[...] = o.astype(ob_ref.dtype)
    s_sc[1] = st


def _gla(z, *, qcol, kcol, vcol, lcol, w2f, bf, w2b, bb, batch, seq, chunk=128):
    chunk = min(chunk, seq)
    n = seq // chunk
    wk, wv = HEADS * GLA_K_DIM, HEADS * LANE

    def specs(cmap):
        return [pl.BlockSpec((chunk, wk), lambda b, c: (cmap(b, c), qcol)),
                pl.BlockSpec((chunk, wk), lambda b, c: (cmap(b, c), kcol)),
                pl.BlockSpec((chunk, wv), lambda b, c: (cmap(b, c), vcol)),
                pl.BlockSpec((chunk, LANE), lambda b, c: (cmap(b, c), lcol))]

    fmap = lambda b, c: b * n + c
    bmap = lambda b, c: b * n + n - 1 - c
    wspec = [pl.BlockSpec((LANE, wk), lambda b, c: (0, 0)), pl.BlockSpec((1, wk), lambda b, c: (0, 0))]
    out = jax.ShapeDtypeStruct((batch * seq, wv), BF16)
    return pl.pallas_call(
        functools.partial(_gla_kernel, qscale=GLA_K_DIM ** -0.5),
        grid=(batch, n),
        in_specs=specs(fmap) + specs(bmap) + wspec + wspec,
        out_specs=[pl.BlockSpec((chunk, wv), lambda b, c: (fmap(b, c), 0)),
                   pl.BlockSpec((chunk, wv), lambda b, c: (bmap(b, c), 0))],
        out_shape=[out, out],
        scratch_shapes=[pltpu.VMEM((2, wv, wk), F32)],
        compiler_params=_cparams(("parallel", "arbitrary")),
        name="gla",
    )(z, z, z, z, z, z, z, z, w2f, bf, w2b, bb)


def _layer_norm_rows(r, g, b):
    mu = jnp.mean(r, axis=-1, keepdims=True)
    d = r - mu
    var = jnp.mean(d * d, axis=-1, keepdims=True)
    return d * lax.rsqrt(var + LN_EPS) * g + b


def _outproj_kernel(x_ref, fin_ref, of_ref, ob_ref, gate_ref, ng_ref, wa_ref, wb_ref, lg_ref, lb_ref, rw_ref, rb_ref,
                    o_ref, ids_ref, gw_ref, cnt_ref, run_sc, *, group_norm):
    lin = of_ref[...].astype(F32) + ob_ref[...].astype(F32)
    parts = []
    for h in range(HEADS):
        zh = lin[:, h * LANE:(h + 1) * LANE]
        if group_norm:
            mu = jnp.mean(zh, axis=-1, keepdims=True)
            dz = zh - mu
            parts.append(dz * lax.rsqrt(jnp.mean(dz * dz, axis=-1, keepdims=True) + LN_EPS))
        else:
            parts.append(zh * lax.rsqrt(jnp.mean(zh * zh, axis=-1, keepdims=True) + RMS_EPS) * ng_ref[...])
    gate = gate_ref[...]
    lin = jnp.concatenate(parts, axis=1) * (gate * jax.nn.sigmoid(gate))
    y = (jnp.dot(fin_ref[...].astype(BF16), wa_ref[...], preferred_element_type=F32)
         + jnp.dot(lin.astype(BF16), wb_ref[...], preferred_element_type=F32))
    x1 = _layer_norm_rows(ALPHA * x_ref[...] + y, lg_ref[...], lb_ref[...])
    o_ref[...] = x1
    _route_rows(x1, rw_ref, rb_ref, ids_ref, gw_ref, cnt_ref, run_sc)


def _outproj(x, fin, of, ob, gate_src, gate_col, norm_gain, wa, wb, ln_g, ln_b, route_w, route_b, *, group_norm, tm=512):
    n, d = x.shape
    w = HEADS * LANE
    tm = min(tm, n)
    row = lambda i: (i, 0)
    const = lambda i: (0, 0)
    rw_hi = route_w.astype(BF16)
    rw = jnp.stack([rw_hi, (route_w - rw_hi.astype(F32)).astype(BF16)])
    return pl.pallas_call(
        functools.partial(_outproj_kernel, group_norm=group_norm),
        grid=(n // tm,),
        in_specs=[pl.BlockSpec((tm, d), row), pl.BlockSpec((tm, w), row), pl.BlockSpec((tm, w), row),
                  pl.BlockSpec((tm, w), row), pl.BlockSpec((tm, w), lambda i: (i, gate_col)),
                  pl.BlockSpec((1, LANE), const), pl.BlockSpec((w, d), const), pl.BlockSpec((w, d), const),
                  pl.BlockSpec((1, d), const), pl.BlockSpec((1, d), const),
                  pl.BlockSpec((2, d, LANE), lambda i: (0, 0, 0)), pl.BlockSpec((1, LANE), const)],
        out_specs=[pl.BlockSpec((tm, d), row), pl.BlockSpec((tm, LANE), row), pl.BlockSpec((tm, LANE), row),
                   pl.BlockSpec((1, LANE), const)],
        out_shape=[jax.ShapeDtypeStruct((n, d), F32), jax.ShapeDtypeStruct((n, LANE), jnp.int32),
                   jax.ShapeDtypeStruct((n, LANE), F32), jax.ShapeDtypeStruct((1, LANE), jnp.int32)],
        scratch_shapes=[pltpu.VMEM((1, LANE), F32)],
        compiler_params=_cparams(("arbitrary",)),
        name="outproj",
    )(x, fin, of, ob, gate_src, norm_gain.reshape(1, LANE), wa, wb, ln_g.reshape(1, d), ln_b.reshape(1, d), rw, route_b)


def _route_rows(x, w_ref, b_ref, ids_ref, gw_ref, cnt_ref, run_sc):
    @pl.when(pl.program_id(0) == 0)
    def _():
        run_sc[...] = jnp.zeros(run_sc.shape, F32)

    tm = x.shape[0]
    xh = x.astype(BF16)
    xl = (x - xh.astype(F32)).astype(BF16)
    wh, wl = w_ref[0], w_ref[1]
    logits = (jnp.dot(xh, wh, preferred_element_type=F32) + jnp.dot(xh, wl, preferred_element_type=F32)
              + jnp.dot(xl, wh, preferred_element_type=F32)) + b_ref[...]
    lane = lax.broadcasted_iota(jnp.int32, logits.shape, 1)
    neg = -jnp.inf
    gmask = (lane >= N_EXPERTS) & (lane < N_EXPERTS + N_GROUPS)
    gl = jnp.where(gmask, logits, neg)
    gmax = jnp.max(gl, axis=1, keepdims=True)
    lane_f = lane.astype(F32)
    first = lambda hit: jnp.min(jnp.where(hit, lane_f, float(LANE)), axis=1, keepdims=True).astype(jnp.int32)
    gidx = first(gl == gmax) - N_EXPERTS
    p_grp = 1.0 / jnp.sum(jnp.where(gmask, jnp.exp(gl - gmax), 0.0), axis=1, keepdims=True)
    el = jnp.where(_div_pow2(lane, EXPERTS_PER_GROUP) == gidx, logits, neg)
    l1 = jnp.max(el, axis=1, keepdims=True)
    e1 = first(el == l1)
    el2 = jnp.where(lane == e1, neg, el)
    l2 = jnp.max(el2, axis=1, keepdims=True)
    e2 = first(el2 == l2)
    t = jnp.exp(l2 - l1)
    w1 = p_grp / (1.0 + t)
    w2 = p_grp * t / (1.0 + t)

    onehot = jnp.where(lane == e1, 1.0, jnp.where(lane == e2, 1.0, 0.0))
    ri = lax.broadcasted_iota(jnp.int32, (tm, tm), 0)
    ci = lax.broadcasted_iota(jnp.int32, (tm, tm), 1)
    before = jnp.dot(jnp.where(ri > ci, 1.0, 0.0).astype(BF16), onehot.astype(BF16), preferred_element_type=F32)
    before = before + run_sc[...]
    r1 = jnp.sum(jnp.where(lane == e1, before, 0.0), axis=1, keepdims=True).astype(jnp.int32)
    r2 = jnp.sum(jnp.where(lane == e2, before, 0.0), axis=1, keepdims=True).astype(jnp.int32)
    run_sc[...] = run_sc[...] + jnp.sum(onehot, axis=0, keepdims=True)
    cnt_ref[...] = run_sc[...].astype(jnp.int32)
    ids_ref[...] = jnp.where(lane == 0, e1, jnp.where(lane == 1, e2, jnp.where(lane == 2, r1, jnp.where(lane == 3, r2, 0))))
    gw_ref[...] = jnp.where(lane == 0, w1, jnp.where(lane == 1, w2, 0.0))


MOE_BM = 512
SUBL = 8


def _rows_from_linear(ref, rows):
    return jnp.concatenate([ref[pl.ds(s, rows, stride=SUBL), :] for s in range(SUBL)], axis=1)


def _rows_to_linear(ref, val):
    for s in range(SUBL):
        ref[pl.ds(s, val.shape[0], stride=SUBL), :] = val[:, s * LANE:(s + 1) * LANE]


def _dispatch_kernel(pend_ref, padded_ref, dest_ref, x_ref, xs_hbm, idx_smem, zbuf, lin, sem_i, sem_z, sem):
    i = pl.program_id(0)
    tm = x_ref.shape[0]
    bm = zbuf.shape[0] // SUBL

    @pl.when(i == 0)
    def _():
        zbuf[...] = jnp.zeros(zbuf.shape, F32)

        def tail(e):
            start_row = pl.multiple_of((pend_ref[e] - bm) * SUBL, bm * SUBL)
            return pltpu.make_async_copy(zbuf, xs_hbm.at[pl.ds(start_row, bm * SUBL), :], sem_z)

        def start(e, carry):
            @pl.when(padded_ref[e] > 0)
            def _():
                tail(e).start()
            return carry

        def wait(e, carry):
            @pl.when(padded_ref[e] > 0)
            def _():
                tail(e).wait()
            return carry

        lax.fori_loop(0, N_EXPERTS, start, 0)
        lax.fori_loop(0, N_EXPERTS, wait, 0)

        def unused(b):
            start_row = pl.multiple_of(b * bm * SUBL, bm * SUBL)
            return pltpu.make_async_copy(zbuf, xs_hbm.at[pl.ds(start_row, bm * SUBL), :], sem_z)

        first_unused = pend_ref[N_EXPERTS - 1] // bm
        n_blocks = xs_hbm.shape[0] // (bm * SUBL)
        lax.fori_loop(first_unused, n_blocks, lambda b, c: (unused(b).start(), c)[1], 0)
        lax.fori_loop(first_unused, n_blocks, lambda b, c: (unused(b).wait(), c)[1], 0)

    cp = pltpu.make_async_copy(dest_ref.at[i], idx_smem, sem_i)
    cp.start()
    _rows_to_linear(lin, x_ref[...])
    cp.wait()

    def scatter(r, carry):
        src = lin.at[pl.ds(pl.multiple_of(r * SUBL, SUBL), SUBL), :]
        for k in range(2):
            dst = pl.multiple_of(idx_smem[2 * r + k] * SUBL, SUBL)
            pltpu.make_async_copy(src, xs_hbm.at[pl.ds(dst, SUBL), :], sem).start(priority=k)
        return carry

    lax.fori_loop(0, tm, scatter, 0, unroll=8)
    for k in range(2):
        pltpu.make_async_copy(lin, xs_hbm.at[pl.ds(0, tm * SUBL), :], sem).wait()


def _dispatch(x, dest, pend, padded, cap, *, tm=512):
    n, d = x.shape
    assert d == SUBL * LANE
    tm = min(tm, n)
    nt = n // tm
    grid_spec = pltpu.PrefetchScalarGridSpec(
        num_scalar_prefetch=2,
        grid=(nt,),
        in_specs=[pl.BlockSpec((nt, 2 * tm), lambda i, pe, pa: (0, 0)),
                  pl.BlockSpec((tm, d), lambda i, pe, pa: (i, 0))],
        out_specs=pl.BlockSpec(memory_space=pl.ANY),
        scratch_shapes=[pltpu.SMEM((2 * tm,), jnp.int32), pltpu.VMEM((MOE_BM * SUBL, LANE), F32),
                        pltpu.VMEM((tm * SUBL, LANE), F32),
                        pltpu.SemaphoreType.DMA(()), pltpu.SemaphoreType.DMA(()), pltpu.SemaphoreType.DMA(())],
    )
    return pl.pallas_call(
        _dispatch_kernel,
        grid_spec=grid_spec,
        out_shape=jax.ShapeDtypeStruct((cap * SUBL, LANE), F32),
        compiler_params=_cparams(("arbitrary",)),
        name="dispatch",
    )(pend, padded, dest.reshape(nt, 2 * tm), x)


def _experts_kernel(blk_e_ref, nused_ref, xs_ref, wg_ref, wu_ref, wd_ref, ys_ref, wgb, wub, wdb, hbuf):
    i = pl.program_id(0)
    nb = pl.num_programs(0) - 1
    cur = jnp.minimum(i, nb - 1)
    prv = jnp.maximum(i - 1, 0)

    @pl.when(i == 0)
    def _():
        hbuf[...] = jnp.zeros(hbuf.shape, BF16)

    @pl.when(jnp.logical_or(i == 0, blk_e_ref[cur] != blk_e_ref[jnp.minimum(prv, nb - 1)]))
    def _():
        wgb[...] = wg_ref[...].astype(BF16)
        wub[...] = wu_ref[...].astype(BF16)

    @pl.when(jnp.logical_or(i == 0, blk_e_ref[prv] != blk_e_ref[jnp.maximum(i - 2, 0)]))
    def _():
        wdb[...] = wd_ref[...].astype(BF16)

    slot = lax.rem(i, 2)
    y = jnp.dot(hbuf[1 - slot], wdb[...], preferred_element_type=F32)
    live = jnp.logical_and(i >= 1, i - 1 < nused_ref[0])
    _rows_to_linear(ys_ref, jnp.where(live, y, 0.0))
    xb = _rows_from_linear(xs_ref, xs_ref.shape[0] // SUBL).astype(BF16)
    hg = jnp.dot(xb, wgb[...], preferred_element_type=F32)
    hu = jnp.dot(xb, wub[...], preferred_element_type=F32)
    hbuf[slot] = (hg * jax.nn.sigmoid(hg) * hu).astype(BF16)


def _experts(xs, blk_e, n_used, w_gate, w_up, w_down, layer):
    cap = xs.shape[0] // SUBL
    bm = MOE_BM
    nb = cap // bm
    d, de = w_gate.shape[2], w_gate.shape[3]
    row_in = lambda i, be, nu: (jnp.minimum(i, nu[0] - 1), 0)
    row = lambda i, be, nu: (jnp.maximum(i - 1, 0), 0)
    w_cur = lambda i, be, nu: (layer, be[jnp.minimum(i, nb - 1)], 0, 0)
    w_prv = lambda i, be, nu: (layer, be[jnp.maximum(i - 1, 0)], 0, 0)
    grid_spec = pltpu.PrefetchScalarGridSpec(
        num_scalar_prefetch=2,
        grid=(nb + 1,),
        in_specs=[pl.BlockSpec((bm * SUBL, LANE), row_in),
                  pl.BlockSpec((None, None, d, de), w_cur),
                  pl.BlockSpec((None, None, d, de), w_cur),
                  pl.BlockSpec((None, None, de, d), w_prv)],
        out_specs=pl.BlockSpec((bm * SUBL, LANE), row),
        scratch_shapes=[pltpu.VMEM((d, de), BF16), pltpu.VMEM((d, de), BF16), pltpu.VMEM((de, d), BF16),
                        pltpu.VMEM((2, bm, de), BF16)],
    )
    return pl.pallas_call(
        _experts_kernel,
        grid_spec=grid_spec,
        out_shape=jax.ShapeDtypeStruct((cap * SUBL, LANE), F32),
        compiler_params=_cparams(("arbitrary",)),
        name="experts",
    )(blk_e, n_used, xs, w_gate, w_up, w_down)


def _combine_kernel(dest_ref, x_ref, gw_ref, g_ref, b_ref, ys_hbm, o_ref, idx_smem, ybuf, sem_i, sem):
    i = pl.program_id(0)
    tm = x_ref.shape[0]

    def issue(tile, slot):
        cp = pltpu.make_async_copy(dest_ref.at[tile], idx_smem, sem_i)
        cp.start()
        cp.wait()

        def gather(r, carry):
            row = pl.multiple_of(r * SUBL, SUBL)
            for k in range(2):
                src = pl.multiple_of(idx_smem[2 * r + k] * SUBL, SUBL)
                pltpu.make_async_copy(ys_hbm.at[pl.ds(src, SUBL), :], ybuf.at[slot, k, pl.ds(row, SUBL), :],
                                      sem.at[slot]).start(priority=k)
            return carry

        lax.fori_loop(0, tm, gather, 0, unroll=8)

    slot = lax.rem(i, 2)

    @pl.when(i == 0)
    def _():
        issue(0, 0)

    @pl.when(i + 1 < pl.num_programs(0))
    def _():
        issue(i + 1, 1 - slot)

    for k in range(2):
        pltpu.make_async_copy(ys_hbm.at[pl.ds(0, tm * SUBL), :], ybuf.at[slot, k], sem.at[slot]).wait()
    gw = gw_ref[...]
    ffn = (_rows_from_linear(ybuf.at[slot, 0], tm) * gw[:, 0:1] + _rows_from_linear(ybuf.at[slot, 1], tm) * gw[:, 1:2])
    o_ref[...] = _layer_norm_rows(ALPHA * x_ref[...] + ffn, g_ref[...], b_ref[...])


def _combine(x, ys, dest, gw, ln_g, ln_b, *, tm=512):
    n, d = x.shape
    tm = min(tm, n)
    nt = n // tm
    return pl.pallas_call(
        _combine_kernel,
        grid=(nt,),
        in_specs=[pl.BlockSpec((nt, 2 * tm), lambda i: (0, 0)),
                  pl.BlockSpec((tm, d), lambda i: (i, 0)), pl.BlockSpec((tm, LANE), lambda i: (i, 0)),
                  pl.BlockSpec((1, d), lambda i: (0, 0)), pl.BlockSpec((1, d), lambda i: (0, 0)),
                  pl.BlockSpec(memory_space=pl.ANY)],
        out_specs=pl.BlockSpec((tm, d), lambda i: (i, 0)),
        out_shape=jax.ShapeDtypeStruct((n, d), F32),
        scratch_shapes=[pltpu.SMEM((2 * tm,), jnp.int32), pltpu.VMEM((2, 2, tm * SUBL, LANE), F32),
                        pltpu.SemaphoreType.DMA(()), pltpu.SemaphoreType.DMA((2,))],
        compiler_params=_cparams(("arbitrary",)),
        name="combine",
    )(dest.reshape(nt, 2 * tm), x, gw, ln_g.reshape(1, d), ln_b.reshape(1, d), ys)


def _router_params(w_grp, b_grp, w_exp, b_exp):
    d = w_exp.shape[0]
    wr = jnp.zeros((d, LANE), F32).at[:, :N_EXPERTS].set(w_exp).at[:, N_EXPERTS:N_EXPERTS + N_GROUPS].set(w_grp)
    br = jnp.zeros((1, LANE), F32).at[0, :N_EXPERTS].set(b_exp).at[0, N_EXPERTS:N_EXPERTS + N_GROUPS].set(b_grp)
    return wr, br


def _moe(x, routing, w_gate, w_up, w_down, layer, ln_g, ln_b):
    n, d = x.shape
    ids, gw, cnt = routing
    bm = MOE_BM
    counts = cnt[0, :N_EXPERTS]
    padded = (counts + bm - 1) // bm * bm
    pend = jnp.cumsum(padded)
    pstart = pend - padded
    e, r = ids[:, 0:2], ids[:, 2:4]
    onehot = e[:, :, None] == jnp.arange(N_EXPERTS, dtype=jnp.int32)[None, None, :]
    dest = jnp.sum(jnp.where(onehot, pstart[None, None, :], 0), axis=-1) + r
    cap = 2 * n + N_EXPERTS * bm
    nb = cap // bm
    blk_start = jnp.arange(nb, dtype=jnp.int32) * bm
    blk_e = jnp.minimum(jnp.sum((pend[None, :] <= blk_start[:, None]).astype(jnp.int32), axis=1), N_EXPERTS - 1)
    n_used = (pend[-1:] // bm).astype(jnp.int32)
    xs = _dispatch(x, dest, pend.astype(jnp.int32), padded.astype(jnp.int32), cap)
    ys = _experts(xs, blk_e, n_used, w_gate, w_up, w_down, layer)
    return _combine(x, ys, dest, gw, ln_g, ln_b)


def _even_layer(x, batch, seq, layer_idx, w_in, dec_f, dec_b, lq1, lk1, lq2, lk2, subln, w_out, ln_g, ln_b, route):
    d = x.shape[1]
    w = HEADS * LANE
    kw = dict(batch=batch, seq=seq)
    diff_seg = [(0, DIFF_ROT_DIM), (DIFF_HEAD_DIM, DIFF_ROT_DIM)]
    q, k, v, gate, dq, dk, dvt = _even_in(
        x, w_in.astype(BF16), _rope_tables(seq, [(0, LANE)], RET_THETA),
        _rope_tables(seq, diff_seg, ROPE_THETA), **kw)
    decays = jnp.stack([dec_f, dec_b]).astype(F32)
    of, ob = _retention(q, k, v, decays, **kw)
    lam_init = 0.8 - 0.6 * math.exp(-0.3 * layer_idx)
    diff = _flash(dq, dk, dvt, diff=(lq1, lk1, lq2, lk2, subln), lam_init=lam_init, tq=2048, **kw)
    wo = w_out.astype(BF16)
    return _outproj(x, diff, of, ob, gate, 0, jnp.ones((LANE,), F32), wo[w:], wo[:w], ln_g, ln_b, *route,
                    group_norm=True)


def _odd_layer(x, batch, seq, w_in, q_norm, w_uq, kv_norm, w_ukv, w2_f, b_f, w2_b, b_b, gla_norm, w_out, ln_g, ln_b,
               route):
    d = x.shape[1]
    w = HEADS * LANE
    o = np.cumsum([0, MLA_Q_RANK, MLA_KV_RANK, MLA_ROPE, HEADS * GLA_K_DIM, HEADS * GLA_K_DIM, w, w,
                   GLA_GATE_RANK, GLA_GATE_RANK]).tolist()
    zeros = lambda c: jnp.zeros((d, c), F32)
    w_in2 = jnp.concatenate([
        w_in[:, o[0]:o[2]], zeros(MLA_NOPE), w_in[:, o[2]:o[3]], zeros(LANE - MLA_NOPE - MLA_ROPE),
        w_in[:, o[3]:o[7]], w_in[:, o[7]:o[9]], zeros(LANE - 2 * GLA_GATE_RANK)], axis=1).astype(BF16)
    kw = dict(batch=batch, seq=seq)
    qd = MLA_NOPE + MLA_ROPE
    w_uq2 = jnp.pad(w_uq.reshape(MLA_Q_RANK, HEADS, qd), ((0, 0), (0, 0), (0, LANE - qd))).reshape(MLA_Q_RANK, w)
    ukv = w_ukv.reshape(MLA_KV_RANK, HEADS, MLA_NOPE + MLA_V)
    w_uk2 = jnp.pad(ukv[:, :, :MLA_NOPE], ((0, 0), (0, 0), (0, LANE - MLA_NOPE))).reshape(MLA_KV_RANK, w)
    w_uv2 = ukv[:, :, MLA_NOPE:].reshape(MLA_KV_RANK, w)
    q, k, vt, zg = _odd_in(x, w_in2, q_norm, w_uq2.astype(BF16), kv_norm,
                           jnp.concatenate([w_uk2, w_uv2], axis=1).astype(BF16),
                           _rope_tables(seq, [(MLA_NOPE, MLA_ROPE)], ROPE_THETA), **kw)
    mla = _flash(q, k, vt, **kw)
    wk = HEADS * GLA_K_DIM
    pad_rows = lambda m, r0: jnp.zeros((LANE, wk), F32).at[r0:r0 + GLA_GATE_RANK].set(m).astype(BF16)
    of, ob = _gla(zg, qcol=0, kcol=1, vcol=1, lcol=12,
                  w2f=pad_rows(w2_f, 0), bf=b_f.reshape(1, wk), w2b=pad_rows(w2_b, GLA_GATE_RANK), bb=b_b.reshape(1, wk), **kw)
    wo = w_out.astype(BF16)
    return _outproj(x, mla, of, ob, zg, 2, gla_norm, wo[:w], wo[w:], ln_g, ln_b, *route, group_norm=False)


def kernel(x, ev_w_in, ev_ret_decay_f, ev_ret_decay_b, ev_lq1, ev_lk1, ev_lq2, ev_lk2, ev_subln, ev_w_out, od_w_in, od_q_norm, od_w_uq, od_kv_norm, od_w_ukv, od_gla_w2_f, od_gla_b_f, od_gla_w2_b, od_gla_b_b, od_gla_norm, od_w_out, ln1_g, ln1_b, ln2_g, ln2_b, moe_w_grp, moe_b_grp, moe_w_exp, moe_b_exp, moe_w_gate, moe_w_up, moe_w_down):
    batch, seq, d = x.shape
    h = x.reshape(batch * seq, d)
    for i in range(DEPTH):
        j = i // 2
        route = _router_params(moe_w_grp[i], moe_b_grp[i], moe_w_exp[i], moe_b_exp[i])
        if i % 2 == 0:
            h, *routing = _even_layer(h, batch, seq, i, ev_w_in[j], ev_ret_decay_f[j], ev_ret_decay_b[j], ev_lq1[j],
                                      ev_lk1[j], ev_lq2[j], ev_lk2[j], ev_subln[j], ev_w_out[j], ln1_g[i], ln1_b[i], route)
        else:
            h, *routing = _odd_layer(h, batch, seq, od_w_in[j], od_q_norm[j], od_w_uq[j], od_kv_norm[j], od_w_ukv[j],
                                     od_gla_w2_f[j], od_gla_b_f[j], od_gla_w2_b[j], od_gla_b_b[j], od_gla_norm[j],
                                     od_w_out[j], ln1_g[i], ln1_b[i], route)
        h = _moe(h, routing, moe_w_gate, moe_w_up, moe_w_down, i, ln2_g[i], ln2_b[i])
    return h.reshape(batch, seq, d)
```

```python
import functools
import math

import numpy as np
import jax
import jax.numpy as jnp
from jax import lax
from jax.experimental import pallas as pl
from jax.experimental.pallas import tpu as pltpu

F32 = jnp.float32
BF16 = jnp.bfloat16

HEADS = 4
LANE = 128
RET_THETA = 10000.0
ROPE_THETA = 500000.0
DIFF_HEAD_DIM = 64
DIFF_ROT_DIM = 16
MLA_Q_RANK = 256
MLA_KV_RANK = 128
MLA_NOPE = 64
MLA_ROPE = 32
MLA_V = 128
GLA_K_DIM = 64
GLA_GATE_RANK = 16
GLA_TAU = 16.0
N_GROUPS = 4
EXPERTS_PER_GROUP = 8
N_EXPERTS = N_GROUPS * EXPERTS_PER_GROUP
DEPTH = 2
ALPHA = (2.0 * DEPTH) ** 0.25
LN_EPS = 1e-5
RMS_EPS = 1e-6

VMEM_LIMIT = 48 * 1024 * 1024


def _div_pow2(x, n):
    return lax.shift_right_logical(x, int(n).bit_length() - 1)


def _mod_pow2(x, n):
    return lax.bitwise_and(x, int(n) - 1)


def _cparams(sem):
    return pltpu.CompilerParams(dimension_semantics=sem, vmem_limit_bytes=VMEM_LIMIT)


def _rope_heads(z, tabs, sh, scale):
    c, sa, sb = tabs
    outs = []
    for h in range(HEADS):
        zh = z[:, h * LANE:(h + 1) * LANE]
        outs.append((zh * c + pltpu.roll(zh, sh, axis=1) * sa + pltpu.roll(zh, LANE - sh, axis=1) * sb) * scale)
    return outs


def _even_in_kernel(x_ref, w_ref, rc, rsa, rsb, dc, dsa, dsb, q_ref, k_ref, v_ref, g_ref, dq_ref, dk_ref, dvt_ref):
    w = HEADS * LANE
    xb = x_ref[...].astype(BF16)
    part = lambda t: jnp.dot(xb, w_ref[:, t * w:(t + 1) * w], preferred_element_type=F32)
    ret_t = (rc[...], rsa[...], rsb[...])
    diff_t = (dc[...], dsa[...], dsb[...])
    for h, o in enumerate(_rope_heads(part(0), ret_t, LANE // 2, 1.0)):
        q_ref[h] = o.astype(BF16)
    for h, o in enumerate(_rope_heads(part(1), ret_t, LANE // 2, LANE ** -0.5)):
        k_ref[h] = o.astype(BF16)
    rv = part(2)
    for h in range(HEADS):
        v_ref[h] = rv[:, h * LANE:(h + 1) * LANE].astype(BF16)
    g_ref[...] = part(3)
    for h, o in enumerate(_rope_heads(part(4), diff_t, DIFF_ROT_DIM // 2, DIFF_HEAD_DIM ** -0.5 * LOG2E)):
        dq_ref[h] = o.T.astype(BF16)
    for h, o in enumerate(_rope_heads(part(5), diff_t, DIFF_ROT_DIM // 2, 1.0)):
        dk_ref[h] = o.astype(BF16)
    dv = part(6)
    for h in range(HEADS):
        dvt_ref[h, :LANE, :] = dv[:, h * LANE:(h + 1) * LANE].T.astype(BF16)
        dvt_ref[h, LANE:, :] = jnp.ones((ONES_ROWS, dvt_ref.shape[2]), BF16)


def _even_in(x, w, ret_tabs, diff_tabs, *, batch, seq):
    n, d = x.shape
    tm = min(FLASH_TK_DIFF, seq // 2)
    nt = seq // tm
    hw = HEADS * LANE
    heads = jax.ShapeDtypeStruct((batch, HEADS, seq, LANE), BF16)
    head_spec = pl.BlockSpec((None, HEADS, tm, LANE), lambda i: (i // nt, 0, i % nt, 0))
    heads_t = jax.ShapeDtypeStruct((batch, HEADS, LANE, seq), BF16)
    head_t_spec = pl.BlockSpec((None, HEADS, LANE, tm), lambda i: (i // nt, 0, 0, i % nt))
    tab_spec = pl.BlockSpec((tm, LANE), lambda i: (i % nt, 0))
    return pl.pallas_call(
        _even_in_kernel,
        grid=(n // tm,),
        in_specs=[pl.BlockSpec((tm, d), lambda i: (i, 0)), pl.BlockSpec((d, 7 * hw), lambda i: (0, 0))] + [tab_spec] * 6,
        out_specs=[head_spec, head_spec, head_spec, pl.BlockSpec((tm, hw), lambda i: (i, 0)), head_t_spec, head_spec,
                   pl.BlockSpec((None, HEADS, None, LANE + ONES_ROWS, tm), lambda i: (i // nt, 0, i % nt, 0, 0))],
        out_shape=[heads, heads, heads, jax.ShapeDtypeStruct((n, hw), F32), heads_t, heads,
                   jax.ShapeDtypeStruct((batch, HEADS, nt, LANE + ONES_ROWS, tm), BF16)],
        compiler_params=_cparams(("parallel",)),
        name="even_in",
    )(x, w, *ret_tabs, *diff_tabs)


def _rms_rows(z, g):
    return z * lax.rsqrt(jnp.mean(z * z, axis=-1, keepdims=True) + RMS_EPS) * g


def _odd_in_kernel(x_ref, w_ref, qn_ref, wq_ref, kvn_ref, wkv_ref, tc, tsa, tsb, q_ref, k_ref, vt_ref, zg_ref):
    hw = HEADS * LANE
    mla_w = MLA_Q_RANK + MLA_KV_RANK + LANE
    xb = x_ref[...].astype(BF16)
    zg_ref[...] = jnp.dot(xb, w_ref[:, mla_w:], preferred_element_type=F32)
    z1 = jnp.dot(xb, w_ref[:, :mla_w], preferred_element_type=F32)
    tabs = (tc[...], tsa[...], tsb[...])
    sh = MLA_ROPE // 2
    qh = jnp.dot(_rms_rows(z1[:, :MLA_Q_RANK], qn_ref[...]).astype(BF16), wq_ref[...], preferred_element_type=F32)
    for h, o in enumerate(_rope_heads(qh, tabs, sh, (MLA_NOPE + MLA_ROPE) ** -0.5 * LOG2E)):
        q_ref[h] = o.T.astype(BF16)
    ckv = _rms_rows(z1[:, MLA_Q_RANK:MLA_Q_RANK + MLA_KV_RANK], kvn_ref[...]).astype(BF16)
    kv = jnp.dot(ckv, wkv_ref[...], preferred_element_type=F32)
    kr = z1[:, MLA_Q_RANK + MLA_KV_RANK:]
    kr = kr * tabs[0] + pltpu.roll(kr, sh, axis=1) * tabs[1] + pltpu.roll(kr, LANE - sh, axis=1) * tabs[2]
    for h in range(HEADS):
        k_ref[h] = (kv[:, h * LANE:(h + 1) * LANE] + kr).astype(BF16)
        vt_ref[h, :LANE, :] = kv[:, hw + h * LANE:hw + (h + 1) * LANE].T.astype(BF16)
        vt_ref[h, LANE:, :] = jnp.ones((ONES_ROWS, vt_ref.shape[2]), BF16)


def _odd_in(x, w, q_norm, w_uq, kv_norm, w_ukv, tabs, *, batch, seq, tm=512):
    n, d = x.shape
    tm = min(tm, seq // 2)
    nt = seq // tm
    tk = min(FLASH_TK, seq // 2)
    per = tk // tm
    hw = HEADS * LANE
    gw_ = w.shape[1] - (MLA_Q_RANK + MLA_KV_RANK + LANE)
    heads = jax.ShapeDtypeStruct((batch, HEADS, seq, LANE), BF16)
    head_spec = pl.BlockSpec((None, HEADS, tm, LANE), lambda i: (i // nt, 0, i % nt, 0))
    tab_spec = pl.BlockSpec((tm, LANE), lambda i: (i % nt, 0))
    const = lambda i: (0, 0)
    return pl.pallas_call(
        _odd_in_kernel,
        grid=(n // tm,),
        in_specs=[pl.BlockSpec((tm, d), lambda i: (i, 0)), pl.BlockSpec(w.shape, const),
                  pl.BlockSpec((1, MLA_Q_RANK), const), pl.BlockSpec(w_uq.shape, const),
                  pl.BlockSpec((1, MLA_KV_RANK), const), pl.BlockSpec(w_ukv.shape, const)] + [tab_spec] * 3,
        out_specs=[pl.BlockSpec((None, HEADS, LANE, tm), lambda i: (i // nt, 0, 0, i % nt)), head_spec,
                   pl.BlockSpec((None, HEADS, None, LANE + ONES_ROWS, tm),
                                lambda i: (i // nt, 0, (i % nt) // per, 0, (i % nt) % per)),
                   pl.BlockSpec((tm, gw_), lambda i: (i, 0))],
        out_shape=[jax.ShapeDtypeStruct((batch, HEADS, LANE, seq), BF16), heads, jax.ShapeDtypeStruct((batch, HEADS, seq // tk, LANE + ONES_ROWS, tk), BF16),
                   jax.ShapeDtypeStruct((n, gw_), F32)],
        compiler_params=_cparams(("parallel",)),
        name="odd_in",
    )(x, w, q_norm.reshape(1, -1), w_uq, kv_norm.reshape(1, -1), w_ukv, *tabs)


def _rope_tables(seq, segs, theta):
    pos = jnp.arange(seq, dtype=F32)
    inv = jnp.zeros((LANE,), F32)
    lo = np.zeros((LANE,), bool)
    hi = np.zeros((LANE,), bool)
    for start, rot in segs:
        half = rot // 2
        f = jnp.power(jnp.float32(theta), -jnp.arange(0, rot, 2, dtype=F32) / rot)
        inv = inv.at[start:start + half].set(f).at[start + half:start + rot].set(f)
        lo[start:start + half] = True
        hi[start + half:start + rot] = True
    ang = pos[:, None] * inv[None, :]
    cos, sin = jnp.cos(ang), jnp.sin(ang)
    c = jnp.where(lo | hi, cos, 1.0)
    sa = jnp.where(hi, sin, 0.0)
    sb = jnp.where(lo, -sin, 0.0)
    return c, sa, sb


ONES_ROWS = 16
LOG2E = math.log2(math.e)
FLASH_TK = 1024
FLASH_TK_DIFF = 512


def _flash_kernel(*refs, ncomp, nk, lam_init):
    if ncomp == 2:
        q_ref, k_ref, vt_ref, lq1, lk1, lq2, lk2, g_ref, o_ref, *scr = refs
    else:
        q_ref, k_ref, vt_ref, o_ref, *scr = refs
    qm_sc, m_sc, acc_sc, s0, s1, cm0, cm1, p0, p1, al0, al1 = scr
    tk = s0.shape[1]
    q = q_ref[...]
    if ncomp == 2:
        chan = lax.broadcasted_iota(jnp.int32, q.shape, 0)
        zero = jnp.zeros_like(q)
        qm_sc[0] = jnp.where(chan < DIFF_HEAD_DIM, q, zero)
        qm_sc[1] = jnp.where(chan >= DIFF_HEAD_DIM, q, zero)
    else:
        qm_sc[0] = q
    m_sc[...] = jnp.full(m_sc.shape, -jnp.inf, F32)
    acc_sc[...] = jnp.zeros(acc_sc.shape, F32)

    def scores(j, s_ref, cm_ref):
        k = k_ref[j * tk:(j + 1) * tk, :]
        for c in range(ncomp):
            s = jnp.dot(k, qm_sc[c], preferred_element_type=F32)
            s_ref[c] = s
            cm_ref[c] = jnp.max(s, axis=0, keepdims=True)

    def softmax(s_ref, cm_ref, p_ref, al_ref):
        for c in range(ncomp):
            m_old = m_sc[c]
            m_new = jnp.maximum(m_old, cm_ref[c])
            al_ref[c] = jnp.exp2(m_old - m_new)
            p_ref[c] = jnp.exp2(s_ref[c] - m_new).astype(BF16)
            m_sc[c] = m_new

    def values(j, p_ref, al_ref):
        vt = vt_ref[j]
        for c in range(ncomp):
            acc_sc[c] = al_ref[c] * acc_sc[c] + jnp.dot(vt, p_ref[c], preferred_element_type=F32)

    bufs = ((s0, cm0, p0, al0), (s1, cm1, p1, al1))
    scores(0, s0, cm0)
    for j in range(nk):
        s_c, cm_c, p_c, al_c = bufs[j % 2]
        s_n, cm_n, p_n, al_n = bufs[(j + 1) % 2]
        if j + 1 < nk:
            scores(j + 1, s_n, cm_n)
        softmax(s_c, cm_c, p_c, al_c)
        if j >= 1:
            values(j - 1, p_n, al_n)
    values(nk - 1, *bufs[(nk - 1) % 2][2:])

    def normalised(c):
        acc = acc_sc[c]
        return acc[:LANE] / acc[LANE:LANE + 1]

    o = normalised(0)
    if ncomp == 2:
        lam = (jnp.exp(jnp.sum(lq1[...] * lk1[...], keepdims=True))
               - jnp.exp(jnp.sum(lq2[...] * lk2[...], keepdims=True)) + lam_init)
        o = o - lam * normalised(1)
        o = o * lax.rsqrt(jnp.mean(o * o, axis=0, keepdims=True) + RMS_EPS) * g_ref[...] * (1.0 - lam_init)
    o_ref[...] = o.T.astype(o_ref.dtype)


def _flash(q, k, vt, *, batch, seq, tq=512, diff=None, lam_init=0.0):
    nk, vrows, tk = vt.shape[2], vt.shape[3], vt.shape[4]
    assert vrows == LANE + ONES_ROWS
    tq = min(tq, seq)
    nq = seq // tq
    ncomp = 2 if diff is not None else 1
    in_specs = [
        pl.BlockSpec((None, None, LANE, tq), lambda b, h, i: (b, h, 0, i)),
        pl.BlockSpec((None, None, seq, LANE), lambda b, h, i: (b, h, 0, 0)),
        pl.BlockSpec((None, None, nk, vrows, tk), lambda b, h, i: (b, h, 0, 0, 0)),
    ]
    args = [q, k, vt]
    if diff is not None:
        lq1, lk1, lq2, lk2, subln = diff
        for v in (lq1, lk1, lq2, lk2):
            in_specs.append(pl.BlockSpec((1, DIFF_HEAD_DIM), lambda b, h, i: (0, 0)))
            args.append(v.reshape(1, DIFF_HEAD_DIM))
        in_specs.append(pl.BlockSpec((LANE, 1), lambda b, h, i: (0, 0)))
        args.append(subln.reshape(LANE, 1))
    return pl.pallas_call(
        functools.partial(_flash_kernel, ncomp=ncomp, nk=nk, lam_init=lam_init),
        grid=(batch, HEADS, nq),
        in_specs=in_specs,
        out_specs=pl.BlockSpec((tq, LANE), lambda b, h, i: (b * nq + i, h)),
        out_shape=jax.ShapeDtypeStruct((batch * seq, HEADS * LANE), BF16),
        scratch_shapes=[pltpu.VMEM((ncomp, LANE, tq), BF16),
                        pltpu.VMEM((ncomp, 1, tq), F32), pltpu.VMEM((ncomp, vrows, tq), F32),
                        pltpu.VMEM((ncomp, tk, tq), F32), pltpu.VMEM((ncomp, tk, tq), F32),
                        pltpu.VMEM((ncomp, 1, tq), F32), pltpu.VMEM((ncomp, 1, tq), F32),
                        pltpu.VMEM((ncomp, tk, tq), BF16), pltpu.VMEM((ncomp, tk, tq), BF16),
                        pltpu.VMEM((ncomp, 1, tq), F32), pltpu.VMEM((ncomp, 1, tq), F32)],
        compiler_params=_cparams(("parallel", "parallel", "parallel")),
        name="flash_diff" if diff is not None else "flash_mla",
    )(*args)


def _ret_kernel(dec_ref, qf, kf, vf, qb, kb, vb, of_ref, ob_ref, s_sc, *, chunk):
    @pl.when(pl.program_id(1) == 0)
    def _():
        s_sc[...] = jnp.zeros(s_sc.shape, F32)

    ii = lax.broadcasted_iota(jnp.int32, (chunk, chunk), 0)
    jj = lax.broadcasted_iota(jnp.int32, (chunk, chunk), 1)
    r = lax.broadcasted_iota(jnp.int32, (chunk, 1), 0).astype(F32)
    for d, (q_ref, k_ref, v_ref, o_ref) in enumerate(((qf, kf, vf, of_ref), (qb, kb, vb, ob_ref))):
        for h in range(HEADS):
            la = -jnp.exp(jnp.full((1, 1), dec_ref[d, h], F32))
            if d == 0:
                mask, dist = ii >= jj, (ii - jj).astype(F32)
                qdec, kdec = jnp.exp(la * (r + 1.0)), jnp.exp(la * (chunk - 1.0 - r))
            else:
                mask, dist = jj > ii, (jj - ii).astype(F32)
                qdec, kdec = jnp.exp(la * (chunk - r)), jnp.exp(la * r)
            decay = jnp.where(mask, jnp.exp(jnp.where(mask, dist * la, 0.0)), 0.0)
            q, k, v = q_ref[h], k_ref[h], v_ref[h]
            s = lax.dot_general(q, k, (((1,), (1,)), ((), ())), preferred_element_type=F32)
            o = jnp.dot((s * decay).astype(BF16), v, preferred_element_type=F32)
            state = s_sc[d, h]
            o = o + qdec * jnp.dot(q, state.astype(BF16), preferred_element_type=F32)
            kd = (k.astype(F32) * kdec).astype(BF16)
            s_sc[d, h] = jnp.exp(la * float(chunk)) * state + lax.dot_general(
                kd, v, (((0,), (0,)), ((), ())), preferred_element_type=F32)
            o_ref[:, h * LANE:(h + 1) * LANE] = o.astype(o_ref.dtype)


def _retention(q, k, v, decays, *, batch, seq, chunk=256):
    chunk = min(chunk, seq)
    n = seq // chunk
    fwd = pl.BlockSpec((None, HEADS, chunk, LANE), lambda b, c: (b, 0, c, 0))
    bwd = pl.BlockSpec((None, HEADS, chunk, LANE), lambda b, c: (b, 0, n - 1 - c, 0))
    w = HEADS * LANE
    out = jax.ShapeDtypeStruct((batch * seq, w), BF16)
    return pl.pallas_call(
        functools.partial(_ret_kernel, chunk=chunk),
        grid=(batch, n),
        in_specs=[pl.BlockSpec(memory_space=pltpu.SMEM), fwd, fwd, fwd, bwd, bwd, bwd],
        out_specs=[pl.BlockSpec((chunk, w), lambda b, c: (b * n + c, 0)),
                   pl.BlockSpec((chunk, w), lambda b, c: (b * n + n - 1 - c, 0))],
        out_shape=[out, out],
        scratch_shapes=[pltpu.VMEM((2, HEADS, LANE, LANE), F32)],
        compiler_params=_cparams(("parallel", "arbitrary")),
        name="retention",
    )(decays, q, k, v, q, k, v)


GLA_SUB = 8


def _split3(x):
    x1 = x.astype(BF16)
    r1 = x - x1.astype(F32)
    x2 = r1.astype(BF16)
    x3 = (r1 - x2.astype(F32)).astype(BF16)
    return x1, x2, x3


def _gla_direction(q, k, v, lr, w2, bias, st, reverse):
    C, wk = q.shape
    wv = v.shape[1]
    dk, dv = wk // HEADS, wv // HEADS
    z = jnp.dot(lr.astype(BF16), w2, preferred_element_type=F32) + bias
    g = (jnp.minimum(z, 0.0) - jnp.log(1.0 + jnp.exp(-jnp.abs(z)))) * (LOG2E / GLA_TAU)
    ii = lax.broadcasted_iota(jnp.int32, (C, C), 0)
    jj = lax.broadcasted_iota(jnp.int32, (C, C), 1)
    tri = jnp.where(ii >= jj, 1.0, 0.0).astype(BF16)
    b = sum(jnp.dot(tri, part, preferred_element_type=F32) for part in _split3(g))
    tot = b[C - 1:C, :]
    c = (tot - b + g) if reverse else b

    qe = (q * jnp.exp2(jnp.minimum(c, 0.0))).astype(BF16)
    o = lax.dot_general(qe, st.astype(BF16), (((1,), (1,)), ((), ())), preferred_element_type=F32)
    ke = (k * jnp.exp2(jnp.minimum(tot - c, 0.0))).astype(BF16)
    upd = lax.dot_general(v.astype(BF16), ke, (((0,), (0,)), ((), ())), preferred_element_type=F32)
    rr = _div_pow2(lax.broadcasted_iota(jnp.int32, (wv, wk), 0), dv)
    cc = _div_pow2(lax.broadcasted_iota(jnp.int32, (wv, wk), 1), dk)
    new_st = jnp.where(rr == cc, st * jnp.exp2(tot) + upd, 0.0)

    lane_head = _div_pow2(lax.broadcasted_iota(jnp.int32, (C, wk), 1), dk)
    scores = [jnp.zeros((C, C), F32) for _ in range(HEADS)]
    hsz = C // 2
    while hsz >= GLA_SUB:
        blk = 2 * hsz
        rows = []
        for m in range(C // blk):
            rrow = m * blk + (hsz if reverse else hsz - 1)
            rows.append(jnp.broadcast_to(c[rrow:rrow + 1, :], (blk, wk)))
        ref = jnp.concatenate(rows, axis=0) if len(rows) > 1 else rows[0]
        qt = q * jnp.exp2(jnp.minimum(c - ref, 0.0))
        kt = (k * jnp.exp2(jnp.minimum(ref - c, 0.0))).astype(BF16)
        same = _div_pow2(ii, blk) == _div_pow2(jj, blk)
        if reverse:
            lvl = same & (_mod_pow2(ii, blk) < hsz) & (_mod_pow2(jj, blk) >= hsz)
        else:
            lvl = same & (_mod_pow2(ii, blk) >= hsz) & (_mod_pow2(jj, blk) < hsz)
        for h in range(HEADS):
            qh = jnp.where(lane_head == h, qt, 0.0).astype(BF16)
            s = lax.dot_general(qh, kt, (((1,), (1,)), ((), ())), preferred_element_type=F32)
            scores[h] = scores[h] + jnp.where(lvl, s, 0.0)
        hsz //= 2

    assert dv == C
    er = _div_pow2(lax.broadcasted_iota(jnp.int32, (wk, wv), 0), dk)
    ec = _div_pow2(lax.broadcasted_iota(jnp.int32, (wk, wv), 1), dv)
    expand = jnp.where(er == ec, 1.0, 0.0).astype(BF16)
    dist = (jj - ii) if reverse else (ii - jj)
    same_sub = _div_pow2(ii, GLA_SUB) == _div_pow2(jj, GLA_SUB)
    for lag in range(1 if reverse else 0, GLA_SUB):
        if lag == 0:
            t = q * k
        else:
            shift = (GLA_SUB - lag) if reverse else lag
            ks = pltpu.roll(k.reshape(C // GLA_SUB, GLA_SUB, wk), shift, axis=1).reshape(C, wk)
            cs = pltpu.roll(c.reshape(C // GLA_SUB, GLA_SUB, wk), shift, axis=1).reshape(C, wk)
            t = q * ks * jnp.exp2(jnp.minimum(c - cs, 0.0))
        red = jnp.dot(t.astype(BF16), expand, preferred_element_type=F32)
        on_diag = same_sub & (dist == lag)
        for h in range(HEADS):
            scores[h] = scores[h] + jnp.where(on_diag, red[:, h * dv:(h + 1) * dv], 0.0)

    vb = v.astype(BF16)
    o = o + jnp.concatenate(
        [jnp.dot(scores[h].astype(BF16), vb[:, h * dv:(h + 1) * dv], preferred_element_type=F32) for h in range(HEADS)],
        axis=1)
    return o, new_st


def _gla_kernel(qf, kf, vf, lf, qb, kb, vb, lb, w2f, bf, w2b, bb, of_ref, ob_ref, s_sc, *, qscale):
    @pl.when(pl.program_id(1) == 0)
    def _():
        s_sc[...] = jnp.zeros(s_sc.shape, F32)

    per = qf.shape[0] // GLA_CHUNK
    st_f, st_b = s_sc[0], s_sc[1]
    for s in range(per):
        rf = pl.ds(s * GLA_CHUNK, GLA_CHUNK)
        o, st_f = _gla_direction(qf[rf, :] * qscale, kf[rf, :], vf[rf, :], lf[rf, :], w2f[...], bf[...], st_f, False)
        of_ref[rf, :] = o.astype(of_ref.dtype)
        rb = pl.ds((per - 1 - s) * GLA_CHUNK, GLA_CHUNK)
        o, st_b = _gla_direction(qb[rb, :] * qscale, kb[rb, :], vb[rb, :], lb[rb, :], w2b[...], bb[...], st_b, True)
        ob_ref[rb, :] = o.astype(ob_ref.dtype)
    s_sc[0] = st_f
    s_sc[1] = st_b


GLA_CHUNK = 128
GLA_CHUNKS_PER_STEP = 4


def _gla(z, *, qcol, kcol, vcol, lcol, w2f, bf, w2b, bb, batch, seq):
    chunk = GLA_CHUNK * GLA_CHUNKS_PER_STEP
    assert seq % chunk == 0
    n = seq // chunk
    wk, wv = HEADS * GLA_K_DIM, HEADS * LANE

    def specs(cmap):
        return [pl.BlockSpec((chunk, wk), lambda b, c: (cmap(b, c), qcol)),
                pl.BlockSpec((chunk, wk), lambda b, c: (cmap(b, c), kcol)),
                pl.BlockSpec((chunk, wv), lambda b, c: (cmap(b, c), vcol)),
                pl.BlockSpec((chunk, LANE), lambda b, c: (cmap(b, c), lcol))]

    fmap = lambda b, c: b * n + c
    bmap = lambda b, c: b * n + n - 1 - c
    wspec = [pl.BlockSpec((LANE, wk), lambda b, c: (0, 0)), pl.BlockSpec((1, wk), lambda b, c: (0, 0))]
    out = jax.ShapeDtypeStruct((batch * seq, wv), BF16)
    return pl.pallas_call(
        functools.partial(_gla_kernel, qscale=GLA_K_DIM ** -0.5),
        grid=(batch, n),
        in_specs=specs(fmap) + specs(bmap) + wspec + wspec,
        out_specs=[pl.BlockSpec((chunk, wv), lambda b, c: (fmap(b, c), 0)),
                   pl.BlockSpec((chunk, wv), lambda b, c: (bmap(b, c), 0))],
        out_shape=[out, out],
        scratch_shapes=[pltpu.VMEM((2, wv, wk), F32)],
        compiler_params=_cparams(("parallel", "arbitrary")),
        name="gla",
    )(z, z, z, z, z, z, z, z, w2f, bf, w2b, bb)


def _layer_norm_rows(r, g, b):
    mu = jnp.mean(r, axis=-1, keepdims=True)
    d = r - mu
    var = jnp.mean(d * d, axis=-1, keepdims=True)
    return d * lax.rsqrt(var + LN_EPS) * g + b


def _outproj_kernel(x_ref, fin_ref, of_ref, ob_ref, gate_ref, ng_ref, wa_ref, wb_ref, lg_ref, lb_ref, rw_ref, rb_ref,
                    o_ref, ids_ref, gw_ref, cnt_ref, run_sc, *, group_norm):
    lin = of_ref[...].astype(F32) + ob_ref[...].astype(F32)
    parts = []
    for h in range(HEADS):
        zh = lin[:, h * LANE:(h + 1) * LANE]
        if group_norm:
            mu = jnp.mean(zh, axis=-1, keepdims=True)
            dz = zh - mu
            parts.append(dz * lax.rsqrt(jnp.mean(dz * dz, axis=-1, keepdims=True) + LN_EPS))
        else:
            parts.append(zh * lax.rsqrt(jnp.mean(zh * zh, axis=-1, keepdims=True) + RMS_EPS) * ng_ref[...])
    gate = gate_ref[...]
    lin = jnp.concatenate(parts, axis=1) * (gate * jax.nn.sigmoid(gate))
    y = (jnp.dot(fin_ref[...].astype(BF16), wa_ref[...], preferred_element_type=F32)
         + jnp.dot(lin.astype(BF16), wb_ref[...], preferred_element_type=F32))
    x1 = _layer_norm_rows(ALPHA * x_ref[...] + y, lg_ref[...], lb_ref[...])
    o_ref[...] = x1
    _route_rows(x1, rw_ref, rb_ref, ids_ref, gw_ref, cnt_ref, run_sc)


def _outproj(x, fin, of, ob, gate_src, gate_col, norm_gain, wa, wb, ln_g, ln_b, route_w, route_b, *, group_norm, tm=512):
    n, d = x.shape
    w = HEADS * LANE
    tm = min(tm, n)
    row = lambda i: (i, 0)
    const = lambda i: (0, 0)
    rw_hi = route_w.astype(BF16)
    rw = jnp.stack([rw_hi, (route_w - rw_hi.astype(F32)).astype(BF16)])
    return pl.pallas_call(
        functools.partial(_outproj_kernel, group_norm=group_norm),
        grid=(n // tm,),
        in_specs=[pl.BlockSpec((tm, d), row), pl.BlockSpec((tm, w), row), pl.BlockSpec((tm, w), row),
                  pl.BlockSpec((tm, w), row), pl.BlockSpec((tm, w), lambda i: (i, gate_col)),
                  pl.BlockSpec((1, LANE), const), pl.BlockSpec((w, d), const), pl.BlockSpec((w, d), const),
                  pl.BlockSpec((1, d), const), pl.BlockSpec((1, d), const),
                  pl.BlockSpec((2, d, LANE), lambda i: (0, 0, 0)), pl.BlockSpec((1, LANE), const)],
        out_specs=[pl.BlockSpec((tm, d), row), pl.BlockSpec((tm, LANE), row), pl.BlockSpec((tm, LANE), row),
                   pl.BlockSpec((1, LANE), const)],
        out_shape=[jax.ShapeDtypeStruct((n, d), F32), jax.ShapeDtypeStruct((n, LANE), jnp.int32),
                   jax.ShapeDtypeStruct((n, LANE), F32), jax.ShapeDtypeStruct((1, LANE), jnp.int32)],
        scratch_shapes=[pltpu.VMEM((1, LANE), F32)],
        compiler_params=_cparams(("arbitrary",)),
        name="outproj",
    )(x, fin, of, ob, gate_src, norm_gain.reshape(1, LANE), wa, wb, ln_g.reshape(1, d), ln_b.reshape(1, d), rw, route_b)


def _route_rows(x, w_ref, b_ref, ids_ref, gw_ref, cnt_ref, run_sc):
    @pl.when(pl.program_id(0) == 0)
    def _():
        run_sc[...] = jnp.zeros(run_sc.shape, F32)

    tm = x.shape[0]
    xh = x.astype(BF16)
    xl = (x - xh.astype(F32)).astype(BF16)
    wh, wl = w_ref[0], w_ref[1]
    logits = (jnp.dot(xh, wh, preferred_element_type=F32) + jnp.dot(xh, wl, preferred_element_type=F32)
              + jnp.dot(xl, wh, preferred_element_type=F32)) + b_ref[...]
    lane = lax.broadcasted_iota(jnp.int32, logits.shape, 1)
    neg = -jnp.inf
    gmask = (lane >= N_EXPERTS) & (lane < N_EXPERTS + N_GROUPS)
    gl = jnp.where(gmask, logits, neg)
    gmax = jnp.max(gl, axis=1, keepdims=True)
    lane_f = lane.astype(F32)
    first = lambda hit: jnp.min(jnp.where(hit, lane_f, float(LANE)), axis=1, keepdims=True).astype(jnp.int32)
    gidx = first(gl == gmax) - N_EXPERTS
    p_grp = 1.0 / jnp.sum(jnp.where(gmask, jnp.exp(gl - gmax), 0.0), axis=1, keepdims=True)
    el = jnp.where(_div_pow2(lane, EXPERTS_PER_GROUP) == gidx, logits, neg)
    l1 = jnp.max(el, axis=1, keepdims=True)
    e1 = first(el == l1)
    el2 = jnp.where(lane == e1, neg, el)
    l2 = jnp.max(el2, axis=1, keepdims=True)
    e2 = first(el2 == l2)
    t = jnp.exp(l2 - l1)
    w1 = p_grp / (1.0 + t)
    w2 = p_grp * t / (1.0 + t)

    onehot = jnp.where(lane == e1, 1.0, jnp.where(lane == e2, 1.0, 0.0))
    ri = lax.broadcasted_iota(jnp.int32, (tm, tm), 0)
    ci = lax.broadcasted_iota(jnp.int32, (tm, tm), 1)
    before = jnp.dot(jnp.where(ri > ci, 1.0, 0.0).astype(BF16), onehot.astype(BF16), preferred_element_type=F32)
    before = before + run_sc[...]
    r1 = jnp.sum(jnp.where(lane == e1, before, 0.0), axis=1, keepdims=True).astype(jnp.int32)
    r2 = jnp.sum(jnp.where(lane == e2, before, 0.0), axis=1, keepdims=True).astype(jnp.int32)
    run_sc[...] = run_sc[...] + jnp.sum(onehot, axis=0, keepdims=True)
    cnt_ref[...] = run_sc[...].astype(jnp.int32)
    ids_ref[...] = jnp.where(lane == 0, e1, jnp.where(lane == 1, e2, jnp.where(lane == 2, r1, jnp.where(lane == 3, r2, 0))))
    gw_ref[...] = jnp.where(lane == 0, w1, jnp.where(lane == 1, w2, 0.0))


MOE_BM = 256
SUBL = 8


def _rows_from_linear(ref, rows):
    return jnp.concatenate([ref[pl.ds(s, rows, stride=SUBL), :] for s in range(SUBL)], axis=1)


def _rows_to_linear(ref, val):
    for s in range(SUBL):
        ref[pl.ds(s, val.shape[0], stride=SUBL), :] = val[:, s * LANE:(s + 1) * LANE]


def _dispatch_kernel(pend_ref, padded_ref, dest_ref, x_ref, xs_hbm, idx_smem, zbuf, lin, sem_i, sem_z, sem):
    i = pl.program_id(0)
    tm = x_ref.shape[0]
    bm = zbuf.shape[0] // SUBL

    @pl.when(i == 0)
    def _():
        zbuf[...] = jnp.zeros(zbuf.shape, F32)

        def tail(e):
            start_row = pl.multiple_of((pend_ref[e] - bm) * SUBL, bm * SUBL)
            return pltpu.make_async_copy(zbuf, xs_hbm.at[pl.ds(start_row, bm * SUBL), :], sem_z)

        def start(e, carry):
            @pl.when(padded_ref[e] > 0)
            def _():
                tail(e).start()
            return carry

        def wait(e, carry):
            @pl.when(padded_ref[e] > 0)
            def _():
                tail(e).wait()
            return carry

        lax.fori_loop(0, N_EXPERTS, start, 0)
        lax.fori_loop(0, N_EXPERTS, wait, 0)

        def unused(b):
            start_row = pl.multiple_of(b * bm * SUBL, bm * SUBL)
            return pltpu.make_async_copy(zbuf, xs_hbm.at[pl.ds(start_row, bm * SUBL), :], sem_z)

        first_unused = pend_ref[N_EXPERTS - 1] // bm
        n_blocks = xs_hbm.shape[0] // (bm * SUBL)
        lax.fori_loop(first_unused, n_blocks, lambda b, c: (unused(b).start(), c)[1], 0)
        lax.fori_loop(first_unused, n_blocks, lambda b, c: (unused(b).wait(), c)[1], 0)

    cp = pltpu.make_async_copy(dest_ref.at[i], idx_smem, sem_i)
    cp.start()
    _rows_to_linear(lin, x_ref[...])
    cp.wait()

    def scatter(r, carry):
        src = lin.at[pl.ds(pl.multiple_of(r * SUBL, SUBL), SUBL), :]
        for k in range(2):
            dst = pl.multiple_of(idx_smem[2 * r + k] * SUBL, SUBL)
            pltpu.make_async_copy(src, xs_hbm.at[pl.ds(dst, SUBL), :], sem).start(priority=k)
        return carry

    lax.fori_loop(0, tm, scatter, 0, unroll=8)
    for k in range(2):
        pltpu.make_async_copy(lin, xs_hbm.at[pl.ds(0, tm * SUBL), :], sem).wait()


def _dispatch(x, dest, pend, padded, cap, *, tm=512):
    n, d = x.shape
    assert d == SUBL * LANE
    tm = min(tm, n)
    nt = n // tm
    grid_spec = pltpu.PrefetchScalarGridSpec(
        num_scalar_prefetch=2,
        grid=(nt,),
        in_specs=[pl.BlockSpec((nt, 2 * tm), lambda i, pe, pa: (0, 0)),
                  pl.BlockSpec((tm, d), lambda i, pe, pa: (i, 0))],
        out_specs=pl.BlockSpec(memory_space=pl.ANY),
        scratch_shapes=[pltpu.SMEM((2 * tm,), jnp.int32), pltpu.VMEM((MOE_BM * SUBL, LANE), F32),
                        pltpu.VMEM((tm * SUBL, LANE), F32),
                        pltpu.SemaphoreType.DMA(()), pltpu.SemaphoreType.DMA(()), pltpu.SemaphoreType.DMA(())],
    )
    return pl.pallas_call(
        _dispatch_kernel,
        grid_spec=grid_spec,
        out_shape=jax.ShapeDtypeStruct((cap * SUBL, LANE), F32),
        compiler_params=_cparams(("arbitrary",)),
        name="dispatch",
    )(pend, padded, dest.reshape(nt, 2 * tm), x)


def _experts_kernel(blk_e_ref, nused_ref, xs_ref, wg_ref, wu_ref, wd_ref, ys_ref, wgb, wub, wdb, hbuf):
    i = pl.program_id(0)
    nb = pl.num_programs(0) - 1
    cur = jnp.minimum(i, nb - 1)
    prv = jnp.maximum(i - 1, 0)

    @pl.when(i == 0)
    def _():
        hbuf[...] = jnp.zeros(hbuf.shape, BF16)

    @pl.when(jnp.logical_or(i == 0, blk_e_ref[cur] != blk_e_ref[jnp.minimum(prv, nb - 1)]))
    def _():
        wgb[...] = wg_ref[...].astype(BF16)
        wub[...] = wu_ref[...].astype(BF16)

    @pl.when(jnp.logical_or(i == 0, blk_e_ref[prv] != blk_e_ref[jnp.maximum(i - 2, 0)]))
    def _():
        wdb[...] = wd_ref[...].astype(BF16)

    slot = lax.rem(i, 2)
    y = jnp.dot(hbuf[1 - slot], wdb[...], preferred_element_type=F32)
    live = jnp.logical_and(i >= 1, i - 1 < nused_ref[0])
    _rows_to_linear(ys_ref, jnp.where(live, y, 0.0))
    xb = _rows_from_linear(xs_ref, xs_ref.shape[0] // SUBL).astype(BF16)
    hg = jnp.dot(xb, wgb[...], preferred_element_type=F32)
    hu = jnp.dot(xb, wub[...], preferred_element_type=F32)
    hbuf[slot] = (hg * jax.nn.sigmoid(hg) * hu).astype(BF16)


def _experts(xs, blk_e, n_used, w_gate, w_up, w_down, layer):
    cap = xs.shape[0] // SUBL
    bm = MOE_BM
    nb = cap // bm
    d, de = w_gate.shape[2], w_gate.shape[3]
    row_in = lambda i, be, nu: (jnp.minimum(i, nu[0] - 1), 0)
    row = lambda i, be, nu: (jnp.maximum(i - 1, 0), 0)
    w_cur = lambda i, be, nu: (layer, be[jnp.minimum(i, nb - 1)], 0, 0)
    w_prv = lambda i, be, nu: (layer, be[jnp.maximum(i - 1, 0)], 0, 0)
    grid_spec = pltpu.PrefetchScalarGridSpec(
        num_scalar_prefetch=2,
        grid=(nb + 1,),
        in_specs=[pl.BlockSpec((bm * SUBL, LANE), row_in),
                  pl.BlockSpec((None, None, d, de), w_cur),
                  pl.BlockSpec((None, None, d, de), w_cur),
                  pl.BlockSpec((None, None, de, d), w_prv)],
        out_specs=pl.BlockSpec((bm * SUBL, LANE), row),
        scratch_shapes=[pltpu.VMEM((d, de), BF16), pltpu.VMEM((d, de), BF16), pltpu.VMEM((de, d), BF16),
                        pltpu.VMEM((2, bm, de), BF16)],
    )
    return pl.pallas_call(
        _experts_kernel,
        grid_spec=grid_spec,
        out_shape=jax.ShapeDtypeStruct((cap * SUBL, LANE), F32),
        compiler_params=_cparams(("arbitrary",)),
        name="experts",
    )(blk_e, n_used, xs, w_gate, w_up, w_down)


def _combine_kernel(dest_ref, x_ref, gw_ref, g_ref, b_ref, ys_hbm, o_ref, idx_smem, ybuf, sem_i, sem):
    i = pl.program_id(0)
    tm = x_ref.shape[0]

    def issue(tile, slot):
        cp = pltpu.make_async_copy(dest_ref.at[tile], idx_smem, sem_i)
        cp.start()
        cp.wait()

        def gather(r, carry):
            row = pl.multiple_of(r * SUBL, SUBL)
            for k in range(2):
                src = pl.multiple_of(idx_smem[2 * r + k] * SUBL, SUBL)
                pltpu.make_async_copy(ys_hbm.at[pl.ds(src, SUBL), :], ybuf.at[slot, k, pl.ds(row, SUBL), :],
                                      sem.at[slot]).start(priority=k)
            return carry

        lax.fori_loop(0, tm, gather, 0, unroll=8)

    slot = lax.rem(i, 2)

    @pl.when(i == 0)
    def _():
        issue(0, 0)

    @pl.when(i + 1 < pl.num_programs(0))
    def _():
        issue(i + 1, 1 - slot)

    for k in range(2):
        pltpu.make_async_copy(ys_hbm.at[pl.ds(0, tm * SUBL), :], ybuf.at[slot, k], sem.at[slot]).wait()
    gw = gw_ref[...]
    ffn = (_rows_from_linear(ybuf.at[slot, 0], tm) * gw[:, 0:1] + _rows_from_linear(ybuf.at[slot, 1], tm) * gw[:, 1:2])
    o_ref[...] = _layer_norm_rows(ALPHA * x_ref[...] + ffn, g_ref[...], b_ref[...])


def _combine(x, ys, dest, gw, ln_g, ln_b, *, tm=512):
    n, d = x.shape
    tm = min(tm, n)
    nt = n // tm
    return pl.pallas_call(
        _combine_kernel,
        grid=(nt,),
        in_specs=[pl.BlockSpec((nt, 2 * tm), lambda i: (0, 0)),
                  pl.BlockSpec((tm, d), lambda i: (i, 0)), pl.BlockSpec((tm, LANE), lambda i: (i, 0)),
                  pl.BlockSpec((1, d), lambda i: (0, 0)), pl.BlockSpec((1, d), lambda i: (0, 0)),
                  pl.BlockSpec(memory_space=pl.ANY)],
        out_specs=pl.BlockSpec((tm, d), lambda i: (i, 0)),
        out_shape=jax.ShapeDtypeStruct((n, d), F32),
        scratch_shapes=[pltpu.SMEM((2 * tm,), jnp.int32), pltpu.VMEM((2, 2, tm * SUBL, LANE), F32),
                        pltpu.SemaphoreType.DMA(()), pltpu.SemaphoreType.DMA((2,))],
        compiler_params=_cparams(("arbitrary",)),
        name="combine",
    )(dest.reshape(nt, 2 * tm), x, gw, ln_g.reshape(1, d), ln_b.reshape(1, d), ys)


def _router_params(w_grp, b_grp, w_exp, b_exp):
    d = w_exp.shape[0]
    wr = jnp.zeros((d, LANE), F32).at[:, :N_EXPERTS].set(w_exp).at[:, N_EXPERTS:N_EXPERTS + N_GROUPS].set(w_grp)
    br = jnp.zeros((1, LANE), F32).at[0, :N_EXPERTS].set(b_exp).at[0, N_EXPERTS:N_EXPERTS + N_GROUPS].set(b_grp)
    return wr, br


def _moe(x, routing, w_gate, w_up, w_down, layer, ln_g, ln_b):
    n, d = x.shape
    ids, gw, cnt = routing
    bm = MOE_BM
    counts = cnt[0, :N_EXPERTS]
    padded = (counts + bm - 1) // bm * bm
    pend = jnp.cumsum(padded)
    pstart = pend - padded
    e, r = ids[:, 0:2], ids[:, 2:4]
    onehot = e[:, :, None] == jnp.arange(N_EXPERTS, dtype=jnp.int32)[None, None, :]
    dest = jnp.sum(jnp.where(onehot, pstart[None, None, :], 0), axis=-1) + r
    cap = 2 * n + N_EXPERTS * bm
    nb = cap // bm
    blk_start = jnp.arange(nb, dtype=jnp.int32) * bm
    blk_e = jnp.minimum(jnp.sum((pend[None, :] <= blk_start[:, None]).astype(jnp.int32), axis=1), N_EXPERTS - 1)
    n_used = (pend[-1:] // bm).astype(jnp.int32)
    xs = _dispatch(x, dest, pend.astype(jnp.int32), padded.astype(jnp.int32), cap)
    ys = _experts(xs, blk_e, n_used, w_gate, w_up, w_down, layer)
    return _combine(x, ys, dest, gw, ln_g, ln_b)


def _even_layer(x, batch, seq, layer_idx, w_in, dec_f, dec_b, lq1, lk1, lq2, lk2, subln, w_out, ln_g, ln_b, route):
    d = x.shape[1]
    w = HEADS * LANE
    kw = dict(batch=batch, seq=seq)
    diff_seg = [(0, DIFF_ROT_DIM), (DIFF_HEAD_DIM, DIFF_ROT_DIM)]
    q, k, v, gate, dq, dk, dvt = _even_in(
        x, w_in.astype(BF16), _rope_tables(seq, [(0, LANE)], RET_THETA),
        _rope_tables(seq, diff_seg, ROPE_THETA), **kw)
    decays = jnp.stack([dec_f, dec_b]).astype(F32)
    of, ob = _retention(q, k, v, decays, **kw)
    lam_init = 0.8 - 0.6 * math.exp(-0.3 * layer_idx)
    diff = _flash(dq, dk, dvt, diff=(lq1, lk1, lq2, lk2, subln), lam_init=lam_init, tq=2048, **kw)
    wo = w_out.astype(BF16)
    return _outproj(x, diff, of, ob, gate, 0, jnp.ones((LANE,), F32), wo[w:], wo[:w], ln_g, ln_b, *route,
                    group_norm=True)


def _odd_layer(x, batch, seq, w_in, q_norm, w_uq, kv_norm, w_ukv, w2_f, b_f, w2_b, b_b, gla_norm, w_out, ln_g, ln_b,
               route):
    d = x.shape[1]
    w = HEADS * LANE
    o = np.cumsum([0, MLA_Q_RANK, MLA_KV_RANK, MLA_ROPE, HEADS * GLA_K_DIM, HEADS * GLA_K_DIM, w, w,
                   GLA_GATE_RANK, GLA_GATE_RANK]).tolist()
    zeros = lambda c: jnp.zeros((d, c), F32)
    w_in2 = jnp.concatenate([
        w_in[:, o[0]:o[2]], zeros(MLA_NOPE), w_in[:, o[2]:o[3]], zeros(LANE - MLA_NOPE - MLA_ROPE),
        w_in[:, o[3]:o[7]], w_in[:, o[7]:o[9]], zeros(LANE - 2 * GLA_GATE_RANK)], axis=1).astype(BF16)
    kw = dict(batch=batch, seq=seq)
    qd = MLA_NOPE + MLA_ROPE
    w_uq2 = jnp.pad(w_uq.reshape(MLA_Q_RANK, HEADS, qd), ((0, 0), (0, 0), (0, LANE - qd))).reshape(MLA_Q_RANK, w)
    ukv = w_ukv.reshape(MLA_KV_RANK, HEADS, MLA_NOPE + MLA_V)
    w_uk2 = jnp.pad(ukv[:, :, :MLA_NOPE], ((0, 0), (0, 0), (0, LANE - MLA_NOPE))).reshape(MLA_KV_RANK, w)
    w_uv2 = ukv[:, :, MLA_NOPE:].reshape(MLA_KV_RANK, w)
    q, k, vt, zg = _odd_in(x, w_in2, q_norm, w_uq2.astype(BF16), kv_norm,
                           jnp.concatenate([w_uk2, w_uv2], axis=1).astype(BF16),
                           _rope_tables(seq, [(MLA_NOPE, MLA_ROPE)], ROPE_THETA), **kw)
    mla = _flash(q, k, vt, **kw)
    wk = HEADS * GLA_K_DIM
    pad_rows = lambda m, r0: jnp.zeros((LANE, wk), F32).at[r0:r0 + GLA_GATE_RANK].set(m).astype(BF16)
    of, ob = _gla(zg, qcol=0, kcol=1, vcol=1, lcol=12,
                  w2f=pad_rows(w2_f, 0), bf=b_f.reshape(1, wk), w2b=pad_rows(w2_b, GLA_GATE_RANK), bb=b_b.reshape(1, wk), **kw)
    wo = w_out.astype(BF16)
    return _outproj(x, mla, of, ob, zg, 2, gla_norm, wo[:w], wo[w:], ln_g, ln_b, *route, group_norm=False)


def kernel(x, ev_w_in, ev_ret_decay_f, ev_ret_decay_b, ev_lq1, ev_lk1, ev_lq2, ev_lk2, ev_subln, ev_w_out, od_w_in, od_q_norm, od_w_uq, od_kv_norm, od_w_ukv, od_gla_w2_f, od_gla_b_f, od_gla_w2_b, od_gla_b_b, od_gla_norm, od_w_out, ln1_g, ln1_b, ln2_g, ln2_b, moe_w_grp, moe_b_grp, moe_w_exp, moe_b_exp, moe_w_gate, moe_w_up, moe_w_down):
    batch, seq, d = x.shape
    h = x.reshape(batch * seq, d)
    for i in range(DEPTH):
        j = i // 2
        route = _router_params(moe_w_grp[i], moe_b_grp[i], moe_w_exp[i], moe_b_exp[i])
        if i % 2 == 0:
            h, *routing = _even_layer(h, batch, seq, i, ev_w_in[j], ev_ret_decay_f[j], ev_ret_decay_b[j], ev_lq1[j],
                                      ev_lk1[j], ev_lq2[j], ev_lk2[j], ev_subln[j], ev_w_out[j], ln1_g[i], ln1_b[i], route)
        else:
            h, *routing = _odd_layer(h, batch, seq, od_w_in[j], od_q_norm[j], od_w_uq[j], od_kv_norm[j], od_w_ukv[j],
                                     od_gla_w2_f[j], od_gla_b_f[j], od_gla_w2_b[j], od_gla_b_b[j], od_gla_norm[j],
                                     od_w_out[j], ln1_g[i], ln1_b[i], route)
        h = _moe(h, routing, moe_w_gate, moe_w_up, moe_w_down, i, ln2_g[i], ln2_b[i])
    return h.reshape(batch, seq, d)
```

```python
import functools
import math

import numpy as np
import jax
import jax.numpy as jnp
from jax import lax
from jax.experimental import pallas as pl
from jax.experimental.pallas import tpu as pltpu

F32 = jnp.float32
BF16 = jnp.bfloat16

HEADS = 4
LANE = 128
RET_THETA = 10000.0
ROPE_THETA = 500000.0
DIFF_HEAD_DIM = 64
DIFF_ROT_DIM = 16
MLA_Q_RANK = 256
MLA_KV_RANK = 128
MLA_NOPE = 64
MLA_ROPE = 32
MLA_V = 128
GLA_K_DIM = 64
GLA_GATE_RANK = 16
GLA_TAU = 16.0
N_GROUPS = 4
EXPERTS_PER_GROUP = 8
N_EXPERTS = N_GROUPS * EXPERTS_PER_GROUP
DEPTH = 2
ALPHA = (2.0 * DEPTH) ** 0.25
LN_EPS = 1e-5
RMS_EPS = 1e-6

VMEM_LIMIT = 48 * 1024 * 1024


def _div_pow2(x, n):
    return lax.shift_right_logical(x, int(n).bit_length() - 1)


def _mod_pow2(x, n):
    return lax.bitwise_and(x, int(n) - 1)


def _cparams(sem):
    return pltpu.CompilerParams(dimension_semantics=sem, vmem_limit_bytes=VMEM_LIMIT)


def _rope_heads(z, tabs, sh, scale):
    c, sa, sb = tabs
    outs = []
    for h in range(HEADS):
        zh = z[:, h * LANE:(h + 1) * LANE]
        outs.append((zh * c + pltpu.roll(zh, sh, axis=1) * sa + pltpu.roll(zh, LANE - sh, axis=1) * sb) * scale)
    return outs


def _even_in_kernel(x_ref, w_ref, rc, rsa, rsb, dc, dsa, dsb, q_ref, k_ref, v_ref, g_ref, dq_ref, dk_ref, dvt_ref):
    w = HEADS * LANE
    xb = x_ref[...].astype(BF16)
    part = lambda t: jnp.dot(xb, w_ref[:, t * w:(t + 1) * w], preferred_element_type=F32)
    ret_t = (rc[...], rsa[...], rsb[...])
    diff_t = (dc[...], dsa[...], dsb[...])
    for h, o in enumerate(_rope_heads(part(0), ret_t, LANE // 2, 1.0)):
        q_ref[h] = o.astype(BF16)
    for h, o in enumerate(_rope_heads(part(1), ret_t, LANE // 2, LANE ** -0.5)):
        k_ref[h] = o.astype(BF16)
    rv = part(2)
    for h in range(HEADS):
        v_ref[h] = rv[:, h * LANE:(h + 1) * LANE].astype(BF16)
    g_ref[...] = part(3)
    for h, o in enumerate(_rope_heads(part(4), diff_t, DIFF_ROT_DIM // 2, DIFF_HEAD_DIM ** -0.5 * LOG2E)):
        dq_ref[h] = o.T.astype(BF16)
    for h, o in enumerate(_rope_heads(part(5), diff_t, DIFF_ROT_DIM // 2, 1.0)):
        dk_ref[h] = o.astype(BF16)
    dv = part(6)
    for h in range(HEADS):
        dvt_ref[h, :LANE, :] = dv[:, h * LANE:(h + 1) * LANE].T.astype(BF16)
        dvt_ref[h, LANE:, :] = jnp.ones((ONES_ROWS, dvt_ref.shape[2]), BF16)


def _even_in(x, w, ret_tabs, diff_tabs, *, batch, seq):
    n, d = x.shape
    tm = min(FLASH_TK_DIFF, seq // 2)
    nt = seq // tm
    hw = HEADS * LANE
    heads = jax.ShapeDtypeStruct((batch, HEADS, seq, LANE), BF16)
    head_spec = pl.BlockSpec((None, HEADS, tm, LANE), lambda i: (i // nt, 0, i % nt, 0))
    heads_t = jax.ShapeDtypeStruct((batch, HEADS, LANE, seq), BF16)
    head_t_spec = pl.BlockSpec((None, HEADS, LANE, tm), lambda i: (i // nt, 0, 0, i % nt))
    tab_spec = pl.BlockSpec((tm, LANE), lambda i: (i % nt, 0))
    return pl.pallas_call(
        _even_in_kernel,
        grid=(n // tm,),
        in_specs=[pl.BlockSpec((tm, d), lambda i: (i, 0)), pl.BlockSpec((d, 7 * hw), lambda i: (0, 0))] + [tab_spec] * 6,
        out_specs=[head_spec, head_spec, head_spec, pl.BlockSpec((tm, hw), lambda i: (i, 0)), head_t_spec, head_spec,
                   pl.BlockSpec((None, HEADS, None, LANE + ONES_ROWS, tm), lambda i: (i // nt, 0, i % nt, 0, 0))],
        out_shape=[heads, heads, heads, jax.ShapeDtypeStruct((n, hw), F32), heads_t, heads,
                   jax.ShapeDtypeStruct((batch, HEADS, nt, LANE + ONES_ROWS, tm), BF16)],
        compiler_params=_cparams(("parallel",)),
        name="even_in",
    )(x, w, *ret_tabs, *diff_tabs)


def _rms_rows(z, g):
    return z * lax.rsqrt(jnp.mean(z * z, axis=-1, keepdims=True) + RMS_EPS) * g


def _odd_in_kernel(x_ref, w_ref, qn_ref, wq_ref, kvn_ref, wkv_ref, tc, tsa, tsb, q_ref, k_ref, vt_ref, zg_ref):
    hw = HEADS * LANE
    mla_w = MLA_Q_RANK + MLA_KV_RANK + LANE
    xb = x_ref[...].astype(BF16)
    zg_ref[...] = jnp.dot(xb, w_ref[:, mla_w:], preferred_element_type=F32)
    z1 = jnp.dot(xb, w_ref[:, :mla_w], preferred_element_type=F32)
    tabs = (tc[...], tsa[...], tsb[...])
    sh = MLA_ROPE // 2
    qh = jnp.dot(_rms_rows(z1[:, :MLA_Q_RANK], qn_ref[...]).astype(BF16), wq_ref[...], preferred_element_type=F32)
    for h, o in enumerate(_rope_heads(qh, tabs, sh, (MLA_NOPE + MLA_ROPE) ** -0.5 * LOG2E)):
        q_ref[h] = o.T.astype(BF16)
    ckv = _rms_rows(z1[:, MLA_Q_RANK:MLA_Q_RANK + MLA_KV_RANK], kvn_ref[...]).astype(BF16)
    kv = jnp.dot(ckv, wkv_ref[...], preferred_element_type=F32)
    kr = z1[:, MLA_Q_RANK + MLA_KV_RANK:]
    kr = kr * tabs[0] + pltpu.roll(kr, sh, axis=1) * tabs[1] + pltpu.roll(kr, LANE - sh, axis=1) * tabs[2]
    for h in range(HEADS):
        k_ref[h] = (kv[:, h * LANE:(h + 1) * LANE] + kr).astype(BF16)
        vt_ref[h, :LANE, :] = kv[:, hw + h * LANE:hw + (h + 1) * LANE].T.astype(BF16)
        vt_ref[h, LANE:, :] = jnp.ones((ONES_ROWS, vt_ref.shape[2]), BF16)


def _odd_in(x, w, q_norm, w_uq, kv_norm, w_ukv, tabs, *, batch, seq, tm=512):
    n, d = x.shape
    tm = min(tm, seq // 2)
    nt = seq // tm
    tk = min(FLASH_TK, seq // 2)
    per = tk // tm
    hw = HEADS * LANE
    gw_ = w.shape[1] - (MLA_Q_RANK + MLA_KV_RANK + LANE)
    heads = jax.ShapeDtypeStruct((batch, HEADS, seq, LANE), BF16)
    head_spec = pl.BlockSpec((None, HEADS, tm, LANE), lambda i: (i // nt, 0, i % nt, 0))
    tab_spec = pl.BlockSpec((tm, LANE), lambda i: (i % nt, 0))
    const = lambda i: (0, 0)
    return pl.pallas_call(
        _odd_in_kernel,
        grid=(n // tm,),
        in_specs=[pl.BlockSpec((tm, d), lambda i: (i, 0)), pl.BlockSpec(w.shape, const),
                  pl.BlockSpec((1, MLA_Q_RANK), const), pl.BlockSpec(w_uq.shape, const),
                  pl.BlockSpec((1, MLA_KV_RANK), const), pl.BlockSpec(w_ukv.shape, const)] + [tab_spec] * 3,
        out_specs=[pl.BlockSpec((None, HEADS, LANE, tm), lambda i: (i // nt, 0, 0, i % nt)), head_spec,
                   pl.BlockSpec((None, HEADS, None, LANE + ONES_ROWS, tm),
                                lambda i: (i // nt, 0, (i % nt) // per, 0, (i % nt) % per)),
                   pl.BlockSpec((tm, gw_), lambda i: (i, 0))],
        out_shape=[jax.ShapeDtypeStruct((batch, HEADS, LANE, seq), BF16), heads, jax.ShapeDtypeStruct((batch, HEADS, seq // tk, LANE + ONES_ROWS, tk), BF16),
                   jax.ShapeDtypeStruct((n, gw_), F32)],
        compiler_params=_cparams(("parallel",)),
        name="odd_in",
    )(x, w, q_norm.reshape(1, -1), w_uq, kv_norm.reshape(1, -1), w_ukv, *tabs)


def _rope_tables(seq, segs, theta):
    pos = jnp.arange(seq, dtype=F32)
    inv = jnp.zeros((LANE,), F32)
    lo = np.zeros((LANE,), bool)
    hi = np.zeros((LANE,), bool)
    for start, rot in segs:
        half = rot // 2
        f = jnp.power(jnp.float32(theta), -jnp.arange(0, rot, 2, dtype=F32) / rot)
        inv = inv.at[start:start + half].set(f).at[start + half:start + rot].set(f)
        lo[start:start + half] = True
        hi[start + half:start + rot] = True
    ang = pos[:, None] * inv[None, :]
    cos, sin = jnp.cos(ang), jnp.sin(ang)
    c = jnp.where(lo | hi, cos, 1.0)
    sa = jnp.where(hi, sin, 0.0)
    sb = jnp.where(lo, -sin, 0.0)
    return c, sa, sb


ONES_ROWS = 16
LOG2E = math.log2(math.e)
FLASH_TK = 1024
FLASH_TK_DIFF = 512


def _flash_kernel(*refs, ncomp, nk, lam_init):
    if ncomp == 2:
        q_ref, k_ref, vt_ref, lq1, lk1, lq2, lk2, g_ref, o_ref, *scr = refs
    else:
        q_ref, k_ref, vt_ref, o_ref, *scr = refs
    qm_sc, m_sc, acc_sc, s0, s1, cm0, cm1, p0, p1, al0, al1 = scr
    tk = s0.shape[1]
    q = q_ref[...]
    if ncomp == 2:
        chan = lax.broadcasted_iota(jnp.int32, q.shape, 0)
        zero = jnp.zeros_like(q)
        qm_sc[0] = jnp.where(chan < DIFF_HEAD_DIM, q, zero)
        qm_sc[1] = jnp.where(chan >= DIFF_HEAD_DIM, q, zero)
    else:
        qm_sc[0] = q
    m_sc[...] = jnp.full(m_sc.shape, -jnp.inf, F32)
    acc_sc[...] = jnp.zeros(acc_sc.shape, F32)

    def scores(j, s_ref, cm_ref):
        k = k_ref[j * tk:(j + 1) * tk, :]
        for c in range(ncomp):
            s = jnp.dot(k, qm_sc[c], preferred_element_type=F32)
            s_ref[c] = s
            cm_ref[c] = jnp.max(s, axis=0, keepdims=True)

    def softmax(s_ref, cm_ref, p_ref, al_ref):
        for c in range(ncomp):
            m_old = m_sc[c]
            m_new = jnp.maximum(m_old, cm_ref[c])
            al_ref[c] = jnp.exp2(m_old - m_new)
            p_ref[c] = jnp.exp2(s_ref[c] - m_new).astype(BF16)
            m_sc[c] = m_new

    def values(j, p_ref, al_ref):
        vt = vt_ref[j]
        for c in range(ncomp):
            acc_sc[c] = al_ref[c] * acc_sc[c] + jnp.dot(vt, p_ref[c], preferred_element_type=F32)

    bufs = ((s0, cm0, p0, al0), (s1, cm1, p1, al1))
    scores(0, s0, cm0)
    for j in range(nk):
        s_c, cm_c, p_c, al_c = bufs[j % 2]
        s_n, cm_n, p_n, al_n = bufs[(j + 1) % 2]
        if j + 1 < nk:
            scores(j + 1, s_n, cm_n)
        softmax(s_c, cm_c, p_c, al_c)
        if j >= 1:
            values(j - 1, p_n, al_n)
    values(nk - 1, *bufs[(nk - 1) % 2][2:])

    def normalised(c):
        acc = acc_sc[c]
        return acc[:LANE] / acc[LANE:LANE + 1]

    o = normalised(0)
    if ncomp == 2:
        lam = (jnp.exp(jnp.sum(lq1[...] * lk1[...], keepdims=True))
               - jnp.exp(jnp.sum(lq2[...] * lk2[...], keepdims=True)) + lam_init)
        o = o - lam * normalised(1)
        o = o * lax.rsqrt(jnp.mean(o * o, axis=0, keepdims=True) + RMS_EPS) * g_ref[...] * (1.0 - lam_init)
    o_ref[...] = o.T.astype(o_ref.dtype)


def _flash(q, k, vt, *, batch, seq, tq=512, diff=None, lam_init=0.0):
    nk, vrows, tk = vt.shape[2], vt.shape[3], vt.shape[4]
    assert vrows == LANE + ONES_ROWS
    tq = min(tq, seq)
    nq = seq // tq
    ncomp = 2 if diff is not None else 1
    in_specs = [
        pl.BlockSpec((None, None, LANE, tq), lambda b, h, i: (b, h, 0, i)),
        pl.BlockSpec((None, None, seq, LANE), lambda b, h, i: (b, h, 0, 0)),
        pl.BlockSpec((None, None, nk, vrows, tk), lambda b, h, i: (b, h, 0, 0, 0)),
    ]
    args = [q, k, vt]
    if diff is not None:
        lq1, lk1, lq2, lk2, subln = diff
        for v in (lq1, lk1, lq2, lk2):
            in_specs.append(pl.BlockSpec((1, DIFF_HEAD_DIM), lambda b, h, i: (0, 0)))
            args.append(v.reshape(1, DIFF_HEAD_DIM))
        in_specs.append(pl.BlockSpec((LANE, 1), lambda b, h, i: (0, 0)))
        args.append(subln.reshape(LANE, 1))
    return pl.pallas_call(
        functools.partial(_flash_kernel, ncomp=ncomp, nk=nk, lam_init=lam_init),
        grid=(batch, HEADS, nq),
        in_specs=in_specs,
        out_specs=pl.BlockSpec((tq, LANE), lambda b, h, i: (b * nq + i, h)),
        out_shape=jax.ShapeDtypeStruct((batch * seq, HEADS * LANE), BF16),
        scratch_shapes=[pltpu.VMEM((ncomp, LANE, tq), BF16),
                        pltpu.VMEM((ncomp, 1, tq), F32), pltpu.VMEM((ncomp, vrows, tq), F32),
                        pltpu.VMEM((ncomp, tk, tq), F32), pltpu.VMEM((ncomp, tk, tq), F32),
                        pltpu.VMEM((ncomp, 1, tq), F32), pltpu.VMEM((ncomp, 1, tq), F32),
                        pltpu.VMEM((ncomp, tk, tq), BF16), pltpu.VMEM((ncomp, tk, tq), BF16),
                        pltpu.VMEM((ncomp, 1, tq), F32), pltpu.VMEM((ncomp, 1, tq), F32)],
        compiler_params=_cparams(("parallel", "parallel", "parallel")),
        name="flash_diff" if diff is not None else "flash_mla",
    )(*args)


def _ret_kernel(dec_ref, qf, kf, vf, qb, kb, vb, of_ref, ob_ref, s_sc, *, chunk):
    @pl.when(pl.program_id(1) == 0)
    def _():
        s_sc[...] = jnp.zeros(s_sc.shape, F32)

    ii = lax.broadcasted_iota(jnp.int32, (chunk, chunk), 0)
    jj = lax.broadcasted_iota(jnp.int32, (chunk, chunk), 1)
    r = lax.broadcasted_iota(jnp.int32, (chunk, 1), 0).astype(F32)
    for d, (q_ref, k_ref, v_ref, o_ref) in enumerate(((qf, kf, vf, of_ref), (qb, kb, vb, ob_ref))):
        for h in range(HEADS):
            la = -jnp.exp(jnp.full((1, 1), dec_ref[d, h], F32))
            if d == 0:
                mask, dist = ii >= jj, (ii - jj).astype(F32)
                qdec, kdec = jnp.exp(la * (r + 1.0)), jnp.exp(la * (chunk - 1.0 - r))
            else:
                mask, dist = jj > ii, (jj - ii).astype(F32)
                qdec, kdec = jnp.exp(la * (chunk - r)), jnp.exp(la * r)
            decay = jnp.where(mask, jnp.exp(jnp.where(mask, dist * la, 0.0)), 0.0)
            state = s_sc[d, h]
            per = q_ref.shape[1] // chunk
            for step in range(per):
                rows = pl.ds((step if d == 0 else per - 1 - step) * chunk, chunk)
                q, k, v = q_ref[h, rows, :], k_ref[h, rows, :], v_ref[h, rows, :]
                s = lax.dot_general(q, k, (((1,), (1,)), ((), ())), preferred_element_type=F32)
                o = jnp.dot((s * decay).astype(BF16), v, preferred_element_type=F32)
                o = o + qdec * jnp.dot(q, state.astype(BF16), preferred_element_type=F32)
                kd = (k.astype(F32) * kdec).astype(BF16)
                state = jnp.exp(la * float(chunk)) * state + lax.dot_general(
                    kd, v, (((0,), (0,)), ((), ())), preferred_element_type=F32)
                o_ref[rows, h * LANE:(h + 1) * LANE] = o.astype(o_ref.dtype)
            s_sc[d, h] = state


RET_CHUNKS_PER_STEP = 4


def _retention(q, k, v, decays, *, batch, seq, chunk=256):
    chunk = min(chunk, seq // RET_CHUNKS_PER_STEP)
    blk = chunk * RET_CHUNKS_PER_STEP
    n = seq // blk
    fwd = pl.BlockSpec((None, HEADS, blk, LANE), lambda b, c: (b, 0, c, 0))
    bwd = pl.BlockSpec((None, HEADS, blk, LANE), lambda b, c: (b, 0, n - 1 - c, 0))
    w = HEADS * LANE
    out = jax.ShapeDtypeStruct((batch * seq, w), BF16)
    return pl.pallas_call(
        functools.partial(_ret_kernel, chunk=chunk),
        grid=(batch, n),
        in_specs=[pl.BlockSpec(memory_space=pltpu.SMEM), fwd, fwd, fwd, bwd, bwd, bwd],
        out_specs=[pl.BlockSpec((blk, w), lambda b, c: (b * n + c, 0)),
                   pl.BlockSpec((blk, w), lambda b, c: (b * n + n - 1 - c, 0))],
        out_shape=[out, out],
        scratch_shapes=[pltpu.VMEM((2, HEADS, LANE, LANE), F32)],
        compiler_params=_cparams(("parallel", "arbitrary")),
        name="retention",
    )(decays, q, k, v, q, k, v)


GLA_SUB = 8


def _split3(x):
    x1 = x.astype(BF16)
    r1 = x - x1.astype(F32)
    x2 = r1.astype(BF16)
    x3 = (r1 - x2.astype(F32)).astype(BF16)
    return x1, x2, x3


def _gla_direction(q, k, v, lr, w2, bias, st, reverse):
    C, wk = q.shape
    wv = v.shape[1]
    dk, dv = wk // HEADS, wv // HEADS
    z = jnp.dot(lr.astype(BF16), w2, preferred_element_type=F32) + bias
    g = (jnp.minimum(z, 0.0) - jnp.log(1.0 + jnp.exp(-jnp.abs(z)))) * (LOG2E / GLA_TAU)
    ii = lax.broadcasted_iota(jnp.int32, (C, C), 0)
    jj = lax.broadcasted_iota(jnp.int32, (C, C), 1)
    tri = jnp.where(ii >= jj, 1.0, 0.0).astype(BF16)
    b = sum(jnp.dot(tri, part, preferred_element_type=F32) for part in _split3(g))
    tot = b[C - 1:C, :]
    c = (tot - b + g) if reverse else b

    qe = (q * jnp.exp2(jnp.minimum(c, 0.0))).astype(BF16)
    o = lax.dot_general(qe, st.astype(BF16), (((1,), (1,)), ((), ())), preferred_element_type=F32)
    ke = (k * jnp.exp2(jnp.minimum(tot - c, 0.0))).astype(BF16)
    upd = lax.dot_general(v.astype(BF16), ke, (((0,), (0,)), ((), ())), preferred_element_type=F32)
    rr = _div_pow2(lax.broadcasted_iota(jnp.int32, (wv, wk), 0), dv)
    cc = _div_pow2(lax.broadcasted_iota(jnp.int32, (wv, wk), 1), dk)
    new_st = jnp.where(rr == cc, st * jnp.exp2(tot) + upd, 0.0)

    lane_head = _div_pow2(lax.broadcasted_iota(jnp.int32, (C, wk), 1), dk)
    scores = [jnp.zeros((C, C), F32) for _ in range(HEADS)]
    hsz = C // 2
    while hsz >= GLA_SUB:
        blk = 2 * hsz
        rows = []
        for m in range(C // blk):
            rrow = m * blk + (hsz if reverse else hsz - 1)
            rows.append(jnp.broadcast_to(c[rrow:rrow + 1, :], (blk, wk)))
        ref = jnp.concatenate(rows, axis=0) if len(rows) > 1 else rows[0]
        qt = q * jnp.exp2(jnp.minimum(c - ref, 0.0))
        kt = (k * jnp.exp2(jnp.minimum(ref - c, 0.0))).astype(BF16)
        same = _div_pow2(ii, blk) == _div_pow2(jj, blk)
        if reverse:
            lvl = same & (_mod_pow2(ii, blk) < hsz) & (_mod_pow2(jj, blk) >= hsz)
        else:
            lvl = same & (_mod_pow2(ii, blk) >= hsz) & (_mod_pow2(jj, blk) < hsz)
        for h in range(HEADS):
            qh = jnp.where(lane_head == h, qt, 0.0).astype(BF16)
            s = lax.dot_general(qh, kt, (((1,), (1,)), ((), ())), preferred_element_type=F32)
            scores[h] = scores[h] + jnp.where(lvl, s, 0.0)
        hsz //= 2

    assert dv == C
    er = _div_pow2(lax.broadcasted_iota(jnp.int32, (wk, wv), 0), dk)
    ec = _div_pow2(lax.broadcasted_iota(jnp.int32, (wk, wv), 1), dv)
    expand = jnp.where(er == ec, 1.0, 0.0).astype(BF16)
    dist = (jj - ii) if reverse else (ii - jj)
    same_sub = _div_pow2(ii, GLA_SUB) == _div_pow2(jj, GLA_SUB)
    for lag in range(1 if reverse else 0, GLA_SUB):
        if lag == 0:
            t = q * k
        else:
            shift = (GLA_SUB - lag) if reverse else lag
            ks = pltpu.roll(k.reshape(C // GLA_SUB, GLA_SUB, wk), shift, axis=1).reshape(C, wk)
            cs = pltpu.roll(c.reshape(C // GLA_SUB, GLA_SUB, wk), shift, axis=1).reshape(C, wk)
            t = q * ks * jnp.exp2(jnp.minimum(c - cs, 0.0))
        red = jnp.dot(t.astype(BF16), expand, preferred_element_type=F32)
        on_diag = same_sub & (dist == lag)
        for h in range(HEADS):
            scores[h] = scores[h] + jnp.where(on_diag, red[:, h * dv:(h + 1) * dv], 0.0)

    vb = v.astype(BF16)
    o = o + jnp.concatenate(
        [jnp.dot(scores[h].astype(BF16), vb[:, h * dv:(h + 1) * dv], preferred_element_type=F32) for h in range(HEADS)],
        axis=1)
    return o, new_st


def _gla_kernel(qf, kf, vf, lf, qb, kb, vb, lb, w2f, bf, w2b, bb, of_ref, ob_ref, s_sc, *, qscale):
    @pl.when(pl.program_id(1) == 0)
    def _():
        s_sc[...] = jnp.zeros(s_sc.shape, F32)

    per = qf.shape[0] // GLA_CHUNK
    st_f, st_b = s_sc[0], s_sc[1]
    for s in range(per):
        rf = pl.ds(s * GLA_CHUNK, GLA_CHUNK)
        o, st_f = _gla_direction(qf[rf, :] * qscale, kf[rf, :], vf[rf, :], lf[rf, :], w2f[...], bf[...], st_f, False)
        of_ref[rf, :] = o.astype(of_ref.dtype)
        rb = pl.ds((per - 1 - s) * GLA_CHUNK, GLA_CHUNK)
        o, st_b = _gla_direction(qb[rb, :] * qscale, kb[rb, :], vb[rb, :], lb[rb, :], w2b[...], bb[...], st_b, True)
        ob_ref[rb, :] = o.astype(ob_ref.dtype)
    s_sc[0] = st_f
    s_sc[1] = st_b


GLA_CHUNK = 128
GLA_CHUNKS_PER_STEP = 4


def _gla(z, *, qcol, kcol, vcol, lcol, w2f, bf, w2b, bb, batch, seq):
    chunk = GLA_CHUNK * GLA_CHUNKS_PER_STEP
    assert seq % chunk == 0
    n = seq // chunk
    wk, wv = HEADS * GLA_K_DIM, HEADS * LANE

    def specs(cmap):
        return [pl.BlockSpec((chunk, wk), lambda b, c: (cmap(b, c), qcol)),
                pl.BlockSpec((chunk, wk), lambda b, c: (cmap(b, c), kcol)),
                pl.BlockSpec((chunk, wv), lambda b, c: (cmap(b, c), vcol)),
                pl.BlockSpec((chunk, LANE), lambda b, c: (cmap(b, c), lcol))]

    fmap = lambda b, c: b * n + c
    bmap = lambda b, c: b * n + n - 1 - c
    wspec = [pl.BlockSpec((LANE, wk), lambda b, c: (0, 0)), pl.BlockSpec((1, wk), lambda b, c: (0, 0))]
    out = jax.ShapeDtypeStruct((batch * seq, wv), BF16)
    return pl.pallas_call(
        functools.partial(_gla_kernel, qscale=GLA_K_DIM ** -0.5),
        grid=(batch, n),
        in_specs=specs(fmap) + specs(bmap) + wspec + wspec,
        out_specs=[pl.BlockSpec((chunk, wv), lambda b, c: (fmap(b, c), 0)),
                   pl.BlockSpec((chunk, wv), lambda b, c: (bmap(b, c), 0))],
        out_shape=[out, out],
        scratch_shapes=[pltpu.VMEM((2, wv, wk), F32)],
        compiler_params=_cparams(("parallel", "arbitrary")),
        name="gla",
    )(z, z, z, z, z, z, z, z, w2f, bf, w2b, bb)


def _layer_norm_rows(r, g, b):
    mu = jnp.mean(r, axis=-1, keepdims=True)
    d = r - mu
    var = jnp.mean(d * d, axis=-1, keepdims=True)
    return d * lax.rsqrt(var + LN_EPS) * g + b


def _outproj_kernel(x_ref, fin_ref, of_ref, ob_ref, gate_ref, ng_ref, wa_ref, wb_ref, lg_ref, lb_ref, rw_ref, rb_ref,
                    o_ref, ids_ref, gw_ref, cnt_ref, run_sc, *, group_norm):
    lin = of_ref[...].astype(F32) + ob_ref[...].astype(F32)
    parts = []
    for h in range(HEADS):
        zh = lin[:, h * LANE:(h + 1) * LANE]
        if group_norm:
            mu = jnp.mean(zh, axis=-1, keepdims=True)
            dz = zh - mu
            parts.append(dz * lax.rsqrt(jnp.mean(dz * dz, axis=-1, keepdims=True) + LN_EPS))
        else:
            parts.append(zh * lax.rsqrt(jnp.mean(zh * zh, axis=-1, keepdims=True) + RMS_EPS) * ng_ref[...])
    gate = gate_ref[...]
    lin = jnp.concatenate(parts, axis=1) * (gate * jax.nn.sigmoid(gate))
    y = (jnp.dot(fin_ref[...].astype(BF16), wa_ref[...], preferred_element_type=F32)
         + jnp.dot(lin.astype(BF16), wb_ref[...], preferred_element_type=F32))
    x1 = _layer_norm_rows(ALPHA * x_ref[...] + y, lg_ref[...], lb_ref[...])
    o_ref[...] = x1
    _route_rows(x1, rw_ref, rb_ref, ids_ref, gw_ref, cnt_ref, run_sc)


def _outproj(x, fin, of, ob, gate_src, gate_col, norm_gain, wa, wb, ln_g, ln_b, route_w, route_b, *, group_norm, tm=512):
    n, d = x.shape
    w = HEADS * LANE
    tm = min(tm, n)
    row = lambda i: (i, 0)
    const = lambda i: (0, 0)
    rw_hi = route_w.astype(BF16)
    rw = jnp.stack([rw_hi, (route_w - rw_hi.astype(F32)).astype(BF16)])
    return pl.pallas_call(
        functools.partial(_outproj_kernel, group_norm=group_norm),
        grid=(n // tm,),
        in_specs=[pl.BlockSpec((tm, d), row), pl.BlockSpec((tm, w), row), pl.BlockSpec((tm, w), row),
                  pl.BlockSpec((tm, w), row), pl.BlockSpec((tm, w), lambda i: (i, gate_col)),
                  pl.BlockSpec((1, LANE), const), pl.BlockSpec((w, d), const), pl.BlockSpec((w, d), const),
                  pl.BlockSpec((1, d), const), pl.BlockSpec((1, d), const),
                  pl.BlockSpec((2, d, LANE), lambda i: (0, 0, 0)), pl.BlockSpec((1, LANE), const)],
        out_specs=[pl.BlockSpec((tm, d), row), pl.BlockSpec((tm, LANE), row), pl.BlockSpec((tm, LANE), row),
                   pl.BlockSpec((1, LANE), const)],
        out_shape=[jax.ShapeDtypeStruct((n, d), F32), jax.ShapeDtypeStruct((n, LANE), jnp.int32),
                   jax.ShapeDtypeStruct((n, LANE), F32), jax.ShapeDtypeStruct((1, LANE), jnp.int32)],
        scratch_shapes=[pltpu.VMEM((1, LANE), F32)],
        compiler_params=_cparams(("arbitrary",)),
        name="outproj",
    )(x, fin, of, ob, gate_src, norm_gain.reshape(1, LANE), wa, wb, ln_g.reshape(1, d), ln_b.reshape(1, d), rw, route_b)


def _route_rows(x, w_ref, b_ref, ids_ref, gw_ref, cnt_ref, run_sc):
    @pl.when(pl.program_id(0) == 0)
    def _():
        run_sc[...] = jnp.zeros(run_sc.shape, F32)

    tm = x.shape[0]
    xh = x.astype(BF16)
    xl = (x - xh.astype(F32)).astype(BF16)
    wh, wl = w_ref[0], w_ref[1]
    logits = (jnp.dot(xh, wh, preferred_element_type=F32) + jnp.dot(xh, wl, preferred_element_type=F32)
              + jnp.dot(xl, wh, preferred_element_type=F32)) + b_ref[...]
    lane = lax.broadcasted_iota(jnp.int32, logits.shape, 1)
    neg = -jnp.inf
    gmask = (lane >= N_EXPERTS) & (lane < N_EXPERTS + N_GROUPS)
    gl = jnp.where(gmask, logits, neg)
    gmax = jnp.max(gl, axis=1, keepdims=True)
    lane_f = lane.astype(F32)
    first = lambda hit: jnp.min(jnp.where(hit, lane_f, float(LANE)), axis=1, keepdims=True).astype(jnp.int32)
    gidx = first(gl == gmax) - N_EXPERTS
    p_grp = 1.0 / jnp.sum(jnp.where(gmask, jnp.exp(gl - gmax), 0.0), axis=1, keepdims=True)
    el = jnp.where(_div_pow2(lane, EXPERTS_PER_GROUP) == gidx, logits, neg)
    l1 = jnp.max(el, axis=1, keepdims=True)
    e1 = first(el == l1)
    el2 = jnp.where(lane == e1, neg, el)
    l2 = jnp.max(el2, axis=1, keepdims=True)
    e2 = first(el2 == l2)
    t = jnp.exp(l2 - l1)
    w1 = p_grp / (1.0 + t)
    w2 = p_grp * t / (1.0 + t)

    onehot = jnp.where(lane == e1, 1.0, jnp.where(lane == e2, 1.0, 0.0))
    ri = lax.broadcasted_iota(jnp.int32, (tm, tm), 0)
    ci = lax.broadcasted_iota(jnp.int32, (tm, tm), 1)
    before = jnp.dot(jnp.where(ri > ci, 1.0, 0.0).astype(BF16), onehot.astype(BF16), preferred_element_type=F32)
    before = before + run_sc[...]
    r1 = jnp.sum(jnp.where(lane == e1, before, 0.0), axis=1, keepdims=True).astype(jnp.int32)
    r2 = jnp.sum(jnp.where(lane == e2, before, 0.0), axis=1, keepdims=True).astype(jnp.int32)
    run_sc[...] = run_sc[...] + jnp.sum(onehot, axis=0, keepdims=True)
    cnt_ref[...] = run_sc[...].astype(jnp.int32)
    ids_ref[...] = jnp.where(lane == 0, e1, jnp.where(lane == 1, e2, jnp.where(lane == 2, r1, jnp.where(lane == 3, r2, 0))))
    gw_ref[...] = jnp.where(lane == 0, w1, jnp.where(lane == 1, w2, 0.0))


MOE_BM = 256
SUBL = 8


def _rows_from_linear(ref, rows):
    return jnp.concatenate([ref[pl.ds(s, rows, stride=SUBL), :] for s in range(SUBL)], axis=1)


def _rows_to_linear(ref, val):
    for s in range(SUBL):
        ref[pl.ds(s, val.shape[0], stride=SUBL), :] = val[:, s * LANE:(s + 1) * LANE]


def _dispatch_kernel(pend_ref, padded_ref, dest_ref, x_ref, xs_hbm, idx_smem, zbuf, lin, sem_i, sem_z, sem):
    i = pl.program_id(0)
    tm = x_ref.shape[0]
    bm = zbuf.shape[0] // SUBL

    @pl.when(i == 0)
    def _():
        zbuf[...] = jnp.zeros(zbuf.shape, F32)

        def tail(e):
            start_row = pl.multiple_of((pend_ref[e] - bm) * SUBL, bm * SUBL)
            return pltpu.make_async_copy(zbuf, xs_hbm.at[pl.ds(start_row, bm * SUBL), :], sem_z)

        def start(e, carry):
            @pl.when(padded_ref[e] > 0)
            def _():
                tail(e).start()
            return carry

        def wait(e, carry):
            @pl.when(padded_ref[e] > 0)
            def _():
                tail(e).wait()
            return carry

        lax.fori_loop(0, N_EXPERTS, start, 0)
        lax.fori_loop(0, N_EXPERTS, wait, 0)

        def unused(b):
            start_row = pl.multiple_of(b * bm * SUBL, bm * SUBL)
            return pltpu.make_async_copy(zbuf, xs_hbm.at[pl.ds(start_row, bm * SUBL), :], sem_z)

        first_unused = pend_ref[N_EXPERTS - 1] // bm
        n_blocks = xs_hbm.shape[0] // (bm * SUBL)
        lax.fori_loop(first_unused, n_blocks, lambda b, c: (unused(b).start(), c)[1], 0)
        lax.fori_loop(first_unused, n_blocks, lambda b, c: (unused(b).wait(), c)[1], 0)

    cp = pltpu.make_async_copy(dest_ref.at[i], idx_smem, sem_i)
    cp.start()
    _rows_to_linear(lin, x_ref[...])
    cp.wait()

    def scatter(r, carry):
        src = lin.at[pl.ds(pl.multiple_of(r * SUBL, SUBL), SUBL), :]
        for k in range(2):
            dst = pl.multiple_of(idx_smem[2 * r + k] * SUBL, SUBL)
            pltpu.make_async_copy(src, xs_hbm.at[pl.ds(dst, SUBL), :], sem).start(priority=k)
        return carry

    lax.fori_loop(0, tm, scatter, 0, unroll=8)
    for k in range(2):
        pltpu.make_async_copy(lin, xs_hbm.at[pl.ds(0, tm * SUBL), :], sem).wait()


def _dispatch(x, dest, pend, padded, cap, *, tm=512):
    n, d = x.shape
    assert d == SUBL * LANE
    tm = min(tm, n)
    nt = n // tm
    grid_spec = pltpu.PrefetchScalarGridSpec(
        num_scalar_prefetch=2,
        grid=(nt,),
        in_specs=[pl.BlockSpec((nt, 2 * tm), lambda i, pe, pa: (0, 0)),
                  pl.BlockSpec((tm, d), lambda i, pe, pa: (i, 0))],
        out_specs=pl.BlockSpec(memory_space=pl.ANY),
        scratch_shapes=[pltpu.SMEM((2 * tm,), jnp.int32), pltpu.VMEM((MOE_BM * SUBL, LANE), F32),
                        pltpu.VMEM((tm * SUBL, LANE), F32),
                        pltpu.SemaphoreType.DMA(()), pltpu.SemaphoreType.DMA(()), pltpu.SemaphoreType.DMA(())],
    )
    return pl.pallas_call(
        _dispatch_kernel,
        grid_spec=grid_spec,
        out_shape=jax.ShapeDtypeStruct((cap * SUBL, LANE), F32),
        compiler_params=_cparams(("arbitrary",)),
        name="dispatch",
    )(pend, padded, dest.reshape(nt, 2 * tm), x)


def _experts_kernel(blk_e_ref, nused_ref, xs_ref, wg_ref, wu_ref, wd_ref, ys_ref, wgb, wub, wdb, hbuf):
    i = pl.program_id(0)
    nb = pl.num_programs(0) - 1
    cur = jnp.minimum(i, nb - 1)
    prv = jnp.maximum(i - 1, 0)

    @pl.when(i == 0)
    def _():
        hbuf[...] = jnp.zeros(hbuf.shape, BF16)

    @pl.when(jnp.logical_or(i == 0, blk_e_ref[cur] != blk_e_ref[jnp.minimum(prv, nb - 1)]))
    def _():
        wgb[...] = wg_ref[...].astype(BF16)
        wub[...] = wu_ref[...].astype(BF16)

    @pl.when(jnp.logical_or(i == 0, blk_e_ref[prv] != blk_e_ref[jnp.maximum(i - 2, 0)]))
    def _():
        wdb[...] = wd_ref[...].astype(BF16)

    slot = lax.rem(i, 2)
    y = jnp.dot(hbuf[1 - slot], wdb[...], preferred_element_type=F32)
    live = jnp.logical_and(i >= 1, i - 1 < nused_ref[0])
    _rows_to_linear(ys_ref, jnp.where(live, y, 0.0))
    xb = _rows_from_linear(xs_ref, xs_ref.shape[0] // SUBL).astype(BF16)
    hg = jnp.dot(xb, wgb[...], preferred_element_type=F32)
    hu = jnp.dot(xb, wub[...], preferred_element_type=F32)
    hbuf[slot] = (hg * jax.nn.sigmoid(hg) * hu).astype(BF16)


def _experts(xs, blk_e, n_used, w_gate, w_up, w_down, layer):
    cap = xs.shape[0] // SUBL
    bm = MOE_BM
    nb = cap // bm
    d, de = w_gate.shape[2], w_gate.shape[3]
    row_in = lambda i, be, nu: (jnp.minimum(i, nu[0] - 1), 0)
    row = lambda i, be, nu: (jnp.maximum(i - 1, 0), 0)
    w_cur = lambda i, be, nu: (layer, be[jnp.minimum(i, nb - 1)], 0, 0)
    w_prv = lambda i, be, nu: (layer, be[jnp.maximum(i - 1, 0)], 0, 0)
    grid_spec = pltpu.PrefetchScalarGridSpec(
        num_scalar_prefetch=2,
        grid=(nb + 1,),
        in_specs=[pl.BlockSpec((bm * SUBL, LANE), row_in),
                  pl.BlockSpec((None, None, d, de), w_cur),
                  pl.BlockSpec((None, None, d, de), w_cur),
                  pl.BlockSpec((None, None, de, d), w_prv)],
        out_specs=pl.BlockSpec((bm * SUBL, LANE), row),
        scratch_shapes=[pltpu.VMEM((d, de), BF16), pltpu.VMEM((d, de), BF16), pltpu.VMEM((de, d), BF16),
                        pltpu.VMEM((2, bm, de), BF16)],
    )
    return pl.pallas_call(
        _experts_kernel,
        grid_spec=grid_spec,
        out_shape=jax.ShapeDtypeStruct((cap * SUBL, LANE), F32),
        compiler_params=_cparams(("arbitrary",)),
        name="experts",
    )(blk_e, n_used, xs, w_gate, w_up, w_down)


def _combine_kernel(dest_ref, x_ref, gw_ref, g_ref, b_ref, ys_hbm, o_ref, idx_smem, ybuf, sem_i, sem):
    i = pl.program_id(0)
    tm = x_ref.shape[0]

    def issue(tile, slot):
        cp = pltpu.make_async_copy(dest_ref.at[tile], idx_smem, sem_i)
        cp.start()
        cp.wait()

        def gather(r, carry):
            row = pl.multiple_of(r * SUBL, SUBL)
            for k in range(2):
                src = pl.multiple_of(idx_smem[2 * r + k] * SUBL, SUBL)
                pltpu.make_async_copy(ys_hbm.at[pl.ds(src, SUBL), :], ybuf.at[slot, k, pl.ds(row, SUBL), :],
                                      sem.at[slot]).start(priority=k)
            return carry

        lax.fori_loop(0, tm, gather, 0, unroll=8)

    slot = lax.rem(i, 2)

    @pl.when(i == 0)
    def _():
        issue(0, 0)

    @pl.when(i + 1 < pl.num_programs(0))
    def _():
        issue(i + 1, 1 - slot)

    for k in range(2):
        pltpu.make_async_copy(ys_hbm.at[pl.ds(0, tm * SUBL), :], ybuf.at[slot, k], sem.at[slot]).wait()
    gw = gw_ref[...]
    ffn = (_rows_from_linear(ybuf.at[slot, 0], tm) * gw[:, 0:1] + _rows_from_linear(ybuf.at[slot, 1], tm) * gw[:, 1:2])
    o_ref[...] = _layer_norm_rows(ALPHA * x_ref[...] + ffn, g_ref[...], b_ref[...])


def _combine(x, ys, dest, gw, ln_g, ln_b, *, tm=512):
    n, d = x.shape
    tm = min(tm, n)
    nt = n // tm
    return pl.pallas_call(
        _combine_kernel,
        grid=(nt,),
        in_specs=[pl.BlockSpec((nt, 2 * tm), lambda i: (0, 0)),
                  pl.BlockSpec((tm, d), lambda i: (i, 0)), pl.BlockSpec((tm, LANE), lambda i: (i, 0)),
                  pl.BlockSpec((1, d), lambda i: (0, 0)), pl.BlockSpec((1, d), lambda i: (0, 0)),
                  pl.BlockSpec(memory_space=pl.ANY)],
        out_specs=pl.BlockSpec((tm, d), lambda i: (i, 0)),
        out_shape=jax.ShapeDtypeStruct((n, d), F32),
        scratch_shapes=[pltpu.SMEM((2 * tm,), jnp.int32), pltpu.VMEM((2, 2, tm * SUBL, LANE), F32),
                        pltpu.SemaphoreType.DMA(()), pltpu.SemaphoreType.DMA((2,))],
        compiler_params=_cparams(("arbitrary",)),
        name="combine",
    )(dest.reshape(nt, 2 * tm), x, gw, ln_g.reshape(1, d), ln_b.reshape(1, d), ys)


def _router_params(w_grp, b_grp, w_exp, b_exp):
    d = w_exp.shape[0]
    wr = jnp.zeros((d, LANE), F32).at[:, :N_EXPERTS].set(w_exp).at[:, N_EXPERTS:N_EXPERTS + N_GROUPS].set(w_grp)
    br = jnp.zeros((1, LANE), F32).at[0, :N_EXPERTS].set(b_exp).at[0, N_EXPERTS:N_EXPERTS + N_GROUPS].set(b_grp)
    return wr, br


def _moe(x, routing, w_gate, w_up, w_down, layer, ln_g, ln_b):
    n, d = x.shape
    ids, gw, cnt = routing
    bm = MOE_BM
    counts = cnt[0, :N_EXPERTS]
    padded = (counts + bm - 1) // bm * bm
    pend = jnp.cumsum(padded)
    pstart = pend - padded
    e, r = ids[:, 0:2], ids[:, 2:4]
    onehot = e[:, :, None] == jnp.arange(N_EXPERTS, dtype=jnp.int32)[None, None, :]
    dest = jnp.sum(jnp.where(onehot, pstart[None, None, :], 0), axis=-1) + r
    cap = 2 * n + N_EXPERTS * bm
    nb = cap // bm
    blk_start = jnp.arange(nb, dtype=jnp.int32) * bm
    blk_e = jnp.minimum(jnp.sum((pend[None, :] <= blk_start[:, None]).astype(jnp.int32), axis=1), N_EXPERTS - 1)
    n_used = (pend[-1:] // bm).astype(jnp.int32)
    xs = _dispatch(x, dest, pend.astype(jnp.int32), padded.astype(jnp.int32), cap)
    ys = _experts(xs, blk_e, n_used, w_gate, w_up, w_down, layer)
    return _combine(x, ys, dest, gw, ln_g, ln_b)


def _even_layer(x, batch, seq, layer_idx, w_in, dec_f, dec_b, lq1, lk1, lq2, lk2, subln, w_out, ln_g, ln_b, route):
    d = x.shape[1]
    w = HEADS * LANE
    kw = dict(batch=batch, seq=seq)
    diff_seg = [(0, DIFF_ROT_DIM), (DIFF_HEAD_DIM, DIFF_ROT_DIM)]
    q, k, v, gate, dq, dk, dvt = _even_in(
        x, w_in.astype(BF16), _rope_tables(seq, [(0, LANE)], RET_THETA),
        _rope_tables(seq, diff_seg, ROPE_THETA), **kw)
    decays = jnp.stack([dec_f, dec_b]).astype(F32)
    of, ob = _retention(q, k, v, decays, **kw)
    lam_init = 0.8 - 0.6 * math.exp(-0.3 * layer_idx)
    diff = _flash(dq, dk, dvt, diff=(lq1, lk1, lq2, lk2, subln), lam_init=lam_init, tq=2048, **kw)
    wo = w_out.astype(BF16)
    return _outproj(x, diff, of, ob, gate, 0, jnp.ones((LANE,), F32), wo[w:], wo[:w], ln_g, ln_b, *route,
                    group_norm=True)


def _odd_layer(x, batch, seq, w_in, q_norm, w_uq, kv_norm, w_ukv, w2_f, b_f, w2_b, b_b, gla_norm, w_out, ln_g, ln_b,
               route):
    d = x.shape[1]
    w = HEADS * LANE
    o = np.cumsum([0, MLA_Q_RANK, MLA_KV_RANK, MLA_ROPE, HEADS * GLA_K_DIM, HEADS * GLA_K_DIM, w, w,
                   GLA_GATE_RANK, GLA_GATE_RANK]).tolist()
    zeros = lambda c: jnp.zeros((d, c), F32)
    w_in2 = jnp.concatenate([
        w_in[:, o[0]:o[2]], zeros(MLA_NOPE), w_in[:, o[2]:o[3]], zeros(LANE - MLA_NOPE - MLA_ROPE),
        w_in[:, o[3]:o[7]], w_in[:, o[7]:o[9]], zeros(LANE - 2 * GLA_GATE_RANK)], axis=1).astype(BF16)
    kw = dict(batch=batch, seq=seq)
    qd = MLA_NOPE + MLA_ROPE
    w_uq2 = jnp.pad(w_uq.reshape(MLA_Q_RANK, HEADS, qd), ((0, 0), (0, 0), (0, LANE - qd))).reshape(MLA_Q_RANK, w)
    ukv = w_ukv.reshape(MLA_KV_RANK, HEADS, MLA_NOPE + MLA_V)
    w_uk2 = jnp.pad(ukv[:, :, :MLA_NOPE], ((0, 0), (0, 0), (0, LANE - MLA_NOPE))).reshape(MLA_KV_RANK, w)
    w_uv2 = ukv[:, :, MLA_NOPE:].reshape(MLA_KV_RANK, w)
    q, k, vt, zg = _odd_in(x, w_in2, q_norm, w_uq2.astype(BF16), kv_norm,
                           jnp.concatenate([w_uk2, w_uv2], axis=1).astype(BF16),
                           _rope_tables(seq, [(MLA_NOPE, MLA_ROPE)], ROPE_THETA), **kw)
    mla = _flash(q, k, vt, **kw)
    wk = HEADS * GLA_K_DIM
    pad_rows = lambda m, r0: jnp.zeros((LANE, wk), F32).at[r0:r0 + GLA_GATE_RANK].set(m).astype(BF16)
    of, ob = _gla(zg, qcol=0, kcol=1, vcol=1, lcol=12,
                  w2f=pad_rows(w2_f, 0), bf=b_f.reshape(1, wk), w2b=pad_rows(w2_b, GLA_GATE_RANK), bb=b_b.reshape(1, wk), **kw)
    wo = w_out.astype(BF16)
    return _outproj(x, mla, of, ob, zg, 2, gla_norm, wo[:w], wo[w:], ln_g, ln_b, *route, group_norm=False)


def kernel(x, ev_w_in, ev_ret_decay_f, ev_ret_decay_b, ev_lq1, ev_lk1, ev_lq2, ev_lk2, ev_subln, ev_w_out, od_w_in, od_q_norm, od_w_uq, od_kv_norm, od_w_ukv, od_gla_w2_f, od_gla_b_f, od_gla_w2_b, od_gla_b_b, od_gla_norm, od_w_out, ln1_g, ln1_b, ln2_g, ln2_b, moe_w_grp, moe_b_grp, moe_w_exp, moe_b_exp, moe_w_gate, moe_w_up, moe_w_down):
    batch, seq, d = x.shape
    h = x.reshape(batch * seq, d)
    for i in range(DEPTH):
        j = i // 2
        route = _router_params(moe_w_grp[i], moe_b_grp[i], moe_w_exp[i], moe_b_exp[i])
        if i % 2 == 0:
            h, *routing = _even_layer(h, batch, seq, i, ev_w_in[j], ev_ret_decay_f[j], ev_ret_decay_b[j], ev_lq1[j],
                                      ev_lk1[j], ev_lq2[j], ev_lk2[j], ev_subln[j], ev_w_out[j], ln1_g[i], ln1_b[i], route)
        else:
            h, *routing = _odd_layer(h, batch, seq, od_w_in[j], od_q_norm[j], od_w_uq[j], od_kv_norm[j], od_w_ukv[j],
                                     od_gla_w2_f[j], od_gla_b_f[j], od_gla_w2_b[j], od_gla_b_b[j], od_gla_norm[j],
                                     od_w_out[j], ln1_g[i], ln1_b[i], route)
        h = _moe(h, routing, moe_w_gate, moe_w_up, moe_w_down, i, ln2_g[i], ln2_b[i])
    return h.reshape(batch, seq, d)
```

```python
import functools
import math

import numpy as np
import jax
import jax.numpy as jnp
from jax import lax
from jax.experimental import pallas as pl
from jax.experimental.pallas import tpu as pltpu

F32 = jnp.float32
BF16 = jnp.bfloat16

HEADS = 4
LANE = 128
RET_THETA = 10000.0
ROPE_THETA = 500000.0
DIFF_HEAD_DIM = 64
DIFF_ROT_DIM = 16
MLA_Q_RANK = 256
MLA_KV_RANK = 128
MLA_NOPE = 64
MLA_ROPE = 32
MLA_V = 128
GLA_K_DIM = 64
GLA_GATE_RANK = 16
GLA_TAU = 16.0
N_GROUPS = 4
EXPERTS_PER_GROUP = 8
N_EXPERTS = N_GROUPS * EXPERTS_PER_GROUP
DEPTH = 2
ALPHA = (2.0 * DEPTH) ** 0.25
LN_EPS = 1e-5
RMS_EPS = 1e-6

VMEM_LIMIT = 48 * 1024 * 1024


def _div_pow2(x, n):
    return lax.shift_right_logical(x, int(n).bit_length() - 1)


def _mod_pow2(x, n):
    return lax.bitwise_and(x, int(n) - 1)


def _cparams(sem):
    return pltpu.CompilerParams(dimension_semantics=sem, vmem_limit_bytes=VMEM_LIMIT)


def _rope_heads(z, tabs, sh, scale):
    c, sa, sb = tabs
    outs = []
    for h in range(HEADS):
        zh = z[:, h * LANE:(h + 1) * LANE]
        outs.append((zh * c + pltpu.roll(zh, sh, axis=1) * sa + pltpu.roll(zh, LANE - sh, axis=1) * sb) * scale)
    return outs


def _even_in_kernel(x_ref, w_ref, rc, rsa, rsb, dc, dsa, dsb, q_ref, k_ref, v_ref, g_ref, dq_ref, dk_ref, dvt_ref):
    w = HEADS * LANE
    xb = x_ref[...].astype(BF16)
    part = lambda t: jnp.dot(xb, w_ref[:, t * w:(t + 1) * w], preferred_element_type=F32)
    ret_t = (rc[...], rsa[...], rsb[...])
    diff_t = (dc[...], dsa[...], dsb[...])
    for h, o in enumerate(_rope_heads(part(0), ret_t, LANE // 2, 1.0)):
        q_ref[h] = o.astype(BF16)
    for h, o in enumerate(_rope_heads(part(1), ret_t, LANE // 2, LANE ** -0.5)):
        k_ref[h] = o.astype(BF16)
    rv = part(2)
    for h in range(HEADS):
        v_ref[h] = rv[:, h * LANE:(h + 1) * LANE].astype(BF16)
    g_ref[...] = part(3)
    for h, o in enumerate(_rope_heads(part(4), diff_t, DIFF_ROT_DIM // 2, DIFF_HEAD_DIM ** -0.5 * LOG2E)):
        dq_ref[h] = o.T.astype(BF16)
    for h, o in enumerate(_rope_heads(part(5), diff_t, DIFF_ROT_DIM // 2, 1.0)):
        dk_ref[h] = o.astype(BF16)
    dv = part(6)
    for h in range(HEADS):
        dvt_ref[h, :LANE, :] = dv[:, h * LANE:(h + 1) * LANE].T.astype(BF16)
        dvt_ref[h, LANE:, :] = jnp.ones((ONES_ROWS, dvt_ref.shape[2]), BF16)


def _even_in(x, w, ret_tabs, diff_tabs, *, batch, seq):
    n, d = x.shape
    tm = min(FLASH_TK_DIFF, seq // 2)
    nt = seq // tm
    hw = HEADS * LANE
    heads = jax.ShapeDtypeStruct((batch, HEADS, seq, LANE), BF16)
    head_spec = pl.BlockSpec((None, HEADS, tm, LANE), lambda i: (i // nt, 0, i % nt, 0))
    heads_t = jax.ShapeDtypeStruct((batch, HEADS, LANE, seq), BF16)
    head_t_spec = pl.BlockSpec((None, HEADS, LANE, tm), lambda i: (i // nt, 0, 0, i % nt))
    tab_spec = pl.BlockSpec((tm, LANE), lambda i: (i % nt, 0))
    return pl.pallas_call(
        _even_in_kernel,
        grid=(n // tm,),
        in_specs=[pl.BlockSpec((tm, d), lambda i: (i, 0)), pl.BlockSpec((d, 7 * hw), lambda i: (0, 0))] + [tab_spec] * 6,
        out_specs=[head_spec, head_spec, head_spec, pl.BlockSpec((tm, hw), lambda i: (i, 0)), head_t_spec, head_spec,
                   pl.BlockSpec((None, HEADS, None, LANE + ONES_ROWS, tm), lambda i: (i // nt, 0, i % nt, 0, 0))],
        out_shape=[heads, heads, heads, jax.ShapeDtypeStruct((n, hw), F32), heads_t, heads,
                   jax.ShapeDtypeStruct((batch, HEADS, nt, LANE + ONES_ROWS, tm), BF16)],
        compiler_params=_cparams(("parallel",)),
        name="even_in",
    )(x, w, *ret_tabs, *diff_tabs)


def _rms_rows(z, g):
    return z * lax.rsqrt(jnp.mean(z * z, axis=-1, keepdims=True) + RMS_EPS) * g


def _odd_in_kernel(x_ref, w_ref, qn_ref, wq_ref, kvn_ref, wkv_ref, tc, tsa, tsb, q_ref, k_ref, vt_ref, zg_ref):
    hw = HEADS * LANE
    mla_w = MLA_Q_RANK + MLA_KV_RANK + LANE
    xb = x_ref[...].astype(BF16)
    zg_ref[...] = jnp.dot(xb, w_ref[:, mla_w:], preferred_element_type=F32)
    z1 = jnp.dot(xb, w_ref[:, :mla_w], preferred_element_type=F32)
    tabs = (tc[...], tsa[...], tsb[...])
    sh = MLA_ROPE // 2
    qh = jnp.dot(_rms_rows(z1[:, :MLA_Q_RANK], qn_ref[...]).astype(BF16), wq_ref[...], preferred_element_type=F32)
    for h, o in enumerate(_rope_heads(qh, tabs, sh, (MLA_NOPE + MLA_ROPE) ** -0.5 * LOG2E)):
        q_ref[h] = o.T.astype(BF16)
    ckv = _rms_rows(z1[:, MLA_Q_RANK:MLA_Q_RANK + MLA_KV_RANK], kvn_ref[...]).astype(BF16)
    kv = jnp.dot(ckv, wkv_ref[...], preferred_element_type=F32)
    kr = z1[:, MLA_Q_RANK + MLA_KV_RANK:]
    kr = kr * tabs[0] + pltpu.roll(kr, sh, axis=1) * tabs[1] + pltpu.roll(kr, LANE - sh, axis=1) * tabs[2]
    for h in range(HEADS):
        k_ref[h] = (kv[:, h * LANE:(h + 1) * LANE] + kr).astype(BF16)
        vt_ref[h, :LANE, :] = kv[:, hw + h * LANE:hw + (h + 1) * LANE].T.astype(BF16)
        vt_ref[h, LANE:, :] = jnp.ones((ONES_ROWS, vt_ref.shape[2]), BF16)


def _odd_in(x, w, q_norm, w_uq, kv_norm, w_ukv, tabs, *, batch, seq, tm=512):
    n, d = x.shape
    tm = min(tm, seq // 2)
    nt = seq // tm
    tk = min(FLASH_TK, seq // 2)
    per = tk // tm
    hw = HEADS * LANE
    gw_ = w.shape[1] - (MLA_Q_RANK + MLA_KV_RANK + LANE)
    heads = jax.ShapeDtypeStruct((batch, HEADS, seq, LANE), BF16)
    head_spec = pl.BlockSpec((None, HEADS, tm, LANE), lambda i: (i // nt, 0, i % nt, 0))
    tab_spec = pl.BlockSpec((tm, LANE), lambda i: (i % nt, 0))
    const = lambda i: (0, 0)
    return pl.pallas_call(
        _odd_in_kernel,
        grid=(n // tm,),
        in_specs=[pl.BlockSpec((tm, d), lambda i: (i, 0)), pl.BlockSpec(w.shape, const),
                  pl.BlockSpec((1, MLA_Q_RANK), const), pl.BlockSpec(w_uq.shape, const),
                  pl.BlockSpec((1, MLA_KV_RANK), const), pl.BlockSpec(w_ukv.shape, const)] + [tab_spec] * 3,
        out_specs=[pl.BlockSpec((None, HEADS, LANE, tm), lambda i: (i // nt, 0, 0, i % nt)), head_spec,
                   pl.BlockSpec((None, HEADS, None, LANE + ONES_ROWS, tm),
                                lambda i: (i // nt, 0, (i % nt) // per, 0, (i % nt) % per)),
                   pl.BlockSpec((tm, gw_), lambda i: (i, 0))],
        out_shape=[jax.ShapeDtypeStruct((batch, HEADS, LANE, seq), BF16), heads, jax.ShapeDtypeStruct((batch, HEADS, seq // tk, LANE + ONES_ROWS, tk), BF16),
                   jax.ShapeDtypeStruct((n, gw_), F32)],
        compiler_params=_cparams(("parallel",)),
        name="odd_in",
    )(x, w, q_norm.reshape(1, -1), w_uq, kv_norm.reshape(1, -1), w_ukv, *tabs)


def _rope_tables(seq, segs, theta):
    pos = jnp.arange(seq, dtype=F32)
    inv = jnp.zeros((LANE,), F32)
    lo = np.zeros((LANE,), bool)
    hi = np.zeros((LANE,), bool)
    for start, rot in segs:
        half = rot // 2
        f = jnp.power(jnp.float32(theta), -jnp.arange(0, rot, 2, dtype=F32) / rot)
        inv = inv.at[start:start + half].set(f).at[start + half:start + rot].set(f)
        lo[start:start + half] = True
        hi[start + half:start + rot] = True
    ang = pos[:, None] * inv[None, :]
    cos, sin = jnp.cos(ang), jnp.sin(ang)
    c = jnp.where(lo | hi, cos, 1.0)
    sa = jnp.where(hi, sin, 0.0)
    sb = jnp.where(lo, -sin, 0.0)
    return c, sa, sb


ONES_ROWS = 16
LOG2E = math.log2(math.e)
FLASH_TK = 1024
FLASH_TK_DIFF = 512


def _flash_kernel(*refs, ncomp, nk, lam_init):
    if ncomp == 2:
        q_ref, k_ref, vt_ref, lq1, lk1, lq2, lk2, g_ref, o_ref, *scr = refs
    else:
        q_ref, k_ref, vt_ref, o_ref, *scr = refs
    qm_sc, m_sc, acc_sc, s0, s1, cm0, cm1, p0, p1, al0, al1 = scr
    tk = s0.shape[1]
    tq = qm_sc.shape[2]

    def start(sub):
        q = q_ref[:, sub * tq:(sub + 1) * tq]
        if ncomp == 2:
            chan = lax.broadcasted_iota(jnp.int32, q.shape, 0)
            zero = jnp.zeros_like(q)
            qm_sc[0] = jnp.where(chan < DIFF_HEAD_DIM, q, zero)
            qm_sc[1] = jnp.where(chan >= DIFF_HEAD_DIM, q, zero)
        else:
            qm_sc[0] = q
        m_sc[...] = jnp.full(m_sc.shape, -jnp.inf, F32)
        acc_sc[...] = jnp.zeros(acc_sc.shape, F32)

    def scores(j, s_ref, cm_ref):
        k = k_ref[j * tk:(j + 1) * tk, :]
        for c in range(ncomp):
            s = jnp.dot(k, qm_sc[c], preferred_element_type=F32)
            s_ref[c] = s
            cm_ref[c] = jnp.max(s, axis=0, keepdims=True)

    def softmax(s_ref, cm_ref, p_ref, al_ref):
        for c in range(ncomp):
            m_old = m_sc[c]
            m_new = jnp.maximum(m_old, cm_ref[c])
            al_ref[c] = jnp.exp2(m_old - m_new)
            p_ref[c] = jnp.exp2(s_ref[c] - m_new).astype(BF16)
            m_sc[c] = m_new

    def values(j, p_ref, al_ref):
        vt = vt_ref[j]
        for c in range(ncomp):
            acc_sc[c] = al_ref[c] * acc_sc[c] + jnp.dot(vt, p_ref[c], preferred_element_type=F32)

    def normalised(c):
        acc = acc_sc[c]
        return acc[:LANE] / acc[LANE:LANE + 1]

    bufs = ((s0, cm0, p0, al0), (s1, cm1, p1, al1))
    for sub in range(q_ref.shape[1] // tq):
        start(sub)
        scores(0, s0, cm0)
        for j in range(nk):
            s_c, cm_c, p_c, al_c = bufs[j % 2]
            s_n, cm_n, p_n, al_n = bufs[(j + 1) % 2]
            if j + 1 < nk:
                scores(j + 1, s_n, cm_n)
            softmax(s_c, cm_c, p_c, al_c)
            if j >= 1:
                values(j - 1, p_n, al_n)
        values(nk - 1, *bufs[(nk - 1) % 2][2:])
        o = normalised(0)
        if ncomp == 2:
            lam = (jnp.exp(jnp.sum(lq1[...] * lk1[...], keepdims=True))
                   - jnp.exp(jnp.sum(lq2[...] * lk2[...], keepdims=True)) + lam_init)
            o = o - lam * normalised(1)
            o = o * lax.rsqrt(jnp.mean(o * o, axis=0, keepdims=True) + RMS_EPS) * g_ref[...] * (1.0 - lam_init)
        o_ref[sub * tq:(sub + 1) * tq, :] = o.T.astype(o_ref.dtype)


def _flash(q, k, vt, *, batch, seq, tq=512, passes=1, diff=None, lam_init=0.0):
    nk, vrows, tk = vt.shape[2], vt.shape[3], vt.shape[4]
    assert vrows == LANE + ONES_ROWS
    tq = min(tq, seq)
    passes = min(passes, seq // tq)
    bq = tq * passes
    nq = seq // bq
    ncomp = 2 if diff is not None else 1
    in_specs = [
        pl.BlockSpec((None, None, LANE, bq), lambda b, h, i: (b, h, 0, i)),
        pl.BlockSpec((None, None, seq, LANE), lambda b, h, i: (b, h, 0, 0)),
        pl.BlockSpec((None, None, nk, vrows, tk), lambda b, h, i: (b, h, 0, 0, 0)),
    ]
    args = [q, k, vt]
    if diff is not None:
        lq1, lk1, lq2, lk2, subln = diff
        for v in (lq1, lk1, lq2, lk2):
            in_specs.append(pl.BlockSpec((1, DIFF_HEAD_DIM), lambda b, h, i: (0, 0)))
            args.append(v.reshape(1, DIFF_HEAD_DIM))
        in_specs.append(pl.BlockSpec((LANE, 1), lambda b, h, i: (0, 0)))
        args.append(subln.reshape(LANE, 1))
    return pl.pallas_call(
        functools.partial(_flash_kernel, ncomp=ncomp, nk=nk, lam_init=lam_init),
        grid=(batch, HEADS, nq),
        in_specs=in_specs,
        out_specs=pl.BlockSpec((bq, LANE), lambda b, h, i: (b * nq + i, h)),
        out_shape=jax.ShapeDtypeStruct((batch * seq, HEADS * LANE), BF16),
        scratch_shapes=[pltpu.VMEM((ncomp, LANE, tq), BF16),
                        pltpu.VMEM((ncomp, 1, tq), F32), pltpu.VMEM((ncomp, vrows, tq), F32),
                        pltpu.VMEM((ncomp, tk, tq), F32), pltpu.VMEM((ncomp, tk, tq), F32),
                        pltpu.VMEM((ncomp, 1, tq), F32), pltpu.VMEM((ncomp, 1, tq), F32),
                        pltpu.VMEM((ncomp, tk, tq), BF16), pltpu.VMEM((ncomp, tk, tq), BF16),
                        pltpu.VMEM((ncomp, 1, tq), F32), pltpu.VMEM((ncomp, 1, tq), F32)],
        compiler_params=_cparams(("parallel", "parallel", "parallel")),
        name="flash_diff" if diff is not None else "flash_mla",
    )(*args)


def _ret_kernel(dec_ref, qf, kf, vf, qb, kb, vb, of_ref, ob_ref, s_sc, *, chunk):
    @pl.when(pl.program_id(1) == 0)
    def _():
        s_sc[...] = jnp.zeros(s_sc.shape, F32)

    ii = lax.broadcasted_iota(jnp.int32, (chunk, chunk), 0)
    jj = lax.broadcasted_iota(jnp.int32, (chunk, chunk), 1)
    r = lax.broadcasted_iota(jnp.int32, (chunk, 1), 0).astype(F32)
    for d, (q_ref, k_ref, v_ref, o_ref) in enumerate(((qf, kf, vf, of_ref), (qb, kb, vb, ob_ref))):
        for h in range(HEADS):
            la = -jnp.exp(jnp.full((1, 1), dec_ref[d, h], F32))
            if d == 0:
                mask, dist = ii >= jj, (ii - jj).astype(F32)
                qdec, kdec = jnp.exp(la * (r + 1.0)), jnp.exp(la * (chunk - 1.0 - r))
            else:
                mask, dist = jj > ii, (jj - ii).astype(F32)
                qdec, kdec = jnp.exp(la * (chunk - r)), jnp.exp(la * r)
            decay = jnp.where(mask, jnp.exp(jnp.where(mask, dist * la, 0.0)), 0.0)
            state = s_sc[d, h]
            per = q_ref.shape[1] // chunk
            for step in range(per):
                rows = pl.ds((step if d == 0 else per - 1 - step) * chunk, chunk)
                q, k, v = q_ref[h, rows, :], k_ref[h, rows, :], v_ref[h, rows, :]
                s = lax.dot_general(q, k, (((1,), (1,)), ((), ())), preferred_element_type=F32)
                o = jnp.dot((s * decay).astype(BF16), v, preferred_element_type=F32)
                o = o + qdec * jnp.dot(q, state.astype(BF16), preferred_element_type=F32)
                kd = (k.astype(F32) * kdec).astype(BF16)
                state = jnp.exp(la * float(chunk)) * state + lax.dot_general(
                    kd, v, (((0,), (0,)), ((), ())), preferred_element_type=F32)
                o_ref[rows, h * LANE:(h + 1) * LANE] = o.astype(o_ref.dtype)
            s_sc[d, h] = state


RET_CHUNKS_PER_STEP = 4


def _retention(q, k, v, decays, *, batch, seq, chunk=256):
    chunk = min(chunk, seq // RET_CHUNKS_PER_STEP)
    blk = chunk * RET_CHUNKS_PER_STEP
    n = seq // blk
    fwd = pl.BlockSpec((None, HEADS, blk, LANE), lambda b, c: (b, 0, c, 0))
    bwd = pl.BlockSpec((None, HEADS, blk, LANE), lambda b, c: (b, 0, n - 1 - c, 0))
    w = HEADS * LANE
    out = jax.ShapeDtypeStruct((batch * seq, w), BF16)
    return pl.pallas_call(
        functools.partial(_ret_kernel, chunk=chunk),
        grid=(batch, n),
        in_specs=[pl.BlockSpec(memory_space=pltpu.SMEM), fwd, fwd, fwd, bwd, bwd, bwd],
        out_specs=[pl.BlockSpec((blk, w), lambda b, c: (b * n + c, 0)),
                   pl.BlockSpec((blk, w), lambda b, c: (b * n + n - 1 - c, 0))],
        out_shape=[out, out],
        scratch_shapes=[pltpu.VMEM((2, HEADS, LANE, LANE), F32)],
        compiler_params=_cparams(("parallel", "arbitrary")),
        name="retention",
    )(decays, q, k, v, q, k, v)


GLA_SUB = 8


def _split3(x):
    x1 = x.astype(BF16)
    r1 = x - x1.astype(F32)
    x2 = r1.astype(BF16)
    x3 = (r1 - x2.astype(F32)).astype(BF16)
    return x1, x2, x3


def _gla_direction(q, k, v, lr, w2, bias, st, reverse):
    C, wk = q.shape
    wv = v.shape[1]
    dk, dv = wk // HEADS, wv // HEADS
    z = jnp.dot(lr.astype(BF16), w2, preferred_element_type=F32) + bias
    g = (jnp.minimum(z, 0.0) - jnp.log(1.0 + jnp.exp(-jnp.abs(z)))) * (LOG2E / GLA_TAU)
    ii = lax.broadcasted_iota(jnp.int32, (C, C), 0)
    jj = lax.broadcasted_iota(jnp.int32, (C, C), 1)
    tri = jnp.where(ii >= jj, 1.0, 0.0).astype(BF16)
    b = sum(jnp.dot(tri, part, preferred_element_type=F32) for part in _split3(g))
    tot = b[C - 1:C, :]
    c = (tot - b + g) if reverse else b

    qe = (q * jnp.exp2(jnp.minimum(c, 0.0))).astype(BF16)
    o = lax.dot_general(qe, st.astype(BF16), (((1,), (1,)), ((), ())), preferred_element_type=F32)
    ke = (k * jnp.exp2(jnp.minimum(tot - c, 0.0))).astype(BF16)
    upd = lax.dot_general(v.astype(BF16), ke, (((0,), (0,)), ((), ())), preferred_element_type=F32)
    rr = _div_pow2(lax.broadcasted_iota(jnp.int32, (wv, wk), 0), dv)
    cc = _div_pow2(lax.broadcasted_iota(jnp.int32, (wv, wk), 1), dk)
    new_st = jnp.where(rr == cc, st * jnp.exp2(tot) + upd, 0.0)

    lane_head = _div_pow2(lax.broadcasted_iota(jnp.int32, (C, wk), 1), dk)
    scores = [jnp.zeros((C, C), F32) for _ in range(HEADS)]
    hsz = C // 2
    while hsz >= GLA_SUB:
        blk = 2 * hsz
        rows = []
        for m in range(C // blk):
            rrow = m * blk + (hsz if reverse else hsz - 1)
            rows.append(jnp.broadcast_to(c[rrow:rrow + 1, :], (blk, wk)))
        ref = jnp.concatenate(rows, axis=0) if len(rows) > 1 else rows[0]
        qt = q * jnp.exp2(jnp.minimum(c - ref, 0.0))
        kt = (k * jnp.exp2(jnp.minimum(ref - c, 0.0))).astype(BF16)
        same = _div_pow2(ii, blk) == _div_pow2(jj, blk)
        if reverse:
            lvl = same & (_mod_pow2(ii, blk) < hsz) & (_mod_pow2(jj, blk) >= hsz)
        else:
            lvl = same & (_mod_pow2(ii, blk) >= hsz) & (_mod_pow2(jj, blk) < hsz)
        for h in range(HEADS):
            qh = jnp.where(lane_head == h, qt, 0.0).astype(BF16)
            s = lax.dot_general(qh, kt, (((1,), (1,)), ((), ())), preferred_element_type=F32)
            scores[h] = scores[h] + jnp.where(lvl, s, 0.0)
        hsz //= 2

    assert dv == C
    er = _div_pow2(lax.broadcasted_iota(jnp.int32, (wk, wv), 0), dk)
    ec = _div_pow2(lax.broadcasted_iota(jnp.int32, (wk, wv), 1), dv)
    expand = jnp.where(er == ec, 1.0, 0.0).astype(BF16)
    dist = (jj - ii) if reverse else (ii - jj)
    same_sub = _div_pow2(ii, GLA_SUB) == _div_pow2(jj, GLA_SUB)
    for lag in range(1 if reverse else 0, GLA_SUB):
        if lag == 0:
            t = q * k
        else:
            shift = (GLA_SUB - lag) if reverse else lag
            ks = pltpu.roll(k.reshape(C // GLA_SUB, GLA_SUB, wk), shift, axis=1).reshape(C, wk)
            cs = pltpu.roll(c.reshape(C // GLA_SUB, GLA_SUB, wk), shift, axis=1).reshape(C, wk)
            t = q * ks * jnp.exp2(jnp.minimum(c - cs, 0.0))
        red = jnp.dot(t.astype(BF16), expand, preferred_element_type=F32)
        on_diag = same_sub & (dist == lag)
        for h in range(HEADS):
            scores[h] = scores[h] + jnp.where(on_diag, red[:, h * dv:(h + 1) * dv], 0.0)

    vb = v.astype(BF16)
    o = o + jnp.concatenate(
        [jnp.dot(scores[h].astype(BF16), vb[:, h * dv:(h + 1) * dv], preferred_element_type=F32) for h in range(HEADS)],
        axis=1)
    return o, new_st


def _gla_kernel(qf, kf, vf, lf, qb, kb, vb, lb, w2f, bf, w2b, bb, of_ref, ob_ref, s_sc, *, qscale):
    @pl.when(pl.program_id(1) == 0)
    def _():
        s_sc[...] = jnp.zeros(s_sc.shape, F32)

    per = qf.shape[0] // GLA_CHUNK
    st_f, st_b = s_sc[0], s_sc[1]
    for s in range(per):
        rf = pl.ds(s * GLA_CHUNK, GLA_CHUNK)
        o, st_f = _gla_direction(qf[rf, :] * qscale, kf[rf, :], vf[rf, :], lf[rf, :], w2f[...], bf[...], st_f, False)
        of_ref[rf, :] = o.astype(of_ref.dtype)
        rb = pl.ds((per - 1 - s) * GLA_CHUNK, GLA_CHUNK)
        o, st_b = _gla_direction(qb[rb, :] * qscale, kb[rb, :], vb[rb, :], lb[rb, :], w2b[...], bb[...], st_b, True)
        ob_ref[rb, :] = o.astype(ob_ref.dtype)
    s_sc[0] = st_f
    s_sc[1] = st_b


GLA_CHUNK = 128
GLA_CHUNKS_PER_STEP = 4


def _gla(z, *, qcol, kcol, vcol, lcol, w2f, bf, w2b, bb, batch, seq):
    chunk = GLA_CHUNK * GLA_CHUNKS_PER_STEP
    assert seq % chunk == 0
    n = seq // chunk
    wk, wv = HEADS * GLA_K_DIM, HEADS * LANE

    def specs(cmap):
        return [pl.BlockSpec((chunk, wk), lambda b, c: (cmap(b, c), qcol)),
                pl.BlockSpec((chunk, wk), lambda b, c: (cmap(b, c), kcol)),
                pl.BlockSpec((chunk, wv), lambda b, c: (cmap(b, c), vcol)),
                pl.BlockSpec((chunk, LANE), lambda b, c: (cmap(b, c), lcol))]

    fmap = lambda b, c: b * n + c
    bmap = lambda b, c: b * n + n - 1 - c
    wspec = [pl.BlockSpec((LANE, wk), lambda b, c: (0, 0)), pl.BlockSpec((1, wk), lambda b, c: (0, 0))]
    out = jax.ShapeDtypeStruct((batch * seq, wv), BF16)
    return pl.pallas_call(
        functools.partial(_gla_kernel, qscale=GLA_K_DIM ** -0.5),
        grid=(batch, n),
        in_specs=specs(fmap) + specs(bmap) + wspec + wspec,
        out_specs=[pl.BlockSpec((chunk, wv), lambda b, c: (fmap(b, c), 0)),
                   pl.BlockSpec((chunk, wv), lambda b, c: (bmap(b, c), 0))],
        out_shape=[out, out],
        scratch_shapes=[pltpu.VMEM((2, wv, wk), F32)],
        compiler_params=_cparams(("parallel", "arbitrary")),
        name="gla",
    )(z, z, z, z, z, z, z, z, w2f, bf, w2b, bb)


def _layer_norm_rows(r, g, b):
    mu = jnp.mean(r, axis=-1, keepdims=True)
    d = r - mu
    var = jnp.mean(d * d, axis=-1, keepdims=True)
    return d * lax.rsqrt(var + LN_EPS) * g + b


def _outproj_kernel(x_ref, fin_ref, of_ref, ob_ref, gate_ref, ng_ref, wa_ref, wb_ref, lg_ref, lb_ref, rw_ref, rb_ref,
                    o_ref, ids_ref, gw_ref, cnt_ref, run_sc, *, group_norm):
    lin = of_ref[...].astype(F32) + ob_ref[...].astype(F32)
    parts = []
    for h in range(HEADS):
        zh = lin[:, h * LANE:(h + 1) * LANE]
        if group_norm:
            mu = jnp.mean(zh, axis=-1, keepdims=True)
            dz = zh - mu
            parts.append(dz * lax.rsqrt(jnp.mean(dz * dz, axis=-1, keepdims=True) + LN_EPS))
        else:
            parts.append(zh * lax.rsqrt(jnp.mean(zh * zh, axis=-1, keepdims=True) + RMS_EPS) * ng_ref[...])
    gate = gate_ref[...]
    lin = jnp.concatenate(parts, axis=1) * (gate * jax.nn.sigmoid(gate))
    y = (jnp.dot(fin_ref[...].astype(BF16), wa_ref[...], preferred_element_type=F32)
         + jnp.dot(lin.astype(BF16), wb_ref[...], preferred_element_type=F32))
    x1 = _layer_norm_rows(ALPHA * x_ref[...] + y, lg_ref[...], lb_ref[...])
    o_ref[...] = x1
    _route_rows(x1, rw_ref, rb_ref, ids_ref, gw_ref, cnt_ref, run_sc)


def _outproj(x, fin, of, ob, gate_src, gate_col, norm_gain, wa, wb, ln_g, ln_b, route_w, route_b, *, group_norm, tm=512):
    n, d = x.shape
    w = HEADS * LANE
    tm = min(tm, n)
    row = lambda i: (i, 0)
    const = lambda i: (0, 0)
    rw_hi = route_w.astype(BF16)
    rw = jnp.stack([rw_hi, (route_w - rw_hi.astype(F32)).astype(BF16)])
    return pl.pallas_call(
        functools.partial(_outproj_kernel, group_norm=group_norm),
        grid=(n // tm,),
        in_specs=[pl.BlockSpec((tm, d), row), pl.BlockSpec((tm, w), row), pl.BlockSpec((tm, w), row),
                  pl.BlockSpec((tm, w), row), pl.BlockSpec((tm, w), lambda i: (i, gate_col)),
                  pl.BlockSpec((1, LANE), const), pl.BlockSpec((w, d), const), pl.BlockSpec((w, d), const),
                  pl.BlockSpec((1, d), const), pl.BlockSpec((1, d), const),
                  pl.BlockSpec((2, d, LANE), lambda i: (0, 0, 0)), pl.BlockSpec((1, LANE), const)],
        out_specs=[pl.BlockSpec((tm, d), row), pl.BlockSpec((tm, LANE), row), pl.BlockSpec((tm, LANE), row),
                   pl.BlockSpec((1, LANE), const)],
        out_shape=[jax.ShapeDtypeStruct((n, d), F32), jax.ShapeDtypeStruct((n, LANE), jnp.int32),
                   jax.ShapeDtypeStruct((n, LANE), F32), jax.ShapeDtypeStruct((1, LANE), jnp.int32)],
        scratch_shapes=[pltpu.VMEM((1, LANE), F32)],
        compiler_params=_cparams(("arbitrary",)),
        name="outproj",
    )(x, fin, of, ob, gate_src, norm_gain.reshape(1, LANE), wa, wb, ln_g.reshape(1, d), ln_b.reshape(1, d), rw, route_b)


def _route_rows(x, w_ref, b_ref, ids_ref, gw_ref, cnt_ref, run_sc):
    @pl.when(pl.program_id(0) == 0)
    def _():
        run_sc[...] = jnp.zeros(run_sc.shape, F32)

    tm = x.shape[0]
    xh = x.astype(BF16)
    xl = (x - xh.astype(F32)).astype(BF16)
    wh, wl = w_ref[0], w_ref[1]
    logits = (jnp.dot(xh, wh, preferred_element_type=F32) + jnp.dot(xh, wl, preferred_element_type=F32)
              + jnp.dot(xl, wh, preferred_element_type=F32)) + b_ref[...]
    lane = lax.broadcasted_iota(jnp.int32, logits.shape, 1)
    neg = -jnp.inf
    gmask = (lane >= N_EXPERTS) & (lane < N_EXPERTS + N_GROUPS)
    gl = jnp.where(gmask, logits, neg)
    gmax = jnp.max(gl, axis=1, keepdims=True)
    lane_f = lane.astype(F32)
    first = lambda hit: jnp.min(jnp.where(hit, lane_f, float(LANE)), axis=1, keepdims=True).astype(jnp.int32)
    gidx = first(gl == gmax) - N_EXPERTS
    p_grp = 1.0 / jnp.sum(jnp.where(gmask, jnp.exp(gl - gmax), 0.0), axis=1, keepdims=True)
    el = jnp.where(_div_pow2(lane, EXPERTS_PER_GROUP) == gidx, logits, neg)
    l1 = jnp.max(el, axis=1, keepdims=True)
    e1 = first(el == l1)
    el2 = jnp.where(lane == e1, neg, el)
    l2 = jnp.max(el2, axis=1, keepdims=True)
    e2 = first(el2 == l2)
    t = jnp.exp(l2 - l1)
    w1 = p_grp / (1.0 + t)
    w2 = p_grp * t / (1.0 + t)

    onehot = jnp.where(lane == e1, 1.0, jnp.where(lane == e2, 1.0, 0.0))
    ri = lax.broadcasted_iota(jnp.int32, (tm, tm), 0)
    ci = lax.broadcasted_iota(jnp.int32, (tm, tm), 1)
    before = jnp.dot(jnp.where(ri > ci, 1.0, 0.0).astype(BF16), onehot.astype(BF16), preferred_element_type=F32)
    before = before + run_sc[...]
    r1 = jnp.sum(jnp.where(lane == e1, before, 0.0), axis=1, keepdims=True).astype(jnp.int32)
    r2 = jnp.sum(jnp.where(lane == e2, before, 0.0), axis=1, keepdims=True).astype(jnp.int32)
    run_sc[...] = run_sc[...] + jnp.sum(onehot, axis=0, keepdims=True)
    cnt_ref[...] = run_sc[...].astype(jnp.int32)
    ids_ref[...] = jnp.where(lane == 0, e1, jnp.where(lane == 1, e2, jnp.where(lane == 2, r1, jnp.where(lane == 3, r2, 0))))
    gw_ref[...] = jnp.where(lane == 0, w1, jnp.where(lane == 1, w2, 0.0))


MOE_BM = 256
SUBL = 8


def _rows_from_linear(ref, rows):
    return jnp.concatenate([ref[pl.ds(s, rows, stride=SUBL), :] for s in range(SUBL)], axis=1)


def _rows_to_linear(ref, val):
    for s in range(SUBL):
        ref[pl.ds(s, val.shape[0], stride=SUBL), :] = val[:, s * LANE:(s + 1) * LANE]


def _dispatch_kernel(pend_ref, padded_ref, dest_ref, x_ref, xs_hbm, idx_smem, zbuf, lin, sem_i, sem_z, sem):
    i = pl.program_id(0)
    tm = x_ref.shape[0]
    bm = zbuf.shape[0] // SUBL

    @pl.when(i == 0)
    def _():
        zbuf[...] = jnp.zeros(zbuf.shape, F32)

        def tail(e):
            start_row = pl.multiple_of((pend_ref[e] - bm) * SUBL, bm * SUBL)
            return pltpu.make_async_copy(zbuf, xs_hbm.at[pl.ds(start_row, bm * SUBL), :], sem_z)

        def start(e, carry):
            @pl.when(padded_ref[e] > 0)
            def _():
                tail(e).start()
            return carry

        def wait(e, carry):
            @pl.when(padded_ref[e] > 0)
            def _():
                tail(e).wait()
            return carry

        lax.fori_loop(0, N_EXPERTS, start, 0)
        lax.fori_loop(0, N_EXPERTS, wait, 0)

        def unused(b):
            start_row = pl.multiple_of(b * bm * SUBL, bm * SUBL)
            return pltpu.make_async_copy(zbuf, xs_hbm.at[pl.ds(start_row, bm * SUBL), :], sem_z)

        first_unused = pend_ref[N_EXPERTS - 1] // bm
        n_blocks = xs_hbm.shape[0] // (bm * SUBL)
        lax.fori_loop(first_unused, n_blocks, lambda b, c: (unused(b).start(), c)[1], 0)
        lax.fori_loop(first_unused, n_blocks, lambda b, c: (unused(b).wait(), c)[1], 0)

    cp = pltpu.make_async_copy(dest_ref.at[i], idx_smem, sem_i)
    cp.start()
    _rows_to_linear(lin, x_ref[...])
    cp.wait()

    def scatter(r, carry):
        src = lin.at[pl.ds(pl.multiple_of(r * SUBL, SUBL), SUBL), :]
        for k in range(2):
            dst = pl.multiple_of(idx_smem[2 * r + k] * SUBL, SUBL)
            pltpu.make_async_copy(src, xs_hbm.at[pl.ds(dst, SUBL), :], sem).start(priority=k)
        return carry

    lax.fori_loop(0, tm, scatter, 0, unroll=8)
    for k in range(2):
        pltpu.make_async_copy(lin, xs_hbm.at[pl.ds(0, tm * SUBL), :], sem).wait()


def _dispatch(x, dest, pend, padded, cap, *, tm=512):
    n, d = x.shape
    assert d == SUBL * LANE
    tm = min(tm, n)
    nt = n // tm
    grid_spec = pltpu.PrefetchScalarGridSpec(
        num_scalar_prefetch=2,
        grid=(nt,),
        in_specs=[pl.BlockSpec((nt, 2 * tm), lambda i, pe, pa: (0, 0)),
                  pl.BlockSpec((tm, d), lambda i, pe, pa: (i, 0))],
        out_specs=pl.BlockSpec(memory_space=pl.ANY),
        scratch_shapes=[pltpu.SMEM((2 * tm,), jnp.int32), pltpu.VMEM((MOE_BM * SUBL, LANE), F32),
                        pltpu.VMEM((tm * SUBL, LANE), F32),
                        pltpu.SemaphoreType.DMA(()), pltpu.SemaphoreType.DMA(()), pltpu.SemaphoreType.DMA(())],
    )
    return pl.pallas_call(
        _dispatch_kernel,
        grid_spec=grid_spec,
        out_shape=jax.ShapeDtypeStruct((cap * SUBL, LANE), F32),
        compiler_params=_cparams(("arbitrary",)),
        name="dispatch",
    )(pend, padded, dest.reshape(nt, 2 * tm), x)


def _experts_kernel(blk_e_ref, nused_ref, xs_ref, wg_ref, wu_ref, wd_ref, ys_ref, wgb, wub, wdb, hbuf):
    i = pl.program_id(0)
    nb = pl.num_programs(0) - 1
    cur = jnp.minimum(i, nb - 1)
    prv = jnp.maximum(i - 1, 0)

    @pl.when(i == 0)
    def _():
        hbuf[...] = jnp.zeros(hbuf.shape, BF16)

    @pl.when(jnp.logical_or(i == 0, blk_e_ref[cur] != blk_e_ref[jnp.minimum(prv, nb - 1)]))
    def _():
        wgb[...] = wg_ref[...].astype(BF16)
        wub[...] = wu_ref[...].astype(BF16)

    @pl.when(jnp.logical_or(i == 0, blk_e_ref[prv] != blk_e_ref[jnp.maximum(i - 2, 0)]))
    def _():
        wdb[...] = wd_ref[...].astype(BF16)

    slot = lax.rem(i, 2)
    y = jnp.dot(hbuf[1 - slot], wdb[...], preferred_element_type=F32)
    live = jnp.logical_and(i >= 1, i - 1 < nused_ref[0])
    _rows_to_linear(ys_ref, jnp.where(live, y, 0.0))
    xb = _rows_from_linear(xs_ref, xs_ref.shape[0] // SUBL).astype(BF16)
    hg = jnp.dot(xb, wgb[...], preferred_element_type=F32)
    hu = jnp.dot(xb, wub[...], preferred_element_type=F32)
    hbuf[slot] = (hg * jax.nn.sigmoid(hg) * hu).astype(BF16)


def _experts(xs, blk_e, n_used, w_gate, w_up, w_down, layer):
    cap = xs.shape[0] // SUBL
    bm = MOE_BM
    nb = cap // bm
    d, de = w_gate.shape[2], w_gate.shape[3]
    row_in = lambda i, be, nu: (jnp.minimum(i, nu[0] - 1), 0)
    row = lambda i, be, nu: (jnp.maximum(i - 1, 0), 0)
    w_cur = lambda i, be, nu: (layer, be[jnp.minimum(i, nb - 1)], 0, 0)
    w_prv = lambda i, be, nu: (layer, be[jnp.maximum(i - 1, 0)], 0, 0)
    grid_spec = pltpu.PrefetchScalarGridSpec(
        num_scalar_prefetch=2,
        grid=(nb + 1,),
        in_specs=[pl.BlockSpec((bm * SUBL, LANE), row_in),
                  pl.BlockSpec((None, None, d, de), w_cur),
                  pl.BlockSpec((None, None, d, de), w_cur),
                  pl.BlockSpec((None, None, de, d), w_prv)],
        out_specs=pl.BlockSpec((bm * SUBL, LANE), row),
        scratch_shapes=[pltpu.VMEM((d, de), BF16), pltpu.VMEM((d, de), BF16), pltpu.VMEM((de, d), BF16),
                        pltpu.VMEM((2, bm, de), BF16)],
    )
    return pl.pallas_call(
        _experts_kernel,
        grid_spec=grid_spec,
        out_shape=jax.ShapeDtypeStruct((cap * SUBL, LANE), F32),
        compiler_params=_cparams(("arbitrary",)),
        name="experts",
    )(blk_e, n_used, xs, w_gate, w_up, w_down)


def _combine_kernel(dest_ref, x_ref, gw_ref, g_ref, b_ref, ys_hbm, o_ref, idx_smem, ybuf, sem_i, sem):
    i = pl.program_id(0)
    tm = x_ref.shape[0]

    def issue(tile, slot):
        cp = pltpu.make_async_copy(dest_ref.at[tile], idx_smem, sem_i)
        cp.start()
        cp.wait()

        def gather(r, carry):
            row = pl.multiple_of(r * SUBL, SUBL)
            for k in range(2):
                src = pl.multiple_of(idx_smem[2 * r + k] * SUBL, SUBL)
                pltpu.make_async_copy(ys_hbm.at[pl.ds(src, SUBL), :], ybuf.at[slot, k, pl.ds(row, SUBL), :],
                                      sem.at[slot]).start(priority=k)
            return carry

        lax.fori_loop(0, tm, gather, 0, unroll=8)

    slot = lax.rem(i, 2)

    @pl.when(i == 0)
    def _():
        issue(0, 0)

    @pl.when(i + 1 < pl.num_programs(0))
    def _():
        issue(i + 1, 1 - slot)

    for k in range(2):
        pltpu.make_async_copy(ys_hbm.at[pl.ds(0, tm * SUBL), :], ybuf.at[slot, k], sem.at[slot]).wait()
    gw = gw_ref[...]
    ffn = (_rows_from_linear(ybuf.at[slot, 0], tm) * gw[:, 0:1] + _rows_from_linear(ybuf.at[slot, 1], tm) * gw[:, 1:2])
    o_ref[...] = _layer_norm_rows(ALPHA * x_ref[...] + ffn, g_ref[...], b_ref[...])


def _combine(x, ys, dest, gw, ln_g, ln_b, *, tm=512):
    n, d = x.shape
    tm = min(tm, n)
    nt = n // tm
    return pl.pallas_call(
        _combine_kernel,
        grid=(nt,),
        in_specs=[pl.BlockSpec((nt, 2 * tm), lambda i: (0, 0)),
                  pl.BlockSpec((tm, d), lambda i: (i, 0)), pl.BlockSpec((tm, LANE), lambda i: (i, 0)),
                  pl.BlockSpec((1, d), lambda i: (0, 0)), pl.BlockSpec((1, d), lambda i: (0, 0)),
                  pl.BlockSpec(memory_space=pl.ANY)],
        out_specs=pl.BlockSpec((tm, d), lambda i: (i, 0)),
        out_shape=jax.ShapeDtypeStruct((n, d), F32),
        scratch_shapes=[pltpu.SMEM((2 * tm,), jnp.int32), pltpu.VMEM((2, 2, tm * SUBL, LANE), F32),
                        pltpu.SemaphoreType.DMA(()), pltpu.SemaphoreType.DMA((2,))],
        compiler_params=_cparams(("arbitrary",)),
        name="combine",
    )(dest.reshape(nt, 2 * tm), x, gw, ln_g.reshape(1, d), ln_b.reshape(1, d), ys)


def _router_params(w_grp, b_grp, w_exp, b_exp):
    d = w_exp.shape[0]
    wr = jnp.zeros((d, LANE), F32).at[:, :N_EXPERTS].set(w_exp).at[:, N_EXPERTS:N_EXPERTS + N_GROUPS].set(w_grp)
    br = jnp.zeros((1, LANE), F32).at[0, :N_EXPERTS].set(b_exp).at[0, N_EXPERTS:N_EXPERTS + N_GROUPS].set(b_grp)
    return wr, br


def _moe(x, routing, w_gate, w_up, w_down, layer, ln_g, ln_b):
    n, d = x.shape
    ids, gw, cnt = routing
    bm = MOE_BM
    counts = cnt[0, :N_EXPERTS]
    padded = (counts + bm - 1) // bm * bm
    pend = jnp.cumsum(padded)
    pstart = pend - padded
    e, r = ids[:, 0:2], ids[:, 2:4]
    onehot = e[:, :, None] == jnp.arange(N_EXPERTS, dtype=jnp.int32)[None, None, :]
    dest = jnp.sum(jnp.where(onehot, pstart[None, None, :], 0), axis=-1) + r
    cap = 2 * n + N_EXPERTS * bm
    nb = cap // bm
    blk_start = jnp.arange(nb, dtype=jnp.int32) * bm
    blk_e = jnp.minimum(jnp.sum((pend[None, :] <= blk_start[:, None]).astype(jnp.int32), axis=1), N_EXPERTS - 1)
    n_used = (pend[-1:] // bm).astype(jnp.int32)
    xs = _dispatch(x, dest, pend.astype(jnp.int32), padded.astype(jnp.int32), cap)
    ys = _experts(xs, blk_e, n_used, w_gate, w_up, w_down, layer)
    return _combine(x, ys, dest, gw, ln_g, ln_b)


def _even_layer(x, batch, seq, layer_idx, w_in, dec_f, dec_b, lq1, lk1, lq2, lk2, subln, w_out, ln_g, ln_b, route):
    d = x.shape[1]
    w = HEADS * LANE
    kw = dict(batch=batch, seq=seq)
    diff_seg = [(0, DIFF_ROT_DIM), (DIFF_HEAD_DIM, DIFF_ROT_DIM)]
    q, k, v, gate, dq, dk, dvt = _even_in(
        x, w_in.astype(BF16), _rope_tables(seq, [(0, LANE)], RET_THETA),
        _rope_tables(seq, diff_seg, ROPE_THETA), **kw)
    decays = jnp.stack([dec_f, dec_b]).astype(F32)
    of, ob = _retention(q, k, v, decays, **kw)
    lam_init = 0.8 - 0.6 * math.exp(-0.3 * layer_idx)
    diff = _flash(dq, dk, dvt, diff=(lq1, lk1, lq2, lk2, subln), lam_init=lam_init, tq=2048, **kw)
    wo = w_out.astype(BF16)
    return _outproj(x, diff, of, ob, gate, 0, jnp.ones((LANE,), F32), wo[w:], wo[:w], ln_g, ln_b, *route,
                    group_norm=True)


def _odd_layer(x, batch, seq, w_in, q_norm, w_uq, kv_norm, w_ukv, w2_f, b_f, w2_b, b_b, gla_norm, w_out, ln_g, ln_b,
               route):
    d = x.shape[1]
    w = HEADS * LANE
    o = np.cumsum([0, MLA_Q_RANK, MLA_KV_RANK, MLA_ROPE, HEADS * GLA_K_DIM, HEADS * GLA_K_DIM, w, w,
                   GLA_GATE_RANK, GLA_GATE_RANK]).tolist()
    zeros = lambda c: jnp.zeros((d, c), F32)
    w_in2 = jnp.concatenate([
        w_in[:, o[0]:o[2]], zeros(MLA_NOPE), w_in[:, o[2]:o[3]], zeros(LANE - MLA_NOPE - MLA_ROPE),
        w_in[:, o[3]:o[7]], w_in[:, o[7]:o[9]], zeros(LANE - 2 * GLA_GATE_RANK)], axis=1).astype(BF16)
    kw = dict(batch=batch, seq=seq)
    qd = MLA_NOPE + MLA_ROPE
    w_uq2 = jnp.pad(w_uq.reshape(MLA_Q_RANK, HEADS, qd), ((0, 0), (0, 0), (0, LANE - qd))).reshape(MLA_Q_RANK, w)
    ukv = w_ukv.reshape(MLA_KV_RANK, HEADS, MLA_NOPE + MLA_V)
    w_uk2 = jnp.pad(ukv[:, :, :MLA_NOPE], ((0, 0), (0, 0), (0, LANE - MLA_NOPE))).reshape(MLA_KV_RANK, w)
    w_uv2 = ukv[:, :, MLA_NOPE:].reshape(MLA_KV_RANK, w)
    q, k, vt, zg = _odd_in(x, w_in2, q_norm, w_uq2.astype(BF16), kv_norm,
                           jnp.concatenate([w_uk2, w_uv2], axis=1).astype(BF16),
                           _rope_tables(seq, [(MLA_NOPE, MLA_ROPE)], ROPE_THETA), **kw)
    mla = _flash(q, k, vt, passes=4, **kw)
    wk = HEADS * GLA_K_DIM
    pad_rows = lambda m, r0: jnp.zeros((LANE, wk), F32).at[r0:r0 + GLA_GATE_RANK].set(m).astype(BF16)
    of, ob = _gla(zg, qcol=0, kcol=1, vcol=1, lcol=12,
                  w2f=pad_rows(w2_f, 0), bf=b_f.reshape(1, wk), w2b=pad_rows(w2_b, GLA_GATE_RANK), bb=b_b.reshape(1, wk), **kw)
    wo = w_out.astype(BF16)
    return _outproj(x, mla, of, ob, zg, 2, gla_norm, wo[:w], wo[w:], ln_g, ln_b, *route, group_norm=False)


def kernel(x, ev_w_in, ev_ret_decay_f, ev_ret_decay_b, ev_lq1, ev_lk1, ev_lq2, ev_lk2, ev_subln, ev_w_out, od_w_in, od_q_norm, od_w_uq, od_kv_norm, od_w_ukv, od_gla_w2_f, od_gla_b_f, od_gla_w2_b, od_gla_b_b, od_gla_norm, od_w_out, ln1_g, ln1_b, ln2_g, ln2_b, moe_w_grp, moe_b_grp, moe_w_exp, moe_b_exp, moe_w_gate, moe_w_up, moe_w_down):
    batch, seq, d = x.shape
    h = x.reshape(batch * seq, d)
    for i in range(DEPTH):
        j = i // 2
        route = _router_params(moe_w_grp[i], moe_b_grp[i], moe_w_exp[i], moe_b_exp[i])
        if i % 2 == 0:
            h, *routing = _even_layer(h, batch, seq, i, ev_w_in[j], ev_ret_decay_f[j], ev_ret_decay_b[j], ev_lq1[j],
                                      ev_lk1[j], ev_lq2[j], ev_lk2[j], ev_subln[j], ev_w_out[j], ln1_g[i], ln1_b[i], route)
        else:
            h, *routing = _odd_layer(h, batch, seq, od_w_in[j], od_q_norm[j], od_w_uq[j], od_kv_norm[j], od_w_ukv[j],
                                     od_gla_w2_f[j], od_gla_b_f[j], od_gla_w2_b[j], od_gla_b_b[j], od_gla_norm[j],
                                     od_w_out[j], ln1_g[i], ln1_b[i], route)
        h = _moe(h, routing, moe_w_gate, moe_w_up, moe_w_down, i, ln2_g[i], ln2_b[i])
    return h.reshape(batch, seq, d)
```

```python
import functools
import math

import numpy as np
import jax
import jax.numpy as jnp
from jax import lax
from jax.experimental import pallas as pl
from jax.experimental.pallas import tpu as pltpu

F32 = jnp.float32
BF16 = jnp.bfloat16

HEADS = 4
LANE = 128
RET_THETA = 10000.0
ROPE_THETA = 500000.0
DIFF_HEAD_DIM = 64
DIFF_ROT_DIM = 16
MLA_Q_RANK = 256
MLA_KV_RANK = 128
MLA_NOPE = 64
MLA_ROPE = 32
MLA_V = 128
GLA_K_DIM = 64
GLA_GATE_RANK = 16
GLA_TAU = 16.0
N_GROUPS = 4
EXPERTS_PER_GROUP = 8
N_EXPERTS = N_GROUPS * EXPERTS_PER_GROUP
DEPTH = 2
ALPHA = (2.0 * DEPTH) ** 0.25
LN_EPS = 1e-5
RMS_EPS = 1e-6

VMEM_LIMIT = 48 * 1024 * 1024


def _div_pow2(x, n):
    return lax.shift_right_logical(x, int(n).bit_length() - 1)


def _mod_pow2(x, n):
    return lax.bitwise_and(x, int(n) - 1)


def _cparams(sem):
    return pltpu.CompilerParams(dimension_semantics=sem, vmem_limit_bytes=VMEM_LIMIT)


def _rope_heads(z, tabs, sh, scale):
    c, sa, sb = tabs
    outs = []
    for h in range(HEADS):
        zh = z[:, h * LANE:(h + 1) * LANE]
        outs.append((zh * c + pltpu.roll(zh, sh, axis=1) * sa + pltpu.roll(zh, LANE - sh, axis=1) * sb) * scale)
    return outs


def _even_in_kernel(x_ref, w_ref, rc, rsa, rsb, dc, dsa, dsb, q_ref, k_ref, v_ref, g_ref, dq_ref, dk_ref, dvt_ref):
    w = HEADS * LANE
    xb = x_ref[...].astype(BF16)
    part = lambda t: jnp.dot(xb, w_ref[:, t * w:(t + 1) * w], preferred_element_type=F32)
    ret_t = (rc[...], rsa[...], rsb[...])
    diff_t = (dc[...], dsa[...], dsb[...])
    for h, o in enumerate(_rope_heads(part(0), ret_t, LANE // 2, 1.0)):
        q_ref[h] = o.astype(BF16)
    for h, o in enumerate(_rope_heads(part(1), ret_t, LANE // 2, LANE ** -0.5)):
        k_ref[h] = o.astype(BF16)
    rv = part(2)
    for h in range(HEADS):
        v_ref[h] = rv[:, h * LANE:(h + 1) * LANE].astype(BF16)
    g_ref[...] = part(3)
    for h, o in enumerate(_rope_heads(part(4), diff_t, DIFF_ROT_DIM // 2, DIFF_HEAD_DIM ** -0.5 * LOG2E)):
        dq_ref[h] = o.T.astype(BF16)
    for h, o in enumerate(_rope_heads(part(5), diff_t, DIFF_ROT_DIM // 2, 1.0)):
        dk_ref[h] = o.astype(BF16)
    dv = part(6)
    for h in range(HEADS):
        dvt_ref[h, :LANE, :] = dv[:, h * LANE:(h + 1) * LANE].T.astype(BF16)
        dvt_ref[h, LANE:, :] = jnp.ones((ONES_ROWS, dvt_ref.shape[2]), BF16)


def _even_in(x, w, ret_tabs, diff_tabs, *, batch, seq):
    n, d = x.shape
    tm = min(FLASH_TK_DIFF, seq // 2)
    nt = seq // tm
    hw = HEADS * LANE
    heads = jax.ShapeDtypeStruct((batch, HEADS, seq, LANE), BF16)
    head_spec = pl.BlockSpec((None, HEADS, tm, LANE), lambda i: (i // nt, 0, i % nt, 0))
    heads_t = jax.ShapeDtypeStruct((batch, HEADS, LANE, seq), BF16)
    head_t_spec = pl.BlockSpec((None, HEADS, LANE, tm), lambda i: (i // nt, 0, 0, i % nt))
    tab_spec = pl.BlockSpec((tm, LANE), lambda i: (i % nt, 0))
    return pl.pallas_call(
        _even_in_kernel,
        grid=(n // tm,),
        in_specs=[pl.BlockSpec((tm, d), lambda i: (i, 0)), pl.BlockSpec((d, 7 * hw), lambda i: (0, 0))] + [tab_spec] * 6,
        out_specs=[head_spec, head_spec, head_spec, pl.BlockSpec((tm, hw), lambda i: (i, 0)), head_t_spec, head_spec,
                   pl.BlockSpec((None, HEADS, None, LANE + ONES_ROWS, tm), lambda i: (i // nt, 0, i % nt, 0, 0))],
        out_shape=[heads, heads, heads, jax.ShapeDtypeStruct((n, hw), F32), heads_t, heads,
                   jax.ShapeDtypeStruct((batch, HEADS, nt, LANE + ONES_ROWS, tm), BF16)],
        compiler_params=_cparams(("parallel",)),
        name="even_in",
    )(x, w, *ret_tabs, *diff_tabs)


def _rms_rows(z, g):
    return z * lax.rsqrt(jnp.mean(z * z, axis=-1, keepdims=True) + RMS_EPS) * g


def _odd_in_kernel(x_ref, w_ref, qn_ref, wq_ref, kvn_ref, wkv_ref, tc, tsa, tsb, q_ref, k_ref, vt_ref, zg_ref):
    hw = HEADS * LANE
    mla_w = MLA_Q_RANK + MLA_KV_RANK + LANE
    xb = x_ref[...].astype(BF16)
    zg_ref[...] = jnp.dot(xb, w_ref[:, mla_w:], preferred_element_type=F32)
    z1 = jnp.dot(xb, w_ref[:, :mla_w], preferred_element_type=F32)
    tabs = (tc[...], tsa[...], tsb[...])
    sh = MLA_ROPE // 2
    qh = jnp.dot(_rms_rows(z1[:, :MLA_Q_RANK], qn_ref[...]).astype(BF16), wq_ref[...], preferred_element_type=F32)
    for h, o in enumerate(_rope_heads(qh, tabs, sh, (MLA_NOPE + MLA_ROPE) ** -0.5 * LOG2E)):
        q_ref[h] = o.T.astype(BF16)
    ckv = _rms_rows(z1[:, MLA_Q_RANK:MLA_Q_RANK + MLA_KV_RANK], kvn_ref[...]).astype(BF16)
    kv = jnp.dot(ckv, wkv_ref[...], preferred_element_type=F32)
    kr = z1[:, MLA_Q_RANK + MLA_KV_RANK:]
    kr = kr * tabs[0] + pltpu.roll(kr, sh, axis=1) * tabs[1] + pltpu.roll(kr, LANE - sh, axis=1) * tabs[2]
    for h in range(HEADS):
        k_ref[h] = (kv[:, h * LANE:(h + 1) * LANE] + kr).astype(BF16)
        vt_ref[h, :LANE, :] = kv[:, hw + h * LANE:hw + (h + 1) * LANE].T.astype(BF16)
        vt_ref[h, LANE:, :] = jnp.ones((ONES_ROWS, vt_ref.shape[2]), BF16)


def _odd_in(x, w, q_norm, w_uq, kv_norm, w_ukv, tabs, *, batch, seq, tm=512):
    n, d = x.shape
    tm = min(tm, seq // 2)
    nt = seq // tm
    tk = min(FLASH_TK, seq // 2)
    per = tk // tm
    hw = HEADS * LANE
    gw_ = w.shape[1] - (MLA_Q_RANK + MLA_KV_RANK + LANE)
    heads = jax.ShapeDtypeStruct((batch, HEADS, seq, LANE), BF16)
    head_spec = pl.BlockSpec((None, HEADS, tm, LANE), lambda i: (i // nt, 0, i % nt, 0))
    tab_spec = pl.BlockSpec((tm, LANE), lambda i: (i % nt, 0))
    const = lambda i: (0, 0)
    return pl.pallas_call(
        _odd_in_kernel,
        grid=(n // tm,),
        in_specs=[pl.BlockSpec((tm, d), lambda i: (i, 0)), pl.BlockSpec(w.shape, const),
                  pl.BlockSpec((1, MLA_Q_RANK), const), pl.BlockSpec(w_uq.shape, const),
                  pl.BlockSpec((1, MLA_KV_RANK), const), pl.BlockSpec(w_ukv.shape, const)] + [tab_spec] * 3,
        out_specs=[pl.BlockSpec((None, HEADS, LANE, tm), lambda i: (i // nt, 0, 0, i % nt)), head_spec,
                   pl.BlockSpec((None, HEADS, None, LANE + ONES_ROWS, tm),
                                lambda i: (i // nt, 0, (i % nt) // per, 0, (i % nt) % per)),
                   pl.BlockSpec((tm, gw_), lambda i: (i, 0))],
        out_shape=[jax.ShapeDtypeStruct((batch, HEADS, LANE, seq), BF16), heads, jax.ShapeDtypeStruct((batch, HEADS, seq // tk, LANE + ONES_ROWS, tk), BF16),
                   jax.ShapeDtypeStruct((n, gw_), F32)],
        compiler_params=_cparams(("parallel",)),
        name="odd_in",
    )(x, w, q_norm.reshape(1, -1), w_uq, kv_norm.reshape(1, -1), w_ukv, *tabs)


def _rope_tables(seq, segs, theta):
    pos = jnp.arange(seq, dtype=F32)
    inv = jnp.zeros((LANE,), F32)
    lo = np.zeros((LANE,), bool)
    hi = np.zeros((LANE,), bool)
    for start, rot in segs:
        half = rot // 2
        f = jnp.power(jnp.float32(theta), -jnp.arange(0, rot, 2, dtype=F32) / rot)
        inv = inv.at[start:start + half].set(f).at[start + half:start + rot].set(f)
        lo[start:start + half] = True
        hi[start + half:start + rot] = True
    ang = pos[:, None] * inv[None, :]
    cos, sin = jnp.cos(ang), jnp.sin(ang)
    c = jnp.where(lo | hi, cos, 1.0)
    sa = jnp.where(hi, sin, 0.0)
    sb = jnp.where(lo, -sin, 0.0)
    return c, sa, sb


ONES_ROWS = 16
LOG2E = math.log2(math.e)
FLASH_TK = 1024
FLASH_TK_DIFF = 512


def _flash_kernel(*refs, ncomp, nk, lam_init):
    if ncomp == 2:
        q_ref, k_ref, vt_ref, lq1, lk1, lq2, lk2, g_ref, o_ref, *scr = refs
    else:
        q_ref, k_ref, vt_ref, o_ref, *scr = refs
    qm_sc, m_sc, acc_sc, s0, s1, cm0, cm1, p0, p1, al0, al1 = scr
    tk = s0.shape[1]
    tq = qm_sc.shape[2]

    def start(sub):
        q = q_ref[:, sub * tq:(sub + 1) * tq]
        if ncomp == 2:
            chan = lax.broadcasted_iota(jnp.int32, q.shape, 0)
            zero = jnp.zeros_like(q)
            qm_sc[0] = jnp.where(chan < DIFF_HEAD_DIM, q, zero)
            qm_sc[1] = jnp.where(chan >= DIFF_HEAD_DIM, q, zero)
        else:
            qm_sc[0] = q
        m_sc[...] = jnp.full(m_sc.shape, -jnp.inf, F32)
        acc_sc[...] = jnp.zeros(acc_sc.shape, F32)

    def scores(j, s_ref, cm_ref):
        k = k_ref[j * tk:(j + 1) * tk, :]
        for c in range(ncomp):
            s = jnp.dot(k, qm_sc[c], preferred_element_type=F32)
            s_ref[c] = s
            cm_ref[c] = jnp.max(s, axis=0, keepdims=True)

    def softmax(s_ref, cm_ref, p_ref, al_ref):
        for c in range(ncomp):
            m_old = m_sc[c]
            m_new = jnp.maximum(m_old, cm_ref[c])
            al_ref[c] = jnp.exp2(m_old - m_new)
            p_ref[c] = jnp.exp2(s_ref[c] - m_new).astype(BF16)
            m_sc[c] = m_new

    def values(j, p_ref, al_ref):
        vt = vt_ref[j]
        for c in range(ncomp):
            acc_sc[c] = al_ref[c] * acc_sc[c] + jnp.dot(vt, p_ref[c], preferred_element_type=F32)

    def normalised(c):
        acc = acc_sc[c]
        return acc[:LANE] / acc[LANE:LANE + 1]

    bufs = ((s0, cm0, p0, al0), (s1, cm1, p1, al1))
    for sub in range(q_ref.shape[1] // tq):
        start(sub)
        scores(0, s0, cm0)
        for j in range(nk):
            s_c, cm_c, p_c, al_c = bufs[j % 2]
            s_n, cm_n, p_n, al_n = bufs[(j + 1) % 2]
            if j + 1 < nk:
                scores(j + 1, s_n, cm_n)
            softmax(s_c, cm_c, p_c, al_c)
            if j >= 1:
                values(j - 1, p_n, al_n)
        values(nk - 1, *bufs[(nk - 1) % 2][2:])
        o = normalised(0)
        if ncomp == 2:
            lam = (jnp.exp(jnp.sum(lq1[...] * lk1[...], keepdims=True))
                   - jnp.exp(jnp.sum(lq2[...] * lk2[...], keepdims=True)) + lam_init)
            o = o - lam * normalised(1)
            o = o * lax.rsqrt(jnp.mean(o * o, axis=0, keepdims=True) + RMS_EPS) * g_ref[...] * (1.0 - lam_init)
        o_ref[sub * tq:(sub + 1) * tq, :] = o.T.astype(o_ref.dtype)


def _flash(q, k, vt, *, batch, seq, tq=512, passes=1, diff=None, lam_init=0.0):
    nk, vrows, tk = vt.shape[2], vt.shape[3], vt.shape[4]
    assert vrows == LANE + ONES_ROWS
    tq = min(tq, seq)
    passes = min(passes, seq // tq)
    bq = tq * passes
    nq = seq // bq
    ncomp = 2 if diff is not None else 1
    in_specs = [
        pl.BlockSpec((None, None, LANE, bq), lambda b, h, i: (b, h, 0, i)),
        pl.BlockSpec((None, None, seq, LANE), lambda b, h, i: (b, h, 0, 0)),
        pl.BlockSpec((None, None, nk, vrows, tk), lambda b, h, i: (b, h, 0, 0, 0)),
    ]
    args = [q, k, vt]
    if diff is not None:
        lq1, lk1, lq2, lk2, subln = diff
        for v in (lq1, lk1, lq2, lk2):
            in_specs.append(pl.BlockSpec((1, DIFF_HEAD_DIM), lambda b, h, i: (0, 0)))
            args.append(v.reshape(1, DIFF_HEAD_DIM))
        in_specs.append(pl.BlockSpec((LANE, 1), lambda b, h, i: (0, 0)))
        args.append(subln.reshape(LANE, 1))
    return pl.pallas_call(
        functools.partial(_flash_kernel, ncomp=ncomp, nk=nk, lam_init=lam_init),
        grid=(batch, HEADS, nq),
        in_specs=in_specs,
        out_specs=pl.BlockSpec((bq, LANE), lambda b, h, i: (b * nq + i, h)),
        out_shape=jax.ShapeDtypeStruct((batch * seq, HEADS * LANE), BF16),
        scratch_shapes=[pltpu.VMEM((ncomp, LANE, tq), BF16),
                        pltpu.VMEM((ncomp, 1, tq), F32), pltpu.VMEM((ncomp, vrows, tq), F32),
                        pltpu.VMEM((ncomp, tk, tq), F32), pltpu.VMEM((ncomp, tk, tq), F32),
                        pltpu.VMEM((ncomp, 1, tq), F32), pltpu.VMEM((ncomp, 1, tq), F32),
                        pltpu.VMEM((ncomp, tk, tq), BF16), pltpu.VMEM((ncomp, tk, tq), BF16),
                        pltpu.VMEM((ncomp, 1, tq), F32), pltpu.VMEM((ncomp, 1, tq), F32)],
        compiler_params=_cparams(("parallel", "parallel", "parallel")),
        name="flash_diff" if diff is not None else "flash_mla",
    )(*args)


def _ret_kernel(dec_ref, qf, kf, vf, qb, kb, vb, of_ref, ob_ref, s_sc, *, chunk):
    @pl.when(pl.program_id(1) == 0)
    def _():
        s_sc[...] = jnp.zeros(s_sc.shape, F32)

    ii = lax.broadcasted_iota(jnp.int32, (chunk, chunk), 0)
    jj = lax.broadcasted_iota(jnp.int32, (chunk, chunk), 1)
    r = lax.broadcasted_iota(jnp.int32, (chunk, 1), 0).astype(F32)
    for d, (q_ref, k_ref, v_ref, o_ref) in enumerate(((qf, kf, vf, of_ref), (qb, kb, vb, ob_ref))):
        for h in range(HEADS):
            la = -jnp.exp(jnp.full((1, 1), dec_ref[d, h], F32))
            if d == 0:
                mask, dist = ii >= jj, (ii - jj).astype(F32)
                qdec, kdec = jnp.exp(la * (r + 1.0)), jnp.exp(la * (chunk - 1.0 - r))
            else:
                mask, dist = jj > ii, (jj - ii).astype(F32)
                qdec, kdec = jnp.exp(la * (chunk - r)), jnp.exp(la * r)
            decay = jnp.where(mask, jnp.exp(jnp.where(mask, dist * la, 0.0)), 0.0)
            state = s_sc[d, h]
            per = q_ref.shape[1] // chunk
            for step in range(per):
                rows = pl.ds((step if d == 0 else per - 1 - step) * chunk, chunk)
                q, k, v = q_ref[h, rows, :], k_ref[h, rows, :], v_ref[h, rows, :]
                s = lax.dot_general(q, k, (((1,), (1,)), ((), ())), preferred_element_type=F32)
                o = jnp.dot((s * decay).astype(BF16), v, preferred_element_type=F32)
                o = o + qdec * jnp.dot(q, state.astype(BF16), preferred_element_type=F32)
                kd = (k.astype(F32) * kdec).astype(BF16)
                state = jnp.exp(la * float(chunk)) * state + lax.dot_general(
                    kd, v, (((0,), (0,)), ((), ())), preferred_element_type=F32)
                o_ref[rows, h * LANE:(h + 1) * LANE] = o.astype(o_ref.dtype)
            s_sc[d, h] = state


RET_CHUNKS_PER_STEP = 4


def _retention(q, k, v, decays, *, batch, seq, chunk=256):
    chunk = min(chunk, seq // RET_CHUNKS_PER_STEP)
    blk = chunk * RET_CHUNKS_PER_STEP
    n = seq // blk
    fwd = pl.BlockSpec((None, HEADS, blk, LANE), lambda b, c: (b, 0, c, 0))
    bwd = pl.BlockSpec((None, HEADS, blk, LANE), lambda b, c: (b, 0, n - 1 - c, 0))
    w = HEADS * LANE
    out = jax.ShapeDtypeStruct((batch * seq, w), BF16)
    return pl.pallas_call(
        functools.partial(_ret_kernel, chunk=chunk),
        grid=(batch, n),
        in_specs=[pl.BlockSpec(memory_space=pltpu.SMEM), fwd, fwd, fwd, bwd, bwd, bwd],
        out_specs=[pl.BlockSpec((blk, w), lambda b, c: (b * n + c, 0)),
                   pl.BlockSpec((blk, w), lambda b, c: (b * n + n - 1 - c, 0))],
        out_shape=[out, out],
        scratch_shapes=[pltpu.VMEM((2, HEADS, LANE, LANE), F32)],
        compiler_params=_cparams(("parallel", "arbitrary")),
        name="retention",
    )(decays, q, k, v, q, k, v)


GLA_SUB = 8


def _split3(x):
    x1 = x.astype(BF16)
    r1 = x - x1.astype(F32)
    x2 = r1.astype(BF16)
    x3 = (r1 - x2.astype(F32)).astype(BF16)
    return x1, x2, x3


def _gla_direction(q, k, v, lr, w2, bias, st, reverse):
    C, wk = q.shape
    wv = v.shape[1]
    dk, dv = wk // HEADS, wv // HEADS
    z = jnp.dot(lr.astype(BF16), w2, preferred_element_type=F32) + bias
    g = (jnp.minimum(z, 0.0) - jnp.log(1.0 + jnp.exp(-jnp.abs(z)))) * (LOG2E / GLA_TAU)
    ii = lax.broadcasted_iota(jnp.int32, (C, C), 0)
    jj = lax.broadcasted_iota(jnp.int32, (C, C), 1)
    tri = jnp.where(ii >= jj, 1.0, 0.0).astype(BF16)
    b = sum(jnp.dot(tri, part, preferred_element_type=F32) for part in _split3(g))
    tot = b[C - 1:C, :]
    c = (tot - b + g) if reverse else b

    qe = (q * jnp.exp2(jnp.minimum(c, 0.0))).astype(BF16)
    o = lax.dot_general(qe, st.astype(BF16), (((1,), (1,)), ((), ())), preferred_element_type=F32)
    ke = (k * jnp.exp2(jnp.minimum(tot - c, 0.0))).astype(BF16)
    upd = lax.dot_general(v.astype(BF16), ke, (((0,), (0,)), ((), ())), preferred_element_type=F32)
    rr = _div_pow2(lax.broadcasted_iota(jnp.int32, (wv, wk), 0), dv)
    cc = _div_pow2(lax.broadcasted_iota(jnp.int32, (wv, wk), 1), dk)
    new_st = jnp.where(rr == cc, st * jnp.exp2(tot) + upd, 0.0)

    lane_head = _div_pow2(lax.broadcasted_iota(jnp.int32, (C, wk), 1), dk)
    scores = [jnp.zeros((C, C), F32) for _ in range(HEADS)]
    hsz = C // 2
    while hsz >= GLA_SUB:
        blk = 2 * hsz
        rows = []
        for m in range(C // blk):
            rrow = m * blk + (hsz if reverse else hsz - 1)
            rows.append(jnp.broadcast_to(c[rrow:rrow + 1, :], (blk, wk)))
        ref = jnp.concatenate(rows, axis=0) if len(rows) > 1 else rows[0]
        qt = q * jnp.exp2(jnp.minimum(c - ref, 0.0))
        kt = (k * jnp.exp2(jnp.minimum(ref - c, 0.0))).astype(BF16)
        same = _div_pow2(ii, blk) == _div_pow2(jj, blk)
        if reverse:
            lvl = same & (_mod_pow2(ii, blk) < hsz) & (_mod_pow2(jj, blk) >= hsz)
        else:
            lvl = same & (_mod_pow2(ii, blk) >= hsz) & (_mod_pow2(jj, blk) < hsz)
        for h in range(HEADS):
            qh = jnp.where(lane_head == h, qt, 0.0).astype(BF16)
            s = lax.dot_general(qh, kt, (((1,), (1,)), ((), ())), preferred_element_type=F32)
            scores[h] = scores[h] + jnp.where(lvl, s, 0.0)
        hsz //= 2

    assert dv == C
    er = _div_pow2(lax.broadcasted_iota(jnp.int32, (wk, wv), 0), dk)
    ec = _div_pow2(lax.broadcasted_iota(jnp.int32, (wk, wv), 1), dv)
    expand = jnp.where(er == ec, 1.0, 0.0).astype(BF16)
    dist = (jj - ii) if reverse else (ii - jj)
    same_sub = _div_pow2(ii, GLA_SUB) == _div_pow2(jj, GLA_SUB)
    for lag in range(1 if reverse else 0, GLA_SUB):
        if lag == 0:
            t = q * k
        else:
            shift = (GLA_SUB - lag) if reverse else lag
            ks = pltpu.roll(k.reshape(C // GLA_SUB, GLA_SUB, wk), shift, axis=1).reshape(C, wk)
            cs = pltpu.roll(c.reshape(C // GLA_SUB, GLA_SUB, wk), shift, axis=1).reshape(C, wk)
            t = q * ks * jnp.exp2(jnp.minimum(c - cs, 0.0))
        red = jnp.dot(t.astype(BF16), expand, preferred_element_type=F32)
        on_diag = same_sub & (dist == lag)
        for h in range(HEADS):
            scores[h] = scores[h] + jnp.where(on_diag, red[:, h * dv:(h + 1) * dv], 0.0)

    vb = v.astype(BF16)
    o = o + jnp.concatenate(
        [jnp.dot(scores[h].astype(BF16), vb[:, h * dv:(h + 1) * dv], preferred_element_type=F32) for h in range(HEADS)],
        axis=1)
    return o, new_st


def _gla_kernel(qf, kf, vf, lf, qb, kb, vb, lb, w2f, bf, w2b, bb, of_ref, ob_ref, s_sc, *, qscale):
    @pl.when(pl.program_id(1) == 0)
    def _():
        s_sc[...] = jnp.zeros(s_sc.shape, F32)

    per = qf.shape[0] // GLA_CHUNK
    st_f, st_b = s_sc[0], s_sc[1]
    for s in range(per):
        rf = pl.ds(s * GLA_CHUNK, GLA_CHUNK)
        o, st_f = _gla_direction(qf[rf, :] * qscale, kf[rf, :], vf[rf, :], lf[rf, :], w2f[...], bf[...], st_f, False)
        of_ref[rf, :] = o.astype(of_ref.dtype)
        rb = pl.ds((per - 1 - s) * GLA_CHUNK, GLA_CHUNK)
        o, st_b = _gla_direction(qb[rb, :] * qscale, kb[rb, :], vb[rb, :], lb[rb, :], w2b[...], bb[...], st_b, True)
        ob_ref[rb, :] = o.astype(ob_ref.dtype)
    s_sc[0] = st_f
    s_sc[1] = st_b


GLA_CHUNK = 128
GLA_CHUNKS_PER_STEP = 4


def _gla(z, *, qcol, kcol, vcol, lcol, w2f, bf, w2b, bb, batch, seq):
    chunk = GLA_CHUNK * GLA_CHUNKS_PER_STEP
    assert seq % chunk == 0
    n = seq // chunk
    wk, wv = HEADS * GLA_K_DIM, HEADS * LANE

    def specs(cmap):
        return [pl.BlockSpec((chunk, wk), lambda b, c: (cmap(b, c), qcol)),
                pl.BlockSpec((chunk, wk), lambda b, c: (cmap(b, c), kcol)),
                pl.BlockSpec((chunk, wv), lambda b, c: (cmap(b, c), vcol)),
                pl.BlockSpec((chunk, LANE), lambda b, c: (cmap(b, c), lcol))]

    fmap = lambda b, c: b * n + c
    bmap = lambda b, c: b * n + n - 1 - c
    wspec = [pl.BlockSpec((LANE, wk), lambda b, c: (0, 0)), pl.BlockSpec((1, wk), lambda b, c: (0, 0))]
    out = jax.ShapeDtypeStruct((batch * seq, wv), BF16)
    return pl.pallas_call(
        functools.partial(_gla_kernel, qscale=GLA_K_DIM ** -0.5),
        grid=(batch, n),
        in_specs=specs(fmap) + specs(bmap) + wspec + wspec,
        out_specs=[pl.BlockSpec((chunk, wv), lambda b, c: (fmap(b, c), 0)),
                   pl.BlockSpec((chunk, wv), lambda b, c: (bmap(b, c), 0))],
        out_shape=[out, out],
        scratch_shapes=[pltpu.VMEM((2, wv, wk), F32)],
        compiler_params=_cparams(("parallel", "arbitrary")),
        name="gla",
    )(z, z, z, z, z, z, z, z, w2f, bf, w2b, bb)


def _layer_norm_rows(r, g, b):
    mu = jnp.mean(r, axis=-1, keepdims=True)
    d = r - mu
    var = jnp.mean(d * d, axis=-1, keepdims=True)
    return d * lax.rsqrt(var + LN_EPS) * g + b


def _outproj_kernel(x_ref, fin_ref, of_ref, ob_ref, gate_ref, ng_ref, wa_ref, wb_ref, lg_ref, lb_ref, rw_ref, rb_ref,
                    o_ref, ids_ref, gw_ref, cnt_ref, run_sc, *, group_norm):
    lin = of_ref[...].astype(F32) + ob_ref[...].astype(F32)
    parts = []
    for h in range(HEADS):
        zh = lin[:, h * LANE:(h + 1) * LANE]
        if group_norm:
            mu = jnp.mean(zh, axis=-1, keepdims=True)
            dz = zh - mu
            parts.append(dz * lax.rsqrt(jnp.mean(dz * dz, axis=-1, keepdims=True) + LN_EPS))
        else:
            parts.append(zh * lax.rsqrt(jnp.mean(zh * zh, axis=-1, keepdims=True) + RMS_EPS) * ng_ref[...])
    gate = gate_ref[...]
    lin = jnp.concatenate(parts, axis=1) * (gate * jax.nn.sigmoid(gate))
    y = (jnp.dot(fin_ref[...].astype(BF16), wa_ref[...], preferred_element_type=F32)
         + jnp.dot(lin.astype(BF16), wb_ref[...], preferred_element_type=F32))
    x1 = _layer_norm_rows(ALPHA * x_ref[...] + y, lg_ref[...], lb_ref[...])
    o_ref[...] = x1
    _route_rows(x1, rw_ref, rb_ref, ids_ref, gw_ref, cnt_ref, run_sc)


def _outproj(x, fin, of, ob, gate_src, gate_col, norm_gain, wa, wb, ln_g, ln_b, route_w, route_b, *, group_norm, tm=512):
    n, d = x.shape
    w = HEADS * LANE
    tm = min(tm, n)
    row = lambda i: (i, 0)
    const = lambda i: (0, 0)
    rw_hi = route_w.astype(BF16)
    rw = jnp.stack([rw_hi, (route_w - rw_hi.astype(F32)).astype(BF16)])
    return pl.pallas_call(
        functools.partial(_outproj_kernel, group_norm=group_norm),
        grid=(n // tm,),
        in_specs=[pl.BlockSpec((tm, d), row), pl.BlockSpec((tm, w), row), pl.BlockSpec((tm, w), row),
                  pl.BlockSpec((tm, w), row), pl.BlockSpec((tm, w), lambda i: (i, gate_col)),
                  pl.BlockSpec((1, LANE), const), pl.BlockSpec((w, d), const), pl.BlockSpec((w, d), const),
                  pl.BlockSpec((1, d), const), pl.BlockSpec((1, d), const),
                  pl.BlockSpec((2, d, LANE), lambda i: (0, 0, 0)), pl.BlockSpec((1, LANE), const)],
        out_specs=[pl.BlockSpec((tm, d), row), pl.BlockSpec((tm, LANE), row), pl.BlockSpec((tm, LANE), row),
                   pl.BlockSpec((1, LANE), const)],
        out_shape=[jax.ShapeDtypeStruct((n, d), F32), jax.ShapeDtypeStruct((n, LANE), jnp.int32),
                   jax.ShapeDtypeStruct((n, LANE), F32), jax.ShapeDtypeStruct((1, LANE), jnp.int32)],
        scratch_shapes=[pltpu.VMEM((1, LANE), F32)],
        compiler_params=_cparams(("arbitrary",)),
        name="outproj",
    )(x, fin, of, ob, gate_src, norm_gain.reshape(1, LANE), wa, wb, ln_g.reshape(1, d), ln_b.reshape(1, d), rw, route_b)


def _route_rows(x, w_ref, b_ref, ids_ref, gw_ref, cnt_ref, run_sc):
    @pl.when(pl.program_id(0) == 0)
    def _():
        run_sc[...] = jnp.zeros(run_sc.shape, F32)

    tm = x.shape[0]
    xh = x.astype(BF16)
    xl = (x - xh.astype(F32)).astype(BF16)
    wh, wl = w_ref[0], w_ref[1]
    logits = (jnp.dot(xh, wh, preferred_element_type=F32) + jnp.dot(xh, wl, preferred_element_type=F32)
              + jnp.dot(xl, wh, preferred_element_type=F32)) + b_ref[...]
    lane = lax.broadcasted_iota(jnp.int32, logits.shape, 1)
    neg = -jnp.inf
    gmask = (lane >= N_EXPERTS) & (lane < N_EXPERTS + N_GROUPS)
    gl = jnp.where(gmask, logits, neg)
    gmax = jnp.max(gl, axis=1, keepdims=True)
    lane_f = lane.astype(F32)
    first = lambda hit: jnp.min(jnp.where(hit, lane_f, float(LANE)), axis=1, keepdims=True).astype(jnp.int32)
    gidx = first(gl == gmax) - N_EXPERTS
    p_grp = 1.0 / jnp.sum(jnp.where(gmask, jnp.exp(gl - gmax), 0.0), axis=1, keepdims=True)
    el = jnp.where(_div_pow2(lane, EXPERTS_PER_GROUP) == gidx, logits, neg)
    l1 = jnp.max(el, axis=1, keepdims=True)
    e1 = first(el == l1)
    el2 = jnp.where(lane == e1, neg, el)
    l2 = jnp.max(el2, axis=1, keepdims=True)
    e2 = first(el2 == l2)
    t = jnp.exp(l2 - l1)
    w1 = p_grp / (1.0 + t)
    w2 = p_grp * t / (1.0 + t)

    onehot = jnp.where(lane == e1, 1.0, jnp.where(lane == e2, 1.0, 0.0))
    ri = lax.broadcasted_iota(jnp.int32, (tm, tm), 0)
    ci = lax.broadcasted_iota(jnp.int32, (tm, tm), 1)
    before = jnp.dot(jnp.where(ri > ci, 1.0, 0.0).astype(BF16), onehot.astype(BF16), preferred_element_type=F32)
    before = before + run_sc[...]
    r1 = jnp.sum(jnp.where(lane == e1, before, 0.0), axis=1, keepdims=True).astype(jnp.int32)
    r2 = jnp.sum(jnp.where(lane == e2, before, 0.0), axis=1, keepdims=True).astype(jnp.int32)
    run_sc[...] = run_sc[...] + jnp.sum(onehot, axis=0, keepdims=True)
    cnt_ref[...] = run_sc[...].astype(jnp.int32)
    ids_ref[...] = jnp.where(lane == 0, e1, jnp.where(lane == 1, e2, jnp.where(lane == 2, r1, jnp.where(lane == 3, r2, 0))))
    gw_ref[...] = jnp.where(lane == 0, w1, jnp.where(lane == 1, w2, 0.0))


MOE_BM = 256
SUBL = 8


def _rows_from_linear(ref, rows):
    return jnp.concatenate([ref[pl.ds(s, rows, stride=SUBL), :] for s in range(SUBL)], axis=1)


def _rows_to_linear(ref, val):
    for s in range(SUBL):
        ref[pl.ds(s, val.shape[0], stride=SUBL), :] = val[:, s * LANE:(s + 1) * LANE]


def _dispatch_kernel(pend_ref, padded_ref, dest_ref, x_ref, xs_hbm, idx_smem, zbuf, lin, sem_i, sem_z, sem):
    i = pl.program_id(0)
    tm = x_ref.shape[0]
    bm = zbuf.shape[0] // SUBL

    @pl.when(i == 0)
    def _():
        zbuf[...] = jnp.zeros(zbuf.shape, F32)

        def tail(e):
            start_row = pl.multiple_of((pend_ref[e] - bm) * SUBL, bm * SUBL)
            return pltpu.make_async_copy(zbuf, xs_hbm.at[pl.ds(start_row, bm * SUBL), :], sem_z)

        def start(e, carry):
            @pl.when(padded_ref[e] > 0)
            def _():
                tail(e).start()
            return carry

        def wait(e, carry):
            @pl.when(padded_ref[e] > 0)
            def _():
                tail(e).wait()
            return carry

        lax.fori_loop(0, N_EXPERTS, start, 0)
        lax.fori_loop(0, N_EXPERTS, wait, 0)

        def unused(b):
            start_row = pl.multiple_of(b * bm * SUBL, bm * SUBL)
            return pltpu.make_async_copy(zbuf, xs_hbm.at[pl.ds(start_row, bm * SUBL), :], sem_z)

        first_unused = pend_ref[N_EXPERTS - 1] // bm
        n_blocks = xs_hbm.shape[0] // (bm * SUBL)
        lax.fori_loop(first_unused, n_blocks, lambda b, c: (unused(b).start(), c)[1], 0)
        lax.fori_loop(first_unused, n_blocks, lambda b, c: (unused(b).wait(), c)[1], 0)

    cp = pltpu.make_async_copy(dest_ref.at[i], idx_smem, sem_i)
    cp.start()
    _rows_to_linear(lin, x_ref[...])
    cp.wait()

    def scatter(r, carry):
        src = lin.at[pl.ds(pl.multiple_of(r * SUBL, SUBL), SUBL), :]
        for k in range(2):
            dst = pl.multiple_of(idx_smem[2 * r + k] * SUBL, SUBL)
            pltpu.make_async_copy(src, xs_hbm.at[pl.ds(dst, SUBL), :], sem).start(priority=k)
        return carry

    lax.fori_loop(0, tm, scatter, 0, unroll=8)
    for k in range(2):
        pltpu.make_async_copy(lin, xs_hbm.at[pl.ds(0, tm * SUBL), :], sem).wait()


def _dispatch(x, dest, pend, padded, cap, *, tm=512):
    n, d = x.shape
    assert d == SUBL * LANE
    tm = min(tm, n)
    nt = n // tm
    grid_spec = pltpu.PrefetchScalarGridSpec(
        num_scalar_prefetch=2,
        grid=(nt,),
        in_specs=[pl.BlockSpec((nt, 2 * tm), lambda i, pe, pa: (0, 0)),
                  pl.BlockSpec((tm, d), lambda i, pe, pa: (i, 0))],
        out_specs=pl.BlockSpec(memory_space=pl.ANY),
        scratch_shapes=[pltpu.SMEM((2 * tm,), jnp.int32), pltpu.VMEM((MOE_BM * SUBL, LANE), F32),
                        pltpu.VMEM((tm * SUBL, LANE), F32),
                        pltpu.SemaphoreType.DMA(()), pltpu.SemaphoreType.DMA(()), pltpu.SemaphoreType.DMA(())],
    )
    return pl.pallas_call(
        _dispatch_kernel,
        grid_spec=grid_spec,
        out_shape=jax.ShapeDtypeStruct((cap * SUBL, LANE), F32),
        compiler_params=_cparams(("arbitrary",)),
        name="dispatch",
    )(pend, padded, dest.reshape(nt, 2 * tm), x)


def _experts_kernel(blk_e_ref, nused_ref, xs_ref, wg_ref, wu_ref, wd_ref, ys_ref, wgb, wub, wdb, hbuf):
    i = pl.program_id(0)
    nb = pl.num_programs(0) - 1
    cur = jnp.minimum(i, nb - 1)
    prv = jnp.maximum(i - 1, 0)

    @pl.when(i == 0)
    def _():
        hbuf[...] = jnp.zeros(hbuf.shape, BF16)

    @pl.when(jnp.logical_or(i == 0, blk_e_ref[cur] != blk_e_ref[jnp.minimum(prv, nb - 1)]))
    def _():
        wgb[...] = wg_ref[...].astype(BF16)
        wub[...] = wu_ref[...].astype(BF16)

    @pl.when(jnp.logical_or(i == 0, blk_e_ref[prv] != blk_e_ref[jnp.maximum(i - 2, 0)]))
    def _():
        wdb[...] = wd_ref[...].astype(BF16)

    slot = lax.rem(i, 2)
    y = jnp.dot(hbuf[1 - slot], wdb[...], preferred_element_type=F32)
    live = jnp.logical_and(i >= 1, i - 1 < nused_ref[0])
    _rows_to_linear(ys_ref, jnp.where(live, y, 0.0))
    xb = _rows_from_linear(xs_ref, xs_ref.shape[0] // SUBL).astype(BF16)
    hg = jnp.dot(xb, wgb[...], preferred_element_type=F32)
    hu = jnp.dot(xb, wub[...], preferred_element_type=F32)
    hbuf[slot] = (hg * jax.nn.sigmoid(hg) * hu).astype(BF16)


def _experts(xs, blk_e, n_used, w_gate, w_up, w_down, layer):
    cap = xs.shape[0] // SUBL
    bm = MOE_BM
    nb = cap // bm
    d, de = w_gate.shape[2], w_gate.shape[3]
    row_in = lambda i, be, nu: (jnp.minimum(i, nu[0] - 1), 0)
    row = lambda i, be, nu: (jnp.maximum(i - 1, 0), 0)
    w_cur = lambda i, be, nu: (layer, be[jnp.minimum(i, nb - 1)], 0, 0)
    w_prv = lambda i, be, nu: (layer, be[jnp.maximum(i - 1, 0)], 0, 0)
    grid_spec = pltpu.PrefetchScalarGridSpec(
        num_scalar_prefetch=2,
        grid=(nb + 1,),
        in_specs=[pl.BlockSpec((bm * SUBL, LANE), row_in),
                  pl.BlockSpec((None, None, d, de), w_cur),
                  pl.BlockSpec((None, None, d, de), w_cur),
                  pl.BlockSpec((None, None, de, d), w_prv)],
        out_specs=pl.BlockSpec((bm * SUBL, LANE), row),
        scratch_shapes=[pltpu.VMEM((d, de), BF16), pltpu.VMEM((d, de), BF16), pltpu.VMEM((de, d), BF16),
                        pltpu.VMEM((2, bm, de), BF16)],
    )
    return pl.pallas_call(
        _experts_kernel,
        grid_spec=grid_spec,
        out_shape=jax.ShapeDtypeStruct((cap * SUBL, LANE), F32),
        compiler_params=_cparams(("arbitrary",)),
        name="experts",
    )(blk_e, n_used, xs, w_gate, w_up, w_down)


def _combine_kernel(dest_ref, x_ref, gw_ref, g_ref, b_ref, ys_hbm, o_ref, idx_smem, ybuf, sem_i, sem):
    i = pl.program_id(0)
    tm = x_ref.shape[0]

    def issue(tile, slot):
        cp = pltpu.make_async_copy(dest_ref.at[tile], idx_smem, sem_i)
        cp.start()
        cp.wait()

        def gather(r, carry):
            row = pl.multiple_of(r * SUBL, SUBL)
            for k in range(2):
                src = pl.multiple_of(idx_smem[2 * r + k] * SUBL, SUBL)
                pltpu.make_async_copy(ys_hbm.at[pl.ds(src, SUBL), :], ybuf.at[slot, k, pl.ds(row, SUBL), :],
                                      sem.at[slot]).start(priority=k)
            return carry

        lax.fori_loop(0, tm, gather, 0, unroll=8)

    slot = lax.rem(i, 2)

    @pl.when(i == 0)
    def _():
        issue(0, 0)

    @pl.when(i + 1 < pl.num_programs(0))
    def _():
        issue(i + 1, 1 - slot)

    for k in range(2):
        pltpu.make_async_copy(ys_hbm.at[pl.ds(0, tm * SUBL), :], ybuf.at[slot, k], sem.at[slot]).wait()
    gw = gw_ref[...]
    ffn = (_rows_from_linear(ybuf.at[slot, 0], tm) * gw[:, 0:1] + _rows_from_linear(ybuf.at[slot, 1], tm) * gw[:, 1:2])
    o_ref[...] = _layer_norm_rows(ALPHA * x_ref[...] + ffn, g_ref[...], b_ref[...])


def _combine(x, ys, dest, gw, ln_g, ln_b, *, tm=512):
    n, d = x.shape
    tm = min(tm, n)
    nt = n // tm
    return pl.pallas_call(
        _combine_kernel,
        grid=(nt,),
        in_specs=[pl.BlockSpec((nt, 2 * tm), lambda i: (0, 0)),
                  pl.BlockSpec((tm, d), lambda i: (i, 0)), pl.BlockSpec((tm, LANE), lambda i: (i, 0)),
                  pl.BlockSpec((1, d), lambda i: (0, 0)), pl.BlockSpec((1, d), lambda i: (0, 0)),
                  pl.BlockSpec(memory_space=pl.ANY)],
        out_specs=pl.BlockSpec((tm, d), lambda i: (i, 0)),
        out_shape=jax.ShapeDtypeStruct((n, d), F32),
        scratch_shapes=[pltpu.SMEM((2 * tm,), jnp.int32), pltpu.VMEM((2, 2, tm * SUBL, LANE), F32),
                        pltpu.SemaphoreType.DMA(()), pltpu.SemaphoreType.DMA((2,))],
        compiler_params=_cparams(("arbitrary",)),
        name="combine",
    )(dest.reshape(nt, 2 * tm), x, gw, ln_g.reshape(1, d), ln_b.reshape(1, d), ys)


def _router_params(w_grp, b_grp, w_exp, b_exp):
    d = w_exp.shape[0]
    wr = jnp.zeros((d, LANE), F32).at[:, :N_EXPERTS].set(w_exp).at[:, N_EXPERTS:N_EXPERTS + N_GROUPS].set(w_grp)
    br = jnp.zeros((1, LANE), F32).at[0, :N_EXPERTS].set(b_exp).at[0, N_EXPERTS:N_EXPERTS + N_GROUPS].set(b_grp)
    return wr, br


def _moe(x, routing, w_gate, w_up, w_down, layer, ln_g, ln_b):
    n, d = x.shape
    ids, gw, cnt = routing
    bm = MOE_BM
    counts = cnt[0, :N_EXPERTS]
    padded = (counts + bm - 1) // bm * bm
    pend = jnp.cumsum(padded)
    pstart = pend - padded
    e, r = ids[:, 0:2], ids[:, 2:4]
    onehot = e[:, :, None] == jnp.arange(N_EXPERTS, dtype=jnp.int32)[None, None, :]
    dest = jnp.sum(jnp.where(onehot, pstart[None, None, :], 0), axis=-1) + r
    cap = 2 * n + N_EXPERTS * bm
    nb = cap // bm
    blk_start = jnp.arange(nb, dtype=jnp.int32) * bm
    blk_e = jnp.minimum(jnp.sum((pend[None, :] <= blk_start[:, None]).astype(jnp.int32), axis=1), N_EXPERTS - 1)
    n_used = (pend[-1:] // bm).astype(jnp.int32)
    xs = _dispatch(x, dest, pend.astype(jnp.int32), padded.astype(jnp.int32), cap)
    ys = _experts(xs, blk_e, n_used, w_gate, w_up, w_down, layer)
    return _combine(x, ys, dest, gw, ln_g, ln_b)


def _even_layer(x, batch, seq, layer_idx, w_in, dec_f, dec_b, lq1, lk1, lq2, lk2, subln, w_out, ln_g, ln_b, route):
    d = x.shape[1]
    w = HEADS * LANE
    kw = dict(batch=batch, seq=seq)
    diff_seg = [(0, DIFF_ROT_DIM), (DIFF_HEAD_DIM, DIFF_ROT_DIM)]
    q, k, v, gate, dq, dk, dvt = _even_in(
        x, w_in.astype(BF16), _rope_tables(seq, [(0, LANE)], RET_THETA),
        _rope_tables(seq, diff_seg, ROPE_THETA), **kw)
    decays = jnp.stack([dec_f, dec_b]).astype(F32)
    of, ob = _retention(q, k, v, decays, **kw)
    lam_init = 0.8 - 0.6 * math.exp(-0.3 * layer_idx)
    diff = _flash(dq, dk, dvt, diff=(lq1, lk1, lq2, lk2, subln), lam_init=lam_init, tq=2048, passes=2, **kw)
    wo = w_out.astype(BF16)
    return _outproj(x, diff, of, ob, gate, 0, jnp.ones((LANE,), F32), wo[w:], wo[:w], ln_g, ln_b, *route,
                    group_norm=True)


def _odd_layer(x, batch, seq, w_in, q_norm, w_uq, kv_norm, w_ukv, w2_f, b_f, w2_b, b_b, gla_norm, w_out, ln_g, ln_b,
               route):
    d = x.shape[1]
    w = HEADS * LANE
    o = np.cumsum([0, MLA_Q_RANK, MLA_KV_RANK, MLA_ROPE, HEADS * GLA_K_DIM, HEADS * GLA_K_DIM, w, w,
                   GLA_GATE_RANK, GLA_GATE_RANK]).tolist()
    zeros = lambda c: jnp.zeros((d, c), F32)
    w_in2 = jnp.concatenate([
        w_in[:, o[0]:o[2]], zeros(MLA_NOPE), w_in[:, o[2]:o[3]], zeros(LANE - MLA_NOPE - MLA_ROPE),
        w_in[:, o[3]:o[7]], w_in[:, o[7]:o[9]], zeros(LANE - 2 * GLA_GATE_RANK)], axis=1).astype(BF16)
    kw = dict(batch=batch, seq=seq)
    qd = MLA_NOPE + MLA_ROPE
    w_uq2 = jnp.pad(w_uq.reshape(MLA_Q_RANK, HEADS, qd), ((0, 0), (0, 0), (0, LANE - qd))).reshape(MLA_Q_RANK, w)
    ukv = w_ukv.reshape(MLA_KV_RANK, HEADS, MLA_NOPE + MLA_V)
    w_uk2 = jnp.pad(ukv[:, :, :MLA_NOPE], ((0, 0), (0, 0), (0, LANE - MLA_NOPE))).reshape(MLA_KV_RANK, w)
    w_uv2 = ukv[:, :, MLA_NOPE:].reshape(MLA_KV_RANK, w)
    q, k, vt, zg = _odd_in(x, w_in2, q_norm, w_uq2.astype(BF16), kv_norm,
                           jnp.concatenate([w_uk2, w_uv2], axis=1).astype(BF16),
                           _rope_tables(seq, [(MLA_NOPE, MLA_ROPE)], ROPE_THETA), **kw)
    mla = _flash(q, k, vt, passes=4, **kw)
    wk = HEADS * GLA_K_DIM
    pad_rows = lambda m, r0: jnp.zeros((LANE, wk), F32).at[r0:r0 + GLA_GATE_RANK].set(m).astype(BF16)
    of, ob = _gla(zg, qcol=0, kcol=1, vcol=1, lcol=12,
                  w2f=pad_rows(w2_f, 0), bf=b_f.reshape(1, wk), w2b=pad_rows(w2_b, GLA_GATE_RANK), bb=b_b.reshape(1, wk), **kw)
    wo = w_out.astype(BF16)
    return _outproj(x, mla, of, ob, zg, 2, gla_norm, wo[:w], wo[w:], ln_g, ln_b, *route, group_norm=False)


def kernel(x, ev_w_in, ev_ret_decay_f, ev_ret_decay_b, ev_lq1, ev_lk1, ev_lq2, ev_lk2, ev_subln, ev_w_out, od_w_in, od_q_norm, od_w_uq, od_kv_norm, od_w_ukv, od_gla_w2_f, od_gla_b_f, od_gla_w2_b, od_gla_b_b, od_gla_norm, od_w_out, ln1_g, ln1_b, ln2_g, ln2_b, moe_w_grp, moe_b_grp, moe_w_exp, moe_b_exp, moe_w_gate, moe_w_up, moe_w_down):
    batch, seq, d = x.shape
    h = x.reshape(batch * seq, d)
    for i in range(DEPTH):
        j = i // 2
        route = _router_params(moe_w_grp[i], moe_b_grp[i], moe_w_exp[i], moe_b_exp[i])
        if i % 2 == 0:
            h, *routing = _even_layer(h, batch, seq, i, ev_w_in[j], ev_ret_decay_f[j], ev_ret_decay_b[j], ev_lq1[j],
                                      ev_lk1[j], ev_lq2[j], ev_lk2[j], ev_subln[j], ev_w_out[j], ln1_g[i], ln1_b[i], route)
        else:
            h, *routing = _odd_layer(h, batch, seq, od_w_in[j], od_q_norm[j], od_w_uq[j], od_kv_norm[j], od_w_ukv[j],
                                     od_gla_w2_f[j], od_gla_b_f[j], od_gla_w2_b[j], od_gla_b_b[j], od_gla_norm[j],
                                     od_w_out[j], ln1_g[i], ln1_b[i], route)
        h = _moe(h, routing, moe_w_gate, moe_w_up, moe_w_down, i, ln2_g[i], ln2_b[i])
    return h.reshape(batch, seq, d)
```

```python
import functools
import math

import numpy as np
import jax
import jax.numpy as jnp
from jax import lax
from jax.experimental import pallas as pl
from jax.experimental.pallas import tpu as pltpu

F32 = jnp.float32
BF16 = jnp.bfloat16

HEADS = 4
LANE = 128
RET_THETA = 10000.0
ROPE_THETA = 500000.0
DIFF_HEAD_DIM = 64
DIFF_ROT_DIM = 16
MLA_Q_RANK = 256
MLA_KV_RANK = 128
MLA_NOPE = 64
MLA_ROPE = 32
MLA_V = 128
GLA_K_DIM = 64
GLA_GATE_RANK = 16
GLA_TAU = 16.0
N_GROUPS = 4
EXPERTS_PER_GROUP = 8
N_EXPERTS = N_GROUPS * EXPERTS_PER_GROUP
DEPTH = 2
ALPHA = (2.0 * DEPTH) ** 0.25
LN_EPS = 1e-5
RMS_EPS = 1e-6

VMEM_LIMIT = 48 * 1024 * 1024


def _div_pow2(x, n):
    return lax.shift_right_logical(x, int(n).bit_length() - 1)


def _mod_pow2(x, n):
    return lax.bitwise_and(x, int(n) - 1)


def _cparams(sem):
    return pltpu.CompilerParams(dimension_semantics=sem, vmem_limit_bytes=VMEM_LIMIT)


def _rope_heads(z, tabs, sh, scale):
    c, sa, sb = tabs
    outs = []
    for h in range(HEADS):
        zh = z[:, h * LANE:(h + 1) * LANE]
        outs.append((zh * c + pltpu.roll(zh, sh, axis=1) * sa + pltpu.roll(zh, LANE - sh, axis=1) * sb) * scale)
    return outs


def _even_in_kernel(x_ref, w_ref, rc, rsa, rsb, dc, dsa, dsb, q_ref, k_ref, v_ref, g_ref, dq_ref, dk_ref, dvt_ref):
    w = HEADS * LANE
    xb = x_ref[...].astype(BF16)
    part = lambda t: jnp.dot(xb, w_ref[:, t * w:(t + 1) * w], preferred_element_type=F32)
    ret_t = (rc[...], rsa[...], rsb[...])
    diff_t = (dc[...], dsa[...], dsb[...])
    for h, o in enumerate(_rope_heads(part(0), ret_t, LANE // 2, 1.0)):
        q_ref[h] = o.astype(BF16)
    for h, o in enumerate(_rope_heads(part(1), ret_t, LANE // 2, LANE ** -0.5)):
        k_ref[h] = o.astype(BF16)
    rv = part(2)
    for h in range(HEADS):
        v_ref[h] = rv[:, h * LANE:(h + 1) * LANE].astype(BF16)
    g_ref[...] = part(3)
    for h, o in enumerate(_rope_heads(part(4), diff_t, DIFF_ROT_DIM // 2, DIFF_HEAD_DIM ** -0.5 * LOG2E)):
        dq_ref[h] = o.T.astype(BF16)
    for h, o in enumerate(_rope_heads(part(5), diff_t, DIFF_ROT_DIM // 2, 1.0)):
        dk_ref[h] = o.astype(BF16)
    dv = part(6)
    for h in range(HEADS):
        dvt_ref[h, :LANE, :] = dv[:, h * LANE:(h + 1) * LANE].T.astype(BF16)
        dvt_ref[h, LANE:, :] = jnp.ones((ONES_ROWS, dvt_ref.shape[2]), BF16)


def _even_in(x, w, ret_tabs, diff_tabs, *, batch, seq):
    n, d = x.shape
    tm = min(FLASH_TK_DIFF, seq // 2)
    nt = seq // tm
    hw = HEADS * LANE
    heads = jax.ShapeDtypeStruct((batch, HEADS, seq, LANE), BF16)
    head_spec = pl.BlockSpec((None, HEADS, tm, LANE), lambda i: (i // nt, 0, i % nt, 0))
    heads_t = jax.ShapeDtypeStruct((batch, HEADS, LANE, seq), BF16)
    head_t_spec = pl.BlockSpec((None, HEADS, LANE, tm), lambda i: (i // nt, 0, 0, i % nt))
    tab_spec = pl.BlockSpec((tm, LANE), lambda i: (i % nt, 0))
    return pl.pallas_call(
        _even_in_kernel,
        grid=(n // tm,),
        in_specs=[pl.BlockSpec((tm, d), lambda i: (i, 0)), pl.BlockSpec((d, 7 * hw), lambda i: (0, 0))] + [tab_spec] * 6,
        out_specs=[head_spec, head_spec, head_spec, pl.BlockSpec((tm, hw), lambda i: (i, 0)), head_t_spec, head_spec,
                   pl.BlockSpec((None, HEADS, None, LANE + ONES_ROWS, tm), lambda i: (i // nt, 0, i % nt, 0, 0))],
        out_shape=[heads, heads, heads, jax.ShapeDtypeStruct((n, hw), F32), heads_t, heads,
                   jax.ShapeDtypeStruct((batch, HEADS, nt, LANE + ONES_ROWS, tm), BF16)],
        compiler_params=_cparams(("parallel",)),
        name="even_in",
    )(x, w, *ret_tabs, *diff_tabs)


def _rms_rows(z, g):
    return z * lax.rsqrt(jnp.mean(z * z, axis=-1, keepdims=True) + RMS_EPS) * g


def _odd_in_kernel(x_ref, w_ref, qn_ref, wq_ref, kvn_ref, wkv_ref, tc, tsa, tsb, q_ref, k_ref, vt_ref, zg_ref):
    hw = HEADS * LANE
    mla_w = MLA_Q_RANK + MLA_KV_RANK + LANE
    xb = x_ref[...].astype(BF16)
    zg_ref[...] = jnp.dot(xb, w_ref[:, mla_w:], preferred_element_type=F32)
    z1 = jnp.dot(xb, w_ref[:, :mla_w], preferred_element_type=F32)
    tabs = (tc[...], tsa[...], tsb[...])
    sh = MLA_ROPE // 2
    qh = jnp.dot(_rms_rows(z1[:, :MLA_Q_RANK], qn_ref[...]).astype(BF16), wq_ref[...], preferred_element_type=F32)
    for h, o in enumerate(_rope_heads(qh, tabs, sh, (MLA_NOPE + MLA_ROPE) ** -0.5 * LOG2E)):
        q_ref[h] = o.T.astype(BF16)
    ckv = _rms_rows(z1[:, MLA_Q_RANK:MLA_Q_RANK + MLA_KV_RANK], kvn_ref[...]).astype(BF16)
    kv = jnp.dot(ckv, wkv_ref[...], preferred_element_type=F32)
    kr = z1[:, MLA_Q_RANK + MLA_KV_RANK:]
    kr = kr * tabs[0] + pltpu.roll(kr, sh, axis=1) * tabs[1] + pltpu.roll(kr, LANE - sh, axis=1) * tabs[2]
    for h in range(HEADS):
        k_ref[h] = (kv[:, h * LANE:(h + 1) * LANE] + kr).astype(BF16)
        vt_ref[h, :LANE, :] = kv[:, hw + h * LANE:hw + (h + 1) * LANE].T.astype(BF16)
        vt_ref[h, LANE:, :] = jnp.ones((ONES_ROWS, vt_ref.shape[2]), BF16)


def _odd_in(x, w, q_norm, w_uq, kv_norm, w_ukv, tabs, *, batch, seq, tm=512):
    n, d = x.shape
    tm = min(tm, seq // 2)
    nt = seq // tm
    tk = min(FLASH_TK, seq // 2)
    per = tk // tm
    hw = HEADS * LANE
    gw_ = w.shape[1] - (MLA_Q_RANK + MLA_KV_RANK + LANE)
    heads = jax.ShapeDtypeStruct((batch, HEADS, seq, LANE), BF16)
    head_spec = pl.BlockSpec((None, HEADS, tm, LANE), lambda i: (i // nt, 0, i % nt, 0))
    tab_spec = pl.BlockSpec((tm, LANE), lambda i: (i % nt, 0))
    const = lambda i: (0, 0)
    return pl.pallas_call(
        _odd_in_kernel,
        grid=(n // tm,),
        in_specs=[pl.BlockSpec((tm, d), lambda i: (i, 0)), pl.BlockSpec(w.shape, const),
                  pl.BlockSpec((1, MLA_Q_RANK), const), pl.BlockSpec(w_uq.shape, const),
                  pl.BlockSpec((1, MLA_KV_RANK), const), pl.BlockSpec(w_ukv.shape, const)] + [tab_spec] * 3,
        out_specs=[pl.BlockSpec((None, HEADS, LANE, tm), lambda i: (i // nt, 0, 0, i % nt)), head_spec,
                   pl.BlockSpec((None, HEADS, None, LANE + ONES_ROWS, tm),
                                lambda i: (i // nt, 0, (i % nt) // per, 0, (i % nt) % per)),
                   pl.BlockSpec((tm, gw_), lambda i: (i, 0))],
        out_shape=[jax.ShapeDtypeStruct((batch, HEADS, LANE, seq), BF16), heads, jax.ShapeDtypeStruct((batch, HEADS, seq // tk, LANE + ONES_ROWS, tk), BF16),
                   jax.ShapeDtypeStruct((n, gw_), F32)],
        compiler_params=_cparams(("parallel",)),
        name="odd_in",
    )(x, w, q_norm.reshape(1, -1), w_uq, kv_norm.reshape(1, -1), w_ukv, *tabs)


def _rope_tables(seq, segs, theta):
    pos = jnp.arange(seq, dtype=F32)
    inv = jnp.zeros((LANE,), F32)
    lo = np.zeros((LANE,), bool)
    hi = np.zeros((LANE,), bool)
    for start, rot in segs:
        half = rot // 2
        f = jnp.power(jnp.float32(theta), -jnp.arange(0, rot, 2, dtype=F32) / rot)
        inv = inv.at[start:start + half].set(f).at[start + half:start + rot].set(f)
        lo[start:start + half] = True
        hi[start + half:start + rot] = True
    ang = pos[:, None] * inv[None, :]
    cos, sin = jnp.cos(ang), jnp.sin(ang)
    c = jnp.where(lo | hi, cos, 1.0)
    sa = jnp.where(hi, sin, 0.0)
    sb = jnp.where(lo, -sin, 0.0)
    return c, sa, sb


ONES_ROWS = 16
LOG2E = math.log2(math.e)
FLASH_TK = 1024
FLASH_TK_DIFF = 512


def _flash_kernel(*refs, ncomp, nk, lam_init):
    if ncomp == 2:
        q_ref, k_ref, vt_ref, lq1, lk1, lq2, lk2, g_ref, o_ref, *scr = refs
    else:
        q_ref, k_ref, vt_ref, o_ref, *scr = refs
    qm_sc, m_sc, acc_sc, s0, s1, cm0, cm1, p0, p1, al0, al1 = scr
    tk = s0.shape[1]
    tq = qm_sc.shape[2]

    def start(sub):
        q = q_ref[:, sub * tq:(sub + 1) * tq]
        if ncomp == 2:
            chan = lax.broadcasted_iota(jnp.int32, q.shape, 0)
            zero = jnp.zeros_like(q)
            qm_sc[0] = jnp.where(chan < DIFF_HEAD_DIM, q, zero)
            qm_sc[1] = jnp.where(chan >= DIFF_HEAD_DIM, q, zero)
        else:
            qm_sc[0] = q
        m_sc[...] = jnp.full(m_sc.shape, -jnp.inf, F32)
        acc_sc[...] = jnp.zeros(acc_sc.shape, F32)

    def scores(j, s_ref, cm_ref):
        k = k_ref[j * tk:(j + 1) * tk, :]
        for c in range(ncomp):
            s = jnp.dot(k, qm_sc[c], preferred_element_type=F32)
            s_ref[c] = s
            cm_ref[c] = jnp.max(s, axis=0, keepdims=True)

    def softmax(s_ref, cm_ref, p_ref, al_ref):
        for c in range(ncomp):
            m_old = m_sc[c]
            m_new = jnp.maximum(m_old, cm_ref[c])
            al_ref[c] = jnp.exp2(m_old - m_new)
            p_ref[c] = jnp.exp2(s_ref[c] - m_new).astype(BF16)
            m_sc[c] = m_new

    def values(j, p_ref, al_ref):
        vt = vt_ref[j]
        for c in range(ncomp):
            acc_sc[c] = al_ref[c] * acc_sc[c] + jnp.dot(vt, p_ref[c], preferred_element_type=F32)

    def normalised(c):
        acc = acc_sc[c]
        return acc[:LANE] / acc[LANE:LANE + 1]

    bufs = ((s0, cm0, p0, al0), (s1, cm1, p1, al1))
    for sub in range(q_ref.shape[1] // tq):
        start(sub)
        scores(0, s0, cm0)
        for j in range(nk):
            s_c, cm_c, p_c, al_c = bufs[j % 2]
            s_n, cm_n, p_n, al_n = bufs[(j + 1) % 2]
            if j + 1 < nk:
                scores(j + 1, s_n, cm_n)
            softmax(s_c, cm_c, p_c, al_c)
            if j >= 1:
                values(j - 1, p_n, al_n)
        values(nk - 1, *bufs[(nk - 1) % 2][2:])
        o = normalised(0)
        if ncomp == 2:
            lam = (jnp.exp(jnp.sum(lq1[...] * lk1[...], keepdims=True))
                   - jnp.exp(jnp.sum(lq2[...] * lk2[...], keepdims=True)) + lam_init)
            o = o - lam * normalised(1)
            o = o * lax.rsqrt(jnp.mean(o * o, axis=0, keepdims=True) + RMS_EPS) * g_ref[...] * (1.0 - lam_init)
        o_ref[sub * tq:(sub + 1) * tq, :] = o.T.astype(o_ref.dtype)


def _flash(q, k, vt, *, batch, seq, tq=512, passes=1, diff=None, lam_init=0.0):
    nk, vrows, tk = vt.shape[2], vt.shape[3], vt.shape[4]
    assert vrows == LANE + ONES_ROWS
    tq = min(tq, seq)
    passes = min(passes, seq // tq)
    bq = tq * passes
    nq = seq // bq
    ncomp = 2 if diff is not None else 1
    in_specs = [
        pl.BlockSpec((None, None, LANE, bq), lambda b, h, i: (b, h, 0, i)),
        pl.BlockSpec((None, None, seq, LANE), lambda b, h, i: (b, h, 0, 0)),
        pl.BlockSpec((None, None, nk, vrows, tk), lambda b, h, i: (b, h, 0, 0, 0)),
    ]
    args = [q, k, vt]
    if diff is not None:
        lq1, lk1, lq2, lk2, subln = diff
        for v in (lq1, lk1, lq2, lk2):
            in_specs.append(pl.BlockSpec((1, DIFF_HEAD_DIM), lambda b, h, i: (0, 0)))
            args.append(v.reshape(1, DIFF_HEAD_DIM))
        in_specs.append(pl.BlockSpec((LANE, 1), lambda b, h, i: (0, 0)))
        args.append(subln.reshape(LANE, 1))
    return pl.pallas_call(
        functools.partial(_flash_kernel, ncomp=ncomp, nk=nk, lam_init=lam_init),
        grid=(batch, HEADS, nq),
        in_specs=in_specs,
        out_specs=pl.BlockSpec((bq, LANE), lambda b, h, i: (b * nq + i, h)),
        out_shape=jax.ShapeDtypeStruct((batch * seq, HEADS * LANE), BF16),
        scratch_shapes=[pltpu.VMEM((ncomp, LANE, tq), BF16),
                        pltpu.VMEM((ncomp, 1, tq), F32), pltpu.VMEM((ncomp, vrows, tq), F32),
                        pltpu.VMEM((ncomp, tk, tq), F32), pltpu.VMEM((ncomp, tk, tq), F32),
                        pltpu.VMEM((ncomp, 1, tq), F32), pltpu.VMEM((ncomp, 1, tq), F32),
                        pltpu.VMEM((ncomp, tk, tq), BF16), pltpu.VMEM((ncomp, tk, tq), BF16),
                        pltpu.VMEM((ncomp, 1, tq), F32), pltpu.VMEM((ncomp, 1, tq), F32)],
        compiler_params=_cparams(("parallel", "parallel", "parallel")),
        name="flash_diff" if diff is not None else "flash_mla",
    )(*args)


def _ret_kernel(dec_ref, qf, kf, vf, qb, kb, vb, of_ref, ob_ref, s_sc, *, chunk):
    @pl.when(pl.program_id(1) == 0)
    def _():
        s_sc[...] = jnp.zeros(s_sc.shape, F32)

    ii = lax.broadcasted_iota(jnp.int32, (chunk, chunk), 0)
    jj = lax.broadcasted_iota(jnp.int32, (chunk, chunk), 1)
    r = lax.broadcasted_iota(jnp.int32, (chunk, 1), 0).astype(F32)
    for d, (q_ref, k_ref, v_ref, o_ref) in enumerate(((qf, kf, vf, of_ref), (qb, kb, vb, ob_ref))):
        for h in range(HEADS):
            la = -jnp.exp(jnp.full((1, 1), dec_ref[d, h], F32))
            if d == 0:
                mask, dist = ii >= jj, (ii - jj).astype(F32)
                qdec, kdec = jnp.exp(la * (r + 1.0)), jnp.exp(la * (chunk - 1.0 - r))
            else:
                mask, dist = jj > ii, (jj - ii).astype(F32)
                qdec, kdec = jnp.exp(la * (chunk - r)), jnp.exp(la * r)
            decay = jnp.where(mask, jnp.exp(jnp.where(mask, dist * la, 0.0)), 0.0)
            state = s_sc[d, h]
            per = q_ref.shape[1] // chunk
            for step in range(per):
                rows = pl.ds((step if d == 0 else per - 1 - step) * chunk, chunk)
                q, k, v = q_ref[h, rows, :], k_ref[h, rows, :], v_ref[h, rows, :]
                s = lax.dot_general(q, k, (((1,), (1,)), ((), ())), preferred_element_type=F32)
                o = jnp.dot((s * decay).astype(BF16), v, preferred_element_type=F32)
                o = o + qdec * jnp.dot(q, state.astype(BF16), preferred_element_type=F32)
                kd = (k.astype(F32) * kdec).astype(BF16)
                state = jnp.exp(la * float(chunk)) * state + lax.dot_general(
                    kd, v, (((0,), (0,)), ((), ())), preferred_element_type=F32)
                o_ref[rows, h * LANE:(h + 1) * LANE] = o.astype(o_ref.dtype)
            s_sc[d, h] = state


RET_CHUNKS_PER_STEP = 4


def _retention(q, k, v, decays, *, batch, seq, chunk=256):
    chunk = min(chunk, seq // RET_CHUNKS_PER_STEP)
    blk = chunk * RET_CHUNKS_PER_STEP
    n = seq // blk
    fwd = pl.BlockSpec((None, HEADS, blk, LANE), lambda b, c: (b, 0, c, 0))
    bwd = pl.BlockSpec((None, HEADS, blk, LANE), lambda b, c: (b, 0, n - 1 - c, 0))
    w = HEADS * LANE
    out = jax.ShapeDtypeStruct((batch * seq, w), BF16)
    return pl.pallas_call(
        functools.partial(_ret_kernel, chunk=chunk),
        grid=(batch, n),
        in_specs=[pl.BlockSpec(memory_space=pltpu.SMEM), fwd, fwd, fwd, bwd, bwd, bwd],
        out_specs=[pl.BlockSpec((blk, w), lambda b, c: (b * n + c, 0)),
                   pl.BlockSpec((blk, w), lambda b, c: (b * n + n - 1 - c, 0))],
        out_shape=[out, out],
        scratch_shapes=[pltpu.VMEM((2, HEADS, LANE, LANE), F32)],
        compiler_params=_cparams(("parallel", "arbitrary")),
        name="retention",
    )(decays, q, k, v, q, k, v)


GLA_SUB = 8


def _split3(x):
    x1 = x.astype(BF16)
    r1 = x - x1.astype(F32)
    x2 = r1.astype(BF16)
    x3 = (r1 - x2.astype(F32)).astype(BF16)
    return x1, x2, x3


def _gla_direction(q, k, v, lr, w2, bias, st, reverse):
    C, wk = q.shape
    wv = v.shape[1]
    dk, dv = wk // HEADS, wv // HEADS
    z = jnp.dot(lr.astype(BF16), w2, preferred_element_type=F32) + bias
    g = (jnp.minimum(z, 0.0) - jnp.log(1.0 + jnp.exp(-jnp.abs(z)))) * (LOG2E / GLA_TAU)
    ii = lax.broadcasted_iota(jnp.int32, (C, C), 0)
    jj = lax.broadcasted_iota(jnp.int32, (C, C), 1)
    tri = jnp.where(ii >= jj, 1.0, 0.0).astype(BF16)
    b = sum(jnp.dot(tri, part, preferred_element_type=F32) for part in _split3(g))
    tot = b[C - 1:C, :]
    c = (tot - b + g) if reverse else b

    qe = (q * jnp.exp2(jnp.minimum(c, 0.0))).astype(BF16)
    o = lax.dot_general(qe, st.astype(BF16), (((1,), (1,)), ((), ())), preferred_element_type=F32)
    ke = (k * jnp.exp2(jnp.minimum(tot - c, 0.0))).astype(BF16)
    upd = lax.dot_general(v.astype(BF16), ke, (((0,), (0,)), ((), ())), preferred_element_type=F32)
    rr = _div_pow2(lax.broadcasted_iota(jnp.int32, (wv, wk), 0), dv)
    cc = _div_pow2(lax.broadcasted_iota(jnp.int32, (wv, wk), 1), dk)
    new_st = jnp.where(rr == cc, st * jnp.exp2(tot) + upd, 0.0)

    lane_head = _div_pow2(lax.broadcasted_iota(jnp.int32, (C, wk), 1), dk)
    scores = [jnp.zeros((C, C), F32) for _ in range(HEADS)]
    hsz = C // 2
    while hsz >= GLA_SUB:
        blk = 2 * hsz
        rows = []
        for m in range(C // blk):
            rrow = m * blk + (hsz if reverse else hsz - 1)
            rows.append(jnp.broadcast_to(c[rrow:rrow + 1, :], (blk, wk)))
        ref = jnp.concatenate(rows, axis=0) if len(rows) > 1 else rows[0]
        qt = q * jnp.exp2(jnp.minimum(c - ref, 0.0))
        kt = (k * jnp.exp2(jnp.minimum(ref - c, 0.0))).astype(BF16)
        same = _div_pow2(ii, blk) == _div_pow2(jj, blk)
        if reverse:
            lvl = same & (_mod_pow2(ii, blk) < hsz) & (_mod_pow2(jj, blk) >= hsz)
        else:
            lvl = same & (_mod_pow2(ii, blk) >= hsz) & (_mod_pow2(jj, blk) < hsz)
        for h in range(HEADS):
            qh = jnp.where(lane_head == h, qt, 0.0).astype(BF16)
            s = lax.dot_general(qh, kt, (((1,), (1,)), ((), ())), preferred_element_type=F32)
            scores[h] = scores[h] + jnp.where(lvl, s, 0.0)
        hsz //= 2

    assert dv == C
    er = _div_pow2(lax.broadcasted_iota(jnp.int32, (wk, wv), 0), dk)
    ec = _div_pow2(lax.broadcasted_iota(jnp.int32, (wk, wv), 1), dv)
    expand = jnp.where(er == ec, 1.0, 0.0).astype(BF16)
    dist = (jj - ii) if reverse else (ii - jj)
    same_sub = _div_pow2(ii, GLA_SUB) == _div_pow2(jj, GLA_SUB)
    for lag in range(1 if reverse else 0, GLA_SUB):
        if lag == 0:
            t = q * k
        else:
            shift = (GLA_SUB - lag) if reverse else lag
            ks = pltpu.roll(k.reshape(C // GLA_SUB, GLA_SUB, wk), shift, axis=1).reshape(C, wk)
            cs = pltpu.roll(c.reshape(C // GLA_SUB, GLA_SUB, wk), shift, axis=1).reshape(C, wk)
            t = q * ks * jnp.exp2(jnp.minimum(c - cs, 0.0))
        red = jnp.dot(t.astype(BF16), expand, preferred_element_type=F32)
        on_diag = same_sub & (dist == lag)
        for h in range(HEADS):
            scores[h] = scores[h] + jnp.where(on_diag, red[:, h * dv:(h + 1) * dv], 0.0)

    vb = v.astype(BF16)
    o = o + jnp.concatenate(
        [jnp.dot(scores[h].astype(BF16), vb[:, h * dv:(h + 1) * dv], preferred_element_type=F32) for h in range(HEADS)],
        axis=1)
    return o, new_st


def _gla_kernel(qf, kf, vf, lf, qb, kb, vb, lb, w2f, bf, w2b, bb, of_ref, ob_ref, s_sc, *, qscale):
    @pl.when(pl.program_id(1) == 0)
    def _():
        s_sc[...] = jnp.zeros(s_sc.shape, F32)

    per = qf.shape[0] // GLA_CHUNK
    st_f, st_b = s_sc[0], s_sc[1]
    for s in range(per):
        rf = pl.ds(s * GLA_CHUNK, GLA_CHUNK)
        o, st_f = _gla_direction(qf[rf, :] * qscale, kf[rf, :], vf[rf, :], lf[rf, :], w2f[...], bf[...], st_f, False)
        of_ref[rf, :] = o.astype(of_ref.dtype)
        rb = pl.ds((per - 1 - s) * GLA_CHUNK, GLA_CHUNK)
        o, st_b = _gla_direction(qb[rb, :] * qscale, kb[rb, :], vb[rb, :], lb[rb, :], w2b[...], bb[...], st_b, True)
        ob_ref[rb, :] = o.astype(ob_ref.dtype)
    s_sc[0] = st_f
    s_sc[1] = st_b


GLA_CHUNK = 128
GLA_CHUNKS_PER_STEP = 4


def _gla(z, *, qcol, kcol, vcol, lcol, w2f, bf, w2b, bb, batch, seq):
    chunk = GLA_CHUNK * GLA_CHUNKS_PER_STEP
    assert seq % chunk == 0
    n = seq // chunk
    wk, wv = HEADS * GLA_K_DIM, HEADS * LANE

    def specs(cmap):
        return [pl.BlockSpec((chunk, wk), lambda b, c: (cmap(b, c), qcol)),
                pl.BlockSpec((chunk, wk), lambda b, c: (cmap(b, c), kcol)),
                pl.BlockSpec((chunk, wv), lambda b, c: (cmap(b, c), vcol)),
                pl.BlockSpec((chunk, LANE), lambda b, c: (cmap(b, c), lcol))]

    fmap = lambda b, c: b * n + c
    bmap = lambda b, c: b * n + n - 1 - c
    wspec = [pl.BlockSpec((LANE, wk), lambda b, c: (0, 0)), pl.BlockSpec((1, wk), lambda b, c: (0, 0))]
    out = jax.ShapeDtypeStruct((batch * seq, wv), BF16)
    return pl.pallas_call(
        functools.partial(_gla_kernel, qscale=GLA_K_DIM ** -0.5),
        grid=(batch, n),
        in_specs=specs(fmap) + specs(bmap) + wspec + wspec,
        out_specs=[pl.BlockSpec((chunk, wv), lambda b, c: (fmap(b, c), 0)),
                   pl.BlockSpec((chunk, wv), lambda b, c: (bmap(b, c), 0))],
        out_shape=[out, out],
        scratch_shapes=[pltpu.VMEM((2, wv, wk), F32)],
        compiler_params=_cparams(("parallel", "arbitrary")),
        name="gla",
    )(z, z, z, z, z, z, z, z, w2f, bf, w2b, bb)


def _layer_norm_rows(r, g, b):
    mu = jnp.mean(r, axis=-1, keepdims=True)
    d = r - mu
    var = jnp.mean(d * d, axis=-1, keepdims=True)
    return d * lax.rsqrt(var + LN_EPS) * g + b


def _outproj_kernel(x_ref, fin_ref, of_ref, ob_ref, gate_ref, ng_ref, wa_ref, wb_ref, lg_ref, lb_ref, rw_ref, rb_ref,
                    o_ref, ids_ref, gw_ref, cnt_ref, run_sc, *, group_norm):
    lin = of_ref[...].astype(F32) + ob_ref[...].astype(F32)
    parts = []
    for h in range(HEADS):
        zh = lin[:, h * LANE:(h + 1) * LANE]
        if group_norm:
            mu = jnp.mean(zh, axis=-1, keepdims=True)
            dz = zh - mu
            parts.append(dz * lax.rsqrt(jnp.mean(dz * dz, axis=-1, keepdims=True) + LN_EPS))
        else:
            parts.append(zh * lax.rsqrt(jnp.mean(zh * zh, axis=-1, keepdims=True) + RMS_EPS) * ng_ref[...])
    gate = gate_ref[...]
    lin = jnp.concatenate(parts, axis=1) * (gate * jax.nn.sigmoid(gate))
    y = (jnp.dot(fin_ref[...].astype(BF16), wa_ref[...], preferred_element_type=F32)
         + jnp.dot(lin.astype(BF16), wb_ref[...], preferred_element_type=F32))
    x1 = _layer_norm_rows(ALPHA * x_ref[...] + y, lg_ref[...], lb_ref[...])
    o_ref[...] = x1
    _route_rows(x1, rw_ref, rb_ref, ids_ref, gw_ref, cnt_ref, run_sc)


def _outproj(x, fin, of, ob, gate_src, gate_col, norm_gain, wa, wb, ln_g, ln_b, route_w, route_b, *, group_norm, tm=512):
    n, d = x.shape
    w = HEADS * LANE
    tm = min(tm, n)
    row = lambda i: (i, 0)
    const = lambda i: (0, 0)
    rw_hi = route_w.astype(BF16)
    rw = jnp.stack([rw_hi, (route_w - rw_hi.astype(F32)).astype(BF16)])
    return pl.pallas_call(
        functools.partial(_outproj_kernel, group_norm=group_norm),
        grid=(n // tm,),
        in_specs=[pl.BlockSpec((tm, d), row), pl.BlockSpec((tm, w), row), pl.BlockSpec((tm, w), row),
                  pl.BlockSpec((tm, w), row), pl.BlockSpec((tm, w), lambda i: (i, gate_col)),
                  pl.BlockSpec((1, LANE), const), pl.BlockSpec((w, d), const), pl.BlockSpec((w, d), const),
                  pl.BlockSpec((1, d), const), pl.BlockSpec((1, d), const),
                  pl.BlockSpec((2, d, LANE), lambda i: (0, 0, 0)), pl.BlockSpec((1, LANE), const)],
        out_specs=[pl.BlockSpec((tm, d), row), pl.BlockSpec((tm, LANE), row), pl.BlockSpec((tm, LANE), row),
                   pl.BlockSpec((1, LANE), const)],
        out_shape=[jax.ShapeDtypeStruct((n, d), F32), jax.ShapeDtypeStruct((n, LANE), jnp.int32),
                   jax.ShapeDtypeStruct((n, LANE), F32), jax.ShapeDtypeStruct((1, LANE), jnp.int32)],
        scratch_shapes=[pltpu.VMEM((1, LANE), F32)],
        compiler_params=_cparams(("arbitrary",)),
        name="outproj",
    )(x, fin, of, ob, gate_src, norm_gain.reshape(1, LANE), wa, wb, ln_g.reshape(1, d), ln_b.reshape(1, d), rw, route_b)


def _route_rows(x, w_ref, b_ref, ids_ref, gw_ref, cnt_ref, run_sc):
    @pl.when(pl.program_id(0) == 0)
    def _():
        run_sc[...] = jnp.zeros(run_sc.shape, F32)

    tm = x.shape[0]
    xh = x.astype(BF16)
    xl = (x - xh.astype(F32)).astype(BF16)
    wh, wl = w_ref[0], w_ref[1]
    logits = (jnp.dot(xh, wh, preferred_element_type=F32) + jnp.dot(xh, wl, preferred_element_type=F32)
              + jnp.dot(xl, wh, preferred_element_type=F32)) + b_ref[...]
    lane = lax.broadcasted_iota(jnp.int32, logits.shape, 1)
    neg = -jnp.inf
    gmask = (lane >= N_EXPERTS) & (lane < N_EXPERTS + N_GROUPS)
    gl = jnp.where(gmask, logits, neg)
    gmax = jnp.max(gl, axis=1, keepdims=True)
    lane_f = lane.astype(F32)
    first = lambda hit: jnp.min(jnp.where(hit, lane_f, float(LANE)), axis=1, keepdims=True).astype(jnp.int32)
    gidx = first(gl == gmax) - N_EXPERTS
    p_grp = 1.0 / jnp.sum(jnp.where(gmask, jnp.exp(gl - gmax), 0.0), axis=1, keepdims=True)
    el = jnp.where(_div_pow2(lane, EXPERTS_PER_GROUP) == gidx, logits, neg)
    l1 = jnp.max(el, axis=1, keepdims=True)
    e1 = first(el == l1)
    el2 = jnp.where(lane == e1, neg, el)
    l2 = jnp.max(el2, axis=1, keepdims=True)
    e2 = first(el2 == l2)
    t = jnp.exp(l2 - l1)
    w1 = p_grp / (1.0 + t)
    w2 = p_grp * t / (1.0 + t)

    onehot = jnp.where(lane == e1, 1.0, jnp.where(lane == e2, 1.0, 0.0))
    ri = lax.broadcasted_iota(jnp.int32, (tm, tm), 0)
    ci = lax.broadcasted_iota(jnp.int32, (tm, tm), 1)
    before = jnp.dot(jnp.where(ri > ci, 1.0, 0.0).astype(BF16), onehot.astype(BF16), preferred_element_type=F32)
    before = before + run_sc[...]
    r1 = jnp.sum(jnp.where(lane == e1, before, 0.0), axis=1, keepdims=True).astype(jnp.int32)
    r2 = jnp.sum(jnp.where(lane == e2, before, 0.0), axis=1, keepdims=True).astype(jnp.int32)
    run_sc[...] = run_sc[...] + jnp.sum(onehot, axis=0, keepdims=True)
    cnt_ref[...] = run_sc[...].astype(jnp.int32)
    ids_ref[...] = jnp.where(lane == 0, e1, jnp.where(lane == 1, e2, jnp.where(lane == 2, r1, jnp.where(lane == 3, r2, 0))))
    gw_ref[...] = jnp.where(lane == 0, w1, jnp.where(lane == 1, w2, 0.0))


MOE_BM = 256
SUBL = 8


def _rows_from_linear(ref, rows):
    return jnp.concatenate([ref[pl.ds(s, rows, stride=SUBL), :] for s in range(SUBL)], axis=1)


def _rows_to_linear(ref, val):
    for s in range(SUBL):
        ref[pl.ds(s, val.shape[0], stride=SUBL), :] = val[:, s * LANE:(s + 1) * LANE]


def _dispatch_kernel(pend_ref, padded_ref, dest_ref, x_ref, xs_hbm, idx_smem, zbuf, lin, sem_i, sem_z, sem):
    i = pl.program_id(0)
    tm = x_ref.shape[0]
    bm = zbuf.shape[0] // SUBL

    @pl.when(i == 0)
    def _():
        zbuf[...] = jnp.zeros(zbuf.shape, F32)

        def tail(e):
            start_row = pl.multiple_of((pend_ref[e] - bm) * SUBL, bm * SUBL)
            return pltpu.make_async_copy(zbuf, xs_hbm.at[pl.ds(start_row, bm * SUBL), :], sem_z)

        def start(e, carry):
            @pl.when(padded_ref[e] > 0)
            def _():
                tail(e).start()
            return carry

        def wait(e, carry):
            @pl.when(padded_ref[e] > 0)
            def _():
                tail(e).wait()
            return carry

        lax.fori_loop(0, N_EXPERTS, start, 0)
        lax.fori_loop(0, N_EXPERTS, wait, 0)

        def unused(b):
            start_row = pl.multiple_of(b * bm * SUBL, bm * SUBL)
            return pltpu.make_async_copy(zbuf, xs_hbm.at[pl.ds(start_row, bm * SUBL), :], sem_z)

        first_unused = pend_ref[N_EXPERTS - 1] // bm
        n_blocks = xs_hbm.shape[0] // (bm * SUBL)
        lax.fori_loop(first_unused, n_blocks, lambda b, c: (unused(b).start(), c)[1], 0)
        lax.fori_loop(first_unused, n_blocks, lambda b, c: (unused(b).wait(), c)[1], 0)

    cp = pltpu.make_async_copy(dest_ref.at[i], idx_smem, sem_i)
    cp.start()
    _rows_to_linear(lin, x_ref[...])
    cp.wait()

    def scatter(r, carry):
        src = lin.at[pl.ds(pl.multiple_of(r * SUBL, SUBL), SUBL), :]
        for k in range(2):
            dst = pl.multiple_of(idx_smem[2 * r + k] * SUBL, SUBL)
            pltpu.make_async_copy(src, xs_hbm.at[pl.ds(dst, SUBL), :], sem).start(priority=k)
        return carry

    lax.fori_loop(0, tm, scatter, 0, unroll=8)
    for k in range(2):
        pltpu.make_async_copy(lin, xs_hbm.at[pl.ds(0, tm * SUBL), :], sem).wait()


def _dispatch(x, dest, pend, padded, cap, *, tm=512):
    n, d = x.shape
    assert d == SUBL * LANE
    tm = min(tm, n)
    nt = n // tm
    grid_spec = pltpu.PrefetchScalarGridSpec(
        num_scalar_prefetch=2,
        grid=(nt,),
        in_specs=[pl.BlockSpec((nt, 2 * tm), lambda i, pe, pa: (0, 0)),
                  pl.BlockSpec((tm, d), lambda i, pe, pa: (i, 0))],
        out_specs=pl.BlockSpec(memory_space=pl.ANY),
        scratch_shapes=[pltpu.SMEM((2 * tm,), jnp.int32), pltpu.VMEM((MOE_BM * SUBL, LANE), F32),
                        pltpu.VMEM((tm * SUBL, LANE), F32),
                        pltpu.SemaphoreType.DMA(()), pltpu.SemaphoreType.DMA(()), pltpu.SemaphoreType.DMA(())],
    )
    return pl.pallas_call(
        _dispatch_kernel,
        grid_spec=grid_spec,
        out_shape=jax.ShapeDtypeStruct((cap * SUBL, LANE), F32),
        compiler_params=_cparams(("arbitrary",)),
        name="dispatch",
    )(pend, padded, dest.reshape(nt, 2 * tm), x)


def _experts_kernel(blk_e_ref, nused_ref, xs_ref, wg_ref, wu_ref, wd_ref, ys_ref, wgb, wub, wdb, hbuf):
    i = pl.program_id(0)
    nb = pl.num_programs(0) - 1
    cur = jnp.minimum(i, nb - 1)
    prv = jnp.maximum(i - 1, 0)

    @pl.when(i == 0)
    def _():
        hbuf[...] = jnp.zeros(hbuf.shape, BF16)

    @pl.when(jnp.logical_or(i == 0, blk_e_ref[cur] != blk_e_ref[jnp.minimum(prv, nb - 1)]))
    def _():
        wgb[...] = wg_ref[...].astype(BF16)
        wub[...] = wu_ref[...].astype(BF16)

    @pl.when(jnp.logical_or(i == 0, blk_e_ref[prv] != blk_e_ref[jnp.maximum(i - 2, 0)]))
    def _():
        wdb[...] = wd_ref[...].astype(BF16)

    slot = lax.rem(i, 2)
    y = jnp.dot(hbuf[1 - slot], wdb[...], preferred_element_type=F32)
    live = jnp.logical_and(i >= 1, i - 1 < nused_ref[0])
    _rows_to_linear(ys_ref, jnp.where(live, y, 0.0))
    xb = _rows_from_linear(xs_ref, xs_ref.shape[0] // SUBL).astype(BF16)
    hg = jnp.dot(xb, wgb[...], preferred_element_type=F32)
    hu = jnp.dot(xb, wub[...], preferred_element_type=F32)
    hbuf[slot] = (hg * jax.nn.sigmoid(hg) * hu).astype(BF16)


def _experts(xs, blk_e, n_used, w_gate, w_up, w_down, layer):
    cap = xs.shape[0] // SUBL
    bm = MOE_BM
    nb = cap // bm
    d, de = w_gate.shape[2], w_gate.shape[3]
    row_in = lambda i, be, nu: (jnp.minimum(i, nu[0] - 1), 0)
    row = lambda i, be, nu: (jnp.maximum(i - 1, 0), 0)
    w_cur = lambda i, be, nu: (layer, be[jnp.minimum(i, nb - 1)], 0, 0)
    w_prv = lambda i, be, nu: (layer, be[jnp.maximum(i - 1, 0)], 0, 0)
    grid_spec = pltpu.PrefetchScalarGridSpec(
        num_scalar_prefetch=2,
        grid=(nb + 1,),
        in_specs=[pl.BlockSpec((bm * SUBL, LANE), row_in),
                  pl.BlockSpec((None, None, d, de), w_cur),
                  pl.BlockSpec((None, None, d, de), w_cur),
                  pl.BlockSpec((None, None, de, d), w_prv)],
        out_specs=pl.BlockSpec((bm * SUBL, LANE), row),
        scratch_shapes=[pltpu.VMEM((d, de), BF16), pltpu.VMEM((d, de), BF16), pltpu.VMEM((de, d), BF16),
                        pltpu.VMEM((2, bm, de), BF16)],
    )
    return pl.pallas_call(
        _experts_kernel,
        grid_spec=grid_spec,
        out_shape=jax.ShapeDtypeStruct((cap * SUBL, LANE), F32),
        compiler_params=_cparams(("arbitrary",)),
        name="experts",
    )(blk_e, n_used, xs, w_gate, w_up, w_down)


def _combine_kernel(dest_ref, x_ref, gw_ref, g_ref, b_ref, ys_hbm, o_ref, idx_smem, ybuf, sem_i, sem):
    i = pl.program_id(0)
    tm = x_ref.shape[0]

    def issue(tile, slot):
        cp = pltpu.make_async_copy(dest_ref.at[tile], idx_smem, sem_i)
        cp.start()
        cp.wait()

        def gather(r, carry):
            row = pl.multiple_of(r * SUBL, SUBL)
            for k in range(2):
                src = pl.multiple_of(idx_smem[2 * r + k] * SUBL, SUBL)
                pltpu.make_async_copy(ys_hbm.at[pl.ds(src, SUBL), :], ybuf.at[slot, k, pl.ds(row, SUBL), :],
                                      sem.at[slot]).start(priority=k)
            return carry

        lax.fori_loop(0, tm, gather, 0, unroll=8)

    slot = lax.rem(i, 2)

    @pl.when(i == 0)
    def _():
        issue(0, 0)

    @pl.when(i + 1 < pl.num_programs(0))
    def _():
        issue(i + 1, 1 - slot)

    for k in range(2):
        pltpu.make_async_copy(ys_hbm.at[pl.ds(0, tm * SUBL), :], ybuf.at[slot, k], sem.at[slot]).wait()
    gw = gw_ref[...]
    ffn = (_rows_from_linear(ybuf.at[slot, 0], tm) * gw[:, 0:1] + _rows_from_linear(ybuf.at[slot, 1], tm) * gw[:, 1:2])
    o_ref[...] = _layer_norm_rows(ALPHA * x_ref[...] + ffn, g_ref[...], b_ref[...])


def _combine(x, ys, dest, gw, ln_g, ln_b, *, tm=512):
    n, d = x.shape
    tm = min(tm, n)
    nt = n // tm
    return pl.pallas_call(
        _combine_kernel,
        grid=(nt,),
        in_specs=[pl.BlockSpec((nt, 2 * tm), lambda i: (0, 0)),
                  pl.BlockSpec((tm, d), lambda i: (i, 0)), pl.BlockSpec((tm, LANE), lambda i: (i, 0)),
                  pl.BlockSpec((1, d), lambda i: (0, 0)), pl.BlockSpec((1, d), lambda i: (0, 0)),
                  pl.BlockSpec(memory_space=pl.ANY)],
        out_specs=pl.BlockSpec((tm, d), lambda i: (i, 0)),
        out_shape=jax.ShapeDtypeStruct((n, d), F32),
        scratch_shapes=[pltpu.SMEM((2 * tm,), jnp.int32), pltpu.VMEM((2, 2, tm * SUBL, LANE), F32),
                        pltpu.SemaphoreType.DMA(()), pltpu.SemaphoreType.DMA((2,))],
        compiler_params=_cparams(("arbitrary",)),
        name="combine",
    )(dest.reshape(nt, 2 * tm), x, gw, ln_g.reshape(1, d), ln_b.reshape(1, d), ys)


def _router_params(w_grp, b_grp, w_exp, b_exp):
    d = w_exp.shape[0]
    wr = jnp.zeros((d, LANE), F32).at[:, :N_EXPERTS].set(w_exp).at[:, N_EXPERTS:N_EXPERTS + N_GROUPS].set(w_grp)
    br = jnp.zeros((1, LANE), F32).at[0, :N_EXPERTS].set(b_exp).at[0, N_EXPERTS:N_EXPERTS + N_GROUPS].set(b_grp)
    return wr, br


def _moe(x, routing, w_gate, w_up, w_down, layer, ln_g, ln_b):
    n, d = x.shape
    ids, gw, cnt = routing
    bm = MOE_BM
    counts = cnt[0, :N_EXPERTS]
    padded = (counts + bm - 1) // bm * bm
    pend = jnp.cumsum(padded)
    pstart = pend - padded
    e, r = ids[:, 0:2], ids[:, 2:4]
    onehot = e[:, :, None] == jnp.arange(N_EXPERTS, dtype=jnp.int32)[None, None, :]
    dest = jnp.sum(jnp.where(onehot, pstart[None, None, :], 0), axis=-1) + r
    cap = 2 * n + N_EXPERTS * bm
    nb = cap // bm
    blk_start = jnp.arange(nb, dtype=jnp.int32) * bm
    blk_e = jnp.minimum(jnp.sum((pend[None, :] <= blk_start[:, None]).astype(jnp.int32), axis=1), N_EXPERTS - 1)
    n_used = (pend[-1:] // bm).astype(jnp.int32)
    xs = _dispatch(x, dest, pend.astype(jnp.int32), padded.astype(jnp.int32), cap)
    ys = _experts(xs, blk_e, n_used, w_gate, w_up, w_down, layer)
    return _combine(x, ys, dest, gw, ln_g, ln_b)


def _even_layer(x, batch, seq, layer_idx, w_in, dec_f, dec_b, lq1, lk1, lq2, lk2, subln, w_out, ln_g, ln_b, route):
    d = x.shape[1]
    w = HEADS * LANE
    kw = dict(batch=batch, seq=seq)
    diff_seg = [(0, DIFF_ROT_DIM), (DIFF_HEAD_DIM, DIFF_ROT_DIM)]
    q, k, v, gate, dq, dk, dvt = _even_in(
        x, w_in.astype(BF16), _rope_tables(seq, [(0, LANE)], RET_THETA),
        _rope_tables(seq, diff_seg, ROPE_THETA), **kw)
    decays = jnp.stack([dec_f, dec_b]).astype(F32)
    of, ob = _retention(q, k, v, decays, **kw)
    lam_init = 0.8 - 0.6 * math.exp(-0.3 * layer_idx)
    diff = _flash(dq, dk, dvt, diff=(lq1, lk1, lq2, lk2, subln), lam_init=lam_init, tq=1024, passes=2, **kw)
    wo = w_out.astype(BF16)
    return _outproj(x, diff, of, ob, gate, 0, jnp.ones((LANE,), F32), wo[w:], wo[:w], ln_g, ln_b, *route,
                    group_norm=True)


def _odd_layer(x, batch, seq, w_in, q_norm, w_uq, kv_norm, w_ukv, w2_f, b_f, w2_b, b_b, gla_norm, w_out, ln_g, ln_b,
               route):
    d = x.shape[1]
    w = HEADS * LANE
    o = np.cumsum([0, MLA_Q_RANK, MLA_KV_RANK, MLA_ROPE, HEADS * GLA_K_DIM, HEADS * GLA_K_DIM, w, w,
                   GLA_GATE_RANK, GLA_GATE_RANK]).tolist()
    zeros = lambda c: jnp.zeros((d, c), F32)
    w_in2 = jnp.concatenate([
        w_in[:, o[0]:o[2]], zeros(MLA_NOPE), w_in[:, o[2]:o[3]], zeros(LANE - MLA_NOPE - MLA_ROPE),
        w_in[:, o[3]:o[7]], w_in[:, o[7]:o[9]], zeros(LANE - 2 * GLA_GATE_RANK)], axis=1).astype(BF16)
    kw = dict(batch=batch, seq=seq)
    qd = MLA_NOPE + MLA_ROPE
    w_uq2 = jnp.pad(w_uq.reshape(MLA_Q_RANK, HEADS, qd), ((0, 0), (0, 0), (0, LANE - qd))).reshape(MLA_Q_RANK, w)
    ukv = w_ukv.reshape(MLA_KV_RANK, HEADS, MLA_NOPE + MLA_V)
    w_uk2 = jnp.pad(ukv[:, :, :MLA_NOPE], ((0, 0), (0, 0), (0, LANE - MLA_NOPE))).reshape(MLA_KV_RANK, w)
    w_uv2 = ukv[:, :, MLA_NOPE:].reshape(MLA_KV_RANK, w)
    q, k, vt, zg = _odd_in(x, w_in2, q_norm, w_uq2.astype(BF16), kv_norm,
                           jnp.concatenate([w_uk2, w_uv2], axis=1).astype(BF16),
                           _rope_tables(seq, [(MLA_NOPE, MLA_ROPE)], ROPE_THETA), **kw)
    mla = _flash(q, k, vt, passes=4, **kw)
    wk = HEADS * GLA_K_DIM
    pad_rows = lambda m, r0: jnp.zeros((LANE, wk), F32).at[r0:r0 + GLA_GATE_RANK].set(m).astype(BF16)
    of, ob = _gla(zg, qcol=0, kcol=1, vcol=1, lcol=12,
                  w2f=pad_rows(w2_f, 0), bf=b_f.reshape(1, wk), w2b=pad_rows(w2_b, GLA_GATE_RANK), bb=b_b.reshape(1, wk), **kw)
    wo = w_out.astype(BF16)
    return _outproj(x, mla, of, ob, zg, 2, gla_norm, wo[:w], wo[w:], ln_g, ln_b, *route, group_norm=False)


def kernel(x, ev_w_in, ev_ret_decay_f, ev_ret_decay_b, ev_lq1, ev_lk1, ev_lq2, ev_lk2, ev_subln, ev_w_out, od_w_in, od_q_norm, od_w_uq, od_kv_norm, od_w_ukv, od_gla_w2_f, od_gla_b_f, od_gla_w2_b, od_gla_b_b, od_gla_norm, od_w_out, ln1_g, ln1_b, ln2_g, ln2_b, moe_w_grp, moe_b_grp, moe_w_exp, moe_b_exp, moe_w_gate, moe_w_up, moe_w_down):
    batch, seq, d = x.shape
    h = x.reshape(batch * seq, d)
    for i in range(DEPTH):
        j = i // 2
        route = _router_params(moe_w_grp[i], moe_b_grp[i], moe_w_exp[i], moe_b_exp[i])
        if i % 2 == 0:
            h, *routing = _even_layer(h, batch, seq, i, ev_w_in[j], ev_ret_decay_f[j], ev_ret_decay_b[j], ev_lq1[j],
                                      ev_lk1[j], ev_lq2[j], ev_lk2[j], ev_subln[j], ev_w_out[j], ln1_g[i], ln1_b[i], route)
        else:
            h, *routing = _odd_layer(h, batch, seq, od_w_in[j], od_q_norm[j], od_w_uq[j], od_kv_norm[j], od_w_ukv[j],
                                     od_gla_w2_f[j], od_gla_b_f[j], od_gla_w2_b[j], od_gla_b_b[j], od_gla_norm[j],
                                     od_w_out[j], ln1_g[i], ln1_b[i], route)
        h = _moe(h, routing, moe_w_gate, moe_w_up, moe_w_down, i, ln2_g[i], ln2_b[i])
    return h.reshape(batch, seq, d)
```

```python
import functools
import math

import numpy as np
import jax
import jax.numpy as jnp
from jax import lax
from jax.experimental import pallas as pl
from jax.experimental.pallas import tpu as pltpu

F32 = jnp.float32
BF16 = jnp.bfloat16

HEADS = 4
LANE = 128
RET_THETA = 10000.0
ROPE_THETA = 500000.0
DIFF_HEAD_DIM = 64
DIFF_ROT_DIM = 16
MLA_Q_RANK = 256
MLA_KV_RANK = 128
MLA_NOPE = 64
MLA_ROPE = 32
MLA_V = 128
GLA_K_DIM = 64
GLA_GATE_RANK = 16
GLA_TAU = 16.0
N_GROUPS = 4
EXPERTS_PER_GROUP = 8
N_EXPERTS = N_GROUPS * EXPERTS_PER_GROUP
DEPTH = 2
ALPHA = (2.0 * DEPTH) ** 0.25
LN_EPS = 1e-5
RMS_EPS = 1e-6

VMEM_LIMIT = 48 * 1024 * 1024


def _div_pow2(x, n):
    return lax.shift_right_logical(x, int(n).bit_length() - 1)


def _mod_pow2(x, n):
    return lax.bitwise_and(x, int(n) - 1)


def _cparams(sem):
    return pltpu.CompilerParams(dimension_semantics=sem, vmem_limit_bytes=VMEM_LIMIT)


def _rope_heads(z, tabs, sh, scale):
    c, sa, sb = tabs
    outs = []
    for h in range(HEADS):
        zh = z[:, h * LANE:(h + 1) * LANE]
        outs.append((zh * c + pltpu.roll(zh, sh, axis=1) * sa + pltpu.roll(zh, LANE - sh, axis=1) * sb) * scale)
    return outs


def _even_in_kernel(x_ref, w_ref, rc, rsa, rsb, dc, dsa, dsb, q_ref, k_ref, v_ref, g_ref, dq_ref, dk_ref, dvt_ref):
    w = HEADS * LANE
    xb = x_ref[...].astype(BF16)
    part = lambda t: jnp.dot(xb, w_ref[:, t * w:(t + 1) * w], preferred_element_type=F32)
    ret_t = (rc[...], rsa[...], rsb[...])
    diff_t = (dc[...], dsa[...], dsb[...])
    for h, o in enumerate(_rope_heads(part(0), ret_t, LANE // 2, 1.0)):
        q_ref[h] = o.astype(BF16)
    for h, o in enumerate(_rope_heads(part(1), ret_t, LANE // 2, LANE ** -0.5)):
        k_ref[h] = o.astype(BF16)
    rv = part(2)
    for h in range(HEADS):
        v_ref[h] = rv[:, h * LANE:(h + 1) * LANE].astype(BF16)
    g_ref[...] = part(3)
    for h, o in enumerate(_rope_heads(part(4), diff_t, DIFF_ROT_DIM // 2, DIFF_HEAD_DIM ** -0.5 * LOG2E)):
        dq_ref[h] = o.T.astype(BF16)
    for h, o in enumerate(_rope_heads(part(5), diff_t, DIFF_ROT_DIM // 2, 1.0)):
        dk_ref[h] = o.astype(BF16)
    dv = part(6)
    for h in range(HEADS):
        dvt_ref[h, :LANE, :] = dv[:, h * LANE:(h + 1) * LANE].T.astype(BF16)
        dvt_ref[h, LANE:, :] = jnp.ones((ONES_ROWS, dvt_ref.shape[2]), BF16)


def _even_in(x, w, ret_tabs, diff_tabs, *, batch, seq):
    n, d = x.shape
    tm = min(FLASH_TK_DIFF, seq // 2)
    nt = seq // tm
    hw = HEADS * LANE
    heads = jax.ShapeDtypeStruct((batch, HEADS, seq, LANE), BF16)
    head_spec = pl.BlockSpec((None, HEADS, tm, LANE), lambda i: (i // nt, 0, i % nt, 0))
    heads_t = jax.ShapeDtypeStruct((batch, HEADS, LANE, seq), BF16)
    head_t_spec = pl.BlockSpec((None, HEADS, LANE, tm), lambda i: (i // nt, 0, 0, i % nt))
    tab_spec = pl.BlockSpec((tm, LANE), lambda i: (i % nt, 0))
    return pl.pallas_call(
        _even_in_kernel,
        grid=(n // tm,),
        in_specs=[pl.BlockSpec((tm, d), lambda i: (i, 0)), pl.BlockSpec((d, 7 * hw), lambda i: (0, 0))] + [tab_spec] * 6,
        out_specs=[head_spec, head_spec, head_spec, pl.BlockSpec((tm, hw), lambda i: (i, 0)), head_t_spec, head_spec,
                   pl.BlockSpec((None, HEADS, None, LANE + ONES_ROWS, tm), lambda i: (i // nt, 0, i % nt, 0, 0))],
        out_shape=[heads, heads, heads, jax.ShapeDtypeStruct((n, hw), F32), heads_t, heads,
                   jax.ShapeDtypeStruct((batch, HEADS, nt, LANE + ONES_ROWS, tm), BF16)],
        compiler_params=_cparams(("parallel",)),
        name="even_in",
    )(x, w, *ret_tabs, *diff_tabs)


def _rms_rows(z, g):
    return z * lax.rsqrt(jnp.mean(z * z, axis=-1, keepdims=True) + RMS_EPS) * g


def _odd_in_kernel(x_ref, w_ref, qn_ref, wq_ref, kvn_ref, wkv_ref, tc, tsa, tsb, q_ref, k_ref, vt_ref, zg_ref):
    hw = HEADS * LANE
    mla_w = MLA_Q_RANK + MLA_KV_RANK + LANE
    xb = x_ref[...].astype(BF16)
    zg_ref[...] = jnp.dot(xb, w_ref[:, mla_w:], preferred_element_type=F32)
    z1 = jnp.dot(xb, w_ref[:, :mla_w], preferred_element_type=F32)
    tabs = (tc[...], tsa[...], tsb[...])
    sh = MLA_ROPE // 2
    qh = jnp.dot(_rms_rows(z1[:, :MLA_Q_RANK], qn_ref[...]).astype(BF16), wq_ref[...], preferred_element_type=F32)
    for h, o in enumerate(_rope_heads(qh, tabs, sh, (MLA_NOPE + MLA_ROPE) ** -0.5 * LOG2E)):
        q_ref[h] = o.T.astype(BF16)
    ckv = _rms_rows(z1[:, MLA_Q_RANK:MLA_Q_RANK + MLA_KV_RANK], kvn_ref[...]).astype(BF16)
    kv = jnp.dot(ckv, wkv_ref[...], preferred_element_type=F32)
    kr = z1[:, MLA_Q_RANK + MLA_KV_RANK:]
    kr = kr * tabs[0] + pltpu.roll(kr, sh, axis=1) * tabs[1] + pltpu.roll(kr, LANE - sh, axis=1) * tabs[2]
    for h in range(HEADS):
        k_ref[h] = (kv[:, h * LANE:(h + 1) * LANE] + kr).astype(BF16)
        vt_ref[h, :LANE, :] = kv[:, hw + h * LANE:hw + (h + 1) * LANE].T.astype(BF16)
        vt_ref[h, LANE:, :] = jnp.ones((ONES_ROWS, vt_ref.shape[2]), BF16)


def _odd_in(x, w, q_norm, w_uq, kv_norm, w_ukv, tabs, *, batch, seq, tm=512):
    n, d = x.shape
    tm = min(tm, seq // 2)
    nt = seq // tm
    tk = min(FLASH_TK, seq // 2)
    per = tk // tm
    hw = HEADS * LANE
    gw_ = w.shape[1] - (MLA_Q_RANK + MLA_KV_RANK + LANE)
    heads = jax.ShapeDtypeStruct((batch, HEADS, seq, LANE), BF16)
    head_spec = pl.BlockSpec((None, HEADS, tm, LANE), lambda i: (i // nt, 0, i % nt, 0))
    tab_spec = pl.BlockSpec((tm, LANE), lambda i: (i % nt, 0))
    const = lambda i: (0, 0)
    return pl.pallas_call(
        _odd_in_kernel,
        grid=(n // tm,),
        in_specs=[pl.BlockSpec((tm, d), lambda i: (i, 0)), pl.BlockSpec(w.shape, const),
                  pl.BlockSpec((1, MLA_Q_RANK), const), pl.BlockSpec(w_uq.shape, const),
                  pl.BlockSpec((1, MLA_KV_RANK), const), pl.BlockSpec(w_ukv.shape, const)] + [tab_spec] * 3,
        out_specs=[pl.BlockSpec((None, HEADS, LANE, tm), lambda i: (i // nt, 0, 0, i % nt)), head_spec,
                   pl.BlockSpec((None, HEADS, None, LANE + ONES_ROWS, tm),
                                lambda i: (i // nt, 0, (i % nt) // per, 0, (i % nt) % per)),
                   pl.BlockSpec((tm, gw_), lambda i: (i, 0))],
        out_shape=[jax.ShapeDtypeStruct((batch, HEADS, LANE, seq), BF16), heads, jax.ShapeDtypeStruct((batch, HEADS, seq // tk, LANE + ONES_ROWS, tk), BF16),
                   jax.ShapeDtypeStruct((n, gw_), F32)],
        compiler_params=_cparams(("parallel",)),
        name="odd_in",
    )(x, w, q_norm.reshape(1, -1), w_uq, kv_norm.reshape(1, -1), w_ukv, *tabs)


def _rope_tables(seq, segs, theta):
    pos = jnp.arange(seq, dtype=F32)
    inv = jnp.zeros((LANE,), F32)
    lo = np.zeros((LANE,), bool)
    hi = np.zeros((LANE,), bool)
    for start, rot in segs:
        half = rot // 2
        f = jnp.power(jnp.float32(theta), -jnp.arange(0, rot, 2, dtype=F32) / rot)
        inv = inv.at[start:start + half].set(f).at[start + half:start + rot].set(f)
        lo[start:start + half] = True
        hi[start + half:start + rot] = True
    ang = pos[:, None] * inv[None, :]
    cos, sin = jnp.cos(ang), jnp.sin(ang)
    c = jnp.where(lo | hi, cos, 1.0)
    sa = jnp.where(hi, sin, 0.0)
    sb = jnp.where(lo, -sin, 0.0)
    return c, sa, sb


ONES_ROWS = 16
LOG2E = math.log2(math.e)
FLASH_TK = 1024
FLASH_TK_DIFF = 512


def _flash_kernel(*refs, ncomp, nk, lam_init):
    if ncomp == 2:
        q_ref, k_ref, vt_ref, lq1, lk1, lq2, lk2, g_ref, o_ref, *scr = refs
    else:
        q_ref, k_ref, vt_ref, o_ref, *scr = refs
    qm_sc, m_sc, acc_sc, s0, s1, cm0, cm1, p0, p1, al0, al1 = scr
    tk = s0.shape[1]
    tq = qm_sc.shape[2]

    def start(sub):
        q = q_ref[:, sub * tq:(sub + 1) * tq]
        if ncomp == 2:
            chan = lax.broadcasted_iota(jnp.int32, q.shape, 0)
            zero = jnp.zeros_like(q)
            qm_sc[0] = jnp.where(chan < DIFF_HEAD_DIM, q, zero)
            qm_sc[1] = jnp.where(chan >= DIFF_HEAD_DIM, q, zero)
        else:
            qm_sc[0] = q
        m_sc[...] = jnp.full(m_sc.shape, -jnp.inf, F32)
        acc_sc[...] = jnp.zeros(acc_sc.shape, F32)

    def scores(j, s_ref, cm_ref):
        k = k_ref[j * tk:(j + 1) * tk, :]
        for c in range(ncomp):
            s = jnp.dot(k, qm_sc[c], preferred_element_type=F32)
            s_ref[c] = s
            cm_ref[c] = jnp.max(s, axis=0, keepdims=True)

    def softmax(s_ref, cm_ref, p_ref, al_ref):
        for c in range(ncomp):
            m_old = m_sc[c]
            m_new = jnp.maximum(m_old, cm_ref[c])
            al_ref[c] = jnp.exp2(m_old - m_new)
            p_ref[c] = jnp.exp2(s_ref[c] - m_new).astype(BF16)
            m_sc[c] = m_new

    def values(j, p_ref, al_ref):
        vt = vt_ref[j]
        for c in range(ncomp):
            acc_sc[c] = al_ref[c] * acc_sc[c] + jnp.dot(vt, p_ref[c], preferred_element_type=F32)

    def normalised(c):
        acc = acc_sc[c]
        return acc[:LANE] / acc[LANE:LANE + 1]

    bufs = ((s0, cm0, p0, al0), (s1, cm1, p1, al1))
    for sub in range(q_ref.shape[1] // tq):
        start(sub)
        scores(0, s0, cm0)
        for j in range(nk):
            s_c, cm_c, p_c, al_c = bufs[j % 2]
            s_n, cm_n, p_n, al_n = bufs[(j + 1) % 2]
            if j + 1 < nk:
                scores(j + 1, s_n, cm_n)
            softmax(s_c, cm_c, p_c, al_c)
            if j >= 1:
                values(j - 1, p_n, al_n)
        values(nk - 1, *bufs[(nk - 1) % 2][2:])
        o = normalised(0)
        if ncomp == 2:
            lam = (jnp.exp(jnp.sum(lq1[...] * lk1[...], keepdims=True))
                   - jnp.exp(jnp.sum(lq2[...] * lk2[...], keepdims=True)) + lam_init)
            o = o - lam * normalised(1)
            o = o * lax.rsqrt(jnp.mean(o * o, axis=0, keepdims=True) + RMS_EPS) * g_ref[...] * (1.0 - lam_init)
        o_ref[sub * tq:(sub + 1) * tq, :] = o.T.astype(o_ref.dtype)


def _flash(q, k, vt, *, batch, seq, tq=512, passes=1, diff=None, lam_init=0.0):
    nk, vrows, tk = vt.shape[2], vt.shape[3], vt.shape[4]
    assert vrows == LANE + ONES_ROWS
    tq = min(tq, seq)
    passes = min(passes, seq // tq)
    bq = tq * passes
    nq = seq // bq
    ncomp = 2 if diff is not None else 1
    in_specs = [
        pl.BlockSpec((None, None, LANE, bq), lambda b, h, i: (b, h, 0, i)),
        pl.BlockSpec((None, None, seq, LANE), lambda b, h, i: (b, h, 0, 0)),
        pl.BlockSpec((None, None, nk, vrows, tk), lambda b, h, i: (b, h, 0, 0, 0)),
    ]
    args = [q, k, vt]
    if diff is not None:
        lq1, lk1, lq2, lk2, subln = diff
        for v in (lq1, lk1, lq2, lk2):
            in_specs.append(pl.BlockSpec((1, DIFF_HEAD_DIM), lambda b, h, i: (0, 0)))
            args.append(v.reshape(1, DIFF_HEAD_DIM))
        in_specs.append(pl.BlockSpec((LANE, 1), lambda b, h, i: (0, 0)))
        args.append(subln.reshape(LANE, 1))
    return pl.pallas_call(
        functools.partial(_flash_kernel, ncomp=ncomp, nk=nk, lam_init=lam_init),
        grid=(batch, HEADS, nq),
        in_specs=in_specs,
        out_specs=pl.BlockSpec((bq, LANE), lambda b, h, i: (b * nq + i, h)),
        out_shape=jax.ShapeDtypeStruct((batch * seq, HEADS * LANE), BF16),
        scratch_shapes=[pltpu.VMEM((ncomp, LANE, tq), BF16),
                        pltpu.VMEM((ncomp, 1, tq), F32), pltpu.VMEM((ncomp, vrows, tq), F32),
                        pltpu.VMEM((ncomp, tk, tq), F32), pltpu.VMEM((ncomp, tk, tq), F32),
                        pltpu.VMEM((ncomp, 1, tq), F32), pltpu.VMEM((ncomp, 1, tq), F32),
                        pltpu.VMEM((ncomp, tk, tq), BF16), pltpu.VMEM((ncomp, tk, tq), BF16),
                        pltpu.VMEM((ncomp, 1, tq), F32), pltpu.VMEM((ncomp, 1, tq), F32)],
        compiler_params=_cparams(("parallel", "parallel", "parallel")),
        name="flash_diff" if diff is not None else "flash_mla",
    )(*args)


def _ret_kernel(dec_ref, qf, kf, vf, qb, kb, vb, of_ref, ob_ref, s_sc, *, chunk):
    @pl.when(pl.program_id(1) == 0)
    def _():
        s_sc[...] = jnp.zeros(s_sc.shape, F32)

    ii = lax.broadcasted_iota(jnp.int32, (chunk, chunk), 0)
    jj = lax.broadcasted_iota(jnp.int32, (chunk, chunk), 1)
    r = lax.broadcasted_iota(jnp.int32, (chunk, 1), 0).astype(F32)
    for d, (q_ref, k_ref, v_ref, o_ref) in enumerate(((qf, kf, vf, of_ref), (qb, kb, vb, ob_ref))):
        for h in range(HEADS):
            la = -jnp.exp(jnp.full((1, 1), dec_ref[d, h], F32))
            if d == 0:
                mask, dist = ii >= jj, (ii - jj).astype(F32)
                qdec, kdec = jnp.exp(la * (r + 1.0)), jnp.exp(la * (chunk - 1.0 - r))
            else:
                mask, dist = jj > ii, (jj - ii).astype(F32)
                qdec, kdec = jnp.exp(la * (chunk - r)), jnp.exp(la * r)
            decay = jnp.where(mask, jnp.exp(jnp.where(mask, dist * la, 0.0)), 0.0)
            state = s_sc[d, h]
            per = q_ref.shape[1] // chunk
            for step in range(per):
                rows = pl.ds((step if d == 0 else per - 1 - step) * chunk, chunk)
                q, k, v = q_ref[h, rows, :], k_ref[h, rows, :], v_ref[h, rows, :]
                s = lax.dot_general(q, k, (((1,), (1,)), ((), ())), preferred_element_type=F32)
                o = jnp.dot((s * decay).astype(BF16), v, preferred_element_type=F32)
                o = o + qdec * jnp.dot(q, state.astype(BF16), preferred_element_type=F32)
                kd = (k.astype(F32) * kdec).astype(BF16)
                state = jnp.exp(la * float(chunk)) * state + lax.dot_general(
                    kd, v, (((0,), (0,)), ((), ())), preferred_element_type=F32)
                o_ref[rows, h * LANE:(h + 1) * LANE] = o.astype(o_ref.dtype)
            s_sc[d, h] = state


RET_CHUNKS_PER_STEP = 4


def _retention(q, k, v, decays, *, batch, seq, chunk=256):
    chunk = min(chunk, seq // RET_CHUNKS_PER_STEP)
    blk = chunk * RET_CHUNKS_PER_STEP
    n = seq // blk
    fwd = pl.BlockSpec((None, HEADS, blk, LANE), lambda b, c: (b, 0, c, 0))
    bwd = pl.BlockSpec((None, HEADS, blk, LANE), lambda b, c: (b, 0, n - 1 - c, 0))
    w = HEADS * LANE
    out = jax.ShapeDtypeStruct((batch * seq, w), BF16)
    return pl.pallas_call(
        functools.partial(_ret_kernel, chunk=chunk),
        grid=(batch, n),
        in_specs=[pl.BlockSpec(memory_space=pltpu.SMEM), fwd, fwd, fwd, bwd, bwd, bwd],
        out_specs=[pl.BlockSpec((blk, w), lambda b, c: (b * n + c, 0)),
                   pl.BlockSpec((blk, w), lambda b, c: (b * n + n - 1 - c, 0))],
        out_shape=[out, out],
        scratch_shapes=[pltpu.VMEM((2, HEADS, LANE, LANE), F32)],
        compiler_params=_cparams(("parallel", "arbitrary")),
        name="retention",
    )(decays, q, k, v, q, k, v)


GLA_SUB = 8


def _split3(x):
    x1 = x.astype(BF16)
    r1 = x - x1.astype(F32)
    x2 = r1.astype(BF16)
    x3 = (r1 - x2.astype(F32)).astype(BF16)
    return x1, x2, x3


def _gla_direction(q, k, v, lr, w2, bias, st, reverse):
    C, wk = q.shape
    wv = v.shape[1]
    dk, dv = wk // HEADS, wv // HEADS
    z = jnp.dot(lr.astype(BF16), w2, preferred_element_type=F32) + bias
    g = (jnp.minimum(z, 0.0) - jnp.log(1.0 + jnp.exp(-jnp.abs(z)))) * (LOG2E / GLA_TAU)
    ii = lax.broadcasted_iota(jnp.int32, (C, C), 0)
    jj = lax.broadcasted_iota(jnp.int32, (C, C), 1)
    tri = jnp.where(ii >= jj, 1.0, 0.0).astype(BF16)
    b = sum(jnp.dot(tri, part, preferred_element_type=F32) for part in _split3(g))
    tot = b[C - 1:C, :]
    c = (tot - b + g) if reverse else b

    qe = (q * jnp.exp2(jnp.minimum(c, 0.0))).astype(BF16)
    o = lax.dot_general(qe, st.astype(BF16), (((1,), (1,)), ((), ())), preferred_element_type=F32)
    ke = (k * jnp.exp2(jnp.minimum(tot - c, 0.0))).astype(BF16)
    upd = lax.dot_general(v.astype(BF16), ke, (((0,), (0,)), ((), ())), preferred_element_type=F32)
    rr = _div_pow2(lax.broadcasted_iota(jnp.int32, (wv, wk), 0), dv)
    cc = _div_pow2(lax.broadcasted_iota(jnp.int32, (wv, wk), 1), dk)
    new_st = jnp.where(rr == cc, st * jnp.exp2(tot) + upd, 0.0)

    lane_head = _div_pow2(lax.broadcasted_iota(jnp.int32, (C, wk), 1), dk)
    scores = [jnp.zeros((C, C), F32) for _ in range(HEADS)]
    hsz = C // 2
    while hsz >= GLA_SUB:
        blk = 2 * hsz
        rows = []
        for m in range(C // blk):
            rrow = m * blk + (hsz if reverse else hsz - 1)
            rows.append(jnp.broadcast_to(c[rrow:rrow + 1, :], (blk, wk)))
        ref = jnp.concatenate(rows, axis=0) if len(rows) > 1 else rows[0]
        qt = q * jnp.exp2(jnp.minimum(c - ref, 0.0))
        kt = (k * jnp.exp2(jnp.minimum(ref - c, 0.0))).astype(BF16)
        same = _div_pow2(ii, blk) == _div_pow2(jj, blk)
        if reverse:
            lvl = same & (_mod_pow2(ii, blk) < hsz) & (_mod_pow2(jj, blk) >= hsz)
        else:
            lvl = same & (_mod_pow2(ii, blk) >= hsz) & (_mod_pow2(jj, blk) < hsz)
        for h in range(HEADS):
            qh = jnp.where(lane_head == h, qt, 0.0).astype(BF16)
            s = lax.dot_general(qh, kt, (((1,), (1,)), ((), ())), preferred_element_type=F32)
            scores[h] = scores[h] + jnp.where(lvl, s, 0.0)
        hsz //= 2

    assert dv == C
    er = _div_pow2(lax.broadcasted_iota(jnp.int32, (wk, wv), 0), dk)
    ec = _div_pow2(lax.broadcasted_iota(jnp.int32, (wk, wv), 1), dv)
    expand = jnp.where(er == ec, 1.0, 0.0).astype(BF16)
    dist = (jj - ii) if reverse else (ii - jj)
    same_sub = _div_pow2(ii, GLA_SUB) == _div_pow2(jj, GLA_SUB)
    for lag in range(1 if reverse else 0, GLA_SUB):
        if lag == 0:
            t = q * k
        else:
            shift = (GLA_SUB - lag) if reverse else lag
            ks = pltpu.roll(k.reshape(C // GLA_SUB, GLA_SUB, wk), shift, axis=1).reshape(C, wk)
            cs = pltpu.roll(c.reshape(C // GLA_SUB, GLA_SUB, wk), shift, axis=1).reshape(C, wk)
            t = q * ks * jnp.exp2(jnp.minimum(c - cs, 0.0))
        red = jnp.dot(t.astype(BF16), expand, preferred_element_type=F32)
        on_diag = same_sub & (dist == lag)
        for h in range(HEADS):
            scores[h] = scores[h] + jnp.where(on_diag, red[:, h * dv:(h + 1) * dv], 0.0)

    vb = v.astype(BF16)
    o = o + jnp.concatenate(
        [jnp.dot(scores[h].astype(BF16), vb[:, h * dv:(h + 1) * dv], preferred_element_type=F32) for h in range(HEADS)],
        axis=1)
    return o, new_st


def _gla_kernel(qf, kf, vf, lf, qb, kb, vb, lb, w2f, bf, w2b, bb, of_ref, ob_ref, s_sc, *, qscale):
    @pl.when(pl.program_id(1) == 0)
    def _():
        s_sc[...] = jnp.zeros(s_sc.shape, F32)

    per = qf.shape[0] // GLA_CHUNK
    st_f, st_b = s_sc[0], s_sc[1]
    for s in range(per):
        rf = pl.ds(s * GLA_CHUNK, GLA_CHUNK)
        o, st_f = _gla_direction(qf[rf, :] * qscale, kf[rf, :], vf[rf, :], lf[rf, :], w2f[...], bf[...], st_f, False)
        of_ref[rf, :] = o.astype(of_ref.dtype)
        rb = pl.ds((per - 1 - s) * GLA_CHUNK, GLA_CHUNK)
        o, st_b = _gla_direction(qb[rb, :] * qscale, kb[rb, :], vb[rb, :], lb[rb, :], w2b[...], bb[...], st_b, True)
        ob_ref[rb, :] = o.astype(ob_ref.dtype)
    s_sc[0] = st_f
    s_sc[1] = st_b


GLA_CHUNK = 128
GLA_CHUNKS_PER_STEP = 4


def _gla(z, *, qcol, kcol, vcol, lcol, w2f, bf, w2b, bb, batch, seq):
    chunk = GLA_CHUNK * GLA_CHUNKS_PER_STEP
    assert seq % chunk == 0
    n = seq // chunk
    wk, wv = HEADS * GLA_K_DIM, HEADS * LANE

    def specs(cmap):
        return [pl.BlockSpec((chunk, wk), lambda b, c: (cmap(b, c), qcol)),
                pl.BlockSpec((chunk, wk), lambda b, c: (cmap(b, c), kcol)),
                pl.BlockSpec((chunk, wv), lambda b, c: (cmap(b, c), vcol)),
                pl.BlockSpec((chunk, LANE), lambda b, c: (cmap(b, c), lcol))]

    fmap = lambda b, c: b * n + c
    bmap = lambda b, c: b * n + n - 1 - c
    wspec = [pl.BlockSpec((LANE, wk), lambda b, c: (0, 0)), pl.BlockSpec((1, wk), lambda b, c: (0, 0))]
    out = jax.ShapeDtypeStruct((batch * seq, wv), BF16)
    return pl.pallas_call(
        functools.partial(_gla_kernel, qscale=GLA_K_DIM ** -0.5),
        grid=(batch, n),
        in_specs=specs(fmap) + specs(bmap) + wspec + wspec,
        out_specs=[pl.BlockSpec((chunk, wv), lambda b, c: (fmap(b, c), 0)),
                   pl.BlockSpec((chunk, wv), lambda b, c: (bmap(b, c), 0))],
        out_shape=[out, out],
        scratch_shapes=[pltpu.VMEM((2, wv, wk), F32)],
        compiler_params=_cparams(("parallel", "arbitrary")),
        name="gla",
    )(z, z, z, z, z, z, z, z, w2f, bf, w2b, bb)


def _layer_norm_rows(r, g, b):
    mu = jnp.mean(r, axis=-1, keepdims=True)
    d = r - mu
    var = jnp.mean(d * d, axis=-1, keepdims=True)
    return d * lax.rsqrt(var + LN_EPS) * g + b


def _outproj_kernel(x_ref, fin_ref, of_ref, ob_ref, gate_ref, ng_ref, wa_ref, wb_ref, lg_ref, lb_ref, rw_ref, rb_ref,
                    o_ref, ids_ref, gw_ref, cnt_ref, run_sc, *, group_norm):
    lin = of_ref[...].astype(F32) + ob_ref[...].astype(F32)
    parts = []
    for h in range(HEADS):
        zh = lin[:, h * LANE:(h + 1) * LANE]
        if group_norm:
            mu = jnp.mean(zh, axis=-1, keepdims=True)
            dz = zh - mu
            parts.append(dz * lax.rsqrt(jnp.mean(dz * dz, axis=-1, keepdims=True) + LN_EPS))
        else:
            parts.append(zh * lax.rsqrt(jnp.mean(zh * zh, axis=-1, keepdims=True) + RMS_EPS) * ng_ref[...])
    gate = gate_ref[...]
    lin = jnp.concatenate(parts, axis=1) * (gate * jax.nn.sigmoid(gate))
    y = (jnp.dot(fin_ref[...].astype(BF16), wa_ref[...], preferred_element_type=F32)
         + jnp.dot(lin.astype(BF16), wb_ref[...], preferred_element_type=F32))
    x1 = _layer_norm_rows(ALPHA * x_ref[...] + y, lg_ref[...], lb_ref[...])
    o_ref[...] = x1
    _route_rows(x1, rw_ref, rb_ref, ids_ref, gw_ref, cnt_ref, run_sc)


def _outproj(x, fin, of, ob, gate_src, gate_col, norm_gain, wa, wb, ln_g, ln_b, route_w, route_b, *, group_norm, tm=512):
    n, d = x.shape
    w = HEADS * LANE
    tm = min(tm, n)
    row = lambda i: (i, 0)
    const = lambda i: (0, 0)
    rw_hi = route_w.astype(BF16)
    rw = jnp.stack([rw_hi, (route_w - rw_hi.astype(F32)).astype(BF16)])
    return pl.pallas_call(
        functools.partial(_outproj_kernel, group_norm=group_norm),
        grid=(n // tm,),
        in_specs=[pl.BlockSpec((tm, d), row), pl.BlockSpec((tm, w), row), pl.BlockSpec((tm, w), row),
                  pl.BlockSpec((tm, w), row), pl.BlockSpec((tm, w), lambda i: (i, gate_col)),
                  pl.BlockSpec((1, LANE), const), pl.BlockSpec((w, d), const), pl.BlockSpec((w, d), const),
                  pl.BlockSpec((1, d), const), pl.BlockSpec((1, d), const),
                  pl.BlockSpec((2, d, LANE), lambda i: (0, 0, 0)), pl.BlockSpec((1, LANE), const)],
        out_specs=[pl.BlockSpec((tm, d), row), pl.BlockSpec((tm, LANE), row), pl.BlockSpec((tm, LANE), row),
                   pl.BlockSpec((1, LANE), const)],
        out_shape=[jax.ShapeDtypeStruct((n, d), F32), jax.ShapeDtypeStruct((n, LANE), jnp.int32),
                   jax.ShapeDtypeStruct((n, LANE), F32), jax.ShapeDtypeStruct((1, LANE), jnp.int32)],
        scratch_shapes=[pltpu.VMEM((1, LANE), F32)],
        compiler_params=_cparams(("arbitrary",)),
        name="outproj",
    )(x, fin, of, ob, gate_src, norm_gain.reshape(1, LANE), wa, wb, ln_g.reshape(1, d), ln_b.reshape(1, d), rw, route_b)


def _route_rows(x, w_ref, b_ref, ids_ref, gw_ref, cnt_ref, run_sc):
    @pl.when(pl.program_id(0) == 0)
    def _():
        run_sc[...] = jnp.zeros(run_sc.shape, F32)

    tm = x.shape[0]
    xh = x.astype(BF16)
    xl = (x - xh.astype(F32)).astype(BF16)
    wh, wl = w_ref[0], w_ref[1]
    logits = (jnp.dot(xh, wh, preferred_element_type=F32) + jnp.dot(xh, wl, preferred_element_type=F32)
              + jnp.dot(xl, wh, preferred_element_type=F32)) + b_ref[...]
    lane = lax.broadcasted_iota(jnp.int32, logits.shape, 1)
    neg = -jnp.inf
    gmask = (lane >= N_EXPERTS) & (lane < N_EXPERTS + N_GROUPS)
    gl = jnp.where(gmask, logits, neg)
    gmax = jnp.max(gl, axis=1, keepdims=True)
    lane_f = lane.astype(F32)
    first = lambda hit: jnp.min(jnp.where(hit, lane_f, float(LANE)), axis=1, keepdims=True).astype(jnp.int32)
    gidx = first(gl == gmax) - N_EXPERTS
    p_grp = 1.0 / jnp.sum(jnp.where(gmask, jnp.exp(gl - gmax), 0.0), axis=1, keepdims=True)
    el = jnp.where(_div_pow2(lane, EXPERTS_PER_GROUP) == gidx, logits, neg)
    l1 = jnp.max(el, axis=1, keepdims=True)
    e1 = first(el == l1)
    el2 = jnp.where(lane == e1, neg, el)
    l2 = jnp.max(el2, axis=1, keepdims=True)
    e2 = first(el2 == l2)
    t = jnp.exp(l2 - l1)
    w1 = p_grp / (1.0 + t)
    w2 = p_grp * t / (1.0 + t)

    onehot = jnp.where(lane == e1, 1.0, jnp.where(lane == e2, 1.0, 0.0))
    ri = lax.broadcasted_iota(jnp.int32, (tm, tm), 0)
    ci = lax.broadcasted_iota(jnp.int32, (tm, tm), 1)
    before = jnp.dot(jnp.where(ri > ci, 1.0, 0.0).astype(BF16), onehot.astype(BF16), preferred_element_type=F32)
    before = before + run_sc[...]
    r1 = jnp.sum(jnp.where(lane == e1, before, 0.0), axis=1, keepdims=True).astype(jnp.int32)
    r2 = jnp.sum(jnp.where(lane == e2, before, 0.0), axis=1, keepdims=True).astype(jnp.int32)
    run_sc[...] = run_sc[...] + jnp.sum(onehot, axis=0, keepdims=True)
    cnt_ref[...] = run_sc[...].astype(jnp.int32)
    ids_ref[...] = jnp.where(lane == 0, e1, jnp.where(lane == 1, e2, jnp.where(lane == 2, r1, jnp.where(lane == 3, r2, 0))))
    gw_ref[...] = jnp.where(lane == 0, w1, jnp.where(lane == 1, w2, 0.0))


MOE_BM = 256
SUBL = 8


def _rows_from_linear(ref, rows):
    return jnp.concatenate([ref[pl.ds(s, rows, stride=SUBL), :] for s in range(SUBL)], axis=1)


def _rows_to_linear(ref, val):
    for s in range(SUBL):
        ref[pl.ds(s, val.shape[0], stride=SUBL), :] = val[:, s * LANE:(s + 1) * LANE]


def _dispatch_kernel(pend_ref, padded_ref, dest_ref, x_ref, xs_hbm, idx_smem, zbuf, lin, sem_i, sem_z, sem):
    i = pl.program_id(0)
    tm = x_ref.shape[0]
    bm = zbuf.shape[0] // SUBL

    @pl.when(i == 0)
    def _():
        zbuf[...] = jnp.zeros(zbuf.shape, F32)

        def tail(e):
            start_row = pl.multiple_of((pend_ref[e] - bm) * SUBL, bm * SUBL)
            return pltpu.make_async_copy(zbuf, xs_hbm.at[pl.ds(start_row, bm * SUBL), :], sem_z)

        def start(e, carry):
            @pl.when(padded_ref[e] > 0)
            def _():
                tail(e).start()
            return carry

        def wait(e, carry):
            @pl.when(padded_ref[e] > 0)
            def _():
                tail(e).wait()
            return carry

        lax.fori_loop(0, N_EXPERTS, start, 0)
        lax.fori_loop(0, N_EXPERTS, wait, 0)

        def unused(b):
            start_row = pl.multiple_of(b * bm * SUBL, bm * SUBL)
            return pltpu.make_async_copy(zbuf, xs_hbm.at[pl.ds(start_row, bm * SUBL), :], sem_z)

        first_unused = pend_ref[N_EXPERTS - 1] // bm
        n_blocks = xs_hbm.shape[0] // (bm * SUBL)
        lax.fori_loop(first_unused, n_blocks, lambda b, c: (unused(b).start(), c)[1], 0)
        lax.fori_loop(first_unused, n_blocks, lambda b, c: (unused(b).wait(), c)[1], 0)

    cp = pltpu.make_async_copy(dest_ref.at[i], idx_smem, sem_i)
    cp.start()
    _rows_to_linear(lin, x_ref[...])
    cp.wait()

    def scatter(r, carry):
        src = lin.at[pl.ds(pl.multiple_of(r * SUBL, SUBL), SUBL), :]
        for k in range(2):
            dst = pl.multiple_of(idx_smem[2 * r + k] * SUBL, SUBL)
            pltpu.make_async_copy(src, xs_hbm.at[pl.ds(dst, SUBL), :], sem).start(priority=k)
        return carry

    lax.fori_loop(0, tm, scatter, 0, unroll=8)
    for k in range(2):
        pltpu.make_async_copy(lin, xs_hbm.at[pl.ds(0, tm * SUBL), :], sem).wait()


def _dispatch(x, dest, pend, padded, cap, *, tm=1024):
    n, d = x.shape
    assert d == SUBL * LANE
    tm = min(tm, n)
    nt = n // tm
    grid_spec = pltpu.PrefetchScalarGridSpec(
        num_scalar_prefetch=2,
        grid=(nt,),
        in_specs=[pl.BlockSpec((nt, 2 * tm), lambda i, pe, pa: (0, 0)),
                  pl.BlockSpec((tm, d), lambda i, pe, pa: (i, 0))],
        out_specs=pl.BlockSpec(memory_space=pl.ANY),
        scratch_shapes=[pltpu.SMEM((2 * tm,), jnp.int32), pltpu.VMEM((MOE_BM * SUBL, LANE), F32),
                        pltpu.VMEM((tm * SUBL, LANE), F32),
                        pltpu.SemaphoreType.DMA(()), pltpu.SemaphoreType.DMA(()), pltpu.SemaphoreType.DMA(())],
    )
    return pl.pallas_call(
        _dispatch_kernel,
        grid_spec=grid_spec,
        out_shape=jax.ShapeDtypeStruct((cap * SUBL, LANE), F32),
        compiler_params=_cparams(("arbitrary",)),
        name="dispatch",
    )(pend, padded, dest.reshape(nt, 2 * tm), x)


def _experts_kernel(blk_e_ref, nused_ref, xs_ref, wg_ref, wu_ref, wd_ref, ys_ref, wgb, wub, wdb, hbuf):
    i = pl.program_id(0)
    nb = pl.num_programs(0) - 1
    cur = jnp.minimum(i, nb - 1)
    prv = jnp.maximum(i - 1, 0)

    @pl.when(i == 0)
    def _():
        hbuf[...] = jnp.zeros(hbuf.shape, BF16)

    @pl.when(jnp.logical_or(i == 0, blk_e_ref[cur] != blk_e_ref[jnp.minimum(prv, nb - 1)]))
    def _():
        wgb[...] = wg_ref[...].astype(BF16)
        wub[...] = wu_ref[...].astype(BF16)

    @pl.when(jnp.logical_or(i == 0, blk_e_ref[prv] != blk_e_ref[jnp.maximum(i - 2, 0)]))
    def _():
        wdb[...] = wd_ref[...].astype(BF16)

    slot = lax.rem(i, 2)
    y = jnp.dot(hbuf[1 - slot], wdb[...], preferred_element_type=F32)
    live = jnp.logical_and(i >= 1, i - 1 < nused_ref[0])
    _rows_to_linear(ys_ref, jnp.where(live, y, 0.0))
    xb = _rows_from_linear(xs_ref, xs_ref.shape[0] // SUBL).astype(BF16)
    hg = jnp.dot(xb, wgb[...], preferred_element_type=F32)
    hu = jnp.dot(xb, wub[...], preferred_element_type=F32)
    hbuf[slot] = (hg * jax.nn.sigmoid(hg) * hu).astype(BF16)


def _experts(xs, blk_e, n_used, w_gate, w_up, w_down, layer):
    cap = xs.shape[0] // SUBL
    bm = MOE_BM
    nb = cap // bm
    d, de = w_gate.shape[2], w_gate.shape[3]
    row_in = lambda i, be, nu: (jnp.minimum(i, nu[0] - 1), 0)
    row = lambda i, be, nu: (jnp.maximum(i - 1, 0), 0)
    w_cur = lambda i, be, nu: (layer, be[jnp.minimum(i, nb - 1)], 0, 0)
    w_prv = lambda i, be, nu: (layer, be[jnp.maximum(i - 1, 0)], 0, 0)
    grid_spec = pltpu.PrefetchScalarGridSpec(
        num_scalar_prefetch=2,
        grid=(nb + 1,),
        in_specs=[pl.BlockSpec((bm * SUBL, LANE), row_in),
                  pl.BlockSpec((None, None, d, de), w_cur),
                  pl.BlockSpec((None, None, d, de), w_cur),
                  pl.BlockSpec((None, None, de, d), w_prv)],
        out_specs=pl.BlockSpec((bm * SUBL, LANE), row),
        scratch_shapes=[pltpu.VMEM((d, de), BF16), pltpu.VMEM((d, de), BF16), pltpu.VMEM((de, d), BF16),
                        pltpu.VMEM((2, bm, de), BF16)],
    )
    return pl.pallas_call(
        _experts_kernel,
        grid_spec=grid_spec,
        out_shape=jax.ShapeDtypeStruct((cap * SUBL, LANE), F32),
        compiler_params=_cparams(("arbitrary",)),
        name="experts",
    )(blk_e, n_used, xs, w_gate, w_up, w_down)


def _combine_kernel(dest_ref, x_ref, gw_ref, g_ref, b_ref, ys_hbm, o_ref, idx_smem, ybuf, sem_i, sem):
    i = pl.program_id(0)
    tm = x_ref.shape[0]

    def issue(tile, slot):
        cp = pltpu.make_async_copy(dest_ref.at[tile], idx_smem, sem_i)
        cp.start()
        cp.wait()

        def gather(r, carry):
            row = pl.multiple_of(r * SUBL, SUBL)
            for k in range(2):
                src = pl.multiple_of(idx_smem[2 * r + k] * SUBL, SUBL)
                pltpu.make_async_copy(ys_hbm.at[pl.ds(src, SUBL), :], ybuf.at[slot, k, pl.ds(row, SUBL), :],
                                      sem.at[slot]).start(priority=k)
            return carry

        lax.fori_loop(0, tm, gather, 0, unroll=8)

    slot = lax.rem(i, 2)

    @pl.when(i == 0)
    def _():
        issue(0, 0)

    @pl.when(i + 1 < pl.num_programs(0))
    def _():
        issue(i + 1, 1 - slot)

    for k in range(2):
        pltpu.make_async_copy(ys_hbm.at[pl.ds(0, tm * SUBL), :], ybuf.at[slot, k], sem.at[slot]).wait()
    gw = gw_ref[...]
    ffn = (_rows_from_linear(ybuf.at[slot, 0], tm) * gw[:, 0:1] + _rows_from_linear(ybuf.at[slot, 1], tm) * gw[:, 1:2])
    o_ref[...] = _layer_norm_rows(ALPHA * x_ref[...] + ffn, g_ref[...], b_ref[...])


def _combine(x, ys, dest, gw, ln_g, ln_b, *, tm=1024):
    n, d = x.shape
    tm = min(tm, n)
    nt = n // tm
    return pl.pallas_call(
        _combine_kernel,
        grid=(nt,),
        in_specs=[pl.BlockSpec((nt, 2 * tm), lambda i: (0, 0)),
                  pl.BlockSpec((tm, d), lambda i: (i, 0)), pl.BlockSpec((tm, LANE), lambda i: (i, 0)),
                  pl.BlockSpec((1, d), lambda i: (0, 0)), pl.BlockSpec((1, d), lambda i: (0, 0)),
                  pl.BlockSpec(memory_space=pl.ANY)],
        out_specs=pl.BlockSpec((tm, d), lambda i: (i, 0)),
        out_shape=jax.ShapeDtypeStruct((n, d), F32),
        scratch_shapes=[pltpu.SMEM((2 * tm,), jnp.int32), pltpu.VMEM((2, 2, tm * SUBL, LANE), F32),
                        pltpu.SemaphoreType.DMA(()), pltpu.SemaphoreType.DMA((2,))],
        compiler_params=_cparams(("arbitrary",)),
        name="combine",
    )(dest.reshape(nt, 2 * tm), x, gw, ln_g.reshape(1, d), ln_b.reshape(1, d), ys)


def _router_params(w_grp, b_grp, w_exp, b_exp):
    d = w_exp.shape[0]
    wr = jnp.zeros((d, LANE), F32).at[:, :N_EXPERTS].set(w_exp).at[:, N_EXPERTS:N_EXPERTS + N_GROUPS].set(w_grp)
    br = jnp.zeros((1, LANE), F32).at[0, :N_EXPERTS].set(b_exp).at[0, N_EXPERTS:N_EXPERTS + N_GROUPS].set(b_grp)
    return wr, br


def _moe(x, routing, w_gate, w_up, w_down, layer, ln_g, ln_b):
    n, d = x.shape
    ids, gw, cnt = routing
    bm = MOE_BM
    counts = cnt[0, :N_EXPERTS]
    padded = (counts + bm - 1) // bm * bm
    pend = jnp.cumsum(padded)
    pstart = pend - padded
    e, r = ids[:, 0:2], ids[:, 2:4]
    onehot = e[:, :, None] == jnp.arange(N_EXPERTS, dtype=jnp.int32)[None, None, :]
    dest = jnp.sum(jnp.where(onehot, pstart[None, None, :], 0), axis=-1) + r
    cap = 2 * n + N_EXPERTS * bm
    nb = cap // bm
    blk_start = jnp.arange(nb, dtype=jnp.int32) * bm
    blk_e = jnp.minimum(jnp.sum((pend[None, :] <= blk_start[:, None]).astype(jnp.int32), axis=1), N_EXPERTS - 1)
    n_used = (pend[-1:] // bm).astype(jnp.int32)
    xs = _dispatch(x, dest, pend.astype(jnp.int32), padded.astype(jnp.int32), cap)
    ys = _experts(xs, blk_e, n_used, w_gate, w_up, w_down, layer)
    return _combine(x, ys, dest, gw, ln_g, ln_b)


def _even_layer(x, batch, seq, layer_idx, w_in, dec_f, dec_b, lq1, lk1, lq2, lk2, subln, w_out, ln_g, ln_b, route):
    d = x.shape[1]
    w = HEADS * LANE
    kw = dict(batch=batch, seq=seq)
    diff_seg = [(0, DIFF_ROT_DIM), (DIFF_HEAD_DIM, DIFF_ROT_DIM)]
    q, k, v, gate, dq, dk, dvt = _even_in(
        x, w_in.astype(BF16), _rope_tables(seq, [(0, LANE)], RET_THETA),
        _rope_tables(seq, diff_seg, ROPE_THETA), **kw)
    decays = jnp.stack([dec_f, dec_b]).astype(F32)
    of, ob = _retention(q, k, v, decays, **kw)
    lam_init = 0.8 - 0.6 * math.exp(-0.3 * layer_idx)
    diff = _flash(dq, dk, dvt, diff=(lq1, lk1, lq2, lk2, subln), lam_init=lam_init, tq=2048, **kw)
    wo = w_out.astype(BF16)
    return _outproj(x, diff, of, ob, gate, 0, jnp.ones((LANE,), F32), wo[w:], wo[:w], ln_g, ln_b, *route,
                    group_norm=True)


def _odd_layer(x, batch, seq, w_in, q_norm, w_uq, kv_norm, w_ukv, w2_f, b_f, w2_b, b_b, gla_norm, w_out, ln_g, ln_b,
               route):
    d = x.shape[1]
    w = HEADS * LANE
    o = np.cumsum([0, MLA_Q_RANK, MLA_KV_RANK, MLA_ROPE, HEADS * GLA_K_DIM, HEADS * GLA_K_DIM, w, w,
                   GLA_GATE_RANK, GLA_GATE_RANK]).tolist()
    zeros = lambda c: jnp.zeros((d, c), F32)
    w_in2 = jnp.concatenate([
        w_in[:, o[0]:o[2]], zeros(MLA_NOPE), w_in[:, o[2]:o[3]], zeros(LANE - MLA_NOPE - MLA_ROPE),
        w_in[:, o[3]:o[7]], w_in[:, o[7]:o[9]], zeros(LANE - 2 * GLA_GATE_RANK)], axis=1).astype(BF16)
    kw = dict(batch=batch, seq=seq)
    qd = MLA_NOPE + MLA_ROPE
    w_uq2 = jnp.pad(w_uq.reshape(MLA_Q_RANK, HEADS, qd), ((0, 0), (0, 0), (0, LANE - qd))).reshape(MLA_Q_RANK, w)
    ukv = w_ukv.reshape(MLA_KV_RANK, HEADS, MLA_NOPE + MLA_V)
    w_uk2 = jnp.pad(ukv[:, :, :MLA_NOPE], ((0, 0), (0, 0), (0, LANE - MLA_NOPE))).reshape(MLA_KV_RANK, w)
    w_uv2 = ukv[:, :, MLA_NOPE:].reshape(MLA_KV_RANK, w)
    q, k, vt, zg = _odd_in(x, w_in2, q_norm, w_uq2.astype(BF16), kv_norm,
                           jnp.concatenate([w_uk2, w_uv2], axis=1).astype(BF16),
                           _rope_tables(seq, [(MLA_NOPE, MLA_ROPE)], ROPE_THETA), **kw)
    mla = _flash(q, k, vt, passes=4, **kw)
    wk = HEADS * GLA_K_DIM
    pad_rows = lambda m, r0: jnp.zeros((LANE, wk), F32).at[r0:r0 + GLA_GATE_RANK].set(m).astype(BF16)
    of, ob = _gla(zg, qcol=0, kcol=1, vcol=1, lcol=12,
                  w2f=pad_rows(w2_f, 0), bf=b_f.reshape(1, wk), w2b=pad_rows(w2_b, GLA_GATE_RANK), bb=b_b.reshape(1, wk), **kw)
    wo = w_out.astype(BF16)
    return _outproj(x, mla, of, ob, zg, 2, gla_norm, wo[:w], wo[w:], ln_g, ln_b, *route, group_norm=False)


def kernel(x, ev_w_in, ev_ret_decay_f, ev_ret_decay_b, ev_lq1, ev_lk1, ev_lq2, ev_lk2, ev_subln, ev_w_out, od_w_in, od_q_norm, od_w_uq, od_kv_norm, od_w_ukv, od_gla_w2_f, od_gla_b_f, od_gla_w2_b, od_gla_b_b, od_gla_norm, od_w_out, ln1_g, ln1_b, ln2_g, ln2_b, moe_w_grp, moe_b_grp, moe_w_exp, moe_b_exp, moe_w_gate, moe_w_up, moe_w_down):
    batch, seq, d = x.shape
    h = x.reshape(batch * seq, d)
    for i in range(DEPTH):
        j = i // 2
        route = _router_params(moe_w_grp[i], moe_b_grp[i], moe_w_exp[i], moe_b_exp[i])
        if i % 2 == 0:
            h, *routing = _even_layer(h, batch, seq, i, ev_w_in[j], ev_ret_decay_f[j], ev_ret_decay_b[j], ev_lq1[j],
                                      ev_lk1[j], ev_lq2[j], ev_lk2[j], ev_subln[j], ev_w_out[j], ln1_g[i], ln1_b[i], route)
        else:
            h, *routing = _odd_layer(h, batch, seq, od_w_in[j], od_q_norm[j], od_w_uq[j], od_kv_norm[j], od_w_ukv[j],
                                     od_gla_w2_f[j], od_gla_b_f[j], od_gla_w2_b[j], od_gla_b_b[j], od_gla_norm[j],
                                     od_w_out[j], ln1_g[i], ln1_b[i], route)
        h = _moe(h, routing, moe_w_gate, moe_w_up, moe_w_down, i, ln2_g[i], ln2_b[i])
    return h.reshape(batch, seq, d)
```

```python
import functools
import math

import numpy as np
import jax
import jax.numpy as jnp
from jax import lax
from jax.experimental import pallas as pl
from jax.experimental.pallas import tpu as pltpu

F32 = jnp.float32
BF16 = jnp.bfloat16

HEADS = 4
LANE = 128
RET_THETA = 10000.0
ROPE_THETA = 500000.0
DIFF_HEAD_DIM = 64
DIFF_ROT_DIM = 16
MLA_Q_RANK = 256
MLA_KV_RANK = 128
MLA_NOPE = 64
MLA_ROPE = 32
MLA_V = 128
GLA_K_DIM = 64
GLA_GATE_RANK = 16
GLA_TAU = 16.0
N_GROUPS = 4
EXPERTS_PER_GROUP = 8
N_EXPERTS = N_GROUPS * EXPERTS_PER_GROUP
DEPTH = 2
ALPHA = (2.0 * DEPTH) ** 0.25
LN_EPS = 1e-5
RMS_EPS = 1e-6

VMEM_LIMIT = 48 * 1024 * 1024


def _div_pow2(x, n):
    return lax.shift_right_logical(x, int(n).bit_length() - 1)


def _mod_pow2(x, n):
    return lax.bitwise_and(x, int(n) - 1)


def _cparams(sem):
    return pltpu.CompilerParams(dimension_semantics=sem, vmem_limit_bytes=VMEM_LIMIT)


def _rope_heads(z, tabs, sh, scale):
    c, sa, sb = tabs
    outs = []
    for h in range(HEADS):
        zh = z[:, h * LANE:(h + 1) * LANE]
        outs.append((zh * c + pltpu.roll(zh, sh, axis=1) * sa + pltpu.roll(zh, LANE - sh, axis=1) * sb) * scale)
    return outs


def _even_in_kernel(x_ref, w_ref, rc, rsa, rsb, dc, dsa, dsb, q_ref, k_ref, v_ref, g_ref, dq_ref, dk_ref, dvt_ref):
    w = HEADS * LANE
    xb = x_ref[...].astype(BF16)
    part = lambda t: jnp.dot(xb, w_ref[:, t * w:(t + 1) * w], preferred_element_type=F32)
    ret_t = (rc[...], rsa[...], rsb[...])
    diff_t = (dc[...], dsa[...], dsb[...])
    for h, o in enumerate(_rope_heads(part(0), ret_t, LANE // 2, 1.0)):
        q_ref[h] = o.astype(BF16)
    for h, o in enumerate(_rope_heads(part(1), ret_t, LANE // 2, LANE ** -0.5)):
        k_ref[h] = o.astype(BF16)
    rv = part(2)
    for h in range(HEADS):
        v_ref[h] = rv[:, h * LANE:(h + 1) * LANE].astype(BF16)
    g_ref[...] = part(3)
    for h, o in enumerate(_rope_heads(part(4), diff_t, DIFF_ROT_DIM // 2, DIFF_HEAD_DIM ** -0.5 * LOG2E)):
        dq_ref[h] = o.T.astype(BF16)
    for h, o in enumerate(_rope_heads(part(5), diff_t, DIFF_ROT_DIM // 2, 1.0)):
        dk_ref[h] = o.astype(BF16)
    dv = part(6)
    for h in range(HEADS):
        dvt_ref[h, :LANE, :] = dv[:, h * LANE:(h + 1) * LANE].T.astype(BF16)
        dvt_ref[h, LANE:, :] = jnp.ones((ONES_ROWS, dvt_ref.shape[2]), BF16)


def _even_in(x, w, ret_tabs, diff_tabs, *, batch, seq):
    n, d = x.shape
    tm = min(FLASH_TK_DIFF, seq // 2)
    nt = seq // tm
    hw = HEADS * LANE
    heads = jax.ShapeDtypeStruct((batch, HEADS, seq, LANE), BF16)
    head_spec = pl.BlockSpec((None, HEADS, tm, LANE), lambda i: (i // nt, 0, i % nt, 0))
    heads_t = jax.ShapeDtypeStruct((batch, HEADS, LANE, seq), BF16)
    head_t_spec = pl.BlockSpec((None, HEADS, LANE, tm), lambda i: (i // nt, 0, 0, i % nt))
    tab_spec = pl.BlockSpec((tm, LANE), lambda i: (i % nt, 0))
    return pl.pallas_call(
        _even_in_kernel,
        grid=(n // tm,),
        in_specs=[pl.BlockSpec((tm, d), lambda i: (i, 0)), pl.BlockSpec((d, 7 * hw), lambda i: (0, 0))] + [tab_spec] * 6,
        out_specs=[head_spec, head_spec, head_spec, pl.BlockSpec((tm, hw), lambda i: (i, 0)), head_t_spec, head_spec,
                   pl.BlockSpec((None, HEADS, None, LANE + ONES_ROWS, tm), lambda i: (i // nt, 0, i % nt, 0, 0))],
        out_shape=[heads, heads, heads, jax.ShapeDtypeStruct((n, hw), F32), heads_t, heads,
                   jax.ShapeDtypeStruct((batch, HEADS, nt, LANE + ONES_ROWS, tm), BF16)],
        compiler_params=_cparams(("parallel",)),
        name="even_in",
    )(x, w, *ret_tabs, *diff_tabs)


def _rms_rows(z, g):
    return z * lax.rsqrt(jnp.mean(z * z, axis=-1, keepdims=True) + RMS_EPS) * g


def _odd_in_kernel(x_ref, w_ref, qn_ref, wq_ref, kvn_ref, wkv_ref, tc, tsa, tsb, q_ref, k_ref, vt_ref, zg_ref):
    hw = HEADS * LANE
    mla_w = MLA_Q_RANK + MLA_KV_RANK + LANE
    xb = x_ref[...].astype(BF16)
    zg_ref[...] = jnp.dot(xb, w_ref[:, mla_w:], preferred_element_type=F32)
    z1 = jnp.dot(xb, w_ref[:, :mla_w], preferred_element_type=F32)
    tabs = (tc[...], tsa[...], tsb[...])
    sh = MLA_ROPE // 2
    qh = jnp.dot(_rms_rows(z1[:, :MLA_Q_RANK], qn_ref[...]).astype(BF16), wq_ref[...], preferred_element_type=F32)
    for h, o in enumerate(_rope_heads(qh, tabs, sh, (MLA_NOPE + MLA_ROPE) ** -0.5 * LOG2E)):
        q_ref[h] = o.T.astype(BF16)
    ckv = _rms_rows(z1[:, MLA_Q_RANK:MLA_Q_RANK + MLA_KV_RANK], kvn_ref[...]).astype(BF16)
    kv = jnp.dot(ckv, wkv_ref[...], preferred_element_type=F32)
    kr = z1[:, MLA_Q_RANK + MLA_KV_RANK:]
    kr = kr * tabs[0] + pltpu.roll(kr, sh, axis=1) * tabs[1] + pltpu.roll(kr, LANE - sh, axis=1) * tabs[2]
    for h in range(HEADS):
        k_ref[h] = (kv[:, h * LANE:(h + 1) * LANE] + kr).astype(BF16)
        vt_ref[h, :LANE, :] = kv[:, hw + h * LANE:hw + (h + 1) * LANE].T.astype(BF16)
        vt_ref[h, LANE:, :] = jnp.ones((ONES_ROWS, vt_ref.shape[2]), BF16)


def _odd_in(x, w, q_norm, w_uq, kv_norm, w_ukv, tabs, *, batch, seq, tm=512):
    n, d = x.shape
    tm = min(tm, seq // 2)
    nt = seq // tm
    tk = min(FLASH_TK, seq // 2)
    per = tk // tm
    hw = HEADS * LANE
    gw_ = w.shape[1] - (MLA_Q_RANK + MLA_KV_RANK + LANE)
    heads = jax.ShapeDtypeStruct((batch, HEADS, seq, LANE), BF16)
    head_spec = pl.BlockSpec((None, HEADS, tm, LANE), lambda i: (i // nt, 0, i % nt, 0))
    tab_spec = pl.BlockSpec((tm, LANE), lambda i: (i % nt, 0))
    const = lambda i: (0, 0)
    return pl.pallas_call(
        _odd_in_kernel,
        grid=(n // tm,),
        in_specs=[pl.BlockSpec((tm, d), lambda i: (i, 0)), pl.BlockSpec(w.shape, const),
                  pl.BlockSpec((1, MLA_Q_RANK), const), pl.BlockSpec(w_uq.shape, const),
                  pl.BlockSpec((1, MLA_KV_RANK), const), pl.BlockSpec(w_ukv.shape, const)] + [tab_spec] * 3,
        out_specs=[pl.BlockSpec((None, HEADS, LANE, tm), lambda i: (i // nt, 0, 0, i % nt)), head_spec,
                   pl.BlockSpec((None, HEADS, None, LANE + ONES_ROWS, tm),
                                lambda i: (i // nt, 0, (i % nt) // per, 0, (i % nt) % per)),
                   pl.BlockSpec((tm, gw_), lambda i: (i, 0))],
        out_shape=[jax.ShapeDtypeStruct((batch, HEADS, LANE, seq), BF16), heads, jax.ShapeDtypeStruct((batch, HEADS, seq // tk, LANE + ONES_ROWS, tk), BF16),
                   jax.ShapeDtypeStruct((n, gw_), F32)],
        compiler_params=_cparams(("parallel",)),
        name="odd_in",
    )(x, w, q_norm.reshape(1, -1), w_uq, kv_norm.reshape(1, -1), w_ukv, *tabs)


def _rope_tables(seq, segs, theta):
    pos = jnp.arange(seq, dtype=F32)
    inv = jnp.zeros((LANE,), F32)
    lo = np.zeros((LANE,), bool)
    hi = np.zeros((LANE,), bool)
    for start, rot in segs:
        half = rot // 2
        f = jnp.power(jnp.float32(theta), -jnp.arange(0, rot, 2, dtype=F32) / rot)
        inv = inv.at[start:start + half].set(f).at[start + half:start + rot].set(f)
        lo[start:start + half] = True
        hi[start + half:start + rot] = True
    ang = pos[:, None] * inv[None, :]
    cos, sin = jnp.cos(ang), jnp.sin(ang)
    c = jnp.where(lo | hi, cos, 1.0)
    sa = jnp.where(hi, sin, 0.0)
    sb = jnp.where(lo, -sin, 0.0)
    return c, sa, sb


ONES_ROWS = 16
LOG2E = math.log2(math.e)
FLASH_TK = 1024
FLASH_TK_DIFF = 512


def _flash_kernel(*refs, ncomp, nk, lam_init):
    if ncomp == 2:
        q_ref, k_ref, vt_ref, lq1, lk1, lq2, lk2, g_ref, o_ref, *scr = refs
    else:
        q_ref, k_ref, vt_ref, o_ref, *scr = refs
    qm_sc, m_sc, acc_sc, s0, s1, cm0, cm1, p0, p1, al0, al1 = scr
    tk = s0.shape[1]
    tq = qm_sc.shape[2]

    def start(sub):
        q = q_ref[:, sub * tq:(sub + 1) * tq]
        if ncomp == 2:
            chan = lax.broadcasted_iota(jnp.int32, q.shape, 0)
            zero = jnp.zeros_like(q)
            qm_sc[0] = jnp.where(chan < DIFF_HEAD_DIM, q, zero)
            qm_sc[1] = jnp.where(chan >= DIFF_HEAD_DIM, q, zero)
        else:
            qm_sc[0] = q
        m_sc[...] = jnp.full(m_sc.shape, -jnp.inf, F32)
        acc_sc[...] = jnp.zeros(acc_sc.shape, F32)

    def scores(j, s_ref, cm_ref):
        k = k_ref[j * tk:(j + 1) * tk, :]
        for c in range(ncomp):
            s = jnp.dot(k, qm_sc[c], preferred_element_type=F32)
            s_ref[c] = s
            cm_ref[c] = jnp.max(s, axis=0, keepdims=True)

    def softmax(s_ref, cm_ref, p_ref, al_ref):
        for c in range(ncomp):
            m_old = m_sc[c]
            m_new = jnp.maximum(m_old, cm_ref[c])
            al_ref[c] = jnp.exp2(m_old - m_new)
            p_ref[c] = jnp.exp2(s_ref[c] - m_new).astype(BF16)
            m_sc[c] = m_new

    def values(j, p_ref, al_ref):
        vt = vt_ref[j]
        for c in range(ncomp):
            acc_sc[c] = al_ref[c] * acc_sc[c] + jnp.dot(vt, p_ref[c], preferred_element_type=F32)

    def normalised(c):
        acc = acc_sc[c]
        return acc[:LANE] / acc[LANE:LANE + 1]

    bufs = ((s0, cm0, p0, al0), (s1, cm1, p1, al1))
    for sub in range(q_ref.shape[1] // tq):
        start(sub)
        scores(0, s0, cm0)
        for j in range(nk):
            s_c, cm_c, p_c, al_c = bufs[j % 2]
            s_n, cm_n, p_n, al_n = bufs[(j + 1) % 2]
            if j + 1 < nk:
                scores(j + 1, s_n, cm_n)
            softmax(s_c, cm_c, p_c, al_c)
            if j >= 1:
                values(j - 1, p_n, al_n)
        values(nk - 1, *bufs[(nk - 1) % 2][2:])
        o = normalised(0)
        if ncomp == 2:
            lam = (jnp.exp(jnp.sum(lq1[...] * lk1[...], keepdims=True))
                   - jnp.exp(jnp.sum(lq2[...] * lk2[...], keepdims=True)) + lam_init)
            o = o - lam * normalised(1)
            o = o * lax.rsqrt(jnp.mean(o * o, axis=0, keepdims=True) + RMS_EPS) * g_ref[...] * (1.0 - lam_init)
        o_ref[sub * tq:(sub + 1) * tq, :] = o.T.astype(o_ref.dtype)


def _flash(q, k, vt, *, batch, seq, tq=512, passes=1, diff=None, lam_init=0.0):
    nk, vrows, tk = vt.shape[2], vt.shape[3], vt.shape[4]
    assert vrows == LANE + ONES_ROWS
    tq = min(tq, seq)
    passes = min(passes, seq // tq)
    bq = tq * passes
    nq = seq // bq
    ncomp = 2 if diff is not None else 1
    in_specs = [
        pl.BlockSpec((None, None, LANE, bq), lambda b, h, i: (b, h, 0, i)),
        pl.BlockSpec((None, None, seq, LANE), lambda b, h, i: (b, h, 0, 0)),
        pl.BlockSpec((None, None, nk, vrows, tk), lambda b, h, i: (b, h, 0, 0, 0)),
    ]
    args = [q, k, vt]
    if diff is not None:
        lq1, lk1, lq2, lk2, subln = diff
        for v in (lq1, lk1, lq2, lk2):
            in_specs.append(pl.BlockSpec((1, DIFF_HEAD_DIM), lambda b, h, i: (0, 0)))
            args.append(v.reshape(1, DIFF_HEAD_DIM))
        in_specs.append(pl.BlockSpec((LANE, 1), lambda b, h, i: (0, 0)))
        args.append(subln.reshape(LANE, 1))
    return pl.pallas_call(
        functools.partial(_flash_kernel, ncomp=ncomp, nk=nk, lam_init=lam_init),
        grid=(batch, HEADS, nq),
        in_specs=in_specs,
        out_specs=pl.BlockSpec((bq, LANE), lambda b, h, i: (b * nq + i, h)),
        out_shape=jax.ShapeDtypeStruct((batch * seq, HEADS * LANE), BF16),
        scratch_shapes=[pltpu.VMEM((ncomp, LANE, tq), BF16),
                        pltpu.VMEM((ncomp, 1, tq), F32), pltpu.VMEM((ncomp, vrows, tq), F32),
                        pltpu.VMEM((ncomp, tk, tq), F32), pltpu.VMEM((ncomp, tk, tq), F32),
                        pltpu.VMEM((ncomp, 1, tq), F32), pltpu.VMEM((ncomp, 1, tq), F32),
                        pltpu.VMEM((ncomp, tk, tq), BF16), pltpu.VMEM((ncomp, tk, tq), BF16),
                        pltpu.VMEM((ncomp, 1, tq), F32), pltpu.VMEM((ncomp, 1, tq), F32)],
        compiler_params=_cparams(("parallel", "parallel", "parallel")),
        name="flash_diff" if diff is not None else "flash_mla",
    )(*args)


def _ret_kernel(dec_ref, qf, kf, vf, qb, kb, vb, of_ref, ob_ref, s_sc, *, chunk):
    @pl.when(pl.program_id(1) == 0)
    def _():
        s_sc[...] = jnp.zeros(s_sc.shape, F32)

    ii = lax.broadcasted_iota(jnp.int32, (chunk, chunk), 0)
    jj = lax.broadcasted_iota(jnp.int32, (chunk, chunk), 1)
    r = lax.broadcasted_iota(jnp.int32, (chunk, 1), 0).astype(F32)
    for d, (q_ref, k_ref, v_ref, o_ref) in enumerate(((qf, kf, vf, of_ref), (qb, kb, vb, ob_ref))):
        for h in range(HEADS):
            la = -jnp.exp(jnp.full((1, 1), dec_ref[d, h], F32))
            if d == 0:
                mask, dist = ii >= jj, (ii - jj).astype(F32)
                qdec, kdec = jnp.exp(la * (r + 1.0)), jnp.exp(la * (chunk - 1.0 - r))
            else:
                mask, dist = jj > ii, (jj - ii).astype(F32)
                qdec, kdec = jnp.exp(la * (chunk - r)), jnp.exp(la * r)
            decay = jnp.where(mask, jnp.exp(jnp.where(mask, dist * la, 0.0)), 0.0)
            state = s_sc[d, h]
            per = q_ref.shape[1] // chunk
            for step in range(per):
                rows = pl.ds((step if d == 0 else per - 1 - step) * chunk, chunk)
                q, k, v = q_ref[h, rows, :], k_ref[h, rows, :], v_ref[h, rows, :]
                s = lax.dot_general(q, k, (((1,), (1,)), ((), ())), preferred_element_type=F32)
                o = jnp.dot((s * decay).astype(BF16), v, preferred_element_type=F32)
                o = o + qdec * jnp.dot(q, state.astype(BF16), preferred_element_type=F32)
                kd = (k.astype(F32) * kdec).astype(BF16)
                state = jnp.exp(la * float(chunk)) * state + lax.dot_general(
                    kd, v, (((0,), (0,)), ((), ())), preferred_element_type=F32)
                o_ref[rows, h * LANE:(h + 1) * LANE] = o.astype(o_ref.dtype)
            s_sc[d, h] = state


RET_CHUNKS_PER_STEP = 4


def _retention(q, k, v, decays, *, batch, seq, chunk=256):
    chunk = min(chunk, seq // RET_CHUNKS_PER_STEP)
    blk = chunk * RET_CHUNKS_PER_STEP
    n = seq // blk
    fwd = pl.BlockSpec((None, HEADS, blk, LANE), lambda b, c: (b, 0, c, 0))
    bwd = pl.BlockSpec((None, HEADS, blk, LANE), lambda b, c: (b, 0, n - 1 - c, 0))
    w = HEADS * LANE
    out = jax.ShapeDtypeStruct((batch * seq, w), BF16)
    return pl.pallas_call(
        functools.partial(_ret_kernel, chunk=chunk),
        grid=(batch, n),
        in_specs=[pl.BlockSpec(memory_space=pltpu.SMEM), fwd, fwd, fwd, bwd, bwd, bwd],
        out_specs=[pl.BlockSpec((blk, w), lambda b, c: (b * n + c, 0)),
                   pl.BlockSpec((blk, w), lambda b, c: (b * n + n - 1 - c, 0))],
        out_shape=[out, out],
        scratch_shapes=[pltpu.VMEM((2, HEADS, LANE, LANE), F32)],
        compiler_params=_cparams(("parallel", "arbitrary")),
        name="retention",
    )(decays, q, k, v, q, k, v)


GLA_SUB = 8


def _split3(x):
    x1 = x.astype(BF16)
    r1 = x - x1.astype(F32)
    x2 = r1.astype(BF16)
    x3 = (r1 - x2.astype(F32)).astype(BF16)
    return x1, x2, x3


def _gla_direction(q, k, v, lr, w2, bias, st, reverse):
    C, wk = q.shape
    wv = v.shape[1]
    dk, dv = wk // HEADS, wv // HEADS
    z = jnp.dot(lr.astype(BF16), w2, preferred_element_type=F32) + bias
    g = (jnp.minimum(z, 0.0) - jnp.log(1.0 + jnp.exp(-jnp.abs(z)))) * (LOG2E / GLA_TAU)
    ii = lax.broadcasted_iota(jnp.int32, (C, C), 0)
    jj = lax.broadcasted_iota(jnp.int32, (C, C), 1)
    tri = jnp.where(ii >= jj, 1.0, 0.0).astype(BF16)
    b = sum(jnp.dot(tri, part, preferred_element_type=F32) for part in _split3(g))
    tot = b[C - 1:C, :]
    c = (tot - b + g) if reverse else b

    qe = (q * jnp.exp2(jnp.minimum(c, 0.0))).astype(BF16)
    o = lax.dot_general(qe, st.astype(BF16), (((1,), (1,)), ((), ())), preferred_element_type=F32)
    ke = (k * jnp.exp2(jnp.minimum(tot - c, 0.0))).astype(BF16)
    upd = lax.dot_general(v.astype(BF16), ke, (((0,), (0,)), ((), ())), preferred_element_type=F32)
    rr = _div_pow2(lax.broadcasted_iota(jnp.int32, (wv, wk), 0), dv)
    cc = _div_pow2(lax.broadcasted_iota(jnp.int32, (wv, wk), 1), dk)
    new_st = jnp.where(rr == cc, st * jnp.exp2(tot) + upd, 0.0)

    lane_head = _div_pow2(lax.broadcasted_iota(jnp.int32, (C, wk), 1), dk)
    scores = [jnp.zeros((C, C), F32) for _ in range(HEADS)]
    hsz = C // 2
    while hsz >= GLA_SUB:
        blk = 2 * hsz
        rows = []
        for m in range(C // blk):
            rrow = m * blk + (hsz if reverse else hsz - 1)
            rows.append(jnp.broadcast_to(c[rrow:rrow + 1, :], (blk, wk)))
        ref = jnp.concatenate(rows, axis=0) if len(rows) > 1 else rows[0]
        qt = q * jnp.exp2(jnp.minimum(c - ref, 0.0))
        kt = (k * jnp.exp2(jnp.minimum(ref - c, 0.0))).astype(BF16)
        same = _div_pow2(ii, blk) == _div_pow2(jj, blk)
        if reverse:
            lvl = same & (_mod_pow2(ii, blk) < hsz) & (_mod_pow2(jj, blk) >= hsz)
        else:
            lvl = same & (_mod_pow2(ii, blk) >= hsz) & (_mod_pow2(jj, blk) < hsz)
        for h in range(HEADS):
            qh = jnp.where(lane_head == h, qt, 0.0).astype(BF16)
            s = lax.dot_general(qh, kt, (((1,), (1,)), ((), ())), preferred_element_type=F32)
            scores[h] = scores[h] + jnp.where(lvl, s, 0.0)
        hsz //= 2

    assert dv == C
    er = _div_pow2(lax.broadcasted_iota(jnp.int32, (wk, wv), 0), dk)
    ec = _div_pow2(lax.broadcasted_iota(jnp.int32, (wk, wv), 1), dv)
    expand = jnp.where(er == ec, 1.0, 0.0).astype(BF16)
    dist = (jj - ii) if reverse else (ii - jj)
    same_sub = _div_pow2(ii, GLA_SUB) == _div_pow2(jj, GLA_SUB)
    for lag in range(1 if reverse else 0, GLA_SUB):
        if lag == 0:
            t = q * k
        else:
            shift = (GLA_SUB - lag) if reverse else lag
            ks = pltpu.roll(k.reshape(C // GLA_SUB, GLA_SUB, wk), shift, axis=1).reshape(C, wk)
            cs = pltpu.roll(c.reshape(C // GLA_SUB, GLA_SUB, wk), shift, axis=1).reshape(C, wk)
            t = q * ks * jnp.exp2(jnp.minimum(c - cs, 0.0))
        red = jnp.dot(t.astype(BF16), expand, preferred_element_type=F32)
        on_diag = same_sub & (dist == lag)
        for h in range(HEADS):
            scores[h] = scores[h] + jnp.where(on_diag, red[:, h * dv:(h + 1) * dv], 0.0)

    vb = v.astype(BF16)
    o = o + jnp.concatenate(
        [jnp.dot(scores[h].astype(BF16), vb[:, h * dv:(h + 1) * dv], preferred_element_type=F32) for h in range(HEADS)],
        axis=1)
    return o, new_st


def _gla_kernel(qf, kf, vf, lf, qb, kb, vb, lb, w2f, bf, w2b, bb, of_ref, ob_ref, s_sc, *, qscale):
    @pl.when(pl.program_id(1) == 0)
    def _():
        s_sc[...] = jnp.zeros(s_sc.shape, F32)

    per = qf.shape[0] // GLA_CHUNK
    st_f, st_b = s_sc[0], s_sc[1]
    for s in range(per):
        rf = pl.ds(s * GLA_CHUNK, GLA_CHUNK)
        o, st_f = _gla_direction(qf[rf, :] * qscale, kf[rf, :], vf[rf, :], lf[rf, :], w2f[...], bf[...], st_f, False)
        of_ref[rf, :] = o.astype(of_ref.dtype)
        rb = pl.ds((per - 1 - s) * GLA_CHUNK, GLA_CHUNK)
        o, st_b = _gla_direction(qb[rb, :] * qscale, kb[rb, :], vb[rb, :], lb[rb, :], w2b[...], bb[...], st_b, True)
        ob_ref[rb, :] = o.astype(ob_ref.dtype)
    s_sc[0] = st_f
    s_sc[1] = st_b


GLA_CHUNK = 128
GLA_CHUNKS_PER_STEP = 4


def _gla(z, *, qcol, kcol, vcol, lcol, w2f, bf, w2b, bb, batch, seq):
    chunk = GLA_CHUNK * GLA_CHUNKS_PER_STEP
    assert seq % chunk == 0
    n = seq // chunk
    wk, wv = HEADS * GLA_K_DIM, HEADS * LANE

    def specs(cmap):
        return [pl.BlockSpec((chunk, wk), lambda b, c: (cmap(b, c), qcol)),
                pl.BlockSpec((chunk, wk), lambda b, c: (cmap(b, c), kcol)),
                pl.BlockSpec((chunk, wv), lambda b, c: (cmap(b, c), vcol)),
                pl.BlockSpec((chunk, LANE), lambda b, c: (cmap(b, c), lcol))]

    fmap = lambda b, c: b * n + c
    bmap = lambda b, c: b * n + n - 1 - c
    wspec = [pl.BlockSpec((LANE, wk), lambda b, c: (0, 0)), pl.BlockSpec((1, wk), lambda b, c: (0, 0))]
    out = jax.ShapeDtypeStruct((batch * seq, wv), BF16)
    return pl.pallas_call(
        functools.partial(_gla_kernel, qscale=GLA_K_DIM ** -0.5),
        grid=(batch, n),
        in_specs=specs(fmap) + specs(bmap) + wspec + wspec,
        out_specs=[pl.BlockSpec((chunk, wv), lambda b, c: (fmap(b, c), 0)),
                   pl.BlockSpec((chunk, wv), lambda b, c: (bmap(b, c), 0))],
        out_shape=[out, out],
        scratch_shapes=[pltpu.VMEM((2, wv, wk), F32)],
        compiler_params=_cparams(("parallel", "arbitrary")),
        name="gla",
    )(z, z, z, z, z, z, z, z, w2f, bf, w2b, bb)


def _layer_norm_rows(r, g, b):
    mu = jnp.mean(r, axis=-1, keepdims=True)
    d = r - mu
    var = jnp.mean(d * d, axis=-1, keepdims=True)
    return d * lax.rsqrt(var + LN_EPS) * g + b


def _outproj_kernel(x_ref, fin_ref, of_ref, ob_ref, gate_ref, ng_ref, wa_ref, wb_ref, lg_ref, lb_ref, rw_ref, rb_ref,
                    o_ref, ids_ref, gw_ref, cnt_ref, run_sc, *, group_norm):
    lin = of_ref[...].astype(F32) + ob_ref[...].astype(F32)
    parts = []
    for h in range(HEADS):
        zh = lin[:, h * LANE:(h + 1) * LANE]
        if group_norm:
            mu = jnp.mean(zh, axis=-1, keepdims=True)
            dz = zh - mu
            parts.append(dz * lax.rsqrt(jnp.mean(dz * dz, axis=-1, keepdims=True) + LN_EPS))
        else:
            parts.append(zh * lax.rsqrt(jnp.mean(zh * zh, axis=-1, keepdims=True) + RMS_EPS) * ng_ref[...])
    gate = gate_ref[...]
    lin = jnp.concatenate(parts, axis=1) * (gate * jax.nn.sigmoid(gate))
    y = (jnp.dot(fin_ref[...].astype(BF16), wa_ref[...], preferred_element_type=F32)
         + jnp.dot(lin.astype(BF16), wb_ref[...], preferred_element_type=F32))
    x1 = _layer_norm_rows(ALPHA * x_ref[...] + y, lg_ref[...], lb_ref[...])
    o_ref[...] = x1
    _route_rows(x1, rw_ref, rb_ref, ids_ref, gw_ref, cnt_ref, run_sc)


def _outproj(x, fin, of, ob, gate_src, gate_col, norm_gain, wa, wb, ln_g, ln_b, route_w, route_b, *, group_norm, tm=256):
    n, d = x.shape
    w = HEADS * LANE
    tm = min(tm, n)
    row = lambda i: (i, 0)
    const = lambda i: (0, 0)
    rw_hi = route_w.astype(BF16)
    rw = jnp.stack([rw_hi, (route_w - rw_hi.astype(F32)).astype(BF16)])
    return pl.pallas_call(
        functools.partial(_outproj_kernel, group_norm=group_norm),
        grid=(n // tm,),
        in_specs=[pl.BlockSpec((tm, d), row), pl.BlockSpec((tm, w), row), pl.BlockSpec((tm, w), row),
                  pl.BlockSpec((tm, w), row), pl.BlockSpec((tm, w), lambda i: (i, gate_col)),
                  pl.BlockSpec((1, LANE), const), pl.BlockSpec((w, d), const), pl.BlockSpec((w, d), const),
                  pl.BlockSpec((1, d), const), pl.BlockSpec((1, d), const),
                  pl.BlockSpec((2, d, LANE), lambda i: (0, 0, 0)), pl.BlockSpec((1, LANE), const)],
        out_specs=[pl.BlockSpec((tm, d), row), pl.BlockSpec((tm, LANE), row), pl.BlockSpec((tm, LANE), row),
                   pl.BlockSpec((1, LANE), const)],
        out_shape=[jax.ShapeDtypeStruct((n, d), F32), jax.ShapeDtypeStruct((n, LANE), jnp.int32),
                   jax.ShapeDtypeStruct((n, LANE), F32), jax.ShapeDtypeStruct((1, LANE), jnp.int32)],
        scratch_shapes=[pltpu.VMEM((1, LANE), F32)],
        compiler_params=_cparams(("arbitrary",)),
        name="outproj",
    )(x, fin, of, ob, gate_src, norm_gain.reshape(1, LANE), wa, wb, ln_g.reshape(1, d), ln_b.reshape(1, d), rw, route_b)


def _route_rows(x, w_ref, b_ref, ids_ref, gw_ref, cnt_ref, run_sc):
    @pl.when(pl.program_id(0) == 0)
    def _():
        run_sc[...] = jnp.zeros(run_sc.shape, F32)

    tm = x.shape[0]
    xh = x.astype(BF16)
    xl = (x - xh.astype(F32)).astype(BF16)
    wh, wl = w_ref[0], w_ref[1]
    logits = (jnp.dot(xh, wh, preferred_element_type=F32) + jnp.dot(xh, wl, preferred_element_type=F32)
              + jnp.dot(xl, wh, preferred_element_type=F32)) + b_ref[...]
    lane = lax.broadcasted_iota(jnp.int32, logits.shape, 1)
    neg = -jnp.inf
    gmask = (lane >= N_EXPERTS) & (lane < N_EXPERTS + N_GROUPS)
    gl = jnp.where(gmask, logits, neg)
    gmax = jnp.max(gl, axis=1, keepdims=True)
    lane_f = lane.astype(F32)
    first = lambda hit: jnp.min(jnp.where(hit, lane_f, float(LANE)), axis=1, keepdims=True).astype(jnp.int32)
    gidx = first(gl == gmax) - N_EXPERTS
    p_grp = 1.0 / jnp.sum(jnp.where(gmask, jnp.exp(gl - gmax), 0.0), axis=1, keepdims=True)
    el = jnp.where(_div_pow2(lane, EXPERTS_PER_GROUP) == gidx, logits, neg)
    l1 = jnp.max(el, axis=1, keepdims=True)
    e1 = first(el == l1)
    el2 = jnp.where(lane == e1, neg, el)
    l2 = jnp.max(el2, axis=1, keepdims=True)
    e2 = first(el2 == l2)
    t = jnp.exp(l2 - l1)
    w1 = p_grp / (1.0 + t)
    w2 = p_grp * t / (1.0 + t)

    onehot = jnp.where(lane == e1, 1.0, jnp.where(lane == e2, 1.0, 0.0))
    ri = lax.broadcasted_iota(jnp.int32, (tm, tm), 0)
    ci = lax.broadcasted_iota(jnp.int32, (tm, tm), 1)
    before = jnp.dot(jnp.where(ri > ci, 1.0, 0.0).astype(BF16), onehot.astype(BF16), preferred_element_type=F32)
    before = before + run_sc[...]
    r1 = jnp.sum(jnp.where(lane == e1, before, 0.0), axis=1, keepdims=True).astype(jnp.int32)
    r2 = jnp.sum(jnp.where(lane == e2, before, 0.0), axis=1, keepdims=True).astype(jnp.int32)
    run_sc[...] = run_sc[...] + jnp.sum(onehot, axis=0, keepdims=True)
    cnt_ref[...] = run_sc[...].astype(jnp.int32)
    ids_ref[...] = jnp.where(lane == 0, e1, jnp.where(lane == 1, e2, jnp.where(lane == 2, r1, jnp.where(lane == 3, r2, 0))))
    gw_ref[...] = jnp.where(lane == 0, w1, jnp.where(lane == 1, w2, 0.0))


MOE_BM = 256
SUBL = 8


def _rows_from_linear(ref, rows):
    return jnp.concatenate([ref[pl.ds(s, rows, stride=SUBL), :] for s in range(SUBL)], axis=1)


def _rows_to_linear(ref, val):
    for s in range(SUBL):
        ref[pl.ds(s, val.shape[0], stride=SUBL), :] = val[:, s * LANE:(s + 1) * LANE]


def _dispatch_kernel(pend_ref, padded_ref, dest_ref, x_ref, xs_hbm, idx_smem, zbuf, lin, sem_i, sem_z, sem):
    i = pl.program_id(0)
    tm = x_ref.shape[0]
    bm = zbuf.shape[0] // SUBL

    @pl.when(i == 0)
    def _():
        zbuf[...] = jnp.zeros(zbuf.shape, F32)

        def tail(e):
            start_row = pl.multiple_of((pend_ref[e] - bm) * SUBL, bm * SUBL)
            return pltpu.make_async_copy(zbuf, xs_hbm.at[pl.ds(start_row, bm * SUBL), :], sem_z)

        def start(e, carry):
            @pl.when(padded_ref[e] > 0)
            def _():
                tail(e).start()
            return carry

        def wait(e, carry):
            @pl.when(padded_ref[e] > 0)
            def _():
                tail(e).wait()
            return carry

        lax.fori_loop(0, N_EXPERTS, start, 0)
        lax.fori_loop(0, N_EXPERTS, wait, 0)

        def unused(b):
            start_row = pl.multiple_of(b * bm * SUBL, bm * SUBL)
            return pltpu.make_async_copy(zbuf, xs_hbm.at[pl.ds(start_row, bm * SUBL), :], sem_z)

        first_unused = pend_ref[N_EXPERTS - 1] // bm
        n_blocks = xs_hbm.shape[0] // (bm * SUBL)
        lax.fori_loop(first_unused, n_blocks, lambda b, c: (unused(b).start(), c)[1], 0)
        lax.fori_loop(first_unused, n_blocks, lambda b, c: (unused(b).wait(), c)[1], 0)

    cp = pltpu.make_async_copy(dest_ref.at[i], idx_smem, sem_i)
    cp.start()
    _rows_to_linear(lin, x_ref[...])
    cp.wait()

    def scatter(r, carry):
        src = lin.at[pl.ds(pl.multiple_of(r * SUBL, SUBL), SUBL), :]
        for k in range(2):
            dst = pl.multiple_of(idx_smem[2 * r + k] * SUBL, SUBL)
            pltpu.make_async_copy(src, xs_hbm.at[pl.ds(dst, SUBL), :], sem).start(priority=k)
        return carry

    lax.fori_loop(0, tm, scatter, 0, unroll=8)
    for k in range(2):
        pltpu.make_async_copy(lin, xs_hbm.at[pl.ds(0, tm * SUBL), :], sem).wait()


def _dispatch(x, dest, pend, padded, cap, *, tm=512):
    n, d = x.shape
    assert d == SUBL * LANE
    tm = min(tm, n)
    nt = n // tm
    grid_spec = pltpu.PrefetchScalarGridSpec(
        num_scalar_prefetch=2,
        grid=(nt,),
        in_specs=[pl.BlockSpec((nt, 2 * tm), lambda i, pe, pa: (0, 0)),
                  pl.BlockSpec((tm, d), lambda i, pe, pa: (i, 0))],
        out_specs=pl.BlockSpec(memory_space=pl.ANY),
        scratch_shapes=[pltpu.SMEM((2 * tm,), jnp.int32), pltpu.VMEM((MOE_BM * SUBL, LANE), F32),
                        pltpu.VMEM((tm * SUBL, LANE), F32),
                        pltpu.SemaphoreType.DMA(()), pltpu.SemaphoreType.DMA(()), pltpu.SemaphoreType.DMA(())],
    )
    return pl.pallas_call(
        _dispatch_kernel,
        grid_spec=grid_spec,
        out_shape=jax.ShapeDtypeStruct((cap * SUBL, LANE), F32),
        compiler_params=_cparams(("arbitrary",)),
        name="dispatch",
    )(pend, padded, dest.reshape(nt, 2 * tm), x)


def _experts_kernel(blk_e_ref, nused_ref, xs_ref, wg_ref, wu_ref, wd_ref, ys_ref, wgb, wub, wdb, hbuf):
    i = pl.program_id(0)
    nb = pl.num_programs(0) - 1
    cur = jnp.minimum(i, nb - 1)
    prv = jnp.maximum(i - 1, 0)

    @pl.when(i == 0)
    def _():
        hbuf[...] = jnp.zeros(hbuf.shape, BF16)

    @pl.when(jnp.logical_or(i == 0, blk_e_ref[cur] != blk_e_ref[jnp.minimum(prv, nb - 1)]))
    def _():
        wgb[...] = wg_ref[...].astype(BF16)
        wub[...] = wu_ref[...].astype(BF16)

    @pl.when(jnp.logical_or(i == 0, blk_e_ref[prv] != blk_e_ref[jnp.maximum(i - 2, 0)]))
    def _():
        wdb[...] = wd_ref[...].astype(BF16)

    slot = lax.rem(i, 2)
    y = jnp.dot(hbuf[1 - slot], wdb[...], preferred_element_type=F32)
    live = jnp.logical_and(i >= 1, i - 1 < nused_ref[0])
    _rows_to_linear(ys_ref, jnp.where(live, y, 0.0))
    xb = _rows_from_linear(xs_ref, xs_ref.shape[0] // SUBL).astype(BF16)
    hg = jnp.dot(xb, wgb[...], preferred_element_type=F32)
    hu = jnp.dot(xb, wub[...], preferred_element_type=F32)
    hbuf[slot] = (hg * jax.nn.sigmoid(hg) * hu).astype(BF16)


def _experts(xs, blk_e, n_used, w_gate, w_up, w_down, layer):
    cap = xs.shape[0] // SUBL
    bm = MOE_BM
    nb = cap // bm
    d, de = w_gate.shape[2], w_gate.shape[3]
    row_in = lambda i, be, nu: (jnp.minimum(i, nu[0] - 1), 0)
    row = lambda i, be, nu: (jnp.maximum(i - 1, 0), 0)
    w_cur = lambda i, be, nu: (layer, be[jnp.minimum(i, nb - 1)], 0, 0)
    w_prv = lambda i, be, nu: (layer, be[jnp.maximum(i - 1, 0)], 0, 0)
    grid_spec = pltpu.PrefetchScalarGridSpec(
        num_scalar_prefetch=2,
        grid=(nb + 1,),
        in_specs=[pl.BlockSpec((bm * SUBL, LANE), row_in),
                  pl.BlockSpec((None, None, d, de), w_cur),
                  pl.BlockSpec((None, None, d, de), w_cur),
                  pl.BlockSpec((None, None, de, d), w_prv)],
        out_specs=pl.BlockSpec((bm * SUBL, LANE), row),
        scratch_shapes=[pltpu.VMEM((d, de), BF16), pltpu.VMEM((d, de), BF16), pltpu.VMEM((de, d), BF16),
                        pltpu.VMEM((2, bm, de), BF16)],
    )
    return pl.pallas_call(
        _experts_kernel,
        grid_spec=grid_spec,
        out_shape=jax.ShapeDtypeStruct((cap * SUBL, LANE), F32),
        compiler_params=_cparams(("arbitrary",)),
        name="experts",
    )(blk_e, n_used, xs, w_gate, w_up, w_down)


def _combine_kernel(dest_ref, x_ref, gw_ref, g_ref, b_ref, ys_hbm, o_ref, idx_smem, ybuf, sem_i, sem):
    i = pl.program_id(0)
    tm = x_ref.shape[0]

    def issue(tile, slot):
        cp = pltpu.make_async_copy(dest_ref.at[tile], idx_smem, sem_i)
        cp.start()
        cp.wait()

        def gather(r, carry):
            row = pl.multiple_of(r * SUBL, SUBL)
            for k in range(2):
                src = pl.multiple_of(idx_smem[2 * r + k] * SUBL, SUBL)
                pltpu.make_async_copy(ys_hbm.at[pl.ds(src, SUBL), :], ybuf.at[slot, k, pl.ds(row, SUBL), :],
                                      sem.at[slot]).start(priority=k)
            return carry

        lax.fori_loop(0, tm, gather, 0, unroll=8)

    slot = lax.rem(i, 2)

    @pl.when(i == 0)
    def _():
        issue(0, 0)

    @pl.when(i + 1 < pl.num_programs(0))
    def _():
        issue(i + 1, 1 - slot)

    for k in range(2):
        pltpu.make_async_copy(ys_hbm.at[pl.ds(0, tm * SUBL), :], ybuf.at[slot, k], sem.at[slot]).wait()
    gw = gw_ref[...]
    ffn = (_rows_from_linear(ybuf.at[slot, 0], tm) * gw[:, 0:1] + _rows_from_linear(ybuf.at[slot, 1], tm) * gw[:, 1:2])
    o_ref[...] = _layer_norm_rows(ALPHA * x_ref[...] + ffn, g_ref[...], b_ref[...])


def _combine(x, ys, dest, gw, ln_g, ln_b, *, tm=512):
    n, d = x.shape
    tm = min(tm, n)
    nt = n // tm
    return pl.pallas_call(
        _combine_kernel,
        grid=(nt,),
        in_specs=[pl.BlockSpec((nt, 2 * tm), lambda i: (0, 0)),
                  pl.BlockSpec((tm, d), lambda i: (i, 0)), pl.BlockSpec((tm, LANE), lambda i: (i, 0)),
                  pl.BlockSpec((1, d), lambda i: (0, 0)), pl.BlockSpec((1, d), lambda i: (0, 0)),
                  pl.BlockSpec(memory_space=pl.ANY)],
        out_specs=pl.BlockSpec((tm, d), lambda i: (i, 0)),
        out_shape=jax.ShapeDtypeStruct((n, d), F32),
        scratch_shapes=[pltpu.SMEM((2 * tm,), jnp.int32), pltpu.VMEM((2, 2, tm * SUBL, LANE), F32),
                        pltpu.SemaphoreType.DMA(()), pltpu.SemaphoreType.DMA((2,))],
        compiler_params=_cparams(("arbitrary",)),
        name="combine",
    )(dest.reshape(nt, 2 * tm), x, gw, ln_g.reshape(1, d), ln_b.reshape(1, d), ys)


def _router_params(w_grp, b_grp, w_exp, b_exp):
    d = w_exp.shape[0]
    wr = jnp.zeros((d, LANE), F32).at[:, :N_EXPERTS].set(w_exp).at[:, N_EXPERTS:N_EXPERTS + N_GROUPS].set(w_grp)
    br = jnp.zeros((1, LANE), F32).at[0, :N_EXPERTS].set(b_exp).at[0, N_EXPERTS:N_EXPERTS + N_GROUPS].set(b_grp)
    return wr, br


def _moe(x, routing, w_gate, w_up, w_down, layer, ln_g, ln_b):
    n, d = x.shape
    ids, gw, cnt = routing
    bm = MOE_BM
    counts = cnt[0, :N_EXPERTS]
    padded = (counts + bm - 1) // bm * bm
    pend = jnp.cumsum(padded)
    pstart = pend - padded
    e, r = ids[:, 0:2], ids[:, 2:4]
    onehot = e[:, :, None] == jnp.arange(N_EXPERTS, dtype=jnp.int32)[None, None, :]
    dest = jnp.sum(jnp.where(onehot, pstart[None, None, :], 0), axis=-1) + r
    cap = 2 * n + N_EXPERTS * bm
    nb = cap // bm
    blk_start = jnp.arange(nb, dtype=jnp.int32) * bm
    blk_e = jnp.minimum(jnp.sum((pend[None, :] <= blk_start[:, None]).astype(jnp.int32), axis=1), N_EXPERTS - 1)
    n_used = (pend[-1:] // bm).astype(jnp.int32)
    xs = _dispatch(x, dest, pend.astype(jnp.int32), padded.astype(jnp.int32), cap)
    ys = _experts(xs, blk_e, n_used, w_gate, w_up, w_down, layer)
    return _combine(x, ys, dest, gw, ln_g, ln_b)


def _even_layer(x, batch, seq, layer_idx, w_in, dec_f, dec_b, lq1, lk1, lq2, lk2, subln, w_out, ln_g, ln_b, route):
    d = x.shape[1]
    w = HEADS * LANE
    kw = dict(batch=batch, seq=seq)
    diff_seg = [(0, DIFF_ROT_DIM), (DIFF_HEAD_DIM, DIFF_ROT_DIM)]
    q, k, v, gate, dq, dk, dvt = _even_in(
        x, w_in.astype(BF16), _rope_tables(seq, [(0, LANE)], RET_THETA),
        _rope_tables(seq, diff_seg, ROPE_THETA), **kw)
    decays = jnp.stack([dec_f, dec_b]).astype(F32)
    of, ob = _retention(q, k, v, decays, **kw)
    lam_init = 0.8 - 0.6 * math.exp(-0.3 * layer_idx)
    diff = _flash(dq, dk, dvt, diff=(lq1, lk1, lq2, lk2, subln), lam_init=lam_init, tq=2048, **kw)
    wo = w_out.astype(BF16)
    return _outproj(x, diff, of, ob, gate, 0, jnp.ones((LANE,), F32), wo[w:], wo[:w], ln_g, ln_b, *route,
                    group_norm=True)


def _odd_layer(x, batch, seq, w_in, q_norm, w_uq, kv_norm, w_ukv, w2_f, b_f, w2_b, b_b, gla_norm, w_out, ln_g, ln_b,
               route):
    d = x.shape[1]
    w = HEADS * LANE
    o = np.cumsum([0, MLA_Q_RANK, MLA_KV_RANK, MLA_ROPE, HEADS * GLA_K_DIM, HEADS * GLA_K_DIM, w, w,
                   GLA_GATE_RANK, GLA_GATE_RANK]).tolist()
    zeros = lambda c: jnp.zeros((d, c), F32)
    w_in2 = jnp.concatenate([
        w_in[:, o[0]:o[2]], zeros(MLA_NOPE), w_in[:, o[2]:o[3]], zeros(LANE - MLA_NOPE - MLA_ROPE),
        w_in[:, o[3]:o[7]], w_in[:, o[7]:o[9]], zeros(LANE - 2 * GLA_GATE_RANK)], axis=1).astype(BF16)
    kw = dict(batch=batch, seq=seq)
    qd = MLA_NOPE + MLA_ROPE
    w_uq2 = jnp.pad(w_uq.reshape(MLA_Q_RANK, HEADS, qd), ((0, 0), (0, 0), (0, LANE - qd))).reshape(MLA_Q_RANK, w)
    ukv = w_ukv.reshape(MLA_KV_RANK, HEADS, MLA_NOPE + MLA_V)
    w_uk2 = jnp.pad(ukv[:, :, :MLA_NOPE], ((0, 0), (0, 0), (0, LANE - MLA_NOPE))).reshape(MLA_KV_RANK, w)
    w_uv2 = ukv[:, :, MLA_NOPE:].reshape(MLA_KV_RANK, w)
    q, k, vt, zg = _odd_in(x, w_in2, q_norm, w_uq2.astype(BF16), kv_norm,
                           jnp.concatenate([w_uk2, w_uv2], axis=1).astype(BF16),
                           _rope_tables(seq, [(MLA_NOPE, MLA_ROPE)], ROPE_THETA), **kw)
    mla = _flash(q, k, vt, passes=4, **kw)
    wk = HEADS * GLA_K_DIM
    pad_rows = lambda m, r0: jnp.zeros((LANE, wk), F32).at[r0:r0 + GLA_GATE_RANK].set(m).astype(BF16)
    of, ob = _gla(zg, qcol=0, kcol=1, vcol=1, lcol=12,
                  w2f=pad_rows(w2_f, 0), bf=b_f.reshape(1, wk), w2b=pad_rows(w2_b, GLA_GATE_RANK), bb=b_b.reshape(1, wk), **kw)
    wo = w_out.astype(BF16)
    return _outproj(x, mla, of, ob, zg, 2, gla_norm, wo[:w], wo[w:], ln_g, ln_b, *route, group_norm=False)


def kernel(x, ev_w_in, ev_ret_decay_f, ev_ret_decay_b, ev_lq1, ev_lk1, ev_lq2, ev_lk2, ev_subln, ev_w_out, od_w_in, od_q_norm, od_w_uq, od_kv_norm, od_w_ukv, od_gla_w2_f, od_gla_b_f, od_gla_w2_b, od_gla_b_b, od_gla_norm, od_w_out, ln1_g, ln1_b, ln2_g, ln2_b, moe_w_grp, moe_b_grp, moe_w_exp, moe_b_exp, moe_w_gate, moe_w_up, moe_w_down):
    batch, seq, d = x.shape
    h = x.reshape(batch * seq, d)
    for i in range(DEPTH):
        j = i // 2
        route = _router_params(moe_w_grp[i], moe_b_grp[i], moe_w_exp[i], moe_b_exp[i])
        if i % 2 == 0:
            h, *routing = _even_layer(h, batch, seq, i, ev_w_in[j], ev_ret_decay_f[j], ev_ret_decay_b[j], ev_lq1[j],
                                      ev_lk1[j], ev_lq2[j], ev_lk2[j], ev_subln[j], ev_w_out[j], ln1_g[i], ln1_b[i], route)
        else:
            h, *routing = _odd_layer(h, batch, seq, od_w_in[j], od_q_norm[j], od_w_uq[j], od_kv_norm[j], od_w_ukv[j],
                                     od_gla_w2_f[j], od_gla_b_f[j], od_gla_w2_b[j], od_gla_b_b[j], od_gla_norm[j],
                                     od_w_out[j], ln1_g[i], ln1_b[i], route)
        h = _moe(h, routing, moe_w_gate, moe_w_up, moe_w_down, i, ln2_g[i], ln2_b[i])
    return h.reshape(batch, seq, d)
```
